```python
import math
import jax, jax.numpy as jnp
from jax import lax
import numpy as np

D_MODEL = 1024
BATCH = 8
SEQ = 2048
DEPTH = 2
DEC_BATCH = 128
DEC_SEQ = 4
PAST_LEN = 16384
PAGE_SIZE = 128

H_A = 4
DH_A = 128
D_A = H_A * DH_A
D_B = 512
CONV_B = 3
H_C = 4
DH_C = 128
D_C = H_C * DH_C
CONV_C = 4
CHUNK = 64
N_GROUPS = 4
EXPERTS_PER_GROUP = 8
N_EXPERTS = N_GROUPS * EXPERTS_PER_GROUP
TOP_K = 2
D_EXPERT = 256
MOE_BLOCK = 128
DN_ALPHA = (2 * DEPTH) ** 0.25
DN_BETA = (8 * DEPTH) ** -0.25
LN_EPS = 1e-5
NORM_EPS = 1e-6

SPLIT_SIZES = (D_A, D_A, D_A, D_A, H_A, H_A,
               D_B, D_B, D_B,
               3 * D_C, D_C, H_C, H_C,
               3 * D_MODEL)
N_IN = sum(SPLIT_SIZES)
SPLIT_POINTS = tuple(int(s) for s in np.cumsum(SPLIT_SIZES)[:-1])
OFF_F = 4 * D_A + H_A
STATE_KEYS = ('C', 'n', 'm', 'conv', 'S', 'gconv')

kernel_name = 'hybrid_mlstm_conv_gdn_hiermoe_step'


def _layer_norm(x, g, b):
    xf = x.astype(jnp.float32)
    mu = xf.mean(-1, keepdims=True)
    var = jnp.square(xf - mu).mean(-1, keepdims=True)
    y = (xf - mu) * lax.rsqrt(var + LN_EPS) * g.astype(jnp.float32) + b.astype(jnp.float32)
    return y.astype(x.dtype)


def _l2norm(x):
    return x * lax.rsqrt(jnp.sum(x * x, axis=-1, keepdims=True) + NORM_EPS)


def _heads(a, n_heads):
    B, L, _ = a.shape
    return a.reshape(B, L, n_heads, -1).transpose(0, 2, 1, 3)


def _merge(a):
    B, H, L, Dh = a.shape
    return a.transpose(0, 2, 1, 3).reshape(B, L, H * Dh)


def _causal_dwconv(buf, x, w):
    xp = jnp.concatenate([buf.astype(x.dtype), x], axis=1)
    y = lax.conv_general_dilated(xp, w[:, None, :].astype(x.dtype), window_strides=(1,), padding='VALID',
                                 dimension_numbers=('NWC', 'WIO', 'NWC'), feature_group_count=x.shape[-1])
    return y, xp[:, -(w.shape[0] - 1):]


def _chunk_len(L):
    return CHUNK if L % CHUNK == 0 else L


def _to_chunks(a, cl):
    B, H, L = a.shape[:3]
    return jnp.moveaxis(a.reshape(B, H, L // cl, cl, *a.shape[3:]), 2, 0)


def _from_chunks(a):
    a = jnp.moveaxis(a, 0, 2)
    return a.reshape(a.shape[0], a.shape[1], a.shape[2] * a.shape[3], *a.shape[4:])


def _mlstm(q, k, v, i_log, f_log, C0, n0, m0):
    cl = _chunk_len(q.shape[2])
    causal = jnp.tril(jnp.ones((cl, cl), dtype=bool))

    def step(carry, xs):
        C, n, m = carry
        qc, kc, vc, ic, fc = xs
        b = jnp.cumsum(fc, axis=-1)
        dlog = jnp.where(causal, b[..., :, None] - b[..., None, :] + ic[..., None, :], -jnp.inf)
        m_inter = b + m[..., None]
        m_t = jnp.maximum(m_inter, dlog.max(-1))
        s = jnp.einsum('bhtd,bhsd->bhts', qc, kc) * jnp.exp(dlog - m_t[..., None])
        inter = jnp.exp(m_inter - m_t)
        num = jnp.einsum('bhts,bhse->bhte', s, vc) + inter[..., None] * jnp.einsum('bhtd,bhde->bhte', qc, C)
        den = s.sum(-1) + inter * jnp.einsum('bhtd,bhd->bht', qc, n)
        h = num / jnp.maximum(jnp.abs(den), jnp.exp(-m_t))[..., None]
        m_new = m_t[..., -1]
        wk = jnp.exp(b[..., -1:] - b + ic - m_new[..., None])
        decay = jnp.exp(b[..., -1] + m - m_new)
        C = decay[..., None, None] * C + jnp.einsum('bhsd,bhse->bhde', kc * wk[..., None], vc)
        n = decay[..., None] * n + jnp.einsum('bhsd->bhd', kc * wk[..., None])
        return (C, n, m_new), h

    xs = tuple(_to_chunks(a, cl) for a in (q, k, v, i_log, f_log))
    (C, n, m), h = lax.scan(step, (C0, n0, m0), xs)
    return _from_chunks(h), C, n, m


def _gated_delta(q, k, v, g, beta, S0):
    cl = _chunk_len(q.shape[2])
    dv = v.shape[-1]
    incl = jnp.tril(jnp.ones((cl, cl), dtype=bool))
    strict = jnp.tril(jnp.ones((cl, cl), dtype=bool), k=-1)
    eye = jnp.eye(cl, dtype=jnp.float32)

    def step(S, xs):
        qc, kc, vc, gc, bc = xs
        gam = jnp.cumsum(gc, axis=-1)
        decay = jnp.exp(jnp.where(incl, gam[..., :, None] - gam[..., None, :], -jnp.inf))
        a = jnp.where(strict, bc[..., :, None] * jnp.einsum('bhtd,bhsd->bhts', kc, kc) * decay, 0.0)
        rhs = jnp.concatenate([bc[..., None] * vc, (bc * jnp.exp(gam))[..., None] * kc], axis=-1)
        sol = lax.linalg.triangular_solve(a + eye, rhs, left_side=True, lower=True, unit_diagonal=True)
        u, w = sol[..., :dv], sol[..., dv:]
        v_new = u - jnp.einsum('bhtd,bhde->bhte', w, S)
        qk = jnp.einsum('bhtd,bhsd->bhts', qc, kc) * decay
        o = jnp.einsum('bhtd,bhde->bhte', qc * jnp.exp(gam)[..., None], S) + jnp.einsum('bhts,bhse->bhte', qk, v_new)
        g_last = gam[..., -1]
        S = jnp.exp(g_last)[..., None, None] * S + jnp.einsum(
            'bhsd,bhse->bhde', kc * jnp.exp(g_last[..., None] - gam)[..., None], v_new)
        return S, o

    xs = tuple(_to_chunks(a, cl) for a in (q, k, v, g, beta))
    S, o = lax.scan(step, S0, xs)
    return _from_chunks(o), S


def _token_mixers(u, st, p, l):
    f32 = jnp.float32
    proj = u @ p['w_in'][l] + p['b_in'][l]
    (qa, ka, va, oa, ia, fa, bb, cb, hb, qkvc, zc, betac, ac, gts) = jnp.split(proj, SPLIT_POINTS, axis=-1)

    q = _heads(qa, H_A).astype(f32)
    k = _heads(ka, H_A).astype(f32) * (DH_A ** -0.5)
    v = _heads(va, H_A).astype(f32)
    i_log = ia.astype(f32).transpose(0, 2, 1)
    f_log = jax.nn.log_sigmoid(fa.astype(f32)).transpose(0, 2, 1)
    h, C_new, n_new, m_new = _mlstm(q, k, v, i_log, f_log, st['C'].astype(f32),
                                    st['n'].astype(f32), st['m'].astype(f32))
    mu = h.mean(-1, keepdims=True)
    hn = (h - mu) * lax.rsqrt(jnp.square(h - mu).mean(-1, keepdims=True) + LN_EPS)
    y_a = jax.nn.sigmoid(oa.astype(f32)) * _merge(hn) * p['mlstm_norm_g'][l].astype(f32)

    conv_out, conv_state = _causal_dwconv(st['conv'], cb * hb, p['conv_b_w'][l])
    y_b = bb * conv_out

    qkv, gconv_state = _causal_dwconv(st['gconv'], qkvc, p['conv_c_w'][l])
    qkv = jax.nn.silu(qkv)
    qc_, kc_, vc_ = jnp.split(qkv, 3, axis=-1)
    qh = _l2norm(_heads(qc_, H_C).astype(f32)) * (DH_C ** -0.5)
    kh = _l2norm(_heads(kc_, H_C).astype(f32))
    vh = _heads(vc_, H_C).astype(f32)
    beta = jax.nn.sigmoid(betac.astype(f32)).transpose(0, 2, 1)
    g = -jnp.exp(p['gdn_a_log'][l].astype(f32))[None, :, None] * jax.nn.softplus(
        ac.astype(f32) + p['gdn_dt_bias'][l].astype(f32)).transpose(0, 2, 1)
    o, S_new = _gated_delta(qh, kh, vh, g, beta, st['S'].astype(f32))
    o = o * lax.rsqrt(jnp.mean(o * o, axis=-1, keepdims=True) + NORM_EPS) * p['gdn_norm_g'][l].astype(f32)
    y_c = _merge(o) * jax.nn.silu(zc.astype(f32))

    ga, gb, gc = jnp.split(jax.nn.sigmoid(gts), 3, axis=-1)
    merged = (ga * (y_a.astype(u.dtype) @ p['w_br_a'][l]) + gb * (y_b @ p['w_br_b'][l])
              + gc * (y_c.astype(u.dtype) @ p['w_br_c'][l]))
    out = merged @ p['w_out'][l]
    new_st = {'C': C_new.astype(st['C'].dtype), 'n': n_new.astype(st['n'].dtype),
              'm': m_new.astype(st['m'].dtype), 'conv': conv_state.astype(st['conv'].dtype),
              'S': S_new.astype(st['S'].dtype), 'gconv': gconv_state.astype(st['gconv'].dtype)}
    return out, new_st


def _expert_dispatch(x2, eid, wts, w_gate, w_up, w_down):
    T, D = x2.shape
    TK = T * TOP_K
    n_blocks = TK // MOE_BLOCK + N_EXPERTS
    n_rows = n_blocks * MOE_BLOCK
    e_flat = eid.reshape(TK)
    t_flat = jnp.repeat(jnp.arange(T, dtype=jnp.int32), TOP_K)
    w_flat = wts.reshape(TK)
    e_s, t_s, w_s = lax.sort((e_flat, t_flat, w_flat), num_keys=1, is_stable=True)
    counts = jnp.bincount(e_flat, length=N_EXPERTS)
    starts = jnp.cumsum(counts) - counts
    padded = (counts + MOE_BLOCK - 1) // MOE_BLOCK * MOE_BLOCK
    pends = jnp.cumsum(padded)
    pstarts = pends - padded
    dest = pstarts[e_s] + jnp.arange(TK, dtype=jnp.int32) - starts[e_s]
    tok_rows = jnp.zeros((n_rows,), jnp.int32).at[dest].set(t_s)
    wt_rows = jnp.zeros((n_rows,), w_s.dtype).at[dest].set(w_s)
    block_e = jnp.minimum(jnp.searchsorted(pends, jnp.arange(n_blocks) * MOE_BLOCK, side='right'),
                          N_EXPERTS - 1).astype(jnp.int32)

    def block(args):
        tok, wt, e = args
        xb = x2[tok]
        hb = jax.nn.silu(xb @ w_gate[e]) * (xb @ w_up[e])
        return (hb @ w_down[e]) * wt[:, None]

    out = lax.map(block, (tok_rows.reshape(n_blocks, MOE_BLOCK), wt_rows.reshape(n_blocks, MOE_BLOCK), block_e))
    return jnp.zeros_like(x2).at[tok_rows].add(out.reshape(n_rows, D))


def _moe(u, p, l):
    B, L, D = u.shape
    f32 = jnp.float32
    x2 = u.reshape(B * L, D)
    lg = (x2 @ p['router_g_w'][l] + p['router_g_b'][l]).astype(f32)
    grp = jnp.argmax(lg, axis=-1).astype(jnp.int32)
    p_grp = jax.nn.softmax(lg, axis=-1).max(-1, keepdims=True)
    le = (x2 @ p['router_e_w'][l] + p['router_e_b'][l]).astype(f32).reshape(-1, N_GROUPS, EXPERTS_PER_GROUP)
    le_sel = le[jnp.arange(B * L), grp]
    top_p, top_i = lax.top_k(jax.nn.softmax(le_sel, axis=-1), TOP_K)
    wts = p_grp * top_p / top_p.sum(-1, keepdims=True)
    eid = grp[:, None] * EXPERTS_PER_GROUP + top_i.astype(jnp.int32)
    y = _expert_dispatch(x2, eid, wts.astype(u.dtype), p['exp_w_gate'][l], p['exp_w_up'][l], p['exp_w_down'][l])
    return y.reshape(B, L, D)


def _trunk(x, c, st, p):
    x = _layer_norm(x, p['ln_in_g'], p['ln_in_b'])
    new = {key: [] for key in STATE_KEYS}
    for l in range(DEPTH):
        mod = jax.nn.silu(c) @ p['w_ada'][l] + p['b_ada'][l]
        sh1, sc1, g1, sh2, sc2, g2 = jnp.split(mod[:, None, :], 6, axis=-1)
        u = x * (1 + sc1) + sh1
        mix, st_l = _token_mixers(u, {key: st[key][l] for key in STATE_KEYS}, p, l)
        x = _layer_norm(DN_ALPHA * x + (1 + g1) * mix, p['ln1_g'][l], p['ln1_b'][l])
        u = x * (1 + sc2) + sh2
        x = _layer_norm(DN_ALPHA * x + (1 + g2) * _moe(u, p, l), p['ln2_g'][l], p['ln2_b'][l])
        for key in STATE_KEYS:
            new[key].append(st_l[key])
    return x, {key: jnp.stack(new[key]) for key in STATE_KEYS}


def _zero_state(batch, dtype):
    return {'C': jnp.zeros((DEPTH, batch, H_A, DH_A, DH_A), dtype),
            'n': jnp.zeros((DEPTH, batch, H_A, DH_A), dtype),
            'm': jnp.zeros((DEPTH, batch, H_A), dtype),
            'conv': jnp.zeros((DEPTH, batch, CONV_B - 1, D_B), dtype),
            'S': jnp.zeros((DEPTH, batch, H_C, DH_C, DH_C), dtype),
            'gconv': jnp.zeros((DEPTH, batch, CONV_C - 1, 3 * D_C), dtype)}


def setup_inputs(seed: int = 0) -> dict:
    key = jax.random.key(seed)
    ks = iter(jax.random.split(key, 64))
    f32 = jnp.float32

    def nrm(shape, scale):
        return jax.random.normal(next(ks), shape, f32) * scale

    dsc = D_MODEL ** -0.5
    b_in = nrm((DEPTH, N_IN), 0.02).at[:, OFF_F:OFF_F + H_A].add(jnp.linspace(3.0, 6.0, H_A, dtype=f32))
    a_log = jnp.log(jax.random.uniform(next(ks), (DEPTH, H_C), f32, 1.0, 16.0))
    dt = jnp.exp(jax.random.uniform(next(ks), (DEPTH, H_C), f32, math.log(1e-3), math.log(1e-1)))
    dt_bias = dt + jnp.log(-jnp.expm1(-dt))
    return {
        'x_prompt': nrm((BATCH, SEQ, D_MODEL), 1.0),
        'x_sample': nrm((DEC_BATCH, DEC_SEQ, D_MODEL), 1.0),
        'state_mlstm_C': nrm((DEPTH, DEC_BATCH, H_A, DH_A, DH_A), 0.1),
        'state_mlstm_n': nrm((DEPTH, DEC_BATCH, H_A, DH_A), 0.1),
        'state_mlstm_m': nrm((DEPTH, DEC_BATCH, H_A), 1.0),
        'state_conv': nrm((DEPTH, DEC_BATCH, CONV_B - 1, D_B), 1.0),
        'state_gdn_S': nrm((DEPTH, DEC_BATCH, H_C, DH_C, DH_C), 0.1),
        'state_gdn_conv': nrm((DEPTH, DEC_BATCH, CONV_C - 1, 3 * D_C), 1.0),
        'c_prompt': nrm((BATCH, D_MODEL), 1.0),
        'c_sample': nrm((DEC_BATCH, D_MODEL), 1.0),
        'ln_in_g': 1.0 + nrm((D_MODEL,), 0.02),
        'ln_in_b': nrm((D_MODEL,), 0.02),
        'w_ada': nrm((DEPTH, D_MODEL, 6 * D_MODEL), 0.5 * dsc),
        'b_ada': nrm((DEPTH, 6 * D_MODEL), 0.02),
        'w_in': nrm((DEPTH, D_MODEL, N_IN), dsc),
        'b_in': b_in,
        'mlstm_norm_g': 1.0 + nrm((DEPTH, D_A), 0.02),
        'conv_b_w': nrm((DEPTH, CONV_B, D_B), CONV_B ** -0.5),
        'conv_c_w': nrm((DEPTH, CONV_C, 3 * D_C), CONV_C ** -0.5),
        'gdn_a_log': a_log,
        'gdn_dt_bias': dt_bias,
        'gdn_norm_g': 1.0 + nrm((DEPTH, DH_C), 0.02),
        'w_br_a': nrm((DEPTH, D_A, D_MODEL), D_A ** -0.5),
        'w_br_b': nrm((DEPTH, D_B, D_MODEL), D_B ** -0.5),
        'w_br_c': nrm((DEPTH, D_C, D_MODEL), D_C ** -0.5),
        'w_out': nrm((DEPTH, D_MODEL, D_MODEL), dsc * DN_BETA),
        'ln1_g': 1.0 + nrm((DEPTH, D_MODEL), 0.02),
        'ln1_b': nrm((DEPTH, D_MODEL), 0.02),
        'router_g_w': nrm((DEPTH, D_MODEL, N_GROUPS), dsc),
        'router_g_b': nrm((DEPTH, N_GROUPS), 0.01),
        'router_e_w': nrm((DEPTH, D_MODEL, N_EXPERTS), dsc),
        'router_e_b': nrm((DEPTH, N_EXPERTS), 0.01),
        'exp_w_gate': nrm((DEPTH, N_EXPERTS, D_MODEL, D_EXPERT), dsc),
        'exp_w_up': nrm((DEPTH, N_EXPERTS, D_MODEL, D_EXPERT), dsc),
        'exp_w_down': nrm((DEPTH, N_EXPERTS, D_EXPERT, D_MODEL), D_EXPERT ** -0.5 * DN_BETA),
        'ln2_g': 1.0 + nrm((DEPTH, D_MODEL), 0.02),
        'ln2_b': nrm((DEPTH, D_MODEL), 0.02),
    }


def reference(x_prompt, x_sample, state_mlstm_C, state_mlstm_n, state_mlstm_m, state_conv, state_gdn_S,
              state_gdn_conv, c_prompt, c_sample, ln_in_g, ln_in_b, w_ada, b_ada, w_in, b_in, mlstm_norm_g,
              conv_b_w, conv_c_w, gdn_a_log, gdn_dt_bias, gdn_norm_g, w_br_a, w_br_b, w_br_c, w_out, ln1_g, ln1_b,
              router_g_w, router_g_b, router_e_w, router_e_b, exp_w_gate, exp_w_up, exp_w_down, ln2_g, ln2_b):
    p = dict(ln_in_g=ln_in_g, ln_in_b=ln_in_b, w_ada=w_ada, b_ada=b_ada, w_in=w_in, b_in=b_in,
             mlstm_norm_g=mlstm_norm_g, conv_b_w=conv_b_w, conv_c_w=conv_c_w, gdn_a_log=gdn_a_log,
             gdn_dt_bias=gdn_dt_bias, gdn_norm_g=gdn_norm_g, w_br_a=w_br_a, w_br_b=w_br_b, w_br_c=w_br_c,
             w_out=w_out, ln1_g=ln1_g, ln1_b=ln1_b, router_g_w=router_g_w, router_g_b=router_g_b,
             router_e_w=router_e_w, router_e_b=router_e_b, exp_w_gate=exp_w_gate, exp_w_up=exp_w_up,
             exp_w_down=exp_w_down, ln2_g=ln2_g, ln2_b=ln2_b)
    y_prompt, sp = _trunk(x_prompt, c_prompt, _zero_state(x_prompt.shape[0], x_prompt.dtype), p)
    carried = {'C': state_mlstm_C, 'n': state_mlstm_n, 'm': state_mlstm_m, 'conv': state_conv,
               'S': state_gdn_S, 'gconv': state_gdn_conv}
    y_sample, ss = _trunk(x_sample, c_sample, carried, p)
    p_C, p_n, p_m, p_conv, p_S, p_gconv = sp['C'], sp['n'], sp['m'], sp['conv'], sp['S'], sp['gconv']
    s_C, s_n, s_m, s_conv, s_S, s_gconv = ss['C'], ss['n'], ss['m'], ss['conv'], ss['S'], ss['gconv']
    return (y_prompt, y_sample, p_C, p_n, p_m, p_conv, p_S, p_gconv, s_C, s_n, s_m, s_conv, s_S, s_gconv)
```

```python
import functools

import jax
import jax.numpy as jnp
from jax import lax
from jax.experimental import pallas as pl
from jax.experimental.pallas import tpu as pltpu

F32 = jnp.float32
BF16 = jnp.bfloat16

D_MODEL = 1024
DEPTH = 2
N_HEADS = 4
DH = 128
D_MIX = N_HEADS * DH
N_EXPERTS = 32
EXPERTS_PER_GROUP = 8
N_GROUPS = 4
D_EXPERT = 256
CONV_B = 3
CONV_C = 4
HIST = 8
INV_BLOCK = 8
DN_ALPHA = (2 * DEPTH) ** 0.25
LN_EPS = 1e-5
NORM_EPS = 1e-6
NEG = -1e30

OFF_QKVC = 0
OFF_BCH = 1536
OFF_GTS = 3072
OFF_QKVO = 6144
OFF_Z = 8192
OFF_SA = 8704
OFF_SB = 8832
N_PROJ = 8960
TN_PROJ = 1280

VMEM_LIMIT = 52 * 1024 * 1024


def _dot(a, b):
    return jnp.dot(a, b, preferred_element_type=F32)


def _dot_nt(a, b):
    return lax.dot_general(a, b, (((1,), (1,)), ((), ())), preferred_element_type=F32)


def _split3(x):
    hi = x.astype(BF16)
    r = x - hi.astype(F32)
    mid = r.astype(BF16)
    lo = (r - mid.astype(F32)).astype(BF16)
    return hi, mid, lo


def _dot3(m, x):
    hi, mid, lo = _split3(x)
    return _dot(m, hi) + _dot(m, mid) + _dot(m, lo)


def _dotb(a, b):
    return _dot(a.astype(BF16), b.astype(BF16))


def _unit_lower_inverse(n, rr, cc, tl):
    eye = (rr == cc).astype(F32)
    p = jnp.where((rr // INV_BLOCK) == (cc // INV_BLOCK), n, 0.0)
    x = eye + p
    b = 2
    while b < INV_BLOCK:
        p = _dotb(p, p)
        x = x + _dotb(x, p)
        b *= 2
    b = INV_BLOCK
    while b < tl:
        off = jnp.where(((rr // (2 * b)) == (cc // (2 * b))) & ((rr // b) != (cc // b)), n, 0.0)
        x = x + _dotb(x, _dotb(off, x))
        b *= 2
    return x


def _layer_norm(x, g, b):
    mu = jnp.mean(x, axis=-1, keepdims=True)
    xc = x - mu
    var = jnp.mean(xc * xc, axis=-1, keepdims=True)
    return xc * lax.rsqrt(var + LN_EPS) * g + b


def _sigmoid(x):
    return jax.nn.sigmoid(x)


def _silu(x):
    return x * jax.nn.sigmoid(x)


def _log_sigmoid(x):
    return jnp.minimum(x, 0.0) - jnp.log1p(jnp.exp(-jnp.abs(x)))


def _softplus(x):
    return jnp.maximum(x, 0.0) + jnp.log1p(jnp.exp(-jnp.abs(x)))


def _seq_rows(x3, tl):
    g, _, c = x3.shape
    return jnp.broadcast_to(x3, (g, tl, c)).reshape(g * tl, c)


def _ada_body(c_ref, w_ref, b_ref, o_ref):
    c = c_ref[...]
    s = _silu(c).astype(BF16)
    o_ref[0] = _dot(s, w_ref[0].astype(BF16)) + b_ref[0]


def _ada(c_all, w_ada, b_ada):
    nb = c_all.shape[0]
    return pl.pallas_call(
        _ada_body,
        grid=(DEPTH, 6),
        in_specs=[pl.BlockSpec((nb, D_MODEL), lambda l, j: (0, 0)),
                  pl.BlockSpec((1, D_MODEL, D_MODEL), lambda l, j: (l, 0, j)),
                  pl.BlockSpec((1, 1, D_MODEL), lambda l, j: (l, 0, j))],
        out_specs=pl.BlockSpec((1, nb, D_MODEL), lambda l, j: (l, 0, j)),
        out_shape=jax.ShapeDtypeStruct((DEPTH, nb, 6 * D_MODEL), F32),
        compiler_params=pltpu.CompilerParams(dimension_semantics=("arbitrary", "arbitrary"),
                                             vmem_limit_bytes=VMEM_LIMIT),
        name="ada",
    )(c_all, w_ada, b_ada.reshape(DEPTH, 1, 6 * D_MODEL))


def _inproj_body(x_ref, mod_ref, g_ref, b_ref, w_ref, bias_ref, *rest, apply_ln, bt, lt):
    if apply_ln:
        proj_ref, xn_ref, u_scr = rest
    else:
        proj_ref, u_scr = rest
    j = pl.program_id(2)

    @pl.when(j == 0)
    def _():
        x = x_ref[...]
        if apply_ln:
            x = _layer_norm(x, g_ref[...], b_ref[...])
            xn_ref[...] = x
        u = x * (1.0 + mod_ref[:, 1:2, :]) + mod_ref[:, 0:1, :]
        u_scr[...] = u.reshape(bt * lt, D_MODEL).astype(BF16)

    acc = _dot(u_scr[...], w_ref[0]) + bias_ref[0]
    proj_ref[...] = acc.reshape(bt, lt, TN_PROJ)


def _inproj(x, mod, ln_g, ln_b, w_r, b_r, layer, apply_ln, bt, lt):
    nb, lp, _ = x.shape
    grid = (nb // bt, lp // lt, N_PROJ // TN_PROJ)
    tok = lambda i, t, j: (i, t, 0)
    out_shape = [jax.ShapeDtypeStruct((nb, lp, N_PROJ), F32)]
    out_specs = [pl.BlockSpec((bt, lt, TN_PROJ), lambda i, t, j: (i, t, j))]
    if apply_ln:
        out_shape.append(jax.ShapeDtypeStruct((nb, lp, D_MODEL), F32))
        out_specs.append(pl.BlockSpec((bt, lt, D_MODEL), tok))
    res = pl.pallas_call(
        functools.partial(_inproj_body, apply_ln=apply_ln, bt=bt, lt=lt),
        grid=grid,
        in_specs=[pl.BlockSpec((bt, lt, D_MODEL), tok),
                  pl.BlockSpec((bt, 6, D_MODEL), lambda i, t, j: (i, 0, 0)),
                  pl.BlockSpec((1, D_MODEL), lambda i, t, j: (0, 0)),
                  pl.BlockSpec((1, D_MODEL), lambda i, t, j: (0, 0)),
                  pl.BlockSpec((1, D_MODEL, TN_PROJ), lambda i, t, j: (layer, 0, j)),
                  pl.BlockSpec((1, 1, TN_PROJ), lambda i, t, j: (layer, 0, j))],
        out_specs=out_specs,
        out_shape=out_shape,
        scratch_shapes=[pltpu.VMEM((bt * lt, D_MODEL), BF16)],
        compiler_params=pltpu.CompilerParams(dimension_semantics=("arbitrary", "arbitrary", "arbitrary"),
                                             vmem_limit_bytes=VMEM_LIMIT),
        name=f"inproj_l{layer}_b{bt}",
    )(x, mod, ln_g, ln_b, w_r, b_r)
    return (res[0], res[1]) if apply_ln else (res[0], x)


def _chunk_masks(G, tl):
    R = G * tl
    rr = lax.broadcasted_iota(jnp.int32, (R, R), 0)
    cc = lax.broadcasted_iota(jnp.int32, (R, R), 1)
    incl = rr >= cc
    if G > 1:
        incl = incl & ((rr // tl) == (cc // tl))
    diag = rr == cc
    return incl, diag


def _valid_rows(G, tl, c, lv):
    R = G * tl
    row = lax.broadcasted_iota(jnp.int32, (R, 128), 0)
    pos = (row % tl if G > 1 else row) + c * tl
    return pos < lv


def _conv_taps(xp_s, w_ref, width, tl):
    acc = None
    for j in range(width):
        tap = xp_s[:, pl.ds(HIST - (width - 1) + j, tl), :] * w_ref[j:j + 1, :].reshape(1, 1, -1)
        acc = tap if acc is None else acc + tap
    return acc


def _mlstm_body(qkvo_ref, bch_ref, sa_ref, sb_ref, C0_ref, n0_ref, m0_ref, cv0_ref, ng_ref, cw_ref,
                ya_ref, yb_ref, C_ref, n_ref, m_ref, cv_ref, xp_s, *, G, tl, NC, lv):
    c = pl.program_id(1)
    R = G * tl
    padded = lv < NC * tl
    lvl = lv - (NC - 1) * tl

    @pl.when(c == 0)
    def _():
        C_ref[...] = C0_ref[...]
        n_ref[...] = n0_ref[...]
        m_ref[...] = m0_ref[...]
        xp_s[:, 0:HIST, :] = cv0_ref[...]

    xin = bch_ref[:, :, D_MIX:2 * D_MIX] * bch_ref[:, :, 2 * D_MIX:3 * D_MIX]
    xp_s[:, HIST:HIST + tl, :] = xin
    conv = _conv_taps(xp_s, cw_ref, CONV_B, tl)
    yb_ref[...] = (bch_ref[:, :, 0:D_MIX] * conv).astype(BF16)

    @pl.when(c == NC - 1)
    def _():
        cv_ref[...] = xp_s[:, pl.ds(HIST + lvl - (CONV_B - 1), CONV_B - 1), :]

    if NC > 1:
        xp_s[:, 0:HIST, :] = xp_s[:, tl:tl + HIST, :]

    i_all = sa_ref[...].reshape(R, 128)
    f_all = _log_sigmoid(sb_ref[...].reshape(R, 128))
    if padded:
        valid = _valid_rows(G, tl, c, lv)
        i_all = jnp.where(valid, i_all, NEG)
        f_all = jnp.where(valid, f_all, 0.0)
    incl, _ = _chunk_masks(G, tl)
    bcum = _dot3(incl.astype(BF16), f_all)
    b3 = bcum.reshape(G, tl, 128)
    blast3 = b3[:, tl - 1:tl, :]
    blast = _seq_rows(blast3, tl)
    m03 = m_ref[0].reshape(G, 1, 128)
    m0 = _seq_rows(m03, tl)
    val = blast - bcum + i_all
    mnew3 = jnp.maximum(jnp.max(val.reshape(G, tl, 128), axis=1, keepdims=True), blast3 + m03)
    mnew = _seq_rows(mnew3, tl)
    wk_all = jnp.exp(val - mnew)
    dec3 = jnp.exp(blast3 + m03 - mnew3)
    minter_all = bcum + m0
    iT = i_all.T
    bT = bcum.T

    for h in range(N_HEADS):
        hs = slice(h * DH, (h + 1) * DH)
        q = qkvo_ref[:, :, h * DH:(h + 1) * DH].reshape(R, DH)
        k = qkvo_ref[:, :, D_MIX + h * DH:D_MIX + (h + 1) * DH].reshape(R, DH) * (DH ** -0.5)
        v = qkvo_ref[:, :, 2 * D_MIX + h * DH:2 * D_MIX + (h + 1) * DH].reshape(R, DH)
        o = qkvo_ref[:, :, 3 * D_MIX + h * DH:3 * D_MIX + (h + 1) * DH].reshape(R, DH)
        qb, kb, vb = q.astype(BF16), k.astype(BF16), v.astype(BF16)

        b_col = bcum[:, h:h + 1]
        dlog = jnp.where(incl, b_col - bT[h:h + 1, :] + iT[h:h + 1, :], NEG)
        m_inter = minter_all[:, h:h + 1]
        m_t = jnp.maximum(m_inter, jnp.max(dlog, axis=-1, keepdims=True))
        s = _dot_nt(qb, kb) * jnp.exp(dlog - m_t)
        inter = jnp.exp(m_inter - m_t)

        Ch = C_ref[:, h]
        nh = n_ref[:, h:h + 1, :]
        q3 = q.reshape(G, tl, DH)
        qC = jnp.einsum('gtd,gde->gte', q3.astype(BF16), Ch.astype(BF16),
                        preferred_element_type=F32).reshape(R, DH)
        qn = jnp.sum(q3 * nh, axis=-1, keepdims=True).reshape(R, 1)
        num = _dot(s.astype(BF16), vb) + inter * qC
        den = jnp.sum(s, axis=-1, keepdims=True) + inter * qn
        hh = num / jnp.maximum(jnp.abs(den), jnp.exp(-m_t))

        mu = jnp.mean(hh, axis=-1, keepdims=True)
        hc = hh - mu
        hn = hc * lax.rsqrt(jnp.mean(hc * hc, axis=-1, keepdims=True) + LN_EPS)
        ya = _sigmoid(o) * hn * ng_ref[:, hs]
        ya_ref[:, :, h * DH:(h + 1) * DH] = ya.reshape(G, tl, DH).astype(BF16)

        kw3 = (k * wk_all[:, h:h + 1]).reshape(G, tl, DH)
        dec = dec3[:, :, h:h + 1]
        dC = jnp.einsum('gtd,gte->gde', kw3.astype(BF16), vb.reshape(G, tl, DH),
                        preferred_element_type=F32)
        C_ref[:, h] = dec * Ch + dC
        n_ref[:, h:h + 1, :] = dec * nh + jnp.sum(kw3, axis=1, keepdims=True)

    m_ref[0] = mnew3.reshape(G, 128)


def _mlstm(proj, C0, n0, m0, cv0, ng, cw, G, tl, lv):
    nb, lp, _ = proj.shape
    NC = lp // tl
    seq4 = lambda i, c: (i, 0, 0, 0)
    seq3 = lambda i, c: (i, 0, 0)
    par = lambda i, c: (0, 0)
    return pl.pallas_call(
        functools.partial(_mlstm_body, G=G, tl=tl, NC=NC, lv=lv),
        grid=(nb // G, NC),
        in_specs=[pl.BlockSpec((G, tl, 4 * D_MIX), lambda i, c: (i, c, OFF_QKVO // (4 * D_MIX))),
                  pl.BlockSpec((G, tl, 3 * D_MIX), lambda i, c: (i, c, OFF_BCH // (3 * D_MIX))),
                  pl.BlockSpec((G, tl, 128), lambda i, c: (i, c, OFF_SA // 128)),
                  pl.BlockSpec((G, tl, 128), lambda i, c: (i, c, OFF_SB // 128)),
                  pl.BlockSpec((G, N_HEADS, DH, DH), seq4),
                  pl.BlockSpec((G, N_HEADS, DH), seq3),
                  pl.BlockSpec((1, G, 128), seq3),
                  pl.BlockSpec((G, HIST, D_MIX), seq3),
                  pl.BlockSpec((1, D_MIX), par),
                  pl.BlockSpec((8, D_MIX), par)],
        out_specs=[pl.BlockSpec((G, tl, D_MIX), lambda i, c: (i, c, 0)),
                   pl.BlockSpec((G, tl, D_MIX), lambda i, c: (i, c, 0)),
                   pl.BlockSpec((G, N_HEADS, DH, DH), seq4),
                   pl.BlockSpec((G, N_HEADS, DH), seq3),
                   pl.BlockSpec((1, G, 128), seq3),
                   pl.BlockSpec((G, CONV_B - 1, D_MIX), seq3)],
        out_shape=[jax.ShapeDtypeStruct((nb, lp, D_MIX), BF16),
                   jax.ShapeDtypeStruct((nb, lp, D_MIX), BF16),
                   jax.ShapeDtypeStruct((nb, N_HEADS, DH, DH), F32),
                   jax.ShapeDtypeStruct((nb, N_HEADS, DH), F32),
                   jax.ShapeDtypeStruct((nb // G, G, 128), F32),
                   jax.ShapeDtypeStruct((nb, CONV_B - 1, D_MIX), F32)],
        scratch_shapes=[pltpu.VMEM((G, HIST + tl, D_MIX), F32)],
        compiler_params=pltpu.CompilerParams(dimension_semantics=("arbitrary", "arbitrary"),
                                             vmem_limit_bytes=VMEM_LIMIT),
        name=f"mlstm_g{G}",
    )(proj, proj, proj, proj, C0, n0, m0, cv0, ng, cw)


def _gdn_body(qkvc_ref, z_ref, sa_ref, sb_ref, S0_ref, gc0_ref, cw_ref, alog_ref, dtb_ref, gng_ref,
              yc_ref, S_ref, gcs_ref, xp_s, *, G, tl, NC, lv):
    c = pl.program_id(1)
    R = G * tl
    padded = lv < NC * tl
    lvl = lv - (NC - 1) * tl

    @pl.when(c == 0)
    def _():
        S_ref[...] = S0_ref[...]
        xp_s[:, 0:HIST, :] = gc0_ref[...]

    xp_s[:, HIST:HIST + tl, :] = qkvc_ref[...]
    qkv = _silu(_conv_taps(xp_s, cw_ref, CONV_C, tl)).reshape(R, 3 * D_MIX)

    @pl.when(c == NC - 1)
    def _():
        gcs_ref[...] = xp_s[:, pl.ds(HIST + lvl - (CONV_C - 1), CONV_C - 1), :]

    if NC > 1:
        xp_s[:, 0:HIST, :] = xp_s[:, tl:tl + HIST, :]

    beta_all = _sigmoid(sa_ref[...].reshape(R, 128))
    g_all = -jnp.exp(alog_ref[...]) * _softplus(sb_ref[...].reshape(R, 128) + dtb_ref[...])
    if padded:
        valid = _valid_rows(G, tl, c, lv)
        beta_all = jnp.where(valid, beta_all, 0.0)
        g_all = jnp.where(valid, g_all, 0.0)
    incl, diag = _chunk_masks(G, tl)
    rr = lax.broadcasted_iota(jnp.int32, (R, R), 0)
    cc = lax.broadcasted_iota(jnp.int32, (R, R), 1)
    gam = _dot3(incl.astype(BF16), g_all)
    glast3 = gam.reshape(G, tl, 128)[:, tl - 1:tl, :]
    egam = jnp.exp(gam)
    kdec_all = jnp.exp(_seq_rows(glast3, tl) - gam)
    eglast3 = jnp.exp(glast3)
    gT = gam.T

    for h in range(N_HEADS):
        ch = 4 + h
        hs = slice(h * DH, (h + 1) * DH)
        qh = qkv[:, h * DH:(h + 1) * DH]
        kh = qkv[:, D_MIX + h * DH:D_MIX + (h + 1) * DH]
        vh = qkv[:, 2 * D_MIX + h * DH:2 * D_MIX + (h + 1) * DH]
        qh = qh * lax.rsqrt(jnp.sum(qh * qh, axis=-1, keepdims=True) + NORM_EPS) * (DH ** -0.5)
        kh = kh * lax.rsqrt(jnp.sum(kh * kh, axis=-1, keepdims=True) + NORM_EPS)
        qb, kb = qh.astype(BF16), kh.astype(BF16)

        gam_col = gam[:, ch:ch + 1]
        beta_col = beta_all[:, ch:ch + 1]
        egam_col = egam[:, ch:ch + 1]
        dmat = jnp.exp(jnp.where(incl, gam_col - gT[ch:ch + 1, :], NEG))
        nmat = jnp.where(diag, 0.0, -(beta_col * _dot_nt(kb, kb) * dmat))
        r = jnp.concatenate([beta_col * vh, (beta_col * egam_col) * kh], axis=-1)
        r = _dotb(_unit_lower_inverse(nmat, rr, cc, tl), r)
        u = r[:, 0:DH]
        w = r[:, DH:2 * DH]

        Sh = S_ref[:, h]
        Sb = Sh.astype(BF16)
        wS = jnp.einsum('gtd,gde->gte', w.reshape(G, tl, DH).astype(BF16), Sb,
                        preferred_element_type=F32).reshape(R, DH)
        v_new = u - wS
        vnb = v_new.astype(BF16)
        qkm = _dot_nt(qb, kb) * dmat
        qg = (qh * egam_col).reshape(G, tl, DH).astype(BF16)
        o = (jnp.einsum('gtd,gde->gte', qg, Sb, preferred_element_type=F32).reshape(R, DH)
             + _dot(qkm.astype(BF16), vnb))
        kd3 = (kh * kdec_all[:, ch:ch + 1]).reshape(G, tl, DH).astype(BF16)
        S_ref[:, h] = eglast3[:, :, ch:ch + 1] * Sh + jnp.einsum(
            'gtd,gte->gde', kd3, vnb.reshape(G, tl, DH), preferred_element_type=F32)

        on = o * lax.rsqrt(jnp.mean(o * o, axis=-1, keepdims=True) + NORM_EPS) * gng_ref[...]
        z = z_ref[:, :, h * DH:(h + 1) * DH].reshape(R, DH)
        yc_ref[:, :, h * DH:(h + 1) * DH] = (on * _silu(z)).reshape(G, tl, DH).astype(BF16)


def _gdn(proj, S0, gc0, cw, alog, dtb, gng, G, tl, lv):
    nb, lp, _ = proj.shape
    NC = lp // tl
    assert tl % INV_BLOCK == 0 and (tl // INV_BLOCK) & (tl // INV_BLOCK - 1) == 0
    seq4 = lambda i, c: (i, 0, 0, 0)
    seq3 = lambda i, c: (i, 0, 0)
    par = lambda i, c: (0, 0)
    return pl.pallas_call(
        functools.partial(_gdn_body, G=G, tl=tl, NC=NC, lv=lv),
        grid=(nb // G, NC),
        in_specs=[pl.BlockSpec((G, tl, 3 * D_MIX), lambda i, c: (i, c, OFF_QKVC // (3 * D_MIX))),
                  pl.BlockSpec((G, tl, D_MIX), lambda i, c: (i, c, OFF_Z // D_MIX)),
                  pl.BlockSpec((G, tl, 128), lambda i, c: (i, c, OFF_SA // 128)),
                  pl.BlockSpec((G, tl, 128), lambda i, c: (i, c, OFF_SB // 128)),
                  pl.BlockSpec((G, N_HEADS, DH, DH), seq4),
                  pl.BlockSpec((G, HIST, 3 * D_MIX), seq3),
                  pl.BlockSpec((8, 3 * D_MIX), par),
                  pl.BlockSpec((1, 128), par),
                  pl.BlockSpec((1, 128), par),
                  pl.BlockSpec((1, DH), par)],
        out_specs=[pl.BlockSpec((G, tl, D_MIX), lambda i, c: (i, c, 0)),
                   pl.BlockSpec((G, N_HEADS, DH, DH), seq4),
                   pl.BlockSpec((G, CONV_C - 1, 3 * D_MIX), seq3)],
        out_shape=[jax.ShapeDtypeStruct((nb, lp, D_MIX), BF16),
                   jax.ShapeDtypeStruct((nb, N_HEADS, DH, DH), F32),
                   jax.ShapeDtypeStruct((nb, CONV_C - 1, 3 * D_MIX), F32)],
        scratch_shapes=[pltpu.VMEM((G, HIST + tl, 3 * D_MIX), F32)],
        compiler_params=pltpu.CompilerParams(dimension_semantics=("arbitrary", "arbitrary"),
                                             vmem_limit_bytes=VMEM_LIMIT),
        name=f"gdn_g{G}",
    )(proj, proj, proj, proj, S0, gc0, cw, alog, dtb, gng)


def _route(rl):
    lane = lax.broadcasted_iota(jnp.int32, rl.shape, 1).astype(F32)
    is_g = (lane >= N_EXPERTS) & (lane < N_EXPERTS + N_GROUPS)
    gl = jnp.where(is_g, rl, NEG)
    gmax = jnp.max(gl, axis=-1, keepdims=True)
    grp = jnp.min(jnp.where(gl == gmax, lane - N_EXPERTS, 4.0 * N_EXPERTS), axis=-1, keepdims=True)
    p_grp = 1.0 / jnp.sum(jnp.where(is_g, jnp.exp(gl - gmax), 0.0), axis=-1, keepdims=True)
    lo = grp * EXPERTS_PER_GROUP
    in_grp = (lane >= lo) & (lane < lo + EXPERTS_PER_GROUP)
    el = jnp.where(in_grp, rl, NEG)
    m1 = jnp.max(el, axis=-1, keepdims=True)
    i1 = jnp.min(jnp.where(el == m1, lane, 4.0 * N_EXPERTS), axis=-1, keepdims=True)
    el2 = jnp.where(lane == i1, NEG, el)
    m2 = jnp.max(el2, axis=-1, keepdims=True)
    i2 = jnp.min(jnp.where(el2 == m2, lane, 4.0 * N_EXPERTS), axis=-1, keepdims=True)
    e2 = jnp.exp(m2 - m1)
    w1 = p_grp / (1.0 + e2)
    w2 = p_grp * e2 / (1.0 + e2)
    return jnp.where(lane == i1, w1, 0.0) + jnp.where(lane == i2, w2, 0.0)


def _merge_body(ya_ref, yb_ref, yc_ref, ga_ref, gb_ref, gc_ref, x_ref, mod_ref, wa_ref, wb_ref, wc_ref, wo_ref,
                lg_ref, lb_ref, wrh_ref, wrl_ref, br_ref, x1_ref, gates_ref, *, bt, lt):
    tm = bt * lt

    def r2(ref):
        return ref[...].reshape(tm, ref.shape[-1])

    merged = (_sigmoid(r2(ga_ref)) * _dot(r2(ya_ref), wa_ref[0])
              + _sigmoid(r2(gb_ref)) * _dot(r2(yb_ref), wb_ref[0])
              + _sigmoid(r2(gc_ref)) * _dot(r2(yc_ref), wc_ref[0]))
    out = _dot(merged.astype(BF16), wo_ref[0])
    y = DN_ALPHA * x_ref[...] + (1.0 + mod_ref[:, 2:3, :]) * out.reshape(bt, lt, D_MODEL)
    x1 = _layer_norm(y, lg_ref[0], lb_ref[0])
    x1_ref[...] = x1
    u2 = (x1 * (1.0 + mod_ref[:, 4:5, :]) + mod_ref[:, 3:4, :]).reshape(tm, D_MODEL)
    hi = u2.astype(BF16)
    lo = (u2 - hi.astype(F32)).astype(BF16)
    rl = _dot(hi, wrh_ref[0]) + _dot(lo, wrh_ref[0]) + _dot(hi, wrl_ref[0]) + br_ref[0]
    gates_ref[...] = _route(rl).reshape(bt, lt, 128)


def _merge(ya, yb, yc, proj, x, mod, wa, wb, wc, wo, lg, lb, wrh, wrl, br, layer, bt, lt):
    nb, lp, _ = x.shape
    tok = lambda i, t: (i, t, 0)
    wsp = lambda shape: pl.BlockSpec((1,) + shape, lambda i, t: (layer, 0, 0))
    g0 = OFF_GTS // D_MODEL
    return pl.pallas_call(
        functools.partial(_merge_body, bt=bt, lt=lt),
        grid=(nb // bt, lp // lt),
        in_specs=[pl.BlockSpec((bt, lt, D_MIX), tok),
                  pl.BlockSpec((bt, lt, D_MIX), tok),
                  pl.BlockSpec((bt, lt, D_MIX), tok),
                  pl.BlockSpec((bt, lt, D_MODEL), lambda i, t: (i, t, g0)),
                  pl.BlockSpec((bt, lt, D_MODEL), lambda i, t: (i, t, g0 + 1)),
                  pl.BlockSpec((bt, lt, D_MODEL), lambda i, t: (i, t, g0 + 2)),
                  pl.BlockSpec((bt, lt, D_MODEL), tok),
                  pl.BlockSpec((bt, 6, D_MODEL), lambda i, t: (i, 0, 0)),
                  wsp((D_MIX, D_MODEL)), wsp((D_MIX, D_MODEL)), wsp((D_MIX, D_MODEL)), wsp((D_MODEL, D_MODEL)),
                  wsp((1, D_MODEL)), wsp((1, D_MODEL)),
                  wsp((D_MODEL, 128)), wsp((D_MODEL, 128)), wsp((1, 128))],
        out_specs=[pl.BlockSpec((bt, lt, D_MODEL), tok),
                   pl.BlockSpec((bt, lt, 128), tok)],
        out_shape=[jax.ShapeDtypeStruct((nb, lp, D_MODEL), F32),
                   jax.ShapeDtypeStruct((nb, lp, 128), F32)],
        compiler_params=pltpu.CompilerParams(dimension_semantics=("arbitrary", "arbitrary"),
                                             vmem_limit_bytes=VMEM_LIMIT),
        name=f"merge_l{layer}_b{bt}",
    )(ya, yb, yc, proj, proj, proj, x, mod, wa, wb, wc, wo, lg, lb, wrh, wrl, br)


def _moe_body(x1_ref, mod_ref, gates_ref, wg_ref, wu_ref, wd_ref, lg_ref, lb_ref, x2_ref, u_scr, acc_scr, *, bt, lt):
    tm = bt * lt
    e = pl.program_id(2)

    @pl.when(e == 0)
    def _():
        u2 = x1_ref[...] * (1.0 + mod_ref[:, 4:5, :]) + mod_ref[:, 3:4, :]
        u_scr[...] = u2.reshape(tm, D_MODEL).astype(BF16)
        acc_scr[...] = jnp.zeros_like(acc_scr)

    u = u_scr[...]
    hb = _silu(_dot(u, wg_ref[0, 0].astype(BF16))) * _dot(u, wu_ref[0, 0].astype(BF16))
    out = _dot(hb.astype(BF16), wd_ref[0, 0].astype(BF16))
    gates = gates_ref[...].reshape(tm, 128)
    lane = lax.broadcasted_iota(jnp.int32, (tm, 128), 1)
    gcol = jnp.sum(jnp.where(lane == e, gates, 0.0), axis=-1, keepdims=True)
    acc_scr[...] += jnp.where(gcol != 0.0, out * gcol, 0.0)

    @pl.when(e == N_EXPERTS - 1)
    def _():
        y = DN_ALPHA * x1_ref[...] + (1.0 + mod_ref[:, 5:6, :]) * acc_scr[...].reshape(bt, lt, D_MODEL)
        x2_ref[...] = _layer_norm(y, lg_ref[0], lb_ref[0])


def _moe(x1, mod, gates, wg, wu, wd, lg, lb, layer, bt, lt):
    nb, lp, _ = x1.shape
    tok = lambda i, t, e: (i, t, 0)
    return pl.pallas_call(
        functools.partial(_moe_body, bt=bt, lt=lt),
        grid=(nb // bt, lp // lt, N_EXPERTS),
        in_specs=[pl.BlockSpec((bt, lt, D_MODEL), tok),
                  pl.BlockSpec((bt, 6, D_MODEL), lambda i, t, e: (i, 0, 0)),
                  pl.BlockSpec((bt, lt, 128), tok),
                  pl.BlockSpec((1, 1, D_MODEL, D_EXPERT), lambda i, t, e: (layer, e, 0, 0)),
                  pl.BlockSpec((1, 1, D_MODEL, D_EXPERT), lambda i, t, e: (layer, e, 0, 0)),
                  pl.BlockSpec((1, 1, D_EXPERT, D_MODEL), lambda i, t, e: (layer, e, 0, 0)),
                  pl.BlockSpec((1, 1, D_MODEL), lambda i, t, e: (layer, 0, 0)),
                  pl.BlockSpec((1, 1, D_MODEL), lambda i, t, e: (layer, 0, 0))],
        out_specs=pl.BlockSpec((bt, lt, D_MODEL), tok),
        out_shape=jax.ShapeDtypeStruct((nb, lp, D_MODEL), F32),
        scratch_shapes=[pltpu.VMEM((bt * lt, D_MODEL), BF16), pltpu.VMEM((bt * lt, D_MODEL), F32)],
        compiler_params=pltpu.CompilerParams(dimension_semantics=("arbitrary", "arbitrary", "arbitrary"),
                                             vmem_limit_bytes=VMEM_LIMIT),
        name=f"moe_l{layer}_b{bt}",
    )(x1, mod, gates, wg, wu, wd, lg, lb)


def _hist(state):
    return jnp.pad(state, ((0, 0), (HIST - state.shape[1], 0), (0, 0)))


def _trunk(x, mod, st, p, bt, lt, G, tl, lv):
    nb = x.shape[0]
    new = {key: [] for key in ('C', 'n', 'm', 'conv', 'S', 'gconv')}
    for l in range(DEPTH):
        proj, x = _inproj(x, mod[l], p['ln_in_g'], p['ln_in_b'], p['w_in_r'], p['b_in_r'], l, l == 0, bt, lt)
        m0 = jnp.pad(st['m'][l], ((0, 0), (0, 128 - N_HEADS))).reshape(nb // G, G, 128)
        ya, yb, C, n, m, conv = _mlstm(proj, st['C'][l], st['n'][l], m0, _hist(st['conv'][l]),
                                       p['mlstm_norm_g'][l:l + 1], p['conv_b_w8'][l], G, tl, lv)
        yc, S, gconv = _gdn(proj, st['S'][l], _hist(st['gconv'][l]), p['conv_c_w8'][l], p['alog_row'][l],
                            p['dtb_row'][l], p['gdn_norm_g'][l:l + 1], G, tl, lv)
        x1, gates = _merge(ya, yb, yc, proj, x, mod[l], p['w_br_a'], p['w_br_b'], p['w_br_c'], p['w_out'],
                           p['ln1_g'], p['ln1_b'], p['wr_hi'], p['wr_lo'], p['br'], l, bt, lt)
        x = _moe(x1, mod[l], gates, p['exp_w_gate'], p['exp_w_up'], p['exp_w_down'], p['ln2_g'], p['ln2_b'],
                 l, bt, lt)
        new['C'].append(C)
        new['n'].append(n)
        new['m'].append(m.reshape(nb, 128)[:, :N_HEADS])
        new['conv'].append(conv)
        new['S'].append(S)
        new['gconv'].append(gconv)
    return x, {key: jnp.stack(val) for key, val in new.items()}


def kernel(x_prompt, x_sample, state_mlstm_C, state_mlstm_n, state_mlstm_m, state_conv, state_gdn_S, state_gdn_conv, c_prompt, c_sample, ln_in_g, ln_in_b, w_ada, b_ada, w_in, b_in, mlstm_norm_g, conv_b_w, conv_c_w, gdn_a_log, gdn_dt_bias, gdn_norm_g, w_br_a, w_br_b, w_br_c, w_out, ln1_g, ln1_b, router_g_w, router_g_b, router_e_w, router_e_b, exp_w_gate, exp_w_up, exp_w_down, ln2_g, ln2_b):
    nbp, lp, _ = x_prompt.shape
    nbs, ls, _ = x_sample.shape
    lsp = 8

    def regroup(a):
        parts = [a[..., 3592:5128], a[..., 2056:3592], a[..., 5648:8720], a[..., 0:2048], a[..., 5128:5640],
                 a[..., 2048:2052], a[..., 5640:5644], jnp.zeros(a.shape[:-1] + (120,), a.dtype),
                 a[..., 2052:2056], a[..., 5644:5648], jnp.zeros(a.shape[:-1] + (120,), a.dtype)]
        return jnp.concatenate(parts, axis=-1)

    wr = jnp.concatenate([router_e_w, router_g_w, jnp.zeros((DEPTH, D_MODEL, 128 - N_EXPERTS - N_GROUPS), F32)], axis=-1)
    wr_hi = wr.astype(BF16)
    lane_pad = lambda a: jnp.pad(a, ((0, 0), (4, 128 - 4 - N_HEADS)))[:, None, :]
    p = dict(
        ln_in_g=ln_in_g.reshape(1, D_MODEL), ln_in_b=ln_in_b.reshape(1, D_MODEL),
        w_in_r=regroup(w_in).astype(BF16), b_in_r=regroup(b_in).reshape(DEPTH, 1, N_PROJ),
        mlstm_norm_g=mlstm_norm_g,
        conv_b_w8=jnp.pad(conv_b_w, ((0, 0), (0, 8 - CONV_B), (0, 0))),
        conv_c_w8=jnp.pad(conv_c_w, ((0, 0), (0, 8 - CONV_C), (0, 0))),
        alog_row=lane_pad(gdn_a_log), dtb_row=lane_pad(gdn_dt_bias), gdn_norm_g=gdn_norm_g,
        w_br_a=w_br_a.astype(BF16), w_br_b=w_br_b.astype(BF16), w_br_c=w_br_c.astype(BF16),
        w_out=w_out.astype(BF16),
        ln1_g=ln1_g.reshape(DEPTH, 1, D_MODEL), ln1_b=ln1_b.reshape(DEPTH, 1, D_MODEL),
        wr_hi=wr_hi, wr_lo=(wr - wr_hi.astype(F32)).astype(BF16),
        br=jnp.concatenate([router_e_b, router_g_b, jnp.zeros((DEPTH, 128 - N_EXPERTS - N_GROUPS), F32)],
                           axis=-1).reshape(DEPTH, 1, 128),
        exp_w_gate=exp_w_gate, exp_w_up=exp_w_up, exp_w_down=exp_w_down,
        ln2_g=ln2_g.reshape(DEPTH, 1, D_MODEL), ln2_b=ln2_b.reshape(DEPTH, 1, D_MODEL),
    )

    mod = _ada(jnp.concatenate([c_prompt, c_sample], axis=0), w_ada, b_ada)
    mod = mod.reshape(DEPTH, nbp + nbs, 6, D_MODEL)

    zeros = lambda *s: jnp.zeros((DEPTH, nbp) + s, F32)
    st_p = {'C': zeros(N_HEADS, DH, DH), 'n': zeros(N_HEADS, DH), 'm': zeros(N_HEADS),
            'conv': zeros(CONV_B - 1, D_MIX), 'S': zeros(N_HEADS, DH, DH), 'gconv': zeros(CONV_C - 1, 3 * D_MIX)}
    y_p, sp = _trunk(x_prompt, mod[:, :nbp], st_p, p, bt=1, lt=512, G=1, tl=64, lv=lp)

    st_s = {'C': state_mlstm_C, 'n': state_mlstm_n, 'm': state_mlstm_m, 'conv': state_conv,
            'S': state_gdn_S, 'gconv': state_gdn_conv}
    xs = jnp.pad(x_sample, ((0, 0), (0, lsp - ls), (0, 0)))
    y_s, ss = _trunk(xs, mod[:, nbp:], st_s, p, bt=64, lt=lsp, G=16, tl=lsp, lv=ls)
    y_s = y_s[:, :ls]

    return (y_p, y_s, sp['C'], sp['n'], sp['m'], sp['conv'], sp['S'], sp['gconv'],
            ss['C'], ss['n'], ss['m'], ss['conv'], ss['S'], ss['gconv'])
```

```python
import functools

import jax
import jax.numpy as jnp
from jax import lax
from jax.experimental import pallas as pl
from jax.experimental.pallas import tpu as pltpu

F32 = jnp.float32
BF16 = jnp.bfloat16

D_MODEL = 1024
DEPTH = 2
N_HEADS = 4
DH = 128
D_MIX = N_HEADS * DH
N_EXPERTS = 32
EXPERTS_PER_GROUP = 8
N_GROUPS = 4
D_EXPERT = 256
CONV_B = 3
CONV_C = 4
HIST = 8
RB = 64
INV_BLOCK = 8
DN_ALPHA = (2 * DEPTH) ** 0.25
LN_EPS = 1e-5
NORM_EPS = 1e-6
NEG = -1e30

OFF_QKVC = 0
OFF_BCH = 1536
OFF_GTS = 3072
OFF_QKVO = 6144
OFF_Z = 8192
OFF_SA = 8704
OFF_SB = 8832
N_PROJ = 8960
TN_PROJ = 1280

VMEM_LIMIT = 52 * 1024 * 1024


def _dot(a, b):
    return jnp.dot(a, b, preferred_element_type=F32)


def _dot_nt(a, b):
    return lax.dot_general(a, b, (((1,), (1,)), ((), ())), preferred_element_type=F32)


def _split3(x):
    hi = x.astype(BF16)
    r = x - hi.astype(F32)
    mid = r.astype(BF16)
    lo = (r - mid.astype(F32)).astype(BF16)
    return hi, mid, lo


def _dot3(m, x):
    hi, mid, lo = _split3(x)
    return _dot(m, hi) + _dot(m, mid) + _dot(m, lo)


def _dotb(a, b):
    return _dot(a.astype(BF16), b.astype(BF16))


def _layer_norm(x, g, b):
    mu = jnp.mean(x, axis=-1, keepdims=True)
    xc = x - mu
    var = jnp.mean(xc * xc, axis=-1, keepdims=True)
    return xc * lax.rsqrt(var + LN_EPS) * g + b


def _sigmoid(x):
    return jax.nn.sigmoid(x)


def _silu(x):
    return x * jax.nn.sigmoid(x)


def _log_sigmoid(x):
    return jnp.minimum(x, 0.0) - jnp.log1p(jnp.exp(-jnp.abs(x)))


def _softplus(x):
    return jnp.maximum(x, 0.0) + jnp.log1p(jnp.exp(-jnp.abs(x)))


def _seq_rows(x3, tl):
    g, _, c = x3.shape
    return jnp.broadcast_to(x3, (g, tl, c)).reshape(g * tl, c)


def _ada_body(c_ref, w_ref, b_ref, o_ref):
    c = c_ref[...]
    s = _silu(c).astype(BF16)
    o_ref[0] = _dot(s, w_ref[0].astype(BF16)) + b_ref[0]


def _ada(c_all, w_ada, b_ada):
    nb = c_all.shape[0]
    return pl.pallas_call(
        _ada_body,
        grid=(DEPTH, 6),
        in_specs=[pl.BlockSpec((nb, D_MODEL), lambda l, j: (0, 0)),
                  pl.BlockSpec((1, D_MODEL, D_MODEL), lambda l, j: (l, 0, j)),
                  pl.BlockSpec((1, 1, D_MODEL), lambda l, j: (l, 0, j))],
        out_specs=pl.BlockSpec((1, nb, D_MODEL), lambda l, j: (l, 0, j)),
        out_shape=jax.ShapeDtypeStruct((DEPTH, nb, 6 * D_MODEL), F32),
        compiler_params=pltpu.CompilerParams(dimension_semantics=("arbitrary", "arbitrary"),
                                             vmem_limit_bytes=VMEM_LIMIT),
        name="ada",
    )(c_all, w_ada, b_ada.reshape(DEPTH, 1, 6 * D_MODEL))


def _inproj_body(x_ref, mod_ref, g_ref, b_ref, w_ref, bias_ref, *rest, apply_ln, bt, lt):
    if apply_ln:
        proj_ref, xn_ref, u_scr = rest
    else:
        proj_ref, u_scr = rest
    j = pl.program_id(2)

    @pl.when(j == 0)
    def _():
        x = x_ref[...]
        if apply_ln:
            x = _layer_norm(x, g_ref[...], b_ref[...])
            xn_ref[...] = x
        u = x * (1.0 + mod_ref[:, 1:2, :]) + mod_ref[:, 0:1, :]
        u_scr[...] = u.reshape(bt * lt, D_MODEL).astype(BF16)

    acc = _dot(u_scr[...], w_ref[0]) + bias_ref[0]
    proj_ref[...] = acc.reshape(bt, lt, TN_PROJ)


def _inproj(x, mod, ln_g, ln_b, w_r, b_r, layer, apply_ln, bt, lt):
    nb, lp, _ = x.shape
    grid = (nb // bt, lp // lt, N_PROJ // TN_PROJ)
    tok = lambda i, t, j: (i, t, 0)
    out_shape = [jax.ShapeDtypeStruct((nb, lp, N_PROJ), F32)]
    out_specs = [pl.BlockSpec((bt, lt, TN_PROJ), lambda i, t, j: (i, t, j))]
    if apply_ln:
        out_shape.append(jax.ShapeDtypeStruct((nb, lp, D_MODEL), F32))
        out_specs.append(pl.BlockSpec((bt, lt, D_MODEL), tok))
    res = pl.pallas_call(
        functools.partial(_inproj_body, apply_ln=apply_ln, bt=bt, lt=lt),
        grid=grid,
        in_specs=[pl.BlockSpec((bt, lt, D_MODEL), tok),
                  pl.BlockSpec((bt, 6, D_MODEL), lambda i, t, j: (i, 0, 0)),
                  pl.BlockSpec((1, D_MODEL), lambda i, t, j: (0, 0)),
                  pl.BlockSpec((1, D_MODEL), lambda i, t, j: (0, 0)),
                  pl.BlockSpec((1, D_MODEL, TN_PROJ), lambda i, t, j: (layer, 0, j)),
                  pl.BlockSpec((1, 1, TN_PROJ), lambda i, t, j: (layer, 0, j))],
        out_specs=out_specs,
        out_shape=out_shape,
        scratch_shapes=[pltpu.VMEM((bt * lt, D_MODEL), BF16)],
        compiler_params=pltpu.CompilerParams(dimension_semantics=("arbitrary", "arbitrary", "arbitrary"),
                                             vmem_limit_bytes=VMEM_LIMIT),
        name=f"inproj_l{layer}_b{bt}",
    )(x, mod, ln_g, ln_b, w_r, b_r)
    return (res[0], res[1]) if apply_ln else (res[0], x)


def _chunk_masks(G, tl):
    R = G * tl
    rr = lax.broadcasted_iota(jnp.int32, (R, R), 0)
    cc = lax.broadcasted_iota(jnp.int32, (R, R), 1)
    incl = rr >= cc
    if G > 1:
        incl = incl & ((rr // tl) == (cc // tl))
    diag = rr == cc
    return incl, diag


def _valid_rows(G, tl, c, lv):
    R = G * tl
    row = lax.broadcasted_iota(jnp.int32, (R, 128), 0)
    pos = (row % tl if G > 1 else row) + c * tl
    return pos < lv


def _conv_taps(xp_s, w_ref, width, tl):
    acc = None
    for j in range(width):
        tap = xp_s[:, pl.ds(HIST - (width - 1) + j, tl), :] * w_ref[j:j + 1, :].reshape(1, 1, -1)
        acc = tap if acc is None else acc + tap
    return acc


def _mlstm_body(qkvo_ref, bch_ref, sa_ref, sb_ref, C0_ref, n0_ref, m0_ref, cv0_ref, ng_ref, cw_ref,
                ya_ref, yb_ref, C_ref, n_ref, m_ref, cv_ref, xp_s, *, G, tl, NC, lv):
    c = pl.program_id(1)
    R = G * tl
    padded = lv < NC * tl
    lvl = lv - (NC - 1) * tl

    @pl.when(c == 0)
    def _():
        C_ref[...] = C0_ref[...]
        n_ref[...] = n0_ref[...]
        m_ref[...] = m0_ref[...]
        xp_s[:, 0:HIST, :] = cv0_ref[...]

    xin = bch_ref[:, :, D_MIX:2 * D_MIX] * bch_ref[:, :, 2 * D_MIX:3 * D_MIX]
    xp_s[:, HIST:HIST + tl, :] = xin
    conv = _conv_taps(xp_s, cw_ref, CONV_B, tl)
    yb_ref[...] = (bch_ref[:, :, 0:D_MIX] * conv).astype(BF16)

    @pl.when(c == NC - 1)
    def _():
        cv_ref[...] = xp_s[:, pl.ds(HIST + lvl - (CONV_B - 1), CONV_B - 1), :]

    if NC > 1:
        xp_s[:, 0:HIST, :] = xp_s[:, tl:tl + HIST, :]

    i_all = sa_ref[...].reshape(R, 128)
    f_all = _log_sigmoid(sb_ref[...].reshape(R, 128))
    if padded:
        valid = _valid_rows(G, tl, c, lv)
        i_all = jnp.where(valid, i_all, NEG)
        f_all = jnp.where(valid, f_all, 0.0)
    incl, _ = _chunk_masks(G, tl)
    bcum = _dot3(incl.astype(BF16), f_all)
    b3 = bcum.reshape(G, tl, 128)
    blast3 = b3[:, tl - 1:tl, :]
    blast = _seq_rows(blast3, tl)
    m03 = m_ref[0].reshape(G, 1, 128)
    m0 = _seq_rows(m03, tl)
    val = blast - bcum + i_all
    mnew3 = jnp.maximum(jnp.max(val.reshape(G, tl, 128), axis=1, keepdims=True), blast3 + m03)
    mnew = _seq_rows(mnew3, tl)
    wk_all = jnp.exp(val - mnew)
    dec3 = jnp.exp(blast3 + m03 - mnew3)
    minter_all = bcum + m0
    iT = i_all.T
    bT = bcum.T

    for h in range(N_HEADS):
        hs = slice(h * DH, (h + 1) * DH)
        q = qkvo_ref[:, :, h * DH:(h + 1) * DH].reshape(R, DH)
        k = qkvo_ref[:, :, D_MIX + h * DH:D_MIX + (h + 1) * DH].reshape(R, DH) * (DH ** -0.5)
        v = qkvo_ref[:, :, 2 * D_MIX + h * DH:2 * D_MIX + (h + 1) * DH].reshape(R, DH)
        o = qkvo_ref[:, :, 3 * D_MIX + h * DH:3 * D_MIX + (h + 1) * DH].reshape(R, DH)
        qb, kb, vb = q.astype(BF16), k.astype(BF16), v.astype(BF16)

        b_col = bcum[:, h:h + 1]
        dlog = jnp.where(incl, b_col - bT[h:h + 1, :] + iT[h:h + 1, :], NEG)
        m_inter = minter_all[:, h:h + 1]
        m_t = jnp.maximum(m_inter, jnp.max(dlog, axis=-1, keepdims=True))
        s = _dot_nt(qb, kb) * jnp.exp(dlog - m_t)
        inter = jnp.exp(m_inter - m_t)

        Ch = C_ref[:, h]
        nh = n_ref[:, h:h + 1, :]
        q3 = q.reshape(G, tl, DH)
        qC = jnp.einsum('gtd,gde->gte', q3.astype(BF16), Ch.astype(BF16),
                        preferred_element_type=F32).reshape(R, DH)
        qn = jnp.sum(q3 * nh, axis=-1, keepdims=True).reshape(R, 1)
        num = _dot(s.astype(BF16), vb) + inter * qC
        den = jnp.sum(s, axis=-1, keepdims=True) + inter * qn
        hh = num / jnp.maximum(jnp.abs(den), jnp.exp(-m_t))

        mu = jnp.mean(hh, axis=-1, keepdims=True)
        hc = hh - mu
        hn = hc * lax.rsqrt(jnp.mean(hc * hc, axis=-1, keepdims=True) + LN_EPS)
        ya = _sigmoid(o) * hn * ng_ref[:, hs]
        ya_ref[:, :, h * DH:(h + 1) * DH] = ya.reshape(G, tl, DH).astype(BF16)

        kw3 = (k * wk_all[:, h:h + 1]).reshape(G, tl, DH)
        dec = dec3[:, :, h:h + 1]
        dC = jnp.einsum('gtd,gte->gde', kw3.astype(BF16), vb.reshape(G, tl, DH),
                        preferred_element_type=F32)
        C_ref[:, h] = dec * Ch + dC
        n_ref[:, h:h + 1, :] = dec * nh + jnp.sum(kw3, axis=1, keepdims=True)

    m_ref[0] = mnew3.reshape(G, 128)


def _mlstm(proj, C0, n0, m0, cv0, ng, cw, G, tl, lv):
    nb, lp, _ = proj.shape
    NC = lp // tl
    seq4 = lambda i, c: (i, 0, 0, 0)
    seq3 = lambda i, c: (i, 0, 0)
    par = lambda i, c: (0, 0)
    return pl.pallas_call(
        functools.partial(_mlstm_body, G=G, tl=tl, NC=NC, lv=lv),
        grid=(nb // G, NC),
        in_specs=[pl.BlockSpec((G, tl, 4 * D_MIX), lambda i, c: (i, c, OFF_QKVO // (4 * D_MIX))),
                  pl.BlockSpec((G, tl, 3 * D_MIX), lambda i, c: (i, c, OFF_BCH // (3 * D_MIX))),
                  pl.BlockSpec((G, tl, 128), lambda i, c: (i, c, OFF_SA // 128)),
                  pl.BlockSpec((G, tl, 128), lambda i, c: (i, c, OFF_SB // 128)),
                  pl.BlockSpec((G, N_HEADS, DH, DH), seq4),
                  pl.BlockSpec((G, N_HEADS, DH), seq3),
                  pl.BlockSpec((1, G, 128), seq3),
                  pl.BlockSpec((G, HIST, D_MIX), seq3),
                  pl.BlockSpec((1, D_MIX), par),
                  pl.BlockSpec((8, D_MIX), par)],
        out_specs=[pl.BlockSpec((G, tl, D_MIX), lambda i, c: (i, c, 0)),
                   pl.BlockSpec((G, tl, D_MIX), lambda i, c: (i, c, 0)),
                   pl.BlockSpec((G, N_HEADS, DH, DH), seq4),
                   pl.BlockSpec((G, N_HEADS, DH), seq3),
                   pl.BlockSpec((1, G, 128), seq3),
                   pl.BlockSpec((G, CONV_B - 1, D_MIX), seq3)],
        out_shape=[jax.ShapeDtypeStruct((nb, lp, D_MIX), BF16),
                   jax.ShapeDtypeStruct((nb, lp, D_MIX), BF16),
                   jax.ShapeDtypeStruct((nb, N_HEADS, DH, DH), F32),
                   jax.ShapeDtypeStruct((nb, N_HEADS, DH), F32),
                   jax.ShapeDtypeStruct((nb // G, G, 128), F32),
                   jax.ShapeDtypeStruct((nb, CONV_B - 1, D_MIX), F32)],
        scratch_shapes=[pltpu.VMEM((G, HIST + tl, D_MIX), F32)],
        compiler_params=pltpu.CompilerParams(dimension_semantics=("arbitrary", "arbitrary"),
                                             vmem_limit_bytes=VMEM_LIMIT),
        name=f"mlstm_g{G}",
    )(proj, proj, proj, proj, C0, n0, m0, cv0, ng, cw)


def _heads(x, nb, width):
    return jnp.stack([x[:, :, h * width:(h + 1) * width] for h in range(N_HEADS)],
                     axis=1).reshape(nb * N_HEADS, RB, width)


def _gate_cols(x, nb):
    return jnp.stack([x[:, :, 4 + h:5 + h] for h in range(N_HEADS)], axis=1).reshape(nb * N_HEADS, RB, 1)


def _gate_rows(x, nb):
    xt = jnp.swapaxes(x, 1, 2)
    return jnp.stack([xt[:, 4 + h:5 + h, :] for h in range(N_HEADS)], axis=1).reshape(nb * N_HEADS, 1, RB)


def _bmm(a, b):
    return jnp.einsum('nts,nsu->ntu', a.astype(BF16), b.astype(BF16), preferred_element_type=F32)


def _bmm_nt(a, b):
    return jnp.einsum('ntd,nsd->nts', a, b, preferred_element_type=F32)


def _unit_lower_inverse(n, rr, cc, tl):
    eye = (rr == cc).astype(F32)[None]
    p = jnp.where(((rr // INV_BLOCK) == (cc // INV_BLOCK))[None], n, 0.0)
    x = eye + p
    b = 2
    while b < INV_BLOCK:
        p = _bmm(p, p)
        x = x + _bmm(x, p)
        b *= 2
    b = INV_BLOCK
    while b < tl:
        off = jnp.where((((rr // (2 * b)) == (cc // (2 * b))) & ((rr // b) != (cc // b)))[None], n, 0.0)
        x = x + _bmm(x, _bmm(off, x))
        b *= 2
    return x


def _gdn_prep_body(x_ref, prev_ref, sa_ref, sb_ref, gc0_ref, cw_ref, alog_ref, dtb_ref,
                   u_ref, w_ref, qg_ref, kd_ref, qkm_ref, eg_ref, gcs_ref, xp_s, *, NB, G, tl, tlx, NCS, lv):
    cs = pl.program_id(1)
    lvl = lv - (NCS - 1) * tlx

    if NCS > 1:
        @pl.when(cs == 0)
        def _():
            xp_s[:, 0:HIST, :] = gc0_ref[...]

        @pl.when(cs > 0)
        def _():
            xp_s[:, 0:HIST, :] = prev_ref[...]
    else:
        xp_s[:, 0:HIST, :] = gc0_ref[...]
    xp_s[:, HIST:HIST + tlx, :] = x_ref[...]
    qkv = _silu(_conv_taps(xp_s, cw_ref, CONV_C, tlx)).reshape(NB, RB, 3 * D_MIX)

    @pl.when(cs == NCS - 1)
    def _():
        gcs_ref[...] = xp_s[:, pl.ds(HIST + lvl - (CONV_C - 1), CONV_C - 1), :]

    beta_all = _sigmoid(sa_ref[...].reshape(NB, RB, 128))
    g_all = -jnp.exp(alog_ref[...]) * _softplus(sb_ref[...].reshape(NB, RB, 128) + dtb_ref[...])
    if lv < NCS * tlx:
        assert NCS == 1
        valid = (lax.broadcasted_iota(jnp.int32, (NB, RB, 128), 1) % tl) < lv
        beta_all = jnp.where(valid, beta_all, 0.0)
        g_all = jnp.where(valid, g_all, 0.0)
    rr = lax.broadcasted_iota(jnp.int32, (RB, RB), 0)
    cc = lax.broadcasted_iota(jnp.int32, (RB, RB), 1)
    incl = rr >= cc
    if tl < RB:
        incl = incl & ((rr // tl) == (cc // tl))
    diag = rr == cc
    tril = jnp.broadcast_to(incl.astype(BF16)[None], (NB, RB, RB))
    hi, mid, lo = _split3(g_all)
    gam = _bmm(tril, hi) + _bmm(tril, mid) + _bmm(tril, lo)
    glast = gam.reshape(NB * G, tl, 128)[:, tl - 1:tl, :]
    glast_rows = jnp.broadcast_to(glast, (NB * G, tl, 128)).reshape(NB, RB, 128)
    egam = jnp.exp(gam)
    gcol = _gate_cols(gam, NB)
    bcol = _gate_cols(beta_all, NB)
    egcol = _gate_cols(egam, NB)
    kdcol = _gate_cols(jnp.exp(glast_rows - gam), NB)

    q = _heads(qkv[:, :, 0:D_MIX], NB, DH)
    k = _heads(qkv[:, :, D_MIX:2 * D_MIX], NB, DH)
    v = _heads(qkv[:, :, 2 * D_MIX:3 * D_MIX], NB, DH)
    q = q * lax.rsqrt(jnp.sum(q * q, axis=-1, keepdims=True) + NORM_EPS) * (DH ** -0.5)
    k = k * lax.rsqrt(jnp.sum(k * k, axis=-1, keepdims=True) + NORM_EPS)
    qb, kb = q.astype(BF16), k.astype(BF16)

    dmat = jnp.exp(jnp.where(incl[None], gcol - _gate_rows(gam, NB), NEG))
    nmat = jnp.where(diag[None], 0.0, -(bcol * _bmm_nt(kb, kb) * dmat))
    rhs = jnp.concatenate([bcol * v, (bcol * egcol) * k], axis=-1)
    sol = _bmm(_unit_lower_inverse(nmat, rr, cc, tl), rhs)
    qkm = (_bmm_nt(qb, kb) * dmat).astype(BF16).reshape(NB, N_HEADS, RB, RB)

    def put(ref, val):
        val4 = val.reshape(NB, N_HEADS, RB, DH)
        for h in range(N_HEADS):
            ref[:, :, h] = val4[:, h].reshape(NB, G, tl, DH)

    put(u_ref, sol[:, :, 0:DH])
    put(w_ref, sol[:, :, DH:2 * DH].astype(BF16))
    put(qg_ref, (q * egcol).astype(BF16))
    put(kd_ref, (k * kdcol).astype(BF16))
    eg = jnp.exp(glast).reshape(NB, G, 1, 128)
    for h in range(N_HEADS):
        eg_ref[:, :, h] = jnp.broadcast_to(eg[:, :, :, 4 + h:5 + h], (NB, G, 1, 128))
        for g in range(G):
            qkm_ref[:, g, h] = qkm[:, h, g * tl:(g + 1) * tl, g * tl:(g + 1) * tl]


def _gdn_prep(proj, gc0, cw, alog, dtb, NB, G, tl, lv):
    nb, lp, _ = proj.shape
    assert tl % INV_BLOCK == 0 and (tl // INV_BLOCK) & (tl // INV_BLOCK - 1) == 0 and G * tl == RB
    if G == 1:
        gx, tlx = 1, NB * RB
    else:
        assert lp == tl
        gx, tlx = NB * G, tl
    NI, NCS = nb // gx, lp // tlx
    nbt = nb * lp // RB
    step = lambda i, c: (i * NCS + c, 0, 0, 0, 0)
    par = lambda i, c: (0, 0)
    chain = lambda last, dt: jax.ShapeDtypeStruct((nbt, G, N_HEADS, tl, last), dt)
    cspec = lambda last: pl.BlockSpec((NB, G, N_HEADS, tl, last), step)
    return pl.pallas_call(
        functools.partial(_gdn_prep_body, NB=NB, G=G, tl=tl, tlx=tlx, NCS=NCS, lv=lv),
        grid=(NI, NCS),
        in_specs=[pl.BlockSpec((gx, tlx, 3 * D_MIX), lambda i, c: (i, c, OFF_QKVC // (3 * D_MIX))),
                  pl.BlockSpec((gx, HIST, 3 * D_MIX),
                               lambda i, c: (i, jnp.maximum(c * (tlx // HIST) - 1, 0), OFF_QKVC // (3 * D_MIX))),
                  pl.BlockSpec((gx, tlx, 128), lambda i, c: (i, c, OFF_SA // 128)),
                  pl.BlockSpec((gx, tlx, 128), lambda i, c: (i, c, OFF_SB // 128)),
                  pl.BlockSpec((gx, HIST, 3 * D_MIX), lambda i, c: (i, 0, 0)),
                  pl.BlockSpec((8, 3 * D_MIX), par),
                  pl.BlockSpec((1, 128), par),
                  pl.BlockSpec((1, 128), par)],
        out_specs=[cspec(DH), cspec(DH), cspec(DH), cspec(DH), cspec(tl),
                   pl.BlockSpec((NB, G, N_HEADS, 1, 128), step),
                   pl.BlockSpec((gx, CONV_C - 1, 3 * D_MIX), lambda i, c: (i, 0, 0))],
        out_shape=[chain(DH, F32), chain(DH, BF16), chain(DH, BF16), chain(DH, BF16), chain(tl, BF16),
                   jax.ShapeDtypeStruct((nbt, G, N_HEADS, 1, 128), F32),
                   jax.ShapeDtypeStruct((nb, CONV_C - 1, 3 * D_MIX), F32)],
        scratch_shapes=[pltpu.VMEM((gx, HIST + tlx, 3 * D_MIX), F32)],
        compiler_params=pltpu.CompilerParams(dimension_semantics=("arbitrary", "arbitrary"),
                                             vmem_limit_bytes=VMEM_LIMIT),
        name=f"gdn_prep_g{G}",
    )(proj, proj, proj, proj, gc0, cw, alog, dtb)


def _gdn_scan_body(u_ref, w_ref, qg_ref, kd_ref, qkm_ref, eg_ref, z_ref, S0_ref, gng_ref, yc_ref, S_ref,
                   *, NS, tl):
    c = pl.program_id(1)
    n = NS * N_HEADS

    @pl.when(c == 0)
    def _():
        S_ref[...] = S0_ref[...]

    S = S_ref[...].reshape(n, DH, DH)
    Sb = S.astype(BF16)
    chains = lambda ref: ref[...].reshape(n, tl, ref.shape[-1])
    v_new = chains(u_ref) - jnp.einsum('ntd,nde->nte', chains(w_ref), Sb, preferred_element_type=F32)
    vnb = v_new.astype(BF16)
    o = (jnp.einsum('ntd,nde->nte', chains(qg_ref), Sb, preferred_element_type=F32)
         + jnp.einsum('nts,nse->nte', chains(qkm_ref), vnb, preferred_element_type=F32))
    eg = eg_ref[...].reshape(n, 1, 128)[:, :, 0:1]
    S_new = eg * S + jnp.einsum('ntd,nte->nde', chains(kd_ref), vnb, preferred_element_type=F32)
    S_ref[...] = S_new.reshape(NS, N_HEADS, DH, DH)

    on = (o * lax.rsqrt(jnp.mean(o * o, axis=-1, keepdims=True) + NORM_EPS) * gng_ref[...]).reshape(NS, N_HEADS, tl, DH)
    for h in range(N_HEADS):
        yc_ref[:, :, h * DH:(h + 1) * DH] = (on[:, h] * _silu(z_ref[:, :, h * DH:(h + 1) * DH])).astype(BF16)


def _gdn_scan(pre, proj, S0, gng, SB, G, tl):
    nb, lp, _ = proj.shape
    NS = SB * G
    NI, NC = nb // NS, lp // tl
    six = lambda a: a.reshape((NI * SB, NC) + a.shape[1:])
    cspec = lambda last: pl.BlockSpec((SB, 1, G, N_HEADS, tl, last), lambda i, c: (i, c, 0, 0, 0, 0))
    u, w, qg, kd, qkm, eg = (six(a) for a in pre)
    return pl.pallas_call(
        functools.partial(_gdn_scan_body, NS=NS, tl=tl),
        grid=(NI, NC),
        in_specs=[cspec(DH), cspec(DH), cspec(DH), cspec(DH), cspec(tl),
                  pl.BlockSpec((SB, 1, G, N_HEADS, 1, 128), lambda i, c: (i, c, 0, 0, 0, 0)),
                  pl.BlockSpec((NS, tl, D_MIX), lambda i, c: (i, c, OFF_Z // D_MIX)),
                  pl.BlockSpec((NS, N_HEADS, DH, DH), lambda i, c: (i, 0, 0, 0)),
                  pl.BlockSpec((1, DH), lambda i, c: (0, 0))],
        out_specs=[pl.BlockSpec((NS, tl, D_MIX), lambda i, c: (i, c, 0)),
                   pl.BlockSpec((NS, N_HEADS, DH, DH), lambda i, c: (i, 0, 0, 0))],
        out_shape=[jax.ShapeDtypeStruct((nb, lp, D_MIX), BF16),
                   jax.ShapeDtypeStruct((nb, N_HEADS, DH, DH), F32)],
        compiler_params=pltpu.CompilerParams(dimension_semantics=("arbitrary", "arbitrary"),
                                             vmem_limit_bytes=VMEM_LIMIT),
        name=f"gdn_scan_g{G}",
    )(u, w, qg, kd, qkm, eg, proj, S0, gng)


def _route(rl):
    lane = lax.broadcasted_iota(jnp.int32, rl.shape, 1).astype(F32)
    is_g = (lane >= N_EXPERTS) & (lane < N_EXPERTS + N_GROUPS)
    gl = jnp.where(is_g, rl, NEG)
    gmax = jnp.max(gl, axis=-1, keepdims=True)
    grp = jnp.min(jnp.where(gl == gmax, lane - N_EXPERTS, 4.0 * N_EXPERTS), axis=-1, keepdims=True)
    p_grp = 1.0 / jnp.sum(jnp.where(is_g, jnp.exp(gl - gmax), 0.0), axis=-1, keepdims=True)
    lo = grp * EXPERTS_PER_GROUP
    in_grp = (lane >= lo) & (lane < lo + EXPERTS_PER_GROUP)
    el = jnp.where(in_grp, rl, NEG)
    m1 = jnp.max(el, axis=-1, keepdims=True)
    i1 = jnp.min(jnp.where(el == m1, lane, 4.0 * N_EXPERTS), axis=-1, keepdims=True)
    el2 = jnp.where(lane == i1, NEG, el)
    m2 = jnp.max(el2, axis=-1, keepdims=True)
    i2 = jnp.min(jnp.where(el2 == m2, lane, 4.0 * N_EXPERTS), axis=-1, keepdims=True)
    e2 = jnp.exp(m2 - m1)
    w1 = p_grp / (1.0 + e2)
    w2 = p_grp * e2 / (1.0 + e2)
    return jnp.where(lane == i1, w1, 0.0) + jnp.where(lane == i2, w2, 0.0)


def _merge_body(ya_ref, yb_ref, yc_ref, ga_ref, gb_ref, gc_ref, x_ref, mod_ref, wa_ref, wb_ref, wc_ref, wo_ref,
                lg_ref, lb_ref, wrh_ref, wrl_ref, br_ref, x1_ref, gates_ref, *, bt, lt):
    tm = bt * lt

    def r2(ref):
        return ref[...].reshape(tm, ref.shape[-1])

    merged = (_sigmoid(r2(ga_ref)) * _dot(r2(ya_ref), wa_ref[0])
              + _sigmoid(r2(gb_ref)) * _dot(r2(yb_ref), wb_ref[0])
              + _sigmoid(r2(gc_ref)) * _dot(r2(yc_ref), wc_ref[0]))
    out = _dot(merged.astype(BF16), wo_ref[0])
    y = DN_ALPHA * x_ref[...] + (1.0 + mod_ref[:, 2:3, :]) * out.reshape(bt, lt, D_MODEL)
    x1 = _layer_norm(y, lg_ref[0], lb_ref[0])
    x1_ref[...] = x1
    u2 = (x1 * (1.0 + mod_ref[:, 4:5, :]) + mod_ref[:, 3:4, :]).reshape(tm, D_MODEL)
    hi = u2.astype(BF16)
    lo = (u2 - hi.astype(F32)).astype(BF16)
    rl = _dot(hi, wrh_ref[0]) + _dot(lo, wrh_ref[0]) + _dot(hi, wrl_ref[0]) + br_ref[0]
    gates_ref[...] = _route(rl).reshape(bt, lt, 128)


def _merge(ya, yb, yc, proj, x, mod, wa, wb, wc, wo, lg, lb, wrh, wrl, br, layer, bt, lt):
    nb, lp, _ = x.shape
    tok = lambda i, t: (i, t, 0)
    wsp = lambda shape: pl.BlockSpec((1,) + shape, lambda i, t: (layer, 0, 0))
    g0 = OFF_GTS // D_MODEL
    return pl.pallas_call(
        functools.partial(_merge_body, bt=bt, lt=lt),
        grid=(nb // bt, lp // lt),
        in_specs=[pl.BlockSpec((bt, lt, D_MIX), tok),
                  pl.BlockSpec((bt, lt, D_MIX), tok),
                  pl.BlockSpec((bt, lt, D_MIX), tok),
                  pl.BlockSpec((bt, lt, D_MODEL), lambda i, t: (i, t, g0)),
                  pl.BlockSpec((bt, lt, D_MODEL), lambda i, t: (i, t, g0 + 1)),
                  pl.BlockSpec((bt, lt, D_MODEL), lambda i, t: (i, t, g0 + 2)),
                  pl.BlockSpec((bt, lt, D_MODEL), tok),
                  pl.BlockSpec((bt, 6, D_MODEL), lambda i, t: (i, 0, 0)),
                  wsp((D_MIX, D_MODEL)), wsp((D_MIX, D_MODEL)), wsp((D_MIX, D_MODEL)), wsp((D_MODEL, D_MODEL)),
                  wsp((1, D_MODEL)), wsp((1, D_MODEL)),
                  wsp((D_MODEL, 128)), wsp((D_MODEL, 128)), wsp((1, 128))],
        out_specs=[pl.BlockSpec((bt, lt, D_MODEL), tok),
                   pl.BlockSpec((bt, lt, 128), tok)],
        out_shape=[jax.ShapeDtypeStruct((nb, lp, D_MODEL), F32),
                   jax.ShapeDtypeStruct((nb, lp, 128), F32)],
        compiler_params=pltpu.CompilerParams(dimension_semantics=("arbitrary", "arbitrary"),
                                             vmem_limit_bytes=VMEM_LIMIT),
        name=f"merge_l{layer}_b{bt}",
    )(ya, yb, yc, proj, proj, proj, x, mod, wa, wb, wc, wo, lg, lb, wrh, wrl, br)


def _moe_body(x1_ref, mod_ref, gates_ref, wg_ref, wu_ref, wd_ref, lg_ref, lb_ref, x2_ref, u_scr, acc_scr, *, bt, lt):
    tm = bt * lt
    e = pl.program_id(2)

    @pl.when(e == 0)
    def _():
        u2 = x1_ref[...] * (1.0 + mod_ref[:, 4:5, :]) + mod_ref[:, 3:4, :]
        u_scr[...] = u2.reshape(tm, D_MODEL).astype(BF16)
        acc_scr[...] = jnp.zeros_like(acc_scr)

    u = u_scr[...]
    hb = _silu(_dot(u, wg_ref[0, 0].astype(BF16))) * _dot(u, wu_ref[0, 0].astype(BF16))
    out = _dot(hb.astype(BF16), wd_ref[0, 0].astype(BF16))
    gates = gates_ref[...].reshape(tm, 128)
    lane = lax.broadcasted_iota(jnp.int32, (tm, 128), 1)
    gcol = jnp.sum(jnp.where(lane == e, gates, 0.0), axis=-1, keepdims=True)
    acc_scr[...] += jnp.where(gcol != 0.0, out * gcol, 0.0)

    @pl.when(e == N_EXPERTS - 1)
    def _():
        y = DN_ALPHA * x1_ref[...] + (1.0 + mod_ref[:, 5:6, :]) * acc_scr[...].reshape(bt, lt, D_MODEL)
        x2_ref[...] = _layer_norm(y, lg_ref[0], lb_ref[0])


def _moe(x1, mod, gates, wg, wu, wd, lg, lb, layer, bt, lt):
    nb, lp, _ = x1.shape
    tok = lambda i, t, e: (i, t, 0)
    return pl.pallas_call(
        functools.partial(_moe_body, bt=bt, lt=lt),
        grid=(nb // bt, lp // lt, N_EXPERTS),
        in_specs=[pl.BlockSpec((bt, lt, D_MODEL), tok),
                  pl.BlockSpec((bt, 6, D_MODEL), lambda i, t, e: (i, 0, 0)),
                  pl.BlockSpec((bt, lt, 128), tok),
                  pl.BlockSpec((1, 1, D_MODEL, D_EXPERT), lambda i, t, e: (layer, e, 0, 0)),
                  pl.BlockSpec((1, 1, D_MODEL, D_EXPERT), lambda i, t, e: (layer, e, 0, 0)),
                  pl.BlockSpec((1, 1, D_EXPERT, D_MODEL), lambda i, t, e: (layer, e, 0, 0)),
                  pl.BlockSpec((1, 1, D_MODEL), lambda i, t, e: (layer, 0, 0)),
                  pl.BlockSpec((1, 1, D_MODEL), lambda i, t, e: (layer, 0, 0))],
        out_specs=pl.BlockSpec((bt, lt, D_MODEL), tok),
        out_shape=jax.ShapeDtypeStruct((nb, lp, D_MODEL), F32),
        scratch_shapes=[pltpu.VMEM((bt * lt, D_MODEL), BF16), pltpu.VMEM((bt * lt, D_MODEL), F32)],
        compiler_params=pltpu.CompilerParams(dimension_semantics=("arbitrary", "arbitrary", "arbitrary"),
                                             vmem_limit_bytes=VMEM_LIMIT),
        name=f"moe_l{layer}_b{bt}",
    )(x1, mod, gates, wg, wu, wd, lg, lb)


def _hist(state):
    return jnp.pad(state, ((0, 0), (HIST - state.shape[1], 0), (0, 0)))


def _trunk(x, mod, st, p, bt, lt, G, tl, lv, gdn_cfg):
    nb = x.shape[0]
    new = {key: [] for key in ('C', 'n', 'm', 'conv', 'S', 'gconv')}
    for l in range(DEPTH):
        proj, x = _inproj(x, mod[l], p['ln_in_g'], p['ln_in_b'], p['w_in_r'], p['b_in_r'], l, l == 0, bt, lt)
        m0 = jnp.pad(st['m'][l], ((0, 0), (0, 128 - N_HEADS))).reshape(nb // G, G, 128)
        ya, yb, C, n, m, conv = _mlstm(proj, st['C'][l], st['n'][l], m0, _hist(st['conv'][l]),
                                       p['mlstm_norm_g'][l:l + 1], p['conv_b_w8'][l], G, tl, lv)
        gd = gdn_cfg
        *pre, gconv = _gdn_prep(proj, _hist(st['gconv'][l]), p['conv_c_w8'][l], p['alog_row'][l], p['dtb_row'][l],
                                gd['NB'], gd['G'], gd['tl'], lv)
        yc, S = _gdn_scan(pre, proj, st['S'][l], p['gdn_norm_g'][l:l + 1], gd['SB'], gd['G'], gd['tl'])
        x1, gates = _merge(ya, yb, yc, proj, x, mod[l], p['w_br_a'], p['w_br_b'], p['w_br_c'], p['w_out'],
                           p['ln1_g'], p['ln1_b'], p['wr_hi'], p['wr_lo'], p['br'], l, bt, lt)
        x = _moe(x1, mod[l], gates, p['exp_w_gate'], p['exp_w_up'], p['exp_w_down'], p['ln2_g'], p['ln2_b'],
                 l, bt, lt)
        new['C'].append(C)
        new['n'].append(n)
        new['m'].append(m.reshape(nb, 128)[:, :N_HEADS])
        new['conv'].append(conv)
        new['S'].append(S)
        new['gconv'].append(gconv)
    return x, {key: jnp.stack(val) for key, val in new.items()}


def kernel(x_prompt, x_sample, state_mlstm_C, state_mlstm_n, state_mlstm_m, state_conv, state_gdn_S, state_gdn_conv, c_prompt, c_sample, ln_in_g, ln_in_b, w_ada, b_ada, w_in, b_in, mlstm_norm_g, conv_b_w, conv_c_w, gdn_a_log, gdn_dt_bias, gdn_norm_g, w_br_a, w_br_b, w_br_c, w_out, ln1_g, ln1_b, router_g_w, router_g_b, router_e_w, router_e_b, exp_w_gate, exp_w_up, exp_w_down, ln2_g, ln2_b):
    nbp, lp, _ = x_prompt.shape
    nbs, ls, _ = x_sample.shape
    lsp = 8

    def regroup(a):
        parts = [a[..., 3592:5128], a[..., 2056:3592], a[..., 5648:8720], a[..., 0:2048], a[..., 5128:5640],
                 a[..., 2048:2052], a[..., 5640:5644], jnp.zeros(a.shape[:-1] + (120,), a.dtype),
                 a[..., 2052:2056], a[..., 5644:5648], jnp.zeros(a.shape[:-1] + (120,), a.dtype)]
        return jnp.concatenate(parts, axis=-1)

    wr = jnp.concatenate([router_e_w, router_g_w, jnp.zeros((DEPTH, D_MODEL, 128 - N_EXPERTS - N_GROUPS), F32)], axis=-1)
    wr_hi = wr.astype(BF16)
    lane_pad = lambda a: jnp.pad(a, ((0, 0), (4, 128 - 4 - N_HEADS)))[:, None, :]
    p = dict(
        ln_in_g=ln_in_g.reshape(1, D_MODEL), ln_in_b=ln_in_b.reshape(1, D_MODEL),
        w_in_r=regroup(w_in).astype(BF16), b_in_r=regroup(b_in).reshape(DEPTH, 1, N_PROJ),
        mlstm_norm_g=mlstm_norm_g,
        conv_b_w8=jnp.pad(conv_b_w, ((0, 0), (0, 8 - CONV_B), (0, 0))),
        conv_c_w8=jnp.pad(conv_c_w, ((0, 0), (0, 8 - CONV_C), (0, 0))),
        alog_row=lane_pad(gdn_a_log), dtb_row=lane_pad(gdn_dt_bias), gdn_norm_g=gdn_norm_g,
        w_br_a=w_br_a.astype(BF16), w_br_b=w_br_b.astype(BF16), w_br_c=w_br_c.astype(BF16),
        w_out=w_out.astype(BF16),
        ln1_g=ln1_g.reshape(DEPTH, 1, D_MODEL), ln1_b=ln1_b.reshape(DEPTH, 1, D_MODEL),
        wr_hi=wr_hi, wr_lo=(wr - wr_hi.astype(F32)).astype(BF16),
        br=jnp.concatenate([router_e_b, router_g_b, jnp.zeros((DEPTH, 128 - N_EXPERTS - N_GROUPS), F32)],
                           axis=-1).reshape(DEPTH, 1, 128),
        exp_w_gate=exp_w_gate, exp_w_up=exp_w_up, exp_w_down=exp_w_down,
        ln2_g=ln2_g.reshape(DEPTH, 1, D_MODEL), ln2_b=ln2_b.reshape(DEPTH, 1, D_MODEL),
    )

    mod = _ada(jnp.concatenate([c_prompt, c_sample], axis=0), w_ada, b_ada)
    mod = mod.reshape(DEPTH, nbp + nbs, 6, D_MODEL)

    zeros = lambda *s: jnp.zeros((DEPTH, nbp) + s, F32)
    st_p = {'C': zeros(N_HEADS, DH, DH), 'n': zeros(N_HEADS, DH), 'm': zeros(N_HEADS),
            'conv': zeros(CONV_B - 1, D_MIX), 'S': zeros(N_HEADS, DH, DH), 'gconv': zeros(CONV_C - 1, 3 * D_MIX)}
    y_p, sp = _trunk(x_prompt, mod[:, :nbp], st_p, p, bt=1, lt=512, G=1, tl=64, lv=lp,
                     gdn_cfg=dict(NB=2, G=1, tl=RB, SB=nbp))

    st_s = {'C': state_mlstm_C, 'n': state_mlstm_n, 'm': state_mlstm_m, 'conv': state_conv,
            'S': state_gdn_S, 'gconv': state_gdn_conv}
    xs = jnp.pad(x_sample, ((0, 0), (0, lsp - ls), (0, 0)))
    y_s, ss = _trunk(xs, mod[:, nbp:], st_s, p, bt=64, lt=lsp, G=16, tl=lsp, lv=ls,
                     gdn_cfg=dict(NB=2, G=RB // lsp, tl=lsp, SB=1))
    y_s = y_s[:, :ls]

    return (y_p, y_s, sp['C'], sp['n'], sp['m'], sp['conv'], sp['S'], sp['gconv'],
            ss['C'], ss['n'], ss['m'], ss['conv'], ss['S'], ss['gconv'])
```

```python
import functools

import jax
import jax.numpy as jnp
from jax import lax
from jax.experimental import pallas as pl
from jax.experimental.pallas import tpu as pltpu

F32 = jnp.float32
BF16 = jnp.bfloat16

D_MODEL = 1024
DEPTH = 2
N_HEADS = 4
DH = 128
D_MIX = N_HEADS * DH
N_EXPERTS = 32
EXPERTS_PER_GROUP = 8
N_GROUPS = 4
D_EXPERT = 256
MOE_BLOCK = 128
CONV_B = 3
CONV_C = 4
HIST = 8
RB = 64
INV_BLOCK = 8
DN_ALPHA = (2 * DEPTH) ** 0.25
LN_EPS = 1e-5
NORM_EPS = 1e-6
NEG = -1e30

OFF_QKVC = 0
OFF_BCH = 1536
OFF_GTS = 3072
OFF_QKVO = 6144
OFF_Z = 8192
OFF_SA = 8704
OFF_SB = 8832
N_PROJ = 8960
TN_PROJ = 1280

VMEM_LIMIT = 52 * 1024 * 1024


def _dot(a, b):
    return jnp.dot(a, b, preferred_element_type=F32)


def _dot_nt(a, b):
    return lax.dot_general(a, b, (((1,), (1,)), ((), ())), preferred_element_type=F32)


def _split3(x):
    hi = x.astype(BF16)
    r = x - hi.astype(F32)
    mid = r.astype(BF16)
    lo = (r - mid.astype(F32)).astype(BF16)
    return hi, mid, lo


def _dot3(m, x):
    hi, mid, lo = _split3(x)
    return _dot(m, hi) + _dot(m, mid) + _dot(m, lo)


def _dotb(a, b):
    return _dot(a.astype(BF16), b.astype(BF16))


def _layer_norm(x, g, b):
    mu = jnp.mean(x, axis=-1, keepdims=True)
    xc = x - mu
    var = jnp.mean(xc * xc, axis=-1, keepdims=True)
    return xc * lax.rsqrt(var + LN_EPS) * g + b


def _sigmoid(x):
    return jax.nn.sigmoid(x)


def _silu(x):
    return x * jax.nn.sigmoid(x)


def _log_sigmoid(x):
    return jnp.minimum(x, 0.0) - jnp.log1p(jnp.exp(-jnp.abs(x)))


def _softplus(x):
    return jnp.maximum(x, 0.0) + jnp.log1p(jnp.exp(-jnp.abs(x)))


def _seq_rows(x3, tl):
    g, _, c = x3.shape
    return jnp.broadcast_to(x3, (g, tl, c)).reshape(g * tl, c)


def _ada_body(c_ref, w_ref, b_ref, o_ref):
    c = c_ref[...]
    s = _silu(c).astype(BF16)
    o_ref[0] = _dot(s, w_ref[0].astype(BF16)) + b_ref[0]


def _ada(c_all, w_ada, b_ada):
    nb = c_all.shape[0]
    return pl.pallas_call(
        _ada_body,
        grid=(DEPTH, 6),
        in_specs=[pl.BlockSpec((nb, D_MODEL), lambda l, j: (0, 0)),
                  pl.BlockSpec((1, D_MODEL, D_MODEL), lambda l, j: (l, 0, j)),
                  pl.BlockSpec((1, 1, D_MODEL), lambda l, j: (l, 0, j))],
        out_specs=pl.BlockSpec((1, nb, D_MODEL), lambda l, j: (l, 0, j)),
        out_shape=jax.ShapeDtypeStruct((DEPTH, nb, 6 * D_MODEL), F32),
        compiler_params=pltpu.CompilerParams(dimension_semantics=("arbitrary", "arbitrary"),
                                             vmem_limit_bytes=VMEM_LIMIT),
        name="ada",
    )(c_all, w_ada, b_ada.reshape(DEPTH, 1, 6 * D_MODEL))


def _inproj_body(x_ref, mod_ref, g_ref, b_ref, w_ref, bias_ref, *rest, apply_ln, bt, lt):
    if apply_ln:
        proj_ref, xn_ref, u_scr = rest
    else:
        proj_ref, u_scr = rest
    j = pl.program_id(2)

    @pl.when(j == 0)
    def _():
        x = x_ref[...]
        if apply_ln:
            x = _layer_norm(x, g_ref[...], b_ref[...])
            xn_ref[...] = x
        u = x * (1.0 + mod_ref[:, 1:2, :]) + mod_ref[:, 0:1, :]
        u_scr[...] = u.reshape(bt * lt, D_MODEL).astype(BF16)

    acc = _dot(u_scr[...], w_ref[0]) + bias_ref[0]
    proj_ref[...] = acc.reshape(bt, lt, TN_PROJ)


def _inproj(x, mod, ln_g, ln_b, w_r, b_r, layer, apply_ln, bt, lt):
    nb, lp, _ = x.shape
    grid = (nb // bt, lp // lt, N_PROJ // TN_PROJ)
    tok = lambda i, t, j: (i, t, 0)
    out_shape = [jax.ShapeDtypeStruct((nb, lp, N_PROJ), F32)]
    out_specs = [pl.BlockSpec((bt, lt, TN_PROJ), lambda i, t, j: (i, t, j))]
    if apply_ln:
        out_shape.append(jax.ShapeDtypeStruct((nb, lp, D_MODEL), F32))
        out_specs.append(pl.BlockSpec((bt, lt, D_MODEL), tok))
    res = pl.pallas_call(
        functools.partial(_inproj_body, apply_ln=apply_ln, bt=bt, lt=lt),
        grid=grid,
        in_specs=[pl.BlockSpec((bt, lt, D_MODEL), tok),
                  pl.BlockSpec((bt, 6, D_MODEL), lambda i, t, j: (i, 0, 0)),
                  pl.BlockSpec((1, D_MODEL), lambda i, t, j: (0, 0)),
                  pl.BlockSpec((1, D_MODEL), lambda i, t, j: (0, 0)),
                  pl.BlockSpec((1, D_MODEL, TN_PROJ), lambda i, t, j: (layer, 0, j)),
                  pl.BlockSpec((1, 1, TN_PROJ), lambda i, t, j: (layer, 0, j))],
        out_specs=out_specs,
        out_shape=out_shape,
        scratch_shapes=[pltpu.VMEM((bt * lt, D_MODEL), BF16)],
        compiler_params=pltpu.CompilerParams(dimension_semantics=("arbitrary", "arbitrary", "arbitrary"),
                                             vmem_limit_bytes=VMEM_LIMIT),
        name=f"inproj_l{layer}_b{bt}",
    )(x, mod, ln_g, ln_b, w_r, b_r)
    return (res[0], res[1]) if apply_ln else (res[0], x)


def _chunk_masks(G, tl):
    R = G * tl
    rr = lax.broadcasted_iota(jnp.int32, (R, R), 0)
    cc = lax.broadcasted_iota(jnp.int32, (R, R), 1)
    incl = rr >= cc
    if G > 1:
        incl = incl & ((rr // tl) == (cc // tl))
    diag = rr == cc
    return incl, diag


def _valid_rows(G, tl, c, lv):
    R = G * tl
    row = lax.broadcasted_iota(jnp.int32, (R, 128), 0)
    pos = (row % tl if G > 1 else row) + c * tl
    return pos < lv


def _conv_taps(xp_s, w_ref, width, tl):
    acc = None
    for j in range(width):
        tap = xp_s[:, pl.ds(HIST - (width - 1) + j, tl), :] * w_ref[j:j + 1, :].reshape(1, 1, -1)
        acc = tap if acc is None else acc + tap
    return acc


def _mlstm_body(qkvo_ref, bch_ref, sa_ref, sb_ref, C0_ref, n0_ref, m0_ref, cv0_ref, ng_ref, cw_ref,
                ya_ref, yb_ref, C_ref, n_ref, m_ref, cv_ref, xp_s, *, G, tl, NC, lv):
    c = pl.program_id(1)
    R = G * tl
    padded = lv < NC * tl
    lvl = lv - (NC - 1) * tl

    @pl.when(c == 0)
    def _():
        C_ref[...] = C0_ref[...]
        n_ref[...] = n0_ref[...]
        m_ref[...] = m0_ref[...]
        xp_s[:, 0:HIST, :] = cv0_ref[...]

    xin = bch_ref[:, :, D_MIX:2 * D_MIX] * bch_ref[:, :, 2 * D_MIX:3 * D_MIX]
    xp_s[:, HIST:HIST + tl, :] = xin
    conv = _conv_taps(xp_s, cw_ref, CONV_B, tl)
    yb_ref[...] = (bch_ref[:, :, 0:D_MIX] * conv).astype(BF16)

    @pl.when(c == NC - 1)
    def _():
        cv_ref[...] = xp_s[:, pl.ds(HIST + lvl - (CONV_B - 1), CONV_B - 1), :]

    if NC > 1:
        xp_s[:, 0:HIST, :] = xp_s[:, tl:tl + HIST, :]

    i_all = sa_ref[...].reshape(R, 128)
    f_all = _log_sigmoid(sb_ref[...].reshape(R, 128))
    if padded:
        valid = _valid_rows(G, tl, c, lv)
        i_all = jnp.where(valid, i_all, NEG)
        f_all = jnp.where(valid, f_all, 0.0)
    incl, _ = _chunk_masks(G, tl)
    bcum = _dot3(incl.astype(BF16), f_all)
    b3 = bcum.reshape(G, tl, 128)
    blast3 = b3[:, tl - 1:tl, :]
    blast = _seq_rows(blast3, tl)
    m03 = m_ref[0].reshape(G, 1, 128)
    m0 = _seq_rows(m03, tl)
    val = blast - bcum + i_all
    mnew3 = jnp.maximum(jnp.max(val.reshape(G, tl, 128), axis=1, keepdims=True), blast3 + m03)
    mnew = _seq_rows(mnew3, tl)
    wk_all = jnp.exp(val - mnew)
    dec3 = jnp.exp(blast3 + m03 - mnew3)
    minter_all = bcum + m0
    iT = i_all.T
    bT = bcum.T

    for h in range(N_HEADS):
        hs = slice(h * DH, (h + 1) * DH)
        q = qkvo_ref[:, :, h * DH:(h + 1) * DH].reshape(R, DH)
        k = qkvo_ref[:, :, D_MIX + h * DH:D_MIX + (h + 1) * DH].reshape(R, DH) * (DH ** -0.5)
        v = qkvo_ref[:, :, 2 * D_MIX + h * DH:2 * D_MIX + (h + 1) * DH].reshape(R, DH)
        o = qkvo_ref[:, :, 3 * D_MIX + h * DH:3 * D_MIX + (h + 1) * DH].reshape(R, DH)
        qb, kb, vb = q.astype(BF16), k.astype(BF16), v.astype(BF16)

        b_col = bcum[:, h:h + 1]
        dlog = jnp.where(incl, b_col - bT[h:h + 1, :] + iT[h:h + 1, :], NEG)
        m_inter = minter_all[:, h:h + 1]
        m_t = jnp.maximum(m_inter, jnp.max(dlog, axis=-1, keepdims=True))
        s = _dot_nt(qb, kb) * jnp.exp(dlog - m_t)
        inter = jnp.exp(m_inter - m_t)

        Ch = C_ref[:, h]
        nh = n_ref[:, h:h + 1, :]
        q3 = q.reshape(G, tl, DH)
        qC = jnp.einsum('gtd,gde->gte', q3.astype(BF16), Ch.astype(BF16),
                        preferred_element_type=F32).reshape(R, DH)
        qn = jnp.sum(q3 * nh, axis=-1, keepdims=True).reshape(R, 1)
        num = _dot(s.astype(BF16), vb) + inter * qC
        den = jnp.sum(s, axis=-1, keepdims=True) + inter * qn
        hh = num / jnp.maximum(jnp.abs(den), jnp.exp(-m_t))

        mu = jnp.mean(hh, axis=-1, keepdims=True)
        hc = hh - mu
        hn = hc * lax.rsqrt(jnp.mean(hc * hc, axis=-1, keepdims=True) + LN_EPS)
        ya = _sigmoid(o) * hn * ng_ref[:, hs]
        ya_ref[:, :, h * DH:(h + 1) * DH] = ya.reshape(G, tl, DH).astype(BF16)

        kw3 = (k * wk_all[:, h:h + 1]).reshape(G, tl, DH)
        dec = dec3[:, :, h:h + 1]
        dC = jnp.einsum('gtd,gte->gde', kw3.astype(BF16), vb.reshape(G, tl, DH),
                        preferred_element_type=F32)
        C_ref[:, h] = dec * Ch + dC
        n_ref[:, h:h + 1, :] = dec * nh + jnp.sum(kw3, axis=1, keepdims=True)

    m_ref[0] = mnew3.reshape(G, 128)


def _mlstm(proj, C0, n0, m0, cv0, ng, cw, G, tl, lv):
    nb, lp, _ = proj.shape
    NC = lp // tl
    seq4 = lambda i, c: (i, 0, 0, 0)
    seq3 = lambda i, c: (i, 0, 0)
    par = lambda i, c: (0, 0)
    return pl.pallas_call(
        functools.partial(_mlstm_body, G=G, tl=tl, NC=NC, lv=lv),
        grid=(nb // G, NC),
        in_specs=[pl.BlockSpec((G, tl, 4 * D_MIX), lambda i, c: (i, c, OFF_QKVO // (4 * D_MIX))),
                  pl.BlockSpec((G, tl, 3 * D_MIX), lambda i, c: (i, c, OFF_BCH // (3 * D_MIX))),
                  pl.BlockSpec((G, tl, 128), lambda i, c: (i, c, OFF_SA // 128)),
                  pl.BlockSpec((G, tl, 128), lambda i, c: (i, c, OFF_SB // 128)),
                  pl.BlockSpec((G, N_HEADS, DH, DH), seq4),
                  pl.BlockSpec((G, N_HEADS, DH), seq3),
                  pl.BlockSpec((1, G, 128), seq3),
                  pl.BlockSpec((G, HIST, D_MIX), seq3),
                  pl.BlockSpec((1, D_MIX), par),
                  pl.BlockSpec((8, D_MIX), par)],
        out_specs=[pl.BlockSpec((G, tl, D_MIX), lambda i, c: (i, c, 0)),
                   pl.BlockSpec((G, tl, D_MIX), lambda i, c: (i, c, 0)),
                   pl.BlockSpec((G, N_HEADS, DH, DH), seq4),
                   pl.BlockSpec((G, N_HEADS, DH), seq3),
                   pl.BlockSpec((1, G, 128), seq3),
                   pl.BlockSpec((G, CONV_B - 1, D_MIX), seq3)],
        out_shape=[jax.ShapeDtypeStruct((nb, lp, D_MIX), BF16),
                   jax.ShapeDtypeStruct((nb, lp, D_MIX), BF16),
                   jax.ShapeDtypeStruct((nb, N_HEADS, DH, DH), F32),
                   jax.ShapeDtypeStruct((nb, N_HEADS, DH), F32),
                   jax.ShapeDtypeStruct((nb // G, G, 128), F32),
                   jax.ShapeDtypeStruct((nb, CONV_B - 1, D_MIX), F32)],
        scratch_shapes=[pltpu.VMEM((G, HIST + tl, D_MIX), F32)],
        compiler_params=pltpu.CompilerParams(dimension_semantics=("arbitrary", "arbitrary"),
                                             vmem_limit_bytes=VMEM_LIMIT),
        name=f"mlstm_g{G}",
    )(proj, proj, proj, proj, C0, n0, m0, cv0, ng, cw)


def _heads(x, nb, width):
    return jnp.stack([x[:, :, h * width:(h + 1) * width] for h in range(N_HEADS)],
                     axis=1).reshape(nb * N_HEADS, RB, width)


def _gate_cols(x, nb):
    return jnp.stack([x[:, :, 4 + h:5 + h] for h in range(N_HEADS)], axis=1).reshape(nb * N_HEADS, RB, 1)


def _gate_rows(x, nb):
    xt = jnp.swapaxes(x, 1, 2)
    return jnp.stack([xt[:, 4 + h:5 + h, :] for h in range(N_HEADS)], axis=1).reshape(nb * N_HEADS, 1, RB)


def _bmm(a, b):
    return jnp.einsum('nts,nsu->ntu', a.astype(BF16), b.astype(BF16), preferred_element_type=F32)


def _bmm_nt(a, b):
    return jnp.einsum('ntd,nsd->nts', a, b, preferred_element_type=F32)


def _unit_lower_inverse(n, rr, cc, tl):
    eye = (rr == cc).astype(F32)[None]
    p = jnp.where(((rr // INV_BLOCK) == (cc // INV_BLOCK))[None], n, 0.0)
    x = eye + p
    b = 2
    while b < INV_BLOCK:
        p = _bmm(p, p)
        x = x + _bmm(x, p)
        b *= 2
    b = INV_BLOCK
    while b < tl:
        off = jnp.where((((rr // (2 * b)) == (cc // (2 * b))) & ((rr // b) != (cc // b)))[None], n, 0.0)
        x = x + _bmm(x, _bmm(off, x))
        b *= 2
    return x


def _gdn_prep_body(x_ref, prev_ref, sa_ref, sb_ref, gc0_ref, cw_ref, alog_ref, dtb_ref,
                   u_ref, w_ref, qg_ref, kd_ref, qkm_ref, eg_ref, gcs_ref, xp_s, *, NB, G, tl, tlx, NCS, lv):
    cs = pl.program_id(1)
    lvl = lv - (NCS - 1) * tlx

    if NCS > 1:
        @pl.when(cs == 0)
        def _():
            xp_s[:, 0:HIST, :] = gc0_ref[...]

        @pl.when(cs > 0)
        def _():
            xp_s[:, 0:HIST, :] = prev_ref[...]
    else:
        xp_s[:, 0:HIST, :] = gc0_ref[...]
    xp_s[:, HIST:HIST + tlx, :] = x_ref[...]
    qkv = _silu(_conv_taps(xp_s, cw_ref, CONV_C, tlx)).reshape(NB, RB, 3 * D_MIX)

    @pl.when(cs == NCS - 1)
    def _():
        gcs_ref[...] = xp_s[:, pl.ds(HIST + lvl - (CONV_C - 1), CONV_C - 1), :]

    beta_all = _sigmoid(sa_ref[...].reshape(NB, RB, 128))
    g_all = -jnp.exp(alog_ref[...]) * _softplus(sb_ref[...].reshape(NB, RB, 128) + dtb_ref[...])
    if lv < NCS * tlx:
        assert NCS == 1
        valid = (lax.broadcasted_iota(jnp.int32, (NB, RB, 128), 1) % tl) < lv
        beta_all = jnp.where(valid, beta_all, 0.0)
        g_all = jnp.where(valid, g_all, 0.0)
    rr = lax.broadcasted_iota(jnp.int32, (RB, RB), 0)
    cc = lax.broadcasted_iota(jnp.int32, (RB, RB), 1)
    incl = rr >= cc
    if tl < RB:
        incl = incl & ((rr // tl) == (cc // tl))
    diag = rr == cc
    tril = jnp.broadcast_to(incl.astype(BF16)[None], (NB, RB, RB))
    hi, mid, lo = _split3(g_all)
    gam = _bmm(tril, hi) + _bmm(tril, mid) + _bmm(tril, lo)
    glast = gam.reshape(NB * G, tl, 128)[:, tl - 1:tl, :]
    glast_rows = jnp.broadcast_to(glast, (NB * G, tl, 128)).reshape(NB, RB, 128)
    egam = jnp.exp(gam)
    gcol = _gate_cols(gam, NB)
    bcol = _gate_cols(beta_all, NB)
    egcol = _gate_cols(egam, NB)
    kdcol = _gate_cols(jnp.exp(glast_rows - gam), NB)

    q = _heads(qkv[:, :, 0:D_MIX], NB, DH)
    k = _heads(qkv[:, :, D_MIX:2 * D_MIX], NB, DH)
    v = _heads(qkv[:, :, 2 * D_MIX:3 * D_MIX], NB, DH)
    q = q * lax.rsqrt(jnp.sum(q * q, axis=-1, keepdims=True) + NORM_EPS) * (DH ** -0.5)
    k = k * lax.rsqrt(jnp.sum(k * k, axis=-1, keepdims=True) + NORM_EPS)
    qb, kb = q.astype(BF16), k.astype(BF16)

    dmat = jnp.exp(jnp.where(incl[None], gcol - _gate_rows(gam, NB), NEG))
    nmat = jnp.where(diag[None], 0.0, -(bcol * _bmm_nt(kb, kb) * dmat))
    rhs = jnp.concatenate([bcol * v, (bcol * egcol) * k], axis=-1)
    sol = _bmm(_unit_lower_inverse(nmat, rr, cc, tl), rhs)
    qkm = (_bmm_nt(qb, kb) * dmat).astype(BF16).reshape(NB, N_HEADS, RB, RB)

    def put(ref, val):
        val4 = val.reshape(NB, N_HEADS, RB, DH)
        for h in range(N_HEADS):
            ref[:, :, h] = val4[:, h].reshape(NB, G, tl, DH)

    put(u_ref, sol[:, :, 0:DH])
    put(w_ref, sol[:, :, DH:2 * DH].astype(BF16))
    put(qg_ref, (q * egcol).astype(BF16))
    put(kd_ref, (k * kdcol).astype(BF16))
    eg = jnp.exp(glast).reshape(NB, G, 1, 128)
    for h in range(N_HEADS):
        eg_ref[:, :, h] = jnp.broadcast_to(eg[:, :, :, 4 + h:5 + h], (NB, G, 1, 128))
        for g in range(G):
            qkm_ref[:, g, h] = qkm[:, h, g * tl:(g + 1) * tl, g * tl:(g + 1) * tl]


def _gdn_prep(proj, gc0, cw, alog, dtb, NB, G, tl, lv):
    nb, lp, _ = proj.shape
    assert tl % INV_BLOCK == 0 and (tl // INV_BLOCK) & (tl // INV_BLOCK - 1) == 0 and G * tl == RB
    if G == 1:
        gx, tlx = 1, NB * RB
    else:
        assert lp == tl
        gx, tlx = NB * G, tl
    NI, NCS = nb // gx, lp // tlx
    nbt = nb * lp // RB
    step = lambda i, c: (i * NCS + c, 0, 0, 0, 0)
    par = lambda i, c: (0, 0)
    chain = lambda last, dt: jax.ShapeDtypeStruct((nbt, G, N_HEADS, tl, last), dt)
    cspec = lambda last: pl.BlockSpec((NB, G, N_HEADS, tl, last), step)
    return pl.pallas_call(
        functools.partial(_gdn_prep_body, NB=NB, G=G, tl=tl, tlx=tlx, NCS=NCS, lv=lv),
        grid=(NI, NCS),
        in_specs=[pl.BlockSpec((gx, tlx, 3 * D_MIX), lambda i, c: (i, c, OFF_QKVC // (3 * D_MIX))),
                  pl.BlockSpec((gx, HIST, 3 * D_MIX),
                               lambda i, c: (i, jnp.maximum(c * (tlx // HIST) - 1, 0), OFF_QKVC // (3 * D_MIX))),
                  pl.BlockSpec((gx, tlx, 128), lambda i, c: (i, c, OFF_SA // 128)),
                  pl.BlockSpec((gx, tlx, 128), lambda i, c: (i, c, OFF_SB // 128)),
                  pl.BlockSpec((gx, HIST, 3 * D_MIX), lambda i, c: (i, 0, 0)),
                  pl.BlockSpec((8, 3 * D_MIX), par),
                  pl.BlockSpec((1, 128), par),
                  pl.BlockSpec((1, 128), par)],
        out_specs=[cspec(DH), cspec(DH), cspec(DH), cspec(DH), cspec(tl),
                   pl.BlockSpec((NB, G, N_HEADS, 1, 128), step),
                   pl.BlockSpec((gx, CONV_C - 1, 3 * D_MIX), lambda i, c: (i, 0, 0))],
        out_shape=[chain(DH, F32), chain(DH, BF16), chain(DH, BF16), chain(DH, BF16), chain(tl, BF16),
                   jax.ShapeDtypeStruct((nbt, G, N_HEADS, 1, 128), F32),
                   jax.ShapeDtypeStruct((nb, CONV_C - 1, 3 * D_MIX), F32)],
        scratch_shapes=[pltpu.VMEM((gx, HIST + tlx, 3 * D_MIX), F32)],
        compiler_params=pltpu.CompilerParams(dimension_semantics=("arbitrary", "arbitrary"),
                                             vmem_limit_bytes=VMEM_LIMIT),
        name=f"gdn_prep_g{G}",
    )(proj, proj, proj, proj, gc0, cw, alog, dtb)


def _gdn_scan_body(u_ref, w_ref, qg_ref, kd_ref, qkm_ref, eg_ref, z_ref, S0_ref, gng_ref, yc_ref, S_ref,
                   *, NS, tl):
    c = pl.program_id(1)
    n = NS * N_HEADS

    @pl.when(c == 0)
    def _():
        S_ref[...] = S0_ref[...]

    S = S_ref[...].reshape(n, DH, DH)
    Sb = S.astype(BF16)
    chains = lambda ref: ref[...].reshape(n, tl, ref.shape[-1])
    v_new = chains(u_ref) - jnp.einsum('ntd,nde->nte', chains(w_ref), Sb, preferred_element_type=F32)
    vnb = v_new.astype(BF16)
    o = (jnp.einsum('ntd,nde->nte', chains(qg_ref), Sb, preferred_element_type=F32)
         + jnp.einsum('nts,nse->nte', chains(qkm_ref), vnb, preferred_element_type=F32))
    eg = eg_ref[...].reshape(n, 1, 128)[:, :, 0:1]
    S_new = eg * S + jnp.einsum('ntd,nte->nde', chains(kd_ref), vnb, preferred_element_type=F32)
    S_ref[...] = S_new.reshape(NS, N_HEADS, DH, DH)

    on = (o * lax.rsqrt(jnp.mean(o * o, axis=-1, keepdims=True) + NORM_EPS) * gng_ref[...]).reshape(NS, N_HEADS, tl, DH)
    for h in range(N_HEADS):
        yc_ref[:, :, h * DH:(h + 1) * DH] = (on[:, h] * _silu(z_ref[:, :, h * DH:(h + 1) * DH])).astype(BF16)


def _gdn_scan(pre, proj, S0, gng, SB, G, tl):
    nb, lp, _ = proj.shape
    NS = SB * G
    NI, NC = nb // NS, lp // tl
    six = lambda a: a.reshape((NI * SB, NC) + a.shape[1:])
    cspec = lambda last: pl.BlockSpec((SB, 1, G, N_HEADS, tl, last), lambda i, c: (i, c, 0, 0, 0, 0))
    u, w, qg, kd, qkm, eg = (six(a) for a in pre)
    return pl.pallas_call(
        functools.partial(_gdn_scan_body, NS=NS, tl=tl),
        grid=(NI, NC),
        in_specs=[cspec(DH), cspec(DH), cspec(DH), cspec(DH), cspec(tl),
                  pl.BlockSpec((SB, 1, G, N_HEADS, 1, 128), lambda i, c: (i, c, 0, 0, 0, 0)),
                  pl.BlockSpec((NS, tl, D_MIX), lambda i, c: (i, c, OFF_Z // D_MIX)),
                  pl.BlockSpec((NS, N_HEADS, DH, DH), lambda i, c: (i, 0, 0, 0)),
                  pl.BlockSpec((1, DH), lambda i, c: (0, 0))],
        out_specs=[pl.BlockSpec((NS, tl, D_MIX), lambda i, c: (i, c, 0)),
                   pl.BlockSpec((NS, N_HEADS, DH, DH), lambda i, c: (i, 0, 0, 0))],
        out_shape=[jax.ShapeDtypeStruct((nb, lp, D_MIX), BF16),
                   jax.ShapeDtypeStruct((nb, N_HEADS, DH, DH), F32)],
        compiler_params=pltpu.CompilerParams(dimension_semantics=("arbitrary", "arbitrary"),
                                             vmem_limit_bytes=VMEM_LIMIT),
        name=f"gdn_scan_g{G}",
    )(u, w, qg, kd, qkm, eg, proj, S0, gng)


def _route(rl):
    lane = lax.broadcasted_iota(jnp.int32, rl.shape, 1).astype(F32)
    is_g = (lane >= N_EXPERTS) & (lane < N_EXPERTS + N_GROUPS)
    gl = jnp.where(is_g, rl, NEG)
    gmax = jnp.max(gl, axis=-1, keepdims=True)
    grp = jnp.min(jnp.where(gl == gmax, lane - N_EXPERTS, 4.0 * N_EXPERTS), axis=-1, keepdims=True)
    p_grp = 1.0 / jnp.sum(jnp.where(is_g, jnp.exp(gl - gmax), 0.0), axis=-1, keepdims=True)
    lo = grp * EXPERTS_PER_GROUP
    in_grp = (lane >= lo) & (lane < lo + EXPERTS_PER_GROUP)
    el = jnp.where(in_grp, rl, NEG)
    m1 = jnp.max(el, axis=-1, keepdims=True)
    i1 = jnp.min(jnp.where(el == m1, lane, 4.0 * N_EXPERTS), axis=-1, keepdims=True)
    el2 = jnp.where(lane == i1, NEG, el)
    m2 = jnp.max(el2, axis=-1, keepdims=True)
    i2 = jnp.min(jnp.where(el2 == m2, lane, 4.0 * N_EXPERTS), axis=-1, keepdims=True)
    e2 = jnp.exp(m2 - m1)
    w1 = p_grp / (1.0 + e2)
    w2 = p_grp * e2 / (1.0 + e2)
    return i1, i2, w1, w2


def _merge_body(ya_ref, yb_ref, yc_ref, ga_ref, gb_ref, gc_ref, x_ref, mod_ref, wa_ref, wb_ref, wc_ref, wo_ref,
                lg_ref, lb_ref, wrh_ref, wrl_ref, br_ref, x1_ref, rinfo_ref, cnt_ref, *, bt, lt):
    tm = bt * lt

    @pl.when((pl.program_id(0) == 0) & (pl.program_id(1) == 0))
    def _():
        cnt_ref[...] = jnp.zeros_like(cnt_ref)

    def r2(ref):
        return ref[...].reshape(tm, ref.shape[-1])

    merged = (_sigmoid(r2(ga_ref)) * _dot(r2(ya_ref), wa_ref[0])
              + _sigmoid(r2(gb_ref)) * _dot(r2(yb_ref), wb_ref[0])
              + _sigmoid(r2(gc_ref)) * _dot(r2(yc_ref), wc_ref[0]))
    out = _dot(merged.astype(BF16), wo_ref[0])
    y = DN_ALPHA * x_ref[...] + (1.0 + mod_ref[:, 2:3, :]) * out.reshape(bt, lt, D_MODEL)
    x1 = _layer_norm(y, lg_ref[0], lb_ref[0])
    x1_ref[...] = x1
    u2 = (x1 * (1.0 + mod_ref[:, 4:5, :]) + mod_ref[:, 3:4, :]).reshape(tm, D_MODEL)
    hi = u2.astype(BF16)
    lo = (u2 - hi.astype(F32)).astype(BF16)
    rl = _dot(hi, wrh_ref[0]) + _dot(lo, wrh_ref[0]) + _dot(hi, wrl_ref[0]) + br_ref[0]
    i1, i2, w1, w2 = _route(rl)
    lane = lax.broadcasted_iota(jnp.int32, (tm, 128), 1).astype(F32)
    onehot = jnp.where((lane == i1) | (lane == i2), 1.0, 0.0)
    rr = lax.broadcasted_iota(jnp.int32, (tm, tm), 0)
    cc = lax.broadcasted_iota(jnp.int32, (tm, tm), 1)
    before = _dot((rr > cc).astype(BF16), onehot.astype(BF16)) + cnt_ref[0:1, :]
    rank1 = jnp.sum(jnp.where(lane == i1, before, 0.0), axis=-1, keepdims=True)
    rank2 = jnp.sum(jnp.where(lane == i2, before, 0.0), axis=-1, keepdims=True)
    cnt_ref[0:1, :] += jnp.sum(onehot, axis=0, keepdims=True)
    rinfo = jnp.zeros((tm, 128), F32)
    for k, val in enumerate((i1, i2, w1, w2, rank1, rank2)):
        rinfo = jnp.where(lane == k, val, rinfo)
    rinfo_ref[...] = rinfo.reshape(bt, lt, 128)


def _merge(ya, yb, yc, proj, x, mod, wa, wb, wc, wo, lg, lb, wrh, wrl, br, layer, bt, lt):
    nb, lp, _ = x.shape
    tok = lambda i, t: (i, t, 0)
    wsp = lambda shape: pl.BlockSpec((1,) + shape, lambda i, t: (layer, 0, 0))
    g0 = OFF_GTS // D_MODEL
    return pl.pallas_call(
        functools.partial(_merge_body, bt=bt, lt=lt),
        grid=(nb // bt, lp // lt),
        in_specs=[pl.BlockSpec((bt, lt, D_MIX), tok),
                  pl.BlockSpec((bt, lt, D_MIX), tok),
                  pl.BlockSpec((bt, lt, D_MIX), tok),
                  pl.BlockSpec((bt, lt, D_MODEL), lambda i, t: (i, t, g0)),
                  pl.BlockSpec((bt, lt, D_MODEL), lambda i, t: (i, t, g0 + 1)),
                  pl.BlockSpec((bt, lt, D_MODEL), lambda i, t: (i, t, g0 + 2)),
                  pl.BlockSpec((bt, lt, D_MODEL), tok),
                  pl.BlockSpec((bt, 6, D_MODEL), lambda i, t: (i, 0, 0)),
                  wsp((D_MIX, D_MODEL)), wsp((D_MIX, D_MODEL)), wsp((D_MIX, D_MODEL)), wsp((D_MODEL, D_MODEL)),
                  wsp((1, D_MODEL)), wsp((1, D_MODEL)),
                  wsp((D_MODEL, 128)), wsp((D_MODEL, 128)), wsp((1, 128))],
        out_specs=[pl.BlockSpec((bt, lt, D_MODEL), tok),
                   pl.BlockSpec((bt, lt, 128), tok),
                   pl.BlockSpec((8, 128), lambda i, t: (0, 0))],
        out_shape=[jax.ShapeDtypeStruct((nb, lp, D_MODEL), F32),
                   jax.ShapeDtypeStruct((nb, lp, 128), F32),
                   jax.ShapeDtypeStruct((8, 128), F32)],
        compiler_params=pltpu.CompilerParams(dimension_semantics=("arbitrary", "arbitrary"),
                                             vmem_limit_bytes=VMEM_LIMIT),
        name=f"merge_l{layer}_b{bt}",
    )(ya, yb, yc, proj, proj, proj, x, mod, wa, wb, wc, wo, lg, lb, wrh, wrl, br)


def _moe_plan(rinfo, cnt, tm):
    nb, lp, _ = rinfo.shape
    n_tok = nb * lp
    n_blocks = 2 * n_tok // MOE_BLOCK + N_EXPERTS
    counts = cnt[0, :N_EXPERTS].astype(jnp.int32)
    nblk = (counts + MOE_BLOCK - 1) // MOE_BLOCK
    pend = jnp.cumsum(nblk)
    pstart = pend - nblk
    expert = rinfo[..., 0:2].astype(jnp.int32)
    rank = rinfo[..., 4:6].astype(jnp.int32)
    pos = ((pstart * MOE_BLOCK)[expert] + rank).reshape(n_tok // tm, 1, 2 * tm)
    block_e = jnp.minimum(jnp.searchsorted(pend, jnp.arange(n_blocks), side='right'), N_EXPERTS - 1).astype(jnp.int32)
    n_used = pend[N_EXPERTS - 1:].astype(jnp.int32)
    ztail = jnp.where(nblk > 0, (pend - 1) * MOE_BLOCK, -1).astype(jnp.int32)
    ztail = jnp.concatenate([ztail, n_used]).reshape(1, 1, N_EXPERTS + 1)
    return pos, block_e, n_used, ztail, n_blocks


def _row_copy(src, dst, sem):
    return pltpu.make_async_copy(src, dst, sem)


def _dispatch_body(pos_ref, ztail_ref, x1_ref, mod_ref, disp_ref, u_scr, z_scr, sem, zsem, *, bt, lt, n_blocks):
    tm = bt * lt

    @pl.when((pl.program_id(0) == 0) & (pl.program_id(1) == 0))
    def _():
        z_scr[...] = jnp.zeros_like(z_scr)
        for e in range(N_EXPERTS):
            @pl.when(ztail_ref[0, 0, e] >= 0)
            def _():
                row = pl.multiple_of(ztail_ref[0, 0, e], MOE_BLOCK)
                _row_copy(z_scr, disp_ref.at[pl.ds(row, MOE_BLOCK)], zsem).start()
        def unused(j):
            return _row_copy(z_scr, disp_ref.at[pl.ds(pl.multiple_of(j * MOE_BLOCK, MOE_BLOCK), MOE_BLOCK)], zsem)

        n_used = ztail_ref[0, 0, N_EXPERTS]
        lax.fori_loop(n_used, n_blocks, lambda j, c: (unused(j).start(), c)[1], 0)
        lax.fori_loop(n_used, n_blocks, lambda j, c: (unused(j).wait(), c)[1], 0)
        for e in range(N_EXPERTS):
            @pl.when(ztail_ref[0, 0, e] >= 0)
            def _():
                row = pl.multiple_of(ztail_ref[0, 0, e], MOE_BLOCK)
                _row_copy(z_scr, disp_ref.at[pl.ds(row, MOE_BLOCK)], zsem).wait()

    u2 = x1_ref[...] * (1.0 + mod_ref[:, 4:5, :]) + mod_ref[:, 3:4, :]
    u_scr[...] = u2.reshape(tm, D_MODEL)

    def issue(t, carry):
        for k in range(2):
            _row_copy(u_scr.at[pl.ds(t, 1)], disp_ref.at[pl.ds(pos_ref[0, 0, 2 * t + k], 1)], sem).start()
        return carry

    lax.fori_loop(0, tm, issue, 0, unroll=8)
    for k in range(2):
        _row_copy(u_scr, disp_ref.at[pl.ds(0, tm)], sem).wait()


def _dispatch(x1, mod, pos, ztail, n_blocks, bt, lt):
    nb, lp, _ = x1.shape
    nt = lp // lt
    return pl.pallas_call(
        functools.partial(_dispatch_body, bt=bt, lt=lt, n_blocks=n_blocks),
        grid=(nb // bt, nt),
        in_specs=[pl.BlockSpec((1, 1, 2 * bt * lt), lambda i, t: (i * nt + t, 0, 0), memory_space=pltpu.SMEM),
                  pl.BlockSpec((1, 1, N_EXPERTS + 1), lambda i, t: (0, 0, 0), memory_space=pltpu.SMEM),
                  pl.BlockSpec((bt, lt, D_MODEL), lambda i, t: (i, t, 0)),
                  pl.BlockSpec((bt, 6, D_MODEL), lambda i, t: (i, 0, 0))],
        out_specs=pl.BlockSpec(memory_space=pl.ANY),
        out_shape=jax.ShapeDtypeStruct((n_blocks * MOE_BLOCK, D_MODEL), F32),
        scratch_shapes=[pltpu.VMEM((bt * lt, D_MODEL), F32), pltpu.VMEM((MOE_BLOCK, D_MODEL), F32),
                        pltpu.SemaphoreType.DMA, pltpu.SemaphoreType.DMA],
        compiler_params=pltpu.CompilerParams(dimension_semantics=("arbitrary", "arbitrary"),
                                             vmem_limit_bytes=VMEM_LIMIT),
        name=f"dispatch_b{bt}",
    )(pos, ztail, x1, mod)


def _experts_body(be_ref, nu_ref, x_ref, wg_ref, wu_ref, wd_ref, o_ref):
    j = pl.program_id(0)

    @pl.when(j < nu_ref[0])
    def _():
        x = x_ref[...].astype(BF16)
        hb = _silu(_dot(x, wg_ref[0, 0].astype(BF16))) * _dot(x, wu_ref[0, 0].astype(BF16))
        o_ref[...] = _dot(hb.astype(BF16), wd_ref[0, 0].astype(BF16))

    @pl.when(j >= nu_ref[0])
    def _():
        o_ref[...] = jnp.zeros_like(o_ref)


def _experts(disp, block_e, n_used, wg, wu, wd, layer):
    n_blocks = disp.shape[0] // MOE_BLOCK
    wmap = lambda j, be, nu: (layer, be[j], 0, 0)
    return pl.pallas_call(
        _experts_body,
        grid_spec=pltpu.PrefetchScalarGridSpec(
            num_scalar_prefetch=2,
            grid=(n_blocks,),
            in_specs=[pl.BlockSpec((MOE_BLOCK, D_MODEL), lambda j, be, nu: (jnp.minimum(j, nu[0] - 1), 0)),
                      pl.BlockSpec((1, 1, D_MODEL, D_EXPERT), wmap),
                      pl.BlockSpec((1, 1, D_MODEL, D_EXPERT), wmap),
                      pl.BlockSpec((1, 1, D_EXPERT, D_MODEL), wmap)],
            out_specs=pl.BlockSpec((MOE_BLOCK, D_MODEL), lambda j, be, nu: (j, 0))),
        out_shape=jax.ShapeDtypeStruct(disp.shape, F32),
        compiler_params=pltpu.CompilerParams(dimension_semantics=("arbitrary",), vmem_limit_bytes=VMEM_LIMIT),
        name=f"experts_l{layer}_n{n_blocks}",
    )(block_e, n_used, disp, wg, wu, wd)


def _combine_body(pos_ref, x1_ref, mod_ref, rinfo_ref, eo_ref, lg_ref, lb_ref, x2_ref, r_scr, sem, *, bt, lt):
    tm = bt * lt

    def issue(t, carry):
        for k in range(2):
            _row_copy(eo_ref.at[pl.ds(pos_ref[0, 0, 2 * t + k], 1)], r_scr.at[k, pl.ds(t, 1)], sem).start()
        return carry

    lax.fori_loop(0, tm, issue, 0, unroll=8)
    for k in range(2):
        _row_copy(eo_ref.at[pl.ds(0, tm)], r_scr.at[k], sem).wait()

    rinfo = rinfo_ref[...].reshape(tm, 128)
    moe = r_scr[0] * rinfo[:, 2:3] + r_scr[1] * rinfo[:, 3:4]
    y = DN_ALPHA * x1_ref[...] + (1.0 + mod_ref[:, 5:6, :]) * moe.reshape(bt, lt, D_MODEL)
    x2_ref[...] = _layer_norm(y, lg_ref[0], lb_ref[0])


def _combine(x1, mod, rinfo, eo, pos, lg, lb, layer, bt, lt):
    nb, lp, _ = x1.shape
    nt = lp // lt
    tok = lambda i, t: (i, t, 0)
    return pl.pallas_call(
        functools.partial(_combine_body, bt=bt, lt=lt),
        grid=(nb // bt, nt),
        in_specs=[pl.BlockSpec((1, 1, 2 * bt * lt), lambda i, t: (i * nt + t, 0, 0), memory_space=pltpu.SMEM),
                  pl.BlockSpec((bt, lt, D_MODEL), tok),
                  pl.BlockSpec((bt, 6, D_MODEL), lambda i, t: (i, 0, 0)),
                  pl.BlockSpec((bt, lt, 128), tok),
                  pl.BlockSpec(memory_space=pl.ANY),
                  pl.BlockSpec((1, 1, D_MODEL), lambda i, t: (layer, 0, 0)),
                  pl.BlockSpec((1, 1, D_MODEL), lambda i, t: (layer, 0, 0))],
        out_specs=pl.BlockSpec((bt, lt, D_MODEL), tok),
        out_shape=jax.ShapeDtypeStruct((nb, lp, D_MODEL), F32),
        scratch_shapes=[pltpu.VMEM((2, bt * lt, D_MODEL), F32), pltpu.SemaphoreType.DMA],
        compiler_params=pltpu.CompilerParams(dimension_semantics=("arbitrary", "arbitrary"),
                                             vmem_limit_bytes=VMEM_LIMIT),
        name=f"combine_l{layer}_b{bt}",
    )(pos, x1, mod, rinfo, eo, lg, lb)


def _hist(state):
    return jnp.pad(state, ((0, 0), (HIST - state.shape[1], 0), (0, 0)))


def _trunk(x, mod, st, p, bt, lt, G, tl, lv, gdn_cfg):
    nb = x.shape[0]
    new = {key: [] for key in ('C', 'n', 'm', 'conv', 'S', 'gconv')}
    for l in range(DEPTH):
        proj, x = _inproj(x, mod[l], p['ln_in_g'], p['ln_in_b'], p['w_in_r'], p['b_in_r'], l, l == 0, bt, lt)
        m0 = jnp.pad(st['m'][l], ((0, 0), (0, 128 - N_HEADS))).reshape(nb // G, G, 128)
        ya, yb, C, n, m, conv = _mlstm(proj, st['C'][l], st['n'][l], m0, _hist(st['conv'][l]),
                                       p['mlstm_norm_g'][l:l + 1], p['conv_b_w8'][l], G, tl, lv)
        gd = gdn_cfg
        *pre, gconv = _gdn_prep(proj, _hist(st['gconv'][l]), p['conv_c_w8'][l], p['alog_row'][l], p['dtb_row'][l],
                                gd['NB'], gd['G'], gd['tl'], lv)
        yc, S = _gdn_scan(pre, proj, st['S'][l], p['gdn_norm_g'][l:l + 1], gd['SB'], gd['G'], gd['tl'])
        x1, rinfo, cnt = _merge(ya, yb, yc, proj, x, mod[l], p['w_br_a'], p['w_br_b'], p['w_br_c'], p['w_out'],
                                p['ln1_g'], p['ln1_b'], p['wr_hi'], p['wr_lo'], p['br'], l, bt, lt)
        pos, block_e, n_used, ztail, n_blocks = _moe_plan(rinfo, cnt, bt * lt)
        disp = _dispatch(x1, mod[l], pos, ztail, n_blocks, bt, lt)
        eo = _experts(disp, block_e, n_used, p['exp_w_gate'], p['exp_w_up'], p['exp_w_down'], l)
        x = _combine(x1, mod[l], rinfo, eo, pos, p['ln2_g'], p['ln2_b'], l, bt, lt)
        new['C'].append(C)
        new['n'].append(n)
        new['m'].append(m.reshape(nb, 128)[:, :N_HEADS])
        new['conv'].append(conv)
        new['S'].append(S)
        new['gconv'].append(gconv)
    return x, {key: jnp.stack(val) for key, val in new.items()}


def kernel(x_prompt, x_sample, state_mlstm_C, state_mlstm_n, state_mlstm_m, state_conv, state_gdn_S, state_gdn_conv, c_prompt, c_sample, ln_in_g, ln_in_b, w_ada, b_ada, w_in, b_in, mlstm_norm_g, conv_b_w, conv_c_w, gdn_a_log, gdn_dt_bias, gdn_norm_g, w_br_a, w_br_b, w_br_c, w_out, ln1_g, ln1_b, router_g_w, router_g_b, router_e_w, router_e_b, exp_w_gate, exp_w_up, exp_w_down, ln2_g, ln2_b):
    nbp, lp, _ = x_prompt.shape
    nbs, ls, _ = x_sample.shape
    lsp = 8

    def regroup(a):
        parts = [a[..., 3592:5128], a[..., 2056:3592], a[..., 5648:8720], a[..., 0:2048], a[..., 5128:5640],
                 a[..., 2048:2052], a[..., 5640:5644], jnp.zeros(a.shape[:-1] + (120,), a.dtype),
                 a[..., 2052:2056], a[..., 5644:5648], jnp.zeros(a.shape[:-1] + (120,), a.dtype)]
        return jnp.concatenate(parts, axis=-1)

    wr = jnp.concatenate([router_e_w, router_g_w, jnp.zeros((DEPTH, D_MODEL, 128 - N_EXPERTS - N_GROUPS), F32)], axis=-1)
    wr_hi = wr.astype(BF16)
    lane_pad = lambda a: jnp.pad(a, ((0, 0), (4, 128 - 4 - N_HEADS)))[:, None, :]
    p = dict(
        ln_in_g=ln_in_g.reshape(1, D_MODEL), ln_in_b=ln_in_b.reshape(1, D_MODEL),
        w_in_r=regroup(w_in).astype(BF16), b_in_r=regroup(b_in).reshape(DEPTH, 1, N_PROJ),
        mlstm_norm_g=mlstm_norm_g,
        conv_b_w8=jnp.pad(conv_b_w, ((0, 0), (0, 8 - CONV_B), (0, 0))),
        conv_c_w8=jnp.pad(conv_c_w, ((0, 0), (0, 8 - CONV_C), (0, 0))),
        alog_row=lane_pad(gdn_a_log), dtb_row=lane_pad(gdn_dt_bias), gdn_norm_g=gdn_norm_g,
        w_br_a=w_br_a.astype(BF16), w_br_b=w_br_b.astype(BF16), w_br_c=w_br_c.astype(BF16),
        w_out=w_out.astype(BF16),
        ln1_g=ln1_g.reshape(DEPTH, 1, D_MODEL), ln1_b=ln1_b.reshape(DEPTH, 1, D_MODEL),
        wr_hi=wr_hi, wr_lo=(wr - wr_hi.astype(F32)).astype(BF16),
        br=jnp.concatenate([router_e_b, router_g_b, jnp.zeros((DEPTH, 128 - N_EXPERTS - N_GROUPS), F32)],
                           axis=-1).reshape(DEPTH, 1, 128),
        exp_w_gate=exp_w_gate, exp_w_up=exp_w_up, exp_w_down=exp_w_down,
        ln2_g=ln2_g.reshape(DEPTH, 1, D_MODEL), ln2_b=ln2_b.reshape(DEPTH, 1, D_MODEL),
    )

    mod = _ada(jnp.concatenate([c_prompt, c_sample], axis=0), w_ada, b_ada)
    mod = mod.reshape(DEPTH, nbp + nbs, 6, D_MODEL)

    zeros = lambda *s: jnp.zeros((DEPTH, nbp) + s, F32)
    st_p = {'C': zeros(N_HEADS, DH, DH), 'n': zeros(N_HEADS, DH), 'm': zeros(N_HEADS),
            'conv': zeros(CONV_B - 1, D_MIX), 'S': zeros(N_HEADS, DH, DH), 'gconv': zeros(CONV_C - 1, 3 * D_MIX)}
    y_p, sp = _trunk(x_prompt, mod[:, :nbp], st_p, p, bt=1, lt=512, G=1, tl=64, lv=lp,
                     gdn_cfg=dict(NB=2, G=1, tl=RB, SB=nbp))

    st_s = {'C': state_mlstm_C, 'n': state_mlstm_n, 'm': state_mlstm_m, 'conv': state_conv,
            'S': state_gdn_S, 'gconv': state_gdn_conv}
    xs = jnp.pad(x_sample, ((0, 0), (0, lsp - ls), (0, 0)))
    y_s, ss = _trunk(xs, mod[:, nbp:], st_s, p, bt=64, lt=lsp, G=16, tl=lsp, lv=ls,
                     gdn_cfg=dict(NB=2, G=RB // lsp, tl=lsp, SB=1))
    y_s = y_s[:, :ls]

    return (y_p, y_s, sp['C'], sp['n'], sp['m'], sp['conv'], sp['S'], sp['gconv'],
            ss['C'], ss['n'], ss['m'], ss['conv'], ss['S'], ss['gconv'])
```

```python
import functools

import jax
import jax.numpy as jnp
from jax import lax
from jax.experimental import pallas as pl
from jax.experimental.pallas import tpu as pltpu

F32 = jnp.float32
BF16 = jnp.bfloat16

D_MODEL = 1024
DEPTH = 2
N_HEADS = 4
DH = 128
D_MIX = N_HEADS * DH
N_EXPERTS = 32
EXPERTS_PER_GROUP = 8
N_GROUPS = 4
D_EXPERT = 256
CONV_B = 3
CONV_C = 4
HIST = 8
RB = 64
INV_BLOCK = 8
DN_ALPHA = (2 * DEPTH) ** 0.25
LN_EPS = 1e-5
NORM_EPS = 1e-6
NEG = -1e30

OFF_QKVC = 0
OFF_BCH = 1536
OFF_GTS = 3072
OFF_QKVO = 6144
OFF_Z = 8192
OFF_SA = 8704
OFF_SB = 8832
N_PROJ = 8960
TN_PROJ = 1280

VMEM_LIMIT = 52 * 1024 * 1024


def _dot(a, b):
    return jnp.dot(a, b, preferred_element_type=F32)


def _dot_nt(a, b):
    return lax.dot_general(a, b, (((1,), (1,)), ((), ())), preferred_element_type=F32)


def _split3(x):
    hi = x.astype(BF16)
    r = x - hi.astype(F32)
    mid = r.astype(BF16)
    lo = (r - mid.astype(F32)).astype(BF16)
    return hi, mid, lo


def _dot3(m, x):
    hi, mid, lo = _split3(x)
    return _dot(m, hi) + _dot(m, mid) + _dot(m, lo)


def _dotb(a, b):
    return _dot(a.astype(BF16), b.astype(BF16))


def _layer_norm(x, g, b):
    mu = jnp.mean(x, axis=-1, keepdims=True)
    xc = x - mu
    var = jnp.mean(xc * xc, axis=-1, keepdims=True)
    return xc * lax.rsqrt(var + LN_EPS) * g + b


def _sigmoid(x):
    return jax.nn.sigmoid(x)


def _silu(x):
    return x * jax.nn.sigmoid(x)


def _log_sigmoid(x):
    return jnp.minimum(x, 0.0) - jnp.log1p(jnp.exp(-jnp.abs(x)))


def _softplus(x):
    return jnp.maximum(x, 0.0) + jnp.log1p(jnp.exp(-jnp.abs(x)))


def _seq_rows(x3, tl):
    g, _, c = x3.shape
    return jnp.broadcast_to(x3, (g, tl, c)).reshape(g * tl, c)


def _ada_body(c_ref, w_ref, b_ref, o_ref):
    c = c_ref[...]
    s = _silu(c).astype(BF16)
    o_ref[0] = _dot(s, w_ref[0].astype(BF16)) + b_ref[0]


def _ada(c_all, w_ada, b_ada):
    nb = c_all.shape[0]
    return pl.pallas_call(
        _ada_body,
        grid=(DEPTH, 6),
        in_specs=[pl.BlockSpec((nb, D_MODEL), lambda l, j: (0, 0)),
                  pl.BlockSpec((1, D_MODEL, D_MODEL), lambda l, j: (l, 0, j)),
                  pl.BlockSpec((1, 1, D_MODEL), lambda l, j: (l, 0, j))],
        out_specs=pl.BlockSpec((1, nb, D_MODEL), lambda l, j: (l, 0, j)),
        out_shape=jax.ShapeDtypeStruct((DEPTH, nb, 6 * D_MODEL), F32),
        compiler_params=pltpu.CompilerParams(dimension_semantics=("arbitrary", "arbitrary"),
                                             vmem_limit_bytes=VMEM_LIMIT),
        name="ada",
    )(c_all, w_ada, b_ada.reshape(DEPTH, 1, 6 * D_MODEL))


def _inproj_body(x_ref, mod_ref, g_ref, b_ref, w_ref, bias_ref, *rest, apply_ln, bt, lt):
    if apply_ln:
        proj_ref, xn_ref, u_scr = rest
    else:
        proj_ref, u_scr = rest
    j = pl.program_id(2)

    @pl.when(j == 0)
    def _():
        x = x_ref[...]
        if apply_ln:
            x = _layer_norm(x, g_ref[...], b_ref[...])
            xn_ref[...] = x
        u = x * (1.0 + mod_ref[:, 1:2, :]) + mod_ref[:, 0:1, :]
        u_scr[...] = u.reshape(bt * lt, D_MODEL).astype(BF16)

    acc = _dot(u_scr[...], w_ref[0]) + bias_ref[0]
    proj_ref[...] = acc.reshape(bt, lt, TN_PROJ)


def _inproj(x, mod, ln_g, ln_b, w_r, b_r, layer, apply_ln, bt, lt):
    nb, lp, _ = x.shape
    grid = (nb // bt, lp // lt, N_PROJ // TN_PROJ)
    tok = lambda i, t, j: (i, t, 0)
    out_shape = [jax.ShapeDtypeStruct((nb, lp, N_PROJ), F32)]
    out_specs = [pl.BlockSpec((bt, lt, TN_PROJ), lambda i, t, j: (i, t, j))]
    if apply_ln:
        out_shape.append(jax.ShapeDtypeStruct((nb, lp, D_MODEL), F32))
        out_specs.append(pl.BlockSpec((bt, lt, D_MODEL), tok))
    res = pl.pallas_call(
        functools.partial(_inproj_body, apply_ln=apply_ln, bt=bt, lt=lt),
        grid=grid,
        in_specs=[pl.BlockSpec((bt, lt, D_MODEL), tok),
                  pl.BlockSpec((bt, 6, D_MODEL), lambda i, t, j: (i, 0, 0)),
                  pl.BlockSpec((1, D_MODEL), lambda i, t, j: (0, 0)),
                  pl.BlockSpec((1, D_MODEL), lambda i, t, j: (0, 0)),
                  pl.BlockSpec((1, D_MODEL, TN_PROJ), lambda i, t, j: (layer, 0, j)),
                  pl.BlockSpec((1, 1, TN_PROJ), lambda i, t, j: (layer, 0, j))],
        out_specs=out_specs,
        out_shape=out_shape,
        scratch_shapes=[pltpu.VMEM((bt * lt, D_MODEL), BF16)],
        compiler_params=pltpu.CompilerParams(dimension_semantics=("arbitrary", "arbitrary", "arbitrary"),
                                             vmem_limit_bytes=VMEM_LIMIT),
        name=f"inproj_l{layer}_b{bt}",
    )(x, mod, ln_g, ln_b, w_r, b_r)
    return (res[0], res[1]) if apply_ln else (res[0], x)


def _chunk_masks(G, tl):
    R = G * tl
    rr = lax.broadcasted_iota(jnp.int32, (R, R), 0)
    cc = lax.broadcasted_iota(jnp.int32, (R, R), 1)
    incl = rr >= cc
    if G > 1:
        incl = incl & ((rr // tl) == (cc // tl))
    diag = rr == cc
    return incl, diag


def _valid_rows(G, tl, c, lv):
    R = G * tl
    row = lax.broadcasted_iota(jnp.int32, (R, 128), 0)
    pos = (row % tl if G > 1 else row) + c * tl
    return pos < lv


def _conv_taps(xp_s, w_ref, width, tl):
    acc = None
    for j in range(width):
        tap = xp_s[:, pl.ds(HIST - (width - 1) + j, tl), :] * w_ref[j:j + 1, :].reshape(1, 1, -1)
        acc = tap if acc is None else acc + tap
    return acc


def _mlstm_body(qkvo_ref, bch_ref, sa_ref, sb_ref, C0_ref, n0_ref, m0_ref, cv0_ref, ng_ref, cw_ref,
                ya_ref, yb_ref, C_ref, n_ref, m_ref, cv_ref, xp_s, *, G, tl, NC, lv):
    c = pl.program_id(1)
    R = G * tl
    padded = lv < NC * tl
    lvl = lv - (NC - 1) * tl

    @pl.when(c == 0)
    def _():
        C_ref[...] = C0_ref[...]
        n_ref[...] = n0_ref[...]
        m_ref[...] = m0_ref[...]
        xp_s[:, 0:HIST, :] = cv0_ref[...]

    xin = bch_ref[:, :, D_MIX:2 * D_MIX] * bch_ref[:, :, 2 * D_MIX:3 * D_MIX]
    xp_s[:, HIST:HIST + tl, :] = xin
    conv = _conv_taps(xp_s, cw_ref, CONV_B, tl)
    yb_ref[...] = (bch_ref[:, :, 0:D_MIX] * conv).astype(BF16)

    @pl.when(c == NC - 1)
    def _():
        cv_ref[...] = xp_s[:, pl.ds(HIST + lvl - (CONV_B - 1), CONV_B - 1), :]

    if NC > 1:
        xp_s[:, 0:HIST, :] = xp_s[:, tl:tl + HIST, :]

    i_all = sa_ref[...].reshape(R, 128)
    f_all = _log_sigmoid(sb_ref[...].reshape(R, 128))
    if padded:
        valid = _valid_rows(G, tl, c, lv)
        i_all = jnp.where(valid, i_all, NEG)
        f_all = jnp.where(valid, f_all, 0.0)
    incl, _ = _chunk_masks(G, tl)
    bcum = _dot3(incl.astype(BF16), f_all)
    b3 = bcum.reshape(G, tl, 128)
    blast3 = b3[:, tl - 1:tl, :]
    blast = _seq_rows(blast3, tl)
    m03 = m_ref[0].reshape(G, 1, 128)
    m0 = _seq_rows(m03, tl)
    val = blast - bcum + i_all
    mnew3 = jnp.maximum(jnp.max(val.reshape(G, tl, 128), axis=1, keepdims=True), blast3 + m03)
    mnew = _seq_rows(mnew3, tl)
    wk_all = jnp.exp(val - mnew)
    dec3 = jnp.exp(blast3 + m03 - mnew3)
    minter_all = bcum + m0
    iT = i_all.T
    bT = bcum.T

    for h in range(N_HEADS):
        hs = slice(h * DH, (h + 1) * DH)
        q = qkvo_ref[:, :, h * DH:(h + 1) * DH].reshape(R, DH)
        k = qkvo_ref[:, :, D_MIX + h * DH:D_MIX + (h + 1) * DH].reshape(R, DH) * (DH ** -0.5)
        v = qkvo_ref[:, :, 2 * D_MIX + h * DH:2 * D_MIX + (h + 1) * DH].reshape(R, DH)
        o = qkvo_ref[:, :, 3 * D_MIX + h * DH:3 * D_MIX + (h + 1) * DH].reshape(R, DH)
        qb, kb, vb = q.astype(BF16), k.astype(BF16), v.astype(BF16)

        b_col = bcum[:, h:h + 1]
        dlog = jnp.where(incl, b_col - bT[h:h + 1, :] + iT[h:h + 1, :], NEG)
        m_inter = minter_all[:, h:h + 1]
        m_t = jnp.maximum(m_inter, jnp.max(dlog, axis=-1, keepdims=True))
        s = _dot_nt(qb, kb) * jnp.exp(dlog - m_t)
        inter = jnp.exp(m_inter - m_t)

        Ch = C_ref[:, h]
        nh = n_ref[:, h:h + 1, :]
        q3 = q.reshape(G, tl, DH)
        qC = jnp.einsum('gtd,gde->gte', q3.astype(BF16), Ch.astype(BF16),
                        preferred_element_type=F32).reshape(R, DH)
        qn = jnp.sum(q3 * nh, axis=-1, keepdims=True).reshape(R, 1)
        num = _dot(s.astype(BF16), vb) + inter * qC
        den = jnp.sum(s, axis=-1, keepdims=True) + inter * qn
        hh = num / jnp.maximum(jnp.abs(den), jnp.exp(-m_t))

        mu = jnp.mean(hh, axis=-1, keepdims=True)
        hc = hh - mu
        hn = hc * lax.rsqrt(jnp.mean(hc * hc, axis=-1, keepdims=True) + LN_EPS)
        ya = _sigmoid(o) * hn * ng_ref[:, hs]
        ya_ref[:, :, h * DH:(h + 1) * DH] = ya.reshape(G, tl, DH).astype(BF16)

        kw3 = (k * wk_all[:, h:h + 1]).reshape(G, tl, DH)
        dec = dec3[:, :, h:h + 1]
        dC = jnp.einsum('gtd,gte->gde', kw3.astype(BF16), vb.reshape(G, tl, DH),
                        preferred_element_type=F32)
        C_ref[:, h] = dec * Ch + dC
        n_ref[:, h:h + 1, :] = dec * nh + jnp.sum(kw3, axis=1, keepdims=True)

    m_ref[0] = mnew3.reshape(G, 128)


def _mlstm(proj, C0, n0, m0, cv0, ng, cw, G, tl, lv):
    nb, lp, _ = proj.shape
    NC = lp // tl
    seq4 = lambda i, c: (i, 0, 0, 0)
    seq3 = lambda i, c: (i, 0, 0)
    par = lambda i, c: (0, 0)
    return pl.pallas_call(
        functools.partial(_mlstm_body, G=G, tl=tl, NC=NC, lv=lv),
        grid=(nb // G, NC),
        in_specs=[pl.BlockSpec((G, tl, 4 * D_MIX), lambda i, c: (i, c, OFF_QKVO // (4 * D_MIX))),
                  pl.BlockSpec((G, tl, 3 * D_MIX), lambda i, c: (i, c, OFF_BCH // (3 * D_MIX))),
                  pl.BlockSpec((G, tl, 128), lambda i, c: (i, c, OFF_SA // 128)),
                  pl.BlockSpec((G, tl, 128), lambda i, c: (i, c, OFF_SB // 128)),
                  pl.BlockSpec((G, N_HEADS, DH, DH), seq4),
                  pl.BlockSpec((G, N_HEADS, DH), seq3),
                  pl.BlockSpec((1, G, 128), seq3),
                  pl.BlockSpec((G, HIST, D_MIX), seq3),
                  pl.BlockSpec((1, D_MIX), par),
                  pl.BlockSpec((8, D_MIX), par)],
        out_specs=[pl.BlockSpec((G, tl, D_MIX), lambda i, c: (i, c, 0)),
                   pl.BlockSpec((G, tl, D_MIX), lambda i, c: (i, c, 0)),
                   pl.BlockSpec((G, N_HEADS, DH, DH), seq4),
                   pl.BlockSpec((G, N_HEADS, DH), seq3),
                   pl.BlockSpec((1, G, 128), seq3),
                   pl.BlockSpec((G, CONV_B - 1, D_MIX), seq3)],
        out_shape=[jax.ShapeDtypeStruct((nb, lp, D_MIX), BF16),
                   jax.ShapeDtypeStruct((nb, lp, D_MIX), BF16),
                   jax.ShapeDtypeStruct((nb, N_HEADS, DH, DH), F32),
                   jax.ShapeDtypeStruct((nb, N_HEADS, DH), F32),
                   jax.ShapeDtypeStruct((nb // G, G, 128), F32),
                   jax.ShapeDtypeStruct((nb, CONV_B - 1, D_MIX), F32)],
        scratch_shapes=[pltpu.VMEM((G, HIST + tl, D_MIX), F32)],
        compiler_params=pltpu.CompilerParams(dimension_semantics=("arbitrary", "arbitrary"),
                                             vmem_limit_bytes=VMEM_LIMIT),
        name=f"mlstm_g{G}",
    )(proj, proj, proj, proj, C0, n0, m0, cv0, ng, cw)


def _heads(x, nb, width):
    return jnp.stack([x[:, :, h * width:(h + 1) * width] for h in range(N_HEADS)],
                     axis=1).reshape(nb * N_HEADS, RB, width)


def _gate_cols(x, nb):
    return jnp.stack([x[:, :, 4 + h:5 + h] for h in range(N_HEADS)], axis=1).reshape(nb * N_HEADS, RB, 1)


def _gate_rows(x, nb):
    xt = jnp.swapaxes(x, 1, 2)
    return jnp.stack([xt[:, 4 + h:5 + h, :] for h in range(N_HEADS)], axis=1).reshape(nb * N_HEADS, 1, RB)


def _bmm(a, b):
    return jnp.einsum('nts,nsu->ntu', a.astype(BF16), b.astype(BF16), preferred_element_type=F32)


def _bmm_nt(a, b):
    return jnp.einsum('ntd,nsd->nts', a, b, preferred_element_type=F32)


def _unit_lower_inverse(n, rr, cc, tl):
    eye = (rr == cc).astype(F32)[None]
    p = jnp.where(((rr // INV_BLOCK) == (cc // INV_BLOCK))[None], n, 0.0)
    x = eye + p
    b = 2
    while b < INV_BLOCK:
        p = _bmm(p, p)
        x = x + _bmm(x, p)
        b *= 2
    b = INV_BLOCK
    while b < tl:
        off = jnp.where((((rr // (2 * b)) == (cc // (2 * b))) & ((rr // b) != (cc // b)))[None], n, 0.0)
        x = x + _bmm(x, _bmm(off, x))
        b *= 2
    return x


def _gdn_prep_body(x_ref, prev_ref, sa_ref, sb_ref, gc0_ref, cw_ref, alog_ref, dtb_ref,
                   u_ref, w_ref, qg_ref, kd_ref, qkm_ref, eg_ref, gcs_ref, xp_s, *, NB, G, tl, tlx, NCS, lv):
    cs = pl.program_id(1)
    lvl = lv - (NCS - 1) * tlx

    if NCS > 1:
        @pl.when(cs == 0)
        def _():
            xp_s[:, 0:HIST, :] = gc0_ref[...]

        @pl.when(cs > 0)
        def _():
            xp_s[:, 0:HIST, :] = prev_ref[...]
    else:
        xp_s[:, 0:HIST, :] = gc0_ref[...]
    xp_s[:, HIST:HIST + tlx, :] = x_ref[...]
    qkv = _silu(_conv_taps(xp_s, cw_ref, CONV_C, tlx)).reshape(NB, RB, 3 * D_MIX)

    @pl.when(cs == NCS - 1)
    def _():
        gcs_ref[...] = xp_s[:, pl.ds(HIST + lvl - (CONV_C - 1), CONV_C - 1), :]

    beta_all = _sigmoid(sa_ref[...].reshape(NB, RB, 128))
    g_all = -jnp.exp(alog_ref[...]) * _softplus(sb_ref[...].reshape(NB, RB, 128) + dtb_ref[...])
    if lv < NCS * tlx:
        assert NCS == 1
        valid = (lax.broadcasted_iota(jnp.int32, (NB, RB, 128), 1) % tl) < lv
        beta_all = jnp.where(valid, beta_all, 0.0)
        g_all = jnp.where(valid, g_all, 0.0)
    rr = lax.broadcasted_iota(jnp.int32, (RB, RB), 0)
    cc = lax.broadcasted_iota(jnp.int32, (RB, RB), 1)
    incl = rr >= cc
    if tl < RB:
        incl = incl & ((rr // tl) == (cc // tl))
    diag = rr == cc
    tril = jnp.broadcast_to(incl.astype(BF16)[None], (NB, RB, RB))
    hi, mid, lo = _split3(g_all)
    gam = _bmm(tril, hi) + _bmm(tril, mid) + _bmm(tril, lo)
    glast = gam.reshape(NB * G, tl, 128)[:, tl - 1:tl, :]
    glast_rows = jnp.broadcast_to(glast, (NB * G, tl, 128)).reshape(NB, RB, 128)
    egam = jnp.exp(gam)
    gcol = _gate_cols(gam, NB)
    bcol = _gate_cols(beta_all, NB)
    egcol = _gate_cols(egam, NB)
    kdcol = _gate_cols(jnp.exp(glast_rows - gam), NB)

    q = _heads(qkv[:, :, 0:D_MIX], NB, DH)
    k = _heads(qkv[:, :, D_MIX:2 * D_MIX], NB, DH)
    v = _heads(qkv[:, :, 2 * D_MIX:3 * D_MIX], NB, DH)
    q = q * lax.rsqrt(jnp.sum(q * q, axis=-1, keepdims=True) + NORM_EPS) * (DH ** -0.5)
    k = k * lax.rsqrt(jnp.sum(k * k, axis=-1, keepdims=True) + NORM_EPS)
    qb, kb = q.astype(BF16), k.astype(BF16)

    dmat = jnp.exp(jnp.where(incl[None], gcol - _gate_rows(gam, NB), NEG))
    nmat = jnp.where(diag[None], 0.0, -(bcol * _bmm_nt(kb, kb) * dmat))
    rhs = jnp.concatenate([bcol * v, (bcol * egcol) * k], axis=-1)
    sol = _bmm(_unit_lower_inverse(nmat, rr, cc, tl), rhs)
    qkm = (_bmm_nt(qb, kb) * dmat).astype(BF16).reshape(NB, N_HEADS, RB, RB)

    def put(ref, val):
        val4 = val.reshape(NB, N_HEADS, RB, DH)
        for h in range(N_HEADS):
            ref[:, :, h] = val4[:, h].reshape(NB, G, tl, DH)

    put(u_ref, sol[:, :, 0:DH])
    put(w_ref, sol[:, :, DH:2 * DH].astype(BF16))
    put(qg_ref, (q * egcol).astype(BF16))
    put(kd_ref, (k * kdcol).astype(BF16))
    eg = jnp.exp(glast).reshape(NB, G, 1, 128)
    for h in range(N_HEADS):
        eg_ref[:, :, h] = jnp.broadcast_to(eg[:, :, :, 4 + h:5 + h], (NB, G, 1, 128))
        for g in range(G):
            qkm_ref[:, g, h] = qkm[:, h, g * tl:(g + 1) * tl, g * tl:(g + 1) * tl]


def _gdn_prep(proj, gc0, cw, alog, dtb, NB, G, tl, lv):
    nb, lp, _ = proj.shape
    assert tl % INV_BLOCK == 0 and (tl // INV_BLOCK) & (tl // INV_BLOCK - 1) == 0 and G * tl == RB
    if G == 1:
        gx, tlx = 1, NB * RB
    else:
        assert lp == tl
        gx, tlx = NB * G, tl
    NI, NCS = nb // gx, lp // tlx
    nbt = nb * lp // RB
    step = lambda i, c: (i * NCS + c, 0, 0, 0, 0)
    par = lambda i, c: (0, 0)
    chain = lambda last, dt: jax.ShapeDtypeStruct((nbt, G, N_HEADS, tl, last), dt)
    cspec = lambda last: pl.BlockSpec((NB, G, N_HEADS, tl, last), step)
    return pl.pallas_call(
        functools.partial(_gdn_prep_body, NB=NB, G=G, tl=tl, tlx=tlx, NCS=NCS, lv=lv),
        grid=(NI, NCS),
        in_specs=[pl.BlockSpec((gx, tlx, 3 * D_MIX), lambda i, c: (i, c, OFF_QKVC // (3 * D_MIX))),
                  pl.BlockSpec((gx, HIST, 3 * D_MIX),
                               lambda i, c: (i, jnp.maximum(c * (tlx // HIST) - 1, 0), OFF_QKVC // (3 * D_MIX))),
                  pl.BlockSpec((gx, tlx, 128), lambda i, c: (i, c, OFF_SA // 128)),
                  pl.BlockSpec((gx, tlx, 128), lambda i, c: (i, c, OFF_SB // 128)),
                  pl.BlockSpec((gx, HIST, 3 * D_MIX), lambda i, c: (i, 0, 0)),
                  pl.BlockSpec((8, 3 * D_MIX), par),
                  pl.BlockSpec((1, 128), par),
                  pl.BlockSpec((1, 128), par)],
        out_specs=[cspec(DH), cspec(DH), cspec(DH), cspec(DH), cspec(tl),
                   pl.BlockSpec((NB, G, N_HEADS, 1, 128), step),
                   pl.BlockSpec((gx, CONV_C - 1, 3 * D_MIX), lambda i, c: (i, 0, 0))],
        out_shape=[chain(DH, F32), chain(DH, BF16), chain(DH, BF16), chain(DH, BF16), chain(tl, BF16),
                   jax.ShapeDtypeStruct((nbt, G, N_HEADS, 1, 128), F32),
                   jax.ShapeDtypeStruct((nb, CONV_C - 1, 3 * D_MIX), F32)],
        scratch_shapes=[pltpu.VMEM((gx, HIST + tlx, 3 * D_MIX), F32)],
        compiler_params=pltpu.CompilerParams(dimension_semantics=("arbitrary", "arbitrary"),
                                             vmem_limit_bytes=VMEM_LIMIT),
        name=f"gdn_prep_g{G}",
    )(proj, proj, proj, proj, gc0, cw, alog, dtb)


def _gdn_scan_body(u_ref, w_ref, qg_ref, kd_ref, qkm_ref, eg_ref, z_ref, S0_ref, gng_ref, yc_ref, S_ref,
                   *, NS, tl):
    c = pl.program_id(1)
    n = NS * N_HEADS

    @pl.when(c == 0)
    def _():
        S_ref[...] = S0_ref[...]

    S = S_ref[...].reshape(n, DH, DH)
    Sb = S.astype(BF16)
    chains = lambda ref: ref[...].reshape(n, tl, ref.shape[-1])
    v_new = chains(u_ref) - jnp.einsum('ntd,nde->nte', chains(w_ref), Sb, preferred_element_type=F32)
    vnb = v_new.astype(BF16)
    o = (jnp.einsum('ntd,nde->nte', chains(qg_ref), Sb, preferred_element_type=F32)
         + jnp.einsum('nts,nse->nte', chains(qkm_ref), vnb, preferred_element_type=F32))
    eg = eg_ref[...].reshape(n, 1, 128)[:, :, 0:1]
    S_new = eg * S + jnp.einsum('ntd,nte->nde', chains(kd_ref), vnb, preferred_element_type=F32)
    S_ref[...] = S_new.reshape(NS, N_HEADS, DH, DH)

    on = (o * lax.rsqrt(jnp.mean(o * o, axis=-1, keepdims=True) + NORM_EPS) * gng_ref[...]).reshape(NS, N_HEADS, tl, DH)
    for h in range(N_HEADS):
        yc_ref[:, :, h * DH:(h + 1) * DH] = (on[:, h] * _silu(z_ref[:, :, h * DH:(h + 1) * DH])).astype(BF16)


def _gdn_scan(pre, proj, S0, gng, SB, G, tl):
    nb, lp, _ = proj.shape
    NS = SB * G
    NI, NC = nb // NS, lp // tl
    six = lambda a: a.reshape((NI * SB, NC) + a.shape[1:])
    cspec = lambda last: pl.BlockSpec((SB, 1, G, N_HEADS, tl, last), lambda i, c: (i, c, 0, 0, 0, 0))
    u, w, qg, kd, qkm, eg = (six(a) for a in pre)
    return pl.pallas_call(
        functools.partial(_gdn_scan_body, NS=NS, tl=tl),
        grid=(NI, NC),
        in_specs=[cspec(DH), cspec(DH), cspec(DH), cspec(DH), cspec(tl),
                  pl.BlockSpec((SB, 1, G, N_HEADS, 1, 128), lambda i, c: (i, c, 0, 0, 0, 0)),
                  pl.BlockSpec((NS, tl, D_MIX), lambda i, c: (i, c, OFF_Z // D_MIX)),
                  pl.BlockSpec((NS, N_HEADS, DH, DH), lambda i, c: (i, 0, 0, 0)),
                  pl.BlockSpec((1, DH), lambda i, c: (0, 0))],
        out_specs=[pl.BlockSpec((NS, tl, D_MIX), lambda i, c: (i, c, 0)),
                   pl.BlockSpec((NS, N_HEADS, DH, DH), lambda i, c: (i, 0, 0, 0))],
        out_shape=[jax.ShapeDtypeStruct((nb, lp, D_MIX), BF16),
                   jax.ShapeDtypeStruct((nb, N_HEADS, DH, DH), F32)],
        compiler_params=pltpu.CompilerParams(dimension_semantics=("arbitrary", "arbitrary"),
                                             vmem_limit_bytes=VMEM_LIMIT),
        name=f"gdn_scan_g{G}",
    )(u, w, qg, kd, qkm, eg, proj, S0, gng)


def _route(rl):
    lane = lax.broadcasted_iota(jnp.int32, rl.shape, 1).astype(F32)
    is_g = (lane >= N_EXPERTS) & (lane < N_EXPERTS + N_GROUPS)
    gl = jnp.where(is_g, rl, NEG)
    gmax = jnp.max(gl, axis=-1, keepdims=True)
    grp = jnp.min(jnp.where(gl == gmax, lane - N_EXPERTS, 4.0 * N_EXPERTS), axis=-1, keepdims=True)
    p_grp = 1.0 / jnp.sum(jnp.where(is_g, jnp.exp(gl - gmax), 0.0), axis=-1, keepdims=True)
    lo = grp * EXPERTS_PER_GROUP
    in_grp = (lane >= lo) & (lane < lo + EXPERTS_PER_GROUP)
    el = jnp.where(in_grp, rl, NEG)
    m1 = jnp.max(el, axis=-1, keepdims=True)
    i1 = jnp.min(jnp.where(el == m1, lane, 4.0 * N_EXPERTS), axis=-1, keepdims=True)
    el2 = jnp.where(lane == i1, NEG, el)
    m2 = jnp.max(el2, axis=-1, keepdims=True)
    i2 = jnp.min(jnp.where(el2 == m2, lane, 4.0 * N_EXPERTS), axis=-1, keepdims=True)
    e2 = jnp.exp(m2 - m1)
    w1 = p_grp / (1.0 + e2)
    w2 = p_grp * e2 / (1.0 + e2)
    return i1, i2, w1, w2


def _merge_body(ya_ref, yb_ref, yc_ref, ga_ref, gb_ref, gc_ref, x_ref, mod_ref, wa_ref, wb_ref, wc_ref, wo_ref,
                lg_ref, lb_ref, wrh_ref, wrl_ref, br_ref, x1_ref, rinfo_ref, cnt_ref, *, bt, lt):
    tm = bt * lt

    @pl.when((pl.program_id(0) == 0) & (pl.program_id(1) == 0))
    def _():
        cnt_ref[...] = jnp.zeros_like(cnt_ref)

    def r2(ref):
        return ref[...].reshape(tm, ref.shape[-1])

    merged = (_sigmoid(r2(ga_ref)) * _dot(r2(ya_ref), wa_ref[0])
              + _sigmoid(r2(gb_ref)) * _dot(r2(yb_ref), wb_ref[0])
              + _sigmoid(r2(gc_ref)) * _dot(r2(yc_ref), wc_ref[0]))
    out = _dot(merged.astype(BF16), wo_ref[0])
    y = DN_ALPHA * x_ref[...] + (1.0 + mod_ref[:, 2:3, :]) * out.reshape(bt, lt, D_MODEL)
    x1 = _layer_norm(y, lg_ref[0], lb_ref[0])
    x1_ref[...] = x1
    u2 = (x1 * (1.0 + mod_ref[:, 4:5, :]) + mod_ref[:, 3:4, :]).reshape(tm, D_MODEL)
    hi = u2.astype(BF16)
    lo = (u2 - hi.astype(F32)).astype(BF16)
    rl = _dot(hi, wrh_ref[0]) + _dot(lo, wrh_ref[0]) + _dot(hi, wrl_ref[0]) + br_ref[0]
    i1, i2, w1, w2 = _route(rl)
    lane = lax.broadcasted_iota(jnp.int32, (tm, 128), 1).astype(F32)
    onehot = jnp.where((lane == i1) | (lane == i2), 1.0, 0.0)
    rr = lax.broadcasted_iota(jnp.int32, (tm, tm), 0)
    cc = lax.broadcasted_iota(jnp.int32, (tm, tm), 1)
    before = _dot((rr > cc).astype(BF16), onehot.astype(BF16)) + cnt_ref[0:1, :]
    rank1 = jnp.sum(jnp.where(lane == i1, before, 0.0), axis=-1, keepdims=True)
    rank2 = jnp.sum(jnp.where(lane == i2, before, 0.0), axis=-1, keepdims=True)
    cnt_ref[0:1, :] += jnp.sum(onehot, axis=0, keepdims=True)
    rinfo = jnp.zeros((tm, 128), F32)
    for k, val in enumerate((i1, i2, w1, w2, rank1, rank2)):
        rinfo = jnp.where(lane == k, val, rinfo)
    rinfo_ref[...] = rinfo.reshape(bt, lt, 128)


def _merge(ya, yb, yc, proj, x, mod, wa, wb, wc, wo, lg, lb, wrh, wrl, br, layer, bt, lt):
    nb, lp, _ = x.shape
    tok = lambda i, t: (i, t, 0)
    wsp = lambda shape: pl.BlockSpec((1,) + shape, lambda i, t: (layer, 0, 0))
    g0 = OFF_GTS // D_MODEL
    return pl.pallas_call(
        functools.partial(_merge_body, bt=bt, lt=lt),
        grid=(nb // bt, lp // lt),
        in_specs=[pl.BlockSpec((bt, lt, D_MIX), tok),
                  pl.BlockSpec((bt, lt, D_MIX), tok),
                  pl.BlockSpec((bt, lt, D_MIX), tok),
                  pl.BlockSpec((bt, lt, D_MODEL), lambda i, t: (i, t, g0)),
                  pl.BlockSpec((bt, lt, D_MODEL), lambda i, t: (i, t, g0 + 1)),
                  pl.BlockSpec((bt, lt, D_MODEL), lambda i, t: (i, t, g0 + 2)),
                  pl.BlockSpec((bt, lt, D_MODEL), tok),
                  pl.BlockSpec((bt, 6, D_MODEL), lambda i, t: (i, 0, 0)),
                  wsp((D_MIX, D_MODEL)), wsp((D_MIX, D_MODEL)), wsp((D_MIX, D_MODEL)), wsp((D_MODEL, D_MODEL)),
                  wsp((1, D_MODEL)), wsp((1, D_MODEL)),
                  wsp((D_MODEL, 128)), wsp((D_MODEL, 128)), wsp((1, 128))],
        out_specs=[pl.BlockSpec((bt, lt, D_MODEL), tok),
                   pl.BlockSpec((bt, lt, 128), tok),
                   pl.BlockSpec((8, 128), lambda i, t: (0, 0))],
        out_shape=[jax.ShapeDtypeStruct((nb, lp, D_MODEL), F32),
                   jax.ShapeDtypeStruct((nb, lp, 128), F32),
                   jax.ShapeDtypeStruct((8, 128), F32)],
        compiler_params=pltpu.CompilerParams(dimension_semantics=("arbitrary", "arbitrary"),
                                             vmem_limit_bytes=VMEM_LIMIT),
        name=f"merge_l{layer}_b{bt}",
    )(ya, yb, yc, proj, proj, proj, x, mod, wa, wb, wc, wo, lg, lb, wrh, wrl, br)


def _moe_plan(rinfo, cnt, tm, mb):
    nb, lp, _ = rinfo.shape
    n_tok = nb * lp
    n_blocks = 2 * n_tok // mb + N_EXPERTS
    counts = cnt[0, :N_EXPERTS].astype(jnp.int32)
    nblk = (counts + mb - 1) // mb
    pend = jnp.cumsum(nblk)
    pstart = pend - nblk
    expert = rinfo[..., 0:2].astype(jnp.int32)
    rank = rinfo[..., 4:6].astype(jnp.int32)
    ids = jnp.arange(N_EXPERTS, dtype=jnp.int32)
    first_row = jnp.sum(jnp.where(expert[..., None] == ids, pstart * mb, 0), axis=-1)
    pos = (first_row + rank).reshape(n_tok // tm, 1, 2 * tm)
    block_e = jnp.sum(pend[None, :] <= jnp.arange(n_blocks, dtype=jnp.int32)[:, None], axis=1)
    block_e = jnp.minimum(block_e, N_EXPERTS - 1).astype(jnp.int32)
    n_used = pend[N_EXPERTS - 1:].astype(jnp.int32)
    ztail = jnp.where(nblk > 0, (pend - 1) * mb, -1).astype(jnp.int32)
    ztail = jnp.concatenate([ztail, n_used]).reshape(1, 1, N_EXPERTS + 1)
    return pos, block_e, n_used, ztail, n_blocks


def _row_copy(src, dst, sem):
    return pltpu.make_async_copy(src, dst, sem)


def _dispatch_body(pos_ref, ztail_ref, x1_ref, mod_ref, disp_ref, u_scr, z_scr, sem, zsem, *, bt, lt, n_blocks, mb):
    tm = bt * lt

    @pl.when((pl.program_id(0) == 0) & (pl.program_id(1) == 0))
    def _():
        z_scr[...] = jnp.zeros_like(z_scr)
        for e in range(N_EXPERTS):
            @pl.when(ztail_ref[0, 0, e] >= 0)
            def _():
                row = pl.multiple_of(ztail_ref[0, 0, e], mb)
                _row_copy(z_scr, disp_ref.at[pl.ds(row, mb)], zsem).start()
        def unused(j):
            return _row_copy(z_scr, disp_ref.at[pl.ds(pl.multiple_of(j * mb, mb), mb)], zsem)

        n_used = ztail_ref[0, 0, N_EXPERTS]
        lax.fori_loop(n_used, n_blocks, lambda j, c: (unused(j).start(), c)[1], 0)
        lax.fori_loop(n_used, n_blocks, lambda j, c: (unused(j).wait(), c)[1], 0)
        for e in range(N_EXPERTS):
            @pl.when(ztail_ref[0, 0, e] >= 0)
            def _():
                row = pl.multiple_of(ztail_ref[0, 0, e], mb)
                _row_copy(z_scr, disp_ref.at[pl.ds(row, mb)], zsem).wait()

    u2 = x1_ref[...] * (1.0 + mod_ref[:, 4:5, :]) + mod_ref[:, 3:4, :]
    u_scr[...] = u2.reshape(tm, D_MODEL)

    def issue(t, carry):
        for k in range(2):
            _row_copy(u_scr.at[pl.ds(t, 1)], disp_ref.at[pl.ds(pos_ref[0, 0, 2 * t + k], 1)], sem).start()
        return carry

    lax.fori_loop(0, tm, issue, 0, unroll=8)
    for k in range(2):
        _row_copy(u_scr, disp_ref.at[pl.ds(0, tm)], sem).wait()


def _dispatch(x1, mod, pos, ztail, n_blocks, mb, bt, lt):
    nb, lp, _ = x1.shape
    nt = lp // lt
    return pl.pallas_call(
        functools.partial(_dispatch_body, bt=bt, lt=lt, n_blocks=n_blocks, mb=mb),
        grid=(nb // bt, nt),
        in_specs=[pl.BlockSpec((1, 1, 2 * bt * lt), lambda i, t: (i * nt + t, 0, 0), memory_space=pltpu.SMEM),
                  pl.BlockSpec((1, 1, N_EXPERTS + 1), lambda i, t: (0, 0, 0), memory_space=pltpu.SMEM),
                  pl.BlockSpec((bt, lt, D_MODEL), lambda i, t: (i, t, 0)),
                  pl.BlockSpec((bt, 6, D_MODEL), lambda i, t: (i, 0, 0))],
        out_specs=pl.BlockSpec(memory_space=pl.ANY),
        out_shape=jax.ShapeDtypeStruct((n_blocks * mb, D_MODEL), F32),
        scratch_shapes=[pltpu.VMEM((bt * lt, D_MODEL), F32), pltpu.VMEM((mb, D_MODEL), F32),
                        pltpu.SemaphoreType.DMA, pltpu.SemaphoreType.DMA],
        compiler_params=pltpu.CompilerParams(dimension_semantics=("arbitrary", "arbitrary"),
                                             vmem_limit_bytes=VMEM_LIMIT),
        name=f"dispatch_b{bt}",
    )(pos, ztail, x1, mod)


def _experts_body(be_ref, nu_ref, x_ref, wg_ref, wu_ref, wd_ref, o_ref):
    j = pl.program_id(0)

    @pl.when(j < nu_ref[0])
    def _():
        x = x_ref[...].astype(BF16)
        hb = _silu(_dot(x, wg_ref[0, 0].astype(BF16))) * _dot(x, wu_ref[0, 0].astype(BF16))
        o_ref[...] = _dot(hb.astype(BF16), wd_ref[0, 0].astype(BF16))

    @pl.when(j >= nu_ref[0])
    def _():
        o_ref[...] = jnp.zeros_like(o_ref)


def _experts(disp, block_e, n_used, wg, wu, wd, layer, mb):
    n_blocks = disp.shape[0] // mb
    wmap = lambda j, be, nu: (layer, be[j], 0, 0)
    return pl.pallas_call(
        _experts_body,
        grid_spec=pltpu.PrefetchScalarGridSpec(
            num_scalar_prefetch=2,
            grid=(n_blocks,),
            in_specs=[pl.BlockSpec((mb, D_MODEL), lambda j, be, nu: (jnp.minimum(j, nu[0] - 1), 0)),
                      pl.BlockSpec((1, 1, D_MODEL, D_EXPERT), wmap),
                      pl.BlockSpec((1, 1, D_MODEL, D_EXPERT), wmap),
                      pl.BlockSpec((1, 1, D_EXPERT, D_MODEL), wmap)],
            out_specs=pl.BlockSpec((mb, D_MODEL), lambda j, be, nu: (j, 0))),
        out_shape=jax.ShapeDtypeStruct(disp.shape, F32),
        compiler_params=pltpu.CompilerParams(dimension_semantics=("arbitrary",), vmem_limit_bytes=VMEM_LIMIT),
        name=f"experts_l{layer}_n{n_blocks}",
    )(block_e, n_used, disp, wg, wu, wd)


def _combine_body(pos_ref, x1_ref, mod_ref, rinfo_ref, eo_ref, lg_ref, lb_ref, x2_ref, r_scr, sem, *, bt, lt):
    tm = bt * lt

    def issue(t, carry):
        for k in range(2):
            _row_copy(eo_ref.at[pl.ds(pos_ref[0, 0, 2 * t + k], 1)], r_scr.at[k, pl.ds(t, 1)], sem).start()
        return carry

    lax.fori_loop(0, tm, issue, 0, unroll=8)
    for k in range(2):
        _row_copy(eo_ref.at[pl.ds(0, tm)], r_scr.at[k], sem).wait()

    rinfo = rinfo_ref[...].reshape(tm, 128)
    moe = r_scr[0] * rinfo[:, 2:3] + r_scr[1] * rinfo[:, 3:4]
    y = DN_ALPHA * x1_ref[...] + (1.0 + mod_ref[:, 5:6, :]) * moe.reshape(bt, lt, D_MODEL)
    x2_ref[...] = _layer_norm(y, lg_ref[0], lb_ref[0])


def _combine(x1, mod, rinfo, eo, pos, lg, lb, layer, bt, lt):
    nb, lp, _ = x1.shape
    nt = lp // lt
    tok = lambda i, t: (i, t, 0)
    return pl.pallas_call(
        functools.partial(_combine_body, bt=bt, lt=lt),
        grid=(nb // bt, nt),
        in_specs=[pl.BlockSpec((1, 1, 2 * bt * lt), lambda i, t: (i * nt + t, 0, 0), memory_space=pltpu.SMEM),
                  pl.BlockSpec((bt, lt, D_MODEL), tok),
                  pl.BlockSpec((bt, 6, D_MODEL), lambda i, t: (i, 0, 0)),
                  pl.BlockSpec((bt, lt, 128), tok),
                  pl.BlockSpec(memory_space=pl.ANY),
                  pl.BlockSpec((1, 1, D_MODEL), lambda i, t: (layer, 0, 0)),
                  pl.BlockSpec((1, 1, D_MODEL), lambda i, t: (layer, 0, 0))],
        out_specs=pl.BlockSpec((bt, lt, D_MODEL), tok),
        out_shape=jax.ShapeDtypeStruct((nb, lp, D_MODEL), F32),
        scratch_shapes=[pltpu.VMEM((2, bt * lt, D_MODEL), F32), pltpu.SemaphoreType.DMA],
        compiler_params=pltpu.CompilerParams(dimension_semantics=("arbitrary", "arbitrary"),
                                             vmem_limit_bytes=VMEM_LIMIT),
        name=f"combine_l{layer}_b{bt}",
    )(pos, x1, mod, rinfo, eo, lg, lb)


def _hist(state):
    return jnp.pad(state, ((0, 0), (HIST - state.shape[1], 0), (0, 0)))


def _trunk(x, mod, st, p, bt, lt, G, tl, lv, gdn_cfg, moe_block):
    nb = x.shape[0]
    new = {key: [] for key in ('C', 'n', 'm', 'conv', 'S', 'gconv')}
    for l in range(DEPTH):
        proj, x = _inproj(x, mod[l], p['ln_in_g'], p['ln_in_b'], p['w_in_r'], p['b_in_r'], l, l == 0, bt, lt)
        m0 = jnp.pad(st['m'][l], ((0, 0), (0, 128 - N_HEADS))).reshape(nb // G, G, 128)
        ya, yb, C, n, m, conv = _mlstm(proj, st['C'][l], st['n'][l], m0, _hist(st['conv'][l]),
                                       p['mlstm_norm_g'][l:l + 1], p['conv_b_w8'][l], G, tl, lv)
        gd = gdn_cfg
        *pre, gconv = _gdn_prep(proj, _hist(st['gconv'][l]), p['conv_c_w8'][l], p['alog_row'][l], p['dtb_row'][l],
                                gd['NB'], gd['G'], gd['tl'], lv)
        yc, S = _gdn_scan(pre, proj, st['S'][l], p['gdn_norm_g'][l:l + 1], gd['SB'], gd['G'], gd['tl'])
        x1, rinfo, cnt = _merge(ya, yb, yc, proj, x, mod[l], p['w_br_a'], p['w_br_b'], p['w_br_c'], p['w_out'],
                                p['ln1_g'], p['ln1_b'], p['wr_hi'], p['wr_lo'], p['br'], l, bt, lt)
        pos, block_e, n_used, ztail, n_blocks = _moe_plan(rinfo, cnt, bt * lt, moe_block)
        disp = _dispatch(x1, mod[l], pos, ztail, n_blocks, moe_block, bt, lt)
        eo = _experts(disp, block_e, n_used, p['exp_w_gate'], p['exp_w_up'], p['exp_w_down'], l, moe_block)
        x = _combine(x1, mod[l], rinfo, eo, pos, p['ln2_g'], p['ln2_b'], l, bt, lt)
        new['C'].append(C)
        new['n'].append(n)
        new['m'].append(m.reshape(nb, 128)[:, :N_HEADS])
        new['conv'].append(conv)
        new['S'].append(S)
        new['gconv'].append(gconv)
    return x, {key: jnp.stack(val) for key, val in new.items()}


def kernel(x_prompt, x_sample, state_mlstm_C, state_mlstm_n, state_mlstm_m, state_conv, state_gdn_S, state_gdn_conv, c_prompt, c_sample, ln_in_g, ln_in_b, w_ada, b_ada, w_in, b_in, mlstm_norm_g, conv_b_w, conv_c_w, gdn_a_log, gdn_dt_bias, gdn_norm_g, w_br_a, w_br_b, w_br_c, w_out, ln1_g, ln1_b, router_g_w, router_g_b, router_e_w, router_e_b, exp_w_gate, exp_w_up, exp_w_down, ln2_g, ln2_b):
    nbp, lp, _ = x_prompt.shape
    nbs, ls, _ = x_sample.shape
    lsp = 8

    def regroup(a):
        parts = [a[..., 3592:5128], a[..., 2056:3592], a[..., 5648:8720], a[..., 0:2048], a[..., 5128:5640],
                 a[..., 2048:2052], a[..., 5640:5644], jnp.zeros(a.shape[:-1] + (120,), a.dtype),
                 a[..., 2052:2056], a[..., 5644:5648], jnp.zeros(a.shape[:-1] + (120,), a.dtype)]
        return jnp.concatenate(parts, axis=-1)

    wr = jnp.concatenate([router_e_w, router_g_w, jnp.zeros((DEPTH, D_MODEL, 128 - N_EXPERTS - N_GROUPS), F32)], axis=-1)
    wr_hi = wr.astype(BF16)
    lane_pad = lambda a: jnp.pad(a, ((0, 0), (4, 128 - 4 - N_HEADS)))[:, None, :]
    p = dict(
        ln_in_g=ln_in_g.reshape(1, D_MODEL), ln_in_b=ln_in_b.reshape(1, D_MODEL),
        w_in_r=regroup(w_in).astype(BF16), b_in_r=regroup(b_in).reshape(DEPTH, 1, N_PROJ),
        mlstm_norm_g=mlstm_norm_g,
        conv_b_w8=jnp.pad(conv_b_w, ((0, 0), (0, 8 - CONV_B), (0, 0))),
        conv_c_w8=jnp.pad(conv_c_w, ((0, 0), (0, 8 - CONV_C), (0, 0))),
        alog_row=lane_pad(gdn_a_log), dtb_row=lane_pad(gdn_dt_bias), gdn_norm_g=gdn_norm_g,
        w_br_a=w_br_a.astype(BF16), w_br_b=w_br_b.astype(BF16), w_br_c=w_br_c.astype(BF16),
        w_out=w_out.astype(BF16),
        ln1_g=ln1_g.reshape(DEPTH, 1, D_MODEL), ln1_b=ln1_b.reshape(DEPTH, 1, D_MODEL),
        wr_hi=wr_hi, wr_lo=(wr - wr_hi.astype(F32)).astype(BF16),
        br=jnp.concatenate([router_e_b, router_g_b, jnp.zeros((DEPTH, 128 - N_EXPERTS - N_GROUPS), F32)],
                           axis=-1).reshape(DEPTH, 1, 128),
        exp_w_gate=exp_w_gate, exp_w_up=exp_w_up, exp_w_down=exp_w_down,
        ln2_g=ln2_g.reshape(DEPTH, 1, D_MODEL), ln2_b=ln2_b.reshape(DEPTH, 1, D_MODEL),
    )

    mod = _ada(jnp.concatenate([c_prompt, c_sample], axis=0), w_ada, b_ada)
    mod = mod.reshape(DEPTH, nbp + nbs, 6, D_MODEL)

    zeros = lambda *s: jnp.zeros((DEPTH, nbp) + s, F32)
    st_p = {'C': zeros(N_HEADS, DH, DH), 'n': zeros(N_HEADS, DH), 'm': zeros(N_HEADS),
            'conv': zeros(CONV_B - 1, D_MIX), 'S': zeros(N_HEADS, DH, DH), 'gconv': zeros(CONV_C - 1, 3 * D_MIX)}
    y_p, sp = _trunk(x_prompt, mod[:, :nbp], st_p, p, bt=1, lt=512, G=1, tl=64, lv=lp,
                     gdn_cfg=dict(NB=2, G=1, tl=RB, SB=nbp), moe_block=256)

    st_s = {'C': state_mlstm_C, 'n': state_mlstm_n, 'm': state_mlstm_m, 'conv': state_conv,
            'S': state_gdn_S, 'gconv': state_gdn_conv}
    xs = jnp.pad(x_sample, ((0, 0), (0, lsp - ls), (0, 0)))
    y_s, ss = _trunk(xs, mod[:, nbp:], st_s, p, bt=64, lt=lsp, G=16, tl=lsp, lv=ls,
                     gdn_cfg=dict(NB=2, G=RB // lsp, tl=lsp, SB=1), moe_block=128)
    y_s = y_s[:, :ls]

    return (y_p, y_s, sp['C'], sp['n'], sp['m'], sp['conv'], sp['S'], sp['gconv'],
            ss['C'], ss['n'], ss['m'], ss['conv'], ss['S'], ss['gconv'])
```

```python
import functools

import jax
import jax.numpy as jnp
from jax import lax
from jax.experimental import pallas as pl
from jax.experimental.pallas import tpu as pltpu

F32 = jnp.float32
BF16 = jnp.bfloat16

D_MODEL = 1024
DEPTH = 2
N_HEADS = 4
DH = 128
D_MIX = N_HEADS * DH
N_EXPERTS = 32
EXPERTS_PER_GROUP = 8
N_GROUPS = 4
D_EXPERT = 256
CONV_B = 3
CONV_C = 4
HIST = 8
RB = 64
INV_BLOCK = 8
DN_ALPHA = (2 * DEPTH) ** 0.25
LN_EPS = 1e-5
NORM_EPS = 1e-6
NEG = -1e30

OFF_QKVC = 0
OFF_BCH = 1536
OFF_GTS = 3072
OFF_QKVO = 6144
OFF_Z = 8192
OFF_SA = 8704
OFF_SB = 8832
N_PROJ = 8960
TN_PROJ = 1280

VMEM_LIMIT = 52 * 1024 * 1024


def _dot(a, b):
    return jnp.dot(a, b, preferred_element_type=F32)


def _dot_nt(a, b):
    return lax.dot_general(a, b, (((1,), (1,)), ((), ())), preferred_element_type=F32)


def _split3(x):
    hi = x.astype(BF16)
    r = x - hi.astype(F32)
    mid = r.astype(BF16)
    lo = (r - mid.astype(F32)).astype(BF16)
    return hi, mid, lo


def _dot3(m, x):
    hi, mid, lo = _split3(x)
    return _dot(m, hi) + _dot(m, mid) + _dot(m, lo)


def _dotb(a, b):
    return _dot(a.astype(BF16), b.astype(BF16))


def _layer_norm(x, g, b):
    mu = jnp.mean(x, axis=-1, keepdims=True)
    xc = x - mu
    var = jnp.mean(xc * xc, axis=-1, keepdims=True)
    return xc * lax.rsqrt(var + LN_EPS) * g + b


def _sigmoid(x):
    return jax.nn.sigmoid(x)


def _silu(x):
    return x * jax.nn.sigmoid(x)


def _log_sigmoid(x):
    return jnp.minimum(x, 0.0) - jnp.log1p(jnp.exp(-jnp.abs(x)))


def _softplus(x):
    return jnp.maximum(x, 0.0) + jnp.log1p(jnp.exp(-jnp.abs(x)))


def _seq_rows(x3, tl):
    g, _, c = x3.shape
    return jnp.broadcast_to(x3, (g, tl, c)).reshape(g * tl, c)


def _ada_body(c_ref, w_ref, b_ref, o_ref):
    c = c_ref[...]
    s = _silu(c).astype(BF16)
    o_ref[0] = _dot(s, w_ref[0].astype(BF16)) + b_ref[0]


def _ada(c_all, w_ada, b_ada):
    nb = c_all.shape[0]
    return pl.pallas_call(
        _ada_body,
        grid=(DEPTH, 6),
        in_specs=[pl.BlockSpec((nb, D_MODEL), lambda l, j: (0, 0)),
                  pl.BlockSpec((1, D_MODEL, D_MODEL), lambda l, j: (l, 0, j)),
                  pl.BlockSpec((1, 1, D_MODEL), lambda l, j: (l, 0, j))],
        out_specs=pl.BlockSpec((1, nb, D_MODEL), lambda l, j: (l, 0, j)),
        out_shape=jax.ShapeDtypeStruct((DEPTH, nb, 6 * D_MODEL), F32),
        compiler_params=pltpu.CompilerParams(dimension_semantics=("arbitrary", "arbitrary"),
                                             vmem_limit_bytes=VMEM_LIMIT),
        name="ada",
    )(c_all, w_ada, b_ada.reshape(DEPTH, 1, 6 * D_MODEL))


def _inproj_body(x_ref, mod_ref, g_ref, b_ref, w_ref, bias_ref, proj_ref, *rest, apply_ln, bt, lt):
    x = x_ref[...]
    if apply_ln:
        x = _layer_norm(x, g_ref[...], b_ref[...])
        rest[0][...] = x
    u = (x * (1.0 + mod_ref[:, 1:2, :]) + mod_ref[:, 0:1, :]).reshape(bt * lt, D_MODEL).astype(BF16)
    for j in range(N_PROJ // TN_PROJ):
        cs = slice(j * TN_PROJ, (j + 1) * TN_PROJ)
        proj_ref[:, :, cs] = (_dot(u, w_ref[0, :, cs]) + bias_ref[0, :, cs]).reshape(bt, lt, TN_PROJ)


def _inproj(x, mod, ln_g, ln_b, w_r, b_r, layer, apply_ln, bt, lt):
    nb, lp, _ = x.shape
    tok = lambda i, t: (i, t, 0)
    out_shape = [jax.ShapeDtypeStruct((nb, lp, N_PROJ), F32)]
    out_specs = [pl.BlockSpec((bt, lt, N_PROJ), tok)]
    if apply_ln:
        out_shape.append(jax.ShapeDtypeStruct((nb, lp, D_MODEL), F32))
        out_specs.append(pl.BlockSpec((bt, lt, D_MODEL), tok))
    res = pl.pallas_call(
        functools.partial(_inproj_body, apply_ln=apply_ln, bt=bt, lt=lt),
        grid=(nb // bt, lp // lt),
        in_specs=[pl.BlockSpec((bt, lt, D_MODEL), tok),
                  pl.BlockSpec((bt, 6, D_MODEL), lambda i, t: (i, 0, 0)),
                  pl.BlockSpec((1, D_MODEL), lambda i, t: (0, 0)),
                  pl.BlockSpec((1, D_MODEL), lambda i, t: (0, 0)),
                  pl.BlockSpec((1, D_MODEL, N_PROJ), lambda i, t: (layer, 0, 0), pipeline_mode=pl.Buffered(1)),
                  pl.BlockSpec((1, 1, N_PROJ), lambda i, t: (layer, 0, 0))],
        out_specs=out_specs,
        out_shape=out_shape,
        compiler_params=pltpu.CompilerParams(dimension_semantics=("arbitrary", "arbitrary"),
                                             vmem_limit_bytes=VMEM_LIMIT),
        name=f"inproj_l{layer}_b{bt}",
    )(x, mod, ln_g, ln_b, w_r, b_r)
    return (res[0], res[1]) if apply_ln else (res[0], x)


def _chunk_masks(G, tl):
    R = G * tl
    rr = lax.broadcasted_iota(jnp.int32, (R, R), 0)
    cc = lax.broadcasted_iota(jnp.int32, (R, R), 1)
    incl = rr >= cc
    if G > 1:
        incl = incl & ((rr // tl) == (cc // tl))
    diag = rr == cc
    return incl, diag


def _valid_rows(G, tl, c, lv):
    R = G * tl
    row = lax.broadcasted_iota(jnp.int32, (R, 128), 0)
    pos = (row % tl if G > 1 else row) + c * tl
    return pos < lv


def _conv_taps(xp_s, w_ref, width, tl):
    acc = None
    for j in range(width):
        tap = xp_s[:, pl.ds(HIST - (width - 1) + j, tl), :] * w_ref[j:j + 1, :].reshape(1, 1, -1)
        acc = tap if acc is None else acc + tap
    return acc


def _mlstm_body(qkvo_ref, bch_ref, sa_ref, sb_ref, C0_ref, n0_ref, m0_ref, cv0_ref, ng_ref, cw_ref,
                ya_ref, yb_ref, C_ref, n_ref, m_ref, cv_ref, xp_s, *, G, tl, NC, lv):
    c = pl.program_id(1)
    R = G * tl
    padded = lv < NC * tl
    lvl = lv - (NC - 1) * tl

    @pl.when(c == 0)
    def _():
        C_ref[...] = C0_ref[...]
        n_ref[...] = n0_ref[...]
        m_ref[...] = m0_ref[...]
        xp_s[:, 0:HIST, :] = cv0_ref[...]

    xin = bch_ref[:, :, D_MIX:2 * D_MIX] * bch_ref[:, :, 2 * D_MIX:3 * D_MIX]
    xp_s[:, HIST:HIST + tl, :] = xin
    conv = _conv_taps(xp_s, cw_ref, CONV_B, tl)
    yb_ref[...] = (bch_ref[:, :, 0:D_MIX] * conv).astype(BF16)

    @pl.when(c == NC - 1)
    def _():
        cv_ref[...] = xp_s[:, pl.ds(HIST + lvl - (CONV_B - 1), CONV_B - 1), :]

    if NC > 1:
        xp_s[:, 0:HIST, :] = xp_s[:, tl:tl + HIST, :]

    i_all = sa_ref[...].reshape(R, 128)
    f_all = _log_sigmoid(sb_ref[...].reshape(R, 128))
    if padded:
        valid = _valid_rows(G, tl, c, lv)
        i_all = jnp.where(valid, i_all, NEG)
        f_all = jnp.where(valid, f_all, 0.0)
    incl, _ = _chunk_masks(G, tl)
    bcum = _dot3(incl.astype(BF16), f_all)
    b3 = bcum.reshape(G, tl, 128)
    blast3 = b3[:, tl - 1:tl, :]
    blast = _seq_rows(blast3, tl)
    m03 = m_ref[0].reshape(G, 1, 128)
    m0 = _seq_rows(m03, tl)
    val = blast - bcum + i_all
    mnew3 = jnp.maximum(jnp.max(val.reshape(G, tl, 128), axis=1, keepdims=True), blast3 + m03)
    mnew = _seq_rows(mnew3, tl)
    wk_all = jnp.exp(val - mnew)
    dec3 = jnp.exp(blast3 + m03 - mnew3)
    minter_all = bcum + m0
    iT = i_all.T
    bT = bcum.T

    for h in range(N_HEADS):
        hs = slice(h * DH, (h + 1) * DH)
        q = qkvo_ref[:, :, h * DH:(h + 1) * DH].reshape(R, DH)
        k = qkvo_ref[:, :, D_MIX + h * DH:D_MIX + (h + 1) * DH].reshape(R, DH) * (DH ** -0.5)
        v = qkvo_ref[:, :, 2 * D_MIX + h * DH:2 * D_MIX + (h + 1) * DH].reshape(R, DH)
        o = qkvo_ref[:, :, 3 * D_MIX + h * DH:3 * D_MIX + (h + 1) * DH].reshape(R, DH)
        qb, kb, vb = q.astype(BF16), k.astype(BF16), v.astype(BF16)

        b_col = bcum[:, h:h + 1]
        dlog = jnp.where(incl, b_col - bT[h:h + 1, :] + iT[h:h + 1, :], NEG)
        m_inter = minter_all[:, h:h + 1]
        m_t = jnp.maximum(m_inter, jnp.max(dlog, axis=-1, keepdims=True))
        s = _dot_nt(qb, kb) * jnp.exp(dlog - m_t)
        inter = jnp.exp(m_inter - m_t)

        Ch = C_ref[:, h]
        nh = n_ref[:, h:h + 1, :]
        q3 = q.reshape(G, tl, DH)
        qC = jnp.einsum('gtd,gde->gte', q3.astype(BF16), Ch.astype(BF16),
                        preferred_element_type=F32).reshape(R, DH)
        qn = jnp.sum(q3 * nh, axis=-1, keepdims=True).reshape(R, 1)
        num = _dot(s.astype(BF16), vb) + inter * qC
        den = jnp.sum(s, axis=-1, keepdims=True) + inter * qn
        hh = num / jnp.maximum(jnp.abs(den), jnp.exp(-m_t))

        mu = jnp.mean(hh, axis=-1, keepdims=True)
        hc = hh - mu
        hn = hc * lax.rsqrt(jnp.mean(hc * hc, axis=-1, keepdims=True) + LN_EPS)
        ya = _sigmoid(o) * hn * ng_ref[:, hs]
        ya_ref[:, :, h * DH:(h + 1) * DH] = ya.reshape(G, tl, DH).astype(BF16)

        kw3 = (k * wk_all[:, h:h + 1]).reshape(G, tl, DH)
        dec = dec3[:, :, h:h + 1]
        dC = jnp.einsum('gtd,gte->gde', kw3.astype(BF16), vb.reshape(G, tl, DH),
                        preferred_element_type=F32)
        C_ref[:, h] = dec * Ch + dC
        n_ref[:, h:h + 1, :] = dec * nh + jnp.sum(kw3, axis=1, keepdims=True)

    m_ref[0] = mnew3.reshape(G, 128)


def _mlstm(proj, C0, n0, m0, cv0, ng, cw, G, tl, lv):
    nb, lp, _ = proj.shape
    NC = lp // tl
    seq4 = lambda i, c: (i, 0, 0, 0)
    seq3 = lambda i, c: (i, 0, 0)
    par = lambda i, c: (0, 0)
    return pl.pallas_call(
        functools.partial(_mlstm_body, G=G, tl=tl, NC=NC, lv=lv),
        grid=(nb // G, NC),
        in_specs=[pl.BlockSpec((G, tl, 4 * D_MIX), lambda i, c: (i, c, OFF_QKVO // (4 * D_MIX))),
                  pl.BlockSpec((G, tl, 3 * D_MIX), lambda i, c: (i, c, OFF_BCH // (3 * D_MIX))),
                  pl.BlockSpec((G, tl, 128), lambda i, c: (i, c, OFF_SA // 128)),
                  pl.BlockSpec((G, tl, 128), lambda i, c: (i, c, OFF_SB // 128)),
                  pl.BlockSpec((G, N_HEADS, DH, DH), seq4),
                  pl.BlockSpec((G, N_HEADS, DH), seq3),
                  pl.BlockSpec((1, G, 128), seq3),
                  pl.BlockSpec((G, HIST, D_MIX), seq3),
                  pl.BlockSpec((1, D_MIX), par),
                  pl.BlockSpec((8, D_MIX), par)],
        out_specs=[pl.BlockSpec((G, tl, D_MIX), lambda i, c: (i, c, 0)),
                   pl.BlockSpec((G, tl, D_MIX), lambda i, c: (i, c, 0)),
                   pl.BlockSpec((G, N_HEADS, DH, DH), seq4),
                   pl.BlockSpec((G, N_HEADS, DH), seq3),
                   pl.BlockSpec((1, G, 128), seq3),
                   pl.BlockSpec((G, CONV_B - 1, D_MIX), seq3)],
        out_shape=[jax.ShapeDtypeStruct((nb, lp, D_MIX), BF16),
                   jax.ShapeDtypeStruct((nb, lp, D_MIX), BF16),
                   jax.ShapeDtypeStruct((nb, N_HEADS, DH, DH), F32),
                   jax.ShapeDtypeStruct((nb, N_HEADS, DH), F32),
                   jax.ShapeDtypeStruct((nb // G, G, 128), F32),
                   jax.ShapeDtypeStruct((nb, CONV_B - 1, D_MIX), F32)],
        scratch_shapes=[pltpu.VMEM((G, HIST + tl, D_MIX), F32)],
        compiler_params=pltpu.CompilerParams(dimension_semantics=("arbitrary", "arbitrary"),
                                             vmem_limit_bytes=VMEM_LIMIT),
        name=f"mlstm_g{G}",
    )(proj, proj, proj, proj, C0, n0, m0, cv0, ng, cw)


def _heads(x, nb, width):
    return jnp.stack([x[:, :, h * width:(h + 1) * width] for h in range(N_HEADS)],
                     axis=1).reshape(nb * N_HEADS, RB, width)


def _gate_cols(x, nb):
    return jnp.stack([x[:, :, 4 + h:5 + h] for h in range(N_HEADS)], axis=1).reshape(nb * N_HEADS, RB, 1)


def _gate_rows(x, nb):
    xt = jnp.swapaxes(x, 1, 2)
    return jnp.stack([xt[:, 4 + h:5 + h, :] for h in range(N_HEADS)], axis=1).reshape(nb * N_HEADS, 1, RB)


def _bmm(a, b):
    return jnp.einsum('nts,nsu->ntu', a.astype(BF16), b.astype(BF16), preferred_element_type=F32)


def _bmm_nt(a, b):
    return jnp.einsum('ntd,nsd->nts', a, b, preferred_element_type=F32)


def _unit_lower_inverse(n, rr, cc, tl):
    eye = (rr == cc).astype(F32)[None]
    p = jnp.where(((rr // INV_BLOCK) == (cc // INV_BLOCK))[None], n, 0.0)
    x = eye + p
    b = 2
    while b < INV_BLOCK:
        p = _bmm(p, p)
        x = x + _bmm(x, p)
        b *= 2
    b = INV_BLOCK
    while b < tl:
        off = jnp.where((((rr // (2 * b)) == (cc // (2 * b))) & ((rr // b) != (cc // b)))[None], n, 0.0)
        x = x + _bmm(x, _bmm(off, x))
        b *= 2
    return x


def _gdn_prep_body(x_ref, prev_ref, sa_ref, sb_ref, gc0_ref, cw_ref, alog_ref, dtb_ref,
                   u_ref, w_ref, qg_ref, kd_ref, qkm_ref, eg_ref, gcs_ref, xp_s, *, NB, G, tl, tlx, NCS, lv):
    cs = pl.program_id(1)
    lvl = lv - (NCS - 1) * tlx

    if NCS > 1:
        @pl.when(cs == 0)
        def _():
            xp_s[:, 0:HIST, :] = gc0_ref[...]

        @pl.when(cs > 0)
        def _():
            xp_s[:, 0:HIST, :] = prev_ref[...]
    else:
        xp_s[:, 0:HIST, :] = gc0_ref[...]
    xp_s[:, HIST:HIST + tlx, :] = x_ref[...]
    qkv = _silu(_conv_taps(xp_s, cw_ref, CONV_C, tlx)).reshape(NB, RB, 3 * D_MIX)

    @pl.when(cs == NCS - 1)
    def _():
        gcs_ref[...] = xp_s[:, pl.ds(HIST + lvl - (CONV_C - 1), CONV_C - 1), :]

    beta_all = _sigmoid(sa_ref[...].reshape(NB, RB, 128))
    g_all = -jnp.exp(alog_ref[...]) * _softplus(sb_ref[...].reshape(NB, RB, 128) + dtb_ref[...])
    if lv < NCS * tlx:
        assert NCS == 1
        valid = (lax.broadcasted_iota(jnp.int32, (NB, RB, 128), 1) % tl) < lv
        beta_all = jnp.where(valid, beta_all, 0.0)
        g_all = jnp.where(valid, g_all, 0.0)
    rr = lax.broadcasted_iota(jnp.int32, (RB, RB), 0)
    cc = lax.broadcasted_iota(jnp.int32, (RB, RB), 1)
    incl = rr >= cc
    if tl < RB:
        incl = incl & ((rr // tl) == (cc // tl))
    diag = rr == cc
    tril = jnp.broadcast_to(incl.astype(BF16)[None], (NB, RB, RB))
    hi, mid, lo = _split3(g_all)
    gam = _bmm(tril, hi) + _bmm(tril, mid) + _bmm(tril, lo)
    glast = gam.reshape(NB * G, tl, 128)[:, tl - 1:tl, :]
    glast_rows = jnp.broadcast_to(glast, (NB * G, tl, 128)).reshape(NB, RB, 128)
    egam = jnp.exp(gam)
    gcol = _gate_cols(gam, NB)
    bcol = _gate_cols(beta_all, NB)
    egcol = _gate_cols(egam, NB)
    kdcol = _gate_cols(jnp.exp(glast_rows - gam), NB)

    q = _heads(qkv[:, :, 0:D_MIX], NB, DH)
    k = _heads(qkv[:, :, D_MIX:2 * D_MIX], NB, DH)
    v = _heads(qkv[:, :, 2 * D_MIX:3 * D_MIX], NB, DH)
    q = q * lax.rsqrt(jnp.sum(q * q, axis=-1, keepdims=True) + NORM_EPS) * (DH ** -0.5)
    k = k * lax.rsqrt(jnp.sum(k * k, axis=-1, keepdims=True) + NORM_EPS)
    qb, kb = q.astype(BF16), k.astype(BF16)

    dmat = jnp.exp(jnp.where(incl[None], gcol - _gate_rows(gam, NB), NEG))
    nmat = jnp.where(diag[None], 0.0, -(bcol * _bmm_nt(kb, kb) * dmat))
    rhs = jnp.concatenate([bcol * v, (bcol * egcol) * k], axis=-1)
    sol = _bmm(_unit_lower_inverse(nmat, rr, cc, tl), rhs)
    qkm = (_bmm_nt(qb, kb) * dmat).astype(BF16).reshape(NB, N_HEADS, RB, RB)

    def put(ref, val):
        val4 = val.reshape(NB, N_HEADS, RB, DH)
        for h in range(N_HEADS):
            ref[:, :, h] = val4[:, h].reshape(NB, G, tl, DH)

    put(u_ref, sol[:, :, 0:DH])
    put(w_ref, sol[:, :, DH:2 * DH].astype(BF16))
    put(qg_ref, (q * egcol).astype(BF16))
    put(kd_ref, (k * kdcol).astype(BF16))
    eg = jnp.exp(glast).reshape(NB, G, 1, 128)
    for h in range(N_HEADS):
        eg_ref[:, :, h] = jnp.broadcast_to(eg[:, :, :, 4 + h:5 + h], (NB, G, 1, 128))
        for g in range(G):
            qkm_ref[:, g, h] = qkm[:, h, g * tl:(g + 1) * tl, g * tl:(g + 1) * tl]


def _gdn_prep(proj, gc0, cw, alog, dtb, NB, G, tl, lv):
    nb, lp, _ = proj.shape
    assert tl % INV_BLOCK == 0 and (tl // INV_BLOCK) & (tl // INV_BLOCK - 1) == 0 and G * tl == RB
    if G == 1:
        gx, tlx = 1, NB * RB
    else:
        assert lp == tl
        gx, tlx = NB * G, tl
    NI, NCS = nb // gx, lp // tlx
    nbt = nb * lp // RB
    step = lambda i, c: (i * NCS + c, 0, 0, 0, 0)
    par = lambda i, c: (0, 0)
    chain = lambda last, dt: jax.ShapeDtypeStruct((nbt, G, N_HEADS, tl, last), dt)
    cspec = lambda last: pl.BlockSpec((NB, G, N_HEADS, tl, last), step)
    return pl.pallas_call(
        functools.partial(_gdn_prep_body, NB=NB, G=G, tl=tl, tlx=tlx, NCS=NCS, lv=lv),
        grid=(NI, NCS),
        in_specs=[pl.BlockSpec((gx, tlx, 3 * D_MIX), lambda i, c: (i, c, OFF_QKVC // (3 * D_MIX))),
                  pl.BlockSpec((gx, HIST, 3 * D_MIX),
                               lambda i, c: (i, jnp.maximum(c * (tlx // HIST) - 1, 0), OFF_QKVC // (3 * D_MIX))),
                  pl.BlockSpec((gx, tlx, 128), lambda i, c: (i, c, OFF_SA // 128)),
                  pl.BlockSpec((gx, tlx, 128), lambda i, c: (i, c, OFF_SB // 128)),
                  pl.BlockSpec((gx, HIST, 3 * D_MIX), lambda i, c: (i, 0, 0)),
                  pl.BlockSpec((8, 3 * D_MIX), par),
                  pl.BlockSpec((1, 128), par),
                  pl.BlockSpec((1, 128), par)],
        out_specs=[cspec(DH), cspec(DH), cspec(DH), cspec(DH), cspec(tl),
                   pl.BlockSpec((NB, G, N_HEADS, 1, 128), step),
                   pl.BlockSpec((gx, CONV_C - 1, 3 * D_MIX), lambda i, c: (i, 0, 0))],
        out_shape=[chain(DH, F32), chain(DH, BF16), chain(DH, BF16), chain(DH, BF16), chain(tl, BF16),
                   jax.ShapeDtypeStruct((nbt, G, N_HEADS, 1, 128), F32),
                   jax.ShapeDtypeStruct((nb, CONV_C - 1, 3 * D_MIX), F32)],
        scratch_shapes=[pltpu.VMEM((gx, HIST + tlx, 3 * D_MIX), F32)],
        compiler_params=pltpu.CompilerParams(dimension_semantics=("arbitrary", "arbitrary"),
                                             vmem_limit_bytes=VMEM_LIMIT),
        name=f"gdn_prep_g{G}",
    )(proj, proj, proj, proj, gc0, cw, alog, dtb)


def _gdn_scan_body(u_ref, w_ref, qg_ref, kd_ref, qkm_ref, eg_ref, z_ref, S0_ref, gng_ref, yc_ref, S_ref,
                   *, NS, tl):
    c = pl.program_id(1)
    n = NS * N_HEADS

    @pl.when(c == 0)
    def _():
        S_ref[...] = S0_ref[...]

    S = S_ref[...].reshape(n, DH, DH)
    Sb = S.astype(BF16)
    chains = lambda ref: ref[...].reshape(n, tl, ref.shape[-1])
    v_new = chains(u_ref) - jnp.einsum('ntd,nde->nte', chains(w_ref), Sb, preferred_element_type=F32)
    vnb = v_new.astype(BF16)
    o = (jnp.einsum('ntd,nde->nte', chains(qg_ref), Sb, preferred_element_type=F32)
         + jnp.einsum('nts,nse->nte', chains(qkm_ref), vnb, preferred_element_type=F32))
    eg = eg_ref[...].reshape(n, 1, 128)[:, :, 0:1]
    S_new = eg * S + jnp.einsum('ntd,nte->nde', chains(kd_ref), vnb, preferred_element_type=F32)
    S_ref[...] = S_new.reshape(NS, N_HEADS, DH, DH)

    on = (o * lax.rsqrt(jnp.mean(o * o, axis=-1, keepdims=True) + NORM_EPS) * gng_ref[...]).reshape(NS, N_HEADS, tl, DH)
    for h in range(N_HEADS):
        yc_ref[:, :, h * DH:(h + 1) * DH] = (on[:, h] * _silu(z_ref[:, :, h * DH:(h + 1) * DH])).astype(BF16)


def _gdn_scan(pre, proj, S0, gng, SB, G, tl):
    nb, lp, _ = proj.shape
    NS = SB * G
    NI, NC = nb // NS, lp // tl
    six = lambda a: a.reshape((NI * SB, NC) + a.shape[1:])
    cspec = lambda last: pl.BlockSpec((SB, 1, G, N_HEADS, tl, last), lambda i, c: (i, c, 0, 0, 0, 0))
    u, w, qg, kd, qkm, eg = (six(a) for a in pre)
    return pl.pallas_call(
        functools.partial(_gdn_scan_body, NS=NS, tl=tl),
        grid=(NI, NC),
        in_specs=[cspec(DH), cspec(DH), cspec(DH), cspec(DH), cspec(tl),
                  pl.BlockSpec((SB, 1, G, N_HEADS, 1, 128), lambda i, c: (i, c, 0, 0, 0, 0)),
                  pl.BlockSpec((NS, tl, D_MIX), lambda i, c: (i, c, OFF_Z // D_MIX)),
                  pl.BlockSpec((NS, N_HEADS, DH, DH), lambda i, c: (i, 0, 0, 0)),
                  pl.BlockSpec((1, DH), lambda i, c: (0, 0))],
        out_specs=[pl.BlockSpec((NS, tl, D_MIX), lambda i, c: (i, c, 0)),
                   pl.BlockSpec((NS, N_HEADS, DH, DH), lambda i, c: (i, 0, 0, 0))],
        out_shape=[jax.ShapeDtypeStruct((nb, lp, D_MIX), BF16),
                   jax.ShapeDtypeStruct((nb, N_HEADS, DH, DH), F32)],
        compiler_params=pltpu.CompilerParams(dimension_semantics=("arbitrary", "arbitrary"),
                                             vmem_limit_bytes=VMEM_LIMIT),
        name=f"gdn_scan_g{G}",
    )(u, w, qg, kd, qkm, eg, proj, S0, gng)


def _route(rl):
    lane = lax.broadcasted_iota(jnp.int32, rl.shape, 1).astype(F32)
    is_g = (lane >= N_EXPERTS) & (lane < N_EXPERTS + N_GROUPS)
    gl = jnp.where(is_g, rl, NEG)
    gmax = jnp.max(gl, axis=-1, keepdims=True)
    grp = jnp.min(jnp.where(gl == gmax, lane - N_EXPERTS, 4.0 * N_EXPERTS), axis=-1, keepdims=True)
    p_grp = 1.0 / jnp.sum(jnp.where(is_g, jnp.exp(gl - gmax), 0.0), axis=-1, keepdims=True)
    lo = grp * EXPERTS_PER_GROUP
    in_grp = (lane >= lo) & (lane < lo + EXPERTS_PER_GROUP)
    el = jnp.where(in_grp, rl, NEG)
    m1 = jnp.max(el, axis=-1, keepdims=True)
    i1 = jnp.min(jnp.where(el == m1, lane, 4.0 * N_EXPERTS), axis=-1, keepdims=True)
    el2 = jnp.where(lane == i1, NEG, el)
    m2 = jnp.max(el2, axis=-1, keepdims=True)
    i2 = jnp.min(jnp.where(el2 == m2, lane, 4.0 * N_EXPERTS), axis=-1, keepdims=True)
    e2 = jnp.exp(m2 - m1)
    w1 = p_grp / (1.0 + e2)
    w2 = p_grp * e2 / (1.0 + e2)
    return i1, i2, w1, w2


def _merge_body(ya_ref, yb_ref, yc_ref, ga_ref, gb_ref, gc_ref, x_ref, mod_ref, wa_ref, wb_ref, wc_ref, wo_ref,
                lg_ref, lb_ref, wrh_ref, wrl_ref, br_ref, x1_ref, rinfo_ref, cnt_ref, *, bt, lt):
    tm = bt * lt

    @pl.when((pl.program_id(0) == 0) & (pl.program_id(1) == 0))
    def _():
        cnt_ref[...] = jnp.zeros_like(cnt_ref)

    def r2(ref):
        return ref[...].reshape(tm, ref.shape[-1])

    merged = (_sigmoid(r2(ga_ref)) * _dot(r2(ya_ref), wa_ref[0])
              + _sigmoid(r2(gb_ref)) * _dot(r2(yb_ref), wb_ref[0])
              + _sigmoid(r2(gc_ref)) * _dot(r2(yc_ref), wc_ref[0]))
    out = _dot(merged.astype(BF16), wo_ref[0])
    y = DN_ALPHA * x_ref[...] + (1.0 + mod_ref[:, 2:3, :]) * out.reshape(bt, lt, D_MODEL)
    x1 = _layer_norm(y, lg_ref[0], lb_ref[0])
    x1_ref[...] = x1
    u2 = (x1 * (1.0 + mod_ref[:, 4:5, :]) + mod_ref[:, 3:4, :]).reshape(tm, D_MODEL)
    hi = u2.astype(BF16)
    lo = (u2 - hi.astype(F32)).astype(BF16)
    rl = _dot(hi, wrh_ref[0]) + _dot(lo, wrh_ref[0]) + _dot(hi, wrl_ref[0]) + br_ref[0]
    i1, i2, w1, w2 = _route(rl)
    lane = lax.broadcasted_iota(jnp.int32, (tm, 128), 1).astype(F32)
    onehot = jnp.where((lane == i1) | (lane == i2), 1.0, 0.0)
    rr = lax.broadcasted_iota(jnp.int32, (tm, tm), 0)
    cc = lax.broadcasted_iota(jnp.int32, (tm, tm), 1)
    before = _dot((rr > cc).astype(BF16), onehot.astype(BF16)) + cnt_ref[0:1, :]
    rank1 = jnp.sum(jnp.where(lane == i1, before, 0.0), axis=-1, keepdims=True)
    rank2 = jnp.sum(jnp.where(lane == i2, before, 0.0), axis=-1, keepdims=True)
    cnt_ref[0:1, :] += jnp.sum(onehot, axis=0, keepdims=True)
    rinfo = jnp.zeros((tm, 128), F32)
    for k, val in enumerate((i1, i2, w1, w2, rank1, rank2)):
        rinfo = jnp.where(lane == k, val, rinfo)
    rinfo_ref[...] = rinfo.reshape(bt, lt, 128)


def _merge(ya, yb, yc, proj, x, mod, wa, wb, wc, wo, lg, lb, wrh, wrl, br, layer, bt, lt):
    nb, lp, _ = x.shape
    tok = lambda i, t: (i, t, 0)
    wsp = lambda shape: pl.BlockSpec((1,) + shape, lambda i, t: (layer, 0, 0))
    g0 = OFF_GTS // D_MODEL
    return pl.pallas_call(
        functools.partial(_merge_body, bt=bt, lt=lt),
        grid=(nb // bt, lp // lt),
        in_specs=[pl.BlockSpec((bt, lt, D_MIX), tok),
                  pl.BlockSpec((bt, lt, D_MIX), tok),
                  pl.BlockSpec((bt, lt, D_MIX), tok),
                  pl.BlockSpec((bt, lt, D_MODEL), lambda i, t: (i, t, g0)),
                  pl.BlockSpec((bt, lt, D_MODEL), lambda i, t: (i, t, g0 + 1)),
                  pl.BlockSpec((bt, lt, D_MODEL), lambda i, t: (i, t, g0 + 2)),
                  pl.BlockSpec((bt, lt, D_MODEL), tok),
                  pl.BlockSpec((bt, 6, D_MODEL), lambda i, t: (i, 0, 0)),
                  wsp((D_MIX, D_MODEL)), wsp((D_MIX, D_MODEL)), wsp((D_MIX, D_MODEL)), wsp((D_MODEL, D_MODEL)),
                  wsp((1, D_MODEL)), wsp((1, D_MODEL)),
                  wsp((D_MODEL, 128)), wsp((D_MODEL, 128)), wsp((1, 128))],
        out_specs=[pl.BlockSpec((bt, lt, D_MODEL), tok),
                   pl.BlockSpec((bt, lt, 128), tok),
                   pl.BlockSpec((8, 128), lambda i, t: (0, 0))],
        out_shape=[jax.ShapeDtypeStruct((nb, lp, D_MODEL), F32),
                   jax.ShapeDtypeStruct((nb, lp, 128), F32),
                   jax.ShapeDtypeStruct((8, 128), F32)],
        compiler_params=pltpu.CompilerParams(dimension_semantics=("arbitrary", "arbitrary"),
                                             vmem_limit_bytes=VMEM_LIMIT),
        name=f"merge_l{layer}_b{bt}",
    )(ya, yb, yc, proj, proj, proj, x, mod, wa, wb, wc, wo, lg, lb, wrh, wrl, br)


def _moe_plan(rinfo, cnt, tm, mb):
    nb, lp, _ = rinfo.shape
    n_tok = nb * lp
    n_blocks = 2 * n_tok // mb + N_EXPERTS
    counts = cnt[0, :N_EXPERTS].astype(jnp.int32)
    nblk = (counts + mb - 1) // mb
    pend = jnp.cumsum(nblk)
    pstart = pend - nblk
    expert = rinfo[..., 0:2].astype(jnp.int32)
    rank = rinfo[..., 4:6].astype(jnp.int32)
    ids = jnp.arange(N_EXPERTS, dtype=jnp.int32)
    first_row = jnp.sum(jnp.where(expert[..., None] == ids, pstart * mb, 0), axis=-1)
    pos = (first_row + rank).reshape(n_tok // tm, 1, 2 * tm)
    block_e = jnp.sum(pend[None, :] <= jnp.arange(n_blocks, dtype=jnp.int32)[:, None], axis=1)
    block_e = jnp.minimum(block_e, N_EXPERTS - 1).astype(jnp.int32)
    n_used = pend[N_EXPERTS - 1:].astype(jnp.int32)
    ztail = jnp.where(nblk > 0, (pend - 1) * mb, -1).astype(jnp.int32)
    ztail = jnp.concatenate([ztail, n_used]).reshape(1, 1, N_EXPERTS + 1)
    return pos, block_e, n_used, ztail, n_blocks


def _row_copy(src, dst, sem):
    return pltpu.make_async_copy(src, dst, sem)


def _dispatch_body(pos_ref, ztail_ref, x1_ref, mod_ref, disp_ref, u_scr, z_scr, sem, zsem, *, bt, lt, n_blocks, mb):
    tm = bt * lt

    @pl.when((pl.program_id(0) == 0) & (pl.program_id(1) == 0))
    def _():
        z_scr[...] = jnp.zeros_like(z_scr)
        for e in range(N_EXPERTS):
            @pl.when(ztail_ref[0, 0, e] >= 0)
            def _():
                row = pl.multiple_of(ztail_ref[0, 0, e], mb)
                _row_copy(z_scr, disp_ref.at[pl.ds(row, mb)], zsem).start()
        def unused(j):
            return _row_copy(z_scr, disp_ref.at[pl.ds(pl.multiple_of(j * mb, mb), mb)], zsem)

        n_used = ztail_ref[0, 0, N_EXPERTS]
        lax.fori_loop(n_used, n_blocks, lambda j, c: (unused(j).start(), c)[1], 0)
        lax.fori_loop(n_used, n_blocks, lambda j, c: (unused(j).wait(), c)[1], 0)
        for e in range(N_EXPERTS):
            @pl.when(ztail_ref[0, 0, e] >= 0)
            def _():
                row = pl.multiple_of(ztail_ref[0, 0, e], mb)
                _row_copy(z_scr, disp_ref.at[pl.ds(row, mb)], zsem).wait()

    u2 = x1_ref[...] * (1.0 + mod_ref[:, 4:5, :]) + mod_ref[:, 3:4, :]
    u_scr[...] = u2.reshape(tm, D_MODEL)

    def issue(t, carry):
        for k in range(2):
            _row_copy(u_scr.at[pl.ds(t, 1)], disp_ref.at[pl.ds(pos_ref[0, 0, 2 * t + k], 1)], sem).start()
        return carry

    lax.fori_loop(0, tm, issue, 0, unroll=8)
    for k in range(2):
        _row_copy(u_scr, disp_ref.at[pl.ds(0, tm)], sem).wait()


def _dispatch(x1, mod, pos, ztail, n_blocks, mb, bt, lt):
    nb, lp, _ = x1.shape
    nt = lp // lt
    return pl.pallas_call(
        functools.partial(_dispatch_body, bt=bt, lt=lt, n_blocks=n_blocks, mb=mb),
        grid=(nb // bt, nt),
        in_specs=[pl.BlockSpec((1, 1, 2 * bt * lt), lambda i, t: (i * nt + t, 0, 0), memory_space=pltpu.SMEM),
                  pl.BlockSpec((1, 1, N_EXPERTS + 1), lambda i, t: (0, 0, 0), memory_space=pltpu.SMEM),
                  pl.BlockSpec((bt, lt, D_MODEL), lambda i, t: (i, t, 0)),
                  pl.BlockSpec((bt, 6, D_MODEL), lambda i, t: (i, 0, 0))],
        out_specs=pl.BlockSpec(memory_space=pl.ANY),
        out_shape=jax.ShapeDtypeStruct((n_blocks * mb, D_MODEL), F32),
        scratch_shapes=[pltpu.VMEM((bt * lt, D_MODEL), F32), pltpu.VMEM((mb, D_MODEL), F32),
                        pltpu.SemaphoreType.DMA, pltpu.SemaphoreType.DMA],
        compiler_params=pltpu.CompilerParams(dimension_semantics=("arbitrary", "arbitrary"),
                                             vmem_limit_bytes=VMEM_LIMIT),
        name=f"dispatch_b{bt}",
    )(pos, ztail, x1, mod)


def _experts_body(be_ref, nu_ref, x_ref, wg_ref, wu_ref, wd_ref, o_ref):
    j = pl.program_id(0)

    @pl.when(j < nu_ref[0])
    def _():
        x = x_ref[...].astype(BF16)
        hb = _silu(_dot(x, wg_ref[0, 0].astype(BF16))) * _dot(x, wu_ref[0, 0].astype(BF16))
        o_ref[...] = _dot(hb.astype(BF16), wd_ref[0, 0].astype(BF16))

    @pl.when(j >= nu_ref[0])
    def _():
        o_ref[...] = jnp.zeros_like(o_ref)


def _experts(disp, block_e, n_used, wg, wu, wd, layer, mb):
    n_blocks = disp.shape[0] // mb
    wmap = lambda j, be, nu: (layer, be[j], 0, 0)
    return pl.pallas_call(
        _experts_body,
        grid_spec=pltpu.PrefetchScalarGridSpec(
            num_scalar_prefetch=2,
            grid=(n_blocks,),
            in_specs=[pl.BlockSpec((mb, D_MODEL), lambda j, be, nu: (jnp.minimum(j, nu[0] - 1), 0)),
                      pl.BlockSpec((1, 1, D_MODEL, D_EXPERT), wmap),
                      pl.BlockSpec((1, 1, D_MODEL, D_EXPERT), wmap),
                      pl.BlockSpec((1, 1, D_EXPERT, D_MODEL), wmap)],
            out_specs=pl.BlockSpec((mb, D_MODEL), lambda j, be, nu: (j, 0))),
        out_shape=jax.ShapeDtypeStruct(disp.shape, F32),
        compiler_params=pltpu.CompilerParams(dimension_semantics=("arbitrary",), vmem_limit_bytes=VMEM_LIMIT),
        name=f"experts_l{layer}_n{n_blocks}",
    )(block_e, n_used, disp, wg, wu, wd)


def _combine_body(pos_ref, x1_ref, mod_ref, rinfo_ref, eo_ref, lg_ref, lb_ref, x2_ref, r_scr, sem, *, bt, lt):
    tm = bt * lt

    def issue(t, carry):
        for k in range(2):
            _row_copy(eo_ref.at[pl.ds(pos_ref[0, 0, 2 * t + k], 1)], r_scr.at[k, pl.ds(t, 1)], sem).start()
        return carry

    lax.fori_loop(0, tm, issue, 0, unroll=8)
    for k in range(2):
        _row_copy(eo_ref.at[pl.ds(0, tm)], r_scr.at[k], sem).wait()

    rinfo = rinfo_ref[...].reshape(tm, 128)
    moe = r_scr[0] * rinfo[:, 2:3] + r_scr[1] * rinfo[:, 3:4]
    y = DN_ALPHA * x1_ref[...] + (1.0 + mod_ref[:, 5:6, :]) * moe.reshape(bt, lt, D_MODEL)
    x2_ref[...] = _layer_norm(y, lg_ref[0], lb_ref[0])


def _combine(x1, mod, rinfo, eo, pos, lg, lb, layer, bt, lt):
    nb, lp, _ = x1.shape
    nt = lp // lt
    tok = lambda i, t: (i, t, 0)
    return pl.pallas_call(
        functools.partial(_combine_body, bt=bt, lt=lt),
        grid=(nb // bt, nt),
        in_specs=[pl.BlockSpec((1, 1, 2 * bt * lt), lambda i, t: (i * nt + t, 0, 0), memory_space=pltpu.SMEM),
                  pl.BlockSpec((bt, lt, D_MODEL), tok),
                  pl.BlockSpec((bt, 6, D_MODEL), lambda i, t: (i, 0, 0)),
                  pl.BlockSpec((bt, lt, 128), tok),
                  pl.BlockSpec(memory_space=pl.ANY),
                  pl.BlockSpec((1, 1, D_MODEL), lambda i, t: (layer, 0, 0)),
                  pl.BlockSpec((1, 1, D_MODEL), lambda i, t: (layer, 0, 0))],
        out_specs=pl.BlockSpec((bt, lt, D_MODEL), tok),
        out_shape=jax.ShapeDtypeStruct((nb, lp, D_MODEL), F32),
        scratch_shapes=[pltpu.VMEM((2, bt * lt, D_MODEL), F32), pltpu.SemaphoreType.DMA],
        compiler_params=pltpu.CompilerParams(dimension_semantics=("arbitrary", "arbitrary"),
                                             vmem_limit_bytes=VMEM_LIMIT),
        name=f"combine_l{layer}_b{bt}",
    )(pos, x1, mod, rinfo, eo, lg, lb)


def _hist(state):
    return jnp.pad(state, ((0, 0), (HIST - state.shape[1], 0), (0, 0)))


def _trunk(x, mod, st, p, bt, lt, G, tl, lv, gdn_cfg, moe_block, ip_tile):
    nb = x.shape[0]
    new = {key: [] for key in ('C', 'n', 'm', 'conv', 'S', 'gconv')}
    for l in range(DEPTH):
        proj, x = _inproj(x, mod[l], p['ln_in_g'], p['ln_in_b'], p['w_in_r'], p['b_in_r'], l, l == 0, *ip_tile)
        m0 = jnp.pad(st['m'][l], ((0, 0), (0, 128 - N_HEADS))).reshape(nb // G, G, 128)
        ya, yb, C, n, m, conv = _mlstm(proj, st['C'][l], st['n'][l], m0, _hist(st['conv'][l]),
                                       p['mlstm_norm_g'][l:l + 1], p['conv_b_w8'][l], G, tl, lv)
        gd = gdn_cfg
        *pre, gconv = _gdn_prep(proj, _hist(st['gconv'][l]), p['conv_c_w8'][l], p['alog_row'][l], p['dtb_row'][l],
                                gd['NB'], gd['G'], gd['tl'], lv)
        yc, S = _gdn_scan(pre, proj, st['S'][l], p['gdn_norm_g'][l:l + 1], gd['SB'], gd['G'], gd['tl'])
        x1, rinfo, cnt = _merge(ya, yb, yc, proj, x, mod[l], p['w_br_a'], p['w_br_b'], p['w_br_c'], p['w_out'],
                                p['ln1_g'], p['ln1_b'], p['wr_hi'], p['wr_lo'], p['br'], l, bt, lt)
        pos, block_e, n_used, ztail, n_blocks = _moe_plan(rinfo, cnt, bt * lt, moe_block)
        disp = _dispatch(x1, mod[l], pos, ztail, n_blocks, moe_block, bt, lt)
        eo = _experts(disp, block_e, n_used, p['exp_w_gate'], p['exp_w_up'], p['exp_w_down'], l, moe_block)
        x = _combine(x1, mod[l], rinfo, eo, pos, p['ln2_g'], p['ln2_b'], l, bt, lt)
        new['C'].append(C)
        new['n'].append(n)
        new['m'].append(m.reshape(nb, 128)[:, :N_HEADS])
        new['conv'].append(conv)
        new['S'].append(S)
        new['gconv'].append(gconv)
    return x, {key: jnp.stack(val) for key, val in new.items()}


def kernel(x_prompt, x_sample, state_mlstm_C, state_mlstm_n, state_mlstm_m, state_conv, state_gdn_S, state_gdn_conv, c_prompt, c_sample, ln_in_g, ln_in_b, w_ada, b_ada, w_in, b_in, mlstm_norm_g, conv_b_w, conv_c_w, gdn_a_log, gdn_dt_bias, gdn_norm_g, w_br_a, w_br_b, w_br_c, w_out, ln1_g, ln1_b, router_g_w, router_g_b, router_e_w, router_e_b, exp_w_gate, exp_w_up, exp_w_down, ln2_g, ln2_b):
    nbp, lp, _ = x_prompt.shape
    nbs, ls, _ = x_sample.shape
    lsp = 8

    def regroup(a):
        parts = [a[..., 3592:5128], a[..., 2056:3592], a[..., 5648:8720], a[..., 0:2048], a[..., 5128:5640],
                 a[..., 2048:2052], a[..., 5640:5644], jnp.zeros(a.shape[:-1] + (120,), a.dtype),
                 a[..., 2052:2056], a[..., 5644:5648], jnp.zeros(a.shape[:-1] + (120,), a.dtype)]
        return jnp.concatenate(parts, axis=-1)

    wr = jnp.concatenate([router_e_w, router_g_w, jnp.zeros((DEPTH, D_MODEL, 128 - N_EXPERTS - N_GROUPS), F32)], axis=-1)
    wr_hi = wr.astype(BF16)
    lane_pad = lambda a: jnp.pad(a, ((0, 0), (4, 128 - 4 - N_HEADS)))[:, None, :]
    p = dict(
        ln_in_g=ln_in_g.reshape(1, D_MODEL), ln_in_b=ln_in_b.reshape(1, D_MODEL),
        w_in_r=regroup(w_in).astype(BF16), b_in_r=regroup(b_in).reshape(DEPTH, 1, N_PROJ),
        mlstm_norm_g=mlstm_norm_g,
        conv_b_w8=jnp.pad(conv_b_w, ((0, 0), (0, 8 - CONV_B), (0, 0))),
        conv_c_w8=jnp.pad(conv_c_w, ((0, 0), (0, 8 - CONV_C), (0, 0))),
        alog_row=lane_pad(gdn_a_log), dtb_row=lane_pad(gdn_dt_bias), gdn_norm_g=gdn_norm_g,
        w_br_a=w_br_a.astype(BF16), w_br_b=w_br_b.astype(BF16), w_br_c=w_br_c.astype(BF16),
        w_out=w_out.astype(BF16),
        ln1_g=ln1_g.reshape(DEPTH, 1, D_MODEL), ln1_b=ln1_b.reshape(DEPTH, 1, D_MODEL),
        wr_hi=wr_hi, wr_lo=(wr - wr_hi.astype(F32)).astype(BF16),
        br=jnp.concatenate([router_e_b, router_g_b, jnp.zeros((DEPTH, 128 - N_EXPERTS - N_GROUPS), F32)],
                           axis=-1).reshape(DEPTH, 1, 128),
        exp_w_gate=exp_w_gate, exp_w_up=exp_w_up, exp_w_down=exp_w_down,
        ln2_g=ln2_g.reshape(DEPTH, 1, D_MODEL), ln2_b=ln2_b.reshape(DEPTH, 1, D_MODEL),
    )

    mod = _ada(jnp.concatenate([c_prompt, c_sample], axis=0), w_ada, b_ada)
    mod = mod.reshape(DEPTH, nbp + nbs, 6, D_MODEL)

    zeros = lambda *s: jnp.zeros((DEPTH, nbp) + s, F32)
    st_p = {'C': zeros(N_HEADS, DH, DH), 'n': zeros(N_HEADS, DH), 'm': zeros(N_HEADS),
            'conv': zeros(CONV_B - 1, D_MIX), 'S': zeros(N_HEADS, DH, DH), 'gconv': zeros(CONV_C - 1, 3 * D_MIX)}
    y_p, sp = _trunk(x_prompt, mod[:, :nbp], st_p, p, bt=1, lt=512, G=1, tl=64, lv=lp,
                     gdn_cfg=dict(NB=4, G=1, tl=RB, SB=nbp), moe_block=256, ip_tile=(1, 256))

    st_s = {'C': state_mlstm_C, 'n': state_mlstm_n, 'm': state_mlstm_m, 'conv': state_conv,
            'S': state_gdn_S, 'gconv': state_gdn_conv}
    xs = jnp.pad(x_sample, ((0, 0), (0, lsp - ls), (0, 0)))
    y_s, ss = _trunk(xs, mod[:, nbp:], st_s, p, bt=64, lt=lsp, G=16, tl=lsp, lv=ls,
                     gdn_cfg=dict(NB=2, G=RB // lsp, tl=lsp, SB=1), moe_block=128, ip_tile=(32, lsp))
    y_s = y_s[:, :ls]

    return (y_p, y_s, sp['C'], sp['n'], sp['m'], sp['conv'], sp['S'], sp['gconv'],
            ss['C'], ss['n'], ss['m'], ss['conv'], ss['S'], ss['gconv'])
```

```python
import functools

import jax
import jax.numpy as jnp
from jax import lax
from jax.experimental import pallas as pl
from jax.experimental.pallas import tpu as pltpu

F32 = jnp.float32
BF16 = jnp.bfloat16

D_MODEL = 1024
DEPTH = 2
N_HEADS = 4
DH = 128
D_MIX = N_HEADS * DH
N_EXPERTS = 32
EXPERTS_PER_GROUP = 8
N_GROUPS = 4
D_EXPERT = 256
CONV_B = 3
CONV_C = 4
HIST = 8
RB = 64
INV_BLOCK = 8
DN_ALPHA = (2 * DEPTH) ** 0.25
LN_EPS = 1e-5
NORM_EPS = 1e-6
NEG = -1e30

OFF_QKVC = 0
OFF_BCH = 1536
OFF_GTS = 3072
OFF_QKVO = 6144
OFF_Z = 8192
OFF_SA = 8704
OFF_SB = 8832
N_PROJ = 8960
TN_PROJ = 1280

VMEM_LIMIT = 52 * 1024 * 1024


def _dot(a, b):
    return jnp.dot(a, b, preferred_element_type=F32)


def _split3(x):
    hi = x.astype(BF16)
    r = x - hi.astype(F32)
    mid = r.astype(BF16)
    lo = (r - mid.astype(F32)).astype(BF16)
    return hi, mid, lo


def _layer_norm(x, g, b):
    mu = jnp.mean(x, axis=-1, keepdims=True)
    xc = x - mu
    var = jnp.mean(xc * xc, axis=-1, keepdims=True)
    return xc * lax.rsqrt(var + LN_EPS) * g + b


def _sigmoid(x):
    return jax.nn.sigmoid(x)


def _silu(x):
    return x * jax.nn.sigmoid(x)


def _log_sigmoid(x):
    return jnp.minimum(x, 0.0) - jnp.log1p(jnp.exp(-jnp.abs(x)))


def _softplus(x):
    return jnp.maximum(x, 0.0) + jnp.log1p(jnp.exp(-jnp.abs(x)))


def _ada_body(c_ref, w_ref, b_ref, o_ref):
    c = c_ref[...]
    s = _silu(c).astype(BF16)
    o_ref[0] = _dot(s, w_ref[0].astype(BF16)) + b_ref[0]


def _ada(c_all, w_ada, b_ada):
    nb = c_all.shape[0]
    return pl.pallas_call(
        _ada_body,
        grid=(DEPTH, 6),
        in_specs=[pl.BlockSpec((nb, D_MODEL), lambda l, j: (0, 0)),
                  pl.BlockSpec((1, D_MODEL, D_MODEL), lambda l, j: (l, 0, j)),
                  pl.BlockSpec((1, 1, D_MODEL), lambda l, j: (l, 0, j))],
        out_specs=pl.BlockSpec((1, nb, D_MODEL), lambda l, j: (l, 0, j)),
        out_shape=jax.ShapeDtypeStruct((DEPTH, nb, 6 * D_MODEL), F32),
        compiler_params=pltpu.CompilerParams(dimension_semantics=("arbitrary", "arbitrary"),
                                             vmem_limit_bytes=VMEM_LIMIT),
        name="ada",
    )(c_all, w_ada, b_ada.reshape(DEPTH, 1, 6 * D_MODEL))


def _inproj_body(x_ref, mod_ref, g_ref, b_ref, w_ref, bias_ref, proj_ref, *rest, apply_ln, bt, lt):
    x = x_ref[...]
    if apply_ln:
        x = _layer_norm(x, g_ref[...], b_ref[...])
        rest[0][...] = x
    u = (x * (1.0 + mod_ref[:, 1:2, :]) + mod_ref[:, 0:1, :]).reshape(bt * lt, D_MODEL).astype(BF16)
    for j in range(N_PROJ // TN_PROJ):
        cs = slice(j * TN_PROJ, (j + 1) * TN_PROJ)
        proj_ref[:, :, cs] = (_dot(u, w_ref[0, :, cs]) + bias_ref[0, :, cs]).reshape(bt, lt, TN_PROJ)


def _inproj(x, mod, ln_g, ln_b, w_r, b_r, layer, apply_ln, bt, lt):
    nb, lp, _ = x.shape
    tok = lambda i, t: (i, t, 0)
    out_shape = [jax.ShapeDtypeStruct((nb, lp, N_PROJ), F32)]
    out_specs = [pl.BlockSpec((bt, lt, N_PROJ), tok)]
    if apply_ln:
        out_shape.append(jax.ShapeDtypeStruct((nb, lp, D_MODEL), F32))
        out_specs.append(pl.BlockSpec((bt, lt, D_MODEL), tok))
    res = pl.pallas_call(
        functools.partial(_inproj_body, apply_ln=apply_ln, bt=bt, lt=lt),
        grid=(nb // bt, lp // lt),
        in_specs=[pl.BlockSpec((bt, lt, D_MODEL), tok),
                  pl.BlockSpec((bt, 6, D_MODEL), lambda i, t: (i, 0, 0)),
                  pl.BlockSpec((1, D_MODEL), lambda i, t: (0, 0)),
                  pl.BlockSpec((1, D_MODEL), lambda i, t: (0, 0)),
                  pl.BlockSpec((1, D_MODEL, N_PROJ), lambda i, t: (layer, 0, 0), pipeline_mode=pl.Buffered(1)),
                  pl.BlockSpec((1, 1, N_PROJ), lambda i, t: (layer, 0, 0))],
        out_specs=out_specs,
        out_shape=out_shape,
        compiler_params=pltpu.CompilerParams(dimension_semantics=("arbitrary", "arbitrary"),
                                             vmem_limit_bytes=VMEM_LIMIT),
        name=f"inproj_l{layer}_b{bt}",
    )(x, mod, ln_g, ln_b, w_r, b_r)
    return (res[0], res[1]) if apply_ln else (res[0], x)


def _conv_taps(xp_s, w_ref, width, tl):
    acc = None
    for j in range(width):
        tap = xp_s[:, pl.ds(HIST - (width - 1) + j, tl), :] * w_ref[j:j + 1, :].reshape(1, 1, -1)
        acc = tap if acc is None else acc + tap
    return acc


def _conv_history(xp_s, hist_ref, prev_ref, n_steps):
    if n_steps > 1:
        @pl.when(pl.program_id(1) == 0)
        def _():
            xp_s[:, 0:HIST, :] = hist_ref[...]

        @pl.when(pl.program_id(1) > 0)
        def _():
            xp_s[:, 0:HIST, :] = prev_ref[...]
    else:
        xp_s[:, 0:HIST, :] = hist_ref[...]


def _heads(x, nb, width):
    return jnp.stack([x[:, :, h * width:(h + 1) * width] for h in range(N_HEADS)],
                     axis=1).reshape(nb * N_HEADS, RB, width)


def _gate_cols(x, nb, lane0):
    return jnp.stack([x[:, :, lane0 + h:lane0 + h + 1] for h in range(N_HEADS)],
                     axis=1).reshape(nb * N_HEADS, RB, 1)


def _gate_rows(x, nb, lane0):
    xt = jnp.swapaxes(x, 1, 2)
    return jnp.stack([xt[:, lane0 + h:lane0 + h + 1, :] for h in range(N_HEADS)],
                     axis=1).reshape(nb * N_HEADS, 1, RB)


def _bmm(a, b):
    return jnp.einsum('nts,nsu->ntu', a.astype(BF16), b.astype(BF16), preferred_element_type=F32)


def _bmm_nt(a, b):
    return jnp.einsum('ntd,nsd->nts', a, b, preferred_element_type=F32)


def _block_masks(tl):
    rr = lax.broadcasted_iota(jnp.int32, (RB, RB), 0)
    cc = lax.broadcasted_iota(jnp.int32, (RB, RB), 1)
    incl = rr >= cc
    if tl < RB:
        incl = incl & ((rr // tl) == (cc // tl))
    return rr, cc, incl


def _seq_cumsum(x, incl, nb):
    tril = jnp.broadcast_to(incl.astype(BF16)[None], (nb, RB, RB))
    hi, mid, lo = _split3(x)
    return _bmm(tril, hi) + _bmm(tril, mid) + _bmm(tril, lo)


def _seq_last(x, nb, G, tl):
    return x.reshape(nb * G, tl, 128)[:, tl - 1:tl, :]


def _seq_rows(x3, nb, tl):
    return jnp.broadcast_to(x3, (x3.shape[0], tl, 128)).reshape(nb, RB, 128)


def _put_chains(ref, val, nb, G, tl):
    val4 = val.reshape(nb, N_HEADS, RB, val.shape[-1])
    for h in range(N_HEADS):
        ref[:, :, h] = val4[:, h].reshape(nb, G, tl, val.shape[-1])


def _put_seq_scalars(ref, x3, nb, G, lane0):
    x4 = x3.reshape(nb, G, 1, 128)
    for h in range(N_HEADS):
        ref[:, :, h] = jnp.broadcast_to(x4[:, :, :, lane0 + h:lane0 + h + 1], (nb, G, 1, 128))


def _step_tiling(nb, lp, NB, G, tl):
    assert G * tl == RB
    if G == 1:
        gx, tlx = 1, NB * RB
    else:
        assert lp == tl
        gx, tlx = NB * G, tl
    return gx, tlx, nb // gx, lp // tlx


def _mlstm_prep_body(qkvo_ref, bch_ref, prev_ref, sa_ref, sb_ref, cv0_ref, cw_ref,
                     yb_ref, cv_ref, nv_ref, q_ref, kw_ref, v_ref, cols_ref, bl_ref, bc_ref, kn_ref, xp_s,
                     *, NB, G, tl, tlx, NCS, lv):
    cs = pl.program_id(1)
    n = NB * N_HEADS
    lvl = lv - (NCS - 1) * tlx

    if NCS > 1:
        @pl.when(cs == 0)
        def _():
            xp_s[:, 0:HIST, :] = cv0_ref[...]

        @pl.when(cs > 0)
        def _():
            xp_s[:, 0:HIST, :] = prev_ref[:, :, D_MIX:2 * D_MIX] * prev_ref[:, :, 2 * D_MIX:3 * D_MIX]
    else:
        xp_s[:, 0:HIST, :] = cv0_ref[...]
    xp_s[:, HIST:HIST + tlx, :] = bch_ref[:, :, D_MIX:2 * D_MIX] * bch_ref[:, :, 2 * D_MIX:3 * D_MIX]
    yb_ref[...] = (bch_ref[:, :, 0:D_MIX] * _conv_taps(xp_s, cw_ref, CONV_B, tlx)).astype(BF16)

    @pl.when(cs == NCS - 1)
    def _():
        cv_ref[...] = xp_s[:, pl.ds(HIST + lvl - (CONV_B - 1), CONV_B - 1), :]

    i_all = sa_ref[...].reshape(NB, RB, 128)
    f_all = _log_sigmoid(sb_ref[...].reshape(NB, RB, 128))
    if lv < NCS * tlx:
        assert NCS == 1
        valid = (lax.broadcasted_iota(jnp.int32, (NB, RB, 128), 1) % tl) < lv
        i_all = jnp.where(valid, i_all, NEG)
        f_all = jnp.where(valid, f_all, 0.0)
    _, _, incl = _block_masks(tl)
    bcum = _seq_cumsum(f_all, incl, NB)
    blast = _seq_last(bcum, NB, G, tl)
    val = _seq_rows(blast, NB, tl) - bcum + i_all
    bmax = jnp.max(val.reshape(NB * G, tl, 128), axis=1, keepdims=True)
    wk0 = jnp.exp(val - _seq_rows(bmax, NB, tl))

    qkvo = qkvo_ref[...].reshape(NB, RB, 4 * D_MIX)
    q = _heads(qkvo[:, :, 0:D_MIX], NB, DH)
    k = _heads(qkvo[:, :, D_MIX:2 * D_MIX], NB, DH) * (DH ** -0.5)
    v = _heads(qkvo[:, :, 2 * D_MIX:3 * D_MIX], NB, DH)
    qb, kb, vb = q.astype(BF16), k.astype(BF16), v.astype(BF16)

    b_col = _gate_cols(bcum, NB, 0)
    dlog = jnp.where(incl[None], b_col - _gate_rows(bcum, NB, 0) + _gate_rows(i_all, NB, 0), NEG)
    d = jnp.max(dlog, axis=-1, keepdims=True)
    s0 = _bmm_nt(qb, kb) * jnp.exp(dlog - d)
    kw0 = k * _gate_cols(wk0, NB, 0)

    _put_chains(nv_ref, _bmm(s0, vb), NB, G, tl)
    _put_chains(q_ref, qb, NB, G, tl)
    _put_chains(kw_ref, kw0.astype(BF16), NB, G, tl)
    _put_chains(v_ref, vb, NB, G, tl)
    _put_chains(cols_ref, jnp.concatenate([d, b_col, jnp.sum(s0, axis=-1, keepdims=True),
                                           jnp.zeros((n, RB, 5), F32)], axis=-1), NB, G, tl)
    _put_seq_scalars(bl_ref, blast, NB, G, 0)
    _put_seq_scalars(bc_ref, bmax, NB, G, 0)
    kn = jnp.sum(kw0.reshape(n * G, tl, DH), axis=1, keepdims=True).reshape(NB, N_HEADS, G, 1, DH)
    for h in range(N_HEADS):
        kn_ref[:, :, h] = kn[:, h]


def _mlstm_prep(proj, cv0, cw, NB, G, tl, lv):
    nb, lp, _ = proj.shape
    gx, tlx, NI, NCS = _step_tiling(nb, lp, NB, G, tl)
    nbt = nb * lp // RB
    step = lambda i, c: (i * NCS + c, 0, 0, 0, 0)
    chain = lambda last, dt: jax.ShapeDtypeStruct((nbt, G, N_HEADS, tl, last), dt)
    cspec = lambda last: pl.BlockSpec((NB, G, N_HEADS, tl, last), step)
    scal = jax.ShapeDtypeStruct((nbt, G, N_HEADS, 1, 128), F32)
    sspec = pl.BlockSpec((NB, G, N_HEADS, 1, 128), step)
    bch = OFF_BCH // (3 * D_MIX)
    return pl.pallas_call(
        functools.partial(_mlstm_prep_body, NB=NB, G=G, tl=tl, tlx=tlx, NCS=NCS, lv=lv),
        grid=(NI, NCS),
        in_specs=[pl.BlockSpec((gx, tlx, 4 * D_MIX), lambda i, c: (i, c, OFF_QKVO // (4 * D_MIX))),
                  pl.BlockSpec((gx, tlx, 3 * D_MIX), lambda i, c: (i, c, bch)),
                  pl.BlockSpec((gx, HIST, 3 * D_MIX), lambda i, c: (i, jnp.maximum(c * (tlx // HIST) - 1, 0), bch)),
                  pl.BlockSpec((gx, tlx, 128), lambda i, c: (i, c, OFF_SA // 128)),
                  pl.BlockSpec((gx, tlx, 128), lambda i, c: (i, c, OFF_SB // 128)),
                  pl.BlockSpec((gx, HIST, D_MIX), lambda i, c: (i, 0, 0)),
                  pl.BlockSpec((8, D_MIX), lambda i, c: (0, 0))],
        out_specs=[pl.BlockSpec((gx, tlx, D_MIX), lambda i, c: (i, c, 0)),
                   pl.BlockSpec((gx, CONV_B - 1, D_MIX), lambda i, c: (i, 0, 0)),
                   cspec(DH), cspec(DH), cspec(DH), cspec(DH), cspec(8), sspec, sspec, sspec],
        out_shape=[jax.ShapeDtypeStruct((nb, lp, D_MIX), BF16),
                   jax.ShapeDtypeStruct((nb, CONV_B - 1, D_MIX), F32),
                   chain(DH, F32), chain(DH, BF16), chain(DH, BF16), chain(DH, BF16), chain(8, F32),
                   scal, scal, scal],
        scratch_shapes=[pltpu.VMEM((gx, HIST + tlx, D_MIX), F32)],
        compiler_params=pltpu.CompilerParams(dimension_semantics=("arbitrary", "arbitrary"),
                                             vmem_limit_bytes=VMEM_LIMIT),
        name=f"mlstm_prep_g{G}",
    )(proj, proj, proj, proj, proj, cv0, cw)


def _mlstm_scan_body(nv_ref, q_ref, kw_ref, v_ref, cols_ref, bl_ref, bc_ref, kn_ref, o_ref, C0_ref, n0_ref, m0_ref,
                     ng_ref, ya_ref, C_ref, n_ref, m_ref, *, NS, tl):
    c = pl.program_id(1)
    n = NS * N_HEADS

    @pl.when(c == 0)
    def _():
        C_ref[...] = C0_ref[...]
        n_ref[...] = n0_ref[...]
        m_ref[...] = m0_ref[...]

    chains = lambda ref: ref[...].reshape(n, ref.shape[-2], ref.shape[-1])
    C = C_ref[...].reshape(n, DH, DH)
    nvec = n_ref[...].reshape(n, 1, DH)
    m_prev = chains(m_ref)[:, :, 0:1]
    cols = chains(cols_ref)
    d, b, ds0 = cols[:, :, 0:1], cols[:, :, 1:2], cols[:, :, 2:3]
    qb = chains(q_ref)

    m_inter = b + m_prev
    m_t = jnp.maximum(m_inter, d)
    f = jnp.exp(d - m_t)
    inter = jnp.exp(m_inter - m_t)
    qC = jnp.einsum('ntd,nde->nte', qb, C.astype(BF16), preferred_element_type=F32)
    qn = jnp.sum(qb.astype(F32) * nvec, axis=-1, keepdims=True)
    num = f * chains(nv_ref) + inter * qC
    den = f * ds0 + inter * qn
    hh = num / jnp.maximum(jnp.abs(den), jnp.exp(-m_t))

    m_new = m_t[:, tl - 1:tl, :]
    decay = jnp.exp(chains(bl_ref)[:, :, 0:1] + m_prev - m_new)
    scale = jnp.exp(chains(bc_ref)[:, :, 0:1] - m_new)
    kv = jnp.einsum('ntd,nte->nde', chains(kw_ref), chains(v_ref), preferred_element_type=F32)
    C_ref[...] = (decay * C + scale * kv).reshape(NS, N_HEADS, DH, DH)
    n_ref[...] = (decay * nvec + scale * chains(kn_ref)).reshape(NS, N_HEADS, DH)
    m_ref[...] = jnp.broadcast_to(m_new, (n, 1, 128)).reshape(NS, N_HEADS, 1, 128)

    mu = jnp.mean(hh, axis=-1, keepdims=True)
    hc = hh - mu
    hn = (hc * lax.rsqrt(jnp.mean(hc * hc, axis=-1, keepdims=True) + LN_EPS)).reshape(NS, N_HEADS, tl, DH)
    for h in range(N_HEADS):
        hs = slice(h * DH, (h + 1) * DH)
        ya_ref[:, :, hs] = (_sigmoid(o_ref[:, :, hs]) * hn[:, h] * ng_ref[:, hs]).astype(BF16)


def _mlstm_scan(pre, proj, C0, n0, m0, ng, SB, G, tl):
    nb, lp, _ = proj.shape
    NS = SB * G
    NI, NC = nb // NS, lp // tl
    six = lambda a: a.reshape((NI * SB, NC) + a.shape[1:])
    spec6 = lambda a: pl.BlockSpec((SB, 1) + a.shape[1:], lambda i, c: (i, c, 0, 0, 0, 0))
    seq4 = lambda i, c: (i, 0, 0, 0)
    return pl.pallas_call(
        functools.partial(_mlstm_scan_body, NS=NS, tl=tl),
        grid=(NI, NC),
        in_specs=[spec6(a) for a in pre] + [
            pl.BlockSpec((NS, tl, D_MIX), lambda i, c: (i, c, (OFF_QKVO + 3 * D_MIX) // D_MIX)),
            pl.BlockSpec((NS, N_HEADS, DH, DH), seq4),
            pl.BlockSpec((NS, N_HEADS, DH), lambda i, c: (i, 0, 0)),
            pl.BlockSpec((NS, N_HEADS, 1, 128), seq4),
            pl.BlockSpec((1, D_MIX), lambda i, c: (0, 0))],
        out_specs=[pl.BlockSpec((NS, tl, D_MIX), lambda i, c: (i, c, 0)),
                   pl.BlockSpec((NS, N_HEADS, DH, DH), seq4),
                   pl.BlockSpec((NS, N_HEADS, DH), lambda i, c: (i, 0, 0)),
                   pl.BlockSpec((NS, N_HEADS, 1, 128), seq4)],
        out_shape=[jax.ShapeDtypeStruct((nb, lp, D_MIX), BF16),
                   jax.ShapeDtypeStruct((nb, N_HEADS, DH, DH), F32),
                   jax.ShapeDtypeStruct((nb, N_HEADS, DH), F32),
                   jax.ShapeDtypeStruct((nb, N_HEADS, 1, 128), F32)],
        compiler_params=pltpu.CompilerParams(dimension_semantics=("arbitrary", "arbitrary"),
                                             vmem_limit_bytes=VMEM_LIMIT),
        name=f"mlstm_scan_g{G}",
    )(*[six(a) for a in pre], proj, C0, n0, m0, ng)


def _unit_lower_inverse(n, rr, cc, tl):
    eye = (rr == cc).astype(F32)[None]
    p = jnp.where(((rr // INV_BLOCK) == (cc // INV_BLOCK))[None], n, 0.0)
    x = eye + p
    b = 2
    while b < INV_BLOCK:
        p = _bmm(p, p)
        x = x + _bmm(x, p)
        b *= 2
    b = INV_BLOCK
    while b < tl:
        off = jnp.where((((rr // (2 * b)) == (cc // (2 * b))) & ((rr // b) != (cc // b)))[None], n, 0.0)
        x = x + _bmm(x, _bmm(off, x))
        b *= 2
    return x


def _gdn_prep_body(x_ref, prev_ref, sa_ref, sb_ref, gc0_ref, cw_ref, alog_ref, dtb_ref,
                   u_ref, w_ref, qg_ref, kd_ref, qkm_ref, eg_ref, gcs_ref, xp_s, *, NB, G, tl, tlx, NCS, lv):
    cs = pl.program_id(1)
    lvl = lv - (NCS - 1) * tlx

    _conv_history(xp_s, gc0_ref, prev_ref, NCS)
    xp_s[:, HIST:HIST + tlx, :] = x_ref[...]
    qkv = _silu(_conv_taps(xp_s, cw_ref, CONV_C, tlx)).reshape(NB, RB, 3 * D_MIX)

    @pl.when(cs == NCS - 1)
    def _():
        gcs_ref[...] = xp_s[:, pl.ds(HIST + lvl - (CONV_C - 1), CONV_C - 1), :]

    beta_all = _sigmoid(sa_ref[...].reshape(NB, RB, 128))
    g_all = -jnp.exp(alog_ref[...]) * _softplus(sb_ref[...].reshape(NB, RB, 128) + dtb_ref[...])
    if lv < NCS * tlx:
        assert NCS == 1
        valid = (lax.broadcasted_iota(jnp.int32, (NB, RB, 128), 1) % tl) < lv
        beta_all = jnp.where(valid, beta_all, 0.0)
        g_all = jnp.where(valid, g_all, 0.0)
    rr, cc, incl = _block_masks(tl)
    diag = rr == cc
    gam = _seq_cumsum(g_all, incl, NB)
    glast = _seq_last(gam, NB, G, tl)
    gcol = _gate_cols(gam, NB, 4)
    bcol = _gate_cols(beta_all, NB, 4)
    egcol = _gate_cols(jnp.exp(gam), NB, 4)
    kdcol = _gate_cols(jnp.exp(_seq_rows(glast, NB, tl) - gam), NB, 4)

    q = _heads(qkv[:, :, 0:D_MIX], NB, DH)
    k = _heads(qkv[:, :, D_MIX:2 * D_MIX], NB, DH)
    v = _heads(qkv[:, :, 2 * D_MIX:3 * D_MIX], NB, DH)
    q = q * lax.rsqrt(jnp.sum(q * q, axis=-1, keepdims=True) + NORM_EPS) * (DH ** -0.5)
    k = k * lax.rsqrt(jnp.sum(k * k, axis=-1, keepdims=True) + NORM_EPS)
    qb, kb = q.astype(BF16), k.astype(BF16)

    dmat = jnp.exp(jnp.where(incl[None], gcol - _gate_rows(gam, NB, 4), NEG))
    nmat = jnp.where(diag[None], 0.0, -(bcol * _bmm_nt(kb, kb) * dmat))
    rhs = jnp.concatenate([bcol * v, (bcol * egcol) * k], axis=-1)
    sol = _bmm(_unit_lower_inverse(nmat, rr, cc, tl), rhs)
    qkm = (_bmm_nt(qb, kb) * dmat).astype(BF16).reshape(NB, N_HEADS, RB, RB)

    _put_chains(u_ref, sol[:, :, 0:DH], NB, G, tl)
    _put_chains(w_ref, sol[:, :, DH:2 * DH].astype(BF16), NB, G, tl)
    _put_chains(qg_ref, (q * egcol).astype(BF16), NB, G, tl)
    _put_chains(kd_ref, (k * kdcol).astype(BF16), NB, G, tl)
    _put_seq_scalars(eg_ref, jnp.exp(glast), NB, G, 4)
    for h in range(N_HEADS):
        for g in range(G):
            qkm_ref[:, g, h] = qkm[:, h, g * tl:(g + 1) * tl, g * tl:(g + 1) * tl]


def _gdn_prep(proj, gc0, cw, alog, dtb, NB, G, tl, lv):
    nb, lp, _ = proj.shape
    assert tl % INV_BLOCK == 0 and (tl // INV_BLOCK) & (tl // INV_BLOCK - 1) == 0
    gx, tlx, NI, NCS = _step_tiling(nb, lp, NB, G, tl)
    nbt = nb * lp // RB
    step = lambda i, c: (i * NCS + c, 0, 0, 0, 0)
    par = lambda i, c: (0, 0)
    chain = lambda last, dt: jax.ShapeDtypeStruct((nbt, G, N_HEADS, tl, last), dt)
    cspec = lambda last: pl.BlockSpec((NB, G, N_HEADS, tl, last), step)
    return pl.pallas_call(
        functools.partial(_gdn_prep_body, NB=NB, G=G, tl=tl, tlx=tlx, NCS=NCS, lv=lv),
        grid=(NI, NCS),
        in_specs=[pl.BlockSpec((gx, tlx, 3 * D_MIX), lambda i, c: (i, c, OFF_QKVC // (3 * D_MIX))),
                  pl.BlockSpec((gx, HIST, 3 * D_MIX),
                               lambda i, c: (i, jnp.maximum(c * (tlx // HIST) - 1, 0), OFF_QKVC // (3 * D_MIX))),
                  pl.BlockSpec((gx, tlx, 128), lambda i, c: (i, c, OFF_SA // 128)),
                  pl.BlockSpec((gx, tlx, 128), lambda i, c: (i, c, OFF_SB // 128)),
                  pl.BlockSpec((gx, HIST, 3 * D_MIX), lambda i, c: (i, 0, 0)),
                  pl.BlockSpec((8, 3 * D_MIX), par),
                  pl.BlockSpec((1, 128), par),
                  pl.BlockSpec((1, 128), par)],
        out_specs=[cspec(DH), cspec(DH), cspec(DH), cspec(DH), cspec(tl),
                   pl.BlockSpec((NB, G, N_HEADS, 1, 128), step),
                   pl.BlockSpec((gx, CONV_C - 1, 3 * D_MIX), lambda i, c: (i, 0, 0))],
        out_shape=[chain(DH, F32), chain(DH, BF16), chain(DH, BF16), chain(DH, BF16), chain(tl, BF16),
                   jax.ShapeDtypeStruct((nbt, G, N_HEADS, 1, 128), F32),
                   jax.ShapeDtypeStruct((nb, CONV_C - 1, 3 * D_MIX), F32)],
        scratch_shapes=[pltpu.VMEM((gx, HIST + tlx, 3 * D_MIX), F32)],
        compiler_params=pltpu.CompilerParams(dimension_semantics=("arbitrary", "arbitrary"),
                                             vmem_limit_bytes=VMEM_LIMIT),
        name=f"gdn_prep_g{G}",
    )(proj, proj, proj, proj, gc0, cw, alog, dtb)


def _gdn_scan_body(u_ref, w_ref, qg_ref, kd_ref, qkm_ref, eg_ref, z_ref, S0_ref, gng_ref, yc_ref, S_ref,
                   *, NS, tl):
    c = pl.program_id(1)
    n = NS * N_HEADS

    @pl.when(c == 0)
    def _():
        S_ref[...] = S0_ref[...]

    S = S_ref[...].reshape(n, DH, DH)
    Sb = S.astype(BF16)
    chains = lambda ref: ref[...].reshape(n, tl, ref.shape[-1])
    v_new = chains(u_ref) - jnp.einsum('ntd,nde->nte', chains(w_ref), Sb, preferred_element_type=F32)
    vnb = v_new.astype(BF16)
    o = (jnp.einsum('ntd,nde->nte', chains(qg_ref), Sb, preferred_element_type=F32)
         + jnp.einsum('nts,nse->nte', chains(qkm_ref), vnb, preferred_element_type=F32))
    eg = eg_ref[...].reshape(n, 1, 128)[:, :, 0:1]
    S_new = eg * S + jnp.einsum('ntd,nte->nde', chains(kd_ref), vnb, preferred_element_type=F32)
    S_ref[...] = S_new.reshape(NS, N_HEADS, DH, DH)

    on = (o * lax.rsqrt(jnp.mean(o * o, axis=-1, keepdims=True) + NORM_EPS) * gng_ref[...]).reshape(NS, N_HEADS, tl, DH)
    for h in range(N_HEADS):
        yc_ref[:, :, h * DH:(h + 1) * DH] = (on[:, h] * _silu(z_ref[:, :, h * DH:(h + 1) * DH])).astype(BF16)


def _gdn_scan(pre, proj, S0, gng, SB, G, tl):
    nb, lp, _ = proj.shape
    NS = SB * G
    NI, NC = nb // NS, lp // tl
    six = lambda a: a.reshape((NI * SB, NC) + a.shape[1:])
    cspec = lambda last: pl.BlockSpec((SB, 1, G, N_HEADS, tl, last), lambda i, c: (i, c, 0, 0, 0, 0))
    u, w, qg, kd, qkm, eg = (six(a) for a in pre)
    return pl.pallas_call(
        functools.partial(_gdn_scan_body, NS=NS, tl=tl),
        grid=(NI, NC),
        in_specs=[cspec(DH), cspec(DH), cspec(DH), cspec(DH), cspec(tl),
                  pl.BlockSpec((SB, 1, G, N_HEADS, 1, 128), lambda i, c: (i, c, 0, 0, 0, 0)),
                  pl.BlockSpec((NS, tl, D_MIX), lambda i, c: (i, c, OFF_Z // D_MIX)),
                  pl.BlockSpec((NS, N_HEADS, DH, DH), lambda i, c: (i, 0, 0, 0)),
                  pl.BlockSpec((1, DH), lambda i, c: (0, 0))],
        out_specs=[pl.BlockSpec((NS, tl, D_MIX), lambda i, c: (i, c, 0)),
                   pl.BlockSpec((NS, N_HEADS, DH, DH), lambda i, c: (i, 0, 0, 0))],
        out_shape=[jax.ShapeDtypeStruct((nb, lp, D_MIX), BF16),
                   jax.ShapeDtypeStruct((nb, N_HEADS, DH, DH), F32)],
        compiler_params=pltpu.CompilerParams(dimension_semantics=("arbitrary", "arbitrary"),
                                             vmem_limit_bytes=VMEM_LIMIT),
        name=f"gdn_scan_g{G}",
    )(u, w, qg, kd, qkm, eg, proj, S0, gng)


def _route(rl):
    lane = lax.broadcasted_iota(jnp.int32, rl.shape, 1).astype(F32)
    is_g = (lane >= N_EXPERTS) & (lane < N_EXPERTS + N_GROUPS)
    gl = jnp.where(is_g, rl, NEG)
    gmax = jnp.max(gl, axis=-1, keepdims=True)
    grp = jnp.min(jnp.where(gl == gmax, lane - N_EXPERTS, 4.0 * N_EXPERTS), axis=-1, keepdims=True)
    p_grp = 1.0 / jnp.sum(jnp.where(is_g, jnp.exp(gl - gmax), 0.0), axis=-1, keepdims=True)
    lo = grp * EXPERTS_PER_GROUP
    in_grp = (lane >= lo) & (lane < lo + EXPERTS_PER_GROUP)
    el = jnp.where(in_grp, rl, NEG)
    m1 = jnp.max(el, axis=-1, keepdims=True)
    i1 = jnp.min(jnp.where(el == m1, lane, 4.0 * N_EXPERTS), axis=-1, keepdims=True)
    el2 = jnp.where(lane == i1, NEG, el)
    m2 = jnp.max(el2, axis=-1, keepdims=True)
    i2 = jnp.min(jnp.where(el2 == m2, lane, 4.0 * N_EXPERTS), axis=-1, keepdims=True)
    e2 = jnp.exp(m2 - m1)
    w1 = p_grp / (1.0 + e2)
    w2 = p_grp * e2 / (1.0 + e2)
    return i1, i2, w1, w2


def _merge_body(ya_ref, yb_ref, yc_ref, ga_ref, gb_ref, gc_ref, x_ref, mod_ref, wa_ref, wb_ref, wc_ref, wo_ref,
                lg_ref, lb_ref, wrh_ref, wrl_ref, br_ref, x1_ref, rinfo_ref, cnt_ref, *, bt, lt):
    tm = bt * lt

    @pl.when((pl.program_id(0) == 0) & (pl.program_id(1) == 0))
    def _():
        cnt_ref[...] = jnp.zeros_like(cnt_ref)

    def r2(ref):
        return ref[...].reshape(tm, ref.shape[-1])

    merged = (_sigmoid(r2(ga_ref)) * _dot(r2(ya_ref), wa_ref[0])
              + _sigmoid(r2(gb_ref)) * _dot(r2(yb_ref), wb_ref[0])
              + _sigmoid(r2(gc_ref)) * _dot(r2(yc_ref), wc_ref[0]))
    out = _dot(merged.astype(BF16), wo_ref[0])
    y = DN_ALPHA * x_ref[...] + (1.0 + mod_ref[:, 2:3, :]) * out.reshape(bt, lt, D_MODEL)
    x1 = _layer_norm(y, lg_ref[0], lb_ref[0])
    x1_ref[...] = x1
    u2 = (x1 * (1.0 + mod_ref[:, 4:5, :]) + mod_ref[:, 3:4, :]).reshape(tm, D_MODEL)
    hi = u2.astype(BF16)
    lo = (u2 - hi.astype(F32)).astype(BF16)
    rl = _dot(hi, wrh_ref[0]) + _dot(lo, wrh_ref[0]) + _dot(hi, wrl_ref[0]) + br_ref[0]
    i1, i2, w1, w2 = _route(rl)
    lane = lax.broadcasted_iota(jnp.int32, (tm, 128), 1).astype(F32)
    onehot = jnp.where((lane == i1) | (lane == i2), 1.0, 0.0)
    rr = lax.broadcasted_iota(jnp.int32, (tm, tm), 0)
    cc = lax.broadcasted_iota(jnp.int32, (tm, tm), 1)
    before = _dot((rr > cc).astype(BF16), onehot.astype(BF16)) + cnt_ref[0:1, :]
    rank1 = jnp.sum(jnp.where(lane == i1, before, 0.0), axis=-1, keepdims=True)
    rank2 = jnp.sum(jnp.where(lane == i2, before, 0.0), axis=-1, keepdims=True)
    cnt_ref[0:1, :] += jnp.sum(onehot, axis=0, keepdims=True)
    rinfo = jnp.zeros((tm, 128), F32)
    for k, val in enumerate((i1, i2, w1, w2, rank1, rank2)):
        rinfo = jnp.where(lane == k, val, rinfo)
    rinfo_ref[...] = rinfo.reshape(bt, lt, 128)


def _merge(ya, yb, yc, proj, x, mod, wa, wb, wc, wo, lg, lb, wrh, wrl, br, layer, bt, lt):
    nb, lp, _ = x.shape
    tok = lambda i, t: (i, t, 0)
    wsp = lambda shape: pl.BlockSpec((1,) + shape, lambda i, t: (layer, 0, 0))
    g0 = OFF_GTS // D_MODEL
    return pl.pallas_call(
        functools.partial(_merge_body, bt=bt, lt=lt),
        grid=(nb // bt, lp // lt),
        in_specs=[pl.BlockSpec((bt, lt, D_MIX), tok),
                  pl.BlockSpec((bt, lt, D_MIX), tok),
                  pl.BlockSpec((bt, lt, D_MIX), tok),
                  pl.BlockSpec((bt, lt, D_MODEL), lambda i, t: (i, t, g0)),
                  pl.BlockSpec((bt, lt, D_MODEL), lambda i, t: (i, t, g0 + 1)),
                  pl.BlockSpec((bt, lt, D_MODEL), lambda i, t: (i, t, g0 + 2)),
                  pl.BlockSpec((bt, lt, D_MODEL), tok),
                  pl.BlockSpec((bt, 6, D_MODEL), lambda i, t: (i, 0, 0)),
                  wsp((D_MIX, D_MODEL)), wsp((D_MIX, D_MODEL)), wsp((D_MIX, D_MODEL)), wsp((D_MODEL, D_MODEL)),
                  wsp((1, D_MODEL)), wsp((1, D_MODEL)),
                  wsp((D_MODEL, 128)), wsp((D_MODEL, 128)), wsp((1, 128))],
        out_specs=[pl.BlockSpec((bt, lt, D_MODEL), tok),
                   pl.BlockSpec((bt, lt, 128), tok),
                   pl.BlockSpec((8, 128), lambda i, t: (0, 0))],
        out_shape=[jax.ShapeDtypeStruct((nb, lp, D_MODEL), F32),
                   jax.ShapeDtypeStruct((nb, lp, 128), F32),
                   jax.ShapeDtypeStruct((8, 128), F32)],
        compiler_params=pltpu.CompilerParams(dimension_semantics=("arbitrary", "arbitrary"),
                                             vmem_limit_bytes=VMEM_LIMIT),
        name=f"merge_l{layer}_b{bt}",
    )(ya, yb, yc, proj, proj, proj, x, mod, wa, wb, wc, wo, lg, lb, wrh, wrl, br)


def _moe_plan(rinfo, cnt, tm, mb):
    nb, lp, _ = rinfo.shape
    n_tok = nb * lp
    n_blocks = 2 * n_tok // mb + N_EXPERTS
    counts = cnt[0, :N_EXPERTS].astype(jnp.int32)
    nblk = (counts + mb - 1) // mb
    pend = jnp.cumsum(nblk)
    pstart = pend - nblk
    expert = rinfo[..., 0:2].astype(jnp.int32)
    rank = rinfo[..., 4:6].astype(jnp.int32)
    ids = jnp.arange(N_EXPERTS, dtype=jnp.int32)
    first_row = jnp.sum(jnp.where(expert[..., None] == ids, pstart * mb, 0), axis=-1)
    pos = (first_row + rank).reshape(n_tok // tm, 1, 2 * tm)
    block_e = jnp.sum(pend[None, :] <= jnp.arange(n_blocks, dtype=jnp.int32)[:, None], axis=1)
    block_e = jnp.minimum(block_e, N_EXPERTS - 1).astype(jnp.int32)
    n_used = pend[N_EXPERTS - 1:].astype(jnp.int32)
    ztail = jnp.where(nblk > 0, (pend - 1) * mb, -1).astype(jnp.int32)
    ztail = jnp.concatenate([ztail, n_used]).reshape(1, 1, N_EXPERTS + 1)
    return pos, block_e, n_used, ztail, n_blocks


def _row_copy(src, dst, sem):
    return pltpu.make_async_copy(src, dst, sem)


def _dispatch_body(pos_ref, ztail_ref, x1_ref, mod_ref, disp_ref, u_scr, z_scr, sem, zsem, *, bt, lt, n_blocks, mb):
    tm = bt * lt

    @pl.when((pl.program_id(0) == 0) & (pl.program_id(1) == 0))
    def _():
        z_scr[...] = jnp.zeros_like(z_scr)
        for e in range(N_EXPERTS):
            @pl.when(ztail_ref[0, 0, e] >= 0)
            def _():
                row = pl.multiple_of(ztail_ref[0, 0, e], mb)
                _row_copy(z_scr, disp_ref.at[pl.ds(row, mb)], zsem).start()
        def unused(j):
            return _row_copy(z_scr, disp_ref.at[pl.ds(pl.multiple_of(j * mb, mb), mb)], zsem)

        n_used = ztail_ref[0, 0, N_EXPERTS]
        lax.fori_loop(n_used, n_blocks, lambda j, c: (unused(j).start(), c)[1], 0)
        lax.fori_loop(n_used, n_blocks, lambda j, c: (unused(j).wait(), c)[1], 0)
        for e in range(N_EXPERTS):
            @pl.when(ztail_ref[0, 0, e] >= 0)
            def _():
                row = pl.multiple_of(ztail_ref[0, 0, e], mb)
                _row_copy(z_scr, disp_ref.at[pl.ds(row, mb)], zsem).wait()

    u2 = x1_ref[...] * (1.0 + mod_ref[:, 4:5, :]) + mod_ref[:, 3:4, :]
    u_scr[...] = u2.reshape(tm, D_MODEL)

    def issue(t, carry):
        for k in range(2):
            _row_copy(u_scr.at[pl.ds(t, 1)], disp_ref.at[pl.ds(pos_ref[0, 0, 2 * t + k], 1)], sem).start()
        return carry

    lax.fori_loop(0, tm, issue, 0, unroll=8)
    for k in range(2):
        _row_copy(u_scr, disp_ref.at[pl.ds(0, tm)], sem).wait()


def _dispatch(x1, mod, pos, ztail, n_blocks, mb, bt, lt):
    nb, lp, _ = x1.shape
    nt = lp // lt
    return pl.pallas_call(
        functools.partial(_dispatch_body, bt=bt, lt=lt, n_blocks=n_blocks, mb=mb),
        grid=(nb // bt, nt),
        in_specs=[pl.BlockSpec((1, 1, 2 * bt * lt), lambda i, t: (i * nt + t, 0, 0), memory_space=pltpu.SMEM),
                  pl.BlockSpec((1, 1, N_EXPERTS + 1), lambda i, t: (0, 0, 0), memory_space=pltpu.SMEM),
                  pl.BlockSpec((bt, lt, D_MODEL), lambda i, t: (i, t, 0)),
                  pl.BlockSpec((bt, 6, D_MODEL), lambda i, t: (i, 0, 0))],
        out_specs=pl.BlockSpec(memory_space=pl.ANY),
        out_shape=jax.ShapeDtypeStruct((n_blocks * mb, D_MODEL), F32),
        scratch_shapes=[pltpu.VMEM((bt * lt, D_MODEL), F32), pltpu.VMEM((mb, D_MODEL), F32),
                        pltpu.SemaphoreType.DMA, pltpu.SemaphoreType.DMA],
        compiler_params=pltpu.CompilerParams(dimension_semantics=("arbitrary", "arbitrary"),
                                             vmem_limit_bytes=VMEM_LIMIT),
        name=f"dispatch_b{bt}",
    )(pos, ztail, x1, mod)


def _experts_body(be_ref, nu_ref, x_ref, wg_ref, wu_ref, wd_ref, o_ref):
    j = pl.program_id(0)

    @pl.when(j < nu_ref[0])
    def _():
        x = x_ref[...].astype(BF16)
        hb = _silu(_dot(x, wg_ref[0, 0].astype(BF16))) * _dot(x, wu_ref[0, 0].astype(BF16))
        o_ref[...] = _dot(hb.astype(BF16), wd_ref[0, 0].astype(BF16))

    @pl.when(j >= nu_ref[0])
    def _():
        o_ref[...] = jnp.zeros_like(o_ref)


def _experts(disp, block_e, n_used, wg, wu, wd, layer, mb):
    n_blocks = disp.shape[0] // mb
    wmap = lambda j, be, nu: (layer, be[j], 0, 0)
    return pl.pallas_call(
        _experts_body,
        grid_spec=pltpu.PrefetchScalarGridSpec(
            num_scalar_prefetch=2,
            grid=(n_blocks,),
            in_specs=[pl.BlockSpec((mb, D_MODEL), lambda j, be, nu: (jnp.minimum(j, nu[0] - 1), 0)),
                      pl.BlockSpec((1, 1, D_MODEL, D_EXPERT), wmap),
                      pl.BlockSpec((1, 1, D_MODEL, D_EXPERT), wmap),
                      pl.BlockSpec((1, 1, D_EXPERT, D_MODEL), wmap)],
            out_specs=pl.BlockSpec((mb, D_MODEL), lambda j, be, nu: (j, 0))),
        out_shape=jax.ShapeDtypeStruct(disp.shape, F32),
        compiler_params=pltpu.CompilerParams(dimension_semantics=("arbitrary",), vmem_limit_bytes=VMEM_LIMIT),
        name=f"experts_l{layer}_n{n_blocks}",
    )(block_e, n_used, disp, wg, wu, wd)


def _combine_body(pos_ref, x1_ref, mod_ref, rinfo_ref, eo_ref, lg_ref, lb_ref, x2_ref, r_scr, sem, *, bt, lt):
    tm = bt * lt

    def issue(t, carry):
        for k in range(2):
            _row_copy(eo_ref.at[pl.ds(pos_ref[0, 0, 2 * t + k], 1)], r_scr.at[k, pl.ds(t, 1)], sem).start()
        return carry

    lax.fori_loop(0, tm, issue, 0, unroll=8)
    for k in range(2):
        _row_copy(eo_ref.at[pl.ds(0, tm)], r_scr.at[k], sem).wait()

    rinfo = rinfo_ref[...].reshape(tm, 128)
    moe = r_scr[0] * rinfo[:, 2:3] + r_scr[1] * rinfo[:, 3:4]
    y = DN_ALPHA * x1_ref[...] + (1.0 + mod_ref[:, 5:6, :]) * moe.reshape(bt, lt, D_MODEL)
    x2_ref[...] = _layer_norm(y, lg_ref[0], lb_ref[0])


def _combine(x1, mod, rinfo, eo, pos, lg, lb, layer, bt, lt):
    nb, lp, _ = x1.shape
    nt = lp // lt
    tok = lambda i, t: (i, t, 0)
    return pl.pallas_call(
        functools.partial(_combine_body, bt=bt, lt=lt),
        grid=(nb // bt, nt),
        in_specs=[pl.BlockSpec((1, 1, 2 * bt * lt), lambda i, t: (i * nt + t, 0, 0), memory_space=pltpu.SMEM),
                  pl.BlockSpec((bt, lt, D_MODEL), tok),
                  pl.BlockSpec((bt, 6, D_MODEL), lambda i, t: (i, 0, 0)),
                  pl.BlockSpec((bt, lt, 128), tok),
                  pl.BlockSpec(memory_space=pl.ANY),
                  pl.BlockSpec((1, 1, D_MODEL), lambda i, t: (layer, 0, 0)),
                  pl.BlockSpec((1, 1, D_MODEL), lambda i, t: (layer, 0, 0))],
        out_specs=pl.BlockSpec((bt, lt, D_MODEL), tok),
        out_shape=jax.ShapeDtypeStruct((nb, lp, D_MODEL), F32),
        scratch_shapes=[pltpu.VMEM((2, bt * lt, D_MODEL), F32), pltpu.SemaphoreType.DMA],
        compiler_params=pltpu.CompilerParams(dimension_semantics=("arbitrary", "arbitrary"),
                                             vmem_limit_bytes=VMEM_LIMIT),
        name=f"combine_l{layer}_b{bt}",
    )(pos, x1, mod, rinfo, eo, lg, lb)


def _hist(state):
    return jnp.pad(state, ((0, 0), (HIST - state.shape[1], 0), (0, 0)))


def _trunk(x, mod, st, p, bt, lt, lv, mix, moe_block, ip_tile):
    nb = x.shape[0]
    new = {key: [] for key in ('C', 'n', 'm', 'conv', 'S', 'gconv')}
    for l in range(DEPTH):
        proj, x = _inproj(x, mod[l], p['ln_in_g'], p['ln_in_b'], p['w_in_r'], p['b_in_r'], l, l == 0, *ip_tile)
        m0 = jnp.broadcast_to(st['m'][l][:, :, None, None], (nb, N_HEADS, 1, 128))
        yb, conv, *pre = _mlstm_prep(proj, _hist(st['conv'][l]), p['conv_b_w8'][l], mix['NB'], mix['G'], mix['tl'], lv)
        ya, C, n, m = _mlstm_scan(pre, proj, st['C'][l], st['n'][l], m0, p['mlstm_norm_g'][l:l + 1],
                                  mix['SB'], mix['G'], mix['tl'])
        *pre, gconv = _gdn_prep(proj, _hist(st['gconv'][l]), p['conv_c_w8'][l], p['alog_row'][l], p['dtb_row'][l],
                                mix['NB'], mix['G'], mix['tl'], lv)
        yc, S = _gdn_scan(pre, proj, st['S'][l], p['gdn_norm_g'][l:l + 1], mix['SB'], mix['G'], mix['tl'])
        x1, rinfo, cnt = _merge(ya, yb, yc, proj, x, mod[l], p['w_br_a'], p['w_br_b'], p['w_br_c'], p['w_out'],
                                p['ln1_g'], p['ln1_b'], p['wr_hi'], p['wr_lo'], p['br'], l, bt, lt)
        pos, block_e, n_used, ztail, n_blocks = _moe_plan(rinfo, cnt, bt * lt, moe_block)
        disp = _dispatch(x1, mod[l], pos, ztail, n_blocks, moe_block, bt, lt)
        eo = _experts(disp, block_e, n_used, p['exp_w_gate'], p['exp_w_up'], p['exp_w_down'], l, moe_block)
        x = _combine(x1, mod[l], rinfo, eo, pos, p['ln2_g'], p['ln2_b'], l, bt, lt)
        new['C'].append(C)
        new['n'].append(n)
        new['m'].append(m[:, :, 0, 0])
        new['conv'].append(conv)
        new['S'].append(S)
        new['gconv'].append(gconv)
    return x, {key: jnp.stack(val) for key, val in new.items()}


def kernel(x_prompt, x_sample, state_mlstm_C, state_mlstm_n, state_mlstm_m, state_conv, state_gdn_S, state_gdn_conv, c_prompt, c_sample, ln_in_g, ln_in_b, w_ada, b_ada, w_in, b_in, mlstm_norm_g, conv_b_w, conv_c_w, gdn_a_log, gdn_dt_bias, gdn_norm_g, w_br_a, w_br_b, w_br_c, w_out, ln1_g, ln1_b, router_g_w, router_g_b, router_e_w, router_e_b, exp_w_gate, exp_w_up, exp_w_down, ln2_g, ln2_b):
    nbp, lp, _ = x_prompt.shape
    nbs, ls, _ = x_sample.shape
    lsp = 8

    def regroup(a):
        parts = [a[..., 3592:5128], a[..., 2056:3592], a[..., 5648:8720], a[..., 0:2048], a[..., 5128:5640],
                 a[..., 2048:2052], a[..., 5640:5644], jnp.zeros(a.shape[:-1] + (120,), a.dtype),
                 a[..., 2052:2056], a[..., 5644:5648], jnp.zeros(a.shape[:-1] + (120,), a.dtype)]
        return jnp.concatenate(parts, axis=-1)

    wr = jnp.concatenate([router_e_w, router_g_w, jnp.zeros((DEPTH, D_MODEL, 128 - N_EXPERTS - N_GROUPS), F32)], axis=-1)
    wr_hi = wr.astype(BF16)
    lane_pad = lambda a: jnp.pad(a, ((0, 0), (4, 128 - 4 - N_HEADS)))[:, None, :]
    p = dict(
        ln_in_g=ln_in_g.reshape(1, D_MODEL), ln_in_b=ln_in_b.reshape(1, D_MODEL),
        w_in_r=regroup(w_in).astype(BF16), b_in_r=regroup(b_in).reshape(DEPTH, 1, N_PROJ),
        mlstm_norm_g=mlstm_norm_g,
        conv_b_w8=jnp.pad(conv_b_w, ((0, 0), (0, 8 - CONV_B), (0, 0))),
        conv_c_w8=jnp.pad(conv_c_w, ((0, 0), (0, 8 - CONV_C), (0, 0))),
        alog_row=lane_pad(gdn_a_log), dtb_row=lane_pad(gdn_dt_bias), gdn_norm_g=gdn_norm_g,
        w_br_a=w_br_a.astype(BF16), w_br_b=w_br_b.astype(BF16), w_br_c=w_br_c.astype(BF16),
        w_out=w_out.astype(BF16),
        ln1_g=ln1_g.reshape(DEPTH, 1, D_MODEL), ln1_b=ln1_b.reshape(DEPTH, 1, D_MODEL),
        wr_hi=wr_hi, wr_lo=(wr - wr_hi.astype(F32)).astype(BF16),
        br=jnp.concatenate([router_e_b, router_g_b, jnp.zeros((DEPTH, 128 - N_EXPERTS - N_GROUPS), F32)],
                           axis=-1).reshape(DEPTH, 1, 128),
        exp_w_gate=exp_w_gate, exp_w_up=exp_w_up, exp_w_down=exp_w_down,
        ln2_g=ln2_g.reshape(DEPTH, 1, D_MODEL), ln2_b=ln2_b.reshape(DEPTH, 1, D_MODEL),
    )

    mod = _ada(jnp.concatenate([c_prompt, c_sample], axis=0), w_ada, b_ada)
    mod = mod.reshape(DEPTH, nbp + nbs, 6, D_MODEL)

    zeros = lambda *s: jnp.zeros((DEPTH, nbp) + s, F32)
    st_p = {'C': zeros(N_HEADS, DH, DH), 'n': zeros(N_HEADS, DH), 'm': zeros(N_HEADS),
            'conv': zeros(CONV_B - 1, D_MIX), 'S': zeros(N_HEADS, DH, DH), 'gconv': zeros(CONV_C - 1, 3 * D_MIX)}
    y_p, sp = _trunk(x_prompt, mod[:, :nbp], st_p, p, bt=1, lt=512, lv=lp,
                     mix=dict(NB=4, G=1, tl=RB, SB=nbp), moe_block=256, ip_tile=(1, 256))

    st_s = {'C': state_mlstm_C, 'n': state_mlstm_n, 'm': state_mlstm_m, 'conv': state_conv,
            'S': state_gdn_S, 'gconv': state_gdn_conv}
    xs = jnp.pad(x_sample, ((0, 0), (0, lsp - ls), (0, 0)))
    y_s, ss = _trunk(xs, mod[:, nbp:], st_s, p, bt=64, lt=lsp, lv=ls,
                     mix=dict(NB=2, G=RB // lsp, tl=lsp, SB=1), moe_block=128, ip_tile=(32, lsp))
    y_s = y_s[:, :ls]

    return (y_p, y_s, sp['C'], sp['n'], sp['m'], sp['conv'], sp['S'], sp['gconv'],
            ss['C'], ss['n'], ss['m'], ss['conv'], ss['S'], ss['gconv'])
```

```python
import functools

import jax
import jax.numpy as jnp
from jax import lax
from jax.experimental import pallas as pl
from jax.experimental.pallas import tpu as pltpu

F32 = jnp.float32
BF16 = jnp.bfloat16

D_MODEL = 1024
DEPTH = 2
N_HEADS = 4
DH = 128
D_MIX = N_HEADS * DH
N_EXPERTS = 32
EXPERTS_PER_GROUP = 8
N_GROUPS = 4
D_EXPERT = 256
CONV_B = 3
CONV_C = 4
HIST = 8
RB = 64
INV_BLOCK = 8
DN_ALPHA = (2 * DEPTH) ** 0.25
LN_EPS = 1e-5
NORM_EPS = 1e-6
NEG = -1e30

OFF_QKVC = 0
OFF_BCH = 1536
OFF_GTS = 3072
OFF_QKVO = 6144
OFF_Z = 8192
OFF_SA = 8704
OFF_SB = 8832
N_PROJ = 8960
TN_PROJ = 1280

VMEM_LIMIT = 52 * 1024 * 1024


def _dot(a, b):
    return jnp.dot(a, b, preferred_element_type=F32)


def _split3(x):
    hi = x.astype(BF16)
    r = x - hi.astype(F32)
    mid = r.astype(BF16)
    lo = (r - mid.astype(F32)).astype(BF16)
    return hi, mid, lo


def _layer_norm(x, g, b):
    mu = jnp.mean(x, axis=-1, keepdims=True)
    xc = x - mu
    var = jnp.mean(xc * xc, axis=-1, keepdims=True)
    return xc * lax.rsqrt(var + LN_EPS) * g + b


def _sigmoid(x):
    return jax.nn.sigmoid(x)


def _silu(x):
    return x * jax.nn.sigmoid(x)


def _log_sigmoid(x):
    return jnp.minimum(x, 0.0) - jnp.log1p(jnp.exp(-jnp.abs(x)))


def _softplus(x):
    return jnp.maximum(x, 0.0) + jnp.log1p(jnp.exp(-jnp.abs(x)))


def _ada_body(c_ref, w_ref, b_ref, o_ref):
    c = c_ref[...]
    s = _silu(c).astype(BF16)
    o_ref[0] = _dot(s, w_ref[0].astype(BF16)) + b_ref[0]


def _ada(c_all, w_ada, b_ada):
    nb = c_all.shape[0]
    return pl.pallas_call(
        _ada_body,
        grid=(DEPTH, 6),
        in_specs=[pl.BlockSpec((nb, D_MODEL), lambda l, j: (0, 0)),
                  pl.BlockSpec((1, D_MODEL, D_MODEL), lambda l, j: (l, 0, j)),
                  pl.BlockSpec((1, 1, D_MODEL), lambda l, j: (l, 0, j))],
        out_specs=pl.BlockSpec((1, nb, D_MODEL), lambda l, j: (l, 0, j)),
        out_shape=jax.ShapeDtypeStruct((DEPTH, nb, 6 * D_MODEL), F32),
        compiler_params=pltpu.CompilerParams(dimension_semantics=("arbitrary", "arbitrary"),
                                             vmem_limit_bytes=VMEM_LIMIT),
        name="ada",
    )(c_all, w_ada, b_ada.reshape(DEPTH, 1, 6 * D_MODEL))


def _inproj_body(x_ref, mod_ref, g_ref, b_ref, w_ref, bias_ref, proj_ref, *rest, apply_ln, bt, lt):
    x = x_ref[...]
    if apply_ln:
        x = _layer_norm(x, g_ref[...], b_ref[...])
        rest[0][...] = x
    u = (x * (1.0 + mod_ref[:, 1:2, :]) + mod_ref[:, 0:1, :]).reshape(bt * lt, D_MODEL).astype(BF16)
    for j in range(N_PROJ // TN_PROJ):
        cs = slice(j * TN_PROJ, (j + 1) * TN_PROJ)
        proj_ref[:, :, cs] = (_dot(u, w_ref[0, :, cs]) + bias_ref[0, :, cs]).reshape(bt, lt, TN_PROJ)


def _inproj(x, mod, ln_g, ln_b, w_r, b_r, layer, apply_ln, bt, lt):
    nb, lp, _ = x.shape
    tok = lambda i, t: (i, t, 0)
    out_shape = [jax.ShapeDtypeStruct((nb, lp, N_PROJ), F32)]
    out_specs = [pl.BlockSpec((bt, lt, N_PROJ), tok)]
    if apply_ln:
        out_shape.append(jax.ShapeDtypeStruct((nb, lp, D_MODEL), F32))
        out_specs.append(pl.BlockSpec((bt, lt, D_MODEL), tok))
    res = pl.pallas_call(
        functools.partial(_inproj_body, apply_ln=apply_ln, bt=bt, lt=lt),
        grid=(nb // bt, lp // lt),
        in_specs=[pl.BlockSpec((bt, lt, D_MODEL), tok),
                  pl.BlockSpec((bt, 6, D_MODEL), lambda i, t: (i, 0, 0)),
                  pl.BlockSpec((1, D_MODEL), lambda i, t: (0, 0)),
                  pl.BlockSpec((1, D_MODEL), lambda i, t: (0, 0)),
                  pl.BlockSpec((1, D_MODEL, N_PROJ), lambda i, t: (layer, 0, 0), pipeline_mode=pl.Buffered(1)),
                  pl.BlockSpec((1, 1, N_PROJ), lambda i, t: (layer, 0, 0))],
        out_specs=out_specs,
        out_shape=out_shape,
        compiler_params=pltpu.CompilerParams(dimension_semantics=("arbitrary", "arbitrary"),
                                             vmem_limit_bytes=VMEM_LIMIT),
        name=f"inproj_l{layer}_b{bt}",
    )(x, mod, ln_g, ln_b, w_r, b_r)
    return (res[0], res[1]) if apply_ln else (res[0], x)


def _conv_taps(xp_s, w_ref, width, tl):
    acc = None
    for j in range(width):
        tap = xp_s[:, pl.ds(HIST - (width - 1) + j, tl), :] * w_ref[j:j + 1, :].reshape(1, 1, -1)
        acc = tap if acc is None else acc + tap
    return acc


def _conv_history(xp_s, hist_ref, prev_ref, n_steps):
    if n_steps > 1:
        @pl.when(pl.program_id(1) == 0)
        def _():
            xp_s[:, 0:HIST, :] = hist_ref[...]

        @pl.when(pl.program_id(1) > 0)
        def _():
            xp_s[:, 0:HIST, :] = prev_ref[...]
    else:
        xp_s[:, 0:HIST, :] = hist_ref[...]


def _heads(x, nb, width):
    return jnp.stack([x[:, :, h * width:(h + 1) * width] for h in range(N_HEADS)],
                     axis=1).reshape(nb * N_HEADS, RB, width)


def _gate_cols(x, nb, lane0):
    return jnp.stack([x[:, :, lane0 + h:lane0 + h + 1] for h in range(N_HEADS)],
                     axis=1).reshape(nb * N_HEADS, RB, 1)


def _gate_rows(x, nb, lane0):
    xt = jnp.swapaxes(x, 1, 2)
    return jnp.stack([xt[:, lane0 + h:lane0 + h + 1, :] for h in range(N_HEADS)],
                     axis=1).reshape(nb * N_HEADS, 1, RB)


def _bmm(a, b):
    return jnp.einsum('nts,nsu->ntu', a.astype(BF16), b.astype(BF16), preferred_element_type=F32)


def _bmm_nt(a, b):
    return jnp.einsum('ntd,nsd->nts', a, b, preferred_element_type=F32)


def _block_masks(tl):
    rr = lax.broadcasted_iota(jnp.int32, (RB, RB), 0)
    cc = lax.broadcasted_iota(jnp.int32, (RB, RB), 1)
    incl = rr >= cc
    if tl < RB:
        incl = incl & ((rr // tl) == (cc // tl))
    return rr, cc, incl


def _seq_cumsum(x, incl, nb):
    tril = jnp.broadcast_to(incl.astype(BF16)[None], (nb, RB, RB))
    hi, mid, lo = _split3(x)
    return _bmm(tril, hi) + _bmm(tril, mid) + _bmm(tril, lo)


def _seq_last(x, nb, G, tl):
    return x.reshape(nb * G, tl, 128)[:, tl - 1:tl, :]


def _seq_rows(x3, nb, tl):
    return jnp.broadcast_to(x3, (x3.shape[0], tl, 128)).reshape(nb, RB, 128)


def _put_chains(ref, val, nb, G, tl):
    val4 = val.reshape(nb, N_HEADS, RB, val.shape[-1])
    for h in range(N_HEADS):
        ref[:, :, h] = val4[:, h].reshape(nb, G, tl, val.shape[-1])


def _put_seq_scalars(ref, x3, nb, G, lane0):
    x4 = x3.reshape(nb, G, 1, 128)
    for h in range(N_HEADS):
        ref[:, :, h] = jnp.broadcast_to(x4[:, :, :, lane0 + h:lane0 + h + 1], (nb, G, 1, 128))


def _step_tiling(nb, lp, NB, G, tl):
    assert G * tl == RB
    if G == 1:
        gx, tlx = 1, NB * RB
    else:
        assert lp == tl
        gx, tlx = NB * G, tl
    return gx, tlx, nb // gx, lp // tlx


def _layered_state(layer, nb, NS, prev):
    st_in = pl.BlockSpec((None, NS, N_HEADS, DH, DH), lambda i, c: (layer, i, 0, 0, 0))
    shape = jax.ShapeDtypeStruct((DEPTH, nb, N_HEADS, DH, DH), F32)
    if layer == 0:
        return st_in, pl.BlockSpec((DEPTH, NS, N_HEADS, DH, DH), lambda i, c: (0, i, 0, 0, 0)), shape, [], []
    return st_in, st_in, shape, [pl.BlockSpec(memory_space=pl.ANY)], [prev]


def _fill_later_layers(ref, when):
    @pl.when(when)
    def _():
        for l in range(1, DEPTH):
            ref[l] = ref[0]


def _mlstm_prep_body(qkvo_ref, bch_ref, prev_ref, sa_ref, sb_ref, cv0_ref, cw_ref,
                     yb_ref, cv_ref, nv_ref, q_ref, kw_ref, v_ref, cols_ref, bl_ref, bc_ref, kn_ref, xp_s,
                     *, NB, G, tl, tlx, NCS, lv):
    cs = pl.program_id(1)
    n = NB * N_HEADS
    lvl = lv - (NCS - 1) * tlx

    if NCS > 1:
        @pl.when(cs == 0)
        def _():
            xp_s[:, 0:HIST, :] = cv0_ref[...]

        @pl.when(cs > 0)
        def _():
            xp_s[:, 0:HIST, :] = prev_ref[:, :, D_MIX:2 * D_MIX] * prev_ref[:, :, 2 * D_MIX:3 * D_MIX]
    else:
        xp_s[:, 0:HIST, :] = cv0_ref[...]
    xp_s[:, HIST:HIST + tlx, :] = bch_ref[:, :, D_MIX:2 * D_MIX] * bch_ref[:, :, 2 * D_MIX:3 * D_MIX]
    yb_ref[...] = (bch_ref[:, :, 0:D_MIX] * _conv_taps(xp_s, cw_ref, CONV_B, tlx)).astype(BF16)

    @pl.when(cs == NCS - 1)
    def _():
        cv_ref[...] = xp_s[:, pl.ds(HIST + lvl - (CONV_B - 1), CONV_B - 1), :]

    i_all = sa_ref[...].reshape(NB, RB, 128)
    f_all = _log_sigmoid(sb_ref[...].reshape(NB, RB, 128))
    if lv < NCS * tlx:
        assert NCS == 1
        valid = (lax.broadcasted_iota(jnp.int32, (NB, RB, 128), 1) % tl) < lv
        i_all = jnp.where(valid, i_all, NEG)
        f_all = jnp.where(valid, f_all, 0.0)
    _, _, incl = _block_masks(tl)
    bcum = _seq_cumsum(f_all, incl, NB)
    blast = _seq_last(bcum, NB, G, tl)
    val = _seq_rows(blast, NB, tl) - bcum + i_all
    bmax = jnp.max(val.reshape(NB * G, tl, 128), axis=1, keepdims=True)
    wk0 = jnp.exp(val - _seq_rows(bmax, NB, tl))

    qkvo = qkvo_ref[...].reshape(NB, RB, 4 * D_MIX)
    q = _heads(qkvo[:, :, 0:D_MIX], NB, DH)
    k = _heads(qkvo[:, :, D_MIX:2 * D_MIX], NB, DH) * (DH ** -0.5)
    v = _heads(qkvo[:, :, 2 * D_MIX:3 * D_MIX], NB, DH)
    qb, kb, vb = q.astype(BF16), k.astype(BF16), v.astype(BF16)

    b_col = _gate_cols(bcum, NB, 0)
    dlog = jnp.where(incl[None], b_col - _gate_rows(bcum, NB, 0) + _gate_rows(i_all, NB, 0), NEG)
    d = jnp.max(dlog, axis=-1, keepdims=True)
    s0 = _bmm_nt(qb, kb) * jnp.exp(dlog - d)
    kw0 = k * _gate_cols(wk0, NB, 0)

    _put_chains(nv_ref, _bmm(s0, vb), NB, G, tl)
    _put_chains(q_ref, qb, NB, G, tl)
    _put_chains(kw_ref, kw0.astype(BF16), NB, G, tl)
    _put_chains(v_ref, vb, NB, G, tl)
    _put_chains(cols_ref, jnp.concatenate([d, b_col, jnp.sum(s0, axis=-1, keepdims=True),
                                           jnp.zeros((n, RB, 5), F32)], axis=-1), NB, G, tl)
    _put_seq_scalars(bl_ref, blast, NB, G, 0)
    _put_seq_scalars(bc_ref, bmax, NB, G, 0)
    kn = jnp.sum(kw0.reshape(n * G, tl, DH), axis=1, keepdims=True).reshape(NB, N_HEADS, G, 1, DH)
    for h in range(N_HEADS):
        kn_ref[:, :, h] = kn[:, h]


def _mlstm_prep(proj, cv0, cw, NB, G, tl, lv):
    nb, lp, _ = proj.shape
    gx, tlx, NI, NCS = _step_tiling(nb, lp, NB, G, tl)
    nbt = nb * lp // RB
    step = lambda i, c: (i * NCS + c, 0, 0, 0, 0)
    chain = lambda last, dt: jax.ShapeDtypeStruct((nbt, G, N_HEADS, tl, last), dt)
    cspec = lambda last: pl.BlockSpec((NB, G, N_HEADS, tl, last), step)
    scal = jax.ShapeDtypeStruct((nbt, G, N_HEADS, 1, 128), F32)
    sspec = pl.BlockSpec((NB, G, N_HEADS, 1, 128), step)
    bch = OFF_BCH // (3 * D_MIX)
    return pl.pallas_call(
        functools.partial(_mlstm_prep_body, NB=NB, G=G, tl=tl, tlx=tlx, NCS=NCS, lv=lv),
        grid=(NI, NCS),
        in_specs=[pl.BlockSpec((gx, tlx, 4 * D_MIX), lambda i, c: (i, c, OFF_QKVO // (4 * D_MIX))),
                  pl.BlockSpec((gx, tlx, 3 * D_MIX), lambda i, c: (i, c, bch)),
                  pl.BlockSpec((gx, HIST, 3 * D_MIX), lambda i, c: (i, jnp.maximum(c * (tlx // HIST) - 1, 0), bch)),
                  pl.BlockSpec((gx, tlx, 128), lambda i, c: (i, c, OFF_SA // 128)),
                  pl.BlockSpec((gx, tlx, 128), lambda i, c: (i, c, OFF_SB // 128)),
                  pl.BlockSpec((gx, HIST, D_MIX), lambda i, c: (i, 0, 0)),
                  pl.BlockSpec((8, D_MIX), lambda i, c: (0, 0))],
        out_specs=[pl.BlockSpec((gx, tlx, D_MIX), lambda i, c: (i, c, 0)),
                   pl.BlockSpec((gx, CONV_B - 1, D_MIX), lambda i, c: (i, 0, 0)),
                   cspec(DH), cspec(DH), cspec(DH), cspec(DH), cspec(8), sspec, sspec, sspec],
        out_shape=[jax.ShapeDtypeStruct((nb, lp, D_MIX), BF16),
                   jax.ShapeDtypeStruct((nb, CONV_B - 1, D_MIX), F32),
                   chain(DH, F32), chain(DH, BF16), chain(DH, BF16), chain(DH, BF16), chain(8, F32),
                   scal, scal, scal],
        scratch_shapes=[pltpu.VMEM((gx, HIST + tlx, D_MIX), F32)],
        compiler_params=pltpu.CompilerParams(dimension_semantics=("arbitrary", "arbitrary"),
                                             vmem_limit_bytes=VMEM_LIMIT),
        name=f"mlstm_prep_g{G}",
    )(proj, proj, proj, proj, proj, cv0, cw)


def _mlstm_scan_body(nv_ref, q_ref, kw_ref, v_ref, cols_ref, bl_ref, bc_ref, kn_ref, o_ref, C0_ref, n0_ref, m0_ref,
                     ng_ref, *rest, NS, tl, NC, first):
    ya_ref, C_out, n_ref, m_ref = rest[-4:]
    C_ref = C_out.at[0] if first else C_out
    c = pl.program_id(1)
    n = NS * N_HEADS

    @pl.when(c == 0)
    def _():
        C_ref[...] = C0_ref[...]
        n_ref[...] = n0_ref[...]
        m_ref[...] = m0_ref[...]

    chains = lambda ref: ref[...].reshape(n, ref.shape[-2], ref.shape[-1])
    C = C_ref[...].reshape(n, DH, DH)
    nvec = n_ref[...].reshape(n, 1, DH)
    m_prev = chains(m_ref)[:, :, 0:1]
    cols = chains(cols_ref)
    d, b, ds0 = cols[:, :, 0:1], cols[:, :, 1:2], cols[:, :, 2:3]
    qb = chains(q_ref)

    m_inter = b + m_prev
    m_t = jnp.maximum(m_inter, d)
    f = jnp.exp(d - m_t)
    inter = jnp.exp(m_inter - m_t)
    qC = jnp.einsum('ntd,nde->nte', qb, C.astype(BF16), preferred_element_type=F32)
    qn = jnp.sum(qb.astype(F32) * nvec, axis=-1, keepdims=True)
    num = f * chains(nv_ref) + inter * qC
    den = f * ds0 + inter * qn
    hh = num / jnp.maximum(jnp.abs(den), jnp.exp(-m_t))

    m_new = m_t[:, tl - 1:tl, :]
    decay = jnp.exp(chains(bl_ref)[:, :, 0:1] + m_prev - m_new)
    scale = jnp.exp(chains(bc_ref)[:, :, 0:1] - m_new)
    kv = jnp.einsum('ntd,nte->nde', chains(kw_ref), chains(v_ref), preferred_element_type=F32)
    C_ref[...] = (decay * C + scale * kv).reshape(NS, N_HEADS, DH, DH)
    n_ref[...] = (decay * nvec + scale * chains(kn_ref)).reshape(NS, N_HEADS, DH)
    m_ref[...] = jnp.broadcast_to(m_new, (n, 1, 128)).reshape(NS, N_HEADS, 1, 128)
    if first:
        _fill_later_layers(C_out, c == NC - 1)

    mu = jnp.mean(hh, axis=-1, keepdims=True)
    hc = hh - mu
    hn = (hc * lax.rsqrt(jnp.mean(hc * hc, axis=-1, keepdims=True) + LN_EPS)).reshape(NS, N_HEADS, tl, DH)
    for h in range(N_HEADS):
        hs = slice(h * DH, (h + 1) * DH)
        ya_ref[:, :, hs] = (_sigmoid(o_ref[:, :, hs]) * hn[:, h] * ng_ref[:, hs]).astype(BF16)


def _mlstm_scan(pre, proj, C0, n0, m0, ng, SB, G, tl, layer, C_prev):
    nb, lp, _ = proj.shape
    NS = SB * G
    NI, NC = nb // NS, lp // tl
    six = lambda a: a.reshape((NI * SB, NC) + a.shape[1:])
    spec6 = lambda a: pl.BlockSpec((SB, 1) + a.shape[1:], lambda i, c: (i, c, 0, 0, 0, 0))
    seq4 = lambda i, c: (i, 0, 0, 0)
    st_in, st_out, st_shape, extra_in, extra_args = _layered_state(layer, nb, NS, C_prev)
    n_in = len(pre) + 5
    return pl.pallas_call(
        functools.partial(_mlstm_scan_body, NS=NS, tl=tl, NC=NC, first=layer == 0),
        grid=(NI, NC),
        in_specs=[spec6(a) for a in pre] + [
            pl.BlockSpec((NS, tl, D_MIX), lambda i, c: (i, c, (OFF_QKVO + 3 * D_MIX) // D_MIX)),
            st_in,
            pl.BlockSpec((NS, N_HEADS, DH), lambda i, c: (i, 0, 0)),
            pl.BlockSpec((NS, N_HEADS, 1, 128), seq4),
            pl.BlockSpec((1, D_MIX), lambda i, c: (0, 0))] + extra_in,
        out_specs=[pl.BlockSpec((NS, tl, D_MIX), lambda i, c: (i, c, 0)),
                   st_out,
                   pl.BlockSpec((NS, N_HEADS, DH), lambda i, c: (i, 0, 0)),
                   pl.BlockSpec((NS, N_HEADS, 1, 128), seq4)],
        out_shape=[jax.ShapeDtypeStruct((nb, lp, D_MIX), BF16),
                   st_shape,
                   jax.ShapeDtypeStruct((nb, N_HEADS, DH), F32),
                   jax.ShapeDtypeStruct((nb, N_HEADS, 1, 128), F32)],
        input_output_aliases={n_in: 1} if extra_in else {},
        compiler_params=pltpu.CompilerParams(dimension_semantics=("arbitrary", "arbitrary"),
                                             vmem_limit_bytes=VMEM_LIMIT),
        name=f"mlstm_scan_g{G}",
    )(*[six(a) for a in pre], proj, C0, n0, m0, ng, *extra_args)


def _unit_lower_inverse(n, rr, cc, tl):
    eye = (rr == cc).astype(F32)[None]
    p = jnp.where(((rr // INV_BLOCK) == (cc // INV_BLOCK))[None], n, 0.0)
    x = eye + p
    b = 2
    while b < INV_BLOCK:
        p = _bmm(p, p)
        x = x + _bmm(x, p)
        b *= 2
    b = INV_BLOCK
    while b < tl:
        off = jnp.where((((rr // (2 * b)) == (cc // (2 * b))) & ((rr // b) != (cc // b)))[None], n, 0.0)
        x = x + _bmm(x, _bmm(off, x))
        b *= 2
    return x


def _gdn_prep_body(x_ref, prev_ref, sa_ref, sb_ref, gc0_ref, cw_ref, alog_ref, dtb_ref,
                   u_ref, w_ref, qg_ref, kd_ref, qkm_ref, eg_ref, gcs_ref, xp_s, *, NB, G, tl, tlx, NCS, lv):
    cs = pl.program_id(1)
    lvl = lv - (NCS - 1) * tlx

    _conv_history(xp_s, gc0_ref, prev_ref, NCS)
    xp_s[:, HIST:HIST + tlx, :] = x_ref[...]
    qkv = _silu(_conv_taps(xp_s, cw_ref, CONV_C, tlx)).reshape(NB, RB, 3 * D_MIX)

    @pl.when(cs == NCS - 1)
    def _():
        gcs_ref[...] = xp_s[:, pl.ds(HIST + lvl - (CONV_C - 1), CONV_C - 1), :]

    beta_all = _sigmoid(sa_ref[...].reshape(NB, RB, 128))
    g_all = -jnp.exp(alog_ref[...]) * _softplus(sb_ref[...].reshape(NB, RB, 128) + dtb_ref[...])
    if lv < NCS * tlx:
        assert NCS == 1
        valid = (lax.broadcasted_iota(jnp.int32, (NB, RB, 128), 1) % tl) < lv
        beta_all = jnp.where(valid, beta_all, 0.0)
        g_all = jnp.where(valid, g_all, 0.0)
    rr, cc, incl = _block_masks(tl)
    diag = rr == cc
    gam = _seq_cumsum(g_all, incl, NB)
    glast = _seq_last(gam, NB, G, tl)
    gcol = _gate_cols(gam, NB, 4)
    bcol = _gate_cols(beta_all, NB, 4)
    egcol = _gate_cols(jnp.exp(gam), NB, 4)
    kdcol = _gate_cols(jnp.exp(_seq_rows(glast, NB, tl) - gam), NB, 4)

    q = _heads(qkv[:, :, 0:D_MIX], NB, DH)
    k = _heads(qkv[:, :, D_MIX:2 * D_MIX], NB, DH)
    v = _heads(qkv[:, :, 2 * D_MIX:3 * D_MIX], NB, DH)
    q = q * lax.rsqrt(jnp.sum(q * q, axis=-1, keepdims=True) + NORM_EPS) * (DH ** -0.5)
    k = k * lax.rsqrt(jnp.sum(k * k, axis=-1, keepdims=True) + NORM_EPS)
    qb, kb = q.astype(BF16), k.astype(BF16)

    dmat = jnp.exp(jnp.where(incl[None], gcol - _gate_rows(gam, NB, 4), NEG))
    nmat = jnp.where(diag[None], 0.0, -(bcol * _bmm_nt(kb, kb) * dmat))
    rhs = jnp.concatenate([bcol * v, (bcol * egcol) * k], axis=-1)
    sol = _bmm(_unit_lower_inverse(nmat, rr, cc, tl), rhs)
    qkm = (_bmm_nt(qb, kb) * dmat).astype(BF16).reshape(NB, N_HEADS, RB, RB)

    _put_chains(u_ref, sol[:, :, 0:DH], NB, G, tl)
    _put_chains(w_ref, sol[:, :, DH:2 * DH].astype(BF16), NB, G, tl)
    _put_chains(qg_ref, (q * egcol).astype(BF16), NB, G, tl)
    _put_chains(kd_ref, (k * kdcol).astype(BF16), NB, G, tl)
    _put_seq_scalars(eg_ref, jnp.exp(glast), NB, G, 4)
    for h in range(N_HEADS):
        for g in range(G):
            qkm_ref[:, g, h] = qkm[:, h, g * tl:(g + 1) * tl, g * tl:(g + 1) * tl]


def _gdn_prep(proj, gc0, cw, alog, dtb, NB, G, tl, lv):
    nb, lp, _ = proj.shape
    assert tl % INV_BLOCK == 0 and (tl // INV_BLOCK) & (tl // INV_BLOCK - 1) == 0
    gx, tlx, NI, NCS = _step_tiling(nb, lp, NB, G, tl)
    nbt = nb * lp // RB
    step = lambda i, c: (i * NCS + c, 0, 0, 0, 0)
    par = lambda i, c: (0, 0)
    chain = lambda last, dt: jax.ShapeDtypeStruct((nbt, G, N_HEADS, tl, last), dt)
    cspec = lambda last: pl.BlockSpec((NB, G, N_HEADS, tl, last), step)
    return pl.pallas_call(
        functools.partial(_gdn_prep_body, NB=NB, G=G, tl=tl, tlx=tlx, NCS=NCS, lv=lv),
        grid=(NI, NCS),
        in_specs=[pl.BlockSpec((gx, tlx, 3 * D_MIX), lambda i, c: (i, c, OFF_QKVC // (3 * D_MIX))),
                  pl.BlockSpec((gx, HIST, 3 * D_MIX),
                               lambda i, c: (i, jnp.maximum(c * (tlx // HIST) - 1, 0), OFF_QKVC // (3 * D_MIX))),
                  pl.BlockSpec((gx, tlx, 128), lambda i, c: (i, c, OFF_SA // 128)),
                  pl.BlockSpec((gx, tlx, 128), lambda i, c: (i, c, OFF_SB // 128)),
                  pl.BlockSpec((gx, HIST, 3 * D_MIX), lambda i, c: (i, 0, 0)),
                  pl.BlockSpec((8, 3 * D_MIX), par),
                  pl.BlockSpec((1, 128), par),
                  pl.BlockSpec((1, 128), par)],
        out_specs=[cspec(DH), cspec(DH), cspec(DH), cspec(DH), cspec(tl),
                   pl.BlockSpec((NB, G, N_HEADS, 1, 128), step),
                   pl.BlockSpec((gx, CONV_C - 1, 3 * D_MIX), lambda i, c: (i, 0, 0))],
        out_shape=[chain(DH, F32), chain(DH, BF16), chain(DH, BF16), chain(DH, BF16), chain(tl, BF16),
                   jax.ShapeDtypeStruct((nbt, G, N_HEADS, 1, 128), F32),
                   jax.ShapeDtypeStruct((nb, CONV_C - 1, 3 * D_MIX), F32)],
        scratch_shapes=[pltpu.VMEM((gx, HIST + tlx, 3 * D_MIX), F32)],
        compiler_params=pltpu.CompilerParams(dimension_semantics=("arbitrary", "arbitrary"),
                                             vmem_limit_bytes=VMEM_LIMIT),
        name=f"gdn_prep_g{G}",
    )(proj, proj, proj, proj, gc0, cw, alog, dtb)


def _gdn_scan_body(u_ref, w_ref, qg_ref, kd_ref, qkm_ref, eg_ref, z_ref, S0_ref, gng_ref, *rest,
                   NS, tl, NC, first):
    yc_ref, S_out = rest[-2:]
    S_ref = S_out.at[0] if first else S_out
    c = pl.program_id(1)
    n = NS * N_HEADS

    @pl.when(c == 0)
    def _():
        S_ref[...] = S0_ref[...]

    S = S_ref[...].reshape(n, DH, DH)
    Sb = S.astype(BF16)
    chains = lambda ref: ref[...].reshape(n, tl, ref.shape[-1])
    v_new = chains(u_ref) - jnp.einsum('ntd,nde->nte', chains(w_ref), Sb, preferred_element_type=F32)
    vnb = v_new.astype(BF16)
    o = (jnp.einsum('ntd,nde->nte', chains(qg_ref), Sb, preferred_element_type=F32)
         + jnp.einsum('nts,nse->nte', chains(qkm_ref), vnb, preferred_element_type=F32))
    eg = eg_ref[...].reshape(n, 1, 128)[:, :, 0:1]
    S_new = eg * S + jnp.einsum('ntd,nte->nde', chains(kd_ref), vnb, preferred_element_type=F32)
    S_ref[...] = S_new.reshape(NS, N_HEADS, DH, DH)
    if first:
        _fill_later_layers(S_out, c == NC - 1)

    on = (o * lax.rsqrt(jnp.mean(o * o, axis=-1, keepdims=True) + NORM_EPS) * gng_ref[...]).reshape(NS, N_HEADS, tl, DH)
    for h in range(N_HEADS):
        yc_ref[:, :, h * DH:(h + 1) * DH] = (on[:, h] * _silu(z_ref[:, :, h * DH:(h + 1) * DH])).astype(BF16)


def _gdn_scan(pre, proj, S0, gng, SB, G, tl, layer, S_prev):
    nb, lp, _ = proj.shape
    NS = SB * G
    NI, NC = nb // NS, lp // tl
    six = lambda a: a.reshape((NI * SB, NC) + a.shape[1:])
    cspec = lambda last: pl.BlockSpec((SB, 1, G, N_HEADS, tl, last), lambda i, c: (i, c, 0, 0, 0, 0))
    u, w, qg, kd, qkm, eg = (six(a) for a in pre)
    st_in, st_out, st_shape, extra_in, extra_args = _layered_state(layer, nb, NS, S_prev)
    return pl.pallas_call(
        functools.partial(_gdn_scan_body, NS=NS, tl=tl, NC=NC, first=layer == 0),
        grid=(NI, NC),
        in_specs=[cspec(DH), cspec(DH), cspec(DH), cspec(DH), cspec(tl),
                  pl.BlockSpec((SB, 1, G, N_HEADS, 1, 128), lambda i, c: (i, c, 0, 0, 0, 0)),
                  pl.BlockSpec((NS, tl, D_MIX), lambda i, c: (i, c, OFF_Z // D_MIX)),
                  st_in,
                  pl.BlockSpec((1, DH), lambda i, c: (0, 0))] + extra_in,
        out_specs=[pl.BlockSpec((NS, tl, D_MIX), lambda i, c: (i, c, 0)), st_out],
        out_shape=[jax.ShapeDtypeStruct((nb, lp, D_MIX), BF16), st_shape],
        input_output_aliases={9: 1} if extra_in else {},
        compiler_params=pltpu.CompilerParams(dimension_semantics=("arbitrary", "arbitrary"),
                                             vmem_limit_bytes=VMEM_LIMIT),
        name=f"gdn_scan_g{G}",
    )(u, w, qg, kd, qkm, eg, proj, S0, gng, *extra_args)


def _route(rl):
    lane = lax.broadcasted_iota(jnp.int32, rl.shape, 1).astype(F32)
    is_g = (lane >= N_EXPERTS) & (lane < N_EXPERTS + N_GROUPS)
    gl = jnp.where(is_g, rl, NEG)
    gmax = jnp.max(gl, axis=-1, keepdims=True)
    grp = jnp.min(jnp.where(gl == gmax, lane - N_EXPERTS, 4.0 * N_EXPERTS), axis=-1, keepdims=True)
    p_grp = 1.0 / jnp.sum(jnp.where(is_g, jnp.exp(gl - gmax), 0.0), axis=-1, keepdims=True)
    lo = grp * EXPERTS_PER_GROUP
    in_grp = (lane >= lo) & (lane < lo + EXPERTS_PER_GROUP)
    el = jnp.where(in_grp, rl, NEG)
    m1 = jnp.max(el, axis=-1, keepdims=True)
    i1 = jnp.min(jnp.where(el == m1, lane, 4.0 * N_EXPERTS), axis=-1, keepdims=True)
    el2 = jnp.where(lane == i1, NEG, el)
    m2 = jnp.max(el2, axis=-1, keepdims=True)
    i2 = jnp.min(jnp.where(el2 == m2, lane, 4.0 * N_EXPERTS), axis=-1, keepdims=True)
    e2 = jnp.exp(m2 - m1)
    w1 = p_grp / (1.0 + e2)
    w2 = p_grp * e2 / (1.0 + e2)
    return i1, i2, w1, w2


def _merge_body(ya_ref, yb_ref, yc_ref, ga_ref, gb_ref, gc_ref, x_ref, mod_ref, wa_ref, wb_ref, wc_ref, wo_ref,
                lg_ref, lb_ref, wrh_ref, wrl_ref, br_ref, x1_ref, rinfo_ref, cnt_ref, *, bt, lt):
    tm = bt * lt

    @pl.when((pl.program_id(0) == 0) & (pl.program_id(1) == 0))
    def _():
        cnt_ref[...] = jnp.zeros_like(cnt_ref)

    def r2(ref):
        return ref[...].reshape(tm, ref.shape[-1])

    merged = (_sigmoid(r2(ga_ref)) * _dot(r2(ya_ref), wa_ref[0])
              + _sigmoid(r2(gb_ref)) * _dot(r2(yb_ref), wb_ref[0])
              + _sigmoid(r2(gc_ref)) * _dot(r2(yc_ref), wc_ref[0]))
    out = _dot(merged.astype(BF16), wo_ref[0])
    y = DN_ALPHA * x_ref[...] + (1.0 + mod_ref[:, 2:3, :]) * out.reshape(bt, lt, D_MODEL)
    x1 = _layer_norm(y, lg_ref[0], lb_ref[0])
    x1_ref[...] = x1
    u2 = (x1 * (1.0 + mod_ref[:, 4:5, :]) + mod_ref[:, 3:4, :]).reshape(tm, D_MODEL)
    hi = u2.astype(BF16)
    lo = (u2 - hi.astype(F32)).astype(BF16)
    rl = _dot(hi, wrh_ref[0]) + _dot(lo, wrh_ref[0]) + _dot(hi, wrl_ref[0]) + br_ref[0]
    i1, i2, w1, w2 = _route(rl)
    lane = lax.broadcasted_iota(jnp.int32, (tm, 128), 1).astype(F32)
    onehot = jnp.where((lane == i1) | (lane == i2), 1.0, 0.0)
    rr = lax.broadcasted_iota(jnp.int32, (tm, tm), 0)
    cc = lax.broadcasted_iota(jnp.int32, (tm, tm), 1)
    before = _dot((rr > cc).astype(BF16), onehot.astype(BF16)) + cnt_ref[0:1, :]
    rank1 = jnp.sum(jnp.where(lane == i1, before, 0.0), axis=-1, keepdims=True)
    rank2 = jnp.sum(jnp.where(lane == i2, before, 0.0), axis=-1, keepdims=True)
    cnt_ref[0:1, :] += jnp.sum(onehot, axis=0, keepdims=True)
    rinfo = jnp.zeros((tm, 128), F32)
    for k, val in enumerate((i1, i2, w1, w2, rank1, rank2)):
        rinfo = jnp.where(lane == k, val, rinfo)
    rinfo_ref[...] = rinfo.reshape(bt, lt, 128)


def _merge(ya, yb, yc, proj, x, mod, wa, wb, wc, wo, lg, lb, wrh, wrl, br, layer, bt, lt):
    nb, lp, _ = x.shape
    tok = lambda i, t: (i, t, 0)
    wsp = lambda shape: pl.BlockSpec((1,) + shape, lambda i, t: (layer, 0, 0))
    g0 = OFF_GTS // D_MODEL
    return pl.pallas_call(
        functools.partial(_merge_body, bt=bt, lt=lt),
        grid=(nb // bt, lp // lt),
        in_specs=[pl.BlockSpec((bt, lt, D_MIX), tok),
                  pl.BlockSpec((bt, lt, D_MIX), tok),
                  pl.BlockSpec((bt, lt, D_MIX), tok),
                  pl.BlockSpec((bt, lt, D_MODEL), lambda i, t: (i, t, g0)),
                  pl.BlockSpec((bt, lt, D_MODEL), lambda i, t: (i, t, g0 + 1)),
                  pl.BlockSpec((bt, lt, D_MODEL), lambda i, t: (i, t, g0 + 2)),
                  pl.BlockSpec((bt, lt, D_MODEL), tok),
                  pl.BlockSpec((bt, 6, D_MODEL), lambda i, t: (i, 0, 0)),
                  wsp((D_MIX, D_MODEL)), wsp((D_MIX, D_MODEL)), wsp((D_MIX, D_MODEL)), wsp((D_MODEL, D_MODEL)),
                  wsp((1, D_MODEL)), wsp((1, D_MODEL)),
                  wsp((D_MODEL, 128)), wsp((D_MODEL, 128)), wsp((1, 128))],
        out_specs=[pl.BlockSpec((bt, lt, D_MODEL), tok),
                   pl.BlockSpec((bt, lt, 128), tok),
                   pl.BlockSpec((8, 128), lambda i, t: (0, 0))],
        out_shape=[jax.ShapeDtypeStruct((nb, lp, D_MODEL), F32),
                   jax.ShapeDtypeStruct((nb, lp, 128), F32),
                   jax.ShapeDtypeStruct((8, 128), F32)],
        compiler_params=pltpu.CompilerParams(dimension_semantics=("arbitrary", "arbitrary"),
                                             vmem_limit_bytes=VMEM_LIMIT),
        name=f"merge_l{layer}_b{bt}",
    )(ya, yb, yc, proj, proj, proj, x, mod, wa, wb, wc, wo, lg, lb, wrh, wrl, br)


def _moe_plan(rinfo, cnt, tm, mb):
    nb, lp, _ = rinfo.shape
    n_tok = nb * lp
    n_blocks = 2 * n_tok // mb + N_EXPERTS
    counts = cnt[0, :N_EXPERTS].astype(jnp.int32)
    nblk = (counts + mb - 1) // mb
    pend = jnp.cumsum(nblk)
    pstart = pend - nblk
    expert = rinfo[..., 0:2].astype(jnp.int32)
    rank = rinfo[..., 4:6].astype(jnp.int32)
    ids = jnp.arange(N_EXPERTS, dtype=jnp.int32)
    first_row = jnp.sum(jnp.where(expert[..., None] == ids, pstart * mb, 0), axis=-1)
    pos = (first_row + rank).reshape(n_tok // tm, 1, 2 * tm)
    block_e = jnp.sum(pend[None, :] <= jnp.arange(n_blocks, dtype=jnp.int32)[:, None], axis=1)
    block_e = jnp.minimum(block_e, N_EXPERTS - 1).astype(jnp.int32)
    n_used = pend[N_EXPERTS - 1:].astype(jnp.int32)
    ztail = jnp.where(nblk > 0, (pend - 1) * mb, -1).astype(jnp.int32)
    ztail = jnp.concatenate([ztail, n_used]).reshape(1, 1, N_EXPERTS + 1)
    return pos, block_e, n_used, ztail, n_blocks


def _row_copy(src, dst, sem):
    return pltpu.make_async_copy(src, dst, sem)


def _dispatch_body(pos_ref, ztail_ref, x1_ref, mod_ref, disp_ref, u_scr, z_scr, sem, zsem, *, bt, lt, n_blocks, mb):
    tm = bt * lt

    @pl.when((pl.program_id(0) == 0) & (pl.program_id(1) == 0))
    def _():
        z_scr[...] = jnp.zeros_like(z_scr)
        for e in range(N_EXPERTS):
            @pl.when(ztail_ref[0, 0, e] >= 0)
            def _():
                row = pl.multiple_of(ztail_ref[0, 0, e], mb)
                _row_copy(z_scr, disp_ref.at[pl.ds(row, mb)], zsem).start()
        def unused(j):
            return _row_copy(z_scr, disp_ref.at[pl.ds(pl.multiple_of(j * mb, mb), mb)], zsem)

        n_used = ztail_ref[0, 0, N_EXPERTS]
        lax.fori_loop(n_used, n_blocks, lambda j, c: (unused(j).start(), c)[1], 0)
        lax.fori_loop(n_used, n_blocks, lambda j, c: (unused(j).wait(), c)[1], 0)
        for e in range(N_EXPERTS):
            @pl.when(ztail_ref[0, 0, e] >= 0)
            def _():
                row = pl.multiple_of(ztail_ref[0, 0, e], mb)
                _row_copy(z_scr, disp_ref.at[pl.ds(row, mb)], zsem).wait()

    u2 = x1_ref[...] * (1.0 + mod_ref[:, 4:5, :]) + mod_ref[:, 3:4, :]
    u_scr[...] = u2.reshape(tm, D_MODEL)

    def issue(t, carry):
        for k in range(2):
            _row_copy(u_scr.at[pl.ds(t, 1)], disp_ref.at[pl.ds(pos_ref[0, 0, 2 * t + k], 1)], sem).start()
        return carry

    lax.fori_loop(0, tm, issue, 0, unroll=8)
    for k in range(2):
        _row_copy(u_scr, disp_ref.at[pl.ds(0, tm)], sem).wait()


def _dispatch(x1, mod, pos, ztail, n_blocks, mb, bt, lt):
    nb, lp, _ = x1.shape
    nt = lp // lt
    return pl.pallas_call(
        functools.partial(_dispatch_body, bt=bt, lt=lt, n_blocks=n_blocks, mb=mb),
        grid=(nb // bt, nt),
        in_specs=[pl.BlockSpec((1, 1, 2 * bt * lt), lambda i, t: (i * nt + t, 0, 0), memory_space=pltpu.SMEM),
                  pl.BlockSpec((1, 1, N_EXPERTS + 1), lambda i, t: (0, 0, 0), memory_space=pltpu.SMEM),
                  pl.BlockSpec((bt, lt, D_MODEL), lambda i, t: (i, t, 0)),
                  pl.BlockSpec((bt, 6, D_MODEL), lambda i, t: (i, 0, 0))],
        out_specs=pl.BlockSpec(memory_space=pl.ANY),
        out_shape=jax.ShapeDtypeStruct((n_blocks * mb, D_MODEL), F32),
        scratch_shapes=[pltpu.VMEM((bt * lt, D_MODEL), F32), pltpu.VMEM((mb, D_MODEL), F32),
                        pltpu.SemaphoreType.DMA, pltpu.SemaphoreType.DMA],
        compiler_params=pltpu.CompilerParams(dimension_semantics=("arbitrary", "arbitrary"),
                                             vmem_limit_bytes=VMEM_LIMIT),
        name=f"dispatch_b{bt}",
    )(pos, ztail, x1, mod)


def _experts_body(be_ref, nu_ref, x_ref, wg_ref, wu_ref, wd_ref, o_ref):
    j = pl.program_id(0)

    @pl.when(j < nu_ref[0])
    def _():
        x = x_ref[...].astype(BF16)
        hb = _silu(_dot(x, wg_ref[0, 0].astype(BF16))) * _dot(x, wu_ref[0, 0].astype(BF16))
        o_ref[...] = _dot(hb.astype(BF16), wd_ref[0, 0].astype(BF16))

    @pl.when(j >= nu_ref[0])
    def _():
        o_ref[...] = jnp.zeros_like(o_ref)


def _experts(disp, block_e, n_used, wg, wu, wd, layer, mb):
    n_blocks = disp.shape[0] // mb
    wmap = lambda j, be, nu: (layer, be[j], 0, 0)
    return pl.pallas_call(
        _experts_body,
        grid_spec=pltpu.PrefetchScalarGridSpec(
            num_scalar_prefetch=2,
            grid=(n_blocks,),
            in_specs=[pl.BlockSpec((mb, D_MODEL), lambda j, be, nu: (jnp.minimum(j, nu[0] - 1), 0)),
                      pl.BlockSpec((1, 1, D_MODEL, D_EXPERT), wmap),
                      pl.BlockSpec((1, 1, D_MODEL, D_EXPERT), wmap),
                      pl.BlockSpec((1, 1, D_EXPERT, D_MODEL), wmap)],
            out_specs=pl.BlockSpec((mb, D_MODEL), lambda j, be, nu: (j, 0))),
        out_shape=jax.ShapeDtypeStruct(disp.shape, F32),
        compiler_params=pltpu.CompilerParams(dimension_semantics=("arbitrary",), vmem_limit_bytes=VMEM_LIMIT),
        name=f"experts_l{layer}_n{n_blocks}",
    )(block_e, n_used, disp, wg, wu, wd)


def _combine_body(pos_ref, x1_ref, mod_ref, rinfo_ref, eo_ref, lg_ref, lb_ref, x2_ref, r_scr, sem, *, bt, lt):
    tm = bt * lt

    def issue(t, carry):
        for k in range(2):
            _row_copy(eo_ref.at[pl.ds(pos_ref[0, 0, 2 * t + k], 1)], r_scr.at[k, pl.ds(t, 1)], sem).start()
        return carry

    lax.fori_loop(0, tm, issue, 0, unroll=8)
    for k in range(2):
        _row_copy(eo_ref.at[pl.ds(0, tm)], r_scr.at[k], sem).wait()

    rinfo = rinfo_ref[...].reshape(tm, 128)
    moe = r_scr[0] * rinfo[:, 2:3] + r_scr[1] * rinfo[:, 3:4]
    y = DN_ALPHA * x1_ref[...] + (1.0 + mod_ref[:, 5:6, :]) * moe.reshape(bt, lt, D_MODEL)
    x2_ref[...] = _layer_norm(y, lg_ref[0], lb_ref[0])


def _combine(x1, mod, rinfo, eo, pos, lg, lb, layer, bt, lt):
    nb, lp, _ = x1.shape
    nt = lp // lt
    tok = lambda i, t: (i, t, 0)
    return pl.pallas_call(
        functools.partial(_combine_body, bt=bt, lt=lt),
        grid=(nb // bt, nt),
        in_specs=[pl.BlockSpec((1, 1, 2 * bt * lt), lambda i, t: (i * nt + t, 0, 0), memory_space=pltpu.SMEM),
                  pl.BlockSpec((bt, lt, D_MODEL), tok),
                  pl.BlockSpec((bt, 6, D_MODEL), lambda i, t: (i, 0, 0)),
                  pl.BlockSpec((bt, lt, 128), tok),
                  pl.BlockSpec(memory_space=pl.ANY),
                  pl.BlockSpec((1, 1, D_MODEL), lambda i, t: (layer, 0, 0)),
                  pl.BlockSpec((1, 1, D_MODEL), lambda i, t: (layer, 0, 0))],
        out_specs=pl.BlockSpec((bt, lt, D_MODEL), tok),
        out_shape=jax.ShapeDtypeStruct((nb, lp, D_MODEL), F32),
        scratch_shapes=[pltpu.VMEM((2, bt * lt, D_MODEL), F32), pltpu.SemaphoreType.DMA],
        compiler_params=pltpu.CompilerParams(dimension_semantics=("arbitrary", "arbitrary"),
                                             vmem_limit_bytes=VMEM_LIMIT),
        name=f"combine_l{layer}_b{bt}",
    )(pos, x1, mod, rinfo, eo, lg, lb)


def _hist(state):
    return jnp.pad(state, ((0, 0), (HIST - state.shape[1], 0), (0, 0)))


def _trunk(x, mod, st, p, bt, lt, lv, mix, moe_block, ip_tile):
    nb = x.shape[0]
    new = {key: [] for key in ('n', 'm', 'conv', 'gconv')}
    C = S = None
    for l in range(DEPTH):
        proj, x = _inproj(x, mod[l], p['ln_in_g'], p['ln_in_b'], p['w_in_r'], p['b_in_r'], l, l == 0, *ip_tile)
        m0 = jnp.broadcast_to(st['m'][l][:, :, None, None], (nb, N_HEADS, 1, 128))
        yb, conv, *pre = _mlstm_prep(proj, _hist(st['conv'][l]), p['conv_b_w8'][l], mix['NB'], mix['G'], mix['tl'], lv)
        ya, C, n, m = _mlstm_scan(pre, proj, st['C'], st['n'][l], m0, p['mlstm_norm_g'][l:l + 1],
                                  mix['SB'], mix['G'], mix['tl'], l, C)
        *pre, gconv = _gdn_prep(proj, _hist(st['gconv'][l]), p['conv_c_w8'][l], p['alog_row'][l], p['dtb_row'][l],
                                mix['NB'], mix['G'], mix['tl'], lv)
        yc, S = _gdn_scan(pre, proj, st['S'], p['gdn_norm_g'][l:l + 1], mix['SB'], mix['G'], mix['tl'], l, S)
        x1, rinfo, cnt = _merge(ya, yb, yc, proj, x, mod[l], p['w_br_a'], p['w_br_b'], p['w_br_c'], p['w_out'],
                                p['ln1_g'], p['ln1_b'], p['wr_hi'], p['wr_lo'], p['br'], l, bt, lt)
        pos, block_e, n_used, ztail, n_blocks = _moe_plan(rinfo, cnt, bt * lt, moe_block)
        disp = _dispatch(x1, mod[l], pos, ztail, n_blocks, moe_block, bt, lt)
        eo = _experts(disp, block_e, n_used, p['exp_w_gate'], p['exp_w_up'], p['exp_w_down'], l, moe_block)
        x = _combine(x1, mod[l], rinfo, eo, pos, p['ln2_g'], p['ln2_b'], l, bt, lt)
        new['n'].append(n)
        new['m'].append(m[:, :, 0, 0])
        new['conv'].append(conv)
        new['gconv'].append(gconv)
    return x, dict({key: jnp.stack(val) for key, val in new.items()}, C=C, S=S)


def kernel(x_prompt, x_sample, state_mlstm_C, state_mlstm_n, state_mlstm_m, state_conv, state_gdn_S, state_gdn_conv, c_prompt, c_sample, ln_in_g, ln_in_b, w_ada, b_ada, w_in, b_in, mlstm_norm_g, conv_b_w, conv_c_w, gdn_a_log, gdn_dt_bias, gdn_norm_g, w_br_a, w_br_b, w_br_c, w_out, ln1_g, ln1_b, router_g_w, router_g_b, router_e_w, router_e_b, exp_w_gate, exp_w_up, exp_w_down, ln2_g, ln2_b):
    nbp, lp, _ = x_prompt.shape
    nbs, ls, _ = x_sample.shape
    lsp = 8

    def regroup(a):
        parts = [a[..., 3592:5128], a[..., 2056:3592], a[..., 5648:8720], a[..., 0:2048], a[..., 5128:5640],
                 a[..., 2048:2052], a[..., 5640:5644], jnp.zeros(a.shape[:-1] + (120,), a.dtype),
                 a[..., 2052:2056], a[..., 5644:5648], jnp.zeros(a.shape[:-1] + (120,), a.dtype)]
        return jnp.concatenate(parts, axis=-1)

    wr = jnp.concatenate([router_e_w, router_g_w, jnp.zeros((DEPTH, D_MODEL, 128 - N_EXPERTS - N_GROUPS), F32)], axis=-1)
    wr_hi = wr.astype(BF16)
    lane_pad = lambda a: jnp.pad(a, ((0, 0), (4, 128 - 4 - N_HEADS)))[:, None, :]
    p = dict(
        ln_in_g=ln_in_g.reshape(1, D_MODEL), ln_in_b=ln_in_b.reshape(1, D_MODEL),
        w_in_r=regroup(w_in).astype(BF16), b_in_r=regroup(b_in).reshape(DEPTH, 1, N_PROJ),
        mlstm_norm_g=mlstm_norm_g,
        conv_b_w8=jnp.pad(conv_b_w, ((0, 0), (0, 8 - CONV_B), (0, 0))),
        conv_c_w8=jnp.pad(conv_c_w, ((0, 0), (0, 8 - CONV_C), (0, 0))),
        alog_row=lane_pad(gdn_a_log), dtb_row=lane_pad(gdn_dt_bias), gdn_norm_g=gdn_norm_g,
        w_br_a=w_br_a.astype(BF16), w_br_b=w_br_b.astype(BF16), w_br_c=w_br_c.astype(BF16),
        w_out=w_out.astype(BF16),
        ln1_g=ln1_g.reshape(DEPTH, 1, D_MODEL), ln1_b=ln1_b.reshape(DEPTH, 1, D_MODEL),
        wr_hi=wr_hi, wr_lo=(wr - wr_hi.astype(F32)).astype(BF16),
        br=jnp.concatenate([router_e_b, router_g_b, jnp.zeros((DEPTH, 128 - N_EXPERTS - N_GROUPS), F32)],
                           axis=-1).reshape(DEPTH, 1, 128),
        exp_w_gate=exp_w_gate, exp_w_up=exp_w_up, exp_w_down=exp_w_down,
        ln2_g=ln2_g.reshape(DEPTH, 1, D_MODEL), ln2_b=ln2_b.reshape(DEPTH, 1, D_MODEL),
    )

    mod = _ada(jnp.concatenate([c_prompt, c_sample], axis=0), w_ada, b_ada)
    mod = mod.reshape(DEPTH, nbp + nbs, 6, D_MODEL)

    zeros = lambda *s: jnp.zeros((DEPTH, nbp) + s, F32)
    st_p = {'C': zeros(N_HEADS, DH, DH), 'n': zeros(N_HEADS, DH), 'm': zeros(N_HEADS),
            'conv': zeros(CONV_B - 1, D_MIX), 'S': zeros(N_HEADS, DH, DH), 'gconv': zeros(CONV_C - 1, 3 * D_MIX)}
    y_p, sp = _trunk(x_prompt, mod[:, :nbp], st_p, p, bt=1, lt=512, lv=lp,
                     mix=dict(NB=4, G=1, tl=RB, SB=nbp), moe_block=256, ip_tile=(1, 256))

    st_s = {'C': state_mlstm_C, 'n': state_mlstm_n, 'm': state_mlstm_m, 'conv': state_conv,
            'S': state_gdn_S, 'gconv': state_gdn_conv}
    xs = jnp.pad(x_sample, ((0, 0), (0, lsp - ls), (0, 0)))
    y_s, ss = _trunk(xs, mod[:, nbp:], st_s, p, bt=64, lt=lsp, lv=ls,
                     mix=dict(NB=2, G=RB // lsp, tl=lsp, SB=1), moe_block=128, ip_tile=(32, lsp))
    y_s = y_s[:, :ls]

    return (y_p, y_s, sp['C'], sp['n'], sp['m'], sp['conv'], sp['S'], sp['gconv'],
            ss['C'], ss['n'], ss['m'], ss['conv'], ss['S'], ss['gconv'])
```

```python
import functools

import jax
import jax.numpy as jnp
from jax import lax
from jax.experimental import pallas as pl
from jax.experimental.pallas import tpu as pltpu

F32 = jnp.float32
BF16 = jnp.bfloat16

D_MODEL = 1024
DEPTH = 2
N_HEADS = 4
DH = 128
D_MIX = N_HEADS * DH
N_EXPERTS = 32
EXPERTS_PER_GROUP = 8
N_GROUPS = 4
D_EXPERT = 256
CONV_B = 3
CONV_C = 4
HIST = 8
RB = 64
INV_BLOCK = 8
DN_ALPHA = (2 * DEPTH) ** 0.25
LN_EPS = 1e-5
NORM_EPS = 1e-6
NEG = -1e30

OFF_QKVC = 0
OFF_BCH = 1536
OFF_GTS = 3072
OFF_QKVO = 6144
OFF_Z = 8192
OFF_SA = 8704
OFF_SB = 8832
N_PROJ = 8960
TN_PROJ = 1280
PROJ_SEGMENTS = ((3592, 5128, OFF_QKVC), (2056, 3592, OFF_BCH), (5648, 8720, OFF_GTS), (0, 2048, OFF_QKVO),
                 (5128, 5640, OFF_Z), (2048, 2052, OFF_SA), (5640, 5644, OFF_SA + 4), (2052, 2056, OFF_SB),
                 (5644, 5648, OFF_SB + 4))

VMEM_LIMIT = 52 * 1024 * 1024


def _dot(a, b):
    return jnp.dot(a, b, preferred_element_type=F32)


def _split3(x):
    hi = x.astype(BF16)
    r = x - hi.astype(F32)
    mid = r.astype(BF16)
    lo = (r - mid.astype(F32)).astype(BF16)
    return hi, mid, lo


def _layer_norm(x, g, b):
    mu = jnp.mean(x, axis=-1, keepdims=True)
    xc = x - mu
    var = jnp.mean(xc * xc, axis=-1, keepdims=True)
    return xc * lax.rsqrt(var + LN_EPS) * g + b


def _sigmoid(x):
    return jax.nn.sigmoid(x)


def _silu(x):
    return x * jax.nn.sigmoid(x)


def _log_sigmoid(x):
    return jnp.minimum(x, 0.0) - jnp.log1p(jnp.exp(-jnp.abs(x)))


def _softplus(x):
    return jnp.maximum(x, 0.0) + jnp.log1p(jnp.exp(-jnp.abs(x)))


def _ada_body(c_ref, w_ref, b_ref, o_ref):
    c = c_ref[...]
    s = _silu(c).astype(BF16)
    o_ref[0] = _dot(s, w_ref[0].astype(BF16)) + b_ref[0]


def _ada(c_all, w_ada, b_ada):
    nb = c_all.shape[0]
    return pl.pallas_call(
        _ada_body,
        grid=(DEPTH, 6),
        in_specs=[pl.BlockSpec((nb, D_MODEL), lambda l, j: (0, 0)),
                  pl.BlockSpec((1, D_MODEL, D_MODEL), lambda l, j: (l, 0, j)),
                  pl.BlockSpec((1, 1, D_MODEL), lambda l, j: (l, 0, j))],
        out_specs=pl.BlockSpec((1, nb, D_MODEL), lambda l, j: (l, 0, j)),
        out_shape=jax.ShapeDtypeStruct((DEPTH, nb, 6 * D_MODEL), F32),
        compiler_params=pltpu.CompilerParams(dimension_semantics=("arbitrary", "arbitrary"),
                                             vmem_limit_bytes=VMEM_LIMIT),
        name="ada",
    )(c_all, w_ada, b_ada.reshape(DEPTH, 1, 6 * D_MODEL))


def _regroup_body(w_ref, o_ref):
    o_ref[...] = jnp.zeros_like(o_ref)
    for src, end, dst in PROJ_SEGMENTS:
        o_ref[0, :, dst:dst + end - src] = w_ref[0, :, src:end].astype(BF16)


def _regroup_w_in(w_in):
    n_in = w_in.shape[-1]
    rows = 256
    return pl.pallas_call(
        _regroup_body,
        grid=(DEPTH, D_MODEL // rows),
        in_specs=[pl.BlockSpec((1, rows, n_in), lambda l, i: (l, i, 0))],
        out_specs=pl.BlockSpec((1, rows, N_PROJ), lambda l, i: (l, i, 0)),
        out_shape=jax.ShapeDtypeStruct((DEPTH, D_MODEL, N_PROJ), BF16),
        compiler_params=pltpu.CompilerParams(dimension_semantics=("arbitrary", "arbitrary"),
                                             vmem_limit_bytes=VMEM_LIMIT),
        name="regroup_w_in",
    )(w_in)


def _inproj_body(x_ref, mod_ref, g_ref, b_ref, w_ref, bias_ref, proj_ref, *rest, apply_ln, bt, lt):
    x = x_ref[...]
    if apply_ln:
        x = _layer_norm(x, g_ref[...], b_ref[...])
        rest[0][...] = x
    u = (x * (1.0 + mod_ref[:, 1:2, :]) + mod_ref[:, 0:1, :]).reshape(bt * lt, D_MODEL).astype(BF16)
    for j in range(N_PROJ // TN_PROJ):
        cs = slice(j * TN_PROJ, (j + 1) * TN_PROJ)
        proj_ref[:, :, cs] = (_dot(u, w_ref[0, :, cs]) + bias_ref[0, :, cs]).reshape(bt, lt, TN_PROJ)


def _inproj(x, mod, ln_g, ln_b, w_r, b_r, layer, apply_ln, bt, lt):
    nb, lp, _ = x.shape
    tok = lambda i, t: (i, t, 0)
    out_shape = [jax.ShapeDtypeStruct((nb, lp, N_PROJ), F32)]
    out_specs = [pl.BlockSpec((bt, lt, N_PROJ), tok)]
    if apply_ln:
        out_shape.append(jax.ShapeDtypeStruct((nb, lp, D_MODEL), F32))
        out_specs.append(pl.BlockSpec((bt, lt, D_MODEL), tok))
    res = pl.pallas_call(
        functools.partial(_inproj_body, apply_ln=apply_ln, bt=bt, lt=lt),
        grid=(nb // bt, lp // lt),
        in_specs=[pl.BlockSpec((bt, lt, D_MODEL), tok),
                  pl.BlockSpec((bt, 6, D_MODEL), lambda i, t: (i, 0, 0)),
                  pl.BlockSpec((1, D_MODEL), lambda i, t: (0, 0)),
                  pl.BlockSpec((1, D_MODEL), lambda i, t: (0, 0)),
                  pl.BlockSpec((1, D_MODEL, N_PROJ), lambda i, t: (layer, 0, 0), pipeline_mode=pl.Buffered(1)),
                  pl.BlockSpec((1, 1, N_PROJ), lambda i, t: (layer, 0, 0))],
        out_specs=out_specs,
        out_shape=out_shape,
        compiler_params=pltpu.CompilerParams(dimension_semantics=("arbitrary", "arbitrary"),
                                             vmem_limit_bytes=VMEM_LIMIT),
        name=f"inproj_l{layer}_b{bt}",
    )(x, mod, ln_g, ln_b, w_r, b_r)
    return (res[0], res[1]) if apply_ln else (res[0], x)


def _conv_taps(xp_s, w_ref, width, tl):
    acc = None
    for j in range(width):
        tap = xp_s[:, pl.ds(HIST - (width - 1) + j, tl), :] * w_ref[j:j + 1, :].reshape(1, 1, -1)
        acc = tap if acc is None else acc + tap
    return acc


def _conv_history(xp_s, hist_ref, prev_ref, n_steps):
    if n_steps > 1:
        @pl.when(pl.program_id(1) == 0)
        def _():
            xp_s[:, 0:HIST, :] = hist_ref[...]

        @pl.when(pl.program_id(1) > 0)
        def _():
            xp_s[:, 0:HIST, :] = prev_ref[...]
    else:
        xp_s[:, 0:HIST, :] = hist_ref[...]


def _heads(x, nb, width):
    return jnp.stack([x[:, :, h * width:(h + 1) * width] for h in range(N_HEADS)],
                     axis=1).reshape(nb * N_HEADS, RB, width)


def _gate_cols(x, nb, lane0):
    return jnp.stack([x[:, :, lane0 + h:lane0 + h + 1] for h in range(N_HEADS)],
                     axis=1).reshape(nb * N_HEADS, RB, 1)


def _gate_rows(x, nb, lane0):
    xt = jnp.swapaxes(x, 1, 2)
    return jnp.stack([xt[:, lane0 + h:lane0 + h + 1, :] for h in range(N_HEADS)],
                     axis=1).reshape(nb * N_HEADS, 1, RB)


def _bmm(a, b):
    return jnp.einsum('nts,nsu->ntu', a.astype(BF16), b.astype(BF16), preferred_element_type=F32)


def _bmm_nt(a, b):
    return jnp.einsum('ntd,nsd->nts', a, b, preferred_element_type=F32)


def _block_masks(tl):
    rr = lax.broadcasted_iota(jnp.int32, (RB, RB), 0)
    cc = lax.broadcasted_iota(jnp.int32, (RB, RB), 1)
    incl = rr >= cc
    if tl < RB:
        incl = incl & ((rr // tl) == (cc // tl))
    return rr, cc, incl


def _seq_cumsum(x, incl, nb):
    tril = jnp.broadcast_to(incl.astype(BF16)[None], (nb, RB, RB))
    hi, mid, lo = _split3(x)
    return _bmm(tril, hi) + _bmm(tril, mid) + _bmm(tril, lo)


def _seq_last(x, nb, G, tl):
    return x.reshape(nb * G, tl, 128)[:, tl - 1:tl, :]


def _seq_rows(x3, nb, tl):
    return jnp.broadcast_to(x3, (x3.shape[0], tl, 128)).reshape(nb, RB, 128)


def _put_chains(ref, val, nb, G, tl):
    val4 = val.reshape(nb, N_HEADS, RB, val.shape[-1])
    for h in range(N_HEADS):
        ref[:, :, h] = val4[:, h].reshape(nb, G, tl, val.shape[-1])


def _put_seq_scalars(ref, x3, nb, G, lane0):
    x4 = x3.reshape(nb, G, 1, 128)
    for h in range(N_HEADS):
        ref[:, :, h] = jnp.broadcast_to(x4[:, :, :, lane0 + h:lane0 + h + 1], (nb, G, 1, 128))


def _step_tiling(nb, lp, NB, G, tl):
    assert G * tl == RB
    if G == 1:
        gx, tlx = 1, NB * RB
    else:
        assert lp == tl
        gx, tlx = NB * G, tl
    return gx, tlx, nb // gx, lp // tlx


def _layered_state(layer, nb, NS, prev):
    st_in = pl.BlockSpec((None, NS, N_HEADS, DH, DH), lambda i, c: (layer, i, 0, 0, 0))
    shape = jax.ShapeDtypeStruct((DEPTH, nb, N_HEADS, DH, DH), F32)
    if layer == 0:
        return st_in, pl.BlockSpec((DEPTH, NS, N_HEADS, DH, DH), lambda i, c: (0, i, 0, 0, 0)), shape, [], []
    return st_in, st_in, shape, [pl.BlockSpec(memory_space=pl.ANY)], [prev]


def _fill_later_layers(ref, when):
    @pl.when(when)
    def _():
        for l in range(1, DEPTH):
            ref[l] = ref[0]


def _mlstm_prep_body(qkvo_ref, bch_ref, prev_ref, sa_ref, sb_ref, cv0_ref, cw_ref,
                     yb_ref, cv_ref, nv_ref, q_ref, kw_ref, v_ref, cols_ref, bl_ref, bc_ref, kn_ref, xp_s,
                     *, NB, G, tl, tlx, NCS, lv):
    cs = pl.program_id(1)
    n = NB * N_HEADS
    lvl = lv - (NCS - 1) * tlx

    if NCS > 1:
        @pl.when(cs == 0)
        def _():
            xp_s[:, 0:HIST, :] = cv0_ref[...]

        @pl.when(cs > 0)
        def _():
            xp_s[:, 0:HIST, :] = prev_ref[:, :, D_MIX:2 * D_MIX] * prev_ref[:, :, 2 * D_MIX:3 * D_MIX]
    else:
        xp_s[:, 0:HIST, :] = cv0_ref[...]
    xp_s[:, HIST:HIST + tlx, :] = bch_ref[:, :, D_MIX:2 * D_MIX] * bch_ref[:, :, 2 * D_MIX:3 * D_MIX]
    yb_ref[...] = (bch_ref[:, :, 0:D_MIX] * _conv_taps(xp_s, cw_ref, CONV_B, tlx)).astype(BF16)

    @pl.when(cs == NCS - 1)
    def _():
        cv_ref[...] = xp_s[:, pl.ds(HIST + lvl - (CONV_B - 1), CONV_B - 1), :]

    i_all = sa_ref[...].reshape(NB, RB, 128)
    f_all = _log_sigmoid(sb_ref[...].reshape(NB, RB, 128))
    if lv < NCS * tlx:
        assert NCS == 1
        valid = (lax.broadcasted_iota(jnp.int32, (NB, RB, 128), 1) % tl) < lv
        i_all = jnp.where(valid, i_all, NEG)
        f_all = jnp.where(valid, f_all, 0.0)
    _, _, incl = _block_masks(tl)
    bcum = _seq_cumsum(f_all, incl, NB)
    blast = _seq_last(bcum, NB, G, tl)
    val = _seq_rows(blast, NB, tl) - bcum + i_all
    bmax = jnp.max(val.reshape(NB * G, tl, 128), axis=1, keepdims=True)
    wk0 = jnp.exp(val - _seq_rows(bmax, NB, tl))

    qkvo = qkvo_ref[...].reshape(NB, RB, 4 * D_MIX)
    q = _heads(qkvo[:, :, 0:D_MIX], NB, DH)
    k = _heads(qkvo[:, :, D_MIX:2 * D_MIX], NB, DH) * (DH ** -0.5)
    v = _heads(qkvo[:, :, 2 * D_MIX:3 * D_MIX], NB, DH)
    qb, kb, vb = q.astype(BF16), k.astype(BF16), v.astype(BF16)

    b_col = _gate_cols(bcum, NB, 0)
    dlog = jnp.where(incl[None], b_col - _gate_rows(bcum, NB, 0) + _gate_rows(i_all, NB, 0), NEG)
    d = jnp.max(dlog, axis=-1, keepdims=True)
    s0 = _bmm_nt(qb, kb) * jnp.exp(dlog - d)
    kw0 = k * _gate_cols(wk0, NB, 0)

    _put_chains(nv_ref, _bmm(s0, vb), NB, G, tl)
    _put_chains(q_ref, qb, NB, G, tl)
    _put_chains(kw_ref, kw0.astype(BF16), NB, G, tl)
    _put_chains(v_ref, vb, NB, G, tl)
    _put_chains(cols_ref, jnp.concatenate([d, b_col, jnp.sum(s0, axis=-1, keepdims=True),
                                           jnp.zeros((n, RB, 5), F32)], axis=-1), NB, G, tl)
    _put_seq_scalars(bl_ref, blast, NB, G, 0)
    _put_seq_scalars(bc_ref, bmax, NB, G, 0)
    kn = jnp.sum(kw0.reshape(n * G, tl, DH), axis=1, keepdims=True).reshape(NB, N_HEADS, G, 1, DH)
    for h in range(N_HEADS):
        kn_ref[:, :, h] = kn[:, h]


def _mlstm_prep(proj, cv0, cw, NB, G, tl, lv):
    nb, lp, _ = proj.shape
    gx, tlx, NI, NCS = _step_tiling(nb, lp, NB, G, tl)
    nbt = nb * lp // RB
    step = lambda i, c: (i * NCS + c, 0, 0, 0, 0)
    chain = lambda last, dt: jax.ShapeDtypeStruct((nbt, G, N_HEADS, tl, last), dt)
    cspec = lambda last: pl.BlockSpec((NB, G, N_HEADS, tl, last), step)
    scal = jax.ShapeDtypeStruct((nbt, G, N_HEADS, 1, 128), F32)
    sspec = pl.BlockSpec((NB, G, N_HEADS, 1, 128), step)
    bch = OFF_BCH // (3 * D_MIX)
    return pl.pallas_call(
        functools.partial(_mlstm_prep_body, NB=NB, G=G, tl=tl, tlx=tlx, NCS=NCS, lv=lv),
        grid=(NI, NCS),
        in_specs=[pl.BlockSpec((gx, tlx, 4 * D_MIX), lambda i, c: (i, c, OFF_QKVO // (4 * D_MIX))),
                  pl.BlockSpec((gx, tlx, 3 * D_MIX), lambda i, c: (i, c, bch)),
                  pl.BlockSpec((gx, HIST, 3 * D_MIX), lambda i, c: (i, jnp.maximum(c * (tlx // HIST) - 1, 0), bch)),
                  pl.BlockSpec((gx, tlx, 128), lambda i, c: (i, c, OFF_SA // 128)),
                  pl.BlockSpec((gx, tlx, 128), lambda i, c: (i, c, OFF_SB // 128)),
                  pl.BlockSpec((gx, HIST, D_MIX), lambda i, c: (i, 0, 0)),
                  pl.BlockSpec((8, D_MIX), lambda i, c: (0, 0))],
        out_specs=[pl.BlockSpec((gx, tlx, D_MIX), lambda i, c: (i, c, 0)),
                   pl.BlockSpec((gx, CONV_B - 1, D_MIX), lambda i, c: (i, 0, 0)),
                   cspec(DH), cspec(DH), cspec(DH), cspec(DH), cspec(8), sspec, sspec, sspec],
        out_shape=[jax.ShapeDtypeStruct((nb, lp, D_MIX), BF16),
                   jax.ShapeDtypeStruct((nb, CONV_B - 1, D_MIX), F32),
                   chain(DH, F32), chain(DH, BF16), chain(DH, BF16), chain(DH, BF16), chain(8, F32),
                   scal, scal, scal],
        scratch_shapes=[pltpu.VMEM((gx, HIST + tlx, D_MIX), F32)],
        compiler_params=pltpu.CompilerParams(dimension_semantics=("arbitrary", "arbitrary"),
                                             vmem_limit_bytes=VMEM_LIMIT),
        name=f"mlstm_prep_g{G}",
    )(proj, proj, proj, proj, proj, cv0, cw)


def _mlstm_scan_body(nv_ref, q_ref, kw_ref, v_ref, cols_ref, bl_ref, bc_ref, kn_ref, o_ref, C0_ref, n0_ref, m0_ref,
                     ng_ref, *rest, NS, tl, NC, first):
    ya_ref, C_out, n_ref, m_ref = rest[-4:]
    C_ref = C_out.at[0] if first else C_out
    c = pl.program_id(1)
    n = NS * N_HEADS

    @pl.when(c == 0)
    def _():
        C_ref[...] = C0_ref[...]
        n_ref[...] = n0_ref[...]
        m_ref[...] = m0_ref[...]

    chains = lambda ref: ref[...].reshape(n, ref.shape[-2], ref.shape[-1])
    C = C_ref[...].reshape(n, DH, DH)
    nvec = n_ref[...].reshape(n, 1, DH)
    m_prev = chains(m_ref)[:, :, 0:1]
    cols = chains(cols_ref)
    d, b, ds0 = cols[:, :, 0:1], cols[:, :, 1:2], cols[:, :, 2:3]
    qb = chains(q_ref)

    m_inter = b + m_prev
    m_t = jnp.maximum(m_inter, d)
    f = jnp.exp(d - m_t)
    inter = jnp.exp(m_inter - m_t)
    qC = jnp.einsum('ntd,nde->nte', qb, C.astype(BF16), preferred_element_type=F32)
    qn = jnp.sum(qb.astype(F32) * nvec, axis=-1, keepdims=True)
    num = f * chains(nv_ref) + inter * qC
    den = f * ds0 + inter * qn
    hh = num / jnp.maximum(jnp.abs(den), jnp.exp(-m_t))

    m_new = m_t[:, tl - 1:tl, :]
    decay = jnp.exp(chains(bl_ref)[:, :, 0:1] + m_prev - m_new)
    scale = jnp.exp(chains(bc_ref)[:, :, 0:1] - m_new)
    kv = jnp.einsum('ntd,nte->nde', chains(kw_ref), chains(v_ref), preferred_element_type=F32)
    C_ref[...] = (decay * C + scale * kv).reshape(NS, N_HEADS, DH, DH)
    n_ref[...] = (decay * nvec + scale * chains(kn_ref)).reshape(NS, N_HEADS, DH)
    m_ref[...] = jnp.broadcast_to(m_new, (n, 1, 128)).reshape(NS, N_HEADS, 1, 128)
    if first:
        _fill_later_layers(C_out, c == NC - 1)

    mu = jnp.mean(hh, axis=-1, keepdims=True)
    hc = hh - mu
    hn = (hc * lax.rsqrt(jnp.mean(hc * hc, axis=-1, keepdims=True) + LN_EPS)).reshape(NS, N_HEADS, tl, DH)
    for h in range(N_HEADS):
        hs = slice(h * DH, (h + 1) * DH)
        ya_ref[:, :, hs] = (_sigmoid(o_ref[:, :, hs]) * hn[:, h] * ng_ref[:, hs]).astype(BF16)


def _mlstm_scan(pre, proj, C0, n0, m0, ng, SB, G, tl, layer, C_prev):
    nb, lp, _ = proj.shape
    NS = SB * G
    NI, NC = nb // NS, lp // tl
    six = lambda a: a.reshape((NI * SB, NC) + a.shape[1:])
    spec6 = lambda a: pl.BlockSpec((SB, 1) + a.shape[1:], lambda i, c: (i, c, 0, 0, 0, 0))
    seq4 = lambda i, c: (i, 0, 0, 0)
    st_in, st_out, st_shape, extra_in, extra_args = _layered_state(layer, nb, NS, C_prev)
    n_in = len(pre) + 5
    return pl.pallas_call(
        functools.partial(_mlstm_scan_body, NS=NS, tl=tl, NC=NC, first=layer == 0),
        grid=(NI, NC),
        in_specs=[spec6(a) for a in pre] + [
            pl.BlockSpec((NS, tl, D_MIX), lambda i, c: (i, c, (OFF_QKVO + 3 * D_MIX) // D_MIX)),
            st_in,
            pl.BlockSpec((NS, N_HEADS, DH), lambda i, c: (i, 0, 0)),
            pl.BlockSpec((NS, N_HEADS, 1, 128), seq4),
            pl.BlockSpec((1, D_MIX), lambda i, c: (0, 0))] + extra_in,
        out_specs=[pl.BlockSpec((NS, tl, D_MIX), lambda i, c: (i, c, 0)),
                   st_out,
                   pl.BlockSpec((NS, N_HEADS, DH), lambda i, c: (i, 0, 0)),
                   pl.BlockSpec((NS, N_HEADS, 1, 128), seq4)],
        out_shape=[jax.ShapeDtypeStruct((nb, lp, D_MIX), BF16),
                   st_shape,
                   jax.ShapeDtypeStruct((nb, N_HEADS, DH), F32),
                   jax.ShapeDtypeStruct((nb, N_HEADS, 1, 128), F32)],
        input_output_aliases={n_in: 1} if extra_in else {},
        compiler_params=pltpu.CompilerParams(dimension_semantics=("arbitrary", "arbitrary"),
                                             vmem_limit_bytes=VMEM_LIMIT),
        name=f"mlstm_scan_g{G}",
    )(*[six(a) for a in pre], proj, C0, n0, m0, ng, *extra_args)


def _unit_lower_inverse(n, rr, cc, tl):
    eye = (rr == cc).astype(F32)[None]
    p = jnp.where(((rr // INV_BLOCK) == (cc // INV_BLOCK))[None], n, 0.0)
    x = eye + p
    b = 2
    while b < INV_BLOCK:
        p = _bmm(p, p)
        x = x + _bmm(x, p)
        b *= 2
    b = INV_BLOCK
    while b < tl:
        off = jnp.where((((rr // (2 * b)) == (cc // (2 * b))) & ((rr // b) != (cc // b)))[None], n, 0.0)
        x = x + _bmm(x, _bmm(off, x))
        b *= 2
    return x


def _gdn_prep_body(x_ref, prev_ref, sa_ref, sb_ref, gc0_ref, cw_ref, alog_ref, dtb_ref,
                   u_ref, w_ref, qg_ref, kd_ref, qkm_ref, eg_ref, gcs_ref, xp_s, *, NB, G, tl, tlx, NCS, lv):
    cs = pl.program_id(1)
    lvl = lv - (NCS - 1) * tlx

    _conv_history(xp_s, gc0_ref, prev_ref, NCS)
    xp_s[:, HIST:HIST + tlx, :] = x_ref[...]
    qkv = _silu(_conv_taps(xp_s, cw_ref, CONV_C, tlx)).reshape(NB, RB, 3 * D_MIX)

    @pl.when(cs == NCS - 1)
    def _():
        gcs_ref[...] = xp_s[:, pl.ds(HIST + lvl - (CONV_C - 1), CONV_C - 1), :]

    beta_all = _sigmoid(sa_ref[...].reshape(NB, RB, 128))
    g_all = -jnp.exp(alog_ref[...]) * _softplus(sb_ref[...].reshape(NB, RB, 128) + dtb_ref[...])
    if lv < NCS * tlx:
        assert NCS == 1
        valid = (lax.broadcasted_iota(jnp.int32, (NB, RB, 128), 1) % tl) < lv
        beta_all = jnp.where(valid, beta_all, 0.0)
        g_all = jnp.where(valid, g_all, 0.0)
    rr, cc, incl = _block_masks(tl)
    diag = rr == cc
    gam = _seq_cumsum(g_all, incl, NB)
    glast = _seq_last(gam, NB, G, tl)
    gcol = _gate_cols(gam, NB, 4)
    bcol = _gate_cols(beta_all, NB, 4)
    egcol = _gate_cols(jnp.exp(gam), NB, 4)
    kdcol = _gate_cols(jnp.exp(_seq_rows(glast, NB, tl) - gam), NB, 4)

    q = _heads(qkv[:, :, 0:D_MIX], NB, DH)
    k = _heads(qkv[:, :, D_MIX:2 * D_MIX], NB, DH)
    v = _heads(qkv[:, :, 2 * D_MIX:3 * D_MIX], NB, DH)
    q = q * lax.rsqrt(jnp.sum(q * q, axis=-1, keepdims=True) + NORM_EPS) * (DH ** -0.5)
    k = k * lax.rsqrt(jnp.sum(k * k, axis=-1, keepdims=True) + NORM_EPS)
    qb, kb = q.astype(BF16), k.astype(BF16)

    dmat = jnp.exp(jnp.where(incl[None], gcol - _gate_rows(gam, NB, 4), NEG))
    nmat = jnp.where(diag[None], 0.0, -(bcol * _bmm_nt(kb, kb) * dmat))
    rhs = jnp.concatenate([bcol * v, (bcol * egcol) * k], axis=-1)
    sol = _bmm(_unit_lower_inverse(nmat, rr, cc, tl), rhs)
    qkm = (_bmm_nt(qb, kb) * dmat).astype(BF16).reshape(NB, N_HEADS, RB, RB)

    _put_chains(u_ref, sol[:, :, 0:DH], NB, G, tl)
    _put_chains(w_ref, sol[:, :, DH:2 * DH].astype(BF16), NB, G, tl)
    _put_chains(qg_ref, (q * egcol).astype(BF16), NB, G, tl)
    _put_chains(kd_ref, (k * kdcol).astype(BF16), NB, G, tl)
    _put_seq_scalars(eg_ref, jnp.exp(glast), NB, G, 4)
    for h in range(N_HEADS):
        for g in range(G):
            qkm_ref[:, g, h] = qkm[:, h, g * tl:(g + 1) * tl, g * tl:(g + 1) * tl]


def _gdn_prep(proj, gc0, cw, alog, dtb, NB, G, tl, lv):
    nb, lp, _ = proj.shape
    assert tl % INV_BLOCK == 0 and (tl // INV_BLOCK) & (tl // INV_BLOCK - 1) == 0
    gx, tlx, NI, NCS = _step_tiling(nb, lp, NB, G, tl)
    nbt = nb * lp // RB
    step = lambda i, c: (i * NCS + c, 0, 0, 0, 0)
    par = lambda i, c: (0, 0)
    chain = lambda last, dt: jax.ShapeDtypeStruct((nbt, G, N_HEADS, tl, last), dt)
    cspec = lambda last: pl.BlockSpec((NB, G, N_HEADS, tl, last), step)
    return pl.pallas_call(
        functools.partial(_gdn_prep_body, NB=NB, G=G, tl=tl, tlx=tlx, NCS=NCS, lv=lv),
        grid=(NI, NCS),
        in_specs=[pl.BlockSpec((gx, tlx, 3 * D_MIX), lambda i, c: (i, c, OFF_QKVC // (3 * D_MIX))),
                  pl.BlockSpec((gx, HIST, 3 * D_MIX),
                               lambda i, c: (i, jnp.maximum(c * (tlx // HIST) - 1, 0), OFF_QKVC // (3 * D_MIX))),
                  pl.BlockSpec((gx, tlx, 128), lambda i, c: (i, c, OFF_SA // 128)),
                  pl.BlockSpec((gx, tlx, 128), lambda i, c: (i, c, OFF_SB // 128)),
                  pl.BlockSpec((gx, HIST, 3 * D_MIX), lambda i, c: (i, 0, 0)),
                  pl.BlockSpec((8, 3 * D_MIX), par),
                  pl.BlockSpec((1, 128), par),
                  pl.BlockSpec((1, 128), par)],
        out_specs=[cspec(DH), cspec(DH), cspec(DH), cspec(DH), cspec(tl),
                   pl.BlockSpec((NB, G, N_HEADS, 1, 128), step),
                   pl.BlockSpec((gx, CONV_C - 1, 3 * D_MIX), lambda i, c: (i, 0, 0))],
        out_shape=[chain(DH, F32), chain(DH, BF16), chain(DH, BF16), chain(DH, BF16), chain(tl, BF16),
                   jax.ShapeDtypeStruct((nbt, G, N_HEADS, 1, 128), F32),
                   jax.ShapeDtypeStruct((nb, CONV_C - 1, 3 * D_MIX), F32)],
        scratch_shapes=[pltpu.VMEM((gx, HIST + tlx, 3 * D_MIX), F32)],
        compiler_params=pltpu.CompilerParams(dimension_semantics=("arbitrary", "arbitrary"),
                                             vmem_limit_bytes=VMEM_LIMIT),
        name=f"gdn_prep_g{G}",
    )(proj, proj, proj, proj, gc0, cw, alog, dtb)


def _gdn_scan_body(u_ref, w_ref, qg_ref, kd_ref, qkm_ref, eg_ref, z_ref, S0_ref, gng_ref, *rest,
                   NS, tl, NC, first):
    yc_ref, S_out = rest[-2:]
    S_ref = S_out.at[0] if first else S_out
    c = pl.program_id(1)
    n = NS * N_HEADS

    @pl.when(c == 0)
    def _():
        S_ref[...] = S0_ref[...]

    S = S_ref[...].reshape(n, DH, DH)
    Sb = S.astype(BF16)
    chains = lambda ref: ref[...].reshape(n, tl, ref.shape[-1])
    v_new = chains(u_ref) - jnp.einsum('ntd,nde->nte', chains(w_ref), Sb, preferred_element_type=F32)
    vnb = v_new.astype(BF16)
    o = (jnp.einsum('ntd,nde->nte', chains(qg_ref), Sb, preferred_element_type=F32)
         + jnp.einsum('nts,nse->nte', chains(qkm_ref), vnb, preferred_element_type=F32))
    eg = eg_ref[...].reshape(n, 1, 128)[:, :, 0:1]
    S_new = eg * S + jnp.einsum('ntd,nte->nde', chains(kd_ref), vnb, preferred_element_type=F32)
    S_ref[...] = S_new.reshape(NS, N_HEADS, DH, DH)
    if first:
        _fill_later_layers(S_out, c == NC - 1)

    on = (o * lax.rsqrt(jnp.mean(o * o, axis=-1, keepdims=True) + NORM_EPS) * gng_ref[...]).reshape(NS, N_HEADS, tl, DH)
    for h in range(N_HEADS):
        yc_ref[:, :, h * DH:(h + 1) * DH] = (on[:, h] * _silu(z_ref[:, :, h * DH:(h + 1) * DH])).astype(BF16)


def _gdn_scan(pre, proj, S0, gng, SB, G, tl, layer, S_prev):
    nb, lp, _ = proj.shape
    NS = SB * G
    NI, NC = nb // NS, lp // tl
    six = lambda a: a.reshape((NI * SB, NC) + a.shape[1:])
    cspec = lambda last: pl.BlockSpec((SB, 1, G, N_HEADS, tl, last), lambda i, c: (i, c, 0, 0, 0, 0))
    u, w, qg, kd, qkm, eg = (six(a) for a in pre)
    st_in, st_out, st_shape, extra_in, extra_args = _layered_state(layer, nb, NS, S_prev)
    return pl.pallas_call(
        functools.partial(_gdn_scan_body, NS=NS, tl=tl, NC=NC, first=layer == 0),
        grid=(NI, NC),
        in_specs=[cspec(DH), cspec(DH), cspec(DH), cspec(DH), cspec(tl),
                  pl.BlockSpec((SB, 1, G, N_HEADS, 1, 128), lambda i, c: (i, c, 0, 0, 0, 0)),
                  pl.BlockSpec((NS, tl, D_MIX), lambda i, c: (i, c, OFF_Z // D_MIX)),
                  st_in,
                  pl.BlockSpec((1, DH), lambda i, c: (0, 0))] + extra_in,
        out_specs=[pl.BlockSpec((NS, tl, D_MIX), lambda i, c: (i, c, 0)), st_out],
        out_shape=[jax.ShapeDtypeStruct((nb, lp, D_MIX), BF16), st_shape],
        input_output_aliases={9: 1} if extra_in else {},
        compiler_params=pltpu.CompilerParams(dimension_semantics=("arbitrary", "arbitrary"),
                                             vmem_limit_bytes=VMEM_LIMIT),
        name=f"gdn_scan_g{G}",
    )(u, w, qg, kd, qkm, eg, proj, S0, gng, *extra_args)


def _route(rl):
    lane = lax.broadcasted_iota(jnp.int32, rl.shape, 1).astype(F32)
    is_g = (lane >= N_EXPERTS) & (lane < N_EXPERTS + N_GROUPS)
    gl = jnp.where(is_g, rl, NEG)
    gmax = jnp.max(gl, axis=-1, keepdims=True)
    grp = jnp.min(jnp.where(gl == gmax, lane - N_EXPERTS, 4.0 * N_EXPERTS), axis=-1, keepdims=True)
    p_grp = 1.0 / jnp.sum(jnp.where(is_g, jnp.exp(gl - gmax), 0.0), axis=-1, keepdims=True)
    lo = grp * EXPERTS_PER_GROUP
    in_grp = (lane >= lo) & (lane < lo + EXPERTS_PER_GROUP)
    el = jnp.where(in_grp, rl, NEG)
    m1 = jnp.max(el, axis=-1, keepdims=True)
    i1 = jnp.min(jnp.where(el == m1, lane, 4.0 * N_EXPERTS), axis=-1, keepdims=True)
    el2 = jnp.where(lane == i1, NEG, el)
    m2 = jnp.max(el2, axis=-1, keepdims=True)
    i2 = jnp.min(jnp.where(el2 == m2, lane, 4.0 * N_EXPERTS), axis=-1, keepdims=True)
    e2 = jnp.exp(m2 - m1)
    w1 = p_grp / (1.0 + e2)
    w2 = p_grp * e2 / (1.0 + e2)
    return i1, i2, w1, w2


def _merge_body(ya_ref, yb_ref, yc_ref, ga_ref, gb_ref, gc_ref, x_ref, mod_ref, wa_ref, wb_ref, wc_ref, wo_ref,
                lg_ref, lb_ref, wrh_ref, wrl_ref, br_ref, x1_ref, rinfo_ref, cnt_ref, *, bt, lt):
    tm = bt * lt

    @pl.when((pl.program_id(0) == 0) & (pl.program_id(1) == 0))
    def _():
        cnt_ref[...] = jnp.zeros_like(cnt_ref)

    def r2(ref):
        return ref[...].reshape(tm, ref.shape[-1])

    merged = (_sigmoid(r2(ga_ref)) * _dot(r2(ya_ref), wa_ref[0])
              + _sigmoid(r2(gb_ref)) * _dot(r2(yb_ref), wb_ref[0])
              + _sigmoid(r2(gc_ref)) * _dot(r2(yc_ref), wc_ref[0]))
    out = _dot(merged.astype(BF16), wo_ref[0])
    y = DN_ALPHA * x_ref[...] + (1.0 + mod_ref[:, 2:3, :]) * out.reshape(bt, lt, D_MODEL)
    x1 = _layer_norm(y, lg_ref[0], lb_ref[0])
    x1_ref[...] = x1
    u2 = (x1 * (1.0 + mod_ref[:, 4:5, :]) + mod_ref[:, 3:4, :]).reshape(tm, D_MODEL)
    hi = u2.astype(BF16)
    lo = (u2 - hi.astype(F32)).astype(BF16)
    rl = _dot(hi, wrh_ref[0]) + _dot(lo, wrh_ref[0]) + _dot(hi, wrl_ref[0]) + br_ref[0]
    i1, i2, w1, w2 = _route(rl)
    lane = lax.broadcasted_iota(jnp.int32, (tm, 128), 1).astype(F32)
    onehot = jnp.where((lane == i1) | (lane == i2), 1.0, 0.0)
    rr = lax.broadcasted_iota(jnp.int32, (tm, tm), 0)
    cc = lax.broadcasted_iota(jnp.int32, (tm, tm), 1)
    before = _dot((rr > cc).astype(BF16), onehot.astype(BF16)) + cnt_ref[0:1, :]
    rank1 = jnp.sum(jnp.where(lane == i1, before, 0.0), axis=-1, keepdims=True)
    rank2 = jnp.sum(jnp.where(lane == i2, before, 0.0), axis=-1, keepdims=True)
    cnt_ref[0:1, :] += jnp.sum(onehot, axis=0, keepdims=True)
    rinfo = jnp.zeros((tm, 128), F32)
    for k, val in enumerate((i1, i2, w1, w2, rank1, rank2)):
        rinfo = jnp.where(lane == k, val, rinfo)
    rinfo_ref[...] = rinfo.reshape(bt, lt, 128)


def _merge(ya, yb, yc, proj, x, mod, wa, wb, wc, wo, lg, lb, wrh, wrl, br, layer, bt, lt):
    nb, lp, _ = x.shape
    tok = lambda i, t: (i, t, 0)
    wsp = lambda shape: pl.BlockSpec((1,) + shape, lambda i, t: (layer, 0, 0))
    g0 = OFF_GTS // D_MODEL
    return pl.pallas_call(
        functools.partial(_merge_body, bt=bt, lt=lt),
        grid=(nb // bt, lp // lt),
        in_specs=[pl.BlockSpec((bt, lt, D_MIX), tok),
                  pl.BlockSpec((bt, lt, D_MIX), tok),
                  pl.BlockSpec((bt, lt, D_MIX), tok),
                  pl.BlockSpec((bt, lt, D_MODEL), lambda i, t: (i, t, g0)),
                  pl.BlockSpec((bt, lt, D_MODEL), lambda i, t: (i, t, g0 + 1)),
                  pl.BlockSpec((bt, lt, D_MODEL), lambda i, t: (i, t, g0 + 2)),
                  pl.BlockSpec((bt, lt, D_MODEL), tok),
                  pl.BlockSpec((bt, 6, D_MODEL), lambda i, t: (i, 0, 0)),
                  wsp((D_MIX, D_MODEL)), wsp((D_MIX, D_MODEL)), wsp((D_MIX, D_MODEL)), wsp((D_MODEL, D_MODEL)),
                  wsp((1, D_MODEL)), wsp((1, D_MODEL)),
                  wsp((D_MODEL, 128)), wsp((D_MODEL, 128)), wsp((1, 128))],
        out_specs=[pl.BlockSpec((bt, lt, D_MODEL), tok),
                   pl.BlockSpec((bt, lt, 128), tok),
                   pl.BlockSpec((8, 128), lambda i, t: (0, 0))],
        out_shape=[jax.ShapeDtypeStruct((nb, lp, D_MODEL), F32),
                   jax.ShapeDtypeStruct((nb, lp, 128), F32),
                   jax.ShapeDtypeStruct((8, 128), F32)],
        compiler_params=pltpu.CompilerParams(dimension_semantics=("arbitrary", "arbitrary"),
                                             vmem_limit_bytes=VMEM_LIMIT),
        name=f"merge_l{layer}_b{bt}",
    )(ya, yb, yc, proj, proj, proj, x, mod, wa, wb, wc, wo, lg, lb, wrh, wrl, br)


def _moe_plan(rinfo, cnt, tm, mb):
    nb, lp, _ = rinfo.shape
    n_tok = nb * lp
    n_blocks = 2 * n_tok // mb + N_EXPERTS
    counts = cnt[0, :N_EXPERTS].astype(jnp.int32)
    nblk = (counts + mb - 1) // mb
    pend = jnp.cumsum(nblk)
    pstart = pend - nblk
    expert = rinfo[..., 0:2].astype(jnp.int32)
    rank = rinfo[..., 4:6].astype(jnp.int32)
    ids = jnp.arange(N_EXPERTS, dtype=jnp.int32)
    first_row = jnp.sum(jnp.where(expert[..., None] == ids, pstart * mb, 0), axis=-1)
    pos = (first_row + rank).reshape(n_tok // tm, 1, 2 * tm)
    block_e = jnp.sum(pend[None, :] <= jnp.arange(n_blocks, dtype=jnp.int32)[:, None], axis=1)
    block_e = jnp.minimum(block_e, N_EXPERTS - 1).astype(jnp.int32)
    n_used = pend[N_EXPERTS - 1:].astype(jnp.int32)
    ztail = jnp.where(nblk > 0, (pend - 1) * mb, -1).astype(jnp.int32)
    ztail = jnp.concatenate([ztail, n_used]).reshape(1, 1, N_EXPERTS + 1)
    return pos, block_e, n_used, ztail, n_blocks


def _row_copy(src, dst, sem):
    return pltpu.make_async_copy(src, dst, sem)


def _dispatch_body(pos_ref, ztail_ref, x1_ref, mod_ref, disp_ref, u_scr, z_scr, sem, zsem, *, bt, lt, n_blocks, mb):
    tm = bt * lt

    @pl.when((pl.program_id(0) == 0) & (pl.program_id(1) == 0))
    def _():
        z_scr[...] = jnp.zeros_like(z_scr)
        for e in range(N_EXPERTS):
            @pl.when(ztail_ref[0, 0, e] >= 0)
            def _():
                row = pl.multiple_of(ztail_ref[0, 0, e], mb)
                _row_copy(z_scr, disp_ref.at[pl.ds(row, mb)], zsem).start()
        def unused(j):
            return _row_copy(z_scr, disp_ref.at[pl.ds(pl.multiple_of(j * mb, mb), mb)], zsem)

        n_used = ztail_ref[0, 0, N_EXPERTS]
        lax.fori_loop(n_used, n_blocks, lambda j, c: (unused(j).start(), c)[1], 0)
        lax.fori_loop(n_used, n_blocks, lambda j, c: (unused(j).wait(), c)[1], 0)
        for e in range(N_EXPERTS):
            @pl.when(ztail_ref[0, 0, e] >= 0)
            def _():
                row = pl.multiple_of(ztail_ref[0, 0, e], mb)
                _row_copy(z_scr, disp_ref.at[pl.ds(row, mb)], zsem).wait()

    u2 = x1_ref[...] * (1.0 + mod_ref[:, 4:5, :]) + mod_ref[:, 3:4, :]
    u_scr[...] = u2.reshape(tm, D_MODEL)

    def issue(t, carry):
        for k in range(2):
            _row_copy(u_scr.at[pl.ds(t, 1)], disp_ref.at[pl.ds(pos_ref[0, 0, 2 * t + k], 1)], sem).start()
        return carry

    lax.fori_loop(0, tm, issue, 0, unroll=8)
    for k in range(2):
        _row_copy(u_scr, disp_ref.at[pl.ds(0, tm)], sem).wait()


def _dispatch(x1, mod, pos, ztail, n_blocks, mb, bt, lt):
    nb, lp, _ = x1.shape
    nt = lp // lt
    return pl.pallas_call(
        functools.partial(_dispatch_body, bt=bt, lt=lt, n_blocks=n_blocks, mb=mb),
        grid=(nb // bt, nt),
        in_specs=[pl.BlockSpec((1, 1, 2 * bt * lt), lambda i, t: (i * nt + t, 0, 0), memory_space=pltpu.SMEM),
                  pl.BlockSpec((1, 1, N_EXPERTS + 1), lambda i, t: (0, 0, 0), memory_space=pltpu.SMEM),
                  pl.BlockSpec((bt, lt, D_MODEL), lambda i, t: (i, t, 0)),
                  pl.BlockSpec((bt, 6, D_MODEL), lambda i, t: (i, 0, 0))],
        out_specs=pl.BlockSpec(memory_space=pl.ANY),
        out_shape=jax.ShapeDtypeStruct((n_blocks * mb, D_MODEL), F32),
        scratch_shapes=[pltpu.VMEM((bt * lt, D_MODEL), F32), pltpu.VMEM((mb, D_MODEL), F32),
                        pltpu.SemaphoreType.DMA, pltpu.SemaphoreType.DMA],
        compiler_params=pltpu.CompilerParams(dimension_semantics=("arbitrary", "arbitrary"),
                                             vmem_limit_bytes=VMEM_LIMIT),
        name=f"dispatch_b{bt}",
    )(pos, ztail, x1, mod)


def _experts_body(be_ref, nu_ref, x_ref, wg_ref, wu_ref, wd_ref, o_ref):
    j = pl.program_id(0)

    @pl.when(j < nu_ref[0])
    def _():
        x = x_ref[...].astype(BF16)
        hb = _silu(_dot(x, wg_ref[0, 0].astype(BF16))) * _dot(x, wu_ref[0, 0].astype(BF16))
        o_ref[...] = _dot(hb.astype(BF16), wd_ref[0, 0].astype(BF16))

    @pl.when(j >= nu_ref[0])
    def _():
        o_ref[...] = jnp.zeros_like(o_ref)


def _experts(disp, block_e, n_used, wg, wu, wd, layer, mb):
    n_blocks = disp.shape[0] // mb
    wmap = lambda j, be, nu: (layer, be[j], 0, 0)
    return pl.pallas_call(
        _experts_body,
        grid_spec=pltpu.PrefetchScalarGridSpec(
            num_scalar_prefetch=2,
            grid=(n_blocks,),
            in_specs=[pl.BlockSpec((mb, D_MODEL), lambda j, be, nu: (jnp.minimum(j, nu[0] - 1), 0)),
                      pl.BlockSpec((1, 1, D_MODEL, D_EXPERT), wmap),
                      pl.BlockSpec((1, 1, D_MODEL, D_EXPERT), wmap),
                      pl.BlockSpec((1, 1, D_EXPERT, D_MODEL), wmap)],
            out_specs=pl.BlockSpec((mb, D_MODEL), lambda j, be, nu: (j, 0))),
        out_shape=jax.ShapeDtypeStruct(disp.shape, F32),
        compiler_params=pltpu.CompilerParams(dimension_semantics=("arbitrary",), vmem_limit_bytes=VMEM_LIMIT),
        name=f"experts_l{layer}_n{n_blocks}",
    )(block_e, n_used, disp, wg, wu, wd)


def _combine_body(pos_ref, x1_ref, mod_ref, rinfo_ref, eo_ref, lg_ref, lb_ref, x2_ref, r_scr, sem, *, bt, lt):
    tm = bt * lt

    def issue(t, carry):
        for k in range(2):
            _row_copy(eo_ref.at[pl.ds(pos_ref[0, 0, 2 * t + k], 1)], r_scr.at[k, pl.ds(t, 1)], sem).start()
        return carry

    lax.fori_loop(0, tm, issue, 0, unroll=8)
    for k in range(2):
        _row_copy(eo_ref.at[pl.ds(0, tm)], r_scr.at[k], sem).wait()

    rinfo = rinfo_ref[...].reshape(tm, 128)
    moe = r_scr[0] * rinfo[:, 2:3] + r_scr[1] * rinfo[:, 3:4]
    y = DN_ALPHA * x1_ref[...] + (1.0 + mod_ref[:, 5:6, :]) * moe.reshape(bt, lt, D_MODEL)
    x2_ref[...] = _layer_norm(y, lg_ref[0], lb_ref[0])


def _combine(x1, mod, rinfo, eo, pos, lg, lb, layer, bt, lt):
    nb, lp, _ = x1.shape
    nt = lp // lt
    tok = lambda i, t: (i, t, 0)
    return pl.pallas_call(
        functools.partial(_combine_body, bt=bt, lt=lt),
        grid=(nb // bt, nt),
        in_specs=[pl.BlockSpec((1, 1, 2 * bt * lt), lambda i, t: (i * nt + t, 0, 0), memory_space=pltpu.SMEM),
                  pl.BlockSpec((bt, lt, D_MODEL), tok),
                  pl.BlockSpec((bt, 6, D_MODEL), lambda i, t: (i, 0, 0)),
                  pl.BlockSpec((bt, lt, 128), tok),
                  pl.BlockSpec(memory_space=pl.ANY),
                  pl.BlockSpec((1, 1, D_MODEL), lambda i, t: (layer, 0, 0)),
                  pl.BlockSpec((1, 1, D_MODEL), lambda i, t: (layer, 0, 0))],
        out_specs=pl.BlockSpec((bt, lt, D_MODEL), tok),
        out_shape=jax.ShapeDtypeStruct((nb, lp, D_MODEL), F32),
        scratch_shapes=[pltpu.VMEM((2, bt * lt, D_MODEL), F32), pltpu.SemaphoreType.DMA],
        compiler_params=pltpu.CompilerParams(dimension_semantics=("arbitrary", "arbitrary"),
                                             vmem_limit_bytes=VMEM_LIMIT),
        name=f"combine_l{layer}_b{bt}",
    )(pos, x1, mod, rinfo, eo, lg, lb)


def _hist(state):
    return jnp.pad(state, ((0, 0), (HIST - state.shape[1], 0), (0, 0)))


def _trunk(x, mod, st, p, bt, lt, lv, mix, moe_block, ip_tile):
    nb = x.shape[0]
    new = {key: [] for key in ('n', 'm', 'conv', 'gconv')}
    C = S = None
    for l in range(DEPTH):
        proj, x = _inproj(x, mod[l], p['ln_in_g'], p['ln_in_b'], p['w_in_r'], p['b_in_r'], l, l == 0, *ip_tile)
        m0 = jnp.broadcast_to(st['m'][l][:, :, None, None], (nb, N_HEADS, 1, 128))
        yb, conv, *pre = _mlstm_prep(proj, _hist(st['conv'][l]), p['conv_b_w8'][l], mix['NB'], mix['G'], mix['tl'], lv)
        ya, C, n, m = _mlstm_scan(pre, proj, st['C'], st['n'][l], m0, p['mlstm_norm_g'][l:l + 1],
                                  mix['SB'], mix['G'], mix['tl'], l, C)
        *pre, gconv = _gdn_prep(proj, _hist(st['gconv'][l]), p['conv_c_w8'][l], p['alog_row'][l], p['dtb_row'][l],
                                mix['NB'], mix['G'], mix['tl'], lv)
        yc, S = _gdn_scan(pre, proj, st['S'], p['gdn_norm_g'][l:l + 1], mix['SB'], mix['G'], mix['tl'], l, S)
        x1, rinfo, cnt = _merge(ya, yb, yc, proj, x, mod[l], p['w_br_a'], p['w_br_b'], p['w_br_c'], p['w_out'],
                                p['ln1_g'], p['ln1_b'], p['wr_hi'], p['wr_lo'], p['br'], l, bt, lt)
        pos, block_e, n_used, ztail, n_blocks = _moe_plan(rinfo, cnt, bt * lt, moe_block)
        disp = _dispatch(x1, mod[l], pos, ztail, n_blocks, moe_block, bt, lt)
        eo = _experts(disp, block_e, n_used, p['exp_w_gate'], p['exp_w_up'], p['exp_w_down'], l, moe_block)
        x = _combine(x1, mod[l], rinfo, eo, pos, p['ln2_g'], p['ln2_b'], l, bt, lt)
        new['n'].append(n)
        new['m'].append(m[:, :, 0, 0])
        new['conv'].append(conv)
        new['gconv'].append(gconv)
    return x, dict({key: jnp.stack(val) for key, val in new.items()}, C=C, S=S)


def kernel(x_prompt, x_sample, state_mlstm_C, state_mlstm_n, state_mlstm_m, state_conv, state_gdn_S, state_gdn_conv, c_prompt, c_sample, ln_in_g, ln_in_b, w_ada, b_ada, w_in, b_in, mlstm_norm_g, conv_b_w, conv_c_w, gdn_a_log, gdn_dt_bias, gdn_norm_g, w_br_a, w_br_b, w_br_c, w_out, ln1_g, ln1_b, router_g_w, router_g_b, router_e_w, router_e_b, exp_w_gate, exp_w_up, exp_w_down, ln2_g, ln2_b):
    nbp, lp, _ = x_prompt.shape
    nbs, ls, _ = x_sample.shape
    lsp = 8

    def regroup(a):
        out = jnp.zeros(a.shape[:-1] + (N_PROJ,), a.dtype)
        for src, end, dst in PROJ_SEGMENTS:
            out = out.at[..., dst:dst + end - src].set(a[..., src:end])
        return out

    wr = jnp.concatenate([router_e_w, router_g_w, jnp.zeros((DEPTH, D_MODEL, 128 - N_EXPERTS - N_GROUPS), F32)], axis=-1)
    wr_hi = wr.astype(BF16)
    lane_pad = lambda a: jnp.pad(a, ((0, 0), (4, 128 - 4 - N_HEADS)))[:, None, :]
    p = dict(
        ln_in_g=ln_in_g.reshape(1, D_MODEL), ln_in_b=ln_in_b.reshape(1, D_MODEL),
        w_in_r=_regroup_w_in(w_in), b_in_r=regroup(b_in).reshape(DEPTH, 1, N_PROJ),
        mlstm_norm_g=mlstm_norm_g,
        conv_b_w8=jnp.pad(conv_b_w, ((0, 0), (0, 8 - CONV_B), (0, 0))),
        conv_c_w8=jnp.pad(conv_c_w, ((0, 0), (0, 8 - CONV_C), (0, 0))),
        alog_row=lane_pad(gdn_a_log), dtb_row=lane_pad(gdn_dt_bias), gdn_norm_g=gdn_norm_g,
        w_br_a=w_br_a.astype(BF16), w_br_b=w_br_b.astype(BF16), w_br_c=w_br_c.astype(BF16),
        w_out=w_out.astype(BF16),
        ln1_g=ln1_g.reshape(DEPTH, 1, D_MODEL), ln1_b=ln1_b.reshape(DEPTH, 1, D_MODEL),
        wr_hi=wr_hi, wr_lo=(wr - wr_hi.astype(F32)).astype(BF16),
        br=jnp.concatenate([router_e_b, router_g_b, jnp.zeros((DEPTH, 128 - N_EXPERTS - N_GROUPS), F32)],
                           axis=-1).reshape(DEPTH, 1, 128),
        exp_w_gate=exp_w_gate, exp_w_up=exp_w_up, exp_w_down=exp_w_down,
        ln2_g=ln2_g.reshape(DEPTH, 1, D_MODEL), ln2_b=ln2_b.reshape(DEPTH, 1, D_MODEL),
    )

    mod = _ada(jnp.concatenate([c_prompt, c_sample], axis=0), w_ada, b_ada)
    mod = mod.reshape(DEPTH, nbp + nbs, 6, D_MODEL)

    zeros = lambda *s: jnp.zeros((DEPTH, nbp) + s, F32)
    st_p = {'C': zeros(N_HEADS, DH, DH), 'n': zeros(N_HEADS, DH), 'm': zeros(N_HEADS),
            'conv': zeros(CONV_B - 1, D_MIX), 'S': zeros(N_HEADS, DH, DH), 'gconv': zeros(CONV_C - 1, 3 * D_MIX)}
    y_p, sp = _trunk(x_prompt, mod[:, :nbp], st_p, p, bt=1, lt=512, lv=lp,
                     mix=dict(NB=8, G=1, tl=RB, SB=nbp), moe_block=256, ip_tile=(1, 256))

    st_s = {'C': state_mlstm_C, 'n': state_mlstm_n, 'm': state_mlstm_m, 'conv': state_conv,
            'S': state_gdn_S, 'gconv': state_gdn_conv}
    xs = jnp.pad(x_sample, ((0, 0), (0, lsp - ls), (0, 0)))
    y_s, ss = _trunk(xs, mod[:, nbp:], st_s, p, bt=64, lt=lsp, lv=ls,
                     mix=dict(NB=2, G=RB // lsp, tl=lsp, SB=1), moe_block=128, ip_tile=(32, lsp))
    y_s = y_s[:, :ls]

    return (y_p, y_s, sp['C'], sp['n'], sp['m'], sp['conv'], sp['S'], sp['gconv'],
            ss['C'], ss['n'], ss['m'], ss['conv'], ss['S'], ss['gconv'])
```

```python
import functools

import jax
import jax.numpy as jnp
from jax import lax
from jax.experimental import pallas as pl
from jax.experimental.pallas import tpu as pltpu

F32 = jnp.float32
BF16 = jnp.bfloat16

D_MODEL = 1024
DEPTH = 2
N_HEADS = 4
DH = 128
D_MIX = N_HEADS * DH
N_EXPERTS = 32
EXPERTS_PER_GROUP = 8
N_GROUPS = 4
D_EXPERT = 256
CONV_B = 3
CONV_C = 4
HIST = 8
RB = 64
INV_BLOCK = 8
DN_ALPHA = (2 * DEPTH) ** 0.25
LN_EPS = 1e-5
NORM_EPS = 1e-6
NEG = -1e30

OFF_QKVC = 0
OFF_BCH = 1536
OFF_GTS = 3072
OFF_QKVO = 6144
OFF_Z = 8192
OFF_SA = 8704
OFF_SB = 8832
N_PROJ = 8960
TN_PROJ = 1280
PROJ_SEGMENTS = ((3592, 5128, OFF_QKVC), (2056, 3592, OFF_BCH), (5648, 8720, OFF_GTS), (0, 2048, OFF_QKVO),
                 (5128, 5640, OFF_Z), (2048, 2052, OFF_SA), (5640, 5644, OFF_SA + 4), (2052, 2056, OFF_SB),
                 (5644, 5648, OFF_SB + 4))

VMEM_LIMIT = 52 * 1024 * 1024


def _dot(a, b):
    return jnp.dot(a, b, preferred_element_type=F32)


def _split3(x):
    hi = x.astype(BF16)
    r = x - hi.astype(F32)
    mid = r.astype(BF16)
    lo = (r - mid.astype(F32)).astype(BF16)
    return hi, mid, lo


def _layer_norm(x, g, b):
    mu = jnp.mean(x, axis=-1, keepdims=True)
    xc = x - mu
    var = jnp.mean(xc * xc, axis=-1, keepdims=True)
    return xc * lax.rsqrt(var + LN_EPS) * g + b


def _sigmoid(x):
    return jax.nn.sigmoid(x)


def _silu(x):
    return x * jax.nn.sigmoid(x)


def _log_sigmoid(x):
    return jnp.minimum(x, 0.0) - jnp.log1p(jnp.exp(-jnp.abs(x)))


def _softplus(x):
    return jnp.maximum(x, 0.0) + jnp.log1p(jnp.exp(-jnp.abs(x)))


def _ada_body(c_ref, w_ref, b_ref, o_ref):
    c = c_ref[...]
    s = _silu(c).astype(BF16)
    o_ref[0] = _dot(s, w_ref[0].astype(BF16)) + b_ref[0]


def _ada(c_all, w_ada, b_ada):
    nb = c_all.shape[0]
    return pl.pallas_call(
        _ada_body,
        grid=(DEPTH, 6),
        in_specs=[pl.BlockSpec((nb, D_MODEL), lambda l, j: (0, 0)),
                  pl.BlockSpec((1, D_MODEL, D_MODEL), lambda l, j: (l, 0, j)),
                  pl.BlockSpec((1, 1, D_MODEL), lambda l, j: (l, 0, j))],
        out_specs=pl.BlockSpec((1, nb, D_MODEL), lambda l, j: (l, 0, j)),
        out_shape=jax.ShapeDtypeStruct((DEPTH, nb, 6 * D_MODEL), F32),
        compiler_params=pltpu.CompilerParams(dimension_semantics=("arbitrary", "arbitrary"),
                                             vmem_limit_bytes=VMEM_LIMIT),
        name="ada",
    )(c_all, w_ada, b_ada.reshape(DEPTH, 1, 6 * D_MODEL))


def _regroup_body(w_ref, o_ref):
    o_ref[...] = jnp.zeros_like(o_ref)
    for src, end, dst in PROJ_SEGMENTS:
        o_ref[0, :, dst:dst + end - src] = w_ref[0, :, src:end].astype(BF16)


def _regroup_w_in(w_in):
    n_in = w_in.shape[-1]
    rows = 256
    return pl.pallas_call(
        _regroup_body,
        grid=(DEPTH, D_MODEL // rows),
        in_specs=[pl.BlockSpec((1, rows, n_in), lambda l, i: (l, i, 0))],
        out_specs=pl.BlockSpec((1, rows, N_PROJ), lambda l, i: (l, i, 0)),
        out_shape=jax.ShapeDtypeStruct((DEPTH, D_MODEL, N_PROJ), BF16),
        compiler_params=pltpu.CompilerParams(dimension_semantics=("arbitrary", "arbitrary"),
                                             vmem_limit_bytes=VMEM_LIMIT),
        name="regroup_w_in",
    )(w_in)


def _inproj_body(x_ref, mod_ref, g_ref, b_ref, w_ref, bias_ref, proj_ref, *rest, apply_ln, bt, lt):
    x = x_ref[...]
    if apply_ln:
        x = _layer_norm(x, g_ref[...], b_ref[...])
        rest[0][...] = x
    u = (x * (1.0 + mod_ref[:, 1:2, :]) + mod_ref[:, 0:1, :]).reshape(bt * lt, D_MODEL).astype(BF16)
    for j in range(N_PROJ // TN_PROJ):
        cs = slice(j * TN_PROJ, (j + 1) * TN_PROJ)
        proj_ref[:, :, cs] = (_dot(u, w_ref[0, :, cs]) + bias_ref[0, :, cs]).reshape(bt, lt, TN_PROJ)


def _inproj(x, mod, ln_g, ln_b, w_r, b_r, layer, apply_ln, bt, lt):
    nb, lp, _ = x.shape
    tok = lambda i, t: (i, t, 0)
    out_shape = [jax.ShapeDtypeStruct((nb, lp, N_PROJ), F32)]
    out_specs = [pl.BlockSpec((bt, lt, N_PROJ), tok)]
    if apply_ln:
        out_shape.append(jax.ShapeDtypeStruct((nb, lp, D_MODEL), F32))
        out_specs.append(pl.BlockSpec((bt, lt, D_MODEL), tok))
    res = pl.pallas_call(
        functools.partial(_inproj_body, apply_ln=apply_ln, bt=bt, lt=lt),
        grid=(nb // bt, lp // lt),
        in_specs=[pl.BlockSpec((bt, lt, D_MODEL), tok),
                  pl.BlockSpec((bt, 6, D_MODEL), lambda i, t: (i, 0, 0)),
                  pl.BlockSpec((1, D_MODEL), lambda i, t: (0, 0)),
                  pl.BlockSpec((1, D_MODEL), lambda i, t: (0, 0)),
                  pl.BlockSpec((1, D_MODEL, N_PROJ), lambda i, t: (layer, 0, 0), pipeline_mode=pl.Buffered(1)),
                  pl.BlockSpec((1, 1, N_PROJ), lambda i, t: (layer, 0, 0))],
        out_specs=out_specs,
        out_shape=out_shape,
        compiler_params=pltpu.CompilerParams(dimension_semantics=("arbitrary", "arbitrary"),
                                             vmem_limit_bytes=VMEM_LIMIT),
        name=f"inproj_l{layer}_b{bt}",
    )(x, mod, ln_g, ln_b, w_r, b_r)
    return (res[0], res[1]) if apply_ln else (res[0], x)


def _conv_taps(xp_s, w_ref, width, tl):
    acc = None
    for j in range(width):
        tap = xp_s[:, pl.ds(HIST - (width - 1) + j, tl), :] * w_ref[j:j + 1, :].reshape(1, 1, -1)
        acc = tap if acc is None else acc + tap
    return acc


def _conv_history(xp_s, hist_ref, prev_ref, n_steps):
    if n_steps > 1:
        @pl.when(pl.program_id(1) == 0)
        def _():
            xp_s[:, 0:HIST, :] = hist_ref[...]

        @pl.when(pl.program_id(1) > 0)
        def _():
            xp_s[:, 0:HIST, :] = prev_ref[...]
    else:
        xp_s[:, 0:HIST, :] = hist_ref[...]


def _heads(x, nb, width):
    return jnp.stack([x[:, :, h * width:(h + 1) * width] for h in range(N_HEADS)],
                     axis=1).reshape(nb * N_HEADS, RB, width)


def _gate_cols(x, nb, lane0):
    return jnp.stack([x[:, :, lane0 + h:lane0 + h + 1] for h in range(N_HEADS)],
                     axis=1).reshape(nb * N_HEADS, RB, 1)


def _gate_rows(x, nb, lane0):
    xt = jnp.swapaxes(x, 1, 2)
    return jnp.stack([xt[:, lane0 + h:lane0 + h + 1, :] for h in range(N_HEADS)],
                     axis=1).reshape(nb * N_HEADS, 1, RB)


def _bmm(a, b):
    return jnp.einsum('nts,nsu->ntu', a.astype(BF16), b.astype(BF16), preferred_element_type=F32)


def _bmm_nt(a, b):
    return jnp.einsum('ntd,nsd->nts', a, b, preferred_element_type=F32)


def _block_masks(tl):
    rr = lax.broadcasted_iota(jnp.int32, (RB, RB), 0)
    cc = lax.broadcasted_iota(jnp.int32, (RB, RB), 1)
    incl = rr >= cc
    if tl < RB:
        incl = incl & ((rr // tl) == (cc // tl))
    return rr, cc, incl


def _seq_cumsum(x, incl, nb):
    tril = jnp.broadcast_to(incl.astype(BF16)[None], (nb, RB, RB))
    hi, mid, lo = _split3(x)
    return _bmm(tril, hi) + _bmm(tril, mid) + _bmm(tril, lo)


def _seq_last(x, nb, G, tl):
    return x.reshape(nb * G, tl, 128)[:, tl - 1:tl, :]


def _seq_rows(x3, nb, tl):
    return jnp.broadcast_to(x3, (x3.shape[0], tl, 128)).reshape(nb, RB, 128)


def _put_chains(ref, val, nb, G, tl):
    val4 = val.reshape(nb, N_HEADS, RB, val.shape[-1])
    for h in range(N_HEADS):
        ref[:, :, h] = val4[:, h].reshape(nb, G, tl, val.shape[-1])


def _put_seq_scalars(ref, x3, nb, G, lane0):
    x4 = x3.reshape(nb, G, 1, 128)
    for h in range(N_HEADS):
        ref[:, :, h] = jnp.broadcast_to(x4[:, :, :, lane0 + h:lane0 + h + 1], (nb, G, 1, 128))


def _step_tiling(nb, lp, NB, G, tl):
    assert G * tl == RB
    if G == 1:
        gx, tlx = 1, NB * RB
    else:
        assert lp == tl
        gx, tlx = NB * G, tl
    return gx, tlx, nb // gx, lp // tlx


def _layered_state(layer, nb, NS, prev):
    st_in = pl.BlockSpec((None, NS, N_HEADS, DH, DH), lambda i, c: (layer, i, 0, 0, 0))
    shape = jax.ShapeDtypeStruct((DEPTH, nb, N_HEADS, DH, DH), F32)
    if layer == 0:
        return st_in, pl.BlockSpec((DEPTH, NS, N_HEADS, DH, DH), lambda i, c: (0, i, 0, 0, 0)), shape, [], []
    return st_in, st_in, shape, [pl.BlockSpec(memory_space=pl.ANY)], [prev]


def _fill_later_layers(ref, when):
    @pl.when(when)
    def _():
        for l in range(1, DEPTH):
            ref[l] = ref[0]


def _mlstm_prep_body(qkvo_ref, bch_ref, prev_ref, sa_ref, sb_ref, cv0_ref, cw_ref,
                     yb_ref, cv_ref, nv_ref, q_ref, kw_ref, v_ref, rows_ref, bl_ref, bc_ref, kn_ref, xp_s,
                     *, NB, G, tl, tlx, NCS, lv):
    cs = pl.program_id(1)
    n = NB * N_HEADS
    lvl = lv - (NCS - 1) * tlx

    if NCS > 1:
        @pl.when(cs == 0)
        def _():
            xp_s[:, 0:HIST, :] = cv0_ref[...]

        @pl.when(cs > 0)
        def _():
            xp_s[:, 0:HIST, :] = prev_ref[:, :, D_MIX:2 * D_MIX] * prev_ref[:, :, 2 * D_MIX:3 * D_MIX]
    else:
        xp_s[:, 0:HIST, :] = cv0_ref[...]
    xp_s[:, HIST:HIST + tlx, :] = bch_ref[:, :, D_MIX:2 * D_MIX] * bch_ref[:, :, 2 * D_MIX:3 * D_MIX]
    yb_ref[...] = (bch_ref[:, :, 0:D_MIX] * _conv_taps(xp_s, cw_ref, CONV_B, tlx)).astype(BF16)

    @pl.when(cs == NCS - 1)
    def _():
        cv_ref[...] = xp_s[:, pl.ds(HIST + lvl - (CONV_B - 1), CONV_B - 1), :]

    i_all = sa_ref[...].reshape(NB, RB, 128)
    f_all = _log_sigmoid(sb_ref[...].reshape(NB, RB, 128))
    if lv < NCS * tlx:
        assert NCS == 1
        valid = (lax.broadcasted_iota(jnp.int32, (NB, RB, 128), 1) % tl) < lv
        i_all = jnp.where(valid, i_all, NEG)
        f_all = jnp.where(valid, f_all, 0.0)
    _, _, incl = _block_masks(tl)
    bcum = _seq_cumsum(f_all, incl, NB)
    blast = _seq_last(bcum, NB, G, tl)
    val = _seq_rows(blast, NB, tl) - bcum + i_all
    bmax = jnp.max(val.reshape(NB * G, tl, 128), axis=1, keepdims=True)
    wk0 = jnp.exp(val - _seq_rows(bmax, NB, tl))

    qkvo = qkvo_ref[...].reshape(NB, RB, 4 * D_MIX)
    q = _heads(qkvo[:, :, 0:D_MIX], NB, DH)
    k = _heads(qkvo[:, :, D_MIX:2 * D_MIX], NB, DH) * (DH ** -0.5)
    v = _heads(qkvo[:, :, 2 * D_MIX:3 * D_MIX], NB, DH)
    qb, kb, vb = q.astype(BF16), k.astype(BF16), v.astype(BF16)

    b_col = _gate_cols(bcum, NB, 0)
    dlog = jnp.where(incl[None], b_col - _gate_rows(bcum, NB, 0) + _gate_rows(i_all, NB, 0), NEG)
    d = jnp.max(dlog, axis=-1, keepdims=True)
    s0 = _bmm_nt(qb, kb) * jnp.exp(dlog - d)
    kw0 = k * _gate_cols(wk0, NB, 0)

    _put_chains(nv_ref, _bmm(s0, vb), NB, G, tl)
    _put_chains(q_ref, qb, NB, G, tl)
    _put_chains(kw_ref, kw0.astype(BF16), NB, G, tl)
    _put_chains(v_ref, vb, NB, G, tl)
    rows = jnp.swapaxes(jnp.concatenate([d, b_col, jnp.sum(s0, axis=-1, keepdims=True),
                                         jnp.zeros((n, RB, 5), F32)], axis=-1), 1, 2).reshape(NB, N_HEADS, 8, RB)
    for h in range(N_HEADS):
        for g in range(G):
            rows_ref[:, g, h] = rows[:, h, :, g * tl:(g + 1) * tl]
    _put_seq_scalars(bl_ref, blast, NB, G, 0)
    _put_seq_scalars(bc_ref, bmax, NB, G, 0)
    kn = jnp.sum(kw0.reshape(n * G, tl, DH), axis=1, keepdims=True).reshape(NB, N_HEADS, G, 1, DH)
    for h in range(N_HEADS):
        kn_ref[:, :, h] = kn[:, h]


def _mlstm_prep(proj, cv0, cw, NB, G, tl, lv):
    nb, lp, _ = proj.shape
    gx, tlx, NI, NCS = _step_tiling(nb, lp, NB, G, tl)
    nbt = nb * lp // RB
    step = lambda i, c: (i * NCS + c, 0, 0, 0, 0)
    chain = lambda last, dt: jax.ShapeDtypeStruct((nbt, G, N_HEADS, tl, last), dt)
    cspec = lambda last: pl.BlockSpec((NB, G, N_HEADS, tl, last), step)
    scal = jax.ShapeDtypeStruct((nbt, G, N_HEADS, 1, 128), F32)
    sspec = pl.BlockSpec((NB, G, N_HEADS, 1, 128), step)
    bch = OFF_BCH // (3 * D_MIX)
    return pl.pallas_call(
        functools.partial(_mlstm_prep_body, NB=NB, G=G, tl=tl, tlx=tlx, NCS=NCS, lv=lv),
        grid=(NI, NCS),
        in_specs=[pl.BlockSpec((gx, tlx, 4 * D_MIX), lambda i, c: (i, c, OFF_QKVO // (4 * D_MIX))),
                  pl.BlockSpec((gx, tlx, 3 * D_MIX), lambda i, c: (i, c, bch)),
                  pl.BlockSpec((gx, HIST, 3 * D_MIX), lambda i, c: (i, jnp.maximum(c * (tlx // HIST) - 1, 0), bch)),
                  pl.BlockSpec((gx, tlx, 128), lambda i, c: (i, c, OFF_SA // 128)),
                  pl.BlockSpec((gx, tlx, 128), lambda i, c: (i, c, OFF_SB // 128)),
                  pl.BlockSpec((gx, HIST, D_MIX), lambda i, c: (i, 0, 0)),
                  pl.BlockSpec((8, D_MIX), lambda i, c: (0, 0))],
        out_specs=[pl.BlockSpec((gx, tlx, D_MIX), lambda i, c: (i, c, 0)),
                   pl.BlockSpec((gx, CONV_B - 1, D_MIX), lambda i, c: (i, 0, 0)),
                   cspec(DH), cspec(DH), cspec(DH), cspec(DH),
                   pl.BlockSpec((NB, G, N_HEADS, 8, tl), step), sspec, sspec, sspec],
        out_shape=[jax.ShapeDtypeStruct((nb, lp, D_MIX), BF16),
                   jax.ShapeDtypeStruct((nb, CONV_B - 1, D_MIX), F32),
                   chain(DH, F32), chain(DH, BF16), chain(DH, BF16), chain(DH, BF16),
                   jax.ShapeDtypeStruct((nbt, G, N_HEADS, 8, tl), F32),
                   scal, scal, scal],
        scratch_shapes=[pltpu.VMEM((gx, HIST + tlx, D_MIX), F32)],
        compiler_params=pltpu.CompilerParams(dimension_semantics=("arbitrary", "arbitrary"),
                                             vmem_limit_bytes=VMEM_LIMIT),
        name=f"mlstm_prep_g{G}",
    )(proj, proj, proj, proj, proj, cv0, cw)


def _mlstm_scan_body(nv_ref, q_ref, kw_ref, v_ref, rows_ref, bl_ref, bc_ref, kn_ref, o_ref, C0_ref, n0_ref, m0_ref,
                     ng_ref, *rest, NS, tl, NC, first):
    ya_ref, C_out, n_ref, m_ref = rest[-4:]
    C_ref = C_out.at[0] if first else C_out
    c = pl.program_id(1)
    n = NS * N_HEADS

    @pl.when(c == 0)
    def _():
        C_ref[...] = C0_ref[...]
        n_ref[...] = n0_ref[...]
        m_ref[...] = m0_ref[...]

    chains = lambda ref: ref[...].reshape(n, ref.shape[-2], ref.shape[-1])
    C = C_ref[...].reshape(n, DH, DH)
    nvec = n_ref[...].reshape(n, 1, DH)
    m_prev = chains(m_ref)[:, :, 0:1]
    rows = chains(rows_ref)
    d, b, ds0 = rows[:, 0:1, :], rows[:, 1:2, :], rows[:, 2:3, :]
    qb = chains(q_ref)

    m_inter = b + m_prev
    m_t = jnp.maximum(m_inter, d)
    f = jnp.exp(d - m_t)
    inter = jnp.exp(m_inter - m_t)
    qn = jnp.einsum('nod,ntd->not', nvec.astype(BF16), qb, preferred_element_type=F32)
    den = f * ds0 + inter * qn
    scale_t = 1.0 / jnp.maximum(jnp.abs(den), jnp.exp(-m_t))
    fi = jnp.swapaxes(jnp.concatenate([f * scale_t, inter * scale_t, jnp.zeros((n, 6, tl), F32)], axis=1), 1, 2)
    qC = jnp.einsum('ntd,nde->nte', qb, C.astype(BF16), preferred_element_type=F32)
    hh = fi[:, :, 0:1] * chains(nv_ref) + fi[:, :, 1:2] * qC

    m_new = m_t[:, :, tl - 1:tl]
    decay = jnp.exp(chains(bl_ref)[:, :, 0:1] + m_prev - m_new)
    scale = jnp.exp(chains(bc_ref)[:, :, 0:1] - m_new)
    kv = jnp.einsum('ntd,nte->nde', chains(kw_ref), chains(v_ref), preferred_element_type=F32)
    C_ref[...] = (decay * C + scale * kv).reshape(NS, N_HEADS, DH, DH)
    n_ref[...] = (decay * nvec + scale * chains(kn_ref)).reshape(NS, N_HEADS, DH)
    m_ref[...] = jnp.broadcast_to(m_new, (n, 1, 128)).reshape(NS, N_HEADS, 1, 128)
    if first:
        _fill_later_layers(C_out, c == NC - 1)

    mu = jnp.mean(hh, axis=-1, keepdims=True)
    hc = hh - mu
    hn = (hc * lax.rsqrt(jnp.mean(hc * hc, axis=-1, keepdims=True) + LN_EPS)).reshape(NS, N_HEADS, tl, DH)
    for h in range(N_HEADS):
        hs = slice(h * DH, (h + 1) * DH)
        ya_ref[:, :, hs] = (_sigmoid(o_ref[:, :, hs]) * hn[:, h] * ng_ref[:, hs]).astype(BF16)


def _mlstm_scan(pre, proj, C0, n0, m0, ng, SB, G, tl, layer, C_prev):
    nb, lp, _ = proj.shape
    NS = SB * G
    NI, NC = nb // NS, lp // tl
    six = lambda a: a.reshape((NI * SB, NC) + a.shape[1:])
    spec6 = lambda a: pl.BlockSpec((SB, 1) + a.shape[1:], lambda i, c: (i, c, 0, 0, 0, 0))
    seq4 = lambda i, c: (i, 0, 0, 0)
    st_in, st_out, st_shape, extra_in, extra_args = _layered_state(layer, nb, NS, C_prev)
    n_in = len(pre) + 5
    return pl.pallas_call(
        functools.partial(_mlstm_scan_body, NS=NS, tl=tl, NC=NC, first=layer == 0),
        grid=(NI, NC),
        in_specs=[spec6(a) for a in pre] + [
            pl.BlockSpec((NS, tl, D_MIX), lambda i, c: (i, c, (OFF_QKVO + 3 * D_MIX) // D_MIX)),
            st_in,
            pl.BlockSpec((NS, N_HEADS, DH), lambda i, c: (i, 0, 0)),
            pl.BlockSpec((NS, N_HEADS, 1, 128), seq4),
            pl.BlockSpec((1, D_MIX), lambda i, c: (0, 0))] + extra_in,
        out_specs=[pl.BlockSpec((NS, tl, D_MIX), lambda i, c: (i, c, 0)),
                   st_out,
                   pl.BlockSpec((NS, N_HEADS, DH), lambda i, c: (i, 0, 0)),
                   pl.BlockSpec((NS, N_HEADS, 1, 128), seq4)],
        out_shape=[jax.ShapeDtypeStruct((nb, lp, D_MIX), BF16),
                   st_shape,
                   jax.ShapeDtypeStruct((nb, N_HEADS, DH), F32),
                   jax.ShapeDtypeStruct((nb, N_HEADS, 1, 128), F32)],
        input_output_aliases={n_in: 1} if extra_in else {},
        compiler_params=pltpu.CompilerParams(dimension_semantics=("arbitrary", "arbitrary"),
                                             vmem_limit_bytes=VMEM_LIMIT),
        name=f"mlstm_scan_g{G}",
    )(*[six(a) for a in pre], proj, C0, n0, m0, ng, *extra_args)


def _unit_lower_inverse(n, rr, cc, tl):
    eye = (rr == cc).astype(F32)[None]
    p = jnp.where(((rr // INV_BLOCK) == (cc // INV_BLOCK))[None], n, 0.0)
    x = eye + p
    b = 2
    while b < INV_BLOCK:
        p = _bmm(p, p)
        x = x + _bmm(x, p)
        b *= 2
    b = INV_BLOCK
    while b < tl:
        off = jnp.where((((rr // (2 * b)) == (cc // (2 * b))) & ((rr // b) != (cc // b)))[None], n, 0.0)
        x = x + _bmm(x, _bmm(off, x))
        b *= 2
    return x


def _gdn_prep_body(x_ref, prev_ref, sa_ref, sb_ref, gc0_ref, cw_ref, alog_ref, dtb_ref,
                   u_ref, w_ref, qg_ref, kd_ref, qkm_ref, eg_ref, gcs_ref, xp_s, *, NB, G, tl, tlx, NCS, lv):
    cs = pl.program_id(1)
    lvl = lv - (NCS - 1) * tlx

    _conv_history(xp_s, gc0_ref, prev_ref, NCS)
    xp_s[:, HIST:HIST + tlx, :] = x_ref[...]
    qkv = _silu(_conv_taps(xp_s, cw_ref, CONV_C, tlx)).reshape(NB, RB, 3 * D_MIX)

    @pl.when(cs == NCS - 1)
    def _():
        gcs_ref[...] = xp_s[:, pl.ds(HIST + lvl - (CONV_C - 1), CONV_C - 1), :]

    beta_all = _sigmoid(sa_ref[...].reshape(NB, RB, 128))
    g_all = -jnp.exp(alog_ref[...]) * _softplus(sb_ref[...].reshape(NB, RB, 128) + dtb_ref[...])
    if lv < NCS * tlx:
        assert NCS == 1
        valid = (lax.broadcasted_iota(jnp.int32, (NB, RB, 128), 1) % tl) < lv
        beta_all = jnp.where(valid, beta_all, 0.0)
        g_all = jnp.where(valid, g_all, 0.0)
    rr, cc, incl = _block_masks(tl)
    diag = rr == cc
    gam = _seq_cumsum(g_all, incl, NB)
    glast = _seq_last(gam, NB, G, tl)
    gcol = _gate_cols(gam, NB, 4)
    bcol = _gate_cols(beta_all, NB, 4)
    egcol = _gate_cols(jnp.exp(gam), NB, 4)
    kdcol = _gate_cols(jnp.exp(_seq_rows(glast, NB, tl) - gam), NB, 4)

    q = _heads(qkv[:, :, 0:D_MIX], NB, DH)
    k = _heads(qkv[:, :, D_MIX:2 * D_MIX], NB, DH)
    v = _heads(qkv[:, :, 2 * D_MIX:3 * D_MIX], NB, DH)
    q = q * lax.rsqrt(jnp.sum(q * q, axis=-1, keepdims=True) + NORM_EPS) * (DH ** -0.5)
    k = k * lax.rsqrt(jnp.sum(k * k, axis=-1, keepdims=True) + NORM_EPS)
    qb, kb = q.astype(BF16), k.astype(BF16)

    dmat = jnp.exp(jnp.where(incl[None], gcol - _gate_rows(gam, NB, 4), NEG))
    nmat = jnp.where(diag[None], 0.0, -(bcol * _bmm_nt(kb, kb) * dmat))
    rhs = jnp.concatenate([bcol * v, (bcol * egcol) * k], axis=-1)
    sol = _bmm(_unit_lower_inverse(nmat, rr, cc, tl), rhs)
    qkm = (_bmm_nt(qb, kb) * dmat).astype(BF16).reshape(NB, N_HEADS, RB, RB)

    _put_chains(u_ref, sol[:, :, 0:DH], NB, G, tl)
    _put_chains(w_ref, sol[:, :, DH:2 * DH].astype(BF16), NB, G, tl)
    _put_chains(qg_ref, (q * egcol).astype(BF16), NB, G, tl)
    _put_chains(kd_ref, (k * kdcol).astype(BF16), NB, G, tl)
    _put_seq_scalars(eg_ref, jnp.exp(glast), NB, G, 4)
    for h in range(N_HEADS):
        for g in range(G):
            qkm_ref[:, g, h] = qkm[:, h, g * tl:(g + 1) * tl, g * tl:(g + 1) * tl]


def _gdn_prep(proj, gc0, cw, alog, dtb, NB, G, tl, lv):
    nb, lp, _ = proj.shape
    assert tl % INV_BLOCK == 0 and (tl // INV_BLOCK) & (tl // INV_BLOCK - 1) == 0
    gx, tlx, NI, NCS = _step_tiling(nb, lp, NB, G, tl)
    nbt = nb * lp // RB
    step = lambda i, c: (i * NCS + c, 0, 0, 0, 0)
    par = lambda i, c: (0, 0)
    chain = lambda last, dt: jax.ShapeDtypeStruct((nbt, G, N_HEADS, tl, last), dt)
    cspec = lambda last: pl.BlockSpec((NB, G, N_HEADS, tl, last), step)
    return pl.pallas_call(
        functools.partial(_gdn_prep_body, NB=NB, G=G, tl=tl, tlx=tlx, NCS=NCS, lv=lv),
        grid=(NI, NCS),
        in_specs=[pl.BlockSpec((gx, tlx, 3 * D_MIX), lambda i, c: (i, c, OFF_QKVC // (3 * D_MIX))),
                  pl.BlockSpec((gx, HIST, 3 * D_MIX),
                               lambda i, c: (i, jnp.maximum(c * (tlx // HIST) - 1, 0), OFF_QKVC // (3 * D_MIX))),
                  pl.BlockSpec((gx, tlx, 128), lambda i, c: (i, c, OFF_SA // 128)),
                  pl.BlockSpec((gx, tlx, 128), lambda i, c: (i, c, OFF_SB // 128)),
                  pl.BlockSpec((gx, HIST, 3 * D_MIX), lambda i, c: (i, 0, 0)),
                  pl.BlockSpec((8, 3 * D_MIX), par),
                  pl.BlockSpec((1, 128), par),
                  pl.BlockSpec((1, 128), par)],
        out_specs=[cspec(DH), cspec(DH), cspec(DH), cspec(DH), cspec(tl),
                   pl.BlockSpec((NB, G, N_HEADS, 1, 128), step),
                   pl.BlockSpec((gx, CONV_C - 1, 3 * D_MIX), lambda i, c: (i, 0, 0))],
        out_shape=[chain(DH, F32), chain(DH, BF16), chain(DH, BF16), chain(DH, BF16), chain(tl, BF16),
                   jax.ShapeDtypeStruct((nbt, G, N_HEADS, 1, 128), F32),
                   jax.ShapeDtypeStruct((nb, CONV_C - 1, 3 * D_MIX), F32)],
        scratch_shapes=[pltpu.VMEM((gx, HIST + tlx, 3 * D_MIX), F32)],
        compiler_params=pltpu.CompilerParams(dimension_semantics=("arbitrary", "arbitrary"),
                                             vmem_limit_bytes=VMEM_LIMIT),
        name=f"gdn_prep_g{G}",
    )(proj, proj, proj, proj, gc0, cw, alog, dtb)


def _gdn_scan_body(u_ref, w_ref, qg_ref, kd_ref, qkm_ref, eg_ref, z_ref, S0_ref, gng_ref, *rest,
                   NS, tl, NC, first):
    yc_ref, S_out = rest[-2:]
    S_ref = S_out.at[0] if first else S_out
    c = pl.program_id(1)
    n = NS * N_HEADS

    @pl.when(c == 0)
    def _():
        S_ref[...] = S0_ref[...]

    S = S_ref[...].reshape(n, DH, DH)
    Sb = S.astype(BF16)
    chains = lambda ref: ref[...].reshape(n, tl, ref.shape[-1])
    v_new = chains(u_ref) - jnp.einsum('ntd,nde->nte', chains(w_ref), Sb, preferred_element_type=F32)
    vnb = v_new.astype(BF16)
    o = (jnp.einsum('ntd,nde->nte', chains(qg_ref), Sb, preferred_element_type=F32)
         + jnp.einsum('nts,nse->nte', chains(qkm_ref), vnb, preferred_element_type=F32))
    eg = eg_ref[...].reshape(n, 1, 128)[:, :, 0:1]
    S_new = eg * S + jnp.einsum('ntd,nte->nde', chains(kd_ref), vnb, preferred_element_type=F32)
    S_ref[...] = S_new.reshape(NS, N_HEADS, DH, DH)
    if first:
        _fill_later_layers(S_out, c == NC - 1)

    on = (o * lax.rsqrt(jnp.mean(o * o, axis=-1, keepdims=True) + NORM_EPS) * gng_ref[...]).reshape(NS, N_HEADS, tl, DH)
    for h in range(N_HEADS):
        yc_ref[:, :, h * DH:(h + 1) * DH] = (on[:, h] * _silu(z_ref[:, :, h * DH:(h + 1) * DH])).astype(BF16)


def _gdn_scan(pre, proj, S0, gng, SB, G, tl, layer, S_prev):
    nb, lp, _ = proj.shape
    NS = SB * G
    NI, NC = nb // NS, lp // tl
    six = lambda a: a.reshape((NI * SB, NC) + a.shape[1:])
    cspec = lambda last: pl.BlockSpec((SB, 1, G, N_HEADS, tl, last), lambda i, c: (i, c, 0, 0, 0, 0))
    u, w, qg, kd, qkm, eg = (six(a) for a in pre)
    st_in, st_out, st_shape, extra_in, extra_args = _layered_state(layer, nb, NS, S_prev)
    return pl.pallas_call(
        functools.partial(_gdn_scan_body, NS=NS, tl=tl, NC=NC, first=layer == 0),
        grid=(NI, NC),
        in_specs=[cspec(DH), cspec(DH), cspec(DH), cspec(DH), cspec(tl),
                  pl.BlockSpec((SB, 1, G, N_HEADS, 1, 128), lambda i, c: (i, c, 0, 0, 0, 0)),
                  pl.BlockSpec((NS, tl, D_MIX), lambda i, c: (i, c, OFF_Z // D_MIX)),
                  st_in,
                  pl.BlockSpec((1, DH), lambda i, c: (0, 0))] + extra_in,
        out_specs=[pl.BlockSpec((NS, tl, D_MIX), lambda i, c: (i, c, 0)), st_out],
        out_shape=[jax.ShapeDtypeStruct((nb, lp, D_MIX), BF16), st_shape],
        input_output_aliases={9: 1} if extra_in else {},
        compiler_params=pltpu.CompilerParams(dimension_semantics=("arbitrary", "arbitrary"),
                                             vmem_limit_bytes=VMEM_LIMIT),
        name=f"gdn_scan_g{G}",
    )(u, w, qg, kd, qkm, eg, proj, S0, gng, *extra_args)


def _route(rl):
    lane = lax.broadcasted_iota(jnp.int32, rl.shape, 1).astype(F32)
    is_g = (lane >= N_EXPERTS) & (lane < N_EXPERTS + N_GROUPS)
    gl = jnp.where(is_g, rl, NEG)
    gmax = jnp.max(gl, axis=-1, keepdims=True)
    grp = jnp.min(jnp.where(gl == gmax, lane - N_EXPERTS, 4.0 * N_EXPERTS), axis=-1, keepdims=True)
    p_grp = 1.0 / jnp.sum(jnp.where(is_g, jnp.exp(gl - gmax), 0.0), axis=-1, keepdims=True)
    lo = grp * EXPERTS_PER_GROUP
    in_grp = (lane >= lo) & (lane < lo + EXPERTS_PER_GROUP)
    el = jnp.where(in_grp, rl, NEG)
    m1 = jnp.max(el, axis=-1, keepdims=True)
    i1 = jnp.min(jnp.where(el == m1, lane, 4.0 * N_EXPERTS), axis=-1, keepdims=True)
    el2 = jnp.where(lane == i1, NEG, el)
    m2 = jnp.max(el2, axis=-1, keepdims=True)
    i2 = jnp.min(jnp.where(el2 == m2, lane, 4.0 * N_EXPERTS), axis=-1, keepdims=True)
    e2 = jnp.exp(m2 - m1)
    w1 = p_grp / (1.0 + e2)
    w2 = p_grp * e2 / (1.0 + e2)
    return i1, i2, w1, w2


def _merge_body(ya_ref, yb_ref, yc_ref, ga_ref, gb_ref, gc_ref, x_ref, mod_ref, wa_ref, wb_ref, wc_ref, wo_ref,
                lg_ref, lb_ref, wrh_ref, wrl_ref, br_ref, x1_ref, rinfo_ref, cnt_ref, *, bt, lt):
    tm = bt * lt

    @pl.when((pl.program_id(0) == 0) & (pl.program_id(1) == 0))
    def _():
        cnt_ref[...] = jnp.zeros_like(cnt_ref)

    def r2(ref):
        return ref[...].reshape(tm, ref.shape[-1])

    merged = (_sigmoid(r2(ga_ref)) * _dot(r2(ya_ref), wa_ref[0])
              + _sigmoid(r2(gb_ref)) * _dot(r2(yb_ref), wb_ref[0])
              + _sigmoid(r2(gc_ref)) * _dot(r2(yc_ref), wc_ref[0]))
    out = _dot(merged.astype(BF16), wo_ref[0])
    y = DN_ALPHA * x_ref[...] + (1.0 + mod_ref[:, 2:3, :]) * out.reshape(bt, lt, D_MODEL)
    x1 = _layer_norm(y, lg_ref[0], lb_ref[0])
    x1_ref[...] = x1
    u2 = (x1 * (1.0 + mod_ref[:, 4:5, :]) + mod_ref[:, 3:4, :]).reshape(tm, D_MODEL)
    hi = u2.astype(BF16)
    lo = (u2 - hi.astype(F32)).astype(BF16)
    rl = _dot(hi, wrh_ref[0]) + _dot(lo, wrh_ref[0]) + _dot(hi, wrl_ref[0]) + br_ref[0]
    i1, i2, w1, w2 = _route(rl)
    lane = lax.broadcasted_iota(jnp.int32, (tm, 128), 1).astype(F32)
    onehot = jnp.where((lane == i1) | (lane == i2), 1.0, 0.0)
    rr = lax.broadcasted_iota(jnp.int32, (tm, tm), 0)
    cc = lax.broadcasted_iota(jnp.int32, (tm, tm), 1)
    before = _dot((rr > cc).astype(BF16), onehot.astype(BF16)) + cnt_ref[0:1, :]
    rank1 = jnp.sum(jnp.where(lane == i1, before, 0.0), axis=-1, keepdims=True)
    rank2 = jnp.sum(jnp.where(lane == i2, before, 0.0), axis=-1, keepdims=True)
    cnt_ref[0:1, :] += jnp.sum(onehot, axis=0, keepdims=True)
    rinfo = jnp.zeros((tm, 128), F32)
    for k, val in enumerate((i1, i2, w1, w2, rank1, rank2)):
        rinfo = jnp.where(lane == k, val, rinfo)
    rinfo_ref[...] = rinfo.reshape(bt, lt, 128)


def _merge(ya, yb, yc, proj, x, mod, wa, wb, wc, wo, lg, lb, wrh, wrl, br, layer, bt, lt):
    nb, lp, _ = x.shape
    tok = lambda i, t: (i, t, 0)
    wsp = lambda shape: pl.BlockSpec((1,) + shape, lambda i, t: (layer, 0, 0))
    g0 = OFF_GTS // D_MODEL
    return pl.pallas_call(
        functools.partial(_merge_body, bt=bt, lt=lt),
        grid=(nb // bt, lp // lt),
        in_specs=[pl.BlockSpec((bt, lt, D_MIX), tok),
                  pl.BlockSpec((bt, lt, D_MIX), tok),
                  pl.BlockSpec((bt, lt, D_MIX), tok),
                  pl.BlockSpec((bt, lt, D_MODEL), lambda i, t: (i, t, g0)),
                  pl.BlockSpec((bt, lt, D_MODEL), lambda i, t: (i, t, g0 + 1)),
                  pl.BlockSpec((bt, lt, D_MODEL), lambda i, t: (i, t, g0 + 2)),
                  pl.BlockSpec((bt, lt, D_MODEL), tok),
                  pl.BlockSpec((bt, 6, D_MODEL), lambda i, t: (i, 0, 0)),
                  wsp((D_MIX, D_MODEL)), wsp((D_MIX, D_MODEL)), wsp((D_MIX, D_MODEL)), wsp((D_MODEL, D_MODEL)),
                  wsp((1, D_MODEL)), wsp((1, D_MODEL)),
                  wsp((D_MODEL, 128)), wsp((D_MODEL, 128)), wsp((1, 128))],
        out_specs=[pl.BlockSpec((bt, lt, D_MODEL), tok),
                   pl.BlockSpec((bt, lt, 128), tok),
                   pl.BlockSpec((8, 128), lambda i, t: (0, 0))],
        out_shape=[jax.ShapeDtypeStruct((nb, lp, D_MODEL), F32),
                   jax.ShapeDtypeStruct((nb, lp, 128), F32),
                   jax.ShapeDtypeStruct((8, 128), F32)],
        compiler_params=pltpu.CompilerParams(dimension_semantics=("arbitrary", "arbitrary"),
                                             vmem_limit_bytes=VMEM_LIMIT),
        name=f"merge_l{layer}_b{bt}",
    )(ya, yb, yc, proj, proj, proj, x, mod, wa, wb, wc, wo, lg, lb, wrh, wrl, br)


def _moe_plan(rinfo, cnt, tm, mb):
    nb, lp, _ = rinfo.shape
    n_tok = nb * lp
    n_blocks = 2 * n_tok // mb + N_EXPERTS
    counts = cnt[0, :N_EXPERTS].astype(jnp.int32)
    nblk = (counts + mb - 1) // mb
    pend = jnp.cumsum(nblk)
    pstart = pend - nblk
    expert = rinfo[..., 0:2].astype(jnp.int32)
    rank = rinfo[..., 4:6].astype(jnp.int32)
    ids = jnp.arange(N_EXPERTS, dtype=jnp.int32)
    first_row = jnp.sum(jnp.where(expert[..., None] == ids, pstart * mb, 0), axis=-1)
    pos = (first_row + rank).reshape(n_tok // tm, 1, 2 * tm)
    block_e = jnp.sum(pend[None, :] <= jnp.arange(n_blocks, dtype=jnp.int32)[:, None], axis=1)
    block_e = jnp.minimum(block_e, N_EXPERTS - 1).astype(jnp.int32)
    n_used = pend[N_EXPERTS - 1:].astype(jnp.int32)
    ztail = jnp.where(nblk > 0, (pend - 1) * mb, -1).astype(jnp.int32)
    ztail = jnp.concatenate([ztail, n_used]).reshape(1, 1, N_EXPERTS + 1)
    return pos, block_e, n_used, ztail, n_blocks


def _row_copy(src, dst, sem):
    return pltpu.make_async_copy(src, dst, sem)


def _dispatch_body(pos_ref, ztail_ref, x1_ref, mod_ref, disp_ref, u_scr, z_scr, sem, zsem, *, bt, lt, n_blocks, mb):
    tm = bt * lt

    @pl.when((pl.program_id(0) == 0) & (pl.program_id(1) == 0))
    def _():
        z_scr[...] = jnp.zeros_like(z_scr)
        for e in range(N_EXPERTS):
            @pl.when(ztail_ref[0, 0, e] >= 0)
            def _():
                row = pl.multiple_of(ztail_ref[0, 0, e], mb)
                _row_copy(z_scr, disp_ref.at[pl.ds(row, mb)], zsem).start()
        def unused(j):
            return _row_copy(z_scr, disp_ref.at[pl.ds(pl.multiple_of(j * mb, mb), mb)], zsem)

        n_used = ztail_ref[0, 0, N_EXPERTS]
        lax.fori_loop(n_used, n_blocks, lambda j, c: (unused(j).start(), c)[1], 0)
        lax.fori_loop(n_used, n_blocks, lambda j, c: (unused(j).wait(), c)[1], 0)
        for e in range(N_EXPERTS):
            @pl.when(ztail_ref[0, 0, e] >= 0)
            def _():
                row = pl.multiple_of(ztail_ref[0, 0, e], mb)
                _row_copy(z_scr, disp_ref.at[pl.ds(row, mb)], zsem).wait()

    u2 = x1_ref[...] * (1.0 + mod_ref[:, 4:5, :]) + mod_ref[:, 3:4, :]
    u_scr[...] = u2.reshape(tm, D_MODEL)

    def issue(t, carry):
        for k in range(2):
            _row_copy(u_scr.at[pl.ds(t, 1)], disp_ref.at[pl.ds(pos_ref[0, 0, 2 * t + k], 1)], sem).start()
        return carry

    lax.fori_loop(0, tm, issue, 0, unroll=8)
    for k in range(2):
        _row_copy(u_scr, disp_ref.at[pl.ds(0, tm)], sem).wait()


def _dispatch(x1, mod, pos, ztail, n_blocks, mb, bt, lt):
    nb, lp, _ = x1.shape
    nt = lp // lt
    return pl.pallas_call(
        functools.partial(_dispatch_body, bt=bt, lt=lt, n_blocks=n_blocks, mb=mb),
        grid=(nb // bt, nt),
        in_specs=[pl.BlockSpec((1, 1, 2 * bt * lt), lambda i, t: (i * nt + t, 0, 0), memory_space=pltpu.SMEM),
                  pl.BlockSpec((1, 1, N_EXPERTS + 1), lambda i, t: (0, 0, 0), memory_space=pltpu.SMEM),
                  pl.BlockSpec((bt, lt, D_MODEL), lambda i, t: (i, t, 0)),
                  pl.BlockSpec((bt, 6, D_MODEL), lambda i, t: (i, 0, 0))],
        out_specs=pl.BlockSpec(memory_space=pl.ANY),
        out_shape=jax.ShapeDtypeStruct((n_blocks * mb, D_MODEL), F32),
        scratch_shapes=[pltpu.VMEM((bt * lt, D_MODEL), F32), pltpu.VMEM((mb, D_MODEL), F32),
                        pltpu.SemaphoreType.DMA, pltpu.SemaphoreType.DMA],
        compiler_params=pltpu.CompilerParams(dimension_semantics=("arbitrary", "arbitrary"),
                                             vmem_limit_bytes=VMEM_LIMIT),
        name=f"dispatch_b{bt}",
    )(pos, ztail, x1, mod)


def _experts_body(be_ref, nu_ref, x_ref, wg_ref, wu_ref, wd_ref, o_ref, wg_s, wu_s, wd_s):
    j = pl.program_id(0)

    @pl.when((j == 0) | (be_ref[j] != be_ref[jnp.maximum(j - 1, 0)]))
    def _():
        wg_s[...] = wg_ref[0, 0].astype(BF16)
        wu_s[...] = wu_ref[0, 0].astype(BF16)
        wd_s[...] = wd_ref[0, 0].astype(BF16)

    @pl.when(j < nu_ref[0])
    def _():
        x = x_ref[...].astype(BF16)
        hb = _silu(_dot(x, wg_s[...])) * _dot(x, wu_s[...])
        o_ref[...] = _dot(hb.astype(BF16), wd_s[...])

    @pl.when(j >= nu_ref[0])
    def _():
        o_ref[...] = jnp.zeros_like(o_ref)


def _experts(disp, block_e, n_used, wg, wu, wd, layer, mb):
    n_blocks = disp.shape[0] // mb
    wmap = lambda j, be, nu: (layer, be[j], 0, 0)
    return pl.pallas_call(
        _experts_body,
        grid_spec=pltpu.PrefetchScalarGridSpec(
            num_scalar_prefetch=2,
            grid=(n_blocks,),
            in_specs=[pl.BlockSpec((mb, D_MODEL), lambda j, be, nu: (jnp.minimum(j, nu[0] - 1), 0)),
                      pl.BlockSpec((1, 1, D_MODEL, D_EXPERT), wmap),
                      pl.BlockSpec((1, 1, D_MODEL, D_EXPERT), wmap),
                      pl.BlockSpec((1, 1, D_EXPERT, D_MODEL), wmap)],
            out_specs=pl.BlockSpec((mb, D_MODEL), lambda j, be, nu: (j, 0)),
            scratch_shapes=[pltpu.VMEM((D_MODEL, D_EXPERT), BF16), pltpu.VMEM((D_MODEL, D_EXPERT), BF16),
                            pltpu.VMEM((D_EXPERT, D_MODEL), BF16)]),
        out_shape=jax.ShapeDtypeStruct(disp.shape, F32),
        compiler_params=pltpu.CompilerParams(dimension_semantics=("arbitrary",), vmem_limit_bytes=VMEM_LIMIT),
        name=f"experts_l{layer}_n{n_blocks}",
    )(block_e, n_used, disp, wg, wu, wd)


def _combine_body(pos_ref, x1_ref, mod_ref, rinfo_ref, eo_ref, lg_ref, lb_ref, x2_ref, r_scr, sem, *, bt, lt):
    tm = bt * lt

    def issue(t, carry):
        for k in range(2):
            _row_copy(eo_ref.at[pl.ds(pos_ref[0, 0, 2 * t + k], 1)], r_scr.at[k, pl.ds(t, 1)], sem).start()
        return carry

    lax.fori_loop(0, tm, issue, 0, unroll=8)
    for k in range(2):
        _row_copy(eo_ref.at[pl.ds(0, tm)], r_scr.at[k], sem).wait()

    rinfo = rinfo_ref[...].reshape(tm, 128)
    moe = r_scr[0] * rinfo[:, 2:3] + r_scr[1] * rinfo[:, 3:4]
    y = DN_ALPHA * x1_ref[...] + (1.0 + mod_ref[:, 5:6, :]) * moe.reshape(bt, lt, D_MODEL)
    x2_ref[...] = _layer_norm(y, lg_ref[0], lb_ref[0])


def _combine(x1, mod, rinfo, eo, pos, lg, lb, layer, bt, lt):
    nb, lp, _ = x1.shape
    nt = lp // lt
    tok = lambda i, t: (i, t, 0)
    return pl.pallas_call(
        functools.partial(_combine_body, bt=bt, lt=lt),
        grid=(nb // bt, nt),
        in_specs=[pl.BlockSpec((1, 1, 2 * bt * lt), lambda i, t: (i * nt + t, 0, 0), memory_space=pltpu.SMEM),
                  pl.BlockSpec((bt, lt, D_MODEL), tok),
                  pl.BlockSpec((bt, 6, D_MODEL), lambda i, t: (i, 0, 0)),
                  pl.BlockSpec((bt, lt, 128), tok),
                  pl.BlockSpec(memory_space=pl.ANY),
                  pl.BlockSpec((1, 1, D_MODEL), lambda i, t: (layer, 0, 0)),
                  pl.BlockSpec((1, 1, D_MODEL), lambda i, t: (layer, 0, 0))],
        out_specs=pl.BlockSpec((bt, lt, D_MODEL), tok),
        out_shape=jax.ShapeDtypeStruct((nb, lp, D_MODEL), F32),
        scratch_shapes=[pltpu.VMEM((2, bt * lt, D_MODEL), F32), pltpu.SemaphoreType.DMA],
        compiler_params=pltpu.CompilerParams(dimension_semantics=("arbitrary", "arbitrary"),
                                             vmem_limit_bytes=VMEM_LIMIT),
        name=f"combine_l{layer}_b{bt}",
    )(pos, x1, mod, rinfo, eo, lg, lb)


def _hist(state):
    return jnp.pad(state, ((0, 0), (HIST - state.shape[1], 0), (0, 0)))


def _trunk(x, mod, st, p, bt, lt, lv, mix, moe_block, ip_tile):
    nb = x.shape[0]
    new = {key: [] for key in ('n', 'm', 'conv', 'gconv')}
    C = S = None
    for l in range(DEPTH):
        proj, x = _inproj(x, mod[l], p['ln_in_g'], p['ln_in_b'], p['w_in_r'], p['b_in_r'], l, l == 0, *ip_tile)
        m0 = jnp.broadcast_to(st['m'][l][:, :, None, None], (nb, N_HEADS, 1, 128))
        yb, conv, *pre = _mlstm_prep(proj, _hist(st['conv'][l]), p['conv_b_w8'][l], mix['NB'], mix['G'], mix['tl'], lv)
        ya, C, n, m = _mlstm_scan(pre, proj, st['C'], st['n'][l], m0, p['mlstm_norm_g'][l:l + 1],
                                  mix['SB'], mix['G'], mix['tl'], l, C)
        *pre, gconv = _gdn_prep(proj, _hist(st['gconv'][l]), p['conv_c_w8'][l], p['alog_row'][l], p['dtb_row'][l],
                                mix['NB'], mix['G'], mix['tl'], lv)
        yc, S = _gdn_scan(pre, proj, st['S'], p['gdn_norm_g'][l:l + 1], mix['SB'], mix['G'], mix['tl'], l, S)
        x1, rinfo, cnt = _merge(ya, yb, yc, proj, x, mod[l], p['w_br_a'], p['w_br_b'], p['w_br_c'], p['w_out'],
                                p['ln1_g'], p['ln1_b'], p['wr_hi'], p['wr_lo'], p['br'], l, bt, lt)
        pos, block_e, n_used, ztail, n_blocks = _moe_plan(rinfo, cnt, bt * lt, moe_block)
        disp = _dispatch(x1, mod[l], pos, ztail, n_blocks, moe_block, bt, lt)
        eo = _experts(disp, block_e, n_used, p['exp_w_gate'], p['exp_w_up'], p['exp_w_down'], l, moe_block)
        x = _combine(x1, mod[l], rinfo, eo, pos, p['ln2_g'], p['ln2_b'], l, bt, lt)
        new['n'].append(n)
        new['m'].append(m[:, :, 0, 0])
        new['conv'].append(conv)
        new['gconv'].append(gconv)
    return x, dict({key: jnp.stack(val) for key, val in new.items()}, C=C, S=S)


def kernel(x_prompt, x_sample, state_mlstm_C, state_mlstm_n, state_mlstm_m, state_conv, state_gdn_S, state_gdn_conv, c_prompt, c_sample, ln_in_g, ln_in_b, w_ada, b_ada, w_in, b_in, mlstm_norm_g, conv_b_w, conv_c_w, gdn_a_log, gdn_dt_bias, gdn_norm_g, w_br_a, w_br_b, w_br_c, w_out, ln1_g, ln1_b, router_g_w, router_g_b, router_e_w, router_e_b, exp_w_gate, exp_w_up, exp_w_down, ln2_g, ln2_b):
    nbp, lp, _ = x_prompt.shape
    nbs, ls, _ = x_sample.shape
    lsp = 8

    def regroup(a):
        out = jnp.zeros(a.shape[:-1] + (N_PROJ,), a.dtype)
        for src, end, dst in PROJ_SEGMENTS:
            out = out.at[..., dst:dst + end - src].set(a[..., src:end])
        return out

    wr = jnp.concatenate([router_e_w, router_g_w, jnp.zeros((DEPTH, D_MODEL, 128 - N_EXPERTS - N_GROUPS), F32)], axis=-1)
    wr_hi = wr.astype(BF16)
    lane_pad = lambda a: jnp.pad(a, ((0, 0), (4, 128 - 4 - N_HEADS)))[:, None, :]
    p = dict(
        ln_in_g=ln_in_g.reshape(1, D_MODEL), ln_in_b=ln_in_b.reshape(1, D_MODEL),
        w_in_r=_regroup_w_in(w_in), b_in_r=regroup(b_in).reshape(DEPTH, 1, N_PROJ),
        mlstm_norm_g=mlstm_norm_g,
        conv_b_w8=jnp.pad(conv_b_w, ((0, 0), (0, 8 - CONV_B), (0, 0))),
        conv_c_w8=jnp.pad(conv_c_w, ((0, 0), (0, 8 - CONV_C), (0, 0))),
        alog_row=lane_pad(gdn_a_log), dtb_row=lane_pad(gdn_dt_bias), gdn_norm_g=gdn_norm_g,
        w_br_a=w_br_a.astype(BF16), w_br_b=w_br_b.astype(BF16), w_br_c=w_br_c.astype(BF16),
        w_out=w_out.astype(BF16),
        ln1_g=ln1_g.reshape(DEPTH, 1, D_MODEL), ln1_b=ln1_b.reshape(DEPTH, 1, D_MODEL),
        wr_hi=wr_hi, wr_lo=(wr - wr_hi.astype(F32)).astype(BF16),
        br=jnp.concatenate([router_e_b, router_g_b, jnp.zeros((DEPTH, 128 - N_EXPERTS - N_GROUPS), F32)],
                           axis=-1).reshape(DEPTH, 1, 128),
        exp_w_gate=exp_w_gate, exp_w_up=exp_w_up, exp_w_down=exp_w_down,
        ln2_g=ln2_g.reshape(DEPTH, 1, D_MODEL), ln2_b=ln2_b.reshape(DEPTH, 1, D_MODEL),
    )

    mod = _ada(jnp.concatenate([c_prompt, c_sample], axis=0), w_ada, b_ada)
    mod = mod.reshape(DEPTH, nbp + nbs, 6, D_MODEL)

    zeros = lambda *s: jnp.zeros((DEPTH, nbp) + s, F32)
    st_p = {'C': zeros(N_HEADS, DH, DH), 'n': zeros(N_HEADS, DH), 'm': zeros(N_HEADS),
            'conv': zeros(CONV_B - 1, D_MIX), 'S': zeros(N_HEADS, DH, DH), 'gconv': zeros(CONV_C - 1, 3 * D_MIX)}
    y_p, sp = _trunk(x_prompt, mod[:, :nbp], st_p, p, bt=1, lt=512, lv=lp,
                     mix=dict(NB=8, G=1, tl=RB, SB=nbp), moe_block=256, ip_tile=(1, 256))

    st_s = {'C': state_mlstm_C, 'n': state_mlstm_n, 'm': state_mlstm_m, 'conv': state_conv,
            'S': state_gdn_S, 'gconv': state_gdn_conv}
    xs = jnp.pad(x_sample, ((0, 0), (0, lsp - ls), (0, 0)))
    y_s, ss = _trunk(xs, mod[:, nbp:], st_s, p, bt=64, lt=lsp, lv=ls,
                     mix=dict(NB=2, G=RB // lsp, tl=lsp, SB=1), moe_block=128, ip_tile=(32, lsp))
    y_s = y_s[:, :ls]

    return (y_p, y_s, sp['C'], sp['n'], sp['m'], sp['conv'], sp['S'], sp['gconv'],
            ss['C'], ss['n'], ss['m'], ss['conv'], ss['S'], ss['gconv'])
```

```python
import functools

import jax
import jax.numpy as jnp
from jax import lax
from jax.experimental import pallas as pl
from jax.experimental.pallas import tpu as pltpu

F32 = jnp.float32
BF16 = jnp.bfloat16

D_MODEL = 1024
DEPTH = 2
N_HEADS = 4
DH = 128
D_MIX = N_HEADS * DH
N_EXPERTS = 32
EXPERTS_PER_GROUP = 8
N_GROUPS = 4
D_EXPERT = 256
CONV_B = 3
CONV_C = 4
HIST = 8
RB = 64
INV_BLOCK = 8
DN_ALPHA = (2 * DEPTH) ** 0.25
LN_EPS = 1e-5
NORM_EPS = 1e-6
NEG = -1e30

OFF_QKVC = 0
OFF_BCH = 1536
OFF_GTS = 3072
OFF_QKVO = 6144
OFF_Z = 8192
OFF_SA = 8704
OFF_SB = 8832
N_PROJ = 8960
TN_PROJ = 1280
PROJ_SEGMENTS = ((3592, 5128, OFF_QKVC), (2056, 3592, OFF_BCH), (5648, 8720, OFF_GTS), (0, 2048, OFF_QKVO),
                 (5128, 5640, OFF_Z), (2048, 2052, OFF_SA), (5640, 5644, OFF_SA + 4), (2052, 2056, OFF_SB),
                 (5644, 5648, OFF_SB + 4))

VMEM_LIMIT = 52 * 1024 * 1024


def _dot(a, b):
    return jnp.dot(a, b, preferred_element_type=F32)


def _split3(x):
    hi = x.astype(BF16)
    r = x - hi.astype(F32)
    mid = r.astype(BF16)
    lo = (r - mid.astype(F32)).astype(BF16)
    return hi, mid, lo


def _layer_norm(x, g, b):
    mu = jnp.mean(x, axis=-1, keepdims=True)
    xc = x - mu
    var = jnp.mean(xc * xc, axis=-1, keepdims=True)
    return xc * lax.rsqrt(var + LN_EPS) * g + b


def _sigmoid(x):
    return jax.nn.sigmoid(x)


def _silu(x):
    return x * _sigmoid(x)


def _log_sigmoid(x):
    return jnp.minimum(x, 0.0) - jnp.log1p(jnp.exp(-jnp.abs(x)))


def _softplus(x):
    return jnp.maximum(x, 0.0) + jnp.log1p(jnp.exp(-jnp.abs(x)))


def _ada_body(c_ref, w_ref, b_ref, o_ref):
    c = c_ref[...]
    s = _silu(c).astype(BF16)
    o_ref[0] = _dot(s, w_ref[0].astype(BF16)) + b_ref[0]


def _ada(c_all, w_ada, b_ada):
    nb = c_all.shape[0]
    return pl.pallas_call(
        _ada_body,
        grid=(DEPTH, 6),
        in_specs=[pl.BlockSpec((nb, D_MODEL), lambda l, j: (0, 0)),
                  pl.BlockSpec((1, D_MODEL, D_MODEL), lambda l, j: (l, 0, j)),
                  pl.BlockSpec((1, 1, D_MODEL), lambda l, j: (l, 0, j))],
        out_specs=pl.BlockSpec((1, nb, D_MODEL), lambda l, j: (l, 0, j)),
        out_shape=jax.ShapeDtypeStruct((DEPTH, nb, 6 * D_MODEL), F32),
        compiler_params=pltpu.CompilerParams(dimension_semantics=("arbitrary", "arbitrary"),
                                             vmem_limit_bytes=VMEM_LIMIT),
        name="ada",
    )(c_all, w_ada, b_ada.reshape(DEPTH, 1, 6 * D_MODEL))


def _regroup_body(w_ref, o_ref):
    o_ref[...] = jnp.zeros_like(o_ref)
    for src, end, dst in PROJ_SEGMENTS:
        o_ref[0, :, dst:dst + end - src] = w_ref[0, :, src:end].astype(BF16)


def _regroup_w_in(w_in):
    n_in = w_in.shape[-1]
    rows = 256
    return pl.pallas_call(
        _regroup_body,
        grid=(DEPTH, D_MODEL // rows),
        in_specs=[pl.BlockSpec((1, rows, n_in), lambda l, i: (l, i, 0))],
        out_specs=pl.BlockSpec((1, rows, N_PROJ), lambda l, i: (l, i, 0)),
        out_shape=jax.ShapeDtypeStruct((DEPTH, D_MODEL, N_PROJ), BF16),
        compiler_params=pltpu.CompilerParams(dimension_semantics=("arbitrary", "arbitrary"),
                                             vmem_limit_bytes=VMEM_LIMIT),
        name="regroup_w_in",
    )(w_in)


def _inproj_body(x_ref, mod_ref, g_ref, b_ref, w_ref, bias_ref, proj_ref, *rest, apply_ln, bt, lt):
    x = x_ref[...]
    if apply_ln:
        x = _layer_norm(x, g_ref[...], b_ref[...])
        rest[0][...] = x
    u = (x * (1.0 + mod_ref[:, 1:2, :]) + mod_ref[:, 0:1, :]).reshape(bt * lt, D_MODEL).astype(BF16)
    for j in range(N_PROJ // TN_PROJ):
        cs = slice(j * TN_PROJ, (j + 1) * TN_PROJ)
        proj_ref[:, :, cs] = (_dot(u, w_ref[0, :, cs]) + bias_ref[0, :, cs]).reshape(bt, lt, TN_PROJ)


def _inproj(x, mod, ln_g, ln_b, w_r, b_r, layer, apply_ln, bt, lt):
    nb, lp, _ = x.shape
    tok = lambda i, t: (i, t, 0)
    out_shape = [jax.ShapeDtypeStruct((nb, lp, N_PROJ), F32)]
    out_specs = [pl.BlockSpec((bt, lt, N_PROJ), tok)]
    if apply_ln:
        out_shape.append(jax.ShapeDtypeStruct((nb, lp, D_MODEL), F32))
        out_specs.append(pl.BlockSpec((bt, lt, D_MODEL), tok))
    res = pl.pallas_call(
        functools.partial(_inproj_body, apply_ln=apply_ln, bt=bt, lt=lt),
        grid=(nb // bt, lp // lt),
        in_specs=[pl.BlockSpec((bt, lt, D_MODEL), tok),
                  pl.BlockSpec((bt, 6, D_MODEL), lambda i, t: (i, 0, 0)),
                  pl.BlockSpec((1, D_MODEL), lambda i, t: (0, 0)),
                  pl.BlockSpec((1, D_MODEL), lambda i, t: (0, 0)),
                  pl.BlockSpec((1, D_MODEL, N_PROJ), lambda i, t: (layer, 0, 0), pipeline_mode=pl.Buffered(1)),
                  pl.BlockSpec((1, 1, N_PROJ), lambda i, t: (layer, 0, 0))],
        out_specs=out_specs,
        out_shape=out_shape,
        compiler_params=pltpu.CompilerParams(dimension_semantics=("arbitrary", "arbitrary"),
                                             vmem_limit_bytes=VMEM_LIMIT),
        name=f"inproj_l{layer}_b{bt}",
    )(x, mod, ln_g, ln_b, w_r, b_r)
    return (res[0], res[1]) if apply_ln else (res[0], x)


def _conv_taps(xp_s, w_ref, width, tl):
    acc = None
    for j in range(width):
        tap = xp_s[:, pl.ds(HIST - (width - 1) + j, tl), :] * w_ref[j:j + 1, :].reshape(1, 1, -1)
        acc = tap if acc is None else acc + tap
    return acc


def _conv_history(xp_s, hist_ref, prev_ref, n_steps):
    if n_steps > 1:
        @pl.when(pl.program_id(1) == 0)
        def _():
            xp_s[:, 0:HIST, :] = hist_ref[...]

        @pl.when(pl.program_id(1) > 0)
        def _():
            xp_s[:, 0:HIST, :] = prev_ref[...]
    else:
        xp_s[:, 0:HIST, :] = hist_ref[...]


def _heads(x, nb, width):
    return jnp.stack([x[:, :, h * width:(h + 1) * width] for h in range(N_HEADS)],
                     axis=1).reshape(nb * N_HEADS, RB, width)


def _gate_cols(x, nb, lane0):
    return jnp.stack([x[:, :, lane0 + h:lane0 + h + 1] for h in range(N_HEADS)],
                     axis=1).reshape(nb * N_HEADS, RB, 1)


def _gate_rows(x, nb, lane0):
    xt = jnp.swapaxes(x, 1, 2)
    return jnp.stack([xt[:, lane0 + h:lane0 + h + 1, :] for h in range(N_HEADS)],
                     axis=1).reshape(nb * N_HEADS, 1, RB)


def _bmm(a, b):
    return jnp.einsum('nts,nsu->ntu', a.astype(BF16), b.astype(BF16), preferred_element_type=F32)


def _bmm_nt(a, b):
    return jnp.einsum('ntd,nsd->nts', a, b, preferred_element_type=F32)


def _block_masks(tl):
    rr = lax.broadcasted_iota(jnp.int32, (RB, RB), 0)
    cc = lax.broadcasted_iota(jnp.int32, (RB, RB), 1)
    incl = rr >= cc
    if tl < RB:
        incl = incl & ((rr // tl) == (cc // tl))
    return rr, cc, incl


def _seq_cumsum(x, incl, nb):
    tril = jnp.broadcast_to(incl.astype(BF16)[None], (nb, RB, RB))
    hi, mid, lo = _split3(x)
    return _bmm(tril, hi) + _bmm(tril, mid) + _bmm(tril, lo)


def _seq_last(x, nb, G, tl):
    return x.reshape(nb * G, tl, 128)[:, tl - 1:tl, :]


def _seq_rows(x3, nb, tl):
    return jnp.broadcast_to(x3, (x3.shape[0], tl, 128)).reshape(nb, RB, 128)


def _put_chains(ref, val, nb, G, tl):
    val4 = val.reshape(nb, N_HEADS, RB, val.shape[-1])
    for h in range(N_HEADS):
        ref[:, :, h] = val4[:, h].reshape(nb, G, tl, val.shape[-1])


def _put_seq_scalars(ref, x3, nb, G, lane0):
    x4 = x3.reshape(nb, G, 1, 128)
    for h in range(N_HEADS):
        ref[:, :, h] = jnp.broadcast_to(x4[:, :, :, lane0 + h:lane0 + h + 1], (nb, G, 1, 128))


def _step_tiling(nb, lp, NB, G, tl):
    assert G * tl == RB
    if G == 1:
        gx, tlx = 1, NB * RB
    else:
        assert lp == tl
        gx, tlx = NB * G, tl
    return gx, tlx, nb // gx, lp // tlx


def _layered_state(layer, nb, NS, prev):
    st_in = pl.BlockSpec((None, NS, N_HEADS, DH, DH), lambda i, c: (layer, i, 0, 0, 0))
    shape = jax.ShapeDtypeStruct((DEPTH, nb, N_HEADS, DH, DH), F32)
    if layer == 0:
        return st_in, pl.BlockSpec((DEPTH, NS, N_HEADS, DH, DH), lambda i, c: (0, i, 0, 0, 0)), shape, [], []
    return st_in, st_in, shape, [pl.BlockSpec(memory_space=pl.ANY)], [prev]


def _fill_later_layers(ref, when):
    @pl.when(when)
    def _():
        for l in range(1, DEPTH):
            ref[l] = ref[0]


def _mlstm_prep_body(qkvo_ref, bch_ref, prev_ref, sa_ref, sb_ref, cv0_ref, cw_ref,
                     yb_ref, cv_ref, nv_ref, q_ref, kw_ref, v_ref, rows_ref, bl_ref, bc_ref, kn_ref, xp_s,
                     *, NB, G, tl, tlx, NCS, lv):
    cs = pl.program_id(1)
    n = NB * N_HEADS
    lvl = lv - (NCS - 1) * tlx

    if NCS > 1:
        @pl.when(cs == 0)
        def _():
            xp_s[:, 0:HIST, :] = cv0_ref[...]

        @pl.when(cs > 0)
        def _():
            xp_s[:, 0:HIST, :] = prev_ref[:, :, D_MIX:2 * D_MIX] * prev_ref[:, :, 2 * D_MIX:3 * D_MIX]
    else:
        xp_s[:, 0:HIST, :] = cv0_ref[...]
    xp_s[:, HIST:HIST + tlx, :] = bch_ref[:, :, D_MIX:2 * D_MIX] * bch_ref[:, :, 2 * D_MIX:3 * D_MIX]
    yb_ref[...] = (bch_ref[:, :, 0:D_MIX] * _conv_taps(xp_s, cw_ref, CONV_B, tlx)).astype(BF16)

    @pl.when(cs == NCS - 1)
    def _():
        cv_ref[...] = xp_s[:, pl.ds(HIST + lvl - (CONV_B - 1), CONV_B - 1), :]

    i_all = sa_ref[...].reshape(NB, RB, 128)
    f_all = _log_sigmoid(sb_ref[...].reshape(NB, RB, 128))
    if lv < NCS * tlx:
        assert NCS == 1
        valid = (lax.broadcasted_iota(jnp.int32, (NB, RB, 128), 1) % tl) < lv
        i_all = jnp.where(valid, i_all, NEG)
        f_all = jnp.where(valid, f_all, 0.0)
    _, _, incl = _block_masks(tl)
    bcum = _seq_cumsum(f_all, incl, NB)
    blast = _seq_last(bcum, NB, G, tl)
    val = _seq_rows(blast, NB, tl) - bcum + i_all
    bmax = jnp.max(val.reshape(NB * G, tl, 128), axis=1, keepdims=True)
    wk0 = jnp.exp(val - _seq_rows(bmax, NB, tl))

    qkvo = qkvo_ref[...].reshape(NB, RB, 4 * D_MIX)
    q = _heads(qkvo[:, :, 0:D_MIX], NB, DH)
    k = _heads(qkvo[:, :, D_MIX:2 * D_MIX], NB, DH) * (DH ** -0.5)
    v = _heads(qkvo[:, :, 2 * D_MIX:3 * D_MIX], NB, DH)
    qb, kb, vb = q.astype(BF16), k.astype(BF16), v.astype(BF16)

    b_col = _gate_cols(bcum, NB, 0)
    dlog = jnp.where(incl[None], b_col - _gate_rows(bcum, NB, 0) + _gate_rows(i_all, NB, 0), NEG)
    d = jnp.max(dlog, axis=-1, keepdims=True)
    s0 = _bmm_nt(qb, kb) * jnp.exp(dlog - d)
    kw0 = k * _gate_cols(wk0, NB, 0)

    _put_chains(nv_ref, _bmm(s0, vb), NB, G, tl)
    _put_chains(q_ref, qb, NB, G, tl)
    _put_chains(kw_ref, kw0.astype(BF16), NB, G, tl)
    _put_chains(v_ref, vb, NB, G, tl)
    rows = jnp.swapaxes(jnp.concatenate([d, b_col, jnp.sum(s0, axis=-1, keepdims=True),
                                         jnp.zeros((n, RB, 5), F32)], axis=-1), 1, 2).reshape(NB, N_HEADS, 8, RB)
    for h in range(N_HEADS):
        for g in range(G):
            rows_ref[:, g, h] = rows[:, h, :, g * tl:(g + 1) * tl]
    _put_seq_scalars(bl_ref, blast, NB, G, 0)
    _put_seq_scalars(bc_ref, bmax, NB, G, 0)
    kn = jnp.sum(kw0.reshape(n * G, tl, DH), axis=1, keepdims=True).reshape(NB, N_HEADS, G, 1, DH)
    for h in range(N_HEADS):
        kn_ref[:, :, h] = kn[:, h]


def _mlstm_prep(proj, cv0, cw, NB, G, tl, lv):
    nb, lp, _ = proj.shape
    gx, tlx, NI, NCS = _step_tiling(nb, lp, NB, G, tl)
    nbt = nb * lp // RB
    step = lambda i, c: (i * NCS + c, 0, 0, 0, 0)
    chain = lambda last, dt: jax.ShapeDtypeStruct((nbt, G, N_HEADS, tl, last), dt)
    cspec = lambda last: pl.BlockSpec((NB, G, N_HEADS, tl, last), step)
    scal = jax.ShapeDtypeStruct((nbt, G, N_HEADS, 1, 128), F32)
    sspec = pl.BlockSpec((NB, G, N_HEADS, 1, 128), step)
    bch = OFF_BCH // (3 * D_MIX)
    return pl.pallas_call(
        functools.partial(_mlstm_prep_body, NB=NB, G=G, tl=tl, tlx=tlx, NCS=NCS, lv=lv),
        grid=(NI, NCS),
        in_specs=[pl.BlockSpec((gx, tlx, 4 * D_MIX), lambda i, c: (i, c, OFF_QKVO // (4 * D_MIX))),
                  pl.BlockSpec((gx, tlx, 3 * D_MIX), lambda i, c: (i, c, bch)),
                  pl.BlockSpec((gx, HIST, 3 * D_MIX), lambda i, c: (i, jnp.maximum(c * (tlx // HIST) - 1, 0), bch)),
                  pl.BlockSpec((gx, tlx, 128), lambda i, c: (i, c, OFF_SA // 128)),
                  pl.BlockSpec((gx, tlx, 128), lambda i, c: (i, c, OFF_SB // 128)),
                  pl.BlockSpec((gx, HIST, D_MIX), lambda i, c: (i, 0, 0)),
                  pl.BlockSpec((8, D_MIX), lambda i, c: (0, 0))],
        out_specs=[pl.BlockSpec((gx, tlx, D_MIX), lambda i, c: (i, c, 0)),
                   pl.BlockSpec((gx, CONV_B - 1, D_MIX), lambda i, c: (i, 0, 0)),
                   cspec(DH), cspec(DH), cspec(DH), cspec(DH),
                   pl.BlockSpec((NB, G, N_HEADS, 8, tl), step), sspec, sspec, sspec],
        out_shape=[jax.ShapeDtypeStruct((nb, lp, D_MIX), BF16),
                   jax.ShapeDtypeStruct((nb, CONV_B - 1, D_MIX), F32),
                   chain(DH, F32), chain(DH, BF16), chain(DH, BF16), chain(DH, BF16),
                   jax.ShapeDtypeStruct((nbt, G, N_HEADS, 8, tl), F32),
                   scal, scal, scal],
        scratch_shapes=[pltpu.VMEM((gx, HIST + tlx, D_MIX), F32)],
        compiler_params=pltpu.CompilerParams(dimension_semantics=("arbitrary", "arbitrary"),
                                             vmem_limit_bytes=VMEM_LIMIT),
        name=f"mlstm_prep_g{G}",
    )(proj, proj, proj, proj, proj, cv0, cw)


def _mlstm_scan_body(nv_ref, q_ref, kw_ref, v_ref, rows_ref, bl_ref, bc_ref, kn_ref, o_ref, C0_ref, n0_ref, m0_ref,
                     ng_ref, *rest, NS, tl, NC, first):
    ya_ref, C_out, n_ref, m_ref = rest[-4:]
    C_ref = C_out.at[0] if first else C_out
    c = pl.program_id(1)
    n = NS * N_HEADS

    @pl.when(c == 0)
    def _():
        C_ref[...] = C0_ref[...]
        n_ref[...] = n0_ref[...]
        m_ref[...] = m0_ref[...]

    chains = lambda ref: ref[...].reshape(n, ref.shape[-2], ref.shape[-1])
    C = C_ref[...].reshape(n, DH, DH)
    nvec = n_ref[...].reshape(n, 1, DH)
    m_prev = chains(m_ref)[:, :, 0:1]
    rows = chains(rows_ref)
    d, b, ds0 = rows[:, 0:1, :], rows[:, 1:2, :], rows[:, 2:3, :]
    qb = chains(q_ref)

    m_inter = b + m_prev
    m_t = jnp.maximum(m_inter, d)
    f = jnp.exp(d - m_t)
    inter = jnp.exp(m_inter - m_t)
    qn = jnp.einsum('nod,ntd->not', nvec.astype(BF16), qb, preferred_element_type=F32)
    den = f * ds0 + inter * qn
    scale_t = 1.0 / jnp.maximum(jnp.abs(den), jnp.exp(-m_t))
    fi = jnp.swapaxes(jnp.concatenate([f * scale_t, inter * scale_t, jnp.zeros((n, 6, tl), F32)], axis=1), 1, 2)
    qC = jnp.einsum('ntd,nde->nte', qb, C.astype(BF16), preferred_element_type=F32)
    hh = fi[:, :, 0:1] * chains(nv_ref) + fi[:, :, 1:2] * qC

    m_new = m_t[:, :, tl - 1:tl]
    decay = jnp.exp(chains(bl_ref)[:, :, 0:1] + m_prev - m_new)
    scale = jnp.exp(chains(bc_ref)[:, :, 0:1] - m_new)
    kv = jnp.einsum('ntd,nte->nde', chains(kw_ref), chains(v_ref), preferred_element_type=F32)
    C_ref[...] = (decay * C + scale * kv).reshape(NS, N_HEADS, DH, DH)
    n_ref[...] = (decay * nvec + scale * chains(kn_ref)).reshape(NS, N_HEADS, DH)
    m_ref[...] = jnp.broadcast_to(m_new, (n, 1, 128)).reshape(NS, N_HEADS, 1, 128)
    if first:
        _fill_later_layers(C_out, c == NC - 1)

    mu = jnp.mean(hh, axis=-1, keepdims=True)
    hc = hh - mu
    hn = (hc * lax.rsqrt(jnp.mean(hc * hc, axis=-1, keepdims=True) + LN_EPS)).reshape(NS, N_HEADS, tl, DH)
    for h in range(N_HEADS):
        hs = slice(h * DH, (h + 1) * DH)
        ya_ref[:, :, hs] = (_sigmoid(o_ref[:, :, hs]) * hn[:, h] * ng_ref[:, hs]).astype(BF16)


def _mlstm_scan(pre, proj, C0, n0, m0, ng, SB, G, tl, layer, C_prev):
    nb, lp, _ = proj.shape
    NS = SB * G
    NI, NC = nb // NS, lp // tl
    six = lambda a: a.reshape((NI * SB, NC) + a.shape[1:])
    spec6 = lambda a: pl.BlockSpec((SB, 1) + a.shape[1:], lambda i, c: (i, c, 0, 0, 0, 0))
    seq4 = lambda i, c: (i, 0, 0, 0)
    st_in, st_out, st_shape, extra_in, extra_args = _layered_state(layer, nb, NS, C_prev)
    n_in = len(pre) + 5
    return pl.pallas_call(
        functools.partial(_mlstm_scan_body, NS=NS, tl=tl, NC=NC, first=layer == 0),
        grid=(NI, NC),
        in_specs=[spec6(a) for a in pre] + [
            pl.BlockSpec((NS, tl, D_MIX), lambda i, c: (i, c, (OFF_QKVO + 3 * D_MIX) // D_MIX)),
            st_in,
            pl.BlockSpec((NS, N_HEADS, DH), lambda i, c: (i, 0, 0)),
            pl.BlockSpec((NS, N_HEADS, 1, 128), seq4),
            pl.BlockSpec((1, D_MIX), lambda i, c: (0, 0))] + extra_in,
        out_specs=[pl.BlockSpec((NS, tl, D_MIX), lambda i, c: (i, c, 0)),
                   st_out,
                   pl.BlockSpec((NS, N_HEADS, DH), lambda i, c: (i, 0, 0)),
                   pl.BlockSpec((NS, N_HEADS, 1, 128), seq4)],
        out_shape=[jax.ShapeDtypeStruct((nb, lp, D_MIX), BF16),
                   st_shape,
                   jax.ShapeDtypeStruct((nb, N_HEADS, DH), F32),
                   jax.ShapeDtypeStruct((nb, N_HEADS, 1, 128), F32)],
        input_output_aliases={n_in: 1} if extra_in else {},
        compiler_params=pltpu.CompilerParams(dimension_semantics=("arbitrary", "arbitrary"),
                                             vmem_limit_bytes=VMEM_LIMIT),
        name=f"mlstm_scan_g{G}",
    )(*[six(a) for a in pre], proj, C0, n0, m0, ng, *extra_args)


def _unit_lower_inverse(n, rr, cc, tl):
    eye = (rr == cc).astype(F32)[None]
    p = jnp.where(((rr // INV_BLOCK) == (cc // INV_BLOCK))[None], n, 0.0)
    x = eye + p
    b = 2
    while b < INV_BLOCK:
        p = _bmm(p, p)
        x = x + _bmm(x, p)
        b *= 2
    b = INV_BLOCK
    while b < tl:
        off = jnp.where((((rr // (2 * b)) == (cc // (2 * b))) & ((rr // b) != (cc // b)))[None], n, 0.0)
        x = x + _bmm(x, _bmm(off, x))
        b *= 2
    return x


def _gdn_prep_body(x_ref, prev_ref, sa_ref, sb_ref, gc0_ref, cw_ref, alog_ref, dtb_ref,
                   u_ref, w_ref, qg_ref, kd_ref, qkm_ref, eg_ref, gcs_ref, xp_s, *, NB, G, tl, tlx, NCS, lv):
    cs = pl.program_id(1)
    lvl = lv - (NCS - 1) * tlx

    _conv_history(xp_s, gc0_ref, prev_ref, NCS)
    xp_s[:, HIST:HIST + tlx, :] = x_ref[...]
    qkv = _silu(_conv_taps(xp_s, cw_ref, CONV_C, tlx)).reshape(NB, RB, 3 * D_MIX)

    @pl.when(cs == NCS - 1)
    def _():
        gcs_ref[...] = xp_s[:, pl.ds(HIST + lvl - (CONV_C - 1), CONV_C - 1), :]

    beta_all = _sigmoid(sa_ref[...].reshape(NB, RB, 128))
    g_all = -jnp.exp(alog_ref[...]) * _softplus(sb_ref[...].reshape(NB, RB, 128) + dtb_ref[...])
    if lv < NCS * tlx:
        assert NCS == 1
        valid = (lax.broadcasted_iota(jnp.int32, (NB, RB, 128), 1) % tl) < lv
        beta_all = jnp.where(valid, beta_all, 0.0)
        g_all = jnp.where(valid, g_all, 0.0)
    rr, cc, incl = _block_masks(tl)
    diag = rr == cc
    gam = _seq_cumsum(g_all, incl, NB)
    glast = _seq_last(gam, NB, G, tl)
    gcol = _gate_cols(gam, NB, 4)
    bcol = _gate_cols(beta_all, NB, 4)
    egcol = _gate_cols(jnp.exp(gam), NB, 4)
    kdcol = _gate_cols(jnp.exp(_seq_rows(glast, NB, tl) - gam), NB, 4)

    q = _heads(qkv[:, :, 0:D_MIX], NB, DH)
    k = _heads(qkv[:, :, D_MIX:2 * D_MIX], NB, DH)
    v = _heads(qkv[:, :, 2 * D_MIX:3 * D_MIX], NB, DH)
    q = q * lax.rsqrt(jnp.sum(q * q, axis=-1, keepdims=True) + NORM_EPS) * (DH ** -0.5)
    k = k * lax.rsqrt(jnp.sum(k * k, axis=-1, keepdims=True) + NORM_EPS)
    qb, kb = q.astype(BF16), k.astype(BF16)

    dmat = jnp.exp(jnp.where(incl[None], gcol - _gate_rows(gam, NB, 4), NEG))
    nmat = jnp.where(diag[None], 0.0, -(bcol * _bmm_nt(kb, kb) * dmat))
    rhs = jnp.concatenate([bcol * v, (bcol * egcol) * k], axis=-1)
    sol = _bmm(_unit_lower_inverse(nmat, rr, cc, tl), rhs)
    qkm = (_bmm_nt(qb, kb) * dmat).astype(BF16).reshape(NB, N_HEADS, RB, RB)

    _put_chains(u_ref, sol[:, :, 0:DH], NB, G, tl)
    _put_chains(w_ref, sol[:, :, DH:2 * DH].astype(BF16), NB, G, tl)
    _put_chains(qg_ref, (q * egcol).astype(BF16), NB, G, tl)
    _put_chains(kd_ref, (k * kdcol).astype(BF16), NB, G, tl)
    _put_seq_scalars(eg_ref, jnp.exp(glast), NB, G, 4)
    for h in range(N_HEADS):
        for g in range(G):
            qkm_ref[:, g, h] = qkm[:, h, g * tl:(g + 1) * tl, g * tl:(g + 1) * tl]


def _gdn_prep(proj, gc0, cw, alog, dtb, NB, G, tl, lv):
    nb, lp, _ = proj.shape
    assert tl % INV_BLOCK == 0 and (tl // INV_BLOCK) & (tl // INV_BLOCK - 1) == 0
    gx, tlx, NI, NCS = _step_tiling(nb, lp, NB, G, tl)
    nbt = nb * lp // RB
    step = lambda i, c: (i * NCS + c, 0, 0, 0, 0)
    par = lambda i, c: (0, 0)
    chain = lambda last, dt: jax.ShapeDtypeStruct((nbt, G, N_HEADS, tl, last), dt)
    cspec = lambda last: pl.BlockSpec((NB, G, N_HEADS, tl, last), step)
    return pl.pallas_call(
        functools.partial(_gdn_prep_body, NB=NB, G=G, tl=tl, tlx=tlx, NCS=NCS, lv=lv),
        grid=(NI, NCS),
        in_specs=[pl.BlockSpec((gx, tlx, 3 * D_MIX), lambda i, c: (i, c, OFF_QKVC // (3 * D_MIX))),
                  pl.BlockSpec((gx, HIST, 3 * D_MIX),
                               lambda i, c: (i, jnp.maximum(c * (tlx // HIST) - 1, 0), OFF_QKVC // (3 * D_MIX))),
                  pl.BlockSpec((gx, tlx, 128), lambda i, c: (i, c, OFF_SA // 128)),
                  pl.BlockSpec((gx, tlx, 128), lambda i, c: (i, c, OFF_SB // 128)),
                  pl.BlockSpec((gx, HIST, 3 * D_MIX), lambda i, c: (i, 0, 0)),
                  pl.BlockSpec((8, 3 * D_MIX), par),
                  pl.BlockSpec((1, 128), par),
                  pl.BlockSpec((1, 128), par)],
        out_specs=[cspec(DH), cspec(DH), cspec(DH), cspec(DH), cspec(tl),
                   pl.BlockSpec((NB, G, N_HEADS, 1, 128), step),
                   pl.BlockSpec((gx, CONV_C - 1, 3 * D_MIX), lambda i, c: (i, 0, 0))],
        out_shape=[chain(DH, F32), chain(DH, BF16), chain(DH, BF16), chain(DH, BF16), chain(tl, BF16),
                   jax.ShapeDtypeStruct((nbt, G, N_HEADS, 1, 128), F32),
                   jax.ShapeDtypeStruct((nb, CONV_C - 1, 3 * D_MIX), F32)],
        scratch_shapes=[pltpu.VMEM((gx, HIST + tlx, 3 * D_MIX), F32)],
        compiler_params=pltpu.CompilerParams(dimension_semantics=("arbitrary", "arbitrary"),
                                             vmem_limit_bytes=VMEM_LIMIT),
        name=f"gdn_prep_g{G}",
    )(proj, proj, proj, proj, gc0, cw, alog, dtb)


def _gdn_scan_body(u_ref, w_ref, qg_ref, kd_ref, qkm_ref, eg_ref, z_ref, S0_ref, gng_ref, *rest,
                   NS, tl, NC, first):
    yc_ref, S_out = rest[-2:]
    S_ref = S_out.at[0] if first else S_out
    c = pl.program_id(1)
    n = NS * N_HEADS

    @pl.when(c == 0)
    def _():
        S_ref[...] = S0_ref[...]

    S = S_ref[...].reshape(n, DH, DH)
    Sb = S.astype(BF16)
    chains = lambda ref: ref[...].reshape(n, tl, ref.shape[-1])
    v_new = chains(u_ref) - jnp.einsum('ntd,nde->nte', chains(w_ref), Sb, preferred_element_type=F32)
    vnb = v_new.astype(BF16)
    o = (jnp.einsum('ntd,nde->nte', chains(qg_ref), Sb, preferred_element_type=F32)
         + jnp.einsum('nts,nse->nte', chains(qkm_ref), vnb, preferred_element_type=F32))
    eg = eg_ref[...].reshape(n, 1, 128)[:, :, 0:1]
    S_new = eg * S + jnp.einsum('ntd,nte->nde', chains(kd_ref), vnb, preferred_element_type=F32)
    S_ref[...] = S_new.reshape(NS, N_HEADS, DH, DH)
    if first:
        _fill_later_layers(S_out, c == NC - 1)

    on = (o * lax.rsqrt(jnp.mean(o * o, axis=-1, keepdims=True) + NORM_EPS) * gng_ref[...]).reshape(NS, N_HEADS, tl, DH)
    for h in range(N_HEADS):
        yc_ref[:, :, h * DH:(h + 1) * DH] = (on[:, h] * _silu(z_ref[:, :, h * DH:(h + 1) * DH])).astype(BF16)


def _gdn_scan(pre, proj, S0, gng, SB, G, tl, layer, S_prev):
    nb, lp, _ = proj.shape
    NS = SB * G
    NI, NC = nb // NS, lp // tl
    six = lambda a: a.reshape((NI * SB, NC) + a.shape[1:])
    cspec = lambda last: pl.BlockSpec((SB, 1, G, N_HEADS, tl, last), lambda i, c: (i, c, 0, 0, 0, 0))
    u, w, qg, kd, qkm, eg = (six(a) for a in pre)
    st_in, st_out, st_shape, extra_in, extra_args = _layered_state(layer, nb, NS, S_prev)
    return pl.pallas_call(
        functools.partial(_gdn_scan_body, NS=NS, tl=tl, NC=NC, first=layer == 0),
        grid=(NI, NC),
        in_specs=[cspec(DH), cspec(DH), cspec(DH), cspec(DH), cspec(tl),
                  pl.BlockSpec((SB, 1, G, N_HEADS, 1, 128), lambda i, c: (i, c, 0, 0, 0, 0)),
                  pl.BlockSpec((NS, tl, D_MIX), lambda i, c: (i, c, OFF_Z // D_MIX)),
                  st_in,
                  pl.BlockSpec((1, DH), lambda i, c: (0, 0))] + extra_in,
        out_specs=[pl.BlockSpec((NS, tl, D_MIX), lambda i, c: (i, c, 0)), st_out],
        out_shape=[jax.ShapeDtypeStruct((nb, lp, D_MIX), BF16), st_shape],
        input_output_aliases={9: 1} if extra_in else {},
        compiler_params=pltpu.CompilerParams(dimension_semantics=("arbitrary", "arbitrary"),
                                             vmem_limit_bytes=VMEM_LIMIT),
        name=f"gdn_scan_g{G}",
    )(u, w, qg, kd, qkm, eg, proj, S0, gng, *extra_args)


def _route(rl):
    lane = lax.broadcasted_iota(jnp.int32, rl.shape, 1).astype(F32)
    is_g = (lane >= N_EXPERTS) & (lane < N_EXPERTS + N_GROUPS)
    gl = jnp.where(is_g, rl, NEG)
    gmax = jnp.max(gl, axis=-1, keepdims=True)
    grp = jnp.min(jnp.where(gl == gmax, lane - N_EXPERTS, 4.0 * N_EXPERTS), axis=-1, keepdims=True)
    p_grp = 1.0 / jnp.sum(jnp.where(is_g, jnp.exp(gl - gmax), 0.0), axis=-1, keepdims=True)
    lo = grp * EXPERTS_PER_GROUP
    in_grp = (lane >= lo) & (lane < lo + EXPERTS_PER_GROUP)
    el = jnp.where(in_grp, rl, NEG)
    m1 = jnp.max(el, axis=-1, keepdims=True)
    i1 = jnp.min(jnp.where(el == m1, lane, 4.0 * N_EXPERTS), axis=-1, keepdims=True)
    el2 = jnp.where(lane == i1, NEG, el)
    m2 = jnp.max(el2, axis=-1, keepdims=True)
    i2 = jnp.min(jnp.where(el2 == m2, lane, 4.0 * N_EXPERTS), axis=-1, keepdims=True)
    e2 = jnp.exp(m2 - m1)
    w1 = p_grp / (1.0 + e2)
    w2 = p_grp * e2 / (1.0 + e2)
    return i1, i2, w1, w2


def _merge_body(ya_ref, yb_ref, yc_ref, ga_ref, gb_ref, gc_ref, x_ref, mod_ref, wa_ref, wb_ref, wc_ref, wo_ref,
                lg_ref, lb_ref, wrh_ref, wrl_ref, br_ref, x1_ref, rinfo_ref, cnt_ref, *, bt, lt):
    tm = bt * lt

    @pl.when((pl.program_id(0) == 0) & (pl.program_id(1) == 0))
    def _():
        cnt_ref[...] = jnp.zeros_like(cnt_ref)

    def r2(ref):
        return ref[...].reshape(tm, ref.shape[-1])

    merged = (_sigmoid(r2(ga_ref)) * _dot(r2(ya_ref), wa_ref[0])
              + _sigmoid(r2(gb_ref)) * _dot(r2(yb_ref), wb_ref[0])
              + _sigmoid(r2(gc_ref)) * _dot(r2(yc_ref), wc_ref[0]))
    out = _dot(merged.astype(BF16), wo_ref[0])
    y = DN_ALPHA * x_ref[...] + (1.0 + mod_ref[:, 2:3, :]) * out.reshape(bt, lt, D_MODEL)
    x1 = _layer_norm(y, lg_ref[0], lb_ref[0])
    x1_ref[...] = x1
    u2 = (x1 * (1.0 + mod_ref[:, 4:5, :]) + mod_ref[:, 3:4, :]).reshape(tm, D_MODEL)
    hi = u2.astype(BF16)
    lo = (u2 - hi.astype(F32)).astype(BF16)
    rl = _dot(hi, wrh_ref[0]) + _dot(lo, wrh_ref[0]) + _dot(hi, wrl_ref[0]) + br_ref[0]
    i1, i2, w1, w2 = _route(rl)
    lane = lax.broadcasted_iota(jnp.int32, (tm, 128), 1).astype(F32)
    onehot = jnp.where((lane == i1) | (lane == i2), 1.0, 0.0)
    rr = lax.broadcasted_iota(jnp.int32, (tm, tm), 0)
    cc = lax.broadcasted_iota(jnp.int32, (tm, tm), 1)
    before = _dot((rr > cc).astype(BF16), onehot.astype(BF16)) + cnt_ref[0:1, :]
    rank1 = jnp.sum(jnp.where(lane == i1, before, 0.0), axis=-1, keepdims=True)
    rank2 = jnp.sum(jnp.where(lane == i2, before, 0.0), axis=-1, keepdims=True)
    cnt_ref[0:1, :] += jnp.sum(onehot, axis=0, keepdims=True)
    rinfo = jnp.zeros((tm, 128), F32)
    for k, val in enumerate((i1, i2, w1, w2, rank1, rank2)):
        rinfo = jnp.where(lane == k, val, rinfo)
    rinfo_ref[...] = rinfo.reshape(bt, lt, 128)


def _merge(ya, yb, yc, proj, x, mod, wa, wb, wc, wo, lg, lb, wrh, wrl, br, layer, bt, lt):
    nb, lp, _ = x.shape
    tok = lambda i, t: (i, t, 0)
    wsp = lambda shape: pl.BlockSpec((1,) + shape, lambda i, t: (layer, 0, 0))
    g0 = OFF_GTS // D_MODEL
    return pl.pallas_call(
        functools.partial(_merge_body, bt=bt, lt=lt),
        grid=(nb // bt, lp // lt),
        in_specs=[pl.BlockSpec((bt, lt, D_MIX), tok),
                  pl.BlockSpec((bt, lt, D_MIX), tok),
                  pl.BlockSpec((bt, lt, D_MIX), tok),
                  pl.BlockSpec((bt, lt, D_MODEL), lambda i, t: (i, t, g0)),
                  pl.BlockSpec((bt, lt, D_MODEL), lambda i, t: (i, t, g0 + 1)),
                  pl.BlockSpec((bt, lt, D_MODEL), lambda i, t: (i, t, g0 + 2)),
                  pl.BlockSpec((bt, lt, D_MODEL), tok),
                  pl.BlockSpec((bt, 6, D_MODEL), lambda i, t: (i, 0, 0)),
                  wsp((D_MIX, D_MODEL)), wsp((D_MIX, D_MODEL)), wsp((D_MIX, D_MODEL)), wsp((D_MODEL, D_MODEL)),
                  wsp((1, D_MODEL)), wsp((1, D_MODEL)),
                  wsp((D_MODEL, 128)), wsp((D_MODEL, 128)), wsp((1, 128))],
        out_specs=[pl.BlockSpec((bt, lt, D_MODEL), tok),
                   pl.BlockSpec((bt, lt, 128), tok),
                   pl.BlockSpec((8, 128), lambda i, t: (0, 0))],
        out_shape=[jax.ShapeDtypeStruct((nb, lp, D_MODEL), F32),
                   jax.ShapeDtypeStruct((nb, lp, 128), F32),
                   jax.ShapeDtypeStruct((8, 128), F32)],
        compiler_params=pltpu.CompilerParams(dimension_semantics=("arbitrary", "arbitrary"),
                                             vmem_limit_bytes=VMEM_LIMIT),
        name=f"merge_l{layer}_b{bt}",
    )(ya, yb, yc, proj, proj, proj, x, mod, wa, wb, wc, wo, lg, lb, wrh, wrl, br)


def _moe_plan(rinfo, cnt, tm, mb):
    nb, lp, _ = rinfo.shape
    n_tok = nb * lp
    n_blocks = 2 * n_tok // mb + N_EXPERTS
    counts = cnt[0, :N_EXPERTS].astype(jnp.int32)
    nblk = (counts + mb - 1) // mb
    pend = jnp.cumsum(nblk)
    pstart = pend - nblk
    expert = rinfo[..., 0:2].astype(jnp.int32)
    rank = rinfo[..., 4:6].astype(jnp.int32)
    ids = jnp.arange(N_EXPERTS, dtype=jnp.int32)
    first_row = jnp.sum(jnp.where(expert[..., None] == ids, pstart * mb, 0), axis=-1)
    pos = (first_row + rank).reshape(n_tok // tm, 1, 2 * tm)
    block_e = jnp.sum(pend[None, :] <= jnp.arange(n_blocks, dtype=jnp.int32)[:, None], axis=1)
    block_e = jnp.minimum(block_e, N_EXPERTS - 1).astype(jnp.int32)
    n_used = pend[N_EXPERTS - 1:].astype(jnp.int32)
    ztail = jnp.where(nblk > 0, (pend - 1) * mb, -1).astype(jnp.int32)
    ztail = jnp.concatenate([ztail, n_used]).reshape(1, 1, N_EXPERTS + 1)
    return pos, block_e, n_used, ztail, n_blocks


def _row_copy(src, dst, sem):
    return pltpu.make_async_copy(src, dst, sem)


def _dispatch_body(pos_ref, ztail_ref, x1_ref, mod_ref, disp_ref, u_scr, z_scr, sem, zsem, *, bt, lt, n_blocks, mb):
    tm = bt * lt

    @pl.when((pl.program_id(0) == 0) & (pl.program_id(1) == 0))
    def _():
        z_scr[...] = jnp.zeros_like(z_scr)
        for e in range(N_EXPERTS):
            @pl.when(ztail_ref[0, 0, e] >= 0)
            def _():
                row = pl.multiple_of(ztail_ref[0, 0, e], mb)
                _row_copy(z_scr, disp_ref.at[pl.ds(row, mb)], zsem).start()
        def unused(j):
            return _row_copy(z_scr, disp_ref.at[pl.ds(pl.multiple_of(j * mb, mb), mb)], zsem)

        n_used = ztail_ref[0, 0, N_EXPERTS]
        lax.fori_loop(n_used, n_blocks, lambda j, c: (unused(j).start(), c)[1], 0)
        lax.fori_loop(n_used, n_blocks, lambda j, c: (unused(j).wait(), c)[1], 0)
        for e in range(N_EXPERTS):
            @pl.when(ztail_ref[0, 0, e] >= 0)
            def _():
                row = pl.multiple_of(ztail_ref[0, 0, e], mb)
                _row_copy(z_scr, disp_ref.at[pl.ds(row, mb)], zsem).wait()

    u2 = x1_ref[...] * (1.0 + mod_ref[:, 4:5, :]) + mod_ref[:, 3:4, :]
    u_scr[...] = u2.reshape(tm // 8, 8, D_MODEL)

    def issue(i, carry):
        for r in range(8):
            for k in range(2):
                row = pos_ref[0, 0, 16 * i + 2 * r + k]
                _row_copy(u_scr.at[i, pl.ds(r, 1)], disp_ref.at[pl.ds(row, 1)], sem).start()
        return carry

    lax.fori_loop(0, tm // 8, issue, 0)
    for k in range(2):
        _row_copy(disp_ref.at[pl.ds(0, tm)], disp_ref.at[pl.ds(0, tm)], sem).wait()


def _dispatch(x1, mod, pos, ztail, n_blocks, mb, bt, lt):
    nb, lp, _ = x1.shape
    nt = lp // lt
    return pl.pallas_call(
        functools.partial(_dispatch_body, bt=bt, lt=lt, n_blocks=n_blocks, mb=mb),
        grid=(nb // bt, nt),
        in_specs=[pl.BlockSpec((1, 1, 2 * bt * lt), lambda i, t: (i * nt + t, 0, 0), memory_space=pltpu.SMEM),
                  pl.BlockSpec((1, 1, N_EXPERTS + 1), lambda i, t: (0, 0, 0), memory_space=pltpu.SMEM),
                  pl.BlockSpec((bt, lt, D_MODEL), lambda i, t: (i, t, 0)),
                  pl.BlockSpec((bt, 6, D_MODEL), lambda i, t: (i, 0, 0))],
        out_specs=pl.BlockSpec(memory_space=pl.ANY),
        out_shape=jax.ShapeDtypeStruct((n_blocks * mb, D_MODEL), F32),
        scratch_shapes=[pltpu.VMEM((bt * lt // 8, 8, D_MODEL), F32), pltpu.VMEM((mb, D_MODEL), F32),
                        pltpu.SemaphoreType.DMA, pltpu.SemaphoreType.DMA],
        compiler_params=pltpu.CompilerParams(dimension_semantics=("arbitrary", "arbitrary"),
                                             vmem_limit_bytes=VMEM_LIMIT),
        name=f"dispatch_b{bt}",
    )(pos, ztail, x1, mod)


def _experts_body(be_ref, nu_ref, x_ref, wg_ref, wu_ref, wd_ref, o_ref, wg_s, wu_s, wd_s):
    j = pl.program_id(0)

    @pl.when((j == 0) | (be_ref[j] != be_ref[jnp.maximum(j - 1, 0)]))
    def _():
        wg_s[...] = wg_ref[0, 0].astype(BF16)
        wu_s[...] = wu_ref[0, 0].astype(BF16)
        wd_s[...] = wd_ref[0, 0].astype(BF16)

    @pl.when(j < nu_ref[0])
    def _():
        x = x_ref[...].astype(BF16)
        hb = _silu(_dot(x, wg_s[...])) * _dot(x, wu_s[...])
        o_ref[...] = _dot(hb.astype(BF16), wd_s[...])

    @pl.when(j >= nu_ref[0])
    def _():
        o_ref[...] = jnp.zeros_like(o_ref)


def _experts(disp, block_e, n_used, wg, wu, wd, layer, mb):
    n_blocks = disp.shape[0] // mb
    wmap = lambda j, be, nu: (layer, be[j], 0, 0)
    return pl.pallas_call(
        _experts_body,
        grid_spec=pltpu.PrefetchScalarGridSpec(
            num_scalar_prefetch=2,
            grid=(n_blocks,),
            in_specs=[pl.BlockSpec((mb, D_MODEL), lambda j, be, nu: (jnp.minimum(j, nu[0] - 1), 0)),
                      pl.BlockSpec((1, 1, D_MODEL, D_EXPERT), wmap),
                      pl.BlockSpec((1, 1, D_MODEL, D_EXPERT), wmap),
                      pl.BlockSpec((1, 1, D_EXPERT, D_MODEL), wmap)],
            out_specs=pl.BlockSpec((mb, D_MODEL), lambda j, be, nu: (j, 0)),
            scratch_shapes=[pltpu.VMEM((D_MODEL, D_EXPERT), BF16), pltpu.VMEM((D_MODEL, D_EXPERT), BF16),
                            pltpu.VMEM((D_EXPERT, D_MODEL), BF16)]),
        out_shape=jax.ShapeDtypeStruct(disp.shape, F32),
        compiler_params=pltpu.CompilerParams(dimension_semantics=("arbitrary",), vmem_limit_bytes=VMEM_LIMIT),
        name=f"experts_l{layer}_n{n_blocks}",
    )(block_e, n_used, disp, wg, wu, wd)


def _combine_body(pos_ref, x1_ref, mod_ref, rinfo_ref, eo_ref, lg_ref, lb_ref, x2_ref, r_scr, sem, *, bt, lt):
    tm = bt * lt

    def issue(i, carry):
        for r in range(8):
            for k in range(2):
                row = pos_ref[0, 0, 16 * i + 2 * r + k]
                _row_copy(eo_ref.at[pl.ds(row, 1)], r_scr.at[k, i, pl.ds(r, 1)], sem).start()
        return carry

    lax.fori_loop(0, tm // 8, issue, 0)
    for k in range(2):
        _row_copy(eo_ref.at[pl.ds(0, tm)], eo_ref.at[pl.ds(0, tm)], sem).wait()

    rinfo = rinfo_ref[...].reshape(tm, 128)
    moe = (r_scr[0].reshape(tm, D_MODEL) * rinfo[:, 2:3] + r_scr[1].reshape(tm, D_MODEL) * rinfo[:, 3:4])
    y = DN_ALPHA * x1_ref[...] + (1.0 + mod_ref[:, 5:6, :]) * moe.reshape(bt, lt, D_MODEL)
    x2_ref[...] = _layer_norm(y, lg_ref[0], lb_ref[0])


def _combine(x1, mod, rinfo, eo, pos, lg, lb, layer, bt, lt):
    nb, lp, _ = x1.shape
    nt = lp // lt
    tok = lambda i, t: (i, t, 0)
    return pl.pallas_call(
        functools.partial(_combine_body, bt=bt, lt=lt),
        grid=(nb // bt, nt),
        in_specs=[pl.BlockSpec((1, 1, 2 * bt * lt), lambda i, t: (i * nt + t, 0, 0), memory_space=pltpu.SMEM),
                  pl.BlockSpec((bt, lt, D_MODEL), tok),
                  pl.BlockSpec((bt, 6, D_MODEL), lambda i, t: (i, 0, 0)),
                  pl.BlockSpec((bt, lt, 128), tok),
                  pl.BlockSpec(memory_space=pl.ANY),
                  pl.BlockSpec((1, 1, D_MODEL), lambda i, t: (layer, 0, 0)),
                  pl.BlockSpec((1, 1, D_MODEL), lambda i, t: (layer, 0, 0))],
        out_specs=pl.BlockSpec((bt, lt, D_MODEL), tok),
        out_shape=jax.ShapeDtypeStruct((nb, lp, D_MODEL), F32),
        scratch_shapes=[pltpu.VMEM((2, bt * lt // 8, 8, D_MODEL), F32), pltpu.SemaphoreType.DMA],
        compiler_params=pltpu.CompilerParams(dimension_semantics=("arbitrary", "arbitrary"),
                                             vmem_limit_bytes=VMEM_LIMIT),
        name=f"combine_l{layer}_b{bt}",
    )(pos, x1, mod, rinfo, eo, lg, lb)


def _hist(state):
    return jnp.pad(state, ((0, 0), (HIST - state.shape[1], 0), (0, 0)))


def _trunk(x, mod, st, p, bt, lt, lv, mix, moe_block, ip_tile):
    nb = x.shape[0]
    new = {key: [] for key in ('n', 'm', 'conv', 'gconv')}
    C = S = None
    for l in range(DEPTH):
        proj, x = _inproj(x, mod[l], p['ln_in_g'], p['ln_in_b'], p['w_in_r'], p['b_in_r'], l, l == 0, *ip_tile)
        m0 = jnp.broadcast_to(st['m'][l][:, :, None, None], (nb, N_HEADS, 1, 128))
        yb, conv, *pre = _mlstm_prep(proj, _hist(st['conv'][l]), p['conv_b_w8'][l], mix['NB'], mix['G'], mix['tl'], lv)
        ya, C, n, m = _mlstm_scan(pre, proj, st['C'], st['n'][l], m0, p['mlstm_norm_g'][l:l + 1],
                                  mix['SB'], mix['G'], mix['tl'], l, C)
        *pre, gconv = _gdn_prep(proj, _hist(st['gconv'][l]), p['conv_c_w8'][l], p['alog_row'][l], p['dtb_row'][l],
                                mix['NB'], mix['G'], mix['tl'], lv)
        yc, S = _gdn_scan(pre, proj, st['S'], p['gdn_norm_g'][l:l + 1], mix['SB'], mix['G'], mix['tl'], l, S)
        x1, rinfo, cnt = _merge(ya, yb, yc, proj, x, mod[l], p['w_br_a'], p['w_br_b'], p['w_br_c'], p['w_out'],
                                p['ln1_g'], p['ln1_b'], p['wr_hi'], p['wr_lo'], p['br'], l, bt, lt)
        pos, block_e, n_used, ztail, n_blocks = _moe_plan(rinfo, cnt, bt * lt, moe_block)
        disp = _dispatch(x1, mod[l], pos, ztail, n_blocks, moe_block, bt, lt)
        eo = _experts(disp, block_e, n_used, p['exp_w_gate'], p['exp_w_up'], p['exp_w_down'], l, moe_block)
        x = _combine(x1, mod[l], rinfo, eo, pos, p['ln2_g'], p['ln2_b'], l, bt, lt)
        new['n'].append(n)
        new['m'].append(m[:, :, 0, 0])
        new['conv'].append(conv)
        new['gconv'].append(gconv)
    return x, dict({key: jnp.stack(val) for key, val in new.items()}, C=C, S=S)


def kernel(x_prompt, x_sample, state_mlstm_C, state_mlstm_n, state_mlstm_m, state_conv, state_gdn_S, state_gdn_conv, c_prompt, c_sample, ln_in_g, ln_in_b, w_ada, b_ada, w_in, b_in, mlstm_norm_g, conv_b_w, conv_c_w, gdn_a_log, gdn_dt_bias, gdn_norm_g, w_br_a, w_br_b, w_br_c, w_out, ln1_g, ln1_b, router_g_w, router_g_b, router_e_w, router_e_b, exp_w_gate, exp_w_up, exp_w_down, ln2_g, ln2_b):
    nbp, lp, _ = x_prompt.shape
    nbs, ls, _ = x_sample.shape
    lsp = 8

    def regroup(a):
        out = jnp.zeros(a.shape[:-1] + (N_PROJ,), a.dtype)
        for src, end, dst in PROJ_SEGMENTS:
            out = out.at[..., dst:dst + end - src].set(a[..., src:end])
        return out

    wr = jnp.concatenate([router_e_w, router_g_w, jnp.zeros((DEPTH, D_MODEL, 128 - N_EXPERTS - N_GROUPS), F32)], axis=-1)
    wr_hi = wr.astype(BF16)
    lane_pad = lambda a: jnp.pad(a, ((0, 0), (4, 128 - 4 - N_HEADS)))[:, None, :]
    p = dict(
        ln_in_g=ln_in_g.reshape(1, D_MODEL), ln_in_b=ln_in_b.reshape(1, D_MODEL),
        w_in_r=_regroup_w_in(w_in), b_in_r=regroup(b_in).reshape(DEPTH, 1, N_PROJ),
        mlstm_norm_g=mlstm_norm_g,
        conv_b_w8=jnp.pad(conv_b_w, ((0, 0), (0, 8 - CONV_B), (0, 0))),
        conv_c_w8=jnp.pad(conv_c_w, ((0, 0), (0, 8 - CONV_C), (0, 0))),
        alog_row=lane_pad(gdn_a_log), dtb_row=lane_pad(gdn_dt_bias), gdn_norm_g=gdn_norm_g,
        w_br_a=w_br_a.astype(BF16), w_br_b=w_br_b.astype(BF16), w_br_c=w_br_c.astype(BF16),
        w_out=w_out.astype(BF16),
        ln1_g=ln1_g.reshape(DEPTH, 1, D_MODEL), ln1_b=ln1_b.reshape(DEPTH, 1, D_MODEL),
        wr_hi=wr_hi, wr_lo=(wr - wr_hi.astype(F32)).astype(BF16),
        br=jnp.concatenate([router_e_b, router_g_b, jnp.zeros((DEPTH, 128 - N_EXPERTS - N_GROUPS), F32)],
                           axis=-1).reshape(DEPTH, 1, 128),
        exp_w_gate=exp_w_gate, exp_w_up=exp_w_up, exp_w_down=exp_w_down,
        ln2_g=ln2_g.reshape(DEPTH, 1, D_MODEL), ln2_b=ln2_b.reshape(DEPTH, 1, D_MODEL),
    )

    mod = _ada(jnp.concatenate([c_prompt, c_sample], axis=0), w_ada, b_ada)
    mod = mod.reshape(DEPTH, nbp + nbs, 6, D_MODEL)

    zeros = lambda *s: jnp.zeros((DEPTH, nbp) + s, F32)
    st_p = {'C': zeros(N_HEADS, DH, DH), 'n': zeros(N_HEADS, DH), 'm': zeros(N_HEADS),
            'conv': zeros(CONV_B - 1, D_MIX), 'S': zeros(N_HEADS, DH, DH), 'gconv': zeros(CONV_C - 1, 3 * D_MIX)}
    y_p, sp = _trunk(x_prompt, mod[:, :nbp], st_p, p, bt=1, lt=512, lv=lp,
                     mix=dict(NB=8, G=1, tl=RB, SB=nbp), moe_block=256, ip_tile=(1, 256))

    st_s = {'C': state_mlstm_C, 'n': state_mlstm_n, 'm': state_mlstm_m, 'conv': state_conv,
            'S': state_gdn_S, 'gconv': state_gdn_conv}
    xs = jnp.pad(x_sample, ((0, 0), (0, lsp - ls), (0, 0)))
    y_s, ss = _trunk(xs, mod[:, nbp:], st_s, p, bt=64, lt=lsp, lv=ls,
                     mix=dict(NB=2, G=RB // lsp, tl=lsp, SB=1), moe_block=128, ip_tile=(32, lsp))
    y_s = y_s[:, :ls]

    return (y_p, y_s, sp['C'], sp['n'], sp['m'], sp['conv'], sp['S'], sp['gconv'],
            ss['C'], ss['n'], ss['m'], ss['conv'], ss['S'], ss['gconv'])
```

```python
import functools

import jax
import jax.numpy as jnp
from jax import lax
from jax.experimental import pallas as pl
from jax.experimental.pallas import tpu as pltpu

F32 = jnp.float32
BF16 = jnp.bfloat16

D_MODEL = 1024
DEPTH = 2
N_HEADS = 4
DH = 128
D_MIX = N_HEADS * DH
N_EXPERTS = 32
EXPERTS_PER_GROUP = 8
N_GROUPS = 4
D_EXPERT = 256
CONV_B = 3
CONV_C = 4
HIST = 8
RB = 64
INV_BLOCK = 8
DN_ALPHA = (2 * DEPTH) ** 0.25
LN_EPS = 1e-5
NORM_EPS = 1e-6
NEG = -1e30

OFF_QKVC = 0
OFF_BCH = 1536
OFF_GTS = 3072
OFF_QKVO = 6144
OFF_Z = 8192
OFF_SA = 8704
OFF_SB = 8832
N_PROJ = 8960
TN_PROJ = 1280
PROJ_SEGMENTS = ((3592, 5128, OFF_QKVC), (2056, 3592, OFF_BCH), (5648, 8720, OFF_GTS), (0, 2048, OFF_QKVO),
                 (5128, 5640, OFF_Z), (2048, 2052, OFF_SA), (5640, 5644, OFF_SA + 4), (2052, 2056, OFF_SB),
                 (5644, 5648, OFF_SB + 4))

VMEM_LIMIT = 52 * 1024 * 1024


def _dot(a, b):
    return jnp.dot(a, b, preferred_element_type=F32)


def _split3(x):
    hi = x.astype(BF16)
    r = x - hi.astype(F32)
    mid = r.astype(BF16)
    lo = (r - mid.astype(F32)).astype(BF16)
    return hi, mid, lo


def _layer_norm(x, g, b):
    mu = jnp.mean(x, axis=-1, keepdims=True)
    xc = x - mu
    var = jnp.mean(xc * xc, axis=-1, keepdims=True)
    return xc * lax.rsqrt(var + LN_EPS) * g + b


def _sigmoid(x):
    return jax.nn.sigmoid(x)


def _silu(x):
    return x * _sigmoid(x)


def _log_sigmoid(x):
    return jnp.minimum(x, 0.0) - jnp.log1p(jnp.exp(-jnp.abs(x)))


def _softplus(x):
    return jnp.maximum(x, 0.0) + jnp.log1p(jnp.exp(-jnp.abs(x)))


def _ada_body(c_ref, w_ref, b_ref, o_ref):
    c = c_ref[...]
    s = _silu(c).astype(BF16)
    o_ref[0] = _dot(s, w_ref[0].astype(BF16)) + b_ref[0]


def _ada(c_all, w_ada, b_ada):
    nb = c_all.shape[0]
    return pl.pallas_call(
        _ada_body,
        grid=(DEPTH, 6),
        in_specs=[pl.BlockSpec((nb, D_MODEL), lambda l, j: (0, 0)),
                  pl.BlockSpec((1, D_MODEL, D_MODEL), lambda l, j: (l, 0, j)),
                  pl.BlockSpec((1, 1, D_MODEL), lambda l, j: (l, 0, j))],
        out_specs=pl.BlockSpec((1, nb, D_MODEL), lambda l, j: (l, 0, j)),
        out_shape=jax.ShapeDtypeStruct((DEPTH, nb, 6 * D_MODEL), F32),
        compiler_params=pltpu.CompilerParams(dimension_semantics=("arbitrary", "arbitrary"),
                                             vmem_limit_bytes=VMEM_LIMIT),
        name="ada",
    )(c_all, w_ada, b_ada.reshape(DEPTH, 1, 6 * D_MODEL))


def _regroup_body(w_ref, o_ref):
    o_ref[...] = jnp.zeros_like(o_ref)
    for src, end, dst in PROJ_SEGMENTS:
        o_ref[0, :, dst:dst + end - src] = w_ref[0, :, src:end].astype(BF16)


def _regroup_w_in(w_in):
    n_in = w_in.shape[-1]
    rows = 256
    return pl.pallas_call(
        _regroup_body,
        grid=(DEPTH, D_MODEL // rows),
        in_specs=[pl.BlockSpec((1, rows, n_in), lambda l, i: (l, i, 0))],
        out_specs=pl.BlockSpec((1, rows, N_PROJ), lambda l, i: (l, i, 0)),
        out_shape=jax.ShapeDtypeStruct((DEPTH, D_MODEL, N_PROJ), BF16),
        compiler_params=pltpu.CompilerParams(dimension_semantics=("arbitrary", "arbitrary"),
                                             vmem_limit_bytes=VMEM_LIMIT),
        name="regroup_w_in",
    )(w_in)


def _inproj_body(x_ref, mod_ref, g_ref, b_ref, w_ref, bias_ref, proj_ref, *rest, apply_ln, bt, lt):
    x = x_ref[...]
    if apply_ln:
        x = _layer_norm(x, g_ref[...], b_ref[...])
        rest[0][...] = x
    u = (x * (1.0 + mod_ref[:, 1:2, :]) + mod_ref[:, 0:1, :]).reshape(bt * lt, D_MODEL).astype(BF16)
    for j in range(N_PROJ // TN_PROJ):
        cs = slice(j * TN_PROJ, (j + 1) * TN_PROJ)
        proj_ref[:, :, cs] = (_dot(u, w_ref[0, :, cs]) + bias_ref[0, :, cs]).reshape(bt, lt, TN_PROJ)


def _inproj(x, mod, ln_g, ln_b, w_r, b_r, layer, apply_ln, bt, lt):
    nb, lp, _ = x.shape
    tok = lambda i, t: (i, t, 0)
    out_shape = [jax.ShapeDtypeStruct((nb, lp, N_PROJ), F32)]
    out_specs = [pl.BlockSpec((bt, lt, N_PROJ), tok)]
    if apply_ln:
        out_shape.append(jax.ShapeDtypeStruct((nb, lp, D_MODEL), F32))
        out_specs.append(pl.BlockSpec((bt, lt, D_MODEL), tok))
    res = pl.pallas_call(
        functools.partial(_inproj_body, apply_ln=apply_ln, bt=bt, lt=lt),
        grid=(nb // bt, lp // lt),
        in_specs=[pl.BlockSpec((bt, lt, D_MODEL), tok),
                  pl.BlockSpec((bt, 6, D_MODEL), lambda i, t: (i, 0, 0)),
                  pl.BlockSpec((1, D_MODEL), lambda i, t: (0, 0)),
                  pl.BlockSpec((1, D_MODEL), lambda i, t: (0, 0)),
                  pl.BlockSpec((1, D_MODEL, N_PROJ), lambda i, t: (layer, 0, 0), pipeline_mode=pl.Buffered(1)),
                  pl.BlockSpec((1, 1, N_PROJ), lambda i, t: (layer, 0, 0))],
        out_specs=out_specs,
        out_shape=out_shape,
        compiler_params=pltpu.CompilerParams(dimension_semantics=("arbitrary", "arbitrary"),
                                             vmem_limit_bytes=VMEM_LIMIT),
        name=f"inproj_l{layer}_b{bt}",
    )(x, mod, ln_g, ln_b, w_r, b_r)
    return (res[0], res[1]) if apply_ln else (res[0], x)


def _conv_taps(xp_s, w_ref, width, tl):
    acc = None
    for j in range(width):
        tap = xp_s[:, pl.ds(HIST - (width - 1) + j, tl), :] * w_ref[j:j + 1, :].reshape(1, 1, -1)
        acc = tap if acc is None else acc + tap
    return acc


def _conv_history(xp_s, hist_ref, prev_ref, n_steps):
    if n_steps > 1:
        @pl.when(pl.program_id(1) == 0)
        def _():
            xp_s[:, 0:HIST, :] = hist_ref[...]

        @pl.when(pl.program_id(1) > 0)
        def _():
            xp_s[:, 0:HIST, :] = prev_ref[...]
    else:
        xp_s[:, 0:HIST, :] = hist_ref[...]


def _heads(x, nb, width):
    return jnp.stack([x[:, :, h * width:(h + 1) * width] for h in range(N_HEADS)],
                     axis=1).reshape(nb * N_HEADS, RB, width)


def _gate_cols(x, nb, lane0):
    return jnp.stack([x[:, :, lane0 + h:lane0 + h + 1] for h in range(N_HEADS)],
                     axis=1).reshape(nb * N_HEADS, RB, 1)


def _gate_rows(x, nb, lane0):
    xt = jnp.swapaxes(x, 1, 2)
    return jnp.stack([xt[:, lane0 + h:lane0 + h + 1, :] for h in range(N_HEADS)],
                     axis=1).reshape(nb * N_HEADS, 1, RB)


def _bmm(a, b):
    return jnp.einsum('nts,nsu->ntu', a.astype(BF16), b.astype(BF16), preferred_element_type=F32)


def _bmm_nt(a, b):
    return jnp.einsum('ntd,nsd->nts', a, b, preferred_element_type=F32)


def _block_masks(tl):
    rr = lax.broadcasted_iota(jnp.int32, (RB, RB), 0)
    cc = lax.broadcasted_iota(jnp.int32, (RB, RB), 1)
    incl = rr >= cc
    if tl < RB:
        incl = incl & ((rr // tl) == (cc // tl))
    return rr, cc, incl


def _seq_cumsum(x, incl, nb):
    tril = jnp.broadcast_to(incl.astype(BF16)[None], (nb, RB, RB))
    hi, mid, lo = _split3(x)
    return _bmm(tril, hi) + _bmm(tril, mid) + _bmm(tril, lo)


def _seq_last(x, nb, G, tl):
    return x.reshape(nb * G, tl, 128)[:, tl - 1:tl, :]


def _seq_rows(x3, nb, tl):
    return jnp.broadcast_to(x3, (x3.shape[0], tl, 128)).reshape(nb, RB, 128)


def _put_chains(ref, val, nb, G, tl):
    val4 = val.reshape(nb, N_HEADS, RB, val.shape[-1])
    for h in range(N_HEADS):
        ref[:, :, h] = val4[:, h].reshape(nb, G, tl, val.shape[-1])


def _put_seq_scalars(ref, x3, nb, G, lane0):
    x4 = x3.reshape(nb, G, 1, 128)
    for h in range(N_HEADS):
        ref[:, :, h] = jnp.broadcast_to(x4[:, :, :, lane0 + h:lane0 + h + 1], (nb, G, 1, 128))


def _step_tiling(nb, lp, NB, G, tl):
    assert G * tl == RB
    if G == 1:
        gx, tlx = 1, NB * RB
    else:
        assert lp == tl
        gx, tlx = NB * G, tl
    return gx, tlx, nb // gx, lp // tlx


def _layered_state(layer, nb, NS, prev):
    st_in = pl.BlockSpec((None, NS, N_HEADS, DH, DH), lambda i, c: (layer, i, 0, 0, 0))
    shape = jax.ShapeDtypeStruct((DEPTH, nb, N_HEADS, DH, DH), F32)
    if layer == 0:
        return st_in, pl.BlockSpec((DEPTH, NS, N_HEADS, DH, DH), lambda i, c: (0, i, 0, 0, 0)), shape, [], []
    return st_in, st_in, shape, [pl.BlockSpec(memory_space=pl.ANY)], [prev]


def _fill_later_layers(ref, when):
    @pl.when(when)
    def _():
        for l in range(1, DEPTH):
            ref[l] = ref[0]


def _mlstm_prep_body(qkvo_ref, bch_ref, prev_ref, sa_ref, sb_ref, cv0_ref, cw_ref,
                     yb_ref, cv_ref, nv_ref, q_ref, kw_ref, v_ref, rows_ref, bl_ref, bc_ref, kn_ref, xp_s,
                     *, NB, G, tl, tlx, NCS, lv):
    cs = pl.program_id(1)
    n = NB * N_HEADS
    lvl = lv - (NCS - 1) * tlx

    if NCS > 1:
        @pl.when(cs == 0)
        def _():
            xp_s[:, 0:HIST, :] = cv0_ref[...]

        @pl.when(cs > 0)
        def _():
            xp_s[:, 0:HIST, :] = prev_ref[:, :, D_MIX:2 * D_MIX] * prev_ref[:, :, 2 * D_MIX:3 * D_MIX]
    else:
        xp_s[:, 0:HIST, :] = cv0_ref[...]
    xp_s[:, HIST:HIST + tlx, :] = bch_ref[:, :, D_MIX:2 * D_MIX] * bch_ref[:, :, 2 * D_MIX:3 * D_MIX]
    yb_ref[...] = (bch_ref[:, :, 0:D_MIX] * _conv_taps(xp_s, cw_ref, CONV_B, tlx)).astype(BF16)

    @pl.when(cs == NCS - 1)
    def _():
        cv_ref[...] = xp_s[:, pl.ds(HIST + lvl - (CONV_B - 1), CONV_B - 1), :]

    i_all = sa_ref[...].reshape(NB, RB, 128)
    f_all = _log_sigmoid(sb_ref[...].reshape(NB, RB, 128))
    if lv < NCS * tlx:
        assert NCS == 1
        valid = (lax.broadcasted_iota(jnp.int32, (NB, RB, 128), 1) % tl) < lv
        i_all = jnp.where(valid, i_all, NEG)
        f_all = jnp.where(valid, f_all, 0.0)
    _, _, incl = _block_masks(tl)
    bcum = _seq_cumsum(f_all, incl, NB)
    blast = _seq_last(bcum, NB, G, tl)
    val = _seq_rows(blast, NB, tl) - bcum + i_all
    bmax = jnp.max(val.reshape(NB * G, tl, 128), axis=1, keepdims=True)
    wk0 = jnp.exp(val - _seq_rows(bmax, NB, tl))

    qkvo = qkvo_ref[...].reshape(NB, RB, 4 * D_MIX)
    q = _heads(qkvo[:, :, 0:D_MIX], NB, DH)
    k = _heads(qkvo[:, :, D_MIX:2 * D_MIX], NB, DH) * (DH ** -0.5)
    v = _heads(qkvo[:, :, 2 * D_MIX:3 * D_MIX], NB, DH)
    qb, kb, vb = q.astype(BF16), k.astype(BF16), v.astype(BF16)

    b_col = _gate_cols(bcum, NB, 0)
    dlog = jnp.where(incl[None], b_col - _gate_rows(bcum, NB, 0) + _gate_rows(i_all, NB, 0), NEG)
    d = jnp.max(dlog, axis=-1, keepdims=True)
    s0 = _bmm_nt(qb, kb) * jnp.exp(dlog - d)
    kw0 = k * _gate_cols(wk0, NB, 0)

    _put_chains(nv_ref, _bmm(s0, vb), NB, G, tl)
    _put_chains(q_ref, qb, NB, G, tl)
    _put_chains(kw_ref, kw0.astype(BF16), NB, G, tl)
    _put_chains(v_ref, vb, NB, G, tl)
    rows = jnp.swapaxes(jnp.concatenate([d, b_col, jnp.sum(s0, axis=-1, keepdims=True),
                                         jnp.zeros((n, RB, 5), F32)], axis=-1), 1, 2).reshape(NB, N_HEADS, 8, RB)
    for h in range(N_HEADS):
        for g in range(G):
            rows_ref[:, g, h] = rows[:, h, :, g * tl:(g + 1) * tl]
    _put_seq_scalars(bl_ref, blast, NB, G, 0)
    _put_seq_scalars(bc_ref, bmax, NB, G, 0)
    kn = jnp.sum(kw0.reshape(n * G, tl, DH), axis=1, keepdims=True).reshape(NB, N_HEADS, G, 1, DH)
    for h in range(N_HEADS):
        kn_ref[:, :, h] = kn[:, h]


def _mlstm_prep(proj, cv0, cw, NB, G, tl, lv):
    nb, lp, _ = proj.shape
    gx, tlx, NI, NCS = _step_tiling(nb, lp, NB, G, tl)
    nbt = nb * lp // RB
    step = lambda i, c: (i * NCS + c, 0, 0, 0, 0)
    chain = lambda last, dt: jax.ShapeDtypeStruct((nbt, G, N_HEADS, tl, last), dt)
    cspec = lambda last: pl.BlockSpec((NB, G, N_HEADS, tl, last), step)
    scal = jax.ShapeDtypeStruct((nbt, G, N_HEADS, 1, 128), F32)
    sspec = pl.BlockSpec((NB, G, N_HEADS, 1, 128), step)
    bch = OFF_BCH // (3 * D_MIX)
    return pl.pallas_call(
        functools.partial(_mlstm_prep_body, NB=NB, G=G, tl=tl, tlx=tlx, NCS=NCS, lv=lv),
        grid=(NI, NCS),
        in_specs=[pl.BlockSpec((gx, tlx, 4 * D_MIX), lambda i, c: (i, c, OFF_QKVO // (4 * D_MIX))),
                  pl.BlockSpec((gx, tlx, 3 * D_MIX), lambda i, c: (i, c, bch)),
                  pl.BlockSpec((gx, HIST, 3 * D_MIX), lambda i, c: (i, jnp.maximum(c * (tlx // HIST) - 1, 0), bch)),
                  pl.BlockSpec((gx, tlx, 128), lambda i, c: (i, c, OFF_SA // 128)),
                  pl.BlockSpec((gx, tlx, 128), lambda i, c: (i, c, OFF_SB // 128)),
                  pl.BlockSpec((gx, HIST, D_MIX), lambda i, c: (i, 0, 0)),
                  pl.BlockSpec((8, D_MIX), lambda i, c: (0, 0))],
        out_specs=[pl.BlockSpec((gx, tlx, D_MIX), lambda i, c: (i, c, 0)),
                   pl.BlockSpec((gx, CONV_B - 1, D_MIX), lambda i, c: (i, 0, 0)),
                   cspec(DH), cspec(DH), cspec(DH), cspec(DH),
                   pl.BlockSpec((NB, G, N_HEADS, 8, tl), step), sspec, sspec, sspec],
        out_shape=[jax.ShapeDtypeStruct((nb, lp, D_MIX), BF16),
                   jax.ShapeDtypeStruct((nb, CONV_B - 1, D_MIX), F32),
                   chain(DH, F32), chain(DH, BF16), chain(DH, BF16), chain(DH, BF16),
                   jax.ShapeDtypeStruct((nbt, G, N_HEADS, 8, tl), F32),
                   scal, scal, scal],
        scratch_shapes=[pltpu.VMEM((gx, HIST + tlx, D_MIX), F32)],
        compiler_params=pltpu.CompilerParams(dimension_semantics=("arbitrary", "arbitrary"),
                                             vmem_limit_bytes=VMEM_LIMIT),
        name=f"mlstm_prep_g{G}",
    )(proj, proj, proj, proj, proj, cv0, cw)


def _mlstm_scan_body(nv_ref, q_ref, kw_ref, v_ref, rows_ref, bl_ref, bc_ref, kn_ref, o_ref, C0_ref, n0_ref, m0_ref,
                     ng_ref, *rest, NS, tl, NC, first):
    ya_ref, C_out, n_ref, m_ref = rest[-4:]
    C_ref = C_out.at[0] if first else C_out
    c = pl.program_id(1)
    n = NS * N_HEADS

    @pl.when(c == 0)
    def _():
        C_ref[...] = C0_ref[...]
        n_ref[...] = n0_ref[...]
        m_ref[...] = m0_ref[...]

    chains = lambda ref: ref[...].reshape(n, ref.shape[-2], ref.shape[-1])
    C = C_ref[...].reshape(n, DH, DH)
    nvec = n_ref[...].reshape(n, 1, DH)
    m_prev = chains(m_ref)[:, :, 0:1]
    rows = chains(rows_ref)
    d, b, ds0 = rows[:, 0:1, :], rows[:, 1:2, :], rows[:, 2:3, :]
    qb = chains(q_ref)

    m_inter = b + m_prev
    m_t = jnp.maximum(m_inter, d)
    f = jnp.exp(d - m_t)
    inter = jnp.exp(m_inter - m_t)
    qn = jnp.einsum('nod,ntd->not', nvec.astype(BF16), qb, preferred_element_type=F32)
    den = f * ds0 + inter * qn
    scale_t = 1.0 / jnp.maximum(jnp.abs(den), jnp.exp(-m_t))
    fi = jnp.swapaxes(jnp.concatenate([f * scale_t, inter * scale_t, jnp.zeros((n, 6, tl), F32)], axis=1), 1, 2)
    qC = jnp.einsum('ntd,nde->nte', qb, C.astype(BF16), preferred_element_type=F32)
    hh = fi[:, :, 0:1] * chains(nv_ref) + fi[:, :, 1:2] * qC

    m_new = m_t[:, :, tl - 1:tl]
    decay = jnp.exp(chains(bl_ref)[:, :, 0:1] + m_prev - m_new)
    scale = jnp.exp(chains(bc_ref)[:, :, 0:1] - m_new)
    kv = jnp.einsum('ntd,nte->nde', chains(kw_ref), chains(v_ref), preferred_element_type=F32)
    C_ref[...] = (decay * C + scale * kv).reshape(NS, N_HEADS, DH, DH)
    n_ref[...] = (decay * nvec + scale * chains(kn_ref)).reshape(NS, N_HEADS, DH)
    m_ref[...] = jnp.broadcast_to(m_new, (n, 1, 128)).reshape(NS, N_HEADS, 1, 128)
    if first:
        _fill_later_layers(C_out, c == NC - 1)

    mu = jnp.mean(hh, axis=-1, keepdims=True)
    hc = hh - mu
    hn = (hc * lax.rsqrt(jnp.mean(hc * hc, axis=-1, keepdims=True) + LN_EPS)).reshape(NS, N_HEADS, tl, DH)
    for h in range(N_HEADS):
        hs = slice(h * DH, (h + 1) * DH)
        ya_ref[:, :, hs] = (_sigmoid(o_ref[:, :, hs]) * hn[:, h] * ng_ref[:, hs]).astype(BF16)


def _mlstm_scan(pre, proj, C0, n0, m0, ng, SB, G, tl, layer, C_prev):
    nb, lp, _ = proj.shape
    NS = SB * G
    NI, NC = nb // NS, lp // tl
    six = lambda a: a.reshape((NI * SB, NC) + a.shape[1:])
    spec6 = lambda a: pl.BlockSpec((SB, 1) + a.shape[1:], lambda i, c: (i, c, 0, 0, 0, 0))
    seq4 = lambda i, c: (i, 0, 0, 0)
    st_in, st_out, st_shape, extra_in, extra_args = _layered_state(layer, nb, NS, C_prev)
    n_in = len(pre) + 5
    return pl.pallas_call(
        functools.partial(_mlstm_scan_body, NS=NS, tl=tl, NC=NC, first=layer == 0),
        grid=(NI, NC),
        in_specs=[spec6(a) for a in pre] + [
            pl.BlockSpec((NS, tl, D_MIX), lambda i, c: (i, c, (OFF_QKVO + 3 * D_MIX) // D_MIX)),
            st_in,
            pl.BlockSpec((NS, N_HEADS, DH), lambda i, c: (i, 0, 0)),
            pl.BlockSpec((NS, N_HEADS, 1, 128), seq4),
            pl.BlockSpec((1, D_MIX), lambda i, c: (0, 0))] + extra_in,
        out_specs=[pl.BlockSpec((NS, tl, D_MIX), lambda i, c: (i, c, 0)),
                   st_out,
                   pl.BlockSpec((NS, N_HEADS, DH), lambda i, c: (i, 0, 0)),
                   pl.BlockSpec((NS, N_HEADS, 1, 128), seq4)],
        out_shape=[jax.ShapeDtypeStruct((nb, lp, D_MIX), BF16),
                   st_shape,
                   jax.ShapeDtypeStruct((nb, N_HEADS, DH), F32),
                   jax.ShapeDtypeStruct((nb, N_HEADS, 1, 128), F32)],
        input_output_aliases={n_in: 1} if extra_in else {},
        compiler_params=pltpu.CompilerParams(dimension_semantics=("arbitrary", "arbitrary"),
                                             vmem_limit_bytes=VMEM_LIMIT),
        name=f"mlstm_scan_g{G}",
    )(*[six(a) for a in pre], proj, C0, n0, m0, ng, *extra_args)


def _unit_lower_inverse(n, rr, cc, tl):
    eye = (rr == cc).astype(F32)[None]
    p = jnp.where(((rr // INV_BLOCK) == (cc // INV_BLOCK))[None], n, 0.0)
    x = eye + p
    b = 2
    while b < INV_BLOCK:
        p = _bmm(p, p)
        x = x + _bmm(x, p)
        b *= 2
    b = INV_BLOCK
    while b < tl:
        off = jnp.where((((rr // (2 * b)) == (cc // (2 * b))) & ((rr // b) != (cc // b)))[None], n, 0.0)
        x = x + _bmm(x, _bmm(off, x))
        b *= 2
    return x


def _gdn_prep_body(x_ref, prev_ref, sa_ref, sb_ref, gc0_ref, cw_ref, alog_ref, dtb_ref,
                   u_ref, w_ref, qg_ref, kd_ref, qkm_ref, eg_ref, gcs_ref, xp_s, *, NB, G, tl, tlx, NCS, lv):
    cs = pl.program_id(1)
    lvl = lv - (NCS - 1) * tlx

    _conv_history(xp_s, gc0_ref, prev_ref, NCS)
    xp_s[:, HIST:HIST + tlx, :] = x_ref[...]
    qkv = _silu(_conv_taps(xp_s, cw_ref, CONV_C, tlx)).reshape(NB, RB, 3 * D_MIX)

    @pl.when(cs == NCS - 1)
    def _():
        gcs_ref[...] = xp_s[:, pl.ds(HIST + lvl - (CONV_C - 1), CONV_C - 1), :]

    beta_all = _sigmoid(sa_ref[...].reshape(NB, RB, 128))
    g_all = -jnp.exp(alog_ref[...]) * _softplus(sb_ref[...].reshape(NB, RB, 128) + dtb_ref[...])
    if lv < NCS * tlx:
        assert NCS == 1
        valid = (lax.broadcasted_iota(jnp.int32, (NB, RB, 128), 1) % tl) < lv
        beta_all = jnp.where(valid, beta_all, 0.0)
        g_all = jnp.where(valid, g_all, 0.0)
    rr, cc, incl = _block_masks(tl)
    diag = rr == cc
    gam = _seq_cumsum(g_all, incl, NB)
    glast = _seq_last(gam, NB, G, tl)
    gcol = _gate_cols(gam, NB, 4)
    bcol = _gate_cols(beta_all, NB, 4)
    egcol = _gate_cols(jnp.exp(gam), NB, 4)
    kdcol = _gate_cols(jnp.exp(_seq_rows(glast, NB, tl) - gam), NB, 4)

    q = _heads(qkv[:, :, 0:D_MIX], NB, DH)
    k = _heads(qkv[:, :, D_MIX:2 * D_MIX], NB, DH)
    v = _heads(qkv[:, :, 2 * D_MIX:3 * D_MIX], NB, DH)
    q = q * lax.rsqrt(jnp.sum(q * q, axis=-1, keepdims=True) + NORM_EPS) * (DH ** -0.5)
    k = k * lax.rsqrt(jnp.sum(k * k, axis=-1, keepdims=True) + NORM_EPS)
    qb, kb = q.astype(BF16), k.astype(BF16)

    dmat = jnp.exp(jnp.where(incl[None], gcol - _gate_rows(gam, NB, 4), NEG))
    nmat = jnp.where(diag[None], 0.0, -(bcol * _bmm_nt(kb, kb) * dmat))
    rhs = jnp.concatenate([bcol * v, (bcol * egcol) * k], axis=-1)
    sol = _bmm(_unit_lower_inverse(nmat, rr, cc, tl), rhs)
    qkm = (_bmm_nt(qb, kb) * dmat).astype(BF16).reshape(NB, N_HEADS, RB, RB)

    _put_chains(u_ref, sol[:, :, 0:DH], NB, G, tl)
    _put_chains(w_ref, sol[:, :, DH:2 * DH].astype(BF16), NB, G, tl)
    _put_chains(qg_ref, (q * egcol).astype(BF16), NB, G, tl)
    _put_chains(kd_ref, (k * kdcol).astype(BF16), NB, G, tl)
    _put_seq_scalars(eg_ref, jnp.exp(glast), NB, G, 4)
    for h in range(N_HEADS):
        for g in range(G):
            qkm_ref[:, g, h] = qkm[:, h, g * tl:(g + 1) * tl, g * tl:(g + 1) * tl]


def _gdn_prep(proj, gc0, cw, alog, dtb, NB, G, tl, lv):
    nb, lp, _ = proj.shape
    assert tl % INV_BLOCK == 0 and (tl // INV_BLOCK) & (tl // INV_BLOCK - 1) == 0
    gx, tlx, NI, NCS = _step_tiling(nb, lp, NB, G, tl)
    nbt = nb * lp // RB
    step = lambda i, c: (i * NCS + c, 0, 0, 0, 0)
    par = lambda i, c: (0, 0)
    chain = lambda last, dt: jax.ShapeDtypeStruct((nbt, G, N_HEADS, tl, last), dt)
    cspec = lambda last: pl.BlockSpec((NB, G, N_HEADS, tl, last), step)
    return pl.pallas_call(
        functools.partial(_gdn_prep_body, NB=NB, G=G, tl=tl, tlx=tlx, NCS=NCS, lv=lv),
        grid=(NI, NCS),
        in_specs=[pl.BlockSpec((gx, tlx, 3 * D_MIX), lambda i, c: (i, c, OFF_QKVC // (3 * D_MIX))),
                  pl.BlockSpec((gx, HIST, 3 * D_MIX),
                               lambda i, c: (i, jnp.maximum(c * (tlx // HIST) - 1, 0), OFF_QKVC // (3 * D_MIX))),
                  pl.BlockSpec((gx, tlx, 128), lambda i, c: (i, c, OFF_SA // 128)),
                  pl.BlockSpec((gx, tlx, 128), lambda i, c: (i, c, OFF_SB // 128)),
                  pl.BlockSpec((gx, HIST, 3 * D_MIX), lambda i, c: (i, 0, 0)),
                  pl.BlockSpec((8, 3 * D_MIX), par),
                  pl.BlockSpec((1, 128), par),
                  pl.BlockSpec((1, 128), par)],
        out_specs=[cspec(DH), cspec(DH), cspec(DH), cspec(DH), cspec(tl),
                   pl.BlockSpec((NB, G, N_HEADS, 1, 128), step),
                   pl.BlockSpec((gx, CONV_C - 1, 3 * D_MIX), lambda i, c: (i, 0, 0))],
        out_shape=[chain(DH, F32), chain(DH, BF16), chain(DH, BF16), chain(DH, BF16), chain(tl, BF16),
                   jax.ShapeDtypeStruct((nbt, G, N_HEADS, 1, 128), F32),
                   jax.ShapeDtypeStruct((nb, CONV_C - 1, 3 * D_MIX), F32)],
        scratch_shapes=[pltpu.VMEM((gx, HIST + tlx, 3 * D_MIX), F32)],
        compiler_params=pltpu.CompilerParams(dimension_semantics=("arbitrary", "arbitrary"),
                                             vmem_limit_bytes=VMEM_LIMIT),
        name=f"gdn_prep_g{G}",
    )(proj, proj, proj, proj, gc0, cw, alog, dtb)


def _gdn_scan_body(u_ref, w_ref, qg_ref, kd_ref, qkm_ref, eg_ref, z_ref, S0_ref, gng_ref, *rest,
                   NS, tl, NC, first):
    yc_ref, S_out = rest[-2:]
    S_ref = S_out.at[0] if first else S_out
    c = pl.program_id(1)
    n = NS * N_HEADS

    @pl.when(c == 0)
    def _():
        S_ref[...] = S0_ref[...]

    S = S_ref[...].reshape(n, DH, DH)
    Sb = S.astype(BF16)
    chains = lambda ref: ref[...].reshape(n, tl, ref.shape[-1])
    v_new = chains(u_ref) - jnp.einsum('ntd,nde->nte', chains(w_ref), Sb, preferred_element_type=F32)
    vnb = v_new.astype(BF16)
    o = (jnp.einsum('ntd,nde->nte', chains(qg_ref), Sb, preferred_element_type=F32)
         + jnp.einsum('nts,nse->nte', chains(qkm_ref), vnb, preferred_element_type=F32))
    eg = eg_ref[...].reshape(n, 1, 128)[:, :, 0:1]
    S_new = eg * S + jnp.einsum('ntd,nte->nde', chains(kd_ref), vnb, preferred_element_type=F32)
    S_ref[...] = S_new.reshape(NS, N_HEADS, DH, DH)
    if first:
        _fill_later_layers(S_out, c == NC - 1)

    on = (o * lax.rsqrt(jnp.mean(o * o, axis=-1, keepdims=True) + NORM_EPS) * gng_ref[...]).reshape(NS, N_HEADS, tl, DH)
    for h in range(N_HEADS):
        yc_ref[:, :, h * DH:(h + 1) * DH] = (on[:, h] * _silu(z_ref[:, :, h * DH:(h + 1) * DH])).astype(BF16)


def _gdn_scan(pre, proj, S0, gng, SB, G, tl, layer, S_prev):
    nb, lp, _ = proj.shape
    NS = SB * G
    NI, NC = nb // NS, lp // tl
    six = lambda a: a.reshape((NI * SB, NC) + a.shape[1:])
    cspec = lambda last: pl.BlockSpec((SB, 1, G, N_HEADS, tl, last), lambda i, c: (i, c, 0, 0, 0, 0))
    u, w, qg, kd, qkm, eg = (six(a) for a in pre)
    st_in, st_out, st_shape, extra_in, extra_args = _layered_state(layer, nb, NS, S_prev)
    return pl.pallas_call(
        functools.partial(_gdn_scan_body, NS=NS, tl=tl, NC=NC, first=layer == 0),
        grid=(NI, NC),
        in_specs=[cspec(DH), cspec(DH), cspec(DH), cspec(DH), cspec(tl),
                  pl.BlockSpec((SB, 1, G, N_HEADS, 1, 128), lambda i, c: (i, c, 0, 0, 0, 0)),
                  pl.BlockSpec((NS, tl, D_MIX), lambda i, c: (i, c, OFF_Z // D_MIX)),
                  st_in,
                  pl.BlockSpec((1, DH), lambda i, c: (0, 0))] + extra_in,
        out_specs=[pl.BlockSpec((NS, tl, D_MIX), lambda i, c: (i, c, 0)), st_out],
        out_shape=[jax.ShapeDtypeStruct((nb, lp, D_MIX), BF16), st_shape],
        input_output_aliases={9: 1} if extra_in else {},
        compiler_params=pltpu.CompilerParams(dimension_semantics=("arbitrary", "arbitrary"),
                                             vmem_limit_bytes=VMEM_LIMIT),
        name=f"gdn_scan_g{G}",
    )(u, w, qg, kd, qkm, eg, proj, S0, gng, *extra_args)


def _route(rl):
    lane = lax.broadcasted_iota(jnp.int32, rl.shape, 1).astype(F32)
    is_g = (lane >= N_EXPERTS) & (lane < N_EXPERTS + N_GROUPS)
    gl = jnp.where(is_g, rl, NEG)
    gmax = jnp.max(gl, axis=-1, keepdims=True)
    grp = jnp.min(jnp.where(gl == gmax, lane - N_EXPERTS, 4.0 * N_EXPERTS), axis=-1, keepdims=True)
    p_grp = 1.0 / jnp.sum(jnp.where(is_g, jnp.exp(gl - gmax), 0.0), axis=-1, keepdims=True)
    lo = grp * EXPERTS_PER_GROUP
    in_grp = (lane >= lo) & (lane < lo + EXPERTS_PER_GROUP)
    el = jnp.where(in_grp, rl, NEG)
    m1 = jnp.max(el, axis=-1, keepdims=True)
    i1 = jnp.min(jnp.where(el == m1, lane, 4.0 * N_EXPERTS), axis=-1, keepdims=True)
    el2 = jnp.where(lane == i1, NEG, el)
    m2 = jnp.max(el2, axis=-1, keepdims=True)
    i2 = jnp.min(jnp.where(el2 == m2, lane, 4.0 * N_EXPERTS), axis=-1, keepdims=True)
    e2 = jnp.exp(m2 - m1)
    w1 = p_grp / (1.0 + e2)
    w2 = p_grp * e2 / (1.0 + e2)
    return i1, i2, w1, w2


def _merge_body(ya_ref, yb_ref, yc_ref, ga_ref, gb_ref, gc_ref, x_ref, mod_ref, wa_ref, wb_ref, wc_ref, wo_ref,
                lg_ref, lb_ref, wrh_ref, wrl_ref, br_ref, x1_ref, rinfo_ref, cnt_ref, *, bt, lt):
    tm = bt * lt

    @pl.when((pl.program_id(0) == 0) & (pl.program_id(1) == 0))
    def _():
        cnt_ref[...] = jnp.zeros_like(cnt_ref)

    def r2(ref):
        return ref[...].reshape(tm, ref.shape[-1])

    merged = (_sigmoid(r2(ga_ref)) * _dot(r2(ya_ref), wa_ref[0])
              + _sigmoid(r2(gb_ref)) * _dot(r2(yb_ref), wb_ref[0])
              + _sigmoid(r2(gc_ref)) * _dot(r2(yc_ref), wc_ref[0]))
    out = _dot(merged.astype(BF16), wo_ref[0])
    y = DN_ALPHA * x_ref[...] + (1.0 + mod_ref[:, 2:3, :]) * out.reshape(bt, lt, D_MODEL)
    x1 = _layer_norm(y, lg_ref[0], lb_ref[0])
    x1_ref[...] = x1
    u2 = (x1 * (1.0 + mod_ref[:, 4:5, :]) + mod_ref[:, 3:4, :]).reshape(tm, D_MODEL)
    hi = u2.astype(BF16)
    lo = (u2 - hi.astype(F32)).astype(BF16)
    rl = _dot(hi, wrh_ref[0]) + _dot(lo, wrh_ref[0]) + _dot(hi, wrl_ref[0]) + br_ref[0]
    i1, i2, w1, w2 = _route(rl)
    lane = lax.broadcasted_iota(jnp.int32, (tm, 128), 1).astype(F32)
    onehot = jnp.where((lane == i1) | (lane == i2), 1.0, 0.0)
    rr = lax.broadcasted_iota(jnp.int32, (tm, tm), 0)
    cc = lax.broadcasted_iota(jnp.int32, (tm, tm), 1)
    before = _dot((rr > cc).astype(BF16), onehot.astype(BF16)) + cnt_ref[0:1, :]
    rank1 = jnp.sum(jnp.where(lane == i1, before, 0.0), axis=-1, keepdims=True)
    rank2 = jnp.sum(jnp.where(lane == i2, before, 0.0), axis=-1, keepdims=True)
    cnt_ref[0:1, :] += jnp.sum(onehot, axis=0, keepdims=True)
    rinfo = jnp.zeros((tm, 128), F32)
    for k, val in enumerate((i1, i2, w1, w2, rank1, rank2)):
        rinfo = jnp.where(lane == k, val, rinfo)
    rinfo_ref[...] = rinfo.reshape(bt, lt, 128)


def _merge(ya, yb, yc, proj, x, mod, wa, wb, wc, wo, lg, lb, wrh, wrl, br, layer, bt, lt):
    nb, lp, _ = x.shape
    tok = lambda i, t: (i, t, 0)
    wsp = lambda shape: pl.BlockSpec((1,) + shape, lambda i, t: (layer, 0, 0))
    g0 = OFF_GTS // D_MODEL
    return pl.pallas_call(
        functools.partial(_merge_body, bt=bt, lt=lt),
        grid=(nb // bt, lp // lt),
        in_specs=[pl.BlockSpec((bt, lt, D_MIX), tok),
                  pl.BlockSpec((bt, lt, D_MIX), tok),
                  pl.BlockSpec((bt, lt, D_MIX), tok),
                  pl.BlockSpec((bt, lt, D_MODEL), lambda i, t: (i, t, g0)),
                  pl.BlockSpec((bt, lt, D_MODEL), lambda i, t: (i, t, g0 + 1)),
                  pl.BlockSpec((bt, lt, D_MODEL), lambda i, t: (i, t, g0 + 2)),
                  pl.BlockSpec((bt, lt, D_MODEL), tok),
                  pl.BlockSpec((bt, 6, D_MODEL), lambda i, t: (i, 0, 0)),
                  wsp((D_MIX, D_MODEL)), wsp((D_MIX, D_MODEL)), wsp((D_MIX, D_MODEL)), wsp((D_MODEL, D_MODEL)),
                  wsp((1, D_MODEL)), wsp((1, D_MODEL)),
                  wsp((D_MODEL, 128)), wsp((D_MODEL, 128)), wsp((1, 128))],
        out_specs=[pl.BlockSpec((bt, lt, D_MODEL), tok),
                   pl.BlockSpec((bt, lt, 128), tok),
                   pl.BlockSpec((8, 128), lambda i, t: (0, 0))],
        out_shape=[jax.ShapeDtypeStruct((nb, lp, D_MODEL), F32),
                   jax.ShapeDtypeStruct((nb, lp, 128), F32),
                   jax.ShapeDtypeStruct((8, 128), F32)],
        compiler_params=pltpu.CompilerParams(dimension_semantics=("arbitrary", "arbitrary"),
                                             vmem_limit_bytes=VMEM_LIMIT),
        name=f"merge_l{layer}_b{bt}",
    )(ya, yb, yc, proj, proj, proj, x, mod, wa, wb, wc, wo, lg, lb, wrh, wrl, br)


def _moe_plan(rinfo, cnt, tm, mb):
    nb, lp, _ = rinfo.shape
    n_tok = nb * lp
    n_blocks = 2 * n_tok // mb + N_EXPERTS
    counts = cnt[0, :N_EXPERTS].astype(jnp.int32)
    nblk = (counts + mb - 1) // mb
    pend = jnp.cumsum(nblk)
    pstart = pend - nblk
    expert = rinfo[..., 0:2].astype(jnp.int32)
    rank = rinfo[..., 4:6].astype(jnp.int32)
    ids = jnp.arange(N_EXPERTS, dtype=jnp.int32)
    first_row = jnp.sum(jnp.where(expert[..., None] == ids, pstart * mb, 0), axis=-1)
    pos = (first_row + rank).reshape(n_tok // tm, 1, 2 * tm)
    block_e = jnp.sum(pend[None, :] <= jnp.arange(n_blocks, dtype=jnp.int32)[:, None], axis=1)
    block_e = jnp.minimum(block_e, N_EXPERTS - 1).astype(jnp.int32)
    n_used = pend[N_EXPERTS - 1:].astype(jnp.int32)
    ztail = jnp.where(nblk > 0, (pend - 1) * mb, -1).astype(jnp.int32)
    ztail = jnp.concatenate([ztail, n_used]).reshape(1, 1, N_EXPERTS + 1)
    return pos, block_e, n_used, ztail, n_blocks


def _row_copy(src, dst, sem):
    return pltpu.make_async_copy(src, dst, sem)


def _dispatch_body(pos_ref, ztail_ref, x1_ref, mod_ref, disp_ref, u_scr, z_scr, sem, zsem, *, bt, lt, n_blocks, mb):
    tm = bt * lt

    @pl.when((pl.program_id(0) == 0) & (pl.program_id(1) == 0))
    def _():
        z_scr[...] = jnp.zeros_like(z_scr)
        for e in range(N_EXPERTS):
            @pl.when(ztail_ref[0, 0, e] >= 0)
            def _():
                row = pl.multiple_of(ztail_ref[0, 0, e], mb)
                _row_copy(z_scr, disp_ref.at[pl.ds(row, mb)], zsem).start()
        def unused(j):
            return _row_copy(z_scr, disp_ref.at[pl.ds(pl.multiple_of(j * mb, mb), mb)], zsem)

        n_used = ztail_ref[0, 0, N_EXPERTS]
        lax.fori_loop(n_used, n_blocks, lambda j, c: (unused(j).start(), c)[1], 0)
        lax.fori_loop(n_used, n_blocks, lambda j, c: (unused(j).wait(), c)[1], 0)
        for e in range(N_EXPERTS):
            @pl.when(ztail_ref[0, 0, e] >= 0)
            def _():
                row = pl.multiple_of(ztail_ref[0, 0, e], mb)
                _row_copy(z_scr, disp_ref.at[pl.ds(row, mb)], zsem).wait()

    u2 = x1_ref[...] * (1.0 + mod_ref[:, 4:5, :]) + mod_ref[:, 3:4, :]
    u_scr[...] = u2.reshape(tm // 8, 8, D_MODEL)

    def issue(i, carry):
        for r in range(8):
            for k in range(2):
                row = pos_ref[0, 0, 16 * i + 2 * r + k]
                _row_copy(u_scr.at[i, pl.ds(r, 1)], disp_ref.at[pl.ds(row, 1)], sem).start(priority=k)
        return carry

    lax.fori_loop(0, tm // 8, issue, 0)
    for k in range(2):
        _row_copy(disp_ref.at[pl.ds(0, tm)], disp_ref.at[pl.ds(0, tm)], sem).wait()


def _dispatch(x1, mod, pos, ztail, n_blocks, mb, bt, lt):
    nb, lp, _ = x1.shape
    nt = lp // lt
    return pl.pallas_call(
        functools.partial(_dispatch_body, bt=bt, lt=lt, n_blocks=n_blocks, mb=mb),
        grid=(nb // bt, nt),
        in_specs=[pl.BlockSpec((1, 1, 2 * bt * lt), lambda i, t: (i * nt + t, 0, 0), memory_space=pltpu.SMEM),
                  pl.BlockSpec((1, 1, N_EXPERTS + 1), lambda i, t: (0, 0, 0), memory_space=pltpu.SMEM),
                  pl.BlockSpec((bt, lt, D_MODEL), lambda i, t: (i, t, 0)),
                  pl.BlockSpec((bt, 6, D_MODEL), lambda i, t: (i, 0, 0))],
        out_specs=pl.BlockSpec(memory_space=pl.ANY),
        out_shape=jax.ShapeDtypeStruct((n_blocks * mb, D_MODEL), F32),
        scratch_shapes=[pltpu.VMEM((bt * lt // 8, 8, D_MODEL), F32), pltpu.VMEM((mb, D_MODEL), F32),
                        pltpu.SemaphoreType.DMA, pltpu.SemaphoreType.DMA],
        compiler_params=pltpu.CompilerParams(dimension_semantics=("arbitrary", "arbitrary"),
                                             vmem_limit_bytes=VMEM_LIMIT),
        name=f"dispatch_b{bt}",
    )(pos, ztail, x1, mod)


def _experts_body(be_ref, nu_ref, x_ref, wg_ref, wu_ref, wd_ref, o_ref, wg_s, wu_s, wd_s):
    j = pl.program_id(0)

    @pl.when((j == 0) | (be_ref[j] != be_ref[jnp.maximum(j - 1, 0)]))
    def _():
        wg_s[...] = wg_ref[0, 0].astype(BF16)
        wu_s[...] = wu_ref[0, 0].astype(BF16)
        wd_s[...] = wd_ref[0, 0].astype(BF16)

    @pl.when(j < nu_ref[0])
    def _():
        x = x_ref[...].astype(BF16)
        hb = _silu(_dot(x, wg_s[...])) * _dot(x, wu_s[...])
        o_ref[...] = _dot(hb.astype(BF16), wd_s[...])

    @pl.when(j >= nu_ref[0])
    def _():
        o_ref[...] = jnp.zeros_like(o_ref)


def _experts(disp, block_e, n_used, wg, wu, wd, layer, mb):
    n_blocks = disp.shape[0] // mb
    wmap = lambda j, be, nu: (layer, be[j], 0, 0)
    return pl.pallas_call(
        _experts_body,
        grid_spec=pltpu.PrefetchScalarGridSpec(
            num_scalar_prefetch=2,
            grid=(n_blocks,),
            in_specs=[pl.BlockSpec((mb, D_MODEL), lambda j, be, nu: (jnp.minimum(j, nu[0] - 1), 0)),
                      pl.BlockSpec((1, 1, D_MODEL, D_EXPERT), wmap),
                      pl.BlockSpec((1, 1, D_MODEL, D_EXPERT), wmap),
                      pl.BlockSpec((1, 1, D_EXPERT, D_MODEL), wmap)],
            out_specs=pl.BlockSpec((mb, D_MODEL), lambda j, be, nu: (j, 0)),
            scratch_shapes=[pltpu.VMEM((D_MODEL, D_EXPERT), BF16), pltpu.VMEM((D_MODEL, D_EXPERT), BF16),
                            pltpu.VMEM((D_EXPERT, D_MODEL), BF16)]),
        out_shape=jax.ShapeDtypeStruct(disp.shape, F32),
        compiler_params=pltpu.CompilerParams(dimension_semantics=("arbitrary",), vmem_limit_bytes=VMEM_LIMIT),
        name=f"experts_l{layer}_n{n_blocks}",
    )(block_e, n_used, disp, wg, wu, wd)


def _combine_body(pos_ref, x1_ref, mod_ref, rinfo_ref, eo_ref, lg_ref, lb_ref, x2_ref, r_scr, sem, *, bt, lt):
    tm = bt * lt

    def issue(i, carry):
        for r in range(8):
            for k in range(2):
                row = pos_ref[0, 0, 16 * i + 2 * r + k]
                _row_copy(eo_ref.at[pl.ds(row, 1)], r_scr.at[k, i, pl.ds(r, 1)], sem).start(priority=k)
        return carry

    lax.fori_loop(0, tm // 8, issue, 0)
    for k in range(2):
        _row_copy(eo_ref.at[pl.ds(0, tm)], eo_ref.at[pl.ds(0, tm)], sem).wait()

    rinfo = rinfo_ref[...].reshape(tm, 128)
    moe = (r_scr[0].reshape(tm, D_MODEL) * rinfo[:, 2:3] + r_scr[1].reshape(tm, D_MODEL) * rinfo[:, 3:4])
    y = DN_ALPHA * x1_ref[...] + (1.0 + mod_ref[:, 5:6, :]) * moe.reshape(bt, lt, D_MODEL)
    x2_ref[...] = _layer_norm(y, lg_ref[0], lb_ref[0])


def _combine(x1, mod, rinfo, eo, pos, lg, lb, layer, bt, lt):
    nb, lp, _ = x1.shape
    nt = lp // lt
    tok = lambda i, t: (i, t, 0)
    return pl.pallas_call(
        functools.partial(_combine_body, bt=bt, lt=lt),
        grid=(nb // bt, nt),
        in_specs=[pl.BlockSpec((1, 1, 2 * bt * lt), lambda i, t: (i * nt + t, 0, 0), memory_space=pltpu.SMEM),
                  pl.BlockSpec((bt, lt, D_MODEL), tok),
                  pl.BlockSpec((bt, 6, D_MODEL), lambda i, t: (i, 0, 0)),
                  pl.BlockSpec((bt, lt, 128), tok),
                  pl.BlockSpec(memory_space=pl.ANY),
                  pl.BlockSpec((1, 1, D_MODEL), lambda i, t: (layer, 0, 0)),
                  pl.BlockSpec((1, 1, D_MODEL), lambda i, t: (layer, 0, 0))],
        out_specs=pl.BlockSpec((bt, lt, D_MODEL), tok),
        out_shape=jax.ShapeDtypeStruct((nb, lp, D_MODEL), F32),
        scratch_shapes=[pltpu.VMEM((2, bt * lt // 8, 8, D_MODEL), F32), pltpu.SemaphoreType.DMA],
        compiler_params=pltpu.CompilerParams(dimension_semantics=("arbitrary", "arbitrary"),
                                             vmem_limit_bytes=VMEM_LIMIT),
        name=f"combine_l{layer}_b{bt}",
    )(pos, x1, mod, rinfo, eo, lg, lb)


def _hist(state):
    return jnp.pad(state, ((0, 0), (HIST - state.shape[1], 0), (0, 0)))


def _trunk(x, mod, st, p, bt, lt, lv, mix, moe_block, ip_tile):
    nb = x.shape[0]
    new = {key: [] for key in ('n', 'm', 'conv', 'gconv')}
    C = S = None
    for l in range(DEPTH):
        proj, x = _inproj(x, mod[l], p['ln_in_g'], p['ln_in_b'], p['w_in_r'], p['b_in_r'], l, l == 0, *ip_tile)
        m0 = jnp.broadcast_to(st['m'][l][:, :, None, None], (nb, N_HEADS, 1, 128))
        yb, conv, *pre = _mlstm_prep(proj, _hist(st['conv'][l]), p['conv_b_w8'][l], mix['NB'], mix['G'], mix['tl'], lv)
        ya, C, n, m = _mlstm_scan(pre, proj, st['C'], st['n'][l], m0, p['mlstm_norm_g'][l:l + 1],
                                  mix['SB'], mix['G'], mix['tl'], l, C)
        *pre, gconv = _gdn_prep(proj, _hist(st['gconv'][l]), p['conv_c_w8'][l], p['alog_row'][l], p['dtb_row'][l],
                                mix['NB'], mix['G'], mix['tl'], lv)
        yc, S = _gdn_scan(pre, proj, st['S'], p['gdn_norm_g'][l:l + 1], mix['SB'], mix['G'], mix['tl'], l, S)
        x1, rinfo, cnt = _merge(ya, yb, yc, proj, x, mod[l], p['w_br_a'], p['w_br_b'], p['w_br_c'], p['w_out'],
                                p['ln1_g'], p['ln1_b'], p['wr_hi'], p['wr_lo'], p['br'], l, bt, lt)
        pos, block_e, n_used, ztail, n_blocks = _moe_plan(rinfo, cnt, bt * lt, moe_block)
        disp = _dispatch(x1, mod[l], pos, ztail, n_blocks, moe_block, bt, lt)
        eo = _experts(disp, block_e, n_used, p['exp_w_gate'], p['exp_w_up'], p['exp_w_down'], l, moe_block)
        x = _combine(x1, mod[l], rinfo, eo, pos, p['ln2_g'], p['ln2_b'], l, bt, lt)
        new['n'].append(n)
        new['m'].append(m[:, :, 0, 0])
        new['conv'].append(conv)
        new['gconv'].append(gconv)
    return x, dict({key: jnp.stack(val) for key, val in new.items()}, C=C, S=S)


def kernel(x_prompt, x_sample, state_mlstm_C, state_mlstm_n, state_mlstm_m, state_conv, state_gdn_S, state_gdn_conv, c_prompt, c_sample, ln_in_g, ln_in_b, w_ada, b_ada, w_in, b_in, mlstm_norm_g, conv_b_w, conv_c_w, gdn_a_log, gdn_dt_bias, gdn_norm_g, w_br_a, w_br_b, w_br_c, w_out, ln1_g, ln1_b, router_g_w, router_g_b, router_e_w, router_e_b, exp_w_gate, exp_w_up, exp_w_down, ln2_g, ln2_b):
    nbp, lp, _ = x_prompt.shape
    nbs, ls, _ = x_sample.shape
    lsp = 8

    def regroup(a):
        out = jnp.zeros(a.shape[:-1] + (N_PROJ,), a.dtype)
        for src, end, dst in PROJ_SEGMENTS:
            out = out.at[..., dst:dst + end - src].set(a[..., src:end])
        return out

    wr = jnp.concatenate([router_e_w, router_g_w, jnp.zeros((DEPTH, D_MODEL, 128 - N_EXPERTS - N_GROUPS), F32)], axis=-1)
    wr_hi = wr.astype(BF16)
    lane_pad = lambda a: jnp.pad(a, ((0, 0), (4, 128 - 4 - N_HEADS)))[:, None, :]
    p = dict(
        ln_in_g=ln_in_g.reshape(1, D_MODEL), ln_in_b=ln_in_b.reshape(1, D_MODEL),
        w_in_r=_regroup_w_in(w_in), b_in_r=regroup(b_in).reshape(DEPTH, 1, N_PROJ),
        mlstm_norm_g=mlstm_norm_g,
        conv_b_w8=jnp.pad(conv_b_w, ((0, 0), (0, 8 - CONV_B), (0, 0))),
        conv_c_w8=jnp.pad(conv_c_w, ((0, 0), (0, 8 - CONV_C), (0, 0))),
        alog_row=lane_pad(gdn_a_log), dtb_row=lane_pad(gdn_dt_bias), gdn_norm_g=gdn_norm_g,
        w_br_a=w_br_a.astype(BF16), w_br_b=w_br_b.astype(BF16), w_br_c=w_br_c.astype(BF16),
        w_out=w_out.astype(BF16),
        ln1_g=ln1_g.reshape(DEPTH, 1, D_MODEL), ln1_b=ln1_b.reshape(DEPTH, 1, D_MODEL),
        wr_hi=wr_hi, wr_lo=(wr - wr_hi.astype(F32)).astype(BF16),
        br=jnp.concatenate([router_e_b, router_g_b, jnp.zeros((DEPTH, 128 - N_EXPERTS - N_GROUPS), F32)],
                           axis=-1).reshape(DEPTH, 1, 128),
        exp_w_gate=exp_w_gate, exp_w_up=exp_w_up, exp_w_down=exp_w_down,
        ln2_g=ln2_g.reshape(DEPTH, 1, D_MODEL), ln2_b=ln2_b.reshape(DEPTH, 1, D_MODEL),
    )

    mod = _ada(jnp.concatenate([c_prompt, c_sample], axis=0), w_ada, b_ada)
    mod = mod.reshape(DEPTH, nbp + nbs, 6, D_MODEL)

    zeros = lambda *s: jnp.zeros((DEPTH, nbp) + s, F32)
    st_p = {'C': zeros(N_HEADS, DH, DH), 'n': zeros(N_HEADS, DH), 'm': zeros(N_HEADS),
            'conv': zeros(CONV_B - 1, D_MIX), 'S': zeros(N_HEADS, DH, DH), 'gconv': zeros(CONV_C - 1, 3 * D_MIX)}
    y_p, sp = _trunk(x_prompt, mod[:, :nbp], st_p, p, bt=1, lt=512, lv=lp,
                     mix=dict(NB=8, G=1, tl=RB, SB=nbp), moe_block=256, ip_tile=(1, 256))

    st_s = {'C': state_mlstm_C, 'n': state_mlstm_n, 'm': state_mlstm_m, 'conv': state_conv,
            'S': state_gdn_S, 'gconv': state_gdn_conv}
    xs = jnp.pad(x_sample, ((0, 0), (0, lsp - ls), (0, 0)))
    y_s, ss = _trunk(xs, mod[:, nbp:], st_s, p, bt=64, lt=lsp, lv=ls,
                     mix=dict(NB=2, G=RB // lsp, tl=lsp, SB=1), moe_block=128, ip_tile=(32, lsp))
    y_s = y_s[:, :ls]

    return (y_p, y_s, sp['C'], sp['n'], sp['m'], sp['conv'], sp['S'], sp['gconv'],
            ss['C'], ss['n'], ss['m'], ss['conv'], ss['S'], ss['gconv'])
```

```python
import functools

import jax
import jax.numpy as jnp
from jax import lax
from jax.experimental import pallas as pl
from jax.experimental.pallas import tpu as pltpu

F32 = jnp.float32
BF16 = jnp.bfloat16

D_MODEL = 1024
DEPTH = 2
N_HEADS = 4
DH = 128
D_MIX = N_HEADS * DH
N_EXPERTS = 32
EXPERTS_PER_GROUP = 8
N_GROUPS = 4
D_EXPERT = 256
CONV_B = 3
CONV_C = 4
HIST = 8
RB = 64
INV_BLOCK = 8
DN_ALPHA = (2 * DEPTH) ** 0.25
LN_EPS = 1e-5
NORM_EPS = 1e-6
NEG = -1e30

OFF_QKVC = 0
OFF_BCH = 1536
OFF_GTS = 3072
OFF_QKVO = 6144
OFF_Z = 8192
OFF_SA = 8704
OFF_SB = 8832
N_PROJ = 8960
TN_PROJ = 1280
PROJ_SEGMENTS = ((3592, 5128, OFF_QKVC), (2056, 3592, OFF_BCH), (5648, 8720, OFF_GTS), (0, 2048, OFF_QKVO),
                 (5128, 5640, OFF_Z), (2048, 2052, OFF_SA), (5640, 5644, OFF_SA + 4), (2052, 2056, OFF_SB),
                 (5644, 5648, OFF_SB + 4))

VMEM_LIMIT = 52 * 1024 * 1024


def _dot(a, b):
    return jnp.dot(a, b, preferred_element_type=F32)


def _split3(x):
    hi = x.astype(BF16)
    r = x - hi.astype(F32)
    mid = r.astype(BF16)
    lo = (r - mid.astype(F32)).astype(BF16)
    return hi, mid, lo


def _layer_norm(x, g, b):
    mu = jnp.mean(x, axis=-1, keepdims=True)
    xc = x - mu
    var = jnp.mean(xc * xc, axis=-1, keepdims=True)
    return xc * lax.rsqrt(var + LN_EPS) * g + b


def _sigmoid(x):
    return jax.nn.sigmoid(x)


def _silu(x):
    return x * _sigmoid(x)


def _log_sigmoid(x):
    return jnp.minimum(x, 0.0) - jnp.log1p(jnp.exp(-jnp.abs(x)))


def _softplus(x):
    return jnp.maximum(x, 0.0) + jnp.log1p(jnp.exp(-jnp.abs(x)))


def _ada_body(c_ref, w_ref, b_ref, o_ref):
    c = c_ref[...]
    s = _silu(c).astype(BF16)
    o_ref[0] = _dot(s, w_ref[0].astype(BF16)) + b_ref[0]


def _ada(c_all, w_ada, b_ada):
    nb = c_all.shape[0]
    return pl.pallas_call(
        _ada_body,
        grid=(DEPTH, 6),
        in_specs=[pl.BlockSpec((nb, D_MODEL), lambda l, j: (0, 0)),
                  pl.BlockSpec((1, D_MODEL, D_MODEL), lambda l, j: (l, 0, j)),
                  pl.BlockSpec((1, 1, D_MODEL), lambda l, j: (l, 0, j))],
        out_specs=pl.BlockSpec((1, nb, D_MODEL), lambda l, j: (l, 0, j)),
        out_shape=jax.ShapeDtypeStruct((DEPTH, nb, 6 * D_MODEL), F32),
        compiler_params=pltpu.CompilerParams(dimension_semantics=("arbitrary", "arbitrary"),
                                             vmem_limit_bytes=VMEM_LIMIT),
        name="ada",
    )(c_all, w_ada, b_ada.reshape(DEPTH, 1, 6 * D_MODEL))


def _regroup_body(w_ref, o_ref):
    o_ref[...] = jnp.zeros_like(o_ref)
    for src, end, dst in PROJ_SEGMENTS:
        o_ref[0, :, dst:dst + end - src] = w_ref[0, :, src:end].astype(BF16)


def _regroup_w_in(w_in):
    n_in = w_in.shape[-1]
    rows = 256
    return pl.pallas_call(
        _regroup_body,
        grid=(DEPTH, D_MODEL // rows),
        in_specs=[pl.BlockSpec((1, rows, n_in), lambda l, i: (l, i, 0))],
        out_specs=pl.BlockSpec((1, rows, N_PROJ), lambda l, i: (l, i, 0)),
        out_shape=jax.ShapeDtypeStruct((DEPTH, D_MODEL, N_PROJ), BF16),
        compiler_params=pltpu.CompilerParams(dimension_semantics=("arbitrary", "arbitrary"),
                                             vmem_limit_bytes=VMEM_LIMIT),
        name="regroup_w_in",
    )(w_in)


def _inproj_body(x_ref, mod_ref, g_ref, b_ref, w_ref, bias_ref, proj_ref, *rest, apply_ln, bt, lt):
    x = x_ref[...]
    if apply_ln:
        x = _layer_norm(x, g_ref[...], b_ref[...])
        rest[0][...] = x
    u = (x * (1.0 + mod_ref[:, 1:2, :]) + mod_ref[:, 0:1, :]).reshape(bt * lt, D_MODEL).astype(BF16)
    for j in range(N_PROJ // TN_PROJ):
        cs = slice(j * TN_PROJ, (j + 1) * TN_PROJ)
        proj_ref[:, :, cs] = (_dot(u, w_ref[0, :, cs]) + bias_ref[0, :, cs]).reshape(bt, lt, TN_PROJ)


def _inproj(x, mod, ln_g, ln_b, w_r, b_r, layer, apply_ln, bt, lt):
    nb, lp, _ = x.shape
    tok = lambda i, t: (i, t, 0)
    out_shape = [jax.ShapeDtypeStruct((nb, lp, N_PROJ), F32)]
    out_specs = [pl.BlockSpec((bt, lt, N_PROJ), tok)]
    if apply_ln:
        out_shape.append(jax.ShapeDtypeStruct((nb, lp, D_MODEL), F32))
        out_specs.append(pl.BlockSpec((bt, lt, D_MODEL), tok))
    res = pl.pallas_call(
        functools.partial(_inproj_body, apply_ln=apply_ln, bt=bt, lt=lt),
        grid=(nb // bt, lp // lt),
        in_specs=[pl.BlockSpec((bt, lt, D_MODEL), tok),
                  pl.BlockSpec((bt, 6, D_MODEL), lambda i, t: (i, 0, 0)),
                  pl.BlockSpec((1, D_MODEL), lambda i, t: (0, 0)),
                  pl.BlockSpec((1, D_MODEL), lambda i, t: (0, 0)),
                  pl.BlockSpec((1, D_MODEL, N_PROJ), lambda i, t: (layer, 0, 0), pipeline_mode=pl.Buffered(1)),
                  pl.BlockSpec((1, 1, N_PROJ), lambda i, t: (layer, 0, 0))],
        out_specs=out_specs,
        out_shape=out_shape,
        compiler_params=pltpu.CompilerParams(dimension_semantics=("arbitrary", "arbitrary"),
                                             vmem_limit_bytes=VMEM_LIMIT),
        name=f"inproj_l{layer}_b{bt}",
    )(x, mod, ln_g, ln_b, w_r, b_r)
    return (res[0], res[1]) if apply_ln else (res[0], x)


def _conv_taps(xp_s, w_ref, width, tl):
    acc = None
    for j in range(width):
        tap = xp_s[:, pl.ds(HIST - (width - 1) + j, tl), :] * w_ref[j:j + 1, :].reshape(1, 1, -1)
        acc = tap if acc is None else acc + tap
    return acc


def _conv_history(xp_s, hist_ref, prev_ref, n_steps):
    if n_steps > 1:
        @pl.when(pl.program_id(1) == 0)
        def _():
            xp_s[:, 0:HIST, :] = hist_ref[...]

        @pl.when(pl.program_id(1) > 0)
        def _():
            xp_s[:, 0:HIST, :] = prev_ref[...]
    else:
        xp_s[:, 0:HIST, :] = hist_ref[...]


def _heads(x, nb, width):
    return jnp.stack([x[:, :, h * width:(h + 1) * width] for h in range(N_HEADS)],
                     axis=1).reshape(nb * N_HEADS, RB, width)


def _gate_cols(x, nb, lane0):
    return jnp.stack([x[:, :, lane0 + h:lane0 + h + 1] for h in range(N_HEADS)],
                     axis=1).reshape(nb * N_HEADS, RB, 1)


def _gate_rows(x, nb, lane0):
    xt = jnp.swapaxes(x, 1, 2)
    return jnp.stack([xt[:, lane0 + h:lane0 + h + 1, :] for h in range(N_HEADS)],
                     axis=1).reshape(nb * N_HEADS, 1, RB)


def _bmm(a, b):
    return jnp.einsum('nts,nsu->ntu', a.astype(BF16), b.astype(BF16), preferred_element_type=F32)


def _bmm_nt(a, b):
    return jnp.einsum('ntd,nsd->nts', a, b, preferred_element_type=F32)


def _block_masks(tl):
    rr = lax.broadcasted_iota(jnp.int32, (RB, RB), 0)
    cc = lax.broadcasted_iota(jnp.int32, (RB, RB), 1)
    incl = rr >= cc
    if tl < RB:
        incl = incl & ((rr // tl) == (cc // tl))
    return rr, cc, incl


def _seq_cumsum(x, incl, nb):
    tril = jnp.broadcast_to(incl.astype(BF16)[None], (nb, RB, RB))
    hi, mid, lo = _split3(x)
    return _bmm(tril, hi) + _bmm(tril, mid) + _bmm(tril, lo)


def _seq_last(x, nb, G, tl):
    return x.reshape(nb * G, tl, 128)[:, tl - 1:tl, :]


def _seq_rows(x3, nb, tl):
    return jnp.broadcast_to(x3, (x3.shape[0], tl, 128)).reshape(nb, RB, 128)


def _put_chains(ref, val, nb, G, tl):
    val4 = val.reshape(nb, N_HEADS, RB, val.shape[-1])
    for h in range(N_HEADS):
        ref[:, :, h] = val4[:, h].reshape(nb, G, tl, val.shape[-1])


def _put_seq_scalars(ref, x3, nb, G, lane0):
    x4 = x3.reshape(nb, G, 1, 128)
    for h in range(N_HEADS):
        ref[:, :, h] = jnp.broadcast_to(x4[:, :, :, lane0 + h:lane0 + h + 1], (nb, G, 1, 128))


def _step_tiling(nb, lp, NB, G, tl):
    assert G * tl == RB
    if G == 1:
        gx, tlx = 1, NB * RB
    else:
        assert lp == tl
        gx, tlx = NB * G, tl
    return gx, tlx, nb // gx, lp // tlx


def _layered_state(layer, nb, NS, prev):
    st_in = pl.BlockSpec((None, NS, N_HEADS, DH, DH), lambda i, c: (layer, i, 0, 0, 0))
    shape = jax.ShapeDtypeStruct((DEPTH, nb, N_HEADS, DH, DH), F32)
    if layer == 0:
        return st_in, pl.BlockSpec((DEPTH, NS, N_HEADS, DH, DH), lambda i, c: (0, i, 0, 0, 0)), shape, [], []
    return st_in, st_in, shape, [pl.BlockSpec(memory_space=pl.ANY)], [prev]


def _fill_later_layers(ref, when):
    @pl.when(when)
    def _():
        for l in range(1, DEPTH):
            ref[l] = ref[0]


def _mlstm_prep_body(qkvo_ref, bch_ref, prev_ref, sa_ref, sb_ref, cv0_ref, cw_ref,
                     yb_ref, cv_ref, nv_ref, q_ref, kw_ref, v_ref, rows_ref, bl_ref, bc_ref, kn_ref, xp_s,
                     *, NB, G, tl, tlx, NCS, lv):
    cs = pl.program_id(1)
    n = NB * N_HEADS
    lvl = lv - (NCS - 1) * tlx

    if NCS > 1:
        @pl.when(cs == 0)
        def _():
            xp_s[:, 0:HIST, :] = cv0_ref[...]

        @pl.when(cs > 0)
        def _():
            xp_s[:, 0:HIST, :] = prev_ref[:, :, D_MIX:2 * D_MIX] * prev_ref[:, :, 2 * D_MIX:3 * D_MIX]
    else:
        xp_s[:, 0:HIST, :] = cv0_ref[...]
    xp_s[:, HIST:HIST + tlx, :] = bch_ref[:, :, D_MIX:2 * D_MIX] * bch_ref[:, :, 2 * D_MIX:3 * D_MIX]
    yb_ref[...] = (bch_ref[:, :, 0:D_MIX] * _conv_taps(xp_s, cw_ref, CONV_B, tlx)).astype(BF16)

    @pl.when(cs == NCS - 1)
    def _():
        cv_ref[...] = xp_s[:, pl.ds(HIST + lvl - (CONV_B - 1), CONV_B - 1), :]

    i_all = sa_ref[...].reshape(NB, RB, 128)
    f_all = _log_sigmoid(sb_ref[...].reshape(NB, RB, 128))
    if lv < NCS * tlx:
        assert NCS == 1
        valid = (lax.broadcasted_iota(jnp.int32, (NB, RB, 128), 1) % tl) < lv
        i_all = jnp.where(valid, i_all, NEG)
        f_all = jnp.where(valid, f_all, 0.0)
    _, _, incl = _block_masks(tl)
    bcum = _seq_cumsum(f_all, incl, NB)
    blast = _seq_last(bcum, NB, G, tl)
    val = _seq_rows(blast, NB, tl) - bcum + i_all
    bmax = jnp.max(val.reshape(NB * G, tl, 128), axis=1, keepdims=True)
    wk0 = jnp.exp(val - _seq_rows(bmax, NB, tl))

    qkvo = qkvo_ref[...].reshape(NB, RB, 4 * D_MIX)
    q = _heads(qkvo[:, :, 0:D_MIX], NB, DH)
    k = _heads(qkvo[:, :, D_MIX:2 * D_MIX], NB, DH) * (DH ** -0.5)
    v = _heads(qkvo[:, :, 2 * D_MIX:3 * D_MIX], NB, DH)
    qb, kb, vb = q.astype(BF16), k.astype(BF16), v.astype(BF16)

    b_col = _gate_cols(bcum, NB, 0)
    dlog = jnp.where(incl[None], b_col - _gate_rows(bcum, NB, 0) + _gate_rows(i_all, NB, 0), NEG)
    d = jnp.max(dlog, axis=-1, keepdims=True)
    s0 = _bmm_nt(qb, kb) * jnp.exp(dlog - d)
    kw0 = k * _gate_cols(wk0, NB, 0)

    _put_chains(nv_ref, _bmm(s0, vb), NB, G, tl)
    _put_chains(q_ref, qb, NB, G, tl)
    _put_chains(kw_ref, kw0.astype(BF16), NB, G, tl)
    _put_chains(v_ref, vb, NB, G, tl)
    rows = jnp.swapaxes(jnp.concatenate([d, b_col, jnp.sum(s0, axis=-1, keepdims=True),
                                         jnp.zeros((n, RB, 5), F32)], axis=-1), 1, 2).reshape(NB, N_HEADS, 8, RB)
    for h in range(N_HEADS):
        for g in range(G):
            rows_ref[:, g, h] = rows[:, h, :, g * tl:(g + 1) * tl]
    _put_seq_scalars(bl_ref, blast, NB, G, 0)
    _put_seq_scalars(bc_ref, bmax, NB, G, 0)
    kn = jnp.sum(kw0.reshape(n * G, tl, DH), axis=1, keepdims=True).reshape(NB, N_HEADS, G, 1, DH)
    for h in range(N_HEADS):
        kn_ref[:, :, h] = kn[:, h]


def _mlstm_prep(proj, cv0, cw, NB, G, tl, lv):
    nb, lp, _ = proj.shape
    gx, tlx, NI, NCS = _step_tiling(nb, lp, NB, G, tl)
    nbt = nb * lp // RB
    step = lambda i, c: (i * NCS + c, 0, 0, 0, 0)
    chain = lambda last, dt: jax.ShapeDtypeStruct((nbt, G, N_HEADS, tl, last), dt)
    cspec = lambda last: pl.BlockSpec((NB, G, N_HEADS, tl, last), step)
    scal = jax.ShapeDtypeStruct((nbt, G, N_HEADS, 1, 128), F32)
    sspec = pl.BlockSpec((NB, G, N_HEADS, 1, 128), step)
    bch = OFF_BCH // (3 * D_MIX)
    return pl.pallas_call(
        functools.partial(_mlstm_prep_body, NB=NB, G=G, tl=tl, tlx=tlx, NCS=NCS, lv=lv),
        grid=(NI, NCS),
        in_specs=[pl.BlockSpec((gx, tlx, 4 * D_MIX), lambda i, c: (i, c, OFF_QKVO // (4 * D_MIX))),
                  pl.BlockSpec((gx, tlx, 3 * D_MIX), lambda i, c: (i, c, bch)),
                  pl.BlockSpec((gx, HIST, 3 * D_MIX), lambda i, c: (i, jnp.maximum(c * (tlx // HIST) - 1, 0), bch)),
                  pl.BlockSpec((gx, tlx, 128), lambda i, c: (i, c, OFF_SA // 128)),
                  pl.BlockSpec((gx, tlx, 128), lambda i, c: (i, c, OFF_SB // 128)),
                  pl.BlockSpec((gx, HIST, D_MIX), lambda i, c: (i, 0, 0)),
                  pl.BlockSpec((8, D_MIX), lambda i, c: (0, 0))],
        out_specs=[pl.BlockSpec((gx, tlx, D_MIX), lambda i, c: (i, c, 0)),
                   pl.BlockSpec((gx, CONV_B - 1, D_MIX), lambda i, c: (i, 0, 0)),
                   cspec(DH), cspec(DH), cspec(DH), cspec(DH),
                   pl.BlockSpec((NB, G, N_HEADS, 8, tl), step), sspec, sspec, sspec],
        out_shape=[jax.ShapeDtypeStruct((nb, lp, D_MIX), BF16),
                   jax.ShapeDtypeStruct((nb, CONV_B - 1, D_MIX), F32),
                   chain(DH, F32), chain(DH, BF16), chain(DH, BF16), chain(DH, BF16),
                   jax.ShapeDtypeStruct((nbt, G, N_HEADS, 8, tl), F32),
                   scal, scal, scal],
        scratch_shapes=[pltpu.VMEM((gx, HIST + tlx, D_MIX), F32)],
        compiler_params=pltpu.CompilerParams(dimension_semantics=("arbitrary", "arbitrary"),
                                             vmem_limit_bytes=VMEM_LIMIT),
        name=f"mlstm_prep_g{G}",
    )(proj, proj, proj, proj, proj, cv0, cw)


def _mlstm_scan_body(nv_ref, q_ref, kw_ref, v_ref, rows_ref, bl_ref, bc_ref, kn_ref, o_ref, C0_ref, n0_ref, m0_ref,
                     ng_ref, *rest, NS, tl, NC, first):
    ya_ref, C_out, n_ref, m_ref = rest[-4:]
    C_ref = C_out.at[0] if first else C_out
    c = pl.program_id(1)
    n = NS * N_HEADS

    @pl.when(c == 0)
    def _():
        C_ref[...] = C0_ref[...]
        n_ref[...] = n0_ref[...]
        m_ref[...] = m0_ref[...]

    chains = lambda ref: ref[...].reshape(n, ref.shape[-2], ref.shape[-1])
    C = C_ref[...].reshape(n, DH, DH)
    nvec = n_ref[...].reshape(n, 1, DH)
    m_prev = chains(m_ref)[:, :, 0:1]
    rows = chains(rows_ref)
    d, b, ds0 = rows[:, 0:1, :], rows[:, 1:2, :], rows[:, 2:3, :]
    qb = chains(q_ref)

    m_inter = b + m_prev
    m_t = jnp.maximum(m_inter, d)
    f = jnp.exp(d - m_t)
    inter = jnp.exp(m_inter - m_t)
    qn = jnp.einsum('nod,ntd->not', nvec.astype(BF16), qb, preferred_element_type=F32)
    den = f * ds0 + inter * qn
    scale_t = 1.0 / jnp.maximum(jnp.abs(den), jnp.exp(-m_t))
    fi = jnp.swapaxes(jnp.concatenate([f * scale_t, inter * scale_t, jnp.zeros((n, 6, tl), F32)], axis=1), 1, 2)
    qC = jnp.einsum('ntd,nde->nte', qb, C.astype(BF16), preferred_element_type=F32)
    hh = fi[:, :, 0:1] * chains(nv_ref) + fi[:, :, 1:2] * qC

    m_new = m_t[:, :, tl - 1:tl]
    decay = jnp.exp(chains(bl_ref)[:, :, 0:1] + m_prev - m_new)
    scale = jnp.exp(chains(bc_ref)[:, :, 0:1] - m_new)
    kv = jnp.einsum('ntd,nte->nde', chains(kw_ref), chains(v_ref), preferred_element_type=F32)
    C_ref[...] = (decay * C + scale * kv).reshape(NS, N_HEADS, DH, DH)
    n_ref[...] = (decay * nvec + scale * chains(kn_ref)).reshape(NS, N_HEADS, DH)
    m_ref[...] = jnp.broadcast_to(m_new, (n, 1, 128)).reshape(NS, N_HEADS, 1, 128)
    if first:
        _fill_later_layers(C_out, c == NC - 1)

    mu = jnp.mean(hh, axis=-1, keepdims=True)
    hc = hh - mu
    hn = (hc * lax.rsqrt(jnp.mean(hc * hc, axis=-1, keepdims=True) + LN_EPS)).reshape(NS, N_HEADS, tl, DH)
    for h in range(N_HEADS):
        hs = slice(h * DH, (h + 1) * DH)
        ya_ref[:, :, hs] = (_sigmoid(o_ref[:, :, hs]) * hn[:, h] * ng_ref[:, hs]).astype(BF16)


def _mlstm_scan(pre, proj, C0, n0, m0, ng, SB, G, tl, layer, C_prev):
    nb, lp, _ = proj.shape
    NS = SB * G
    NI, NC = nb // NS, lp // tl
    six = lambda a: a.reshape((NI * SB, NC) + a.shape[1:])
    spec6 = lambda a: pl.BlockSpec((SB, 1) + a.shape[1:], lambda i, c: (i, c, 0, 0, 0, 0))
    seq4 = lambda i, c: (i, 0, 0, 0)
    st_in, st_out, st_shape, extra_in, extra_args = _layered_state(layer, nb, NS, C_prev)
    n_in = len(pre) + 5
    return pl.pallas_call(
        functools.partial(_mlstm_scan_body, NS=NS, tl=tl, NC=NC, first=layer == 0),
        grid=(NI, NC),
        in_specs=[spec6(a) for a in pre] + [
            pl.BlockSpec((NS, tl, D_MIX), lambda i, c: (i, c, (OFF_QKVO + 3 * D_MIX) // D_MIX)),
            st_in,
            pl.BlockSpec((NS, N_HEADS, DH), lambda i, c: (i, 0, 0)),
            pl.BlockSpec((NS, N_HEADS, 1, 128), seq4),
            pl.BlockSpec((1, D_MIX), lambda i, c: (0, 0))] + extra_in,
        out_specs=[pl.BlockSpec((NS, tl, D_MIX), lambda i, c: (i, c, 0)),
                   st_out,
                   pl.BlockSpec((NS, N_HEADS, DH), lambda i, c: (i, 0, 0)),
                   pl.BlockSpec((NS, N_HEADS, 1, 128), seq4)],
        out_shape=[jax.ShapeDtypeStruct((nb, lp, D_MIX), BF16),
                   st_shape,
                   jax.ShapeDtypeStruct((nb, N_HEADS, DH), F32),
                   jax.ShapeDtypeStruct((nb, N_HEADS, 1, 128), F32)],
        input_output_aliases={n_in: 1} if extra_in else {},
        compiler_params=pltpu.CompilerParams(dimension_semantics=("arbitrary", "arbitrary"),
                                             vmem_limit_bytes=VMEM_LIMIT),
        name=f"mlstm_scan_g{G}",
    )(*[six(a) for a in pre], proj, C0, n0, m0, ng, *extra_args)


def _unit_lower_inverse(n, rr, cc, tl):
    eye = (rr == cc).astype(F32)[None]
    p = jnp.where(((rr // INV_BLOCK) == (cc // INV_BLOCK))[None], n, 0.0)
    x = eye + p
    b = 2
    while b < INV_BLOCK:
        p = _bmm(p, p)
        x = x + _bmm(x, p)
        b *= 2
    b = INV_BLOCK
    while b < tl:
        off = jnp.where((((rr // (2 * b)) == (cc // (2 * b))) & ((rr // b) != (cc // b)))[None], n, 0.0)
        x = x + _bmm(x, _bmm(off, x))
        b *= 2
    return x


def _gdn_prep_body(x_ref, prev_ref, sa_ref, sb_ref, gc0_ref, cw_ref, alog_ref, dtb_ref,
                   u_ref, w_ref, qg_ref, kd_ref, qkm_ref, eg_ref, gcs_ref, xp_s, *, NB, G, tl, tlx, NCS, lv):
    cs = pl.program_id(1)
    lvl = lv - (NCS - 1) * tlx

    _conv_history(xp_s, gc0_ref, prev_ref, NCS)
    xp_s[:, HIST:HIST + tlx, :] = x_ref[...]
    qkv = _silu(_conv_taps(xp_s, cw_ref, CONV_C, tlx)).reshape(NB, RB, 3 * D_MIX)

    @pl.when(cs == NCS - 1)
    def _():
        gcs_ref[...] = xp_s[:, pl.ds(HIST + lvl - (CONV_C - 1), CONV_C - 1), :]

    beta_all = _sigmoid(sa_ref[...].reshape(NB, RB, 128))
    g_all = -jnp.exp(alog_ref[...]) * _softplus(sb_ref[...].reshape(NB, RB, 128) + dtb_ref[...])
    if lv < NCS * tlx:
        assert NCS == 1
        valid = (lax.broadcasted_iota(jnp.int32, (NB, RB, 128), 1) % tl) < lv
        beta_all = jnp.where(valid, beta_all, 0.0)
        g_all = jnp.where(valid, g_all, 0.0)
    rr, cc, incl = _block_masks(tl)
    diag = rr == cc
    gam = _seq_cumsum(g_all, incl, NB)
    glast = _seq_last(gam, NB, G, tl)
    gcol = _gate_cols(gam, NB, 4)
    bcol = _gate_cols(beta_all, NB, 4)
    egcol = _gate_cols(jnp.exp(gam), NB, 4)
    kdcol = _gate_cols(jnp.exp(_seq_rows(glast, NB, tl) - gam), NB, 4)

    q = _heads(qkv[:, :, 0:D_MIX], NB, DH)
    k = _heads(qkv[:, :, D_MIX:2 * D_MIX], NB, DH)
    v = _heads(qkv[:, :, 2 * D_MIX:3 * D_MIX], NB, DH)
    q = q * lax.rsqrt(jnp.sum(q * q, axis=-1, keepdims=True) + NORM_EPS) * (DH ** -0.5)
    k = k * lax.rsqrt(jnp.sum(k * k, axis=-1, keepdims=True) + NORM_EPS)
    qb, kb = q.astype(BF16), k.astype(BF16)

    dmat = jnp.exp(jnp.where(incl[None], gcol - _gate_rows(gam, NB, 4), NEG))
    nmat = jnp.where(diag[None], 0.0, -(bcol * _bmm_nt(kb, kb) * dmat))
    rhs = jnp.concatenate([bcol * v, (bcol * egcol) * k], axis=-1)
    sol = _bmm(_unit_lower_inverse(nmat, rr, cc, tl), rhs)
    qkm = (_bmm_nt(qb, kb) * dmat).astype(BF16).reshape(NB, N_HEADS, RB, RB)

    _put_chains(u_ref, sol[:, :, 0:DH], NB, G, tl)
    _put_chains(w_ref, sol[:, :, DH:2 * DH].astype(BF16), NB, G, tl)
    _put_chains(qg_ref, (q * egcol).astype(BF16), NB, G, tl)
    _put_chains(kd_ref, (k * kdcol).astype(BF16), NB, G, tl)
    _put_seq_scalars(eg_ref, jnp.exp(glast), NB, G, 4)
    for h in range(N_HEADS):
        for g in range(G):
            qkm_ref[:, g, h] = qkm[:, h, g * tl:(g + 1) * tl, g * tl:(g + 1) * tl]


def _gdn_prep(proj, gc0, cw, alog, dtb, NB, G, tl, lv):
    nb, lp, _ = proj.shape
    assert tl % INV_BLOCK == 0 and (tl // INV_BLOCK) & (tl // INV_BLOCK - 1) == 0
    gx, tlx, NI, NCS = _step_tiling(nb, lp, NB, G, tl)
    nbt = nb * lp // RB
    step = lambda i, c: (i * NCS + c, 0, 0, 0, 0)
    par = lambda i, c: (0, 0)
    chain = lambda last, dt: jax.ShapeDtypeStruct((nbt, G, N_HEADS, tl, last), dt)
    cspec = lambda last: pl.BlockSpec((NB, G, N_HEADS, tl, last), step)
    return pl.pallas_call(
        functools.partial(_gdn_prep_body, NB=NB, G=G, tl=tl, tlx=tlx, NCS=NCS, lv=lv),
        grid=(NI, NCS),
        in_specs=[pl.BlockSpec((gx, tlx, 3 * D_MIX), lambda i, c: (i, c, OFF_QKVC // (3 * D_MIX))),
                  pl.BlockSpec((gx, HIST, 3 * D_MIX),
                               lambda i, c: (i, jnp.maximum(c * (tlx // HIST) - 1, 0), OFF_QKVC // (3 * D_MIX))),
                  pl.BlockSpec((gx, tlx, 128), lambda i, c: (i, c, OFF_SA // 128)),
                  pl.BlockSpec((gx, tlx, 128), lambda i, c: (i, c, OFF_SB // 128)),
                  pl.BlockSpec((gx, HIST, 3 * D_MIX), lambda i, c: (i, 0, 0)),
                  pl.BlockSpec((8, 3 * D_MIX), par),
                  pl.BlockSpec((1, 128), par),
                  pl.BlockSpec((1, 128), par)],
        out_specs=[cspec(DH), cspec(DH), cspec(DH), cspec(DH), cspec(tl),
                   pl.BlockSpec((NB, G, N_HEADS, 1, 128), step),
                   pl.BlockSpec((gx, CONV_C - 1, 3 * D_MIX), lambda i, c: (i, 0, 0))],
        out_shape=[chain(DH, F32), chain(DH, BF16), chain(DH, BF16), chain(DH, BF16), chain(tl, BF16),
                   jax.ShapeDtypeStruct((nbt, G, N_HEADS, 1, 128), F32),
                   jax.ShapeDtypeStruct((nb, CONV_C - 1, 3 * D_MIX), F32)],
        scratch_shapes=[pltpu.VMEM((gx, HIST + tlx, 3 * D_MIX), F32)],
        compiler_params=pltpu.CompilerParams(dimension_semantics=("arbitrary", "arbitrary"),
                                             vmem_limit_bytes=VMEM_LIMIT),
        name=f"gdn_prep_g{G}",
    )(proj, proj, proj, proj, gc0, cw, alog, dtb)


def _gdn_scan_body(u_ref, w_ref, qg_ref, kd_ref, qkm_ref, eg_ref, z_ref, S0_ref, gng_ref, *rest,
                   NS, tl, NC, first):
    yc_ref, S_out = rest[-2:]
    S_ref = S_out.at[0] if first else S_out
    c = pl.program_id(1)
    n = NS * N_HEADS

    @pl.when(c == 0)
    def _():
        S_ref[...] = S0_ref[...]

    S = S_ref[...].reshape(n, DH, DH)
    Sb = S.astype(BF16)
    chains = lambda ref: ref[...].reshape(n, tl, ref.shape[-1])
    v_new = chains(u_ref) - jnp.einsum('ntd,nde->nte', chains(w_ref), Sb, preferred_element_type=F32)
    vnb = v_new.astype(BF16)
    o = (jnp.einsum('ntd,nde->nte', chains(qg_ref), Sb, preferred_element_type=F32)
         + jnp.einsum('nts,nse->nte', chains(qkm_ref), vnb, preferred_element_type=F32))
    eg = eg_ref[...].reshape(n, 1, 128)[:, :, 0:1]
    S_new = eg * S + jnp.einsum('ntd,nte->nde', chains(kd_ref), vnb, preferred_element_type=F32)
    S_ref[...] = S_new.reshape(NS, N_HEADS, DH, DH)
    if first:
        _fill_later_layers(S_out, c == NC - 1)

    on = (o * lax.rsqrt(jnp.mean(o * o, axis=-1, keepdims=True) + NORM_EPS) * gng_ref[...]).reshape(NS, N_HEADS, tl, DH)
    for h in range(N_HEADS):
        yc_ref[:, :, h * DH:(h + 1) * DH] = (on[:, h] * _silu(z_ref[:, :, h * DH:(h + 1) * DH])).astype(BF16)


def _gdn_scan(pre, proj, S0, gng, SB, G, tl, layer, S_prev):
    nb, lp, _ = proj.shape
    NS = SB * G
    NI, NC = nb // NS, lp // tl
    six = lambda a: a.reshape((NI * SB, NC) + a.shape[1:])
    cspec = lambda last: pl.BlockSpec((SB, 1, G, N_HEADS, tl, last), lambda i, c: (i, c, 0, 0, 0, 0))
    u, w, qg, kd, qkm, eg = (six(a) for a in pre)
    st_in, st_out, st_shape, extra_in, extra_args = _layered_state(layer, nb, NS, S_prev)
    return pl.pallas_call(
        functools.partial(_gdn_scan_body, NS=NS, tl=tl, NC=NC, first=layer == 0),
        grid=(NI, NC),
        in_specs=[cspec(DH), cspec(DH), cspec(DH), cspec(DH), cspec(tl),
                  pl.BlockSpec((SB, 1, G, N_HEADS, 1, 128), lambda i, c: (i, c, 0, 0, 0, 0)),
                  pl.BlockSpec((NS, tl, D_MIX), lambda i, c: (i, c, OFF_Z // D_MIX)),
                  st_in,
                  pl.BlockSpec((1, DH), lambda i, c: (0, 0))] + extra_in,
        out_specs=[pl.BlockSpec((NS, tl, D_MIX), lambda i, c: (i, c, 0)), st_out],
        out_shape=[jax.ShapeDtypeStruct((nb, lp, D_MIX), BF16), st_shape],
        input_output_aliases={9: 1} if extra_in else {},
        compiler_params=pltpu.CompilerParams(dimension_semantics=("arbitrary", "arbitrary"),
                                             vmem_limit_bytes=VMEM_LIMIT),
        name=f"gdn_scan_g{G}",
    )(u, w, qg, kd, qkm, eg, proj, S0, gng, *extra_args)


def _route(rl):
    lane = lax.broadcasted_iota(jnp.int32, rl.shape, 1).astype(F32)
    is_g = (lane >= N_EXPERTS) & (lane < N_EXPERTS + N_GROUPS)
    gl = jnp.where(is_g, rl, NEG)
    gmax = jnp.max(gl, axis=-1, keepdims=True)
    grp = jnp.min(jnp.where(gl == gmax, lane - N_EXPERTS, 4.0 * N_EXPERTS), axis=-1, keepdims=True)
    p_grp = 1.0 / jnp.sum(jnp.where(is_g, jnp.exp(gl - gmax), 0.0), axis=-1, keepdims=True)
    lo = grp * EXPERTS_PER_GROUP
    in_grp = (lane >= lo) & (lane < lo + EXPERTS_PER_GROUP)
    el = jnp.where(in_grp, rl, NEG)
    m1 = jnp.max(el, axis=-1, keepdims=True)
    i1 = jnp.min(jnp.where(el == m1, lane, 4.0 * N_EXPERTS), axis=-1, keepdims=True)
    el2 = jnp.where(lane == i1, NEG, el)
    m2 = jnp.max(el2, axis=-1, keepdims=True)
    i2 = jnp.min(jnp.where(el2 == m2, lane, 4.0 * N_EXPERTS), axis=-1, keepdims=True)
    e2 = jnp.exp(m2 - m1)
    w1 = p_grp / (1.0 + e2)
    w2 = p_grp * e2 / (1.0 + e2)
    return i1, i2, w1, w2


def _merge_body(ya_ref, yb_ref, yc_ref, ga_ref, gb_ref, gc_ref, x_ref, mod_ref, wa_ref, wb_ref, wc_ref, wo_ref,
                lg_ref, lb_ref, wrh_ref, wrl_ref, br_ref, x1_ref, rinfo_ref, cnt_ref, *, bt, lt):
    tm = bt * lt

    @pl.when((pl.program_id(0) == 0) & (pl.program_id(1) == 0))
    def _():
        cnt_ref[...] = jnp.zeros_like(cnt_ref)

    def r2(ref):
        return ref[...].reshape(tm, ref.shape[-1])

    merged = (_sigmoid(r2(ga_ref)) * _dot(r2(ya_ref), wa_ref[0])
              + _sigmoid(r2(gb_ref)) * _dot(r2(yb_ref), wb_ref[0])
              + _sigmoid(r2(gc_ref)) * _dot(r2(yc_ref), wc_ref[0]))
    out = _dot(merged.astype(BF16), wo_ref[0])
    y = DN_ALPHA * x_ref[...] + (1.0 + mod_ref[:, 2:3, :]) * out.reshape(bt, lt, D_MODEL)
    x1 = _layer_norm(y, lg_ref[0], lb_ref[0])
    x1_ref[...] = x1
    u2 = (x1 * (1.0 + mod_ref[:, 4:5, :]) + mod_ref[:, 3:4, :]).reshape(tm, D_MODEL)
    hi = u2.astype(BF16)
    lo = (u2 - hi.astype(F32)).astype(BF16)
    rl = _dot(hi, wrh_ref[0]) + _dot(lo, wrh_ref[0]) + _dot(hi, wrl_ref[0]) + br_ref[0]
    i1, i2, w1, w2 = _route(rl)
    lane = lax.broadcasted_iota(jnp.int32, (tm, 128), 1).astype(F32)
    onehot = jnp.where((lane == i1) | (lane == i2), 1.0, 0.0)
    rr = lax.broadcasted_iota(jnp.int32, (tm, tm), 0)
    cc = lax.broadcasted_iota(jnp.int32, (tm, tm), 1)
    before = _dot((rr > cc).astype(BF16), onehot.astype(BF16)) + cnt_ref[0:1, :]
    rank1 = jnp.sum(jnp.where(lane == i1, before, 0.0), axis=-1, keepdims=True)
    rank2 = jnp.sum(jnp.where(lane == i2, before, 0.0), axis=-1, keepdims=True)
    cnt_ref[0:1, :] += jnp.sum(onehot, axis=0, keepdims=True)
    rinfo = jnp.zeros((tm, 128), F32)
    for k, val in enumerate((i1, i2, w1, w2, rank1, rank2)):
        rinfo = jnp.where(lane == k, val, rinfo)
    rinfo_ref[...] = rinfo.reshape(bt, lt, 128)


def _merge(ya, yb, yc, proj, x, mod, wa, wb, wc, wo, lg, lb, wrh, wrl, br, layer, bt, lt):
    nb, lp, _ = x.shape
    tok = lambda i, t: (i, t, 0)
    wsp = lambda shape: pl.BlockSpec((1,) + shape, lambda i, t: (layer, 0, 0))
    g0 = OFF_GTS // D_MODEL
    return pl.pallas_call(
        functools.partial(_merge_body, bt=bt, lt=lt),
        grid=(nb // bt, lp // lt),
        in_specs=[pl.BlockSpec((bt, lt, D_MIX), tok),
                  pl.BlockSpec((bt, lt, D_MIX), tok),
                  pl.BlockSpec((bt, lt, D_MIX), tok),
                  pl.BlockSpec((bt, lt, D_MODEL), lambda i, t: (i, t, g0)),
                  pl.BlockSpec((bt, lt, D_MODEL), lambda i, t: (i, t, g0 + 1)),
                  pl.BlockSpec((bt, lt, D_MODEL), lambda i, t: (i, t, g0 + 2)),
                  pl.BlockSpec((bt, lt, D_MODEL), tok),
                  pl.BlockSpec((bt, 6, D_MODEL), lambda i, t: (i, 0, 0)),
                  wsp((D_MIX, D_MODEL)), wsp((D_MIX, D_MODEL)), wsp((D_MIX, D_MODEL)), wsp((D_MODEL, D_MODEL)),
                  wsp((1, D_MODEL)), wsp((1, D_MODEL)),
                  wsp((D_MODEL, 128)), wsp((D_MODEL, 128)), wsp((1, 128))],
        out_specs=[pl.BlockSpec((bt, lt, D_MODEL), tok),
                   pl.BlockSpec((bt, lt, 128), tok),
                   pl.BlockSpec((8, 128), lambda i, t: (0, 0))],
        out_shape=[jax.ShapeDtypeStruct((nb, lp, D_MODEL), F32),
                   jax.ShapeDtypeStruct((nb, lp, 128), F32),
                   jax.ShapeDtypeStruct((8, 128), F32)],
        compiler_params=pltpu.CompilerParams(dimension_semantics=("arbitrary", "arbitrary"),
                                             vmem_limit_bytes=VMEM_LIMIT),
        name=f"merge_l{layer}_b{bt}",
    )(ya, yb, yc, proj, proj, proj, x, mod, wa, wb, wc, wo, lg, lb, wrh, wrl, br)


def _moe_plan(rinfo, cnt, tm, mb):
    nb, lp, _ = rinfo.shape
    n_tok = nb * lp
    n_blocks = 2 * n_tok // mb + N_EXPERTS
    counts = cnt[0, :N_EXPERTS].astype(jnp.int32)
    nblk = (counts + mb - 1) // mb
    pend = jnp.cumsum(nblk)
    pstart = pend - nblk
    expert = rinfo[..., 0:2].astype(jnp.int32)
    rank = rinfo[..., 4:6].astype(jnp.int32)
    ids = jnp.arange(N_EXPERTS, dtype=jnp.int32)
    first_row = jnp.sum(jnp.where(expert[..., None] == ids, pstart * mb, 0), axis=-1)
    pos = (first_row + rank).reshape(n_tok // tm, 1, 2 * tm)
    block_e = jnp.sum(pend[None, :] <= jnp.arange(n_blocks, dtype=jnp.int32)[:, None], axis=1)
    block_e = jnp.minimum(block_e, N_EXPERTS - 1).astype(jnp.int32)
    n_used = pend[N_EXPERTS - 1:].astype(jnp.int32)
    ztail = jnp.where(nblk > 0, (pend - 1) * mb, -1).astype(jnp.int32)
    ztail = jnp.concatenate([ztail, n_used]).reshape(1, 1, N_EXPERTS + 1)
    return pos, block_e, n_used, ztail, n_blocks


def _row_copy(src, dst, sem):
    return pltpu.make_async_copy(src, dst, sem)


def _dispatch_body(pos_ref, ztail_ref, x1_ref, mod_ref, disp_ref, u_scr, z_scr, sem, zsem,
                   *, bt, lt, n_blocks, mb, n_steps):
    tm = bt * lt

    @pl.when((pl.program_id(0) == 0) & (pl.program_id(1) == 0))
    def _():
        z_scr[...] = jnp.zeros_like(z_scr)
        for e in range(N_EXPERTS):
            @pl.when(ztail_ref[0, 0, e] >= 0)
            def _():
                row = pl.multiple_of(ztail_ref[0, 0, e], mb)
                _row_copy(z_scr, disp_ref.at[pl.ds(row, mb)], zsem).start()
        def unused(j):
            return _row_copy(z_scr, disp_ref.at[pl.ds(pl.multiple_of(j * mb, mb), mb)], zsem)

        n_used = ztail_ref[0, 0, N_EXPERTS]
        lax.fori_loop(n_used, n_blocks, lambda j, c: (unused(j).start(), c)[1], 0)
        lax.fori_loop(n_used, n_blocks, lambda j, c: (unused(j).wait(), c)[1], 0)
        for e in range(N_EXPERTS):
            @pl.when(ztail_ref[0, 0, e] >= 0)
            def _():
                row = pl.multiple_of(ztail_ref[0, 0, e], mb)
                _row_copy(z_scr, disp_ref.at[pl.ds(row, mb)], zsem).wait()

    step = pl.program_id(0) * pl.num_programs(1) + pl.program_id(1)
    slot = step % 2
    u2 = x1_ref[...] * (1.0 + mod_ref[:, 4:5, :]) + mod_ref[:, 3:4, :]
    u_scr[slot] = u2.reshape(tm // 8, 8, D_MODEL)

    def issue(i, carry):
        for r in range(8):
            for k in range(2):
                row = pos_ref[0, 0, 16 * i + 2 * r + k]
                _row_copy(u_scr.at[slot, i, pl.ds(r, 1)], disp_ref.at[pl.ds(row, 1)],
                          sem.at[slot]).start(priority=k)
        return carry

    lax.fori_loop(0, tm // 8, issue, 0)

    def drain(which):
        for k in range(2):
            _row_copy(disp_ref.at[pl.ds(0, tm)], disp_ref.at[pl.ds(0, tm)], sem.at[which]).wait()

    @pl.when(step > 0)
    def _():
        drain(1 - slot)

    @pl.when(step == n_steps - 1)
    def _():
        drain(slot)


def _dispatch(x1, mod, pos, ztail, n_blocks, mb, bt, lt):
    nb, lp, _ = x1.shape
    nt = lp // lt
    return pl.pallas_call(
        functools.partial(_dispatch_body, bt=bt, lt=lt, n_blocks=n_blocks, mb=mb, n_steps=(nb // bt) * nt),
        grid=(nb // bt, nt),
        in_specs=[pl.BlockSpec((1, 1, 2 * bt * lt), lambda i, t: (i * nt + t, 0, 0), memory_space=pltpu.SMEM),
                  pl.BlockSpec((1, 1, N_EXPERTS + 1), lambda i, t: (0, 0, 0), memory_space=pltpu.SMEM),
                  pl.BlockSpec((bt, lt, D_MODEL), lambda i, t: (i, t, 0)),
                  pl.BlockSpec((bt, 6, D_MODEL), lambda i, t: (i, 0, 0))],
        out_specs=pl.BlockSpec(memory_space=pl.ANY),
        out_shape=jax.ShapeDtypeStruct((n_blocks * mb, D_MODEL), F32),
        scratch_shapes=[pltpu.VMEM((2, bt * lt // 8, 8, D_MODEL), F32), pltpu.VMEM((mb, D_MODEL), F32),
                        pltpu.SemaphoreType.DMA((2,)), pltpu.SemaphoreType.DMA],
        compiler_params=pltpu.CompilerParams(dimension_semantics=("arbitrary", "arbitrary"),
                                             vmem_limit_bytes=VMEM_LIMIT),
        name=f"dispatch_b{bt}",
    )(pos, ztail, x1, mod)


def _experts_body(be_ref, nu_ref, x_ref, wg_ref, wu_ref, wd_ref, o_ref, wg_s, wu_s, wd_s):
    j = pl.program_id(0)

    @pl.when((j == 0) | (be_ref[j] != be_ref[jnp.maximum(j - 1, 0)]))
    def _():
        wg_s[...] = wg_ref[0, 0].astype(BF16)
        wu_s[...] = wu_ref[0, 0].astype(BF16)
        wd_s[...] = wd_ref[0, 0].astype(BF16)

    @pl.when(j < nu_ref[0])
    def _():
        x = x_ref[...].astype(BF16)
        hb = _silu(_dot(x, wg_s[...])) * _dot(x, wu_s[...])
        o_ref[...] = _dot(hb.astype(BF16), wd_s[...])

    @pl.when(j >= nu_ref[0])
    def _():
        o_ref[...] = jnp.zeros_like(o_ref)


def _experts(disp, block_e, n_used, wg, wu, wd, layer, mb):
    n_blocks = disp.shape[0] // mb
    wmap = lambda j, be, nu: (layer, be[j], 0, 0)
    return pl.pallas_call(
        _experts_body,
        grid_spec=pltpu.PrefetchScalarGridSpec(
            num_scalar_prefetch=2,
            grid=(n_blocks,),
            in_specs=[pl.BlockSpec((mb, D_MODEL), lambda j, be, nu: (jnp.minimum(j, nu[0] - 1), 0)),
                      pl.BlockSpec((1, 1, D_MODEL, D_EXPERT), wmap),
                      pl.BlockSpec((1, 1, D_MODEL, D_EXPERT), wmap),
                      pl.BlockSpec((1, 1, D_EXPERT, D_MODEL), wmap)],
            out_specs=pl.BlockSpec((mb, D_MODEL), lambda j, be, nu: (j, 0)),
            scratch_shapes=[pltpu.VMEM((D_MODEL, D_EXPERT), BF16), pltpu.VMEM((D_MODEL, D_EXPERT), BF16),
                            pltpu.VMEM((D_EXPERT, D_MODEL), BF16)]),
        out_shape=jax.ShapeDtypeStruct(disp.shape, F32),
        compiler_params=pltpu.CompilerParams(dimension_semantics=("arbitrary",), vmem_limit_bytes=VMEM_LIMIT),
        name=f"experts_l{layer}_n{n_blocks}",
    )(block_e, n_used, disp, wg, wu, wd)


def _combine_body(pos_ref, pos_next_ref, x1_ref, mod_ref, rinfo_ref, eo_ref, lg_ref, lb_ref, x2_ref, r_scr, sem,
                  *, bt, lt, n_steps):
    tm = bt * lt
    step = pl.program_id(0) * pl.num_programs(1) + pl.program_id(1)
    slot = step % 2

    def gather(p_ref, to_slot):
        def issue(i, carry):
            for r in range(8):
                for k in range(2):
                    row = p_ref[0, 0, 16 * i + 2 * r + k]
                    _row_copy(eo_ref.at[pl.ds(row, 1)], r_scr.at[to_slot, k, i, pl.ds(r, 1)],
                              sem.at[to_slot]).start(priority=k)
            return carry

        lax.fori_loop(0, tm // 8, issue, 0)

    @pl.when(step == 0)
    def _():
        gather(pos_ref, 0)

    @pl.when(step + 1 < n_steps)
    def _():
        gather(pos_next_ref, 1 - slot)

    for k in range(2):
        _row_copy(eo_ref.at[pl.ds(0, tm)], eo_ref.at[pl.ds(0, tm)], sem.at[slot]).wait()

    rinfo = rinfo_ref[...].reshape(tm, 128)
    rows = r_scr[slot]
    moe = (rows[0].reshape(tm, D_MODEL) * rinfo[:, 2:3] + rows[1].reshape(tm, D_MODEL) * rinfo[:, 3:4])
    y = DN_ALPHA * x1_ref[...] + (1.0 + mod_ref[:, 5:6, :]) * moe.reshape(bt, lt, D_MODEL)
    x2_ref[...] = _layer_norm(y, lg_ref[0], lb_ref[0])


def _combine(x1, mod, rinfo, eo, pos, lg, lb, layer, bt, lt):
    nb, lp, _ = x1.shape
    nt = lp // lt
    tok = lambda i, t: (i, t, 0)
    n_steps = (nb // bt) * nt
    return pl.pallas_call(
        functools.partial(_combine_body, bt=bt, lt=lt, n_steps=n_steps),
        grid=(nb // bt, nt),
        in_specs=[pl.BlockSpec((1, 1, 2 * bt * lt), lambda i, t: (i * nt + t, 0, 0), memory_space=pltpu.SMEM),
                  pl.BlockSpec((1, 1, 2 * bt * lt), lambda i, t: (jnp.minimum(i * nt + t + 1, n_steps - 1), 0, 0),
                               memory_space=pltpu.SMEM),
                  pl.BlockSpec((bt, lt, D_MODEL), tok),
                  pl.BlockSpec((bt, 6, D_MODEL), lambda i, t: (i, 0, 0)),
                  pl.BlockSpec((bt, lt, 128), tok),
                  pl.BlockSpec(memory_space=pl.ANY),
                  pl.BlockSpec((1, 1, D_MODEL), lambda i, t: (layer, 0, 0)),
                  pl.BlockSpec((1, 1, D_MODEL), lambda i, t: (layer, 0, 0))],
        out_specs=pl.BlockSpec((bt, lt, D_MODEL), tok),
        out_shape=jax.ShapeDtypeStruct((nb, lp, D_MODEL), F32),
        scratch_shapes=[pltpu.VMEM((2, 2, bt * lt // 8, 8, D_MODEL), F32), pltpu.SemaphoreType.DMA((2,))],
        compiler_params=pltpu.CompilerParams(dimension_semantics=("arbitrary", "arbitrary"),
                                             vmem_limit_bytes=VMEM_LIMIT),
        name=f"combine_l{layer}_b{bt}",
    )(pos, pos, x1, mod, rinfo, eo, lg, lb)


def _hist(state):
    return jnp.pad(state, ((0, 0), (HIST - state.shape[1], 0), (0, 0)))


def _trunk(x, mod, st, p, bt, lt, lv, mix, moe_block, ip_tile):
    nb = x.shape[0]
    new = {key: [] for key in ('n', 'm', 'conv', 'gconv')}
    C = S = None
    for l in range(DEPTH):
        proj, x = _inproj(x, mod[l], p['ln_in_g'], p['ln_in_b'], p['w_in_r'], p['b_in_r'], l, l == 0, *ip_tile)
        m0 = jnp.broadcast_to(st['m'][l][:, :, None, None], (nb, N_HEADS, 1, 128))
        yb, conv, *pre = _mlstm_prep(proj, _hist(st['conv'][l]), p['conv_b_w8'][l], mix['NB'], mix['G'], mix['tl'], lv)
        ya, C, n, m = _mlstm_scan(pre, proj, st['C'], st['n'][l], m0, p['mlstm_norm_g'][l:l + 1],
                                  mix['SB'], mix['G'], mix['tl'], l, C)
        *pre, gconv = _gdn_prep(proj, _hist(st['gconv'][l]), p['conv_c_w8'][l], p['alog_row'][l], p['dtb_row'][l],
                                mix['NB'], mix['G'], mix['tl'], lv)
        yc, S = _gdn_scan(pre, proj, st['S'], p['gdn_norm_g'][l:l + 1], mix['SB'], mix['G'], mix['tl'], l, S)
        x1, rinfo, cnt = _merge(ya, yb, yc, proj, x, mod[l], p['w_br_a'], p['w_br_b'], p['w_br_c'], p['w_out'],
                                p['ln1_g'], p['ln1_b'], p['wr_hi'], p['wr_lo'], p['br'], l, bt, lt)
        pos, block_e, n_used, ztail, n_blocks = _moe_plan(rinfo, cnt, bt * lt, moe_block)
        disp = _dispatch(x1, mod[l], pos, ztail, n_blocks, moe_block, bt, lt)
        eo = _experts(disp, block_e, n_used, p['exp_w_gate'], p['exp_w_up'], p['exp_w_down'], l, moe_block)
        x = _combine(x1, mod[l], rinfo, eo, pos, p['ln2_g'], p['ln2_b'], l, bt, lt)
        new['n'].append(n)
        new['m'].append(m[:, :, 0, 0])
        new['conv'].append(conv)
        new['gconv'].append(gconv)
    return x, dict({key: jnp.stack(val) for key, val in new.items()}, C=C, S=S)


def kernel(x_prompt, x_sample, state_mlstm_C, state_mlstm_n, state_mlstm_m, state_conv, state_gdn_S, state_gdn_conv, c_prompt, c_sample, ln_in_g, ln_in_b, w_ada, b_ada, w_in, b_in, mlstm_norm_g, conv_b_w, conv_c_w, gdn_a_log, gdn_dt_bias, gdn_norm_g, w_br_a, w_br_b, w_br_c, w_out, ln1_g, ln1_b, router_g_w, router_g_b, router_e_w, router_e_b, exp_w_gate, exp_w_up, exp_w_down, ln2_g, ln2_b):
    nbp, lp, _ = x_prompt.shape
    nbs, ls, _ = x_sample.shape
    lsp = 8

    def regroup(a):
        out = jnp.zeros(a.shape[:-1] + (N_PROJ,), a.dtype)
        for src, end, dst in PROJ_SEGMENTS:
            out = out.at[..., dst:dst + end - src].set(a[..., src:end])
        return out

    wr = jnp.concatenate([router_e_w, router_g_w, jnp.zeros((DEPTH, D_MODEL, 128 - N_EXPERTS - N_GROUPS), F32)], axis=-1)
    wr_hi = wr.astype(BF16)
    lane_pad = lambda a: jnp.pad(a, ((0, 0), (4, 128 - 4 - N_HEADS)))[:, None, :]
    p = dict(
        ln_in_g=ln_in_g.reshape(1, D_MODEL), ln_in_b=ln_in_b.reshape(1, D_MODEL),
        w_in_r=_regroup_w_in(w_in), b_in_r=regroup(b_in).reshape(DEPTH, 1, N_PROJ),
        mlstm_norm_g=mlstm_norm_g,
        conv_b_w8=jnp.pad(conv_b_w, ((0, 0), (0, 8 - CONV_B), (0, 0))),
        conv_c_w8=jnp.pad(conv_c_w, ((0, 0), (0, 8 - CONV_C), (0, 0))),
        alog_row=lane_pad(gdn_a_log), dtb_row=lane_pad(gdn_dt_bias), gdn_norm_g=gdn_norm_g,
        w_br_a=w_br_a.astype(BF16), w_br_b=w_br_b.astype(BF16), w_br_c=w_br_c.astype(BF16),
        w_out=w_out.astype(BF16),
        ln1_g=ln1_g.reshape(DEPTH, 1, D_MODEL), ln1_b=ln1_b.reshape(DEPTH, 1, D_MODEL),
        wr_hi=wr_hi, wr_lo=(wr - wr_hi.astype(F32)).astype(BF16),
        br=jnp.concatenate([router_e_b, router_g_b, jnp.zeros((DEPTH, 128 - N_EXPERTS - N_GROUPS), F32)],
                           axis=-1).reshape(DEPTH, 1, 128),
        exp_w_gate=exp_w_gate, exp_w_up=exp_w_up, exp_w_down=exp_w_down,
        ln2_g=ln2_g.reshape(DEPTH, 1, D_MODEL), ln2_b=ln2_b.reshape(DEPTH, 1, D_MODEL),
    )

    mod = _ada(jnp.concatenate([c_prompt, c_sample], axis=0), w_ada, b_ada)
    mod = mod.reshape(DEPTH, nbp + nbs, 6, D_MODEL)

    zeros = lambda *s: jnp.zeros((DEPTH, nbp) + s, F32)
    st_p = {'C': zeros(N_HEADS, DH, DH), 'n': zeros(N_HEADS, DH), 'm': zeros(N_HEADS),
            'conv': zeros(CONV_B - 1, D_MIX), 'S': zeros(N_HEADS, DH, DH), 'gconv': zeros(CONV_C - 1, 3 * D_MIX)}
    y_p, sp = _trunk(x_prompt, mod[:, :nbp], st_p, p, bt=1, lt=512, lv=lp,
                     mix=dict(NB=8, G=1, tl=RB, SB=nbp), moe_block=256, ip_tile=(1, 256))

    st_s = {'C': state_mlstm_C, 'n': state_mlstm_n, 'm': state_mlstm_m, 'conv': state_conv,
            'S': state_gdn_S, 'gconv': state_gdn_conv}
    xs = jnp.pad(x_sample, ((0, 0), (0, lsp - ls), (0, 0)))
    y_s, ss = _trunk(xs, mod[:, nbp:], st_s, p, bt=64, lt=lsp, lv=ls,
                     mix=dict(NB=2, G=RB // lsp, tl=lsp, SB=1), moe_block=128, ip_tile=(32, lsp))
    y_s = y_s[:, :ls]

    return (y_p, y_s, sp['C'], sp['n'], sp['m'], sp['conv'], sp['S'], sp['gconv'],
            ss['C'], ss['n'], ss['m'], ss['conv'], ss['S'], ss['gconv'])
```

```python
import functools

import jax
import jax.numpy as jnp
from jax import lax
from jax.experimental import pallas as pl
from jax.experimental.pallas import tpu as pltpu

F32 = jnp.float32
BF16 = jnp.bfloat16

D_MODEL = 1024
DEPTH = 2
N_HEADS = 4
DH = 128
D_MIX = N_HEADS * DH
N_EXPERTS = 32
EXPERTS_PER_GROUP = 8
N_GROUPS = 4
D_EXPERT = 256
CONV_B = 3
CONV_C = 4
HIST = 8
RB = 64
INV_BLOCK = 8
DN_ALPHA = (2 * DEPTH) ** 0.25
LN_EPS = 1e-5
NORM_EPS = 1e-6
NEG = -1e30

OFF_QKVC = 0
OFF_BCH = 1536
OFF_GTS = 3072
OFF_QKVO = 6144
OFF_Z = 8192
OFF_SA = 8704
OFF_SB = 8832
N_PROJ = 8960
TN_PROJ = 1280
PROJ_SEGMENTS = ((3592, 5128, OFF_QKVC), (2056, 3592, OFF_BCH), (5648, 8720, OFF_GTS), (0, 2048, OFF_QKVO),
                 (5128, 5640, OFF_Z), (2048, 2052, OFF_SA), (5640, 5644, OFF_SA + 4), (2052, 2056, OFF_SB),
                 (5644, 5648, OFF_SB + 4))

VMEM_LIMIT = 52 * 1024 * 1024


def _dot(a, b):
    return jnp.dot(a, b, preferred_element_type=F32)


def _split3(x):
    hi = x.astype(BF16)
    r = x - hi.astype(F32)
    mid = r.astype(BF16)
    lo = (r - mid.astype(F32)).astype(BF16)
    return hi, mid, lo


def _layer_norm(x, g, b):
    mu = jnp.mean(x, axis=-1, keepdims=True)
    xc = x - mu
    var = jnp.mean(xc * xc, axis=-1, keepdims=True)
    return xc * lax.rsqrt(var + LN_EPS) * g + b


def _sigmoid(x):
    return jax.nn.sigmoid(x)


def _silu(x):
    return x * _sigmoid(x)


def _log_sigmoid(x):
    return jnp.minimum(x, 0.0) - jnp.log1p(jnp.exp(-jnp.abs(x)))


def _softplus(x):
    return jnp.maximum(x, 0.0) + jnp.log1p(jnp.exp(-jnp.abs(x)))


def _ada_body(c_ref, w_ref, b_ref, o_ref):
    c = c_ref[...]
    s = _silu(c).astype(BF16)
    o_ref[0] = _dot(s, w_ref[0].astype(BF16)) + b_ref[0]


def _ada(c_all, w_ada, b_ada):
    nb = c_all.shape[0]
    return pl.pallas_call(
        _ada_body,
        grid=(DEPTH, 6),
        in_specs=[pl.BlockSpec((nb, D_MODEL), lambda l, j: (0, 0)),
                  pl.BlockSpec((1, D_MODEL, D_MODEL), lambda l, j: (l, 0, j)),
                  pl.BlockSpec((1, 1, D_MODEL), lambda l, j: (l, 0, j))],
        out_specs=pl.BlockSpec((1, nb, D_MODEL), lambda l, j: (l, 0, j)),
        out_shape=jax.ShapeDtypeStruct((DEPTH, nb, 6 * D_MODEL), F32),
        compiler_params=pltpu.CompilerParams(dimension_semantics=("arbitrary", "arbitrary"),
                                             vmem_limit_bytes=VMEM_LIMIT),
        name="ada",
    )(c_all, w_ada, b_ada.reshape(DEPTH, 1, 6 * D_MODEL))


def _regroup_body(w_ref, o_ref):
    o_ref[...] = jnp.zeros_like(o_ref)
    for src, end, dst in PROJ_SEGMENTS:
        o_ref[0, :, dst:dst + end - src] = w_ref[0, :, src:end].astype(BF16)


def _regroup_w_in(w_in):
    n_in = w_in.shape[-1]
    rows = 256
    return pl.pallas_call(
        _regroup_body,
        grid=(DEPTH, D_MODEL // rows),
        in_specs=[pl.BlockSpec((1, rows, n_in), lambda l, i: (l, i, 0))],
        out_specs=pl.BlockSpec((1, rows, N_PROJ), lambda l, i: (l, i, 0)),
        out_shape=jax.ShapeDtypeStruct((DEPTH, D_MODEL, N_PROJ), BF16),
        compiler_params=pltpu.CompilerParams(dimension_semantics=("arbitrary", "arbitrary"),
                                             vmem_limit_bytes=VMEM_LIMIT),
        name="regroup_w_in",
    )(w_in)


def _inproj_body(x_ref, mod_ref, g_ref, b_ref, w_ref, bias_ref, proj_ref, *rest, apply_ln, bt, lt):
    x = x_ref[...]
    if apply_ln:
        x = _layer_norm(x, g_ref[...], b_ref[...])
        rest[0][...] = x
    u = (x * (1.0 + mod_ref[:, 1:2, :]) + mod_ref[:, 0:1, :]).reshape(bt * lt, D_MODEL).astype(BF16)
    for j in range(N_PROJ // TN_PROJ):
        cs = slice(j * TN_PROJ, (j + 1) * TN_PROJ)
        proj_ref[:, :, cs] = (_dot(u, w_ref[0, :, cs]) + bias_ref[0, :, cs]).reshape(bt, lt, TN_PROJ)


def _inproj(x, mod, ln_g, ln_b, w_r, b_r, layer, apply_ln, bt, lt):
    nb, lp, _ = x.shape
    tok = lambda i, t: (i, t, 0)
    out_shape = [jax.ShapeDtypeStruct((nb, lp, N_PROJ), F32)]
    out_specs = [pl.BlockSpec((bt, lt, N_PROJ), tok)]
    if apply_ln:
        out_shape.append(jax.ShapeDtypeStruct((nb, lp, D_MODEL), F32))
        out_specs.append(pl.BlockSpec((bt, lt, D_MODEL), tok))
    res = pl.pallas_call(
        functools.partial(_inproj_body, apply_ln=apply_ln, bt=bt, lt=lt),
        grid=(nb // bt, lp // lt),
        in_specs=[pl.BlockSpec((bt, lt, D_MODEL), tok),
                  pl.BlockSpec((bt, 6, D_MODEL), lambda i, t: (i, 0, 0)),
                  pl.BlockSpec((1, D_MODEL), lambda i, t: (0, 0)),
                  pl.BlockSpec((1, D_MODEL), lambda i, t: (0, 0)),
                  pl.BlockSpec((1, D_MODEL, N_PROJ), lambda i, t: (layer, 0, 0), pipeline_mode=pl.Buffered(1)),
                  pl.BlockSpec((1, 1, N_PROJ), lambda i, t: (layer, 0, 0))],
        out_specs=out_specs,
        out_shape=out_shape,
        compiler_params=pltpu.CompilerParams(dimension_semantics=("arbitrary", "arbitrary"),
                                             vmem_limit_bytes=VMEM_LIMIT),
        name=f"inproj_l{layer}_b{bt}",
    )(x, mod, ln_g, ln_b, w_r, b_r)
    return (res[0], res[1]) if apply_ln else (res[0], x)


def _conv_taps(xp_s, w_ref, width, tl):
    acc = None
    for j in range(width):
        tap = xp_s[:, pl.ds(HIST - (width - 1) + j, tl), :] * w_ref[j:j + 1, :].reshape(1, 1, -1)
        acc = tap if acc is None else acc + tap
    return acc


def _conv_history(xp_s, hist_ref, prev_ref, n_steps):
    if n_steps > 1:
        @pl.when(pl.program_id(1) == 0)
        def _():
            xp_s[:, 0:HIST, :] = hist_ref[...]

        @pl.when(pl.program_id(1) > 0)
        def _():
            xp_s[:, 0:HIST, :] = prev_ref[...]
    else:
        xp_s[:, 0:HIST, :] = hist_ref[...]


def _heads(x, nb, width):
    return jnp.stack([x[:, :, h * width:(h + 1) * width] for h in range(N_HEADS)],
                     axis=1).reshape(nb * N_HEADS, RB, width)


def _gate_cols(x, nb, lane0):
    return jnp.stack([x[:, :, lane0 + h:lane0 + h + 1] for h in range(N_HEADS)],
                     axis=1).reshape(nb * N_HEADS, RB, 1)


def _gate_rows(x, nb, lane0):
    xt = jnp.swapaxes(x, 1, 2)
    return jnp.stack([xt[:, lane0 + h:lane0 + h + 1, :] for h in range(N_HEADS)],
                     axis=1).reshape(nb * N_HEADS, 1, RB)


def _bmm(a, b):
    return jnp.einsum('nts,nsu->ntu', a.astype(BF16), b.astype(BF16), preferred_element_type=F32)


def _bmm_nt(a, b):
    return jnp.einsum('ntd,nsd->nts', a, b, preferred_element_type=F32)


def _block_masks(tl):
    rr = lax.broadcasted_iota(jnp.int32, (RB, RB), 0)
    cc = lax.broadcasted_iota(jnp.int32, (RB, RB), 1)
    incl = rr >= cc
    if tl < RB:
        incl = incl & ((rr // tl) == (cc // tl))
    return rr, cc, incl


def _seq_cumsum(x, incl, nb):
    tril = jnp.broadcast_to(incl.astype(BF16)[None], (nb, RB, RB))
    hi, mid, lo = _split3(x)
    return _bmm(tril, hi) + _bmm(tril, mid) + _bmm(tril, lo)


def _seq_last(x, nb, G, tl):
    return x.reshape(nb * G, tl, 128)[:, tl - 1:tl, :]


def _seq_rows(x3, nb, tl):
    return jnp.broadcast_to(x3, (x3.shape[0], tl, 128)).reshape(nb, RB, 128)


def _put_chains(ref, val, nb, G, tl):
    val4 = val.reshape(nb, N_HEADS, RB, val.shape[-1])
    for h in range(N_HEADS):
        ref[:, :, h] = val4[:, h].reshape(nb, G, tl, val.shape[-1])


def _put_seq_scalars(ref, x3, nb, G, lane0):
    x4 = x3.reshape(nb, G, 1, 128)
    for h in range(N_HEADS):
        ref[:, :, h] = jnp.broadcast_to(x4[:, :, :, lane0 + h:lane0 + h + 1], (nb, G, 1, 128))


def _step_tiling(nb, lp, NB, G, tl):
    assert G * tl == RB
    if G == 1:
        gx, tlx = 1, NB * RB
    else:
        assert lp == tl
        gx, tlx = NB * G, tl
    return gx, tlx, nb // gx, lp // tlx


def _layered_state(layer, nb, NS, prev):
    st_in = pl.BlockSpec((None, NS, N_HEADS, DH, DH), lambda i, c: (layer, i, 0, 0, 0))
    shape = jax.ShapeDtypeStruct((DEPTH, nb, N_HEADS, DH, DH), F32)
    if layer == 0:
        return st_in, pl.BlockSpec((DEPTH, NS, N_HEADS, DH, DH), lambda i, c: (0, i, 0, 0, 0)), shape, [], []
    return st_in, st_in, shape, [pl.BlockSpec(memory_space=pl.ANY)], [prev]


def _fill_later_layers(ref, when):
    @pl.when(when)
    def _():
        for l in range(1, DEPTH):
            ref[l] = ref[0]


def _mlstm_prep_body(qkvo_ref, bch_ref, prev_ref, sa_ref, sb_ref, cv0_ref, cw_ref,
                     yb_ref, cv_ref, nv_ref, q_ref, kw_ref, v_ref, rows_ref, bl_ref, bc_ref, kn_ref, xp_s,
                     *, NB, G, tl, tlx, NCS, lv):
    cs = pl.program_id(1)
    n = NB * N_HEADS
    lvl = lv - (NCS - 1) * tlx

    if NCS > 1:
        @pl.when(cs == 0)
        def _():
            xp_s[:, 0:HIST, :] = cv0_ref[...]

        @pl.when(cs > 0)
        def _():
            xp_s[:, 0:HIST, :] = prev_ref[:, :, D_MIX:2 * D_MIX] * prev_ref[:, :, 2 * D_MIX:3 * D_MIX]
    else:
        xp_s[:, 0:HIST, :] = cv0_ref[...]
    xp_s[:, HIST:HIST + tlx, :] = bch_ref[:, :, D_MIX:2 * D_MIX] * bch_ref[:, :, 2 * D_MIX:3 * D_MIX]
    yb_ref[...] = (bch_ref[:, :, 0:D_MIX] * _conv_taps(xp_s, cw_ref, CONV_B, tlx)).astype(BF16)

    @pl.when(cs == NCS - 1)
    def _():
        cv_ref[...] = xp_s[:, pl.ds(HIST + lvl - (CONV_B - 1), CONV_B - 1), :]

    i_all = sa_ref[...].reshape(NB, RB, 128)
    f_all = _log_sigmoid(sb_ref[...].reshape(NB, RB, 128))
    if lv < NCS * tlx:
        assert NCS == 1
        valid = (lax.broadcasted_iota(jnp.int32, (NB, RB, 128), 1) % tl) < lv
        i_all = jnp.where(valid, i_all, NEG)
        f_all = jnp.where(valid, f_all, 0.0)
    _, _, incl = _block_masks(tl)
    bcum = _seq_cumsum(f_all, incl, NB)
    blast = _seq_last(bcum, NB, G, tl)
    val = _seq_rows(blast, NB, tl) - bcum + i_all
    bmax = jnp.max(val.reshape(NB * G, tl, 128), axis=1, keepdims=True)
    wk0 = jnp.exp(val - _seq_rows(bmax, NB, tl))

    qkvo = qkvo_ref[...].reshape(NB, RB, 4 * D_MIX)
    q = _heads(qkvo[:, :, 0:D_MIX], NB, DH)
    k = _heads(qkvo[:, :, D_MIX:2 * D_MIX], NB, DH) * (DH ** -0.5)
    v = _heads(qkvo[:, :, 2 * D_MIX:3 * D_MIX], NB, DH)
    qb, kb, vb = q.astype(BF16), k.astype(BF16), v.astype(BF16)

    b_col = _gate_cols(bcum, NB, 0)
    dlog = jnp.where(incl[None], b_col - _gate_rows(bcum, NB, 0) + _gate_rows(i_all, NB, 0), NEG)
    d = jnp.max(dlog, axis=-1, keepdims=True)
    s0 = _bmm_nt(qb, kb) * jnp.exp(dlog - d)
    kw0 = k * _gate_cols(wk0, NB, 0)

    _put_chains(nv_ref, _bmm(s0, vb), NB, G, tl)
    _put_chains(q_ref, qb, NB, G, tl)
    _put_chains(kw_ref, kw0.astype(BF16), NB, G, tl)
    _put_chains(v_ref, vb, NB, G, tl)
    rows = jnp.swapaxes(jnp.concatenate([d, b_col, jnp.sum(s0, axis=-1, keepdims=True),
                                         jnp.zeros((n, RB, 5), F32)], axis=-1), 1, 2).reshape(NB, N_HEADS, 8, RB)
    for h in range(N_HEADS):
        for g in range(G):
            rows_ref[:, g, h] = rows[:, h, :, g * tl:(g + 1) * tl]
    _put_seq_scalars(bl_ref, blast, NB, G, 0)
    _put_seq_scalars(bc_ref, bmax, NB, G, 0)
    kn = jnp.sum(kw0.reshape(n * G, tl, DH), axis=1, keepdims=True).reshape(NB, N_HEADS, G, 1, DH)
    for h in range(N_HEADS):
        kn_ref[:, :, h] = kn[:, h]


def _mlstm_prep(proj, cv0, cw, NB, G, tl, lv):
    nb, lp, _ = proj.shape
    gx, tlx, NI, NCS = _step_tiling(nb, lp, NB, G, tl)
    nbt = nb * lp // RB
    step = lambda i, c: (i * NCS + c, 0, 0, 0, 0)
    chain = lambda last, dt: jax.ShapeDtypeStruct((nbt, G, N_HEADS, tl, last), dt)
    cspec = lambda last: pl.BlockSpec((NB, G, N_HEADS, tl, last), step)
    scal = jax.ShapeDtypeStruct((nbt, G, N_HEADS, 1, 128), F32)
    sspec = pl.BlockSpec((NB, G, N_HEADS, 1, 128), step)
    bch = OFF_BCH // (3 * D_MIX)
    return pl.pallas_call(
        functools.partial(_mlstm_prep_body, NB=NB, G=G, tl=tl, tlx=tlx, NCS=NCS, lv=lv),
        grid=(NI, NCS),
        in_specs=[pl.BlockSpec((gx, tlx, 4 * D_MIX), lambda i, c: (i, c, OFF_QKVO // (4 * D_MIX))),
                  pl.BlockSpec((gx, tlx, 3 * D_MIX), lambda i, c: (i, c, bch)),
                  pl.BlockSpec((gx, HIST, 3 * D_MIX), lambda i, c: (i, jnp.maximum(c * (tlx // HIST) - 1, 0), bch)),
                  pl.BlockSpec((gx, tlx, 128), lambda i, c: (i, c, OFF_SA // 128)),
                  pl.BlockSpec((gx, tlx, 128), lambda i, c: (i, c, OFF_SB // 128)),
                  pl.BlockSpec((gx, HIST, D_MIX), lambda i, c: (i, 0, 0)),
                  pl.BlockSpec((8, D_MIX), lambda i, c: (0, 0))],
        out_specs=[pl.BlockSpec((gx, tlx, D_MIX), lambda i, c: (i, c, 0)),
                   pl.BlockSpec((gx, CONV_B - 1, D_MIX), lambda i, c: (i, 0, 0)),
                   cspec(DH), cspec(DH), cspec(DH), cspec(DH),
                   pl.BlockSpec((NB, G, N_HEADS, 8, tl), step), sspec, sspec, sspec],
        out_shape=[jax.ShapeDtypeStruct((nb, lp, D_MIX), BF16),
                   jax.ShapeDtypeStruct((nb, CONV_B - 1, D_MIX), F32),
                   chain(DH, F32), chain(DH, BF16), chain(DH, BF16), chain(DH, BF16),
                   jax.ShapeDtypeStruct((nbt, G, N_HEADS, 8, tl), F32),
                   scal, scal, scal],
        scratch_shapes=[pltpu.VMEM((gx, HIST + tlx, D_MIX), F32)],
        compiler_params=pltpu.CompilerParams(dimension_semantics=("arbitrary", "arbitrary"),
                                             vmem_limit_bytes=VMEM_LIMIT),
        name=f"mlstm_prep_g{G}",
    )(proj, proj, proj, proj, proj, cv0, cw)


def _mlstm_scan_body(nv_ref, q_ref, kw_ref, v_ref, rows_ref, bl_ref, bc_ref, kn_ref, o_ref, C0_ref, n0_ref, m0_ref,
                     ng_ref, *rest, NS, tl, NC, first):
    ya_ref, C_out, n_ref, m_ref = rest[-4:]
    C_ref = C_out.at[0] if first else C_out
    c = pl.program_id(1)
    n = NS * N_HEADS

    @pl.when(c == 0)
    def _():
        C_ref[...] = C0_ref[...]
        n_ref[...] = n0_ref[...]
        m_ref[...] = m0_ref[...]

    chains = lambda ref: ref[...].reshape(n, ref.shape[-2], ref.shape[-1])
    C = C_ref[...].reshape(n, DH, DH)
    nvec = n_ref[...].reshape(n, 1, DH)
    m_prev = chains(m_ref)[:, :, 0:1]
    rows = chains(rows_ref)
    d, b, ds0 = rows[:, 0:1, :], rows[:, 1:2, :], rows[:, 2:3, :]
    qb = chains(q_ref)

    m_inter = b + m_prev
    m_t = jnp.maximum(m_inter, d)
    f = jnp.exp(d - m_t)
    inter = jnp.exp(m_inter - m_t)
    qn = jnp.einsum('nod,ntd->not', nvec.astype(BF16), qb, preferred_element_type=F32)
    den = f * ds0 + inter * qn
    scale_t = 1.0 / jnp.maximum(jnp.abs(den), jnp.exp(-m_t))
    fi = jnp.swapaxes(jnp.concatenate([f * scale_t, inter * scale_t, jnp.zeros((n, 6, tl), F32)], axis=1), 1, 2)
    qC = jnp.einsum('ntd,nde->nte', qb, C.astype(BF16), preferred_element_type=F32)
    hh = fi[:, :, 0:1] * chains(nv_ref) + fi[:, :, 1:2] * qC

    m_new = m_t[:, :, tl - 1:tl]
    decay = jnp.exp(chains(bl_ref)[:, :, 0:1] + m_prev - m_new)
    scale = jnp.exp(chains(bc_ref)[:, :, 0:1] - m_new)
    kv = jnp.einsum('ntd,nte->nde', chains(kw_ref), chains(v_ref), preferred_element_type=F32)
    C_ref[...] = (decay * C + scale * kv).reshape(NS, N_HEADS, DH, DH)
    n_ref[...] = (decay * nvec + scale * chains(kn_ref)).reshape(NS, N_HEADS, DH)
    m_ref[...] = jnp.broadcast_to(m_new, (n, 1, 128)).reshape(NS, N_HEADS, 1, 128)
    if first:
        _fill_later_layers(C_out, c == NC - 1)

    mu = jnp.mean(hh, axis=-1, keepdims=True)
    hc = hh - mu
    hn = (hc * lax.rsqrt(jnp.mean(hc * hc, axis=-1, keepdims=True) + LN_EPS)).reshape(NS, N_HEADS, tl, DH)
    for h in range(N_HEADS):
        hs = slice(h * DH, (h + 1) * DH)
        ya_ref[:, :, hs] = (_sigmoid(o_ref[:, :, hs]) * hn[:, h] * ng_ref[:, hs]).astype(BF16)


def _mlstm_scan(pre, proj, C0, n0, m0, ng, SB, G, tl, layer, C_prev):
    nb, lp, _ = proj.shape
    NS = SB * G
    NI, NC = nb // NS, lp // tl
    six = lambda a: a.reshape((NI * SB, NC) + a.shape[1:])
    spec6 = lambda a: pl.BlockSpec((SB, 1) + a.shape[1:], lambda i, c: (i, c, 0, 0, 0, 0))
    seq4 = lambda i, c: (i, 0, 0, 0)
    st_in, st_out, st_shape, extra_in, extra_args = _layered_state(layer, nb, NS, C_prev)
    n_in = len(pre) + 5
    return pl.pallas_call(
        functools.partial(_mlstm_scan_body, NS=NS, tl=tl, NC=NC, first=layer == 0),
        grid=(NI, NC),
        in_specs=[spec6(a) for a in pre] + [
            pl.BlockSpec((NS, tl, D_MIX), lambda i, c: (i, c, (OFF_QKVO + 3 * D_MIX) // D_MIX)),
            st_in,
            pl.BlockSpec((NS, N_HEADS, DH), lambda i, c: (i, 0, 0)),
            pl.BlockSpec((NS, N_HEADS, 1, 128), seq4),
            pl.BlockSpec((1, D_MIX), lambda i, c: (0, 0))] + extra_in,
        out_specs=[pl.BlockSpec((NS, tl, D_MIX), lambda i, c: (i, c, 0)),
                   st_out,
                   pl.BlockSpec((NS, N_HEADS, DH), lambda i, c: (i, 0, 0)),
                   pl.BlockSpec((NS, N_HEADS, 1, 128), seq4)],
        out_shape=[jax.ShapeDtypeStruct((nb, lp, D_MIX), BF16),
                   st_shape,
                   jax.ShapeDtypeStruct((nb, N_HEADS, DH), F32),
                   jax.ShapeDtypeStruct((nb, N_HEADS, 1, 128), F32)],
        input_output_aliases={n_in: 1} if extra_in else {},
        compiler_params=pltpu.CompilerParams(dimension_semantics=("arbitrary", "arbitrary"),
                                             vmem_limit_bytes=VMEM_LIMIT),
        name=f"mlstm_scan_g{G}",
    )(*[six(a) for a in pre], proj, C0, n0, m0, ng, *extra_args)


def _unit_lower_inverse(n, rr, cc, tl):
    eye = (rr == cc).astype(F32)[None]
    p = jnp.where(((rr // INV_BLOCK) == (cc // INV_BLOCK))[None], n, 0.0)
    x = eye + p
    b = 2
    while b < INV_BLOCK:
        p = _bmm(p, p)
        x = x + _bmm(x, p)
        b *= 2
    b = INV_BLOCK
    while b < tl:
        off = jnp.where((((rr // (2 * b)) == (cc // (2 * b))) & ((rr // b) != (cc // b)))[None], n, 0.0)
        x = x + _bmm(x, _bmm(off, x))
        b *= 2
    return x


def _gdn_prep_body(x_ref, prev_ref, sa_ref, sb_ref, gc0_ref, cw_ref, alog_ref, dtb_ref,
                   u_ref, w_ref, qg_ref, kd_ref, qkm_ref, eg_ref, gcs_ref, xp_s, *, NB, G, tl, tlx, NCS, lv):
    cs = pl.program_id(1)
    lvl = lv - (NCS - 1) * tlx

    _conv_history(xp_s, gc0_ref, prev_ref, NCS)
    xp_s[:, HIST:HIST + tlx, :] = x_ref[...]
    qkv = _silu(_conv_taps(xp_s, cw_ref, CONV_C, tlx)).reshape(NB, RB, 3 * D_MIX)

    @pl.when(cs == NCS - 1)
    def _():
        gcs_ref[...] = xp_s[:, pl.ds(HIST + lvl - (CONV_C - 1), CONV_C - 1), :]

    beta_all = _sigmoid(sa_ref[...].reshape(NB, RB, 128))
    g_all = -jnp.exp(alog_ref[...]) * _softplus(sb_ref[...].reshape(NB, RB, 128) + dtb_ref[...])
    if lv < NCS * tlx:
        assert NCS == 1
        valid = (lax.broadcasted_iota(jnp.int32, (NB, RB, 128), 1) % tl) < lv
        beta_all = jnp.where(valid, beta_all, 0.0)
        g_all = jnp.where(valid, g_all, 0.0)
    rr, cc, incl = _block_masks(tl)
    diag = rr == cc
    gam = _seq_cumsum(g_all, incl, NB)
    glast = _seq_last(gam, NB, G, tl)
    gcol = _gate_cols(gam, NB, 4)
    bcol = _gate_cols(beta_all, NB, 4)
    egcol = _gate_cols(jnp.exp(gam), NB, 4)
    kdcol = _gate_cols(jnp.exp(_seq_rows(glast, NB, tl) - gam), NB, 4)

    q = _heads(qkv[:, :, 0:D_MIX], NB, DH)
    k = _heads(qkv[:, :, D_MIX:2 * D_MIX], NB, DH)
    v = _heads(qkv[:, :, 2 * D_MIX:3 * D_MIX], NB, DH)
    q = q * lax.rsqrt(jnp.sum(q * q, axis=-1, keepdims=True) + NORM_EPS) * (DH ** -0.5)
    k = k * lax.rsqrt(jnp.sum(k * k, axis=-1, keepdims=True) + NORM_EPS)
    qb, kb = q.astype(BF16), k.astype(BF16)

    dmat = jnp.exp(jnp.where(incl[None], gcol - _gate_rows(gam, NB, 4), NEG))
    nmat = jnp.where(diag[None], 0.0, -(bcol * _bmm_nt(kb, kb) * dmat))
    rhs = jnp.concatenate([bcol * v, (bcol * egcol) * k], axis=-1)
    sol = _bmm(_unit_lower_inverse(nmat, rr, cc, tl), rhs)
    qkm = (_bmm_nt(qb, kb) * dmat).astype(BF16).reshape(NB, N_HEADS, RB, RB)

    _put_chains(u_ref, sol[:, :, 0:DH], NB, G, tl)
    _put_chains(w_ref, sol[:, :, DH:2 * DH].astype(BF16), NB, G, tl)
    _put_chains(qg_ref, (q * egcol).astype(BF16), NB, G, tl)
    _put_chains(kd_ref, (k * kdcol).astype(BF16), NB, G, tl)
    _put_seq_scalars(eg_ref, jnp.exp(glast), NB, G, 4)
    for h in range(N_HEADS):
        for g in range(G):
            qkm_ref[:, g, h] = qkm[:, h, g * tl:(g + 1) * tl, g * tl:(g + 1) * tl]


def _gdn_prep(proj, gc0, cw, alog, dtb, NB, G, tl, lv):
    nb, lp, _ = proj.shape
    assert tl % INV_BLOCK == 0 and (tl // INV_BLOCK) & (tl // INV_BLOCK - 1) == 0
    gx, tlx, NI, NCS = _step_tiling(nb, lp, NB, G, tl)
    nbt = nb * lp // RB
    step = lambda i, c: (i * NCS + c, 0, 0, 0, 0)
    par = lambda i, c: (0, 0)
    chain = lambda last, dt: jax.ShapeDtypeStruct((nbt, G, N_HEADS, tl, last), dt)
    cspec = lambda last: pl.BlockSpec((NB, G, N_HEADS, tl, last), step)
    return pl.pallas_call(
        functools.partial(_gdn_prep_body, NB=NB, G=G, tl=tl, tlx=tlx, NCS=NCS, lv=lv),
        grid=(NI, NCS),
        in_specs=[pl.BlockSpec((gx, tlx, 3 * D_MIX), lambda i, c: (i, c, OFF_QKVC // (3 * D_MIX))),
                  pl.BlockSpec((gx, HIST, 3 * D_MIX),
                               lambda i, c: (i, jnp.maximum(c * (tlx // HIST) - 1, 0), OFF_QKVC // (3 * D_MIX))),
                  pl.BlockSpec((gx, tlx, 128), lambda i, c: (i, c, OFF_SA // 128)),
                  pl.BlockSpec((gx, tlx, 128), lambda i, c: (i, c, OFF_SB // 128)),
                  pl.BlockSpec((gx, HIST, 3 * D_MIX), lambda i, c: (i, 0, 0)),
                  pl.BlockSpec((8, 3 * D_MIX), par),
                  pl.BlockSpec((1, 128), par),
                  pl.BlockSpec((1, 128), par)],
        out_specs=[cspec(DH), cspec(DH), cspec(DH), cspec(DH), cspec(tl),
                   pl.BlockSpec((NB, G, N_HEADS, 1, 128), step),
                   pl.BlockSpec((gx, CONV_C - 1, 3 * D_MIX), lambda i, c: (i, 0, 0))],
        out_shape=[chain(DH, F32), chain(DH, BF16), chain(DH, BF16), chain(DH, BF16), chain(tl, BF16),
                   jax.ShapeDtypeStruct((nbt, G, N_HEADS, 1, 128), F32),
                   jax.ShapeDtypeStruct((nb, CONV_C - 1, 3 * D_MIX), F32)],
        scratch_shapes=[pltpu.VMEM((gx, HIST + tlx, 3 * D_MIX), F32)],
        compiler_params=pltpu.CompilerParams(dimension_semantics=("arbitrary", "arbitrary"),
                                             vmem_limit_bytes=VMEM_LIMIT),
        name=f"gdn_prep_g{G}",
    )(proj, proj, proj, proj, gc0, cw, alog, dtb)


def _gdn_scan_body(u_ref, w_ref, qg_ref, kd_ref, qkm_ref, eg_ref, z_ref, S0_ref, gng_ref, *rest,
                   NS, tl, NC, first):
    yc_ref, S_out = rest[-2:]
    S_ref = S_out.at[0] if first else S_out
    c = pl.program_id(1)
    n = NS * N_HEADS

    @pl.when(c == 0)
    def _():
        S_ref[...] = S0_ref[...]

    S = S_ref[...].reshape(n, DH, DH)
    Sb = S.astype(BF16)
    chains = lambda ref: ref[...].reshape(n, tl, ref.shape[-1])
    v_new = chains(u_ref) - jnp.einsum('ntd,nde->nte', chains(w_ref), Sb, preferred_element_type=F32)
    vnb = v_new.astype(BF16)
    o = (jnp.einsum('ntd,nde->nte', chains(qg_ref), Sb, preferred_element_type=F32)
         + jnp.einsum('nts,nse->nte', chains(qkm_ref), vnb, preferred_element_type=F32))
    eg = eg_ref[...].reshape(n, 1, 128)[:, :, 0:1]
    S_new = eg * S + jnp.einsum('ntd,nte->nde', chains(kd_ref), vnb, preferred_element_type=F32)
    S_ref[...] = S_new.reshape(NS, N_HEADS, DH, DH)
    if first:
        _fill_later_layers(S_out, c == NC - 1)

    on = (o * lax.rsqrt(jnp.mean(o * o, axis=-1, keepdims=True) + NORM_EPS) * gng_ref[...]).reshape(NS, N_HEADS, tl, DH)
    for h in range(N_HEADS):
        yc_ref[:, :, h * DH:(h + 1) * DH] = (on[:, h] * _silu(z_ref[:, :, h * DH:(h + 1) * DH])).astype(BF16)


def _gdn_scan(pre, proj, S0, gng, SB, G, tl, layer, S_prev):
    nb, lp, _ = proj.shape
    NS = SB * G
    NI, NC = nb // NS, lp // tl
    six = lambda a: a.reshape((NI * SB, NC) + a.shape[1:])
    cspec = lambda last: pl.BlockSpec((SB, 1, G, N_HEADS, tl, last), lambda i, c: (i, c, 0, 0, 0, 0))
    u, w, qg, kd, qkm, eg = (six(a) for a in pre)
    st_in, st_out, st_shape, extra_in, extra_args = _layered_state(layer, nb, NS, S_prev)
    return pl.pallas_call(
        functools.partial(_gdn_scan_body, NS=NS, tl=tl, NC=NC, first=layer == 0),
        grid=(NI, NC),
        in_specs=[cspec(DH), cspec(DH), cspec(DH), cspec(DH), cspec(tl),
                  pl.BlockSpec((SB, 1, G, N_HEADS, 1, 128), lambda i, c: (i, c, 0, 0, 0, 0)),
                  pl.BlockSpec((NS, tl, D_MIX), lambda i, c: (i, c, OFF_Z // D_MIX)),
                  st_in,
                  pl.BlockSpec((1, DH), lambda i, c: (0, 0))] + extra_in,
        out_specs=[pl.BlockSpec((NS, tl, D_MIX), lambda i, c: (i, c, 0)), st_out],
        out_shape=[jax.ShapeDtypeStruct((nb, lp, D_MIX), BF16), st_shape],
        input_output_aliases={9: 1} if extra_in else {},
        compiler_params=pltpu.CompilerParams(dimension_semantics=("arbitrary", "arbitrary"),
                                             vmem_limit_bytes=VMEM_LIMIT),
        name=f"gdn_scan_g{G}",
    )(u, w, qg, kd, qkm, eg, proj, S0, gng, *extra_args)


def _route(rl):
    lane = lax.broadcasted_iota(jnp.int32, rl.shape, 1).astype(F32)
    is_g = (lane >= N_EXPERTS) & (lane < N_EXPERTS + N_GROUPS)
    gl = jnp.where(is_g, rl, NEG)
    gmax = jnp.max(gl, axis=-1, keepdims=True)
    grp = jnp.min(jnp.where(gl == gmax, lane - N_EXPERTS, 4.0 * N_EXPERTS), axis=-1, keepdims=True)
    p_grp = 1.0 / jnp.sum(jnp.where(is_g, jnp.exp(gl - gmax), 0.0), axis=-1, keepdims=True)
    lo = grp * EXPERTS_PER_GROUP
    in_grp = (lane >= lo) & (lane < lo + EXPERTS_PER_GROUP)
    el = jnp.where(in_grp, rl, NEG)
    m1 = jnp.max(el, axis=-1, keepdims=True)
    i1 = jnp.min(jnp.where(el == m1, lane, 4.0 * N_EXPERTS), axis=-1, keepdims=True)
    el2 = jnp.where(lane == i1, NEG, el)
    m2 = jnp.max(el2, axis=-1, keepdims=True)
    i2 = jnp.min(jnp.where(el2 == m2, lane, 4.0 * N_EXPERTS), axis=-1, keepdims=True)
    e2 = jnp.exp(m2 - m1)
    w1 = p_grp / (1.0 + e2)
    w2 = p_grp * e2 / (1.0 + e2)
    return i1, i2, w1, w2


def _merge_body(ya_ref, yb_ref, yc_ref, ga_ref, gb_ref, gc_ref, x_ref, mod_ref, wa_ref, wb_ref, wc_ref, wo_ref,
                lg_ref, lb_ref, wrh_ref, wrl_ref, br_ref, cnt0_ref, x1_ref, rinfo_ref, cnt_ref, *, bt, lt):
    tm = bt * lt

    @pl.when((pl.program_id(0) == 0) & (pl.program_id(1) == 0))
    def _():
        cnt_ref[...] = cnt0_ref[...]

    def r2(ref):
        return ref[...].reshape(tm, ref.shape[-1])

    merged = (_sigmoid(r2(ga_ref)) * _dot(r2(ya_ref), wa_ref[0])
              + _sigmoid(r2(gb_ref)) * _dot(r2(yb_ref), wb_ref[0])
              + _sigmoid(r2(gc_ref)) * _dot(r2(yc_ref), wc_ref[0]))
    out = _dot(merged.astype(BF16), wo_ref[0])
    y = DN_ALPHA * x_ref[...] + (1.0 + mod_ref[:, 2:3, :]) * out.reshape(bt, lt, D_MODEL)
    x1 = _layer_norm(y, lg_ref[0], lb_ref[0])
    x1_ref[...] = x1
    u2 = (x1 * (1.0 + mod_ref[:, 4:5, :]) + mod_ref[:, 3:4, :]).reshape(tm, D_MODEL)
    hi = u2.astype(BF16)
    lo = (u2 - hi.astype(F32)).astype(BF16)
    rl = _dot(hi, wrh_ref[0]) + _dot(lo, wrh_ref[0]) + _dot(hi, wrl_ref[0]) + br_ref[0]
    i1, i2, w1, w2 = _route(rl)
    lane = lax.broadcasted_iota(jnp.int32, (tm, 128), 1).astype(F32)
    onehot = jnp.where((lane == i1) | (lane == i2), 1.0, 0.0)
    rr = lax.broadcasted_iota(jnp.int32, (tm, tm), 0)
    cc = lax.broadcasted_iota(jnp.int32, (tm, tm), 1)
    before = _dot((rr > cc).astype(BF16), onehot.astype(BF16)) + cnt_ref[0:1, :]
    rank1 = jnp.sum(jnp.where(lane == i1, before, 0.0), axis=-1, keepdims=True)
    rank2 = jnp.sum(jnp.where(lane == i2, before, 0.0), axis=-1, keepdims=True)
    cnt_ref[0:1, :] += jnp.sum(onehot, axis=0, keepdims=True)
    rinfo = jnp.zeros((tm, 128), F32)
    for k, val in enumerate((i1, i2, w1, w2, rank1, rank2)):
        rinfo = jnp.where(lane == k, val, rinfo)
    rinfo_ref[...] = rinfo.reshape(bt, lt, 128)


def _merge(ya, yb, yc, proj, x, mod, wa, wb, wc, wo, lg, lb, wrh, wrl, br, cnt0, layer, bt, lt):
    nb, lp, _ = x.shape
    tok = lambda i, t: (i, t, 0)
    wsp = lambda shape: pl.BlockSpec((1,) + shape, lambda i, t: (layer, 0, 0))
    g0 = OFF_GTS // D_MODEL
    return pl.pallas_call(
        functools.partial(_merge_body, bt=bt, lt=lt),
        grid=(nb // bt, lp // lt),
        in_specs=[pl.BlockSpec((bt, lt, D_MIX), tok),
                  pl.BlockSpec((bt, lt, D_MIX), tok),
                  pl.BlockSpec((bt, lt, D_MIX), tok),
                  pl.BlockSpec((bt, lt, D_MODEL), lambda i, t: (i, t, g0)),
                  pl.BlockSpec((bt, lt, D_MODEL), lambda i, t: (i, t, g0 + 1)),
                  pl.BlockSpec((bt, lt, D_MODEL), lambda i, t: (i, t, g0 + 2)),
                  pl.BlockSpec((bt, lt, D_MODEL), tok),
                  pl.BlockSpec((bt, 6, D_MODEL), lambda i, t: (i, 0, 0)),
                  wsp((D_MIX, D_MODEL)), wsp((D_MIX, D_MODEL)), wsp((D_MIX, D_MODEL)), wsp((D_MODEL, D_MODEL)),
                  wsp((1, D_MODEL)), wsp((1, D_MODEL)),
                  wsp((D_MODEL, 128)), wsp((D_MODEL, 128)), wsp((1, 128)),
                  pl.BlockSpec((8, 128), lambda i, t: (0, 0))],
        out_specs=[pl.BlockSpec((bt, lt, D_MODEL), tok),
                   pl.BlockSpec((bt, lt, 128), tok),
                   pl.BlockSpec((8, 128), lambda i, t: (0, 0))],
        out_shape=[jax.ShapeDtypeStruct((nb, lp, D_MODEL), F32),
                   jax.ShapeDtypeStruct((nb, lp, 128), F32),
                   jax.ShapeDtypeStruct((8, 128), F32)],
        compiler_params=pltpu.CompilerParams(dimension_semantics=("arbitrary", "arbitrary"),
                                             vmem_limit_bytes=VMEM_LIMIT),
        name=f"merge_l{layer}_b{bt}",
    )(ya, yb, yc, proj, proj, proj, x, mod, wa, wb, wc, wo, lg, lb, wrh, wrl, br, cnt0)


def _moe_plan(cnt, cnt_first, n_tok, mb):
    n_blocks = 2 * n_tok // mb + N_EXPERTS
    counts = cnt[0, :N_EXPERTS].astype(jnp.int32)
    nblk = (counts + mb - 1) // mb
    pend = jnp.cumsum(nblk)
    block_e = jnp.sum(pend[None, :] <= jnp.arange(n_blocks, dtype=jnp.int32)[:, None], axis=1)
    block_e = jnp.minimum(block_e, N_EXPERTS - 1).astype(jnp.int32)
    n_used = pend[N_EXPERTS - 1:].astype(jnp.int32)
    first_row = (pend - nblk) * mb
    lo = jnp.minimum((first_row + cnt_first[0, :N_EXPERTS].astype(jnp.int32)) // mb, pend - 1)
    ztail = jnp.concatenate([jnp.where(nblk > 0, lo, 0), jnp.where(nblk > 0, pend - 1, -1), n_used])
    return first_row, block_e, n_used, ztail.astype(jnp.int32).reshape(1, 1, 2 * N_EXPERTS + 1), n_blocks


def _dispatch_rows(rinfo, first_row, tm):
    nb, lp, _ = rinfo.shape
    expert = rinfo[..., 0:2].astype(jnp.int32)
    rank = rinfo[..., 4:6].astype(jnp.int32)
    ids = jnp.arange(N_EXPERTS, dtype=jnp.int32)
    start = jnp.sum(jnp.where(expert[..., None] == ids, first_row, 0), axis=-1)
    return (start + rank).reshape(nb * lp // tm, 1, 2 * tm)


def _row_copy(src, dst, sem):
    return pltpu.make_async_copy(src, dst, sem)


def _dispatch_body(pos_ref, ztail_ref, x1_ref, mod_ref, *rest, bt, lt, n_blocks, mb, n_steps, first):
    disp_ref, u_scr, z_scr, sem, zsem = rest[-5:]
    tm = bt * lt

    @pl.when((pl.program_id(0) == 0) & (pl.program_id(1) == 0) & first)
    def _():
        z_scr[...] = jnp.zeros_like(z_scr)

        def zero_block(j):
            return _row_copy(z_scr, disp_ref.at[pl.ds(pl.multiple_of(j * mb, mb), mb)], zsem)

        def for_zeroed_blocks(fn):
            for e in range(N_EXPERTS):
                lax.fori_loop(ztail_ref[0, 0, e], ztail_ref[0, 0, N_EXPERTS + e] + 1, fn, 0)
            lax.fori_loop(ztail_ref[0, 0, 2 * N_EXPERTS], n_blocks, fn, 0)

        for_zeroed_blocks(lambda j, c: (zero_block(j).start(), c)[1])
        for_zeroed_blocks(lambda j, c: (zero_block(j).wait(), c)[1])

    step = pl.program_id(0) * pl.num_programs(1) + pl.program_id(1)
    slot = step % 2
    u2 = x1_ref[...] * (1.0 + mod_ref[:, 4:5, :]) + mod_ref[:, 3:4, :]
    u_scr[slot] = u2.reshape(tm // 8, 8, D_MODEL)

    def issue(i, carry):
        for r in range(8):
            for k in range(2):
                row = pos_ref[0, 0, 16 * i + 2 * r + k]
                _row_copy(u_scr.at[slot, i, pl.ds(r, 1)], disp_ref.at[pl.ds(row, 1)],
                          sem.at[slot]).start(priority=k)
        return carry

    lax.fori_loop(0, tm // 8, issue, 0)

    def drain(which):
        for k in range(2):
            _row_copy(disp_ref.at[pl.ds(0, tm)], disp_ref.at[pl.ds(0, tm)], sem.at[which]).wait()

    @pl.when(step > 0)
    def _():
        drain(1 - slot)

    @pl.when(step == n_steps - 1)
    def _():
        drain(slot)


def _dispatch(x1, mod, pos, ztail, n_blocks, mb, bt, lt, buf):
    nb, lp, _ = x1.shape
    nt = lp // lt
    extra_in, extra_args = ([], []) if buf is None else ([pl.BlockSpec(memory_space=pl.ANY)], [buf])
    return pl.pallas_call(
        functools.partial(_dispatch_body, bt=bt, lt=lt, n_blocks=n_blocks, mb=mb, n_steps=(nb // bt) * nt,
                          first=buf is None),
        grid=(nb // bt, nt),
        in_specs=[pl.BlockSpec((1, 1, 2 * bt * lt), lambda i, t: (i * nt + t, 0, 0), memory_space=pltpu.SMEM),
                  pl.BlockSpec((1, 1, 2 * N_EXPERTS + 1), lambda i, t: (0, 0, 0), memory_space=pltpu.SMEM),
                  pl.BlockSpec((bt, lt, D_MODEL), lambda i, t: (i, t, 0)),
                  pl.BlockSpec((bt, 6, D_MODEL), lambda i, t: (i, 0, 0))] + extra_in,
        out_specs=pl.BlockSpec(memory_space=pl.ANY),
        out_shape=jax.ShapeDtypeStruct((n_blocks * mb, D_MODEL), F32),
        scratch_shapes=[pltpu.VMEM((2, bt * lt // 8, 8, D_MODEL), F32), pltpu.VMEM((mb, D_MODEL), F32),
                        pltpu.SemaphoreType.DMA((2,)), pltpu.SemaphoreType.DMA],
        compiler_params=pltpu.CompilerParams(dimension_semantics=("arbitrary", "arbitrary"),
                                             vmem_limit_bytes=VMEM_LIMIT),
        input_output_aliases={4: 0} if extra_in else {},
        name=f"dispatch_b{bt}",
    )(pos, ztail, x1, mod, *extra_args)


def _experts_body(be_ref, nu_ref, x_ref, wg_ref, wu_ref, wd_ref, o_ref, wg_s, wu_s, wd_s):
    j = pl.program_id(0)

    @pl.when((j == 0) | (be_ref[j] != be_ref[jnp.maximum(j - 1, 0)]))
    def _():
        wg_s[...] = wg_ref[0, 0].astype(BF16)
        wu_s[...] = wu_ref[0, 0].astype(BF16)
        wd_s[...] = wd_ref[0, 0].astype(BF16)

    @pl.when(j < nu_ref[0])
    def _():
        x = x_ref[...].astype(BF16)
        hb = _silu(_dot(x, wg_s[...])) * _dot(x, wu_s[...])
        o_ref[...] = _dot(hb.astype(BF16), wd_s[...])

    @pl.when(j >= nu_ref[0])
    def _():
        o_ref[...] = jnp.zeros_like(o_ref)


def _experts(disp, block_e, n_used, wg, wu, wd, layer, mb):
    n_blocks = disp.shape[0] // mb
    wmap = lambda j, be, nu: (layer, be[j], 0, 0)
    return pl.pallas_call(
        _experts_body,
        grid_spec=pltpu.PrefetchScalarGridSpec(
            num_scalar_prefetch=2,
            grid=(n_blocks,),
            in_specs=[pl.BlockSpec((mb, D_MODEL), lambda j, be, nu: (jnp.minimum(j, nu[0] - 1), 0)),
                      pl.BlockSpec((1, 1, D_MODEL, D_EXPERT), wmap),
                      pl.BlockSpec((1, 1, D_MODEL, D_EXPERT), wmap),
                      pl.BlockSpec((1, 1, D_EXPERT, D_MODEL), wmap)],
            out_specs=pl.BlockSpec((mb, D_MODEL), lambda j, be, nu: (j, 0)),
            scratch_shapes=[pltpu.VMEM((D_MODEL, D_EXPERT), BF16), pltpu.VMEM((D_MODEL, D_EXPERT), BF16),
                            pltpu.VMEM((D_EXPERT, D_MODEL), BF16)]),
        out_shape=jax.ShapeDtypeStruct(disp.shape, F32),
        compiler_params=pltpu.CompilerParams(dimension_semantics=("arbitrary",), vmem_limit_bytes=VMEM_LIMIT),
        name=f"experts_l{layer}_n{n_blocks}",
    )(block_e, n_used, disp, wg, wu, wd)


def _combine_body(pos_ref, pos_next_ref, x1_ref, mod_ref, rinfo_ref, eo_ref, lg_ref, lb_ref, x2_ref, r_scr, sem,
                  *, bt, lt, n_steps):
    tm = bt * lt
    step = pl.program_id(0) * pl.num_programs(1) + pl.program_id(1)
    slot = step % 2

    def gather(p_ref, to_slot):
        def issue(i, carry):
            for r in range(8):
                for k in range(2):
                    row = p_ref[0, 0, 16 * i + 2 * r + k]
                    _row_copy(eo_ref.at[pl.ds(row, 1)], r_scr.at[to_slot, k, i, pl.ds(r, 1)],
                              sem.at[to_slot]).start(priority=k)
            return carry

        lax.fori_loop(0, tm // 8, issue, 0)

    @pl.when(step == 0)
    def _():
        gather(pos_ref, 0)

    @pl.when(step + 1 < n_steps)
    def _():
        gather(pos_next_ref, 1 - slot)

    for k in range(2):
        _row_copy(eo_ref.at[pl.ds(0, tm)], eo_ref.at[pl.ds(0, tm)], sem.at[slot]).wait()

    rinfo = rinfo_ref[...].reshape(tm, 128)
    rows = r_scr[slot]
    moe = (rows[0].reshape(tm, D_MODEL) * rinfo[:, 2:3] + rows[1].reshape(tm, D_MODEL) * rinfo[:, 3:4])
    y = DN_ALPHA * x1_ref[...] + (1.0 + mod_ref[:, 5:6, :]) * moe.reshape(bt, lt, D_MODEL)
    x2_ref[...] = _layer_norm(y, lg_ref[0], lb_ref[0])


def _combine(x1, mod, rinfo, eo, pos, lg, lb, layer, bt, lt):
    nb, lp, _ = x1.shape
    nt = lp // lt
    tok = lambda i, t: (i, t, 0)
    n_steps = (nb // bt) * nt
    return pl.pallas_call(
        functools.partial(_combine_body, bt=bt, lt=lt, n_steps=n_steps),
        grid=(nb // bt, nt),
        in_specs=[pl.BlockSpec((1, 1, 2 * bt * lt), lambda i, t: (i * nt + t, 0, 0), memory_space=pltpu.SMEM),
                  pl.BlockSpec((1, 1, 2 * bt * lt), lambda i, t: (jnp.minimum(i * nt + t + 1, n_steps - 1), 0, 0),
                               memory_space=pltpu.SMEM),
                  pl.BlockSpec((bt, lt, D_MODEL), tok),
                  pl.BlockSpec((bt, 6, D_MODEL), lambda i, t: (i, 0, 0)),
                  pl.BlockSpec((bt, lt, 128), tok),
                  pl.BlockSpec(memory_space=pl.ANY),
                  pl.BlockSpec((1, 1, D_MODEL), lambda i, t: (layer, 0, 0)),
                  pl.BlockSpec((1, 1, D_MODEL), lambda i, t: (layer, 0, 0))],
        out_specs=pl.BlockSpec((bt, lt, D_MODEL), tok),
        out_shape=jax.ShapeDtypeStruct((nb, lp, D_MODEL), F32),
        scratch_shapes=[pltpu.VMEM((2, 2, bt * lt // 8, 8, D_MODEL), F32), pltpu.SemaphoreType.DMA((2,))],
        compiler_params=pltpu.CompilerParams(dimension_semantics=("arbitrary", "arbitrary"),
                                             vmem_limit_bytes=VMEM_LIMIT),
        name=f"combine_l{layer}_b{bt}",
    )(pos, pos, x1, mod, rinfo, eo, lg, lb)


def _hist(state):
    return jnp.pad(state, ((0, 0), (HIST - state.shape[1], 0), (0, 0)))


def _trunks(groups, p, mb):
    for g in groups:
        g.update(C=None, S=None, new={key: [] for key in ('n', 'm', 'conv', 'gconv')})
    n_tok = sum(g['x'].shape[0] * g['x'].shape[1] for g in groups)
    for l in range(DEPTH):
        cnt = jnp.zeros((8, 128), F32)
        for g in groups:
            st, mix, lv, nb = g['st'], g['mix'], g['lv'], g['x'].shape[0]
            proj, x = _inproj(g['x'], g['mod'][l], p['ln_in_g'], p['ln_in_b'], p['w_in_r'], p['b_in_r'], l, l == 0,
                              *g['ip_tile'])
            m0 = jnp.broadcast_to(st['m'][l][:, :, None, None], (nb, N_HEADS, 1, 128))
            yb, conv, *pre = _mlstm_prep(proj, _hist(st['conv'][l]), p['conv_b_w8'][l],
                                         mix['NB'], mix['G'], mix['tl'], lv)
            ya, g['C'], n, m = _mlstm_scan(pre, proj, st['C'], st['n'][l], m0, p['mlstm_norm_g'][l:l + 1],
                                           mix['SB'], mix['G'], mix['tl'], l, g['C'])
            *pre, gconv = _gdn_prep(proj, _hist(st['gconv'][l]), p['conv_c_w8'][l], p['alog_row'][l],
                                    p['dtb_row'][l], mix['NB'], mix['G'], mix['tl'], lv)
            yc, g['S'] = _gdn_scan(pre, proj, st['S'], p['gdn_norm_g'][l:l + 1], mix['SB'], mix['G'], mix['tl'],
                                   l, g['S'])
            g['x1'], g['rinfo'], cnt = _merge(ya, yb, yc, proj, x, g['mod'][l], p['w_br_a'], p['w_br_b'], p['w_br_c'],
                                              p['w_out'], p['ln1_g'], p['ln1_b'], p['wr_hi'], p['wr_lo'], p['br'],
                                              cnt, l, g['bt'], g['lt'])
            g['cnt'] = cnt
            for key, val in (('n', n), ('m', m[:, :, 0, 0]), ('conv', conv), ('gconv', gconv)):
                g['new'][key].append(val)
        first_row, block_e, n_used, ztail, n_blocks = _moe_plan(cnt, groups[0]['cnt'], n_tok, mb)
        disp = None
        for g in groups:
            g['pos'] = _dispatch_rows(g['rinfo'], first_row, g['bt'] * g['lt'])
            disp = _dispatch(g['x1'], g['mod'][l], g['pos'], ztail, n_blocks, mb, g['bt'], g['lt'], disp)
        eo = _experts(disp, block_e, n_used, p['exp_w_gate'], p['exp_w_up'], p['exp_w_down'], l, mb)
        for g in groups:
            g['x'] = _combine(g['x1'], g['mod'][l], g['rinfo'], eo, g['pos'], p['ln2_g'], p['ln2_b'], l,
                              g['bt'], g['lt'])
    return [(g['x'], dict({key: jnp.stack(val) for key, val in g['new'].items()}, C=g['C'], S=g['S']))
            for g in groups]


def kernel(x_prompt, x_sample, state_mlstm_C, state_mlstm_n, state_mlstm_m, state_conv, state_gdn_S, state_gdn_conv, c_prompt, c_sample, ln_in_g, ln_in_b, w_ada, b_ada, w_in, b_in, mlstm_norm_g, conv_b_w, conv_c_w, gdn_a_log, gdn_dt_bias, gdn_norm_g, w_br_a, w_br_b, w_br_c, w_out, ln1_g, ln1_b, router_g_w, router_g_b, router_e_w, router_e_b, exp_w_gate, exp_w_up, exp_w_down, ln2_g, ln2_b):
    nbp, lp, _ = x_prompt.shape
    nbs, ls, _ = x_sample.shape
    lsp = 8

    def regroup(a):
        out = jnp.zeros(a.shape[:-1] + (N_PROJ,), a.dtype)
        for src, end, dst in PROJ_SEGMENTS:
            out = out.at[..., dst:dst + end - src].set(a[..., src:end])
        return out

    wr = jnp.concatenate([router_e_w, router_g_w, jnp.zeros((DEPTH, D_MODEL, 128 - N_EXPERTS - N_GROUPS), F32)], axis=-1)
    wr_hi = wr.astype(BF16)
    lane_pad = lambda a: jnp.pad(a, ((0, 0), (4, 128 - 4 - N_HEADS)))[:, None, :]
    p = dict(
        ln_in_g=ln_in_g.reshape(1, D_MODEL), ln_in_b=ln_in_b.reshape(1, D_MODEL),
        w_in_r=_regroup_w_in(w_in), b_in_r=regroup(b_in).reshape(DEPTH, 1, N_PROJ),
        mlstm_norm_g=mlstm_norm_g,
        conv_b_w8=jnp.pad(conv_b_w, ((0, 0), (0, 8 - CONV_B), (0, 0))),
        conv_c_w8=jnp.pad(conv_c_w, ((0, 0), (0, 8 - CONV_C), (0, 0))),
        alog_row=lane_pad(gdn_a_log), dtb_row=lane_pad(gdn_dt_bias), gdn_norm_g=gdn_norm_g,
        w_br_a=w_br_a.astype(BF16), w_br_b=w_br_b.astype(BF16), w_br_c=w_br_c.astype(BF16),
        w_out=w_out.astype(BF16),
        ln1_g=ln1_g.reshape(DEPTH, 1, D_MODEL), ln1_b=ln1_b.reshape(DEPTH, 1, D_MODEL),
        wr_hi=wr_hi, wr_lo=(wr - wr_hi.astype(F32)).astype(BF16),
        br=jnp.concatenate([router_e_b, router_g_b, jnp.zeros((DEPTH, 128 - N_EXPERTS - N_GROUPS), F32)],
                           axis=-1).reshape(DEPTH, 1, 128),
        exp_w_gate=exp_w_gate, exp_w_up=exp_w_up, exp_w_down=exp_w_down,
        ln2_g=ln2_g.reshape(DEPTH, 1, D_MODEL), ln2_b=ln2_b.reshape(DEPTH, 1, D_MODEL),
    )

    mod = _ada(jnp.concatenate([c_prompt, c_sample], axis=0), w_ada, b_ada)
    mod = mod.reshape(DEPTH, nbp + nbs, 6, D_MODEL)

    zeros = lambda *s: jnp.zeros((DEPTH, nbp) + s, F32)
    st_p = {'C': zeros(N_HEADS, DH, DH), 'n': zeros(N_HEADS, DH), 'm': zeros(N_HEADS),
            'conv': zeros(CONV_B - 1, D_MIX), 'S': zeros(N_HEADS, DH, DH), 'gconv': zeros(CONV_C - 1, 3 * D_MIX)}
    prompt = dict(x=x_prompt, mod=mod[:, :nbp], st=st_p, bt=1, lt=512, lv=lp,
                  mix=dict(NB=8, G=1, tl=RB, SB=nbp), ip_tile=(1, 256))

    st_s = {'C': state_mlstm_C, 'n': state_mlstm_n, 'm': state_mlstm_m, 'conv': state_conv,
            'S': state_gdn_S, 'gconv': state_gdn_conv}
    xs = jnp.pad(x_sample, ((0, 0), (0, lsp - ls), (0, 0)))
    sample = dict(x=xs, mod=mod[:, nbp:], st=st_s, bt=64, lt=lsp, lv=ls,
                  mix=dict(NB=2, G=RB // lsp, tl=lsp, SB=1), ip_tile=(32, lsp))
    (y_p, sp), (y_s, ss) = _trunks([prompt, sample], p, mb=256)
    y_s = y_s[:, :ls]

    return (y_p, y_s, sp['C'], sp['n'], sp['m'], sp['conv'], sp['S'], sp['gconv'],
            ss['C'], ss['n'], ss['m'], ss['conv'], ss['S'], ss['gconv'])
```

```python
import functools

import jax
import jax.numpy as jnp
from jax import lax
from jax.experimental import pallas as pl
from jax.experimental.pallas import tpu as pltpu

F32 = jnp.float32
BF16 = jnp.bfloat16

D_MODEL = 1024
DEPTH = 2
N_HEADS = 4
DH = 128
D_MIX = N_HEADS * DH
N_EXPERTS = 32
EXPERTS_PER_GROUP = 8
N_GROUPS = 4
D_EXPERT = 256
CONV_B = 3
CONV_C = 4
HIST = 8
RB = 64
INV_BLOCK = 8
DN_ALPHA = (2 * DEPTH) ** 0.25
LN_EPS = 1e-5
NORM_EPS = 1e-6
NEG = -1e30

OFF_QKVC = 0
OFF_BCH = 1536
OFF_GTS = 3072
OFF_QKVO = 6144
OFF_Z = 8192
OFF_SA = 8704
OFF_SB = 8832
N_PROJ = 8960
TN_PROJ = 1280
PROJ_SEGMENTS = ((3592, 5128, OFF_QKVC), (2056, 3592, OFF_BCH), (5648, 8720, OFF_GTS), (0, 2048, OFF_QKVO),
                 (5128, 5640, OFF_Z), (2048, 2052, OFF_SA), (5640, 5644, OFF_SA + 4), (2052, 2056, OFF_SB),
                 (5644, 5648, OFF_SB + 4))

VMEM_LIMIT = 52 * 1024 * 1024


def _dot(a, b):
    return jnp.dot(a, b, preferred_element_type=F32)


def _split3(x):
    hi = x.astype(BF16)
    r = x - hi.astype(F32)
    mid = r.astype(BF16)
    lo = (r - mid.astype(F32)).astype(BF16)
    return hi, mid, lo


def _layer_norm(x, g, b):
    mu = jnp.mean(x, axis=-1, keepdims=True)
    xc = x - mu
    var = jnp.mean(xc * xc, axis=-1, keepdims=True)
    return xc * lax.rsqrt(var + LN_EPS) * g + b


def _sigmoid(x):
    return jax.nn.sigmoid(x)


def _silu(x):
    return x * _sigmoid(x)


def _log_sigmoid(x):
    return jnp.minimum(x, 0.0) - jnp.log1p(jnp.exp(-jnp.abs(x)))


def _softplus(x):
    return jnp.maximum(x, 0.0) + jnp.log1p(jnp.exp(-jnp.abs(x)))


def _ada_body(c_ref, w_ref, b_ref, o_ref):
    c = c_ref[...]
    s = _silu(c).astype(BF16)
    o_ref[0] = _dot(s, w_ref[0].astype(BF16)) + b_ref[0]


def _ada(c_all, w_ada, b_ada):
    nb = c_all.shape[0]
    return pl.pallas_call(
        _ada_body,
        grid=(DEPTH, 6),
        in_specs=[pl.BlockSpec((nb, D_MODEL), lambda l, j: (0, 0)),
                  pl.BlockSpec((1, D_MODEL, D_MODEL), lambda l, j: (l, 0, j)),
                  pl.BlockSpec((1, 1, D_MODEL), lambda l, j: (l, 0, j))],
        out_specs=pl.BlockSpec((1, nb, D_MODEL), lambda l, j: (l, 0, j)),
        out_shape=jax.ShapeDtypeStruct((DEPTH, nb, 6 * D_MODEL), F32),
        compiler_params=pltpu.CompilerParams(dimension_semantics=("arbitrary", "arbitrary"),
                                             vmem_limit_bytes=VMEM_LIMIT),
        name="ada",
    )(c_all, w_ada, b_ada.reshape(DEPTH, 1, 6 * D_MODEL))


def _regroup_body(w_ref, o_ref):
    o_ref[...] = jnp.zeros_like(o_ref)
    for src, end, dst in PROJ_SEGMENTS:
        o_ref[0, :, dst:dst + end - src] = w_ref[0, :, src:end].astype(BF16)


def _regroup_w_in(w_in):
    n_in = w_in.shape[-1]
    rows = 256
    return pl.pallas_call(
        _regroup_body,
        grid=(DEPTH, D_MODEL // rows),
        in_specs=[pl.BlockSpec((1, rows, n_in), lambda l, i: (l, i, 0))],
        out_specs=pl.BlockSpec((1, rows, N_PROJ), lambda l, i: (l, i, 0)),
        out_shape=jax.ShapeDtypeStruct((DEPTH, D_MODEL, N_PROJ), BF16),
        compiler_params=pltpu.CompilerParams(dimension_semantics=("arbitrary", "arbitrary"),
                                             vmem_limit_bytes=VMEM_LIMIT),
        name="regroup_w_in",
    )(w_in)


def _inproj_body(x_ref, mod_ref, g_ref, b_ref, w_ref, bias_ref, proj_ref, *rest, apply_ln, bt, lt):
    x = x_ref[...]
    if apply_ln:
        x = _layer_norm(x, g_ref[...], b_ref[...])
        rest[0][...] = x
    u = (x * (1.0 + mod_ref[:, 1:2, :]) + mod_ref[:, 0:1, :]).reshape(bt * lt, D_MODEL).astype(BF16)
    for j in range(N_PROJ // TN_PROJ):
        cs = slice(j * TN_PROJ, (j + 1) * TN_PROJ)
        proj_ref[:, :, cs] = (_dot(u, w_ref[0, :, cs]) + bias_ref[0, :, cs]).reshape(bt, lt, TN_PROJ)


def _inproj(x, mod, ln_g, ln_b, w_r, b_r, layer, apply_ln, bt, lt):
    nb, lp, _ = x.shape
    tok = lambda i, t: (i, t, 0)
    out_shape = [jax.ShapeDtypeStruct((nb, lp, N_PROJ), F32)]
    out_specs = [pl.BlockSpec((bt, lt, N_PROJ), tok)]
    if apply_ln:
        out_shape.append(jax.ShapeDtypeStruct((nb, lp, D_MODEL), F32))
        out_specs.append(pl.BlockSpec((bt, lt, D_MODEL), tok))
    res = pl.pallas_call(
        functools.partial(_inproj_body, apply_ln=apply_ln, bt=bt, lt=lt),
        grid=(nb // bt, lp // lt),
        in_specs=[pl.BlockSpec((bt, lt, D_MODEL), tok),
                  pl.BlockSpec((bt, 6, D_MODEL), lambda i, t: (i, 0, 0)),
                  pl.BlockSpec((1, D_MODEL), lambda i, t: (0, 0)),
                  pl.BlockSpec((1, D_MODEL), lambda i, t: (0, 0)),
                  pl.BlockSpec((1, D_MODEL, N_PROJ), lambda i, t: (layer, 0, 0), pipeline_mode=pl.Buffered(1)),
                  pl.BlockSpec((1, 1, N_PROJ), lambda i, t: (layer, 0, 0))],
        out_specs=out_specs,
        out_shape=out_shape,
        compiler_params=pltpu.CompilerParams(dimension_semantics=("arbitrary", "arbitrary"),
                                             vmem_limit_bytes=VMEM_LIMIT),
        name=f"inproj_l{layer}_b{bt}",
    )(x, mod, ln_g, ln_b, w_r, b_r)
    return (res[0], res[1]) if apply_ln else (res[0], x)


def _conv_taps(xp_s, w_ref, width, tl):
    acc = None
    for j in range(width):
        tap = xp_s[:, pl.ds(HIST - (width - 1) + j, tl), :] * w_ref[j:j + 1, :].reshape(1, 1, -1)
        acc = tap if acc is None else acc + tap
    return acc


def _conv_history(xp_s, hist_ref, prev_ref, n_steps):
    if n_steps > 1:
        @pl.when(pl.program_id(1) == 0)
        def _():
            xp_s[:, 0:HIST, :] = hist_ref[...]

        @pl.when(pl.program_id(1) > 0)
        def _():
            xp_s[:, 0:HIST, :] = prev_ref[...]
    else:
        xp_s[:, 0:HIST, :] = hist_ref[...]


def _heads(x, nb, width):
    return jnp.stack([x[:, :, h * width:(h + 1) * width] for h in range(N_HEADS)],
                     axis=1).reshape(nb * N_HEADS, RB, width)


def _gate_cols(x, nb, lane0):
    return jnp.stack([x[:, :, lane0 + h:lane0 + h + 1] for h in range(N_HEADS)],
                     axis=1).reshape(nb * N_HEADS, RB, 1)


def _gate_rows(x, nb, lane0):
    xt = jnp.swapaxes(x, 1, 2)
    return jnp.stack([xt[:, lane0 + h:lane0 + h + 1, :] for h in range(N_HEADS)],
                     axis=1).reshape(nb * N_HEADS, 1, RB)


def _bmm(a, b):
    return jnp.einsum('nts,nsu->ntu', a.astype(BF16), b.astype(BF16), preferred_element_type=F32)


def _bmm_nt(a, b):
    return jnp.einsum('ntd,nsd->nts', a, b, preferred_element_type=F32)


def _block_masks(tl):
    rr = lax.broadcasted_iota(jnp.int32, (RB, RB), 0)
    cc = lax.broadcasted_iota(jnp.int32, (RB, RB), 1)
    incl = rr >= cc
    if tl < RB:
        incl = incl & ((rr // tl) == (cc // tl))
    return rr, cc, incl


def _seq_cumsum(x, incl, nb):
    tril = jnp.broadcast_to(incl.astype(BF16)[None], (nb, RB, RB))
    hi, mid, lo = _split3(x)
    return _bmm(tril, hi) + _bmm(tril, mid) + _bmm(tril, lo)


def _seq_last(x, nb, G, tl):
    return x.reshape(nb * G, tl, 128)[:, tl - 1:tl, :]


def _seq_rows(x3, nb, tl):
    return jnp.broadcast_to(x3, (x3.shape[0], tl, 128)).reshape(nb, RB, 128)


def _put_chains(ref, val, nb, G, tl):
    val4 = val.reshape(nb, N_HEADS, RB, val.shape[-1])
    for h in range(N_HEADS):
        ref[:, :, h] = val4[:, h].reshape(nb, G, tl, val.shape[-1])


def _put_seq_scalars(ref, x3, nb, G, lane0):
    x4 = x3.reshape(nb, G, 1, 128)
    for h in range(N_HEADS):
        ref[:, :, h] = jnp.broadcast_to(x4[:, :, :, lane0 + h:lane0 + h + 1], (nb, G, 1, 128))


def _step_tiling(nb, lp, NB, G, tl):
    assert G * tl == RB
    if G == 1:
        gx, tlx = 1, NB * RB
    else:
        assert lp == tl
        gx, tlx = NB * G, tl
    return gx, tlx, nb // gx, lp // tlx


def _layered_state(layer, nb, NS, prev):
    st_in = pl.BlockSpec((None, NS, N_HEADS, DH, DH), lambda i, c: (layer, i, 0, 0, 0))
    shape = jax.ShapeDtypeStruct((DEPTH, nb, N_HEADS, DH, DH), F32)
    if layer == 0:
        return st_in, pl.BlockSpec((DEPTH, NS, N_HEADS, DH, DH), lambda i, c: (0, i, 0, 0, 0)), shape, [], []
    return st_in, st_in, shape, [pl.BlockSpec(memory_space=pl.ANY)], [prev]


def _fill_later_layers(ref, when):
    @pl.when(when)
    def _():
        for l in range(1, DEPTH):
            ref[l] = ref[0]


def _mlstm_prep_body(qkvo_ref, bch_ref, prev_ref, sa_ref, sb_ref, cv0_ref, cw_ref,
                     yb_ref, cv_ref, nv_ref, q_ref, kw_ref, v_ref, rows_ref, bl_ref, bc_ref, kn_ref, xp_s,
                     *, NB, G, tl, tlx, NCS, lv):
    cs = pl.program_id(1)
    n = NB * N_HEADS
    lvl = lv - (NCS - 1) * tlx

    if NCS > 1:
        @pl.when(cs == 0)
        def _():
            xp_s[:, 0:HIST, :] = cv0_ref[...]

        @pl.when(cs > 0)
        def _():
            xp_s[:, 0:HIST, :] = prev_ref[:, :, D_MIX:2 * D_MIX] * prev_ref[:, :, 2 * D_MIX:3 * D_MIX]
    else:
        xp_s[:, 0:HIST, :] = cv0_ref[...]
    xp_s[:, HIST:HIST + tlx, :] = bch_ref[:, :, D_MIX:2 * D_MIX] * bch_ref[:, :, 2 * D_MIX:3 * D_MIX]
    yb_ref[...] = (bch_ref[:, :, 0:D_MIX] * _conv_taps(xp_s, cw_ref, CONV_B, tlx)).astype(BF16)

    @pl.when(cs == NCS - 1)
    def _():
        cv_ref[...] = xp_s[:, pl.ds(HIST + lvl - (CONV_B - 1), CONV_B - 1), :]

    i_all = sa_ref[...].reshape(NB, RB, 128)
    f_all = _log_sigmoid(sb_ref[...].reshape(NB, RB, 128))
    if lv < NCS * tlx:
        assert NCS == 1
        valid = (lax.broadcasted_iota(jnp.int32, (NB, RB, 128), 1) % tl) < lv
        i_all = jnp.where(valid, i_all, NEG)
        f_all = jnp.where(valid, f_all, 0.0)
    _, _, incl = _block_masks(tl)
    bcum = _seq_cumsum(f_all, incl, NB)
    blast = _seq_last(bcum, NB, G, tl)
    val = _seq_rows(blast, NB, tl) - bcum + i_all
    bmax = jnp.max(val.reshape(NB * G, tl, 128), axis=1, keepdims=True)
    wk0 = jnp.exp(val - _seq_rows(bmax, NB, tl))

    qkvo = qkvo_ref[...].reshape(NB, RB, 4 * D_MIX)
    q = _heads(qkvo[:, :, 0:D_MIX], NB, DH)
    k = _heads(qkvo[:, :, D_MIX:2 * D_MIX], NB, DH) * (DH ** -0.5)
    v = _heads(qkvo[:, :, 2 * D_MIX:3 * D_MIX], NB, DH)
    qb, kb, vb = q.astype(BF16), k.astype(BF16), v.astype(BF16)

    b_col = _gate_cols(bcum, NB, 0)
    dlog = jnp.where(incl[None], b_col - _gate_rows(bcum, NB, 0) + _gate_rows(i_all, NB, 0), NEG)
    d = jnp.max(dlog, axis=-1, keepdims=True)
    s0 = _bmm_nt(qb, kb) * jnp.exp(dlog - d)
    kw0 = k * _gate_cols(wk0, NB, 0)

    _put_chains(nv_ref, _bmm(s0, vb), NB, G, tl)
    _put_chains(q_ref, qb, NB, G, tl)
    _put_chains(kw_ref, kw0.astype(BF16), NB, G, tl)
    _put_chains(v_ref, vb, NB, G, tl)
    rows = jnp.swapaxes(jnp.concatenate([d, b_col, jnp.sum(s0, axis=-1, keepdims=True),
                                         jnp.zeros((n, RB, 5), F32)], axis=-1), 1, 2).reshape(NB, N_HEADS, 8, RB)
    for h in range(N_HEADS):
        for g in range(G):
            rows_ref[:, g, h] = rows[:, h, :, g * tl:(g + 1) * tl]
    _put_seq_scalars(bl_ref, blast, NB, G, 0)
    _put_seq_scalars(bc_ref, bmax, NB, G, 0)
    kn = jnp.sum(kw0.reshape(n * G, tl, DH), axis=1, keepdims=True).reshape(NB, N_HEADS, G, 1, DH)
    for h in range(N_HEADS):
        kn_ref[:, :, h] = kn[:, h]


def _mlstm_prep(proj, cv0, cw, NB, G, tl, lv):
    nb, lp, _ = proj.shape
    gx, tlx, NI, NCS = _step_tiling(nb, lp, NB, G, tl)
    nbt = nb * lp // RB
    step = lambda i, c: (i * NCS + c, 0, 0, 0, 0)
    chain = lambda last, dt: jax.ShapeDtypeStruct((nbt, G, N_HEADS, tl, last), dt)
    cspec = lambda last: pl.BlockSpec((NB, G, N_HEADS, tl, last), step)
    scal = jax.ShapeDtypeStruct((nbt, G, N_HEADS, 1, 128), F32)
    sspec = pl.BlockSpec((NB, G, N_HEADS, 1, 128), step)
    bch = OFF_BCH // (3 * D_MIX)
    return pl.pallas_call(
        functools.partial(_mlstm_prep_body, NB=NB, G=G, tl=tl, tlx=tlx, NCS=NCS, lv=lv),
        grid=(NI, NCS),
        in_specs=[pl.BlockSpec((gx, tlx, 4 * D_MIX), lambda i, c: (i, c, OFF_QKVO // (4 * D_MIX))),
                  pl.BlockSpec((gx, tlx, 3 * D_MIX), lambda i, c: (i, c, bch)),
                  pl.BlockSpec((gx, HIST, 3 * D_MIX), lambda i, c: (i, jnp.maximum(c * (tlx // HIST) - 1, 0), bch)),
                  pl.BlockSpec((gx, tlx, 128), lambda i, c: (i, c, OFF_SA // 128)),
                  pl.BlockSpec((gx, tlx, 128), lambda i, c: (i, c, OFF_SB // 128)),
                  pl.BlockSpec((gx, HIST, D_MIX), lambda i, c: (i, 0, 0)),
                  pl.BlockSpec((8, D_MIX), lambda i, c: (0, 0))],
        out_specs=[pl.BlockSpec((gx, tlx, D_MIX), lambda i, c: (i, c, 0)),
                   pl.BlockSpec((gx, CONV_B - 1, D_MIX), lambda i, c: (i, 0, 0)),
                   cspec(DH), cspec(DH), cspec(DH), cspec(DH),
                   pl.BlockSpec((NB, G, N_HEADS, 8, tl), step), sspec, sspec, sspec],
        out_shape=[jax.ShapeDtypeStruct((nb, lp, D_MIX), BF16),
                   jax.ShapeDtypeStruct((nb, CONV_B - 1, D_MIX), F32),
                   chain(DH, F32), chain(DH, BF16), chain(DH, BF16), chain(DH, BF16),
                   jax.ShapeDtypeStruct((nbt, G, N_HEADS, 8, tl), F32),
                   scal, scal, scal],
        scratch_shapes=[pltpu.VMEM((gx, HIST + tlx, D_MIX), F32)],
        compiler_params=pltpu.CompilerParams(dimension_semantics=("arbitrary", "arbitrary"),
                                             vmem_limit_bytes=VMEM_LIMIT),
        name=f"mlstm_prep_g{G}",
    )(proj, proj, proj, proj, proj, cv0, cw)


def _mlstm_scan_body(nv_ref, q_ref, kw_ref, v_ref, rows_ref, bl_ref, bc_ref, kn_ref, o_ref, C0_ref, n0_ref, m0_ref,
                     ng_ref, *rest, NS, tl, NC, first):
    ya_ref, C_out, n_ref, m_ref = rest[-4:]
    C_ref = C_out.at[0] if first else C_out
    c = pl.program_id(1)
    n = NS * N_HEADS

    @pl.when(c == 0)
    def _():
        C_ref[...] = C0_ref[...]
        n_ref[...] = n0_ref[...]
        m_ref[...] = m0_ref[...]

    chains = lambda ref: ref[...].reshape(n, ref.shape[-2], ref.shape[-1])
    C = C_ref[...].reshape(n, DH, DH)
    nvec = n_ref[...].reshape(n, 1, DH)
    m_prev = chains(m_ref)[:, :, 0:1]
    rows = chains(rows_ref)
    d, b, ds0 = rows[:, 0:1, :], rows[:, 1:2, :], rows[:, 2:3, :]
    qb = chains(q_ref)

    m_inter = b + m_prev
    m_t = jnp.maximum(m_inter, d)
    f = jnp.exp(d - m_t)
    inter = jnp.exp(m_inter - m_t)
    qn = jnp.einsum('nod,ntd->not', nvec.astype(BF16), qb, preferred_element_type=F32)
    den = f * ds0 + inter * qn
    scale_t = 1.0 / jnp.maximum(jnp.abs(den), jnp.exp(-m_t))
    fi = jnp.swapaxes(jnp.concatenate([f * scale_t, inter * scale_t, jnp.zeros((n, 6, tl), F32)], axis=1), 1, 2)
    qC = jnp.einsum('ntd,nde->nte', qb, C.astype(BF16), preferred_element_type=F32)
    hh = fi[:, :, 0:1] * chains(nv_ref) + fi[:, :, 1:2] * qC

    m_new = m_t[:, :, tl - 1:tl]
    decay = jnp.exp(chains(bl_ref)[:, :, 0:1] + m_prev - m_new)
    scale = jnp.exp(chains(bc_ref)[:, :, 0:1] - m_new)
    kv = jnp.einsum('ntd,nte->nde', chains(kw_ref), chains(v_ref), preferred_element_type=F32)
    C_ref[...] = (decay * C + scale * kv).reshape(NS, N_HEADS, DH, DH)
    n_ref[...] = (decay * nvec + scale * chains(kn_ref)).reshape(NS, N_HEADS, DH)
    m_ref[...] = jnp.broadcast_to(m_new, (n, 1, 128)).reshape(NS, N_HEADS, 1, 128)
    if first:
        _fill_later_layers(C_out, c == NC - 1)

    mu = jnp.mean(hh, axis=-1, keepdims=True)
    hc = hh - mu
    hn = (hc * lax.rsqrt(jnp.mean(hc * hc, axis=-1, keepdims=True) + LN_EPS)).reshape(NS, N_HEADS, tl, DH)
    for h in range(N_HEADS):
        hs = slice(h * DH, (h + 1) * DH)
        ya_ref[:, :, hs] = (_sigmoid(o_ref[:, :, hs]) * hn[:, h] * ng_ref[:, hs]).astype(BF16)


def _mlstm_scan(pre, proj, C0, n0, m0, ng, SB, G, tl, layer, C_prev):
    nb, lp, _ = proj.shape
    NS = SB * G
    NI, NC = nb // NS, lp // tl
    six = lambda a: a.reshape((NI * SB, NC) + a.shape[1:])
    spec6 = lambda a: pl.BlockSpec((SB, 1) + a.shape[1:], lambda i, c: (i, c, 0, 0, 0, 0))
    seq4 = lambda i, c: (i, 0, 0, 0)
    st_in, st_out, st_shape, extra_in, extra_args = _layered_state(layer, nb, NS, C_prev)
    n_in = len(pre) + 5
    return pl.pallas_call(
        functools.partial(_mlstm_scan_body, NS=NS, tl=tl, NC=NC, first=layer == 0),
        grid=(NI, NC),
        in_specs=[spec6(a) for a in pre] + [
            pl.BlockSpec((NS, tl, D_MIX), lambda i, c: (i, c, (OFF_QKVO + 3 * D_MIX) // D_MIX)),
            st_in,
            pl.BlockSpec((NS, N_HEADS, DH), lambda i, c: (i, 0, 0)),
            pl.BlockSpec((NS, N_HEADS, 1, 128), seq4),
            pl.BlockSpec((1, D_MIX), lambda i, c: (0, 0))] + extra_in,
        out_specs=[pl.BlockSpec((NS, tl, D_MIX), lambda i, c: (i, c, 0)),
                   st_out,
                   pl.BlockSpec((NS, N_HEADS, DH), lambda i, c: (i, 0, 0)),
                   pl.BlockSpec((NS, N_HEADS, 1, 128), seq4)],
        out_shape=[jax.ShapeDtypeStruct((nb, lp, D_MIX), BF16),
                   st_shape,
                   jax.ShapeDtypeStruct((nb, N_HEADS, DH), F32),
                   jax.ShapeDtypeStruct((nb, N_HEADS, 1, 128), F32)],
        input_output_aliases={n_in: 1} if extra_in else {},
        compiler_params=pltpu.CompilerParams(dimension_semantics=("arbitrary", "arbitrary"),
                                             vmem_limit_bytes=VMEM_LIMIT),
        name=f"mlstm_scan_g{G}",
    )(*[six(a) for a in pre], proj, C0, n0, m0, ng, *extra_args)


def _unit_lower_inverse(n, rr, cc, tl):
    eye = (rr == cc).astype(F32)[None]
    p = jnp.where(((rr // INV_BLOCK) == (cc // INV_BLOCK))[None], n, 0.0)
    x = eye + p
    b = 2
    while b < INV_BLOCK:
        p = _bmm(p, p)
        x = x + _bmm(x, p)
        b *= 2
    b = INV_BLOCK
    while b < tl:
        off = jnp.where((((rr // (2 * b)) == (cc // (2 * b))) & ((rr // b) != (cc // b)))[None], n, 0.0)
        x = x + _bmm(x, _bmm(off, x))
        b *= 2
    return x


def _gdn_prep_body(x_ref, prev_ref, sa_ref, sb_ref, gc0_ref, cw_ref, alog_ref, dtb_ref,
                   u_ref, w_ref, qg_ref, kd_ref, qkm_ref, eg_ref, gcs_ref, xp_s, *, NB, G, tl, tlx, NCS, lv):
    cs = pl.program_id(1)
    lvl = lv - (NCS - 1) * tlx

    _conv_history(xp_s, gc0_ref, prev_ref, NCS)
    xp_s[:, HIST:HIST + tlx, :] = x_ref[...]
    qkv = _silu(_conv_taps(xp_s, cw_ref, CONV_C, tlx)).reshape(NB, RB, 3 * D_MIX)

    @pl.when(cs == NCS - 1)
    def _():
        gcs_ref[...] = xp_s[:, pl.ds(HIST + lvl - (CONV_C - 1), CONV_C - 1), :]

    beta_all = _sigmoid(sa_ref[...].reshape(NB, RB, 128))
    g_all = -jnp.exp(alog_ref[...]) * _softplus(sb_ref[...].reshape(NB, RB, 128) + dtb_ref[...])
    if lv < NCS * tlx:
        assert NCS == 1
        valid = (lax.broadcasted_iota(jnp.int32, (NB, RB, 128), 1) % tl) < lv
        beta_all = jnp.where(valid, beta_all, 0.0)
        g_all = jnp.where(valid, g_all, 0.0)
    rr, cc, incl = _block_masks(tl)
    diag = rr == cc
    gam = _seq_cumsum(g_all, incl, NB)
    glast = _seq_last(gam, NB, G, tl)
    gcol = _gate_cols(gam, NB, 4)
    bcol = _gate_cols(beta_all, NB, 4)
    egcol = _gate_cols(jnp.exp(gam), NB, 4)
    kdcol = _gate_cols(jnp.exp(_seq_rows(glast, NB, tl) - gam), NB, 4)

    q = _heads(qkv[:, :, 0:D_MIX], NB, DH)
    k = _heads(qkv[:, :, D_MIX:2 * D_MIX], NB, DH)
    v = _heads(qkv[:, :, 2 * D_MIX:3 * D_MIX], NB, DH)
    q = q * lax.rsqrt(jnp.sum(q * q, axis=-1, keepdims=True) + NORM_EPS) * (DH ** -0.5)
    k = k * lax.rsqrt(jnp.sum(k * k, axis=-1, keepdims=True) + NORM_EPS)
    qb, kb = q.astype(BF16), k.astype(BF16)

    dmat = jnp.exp(jnp.where(incl[None], gcol - _gate_rows(gam, NB, 4), NEG))
    nmat = jnp.where(diag[None], 0.0, -(bcol * _bmm_nt(kb, kb) * dmat))
    rhs = jnp.concatenate([bcol * v, (bcol * egcol) * k], axis=-1)
    sol = _bmm(_unit_lower_inverse(nmat, rr, cc, tl), rhs)
    qkm = (_bmm_nt(qb, kb) * dmat).astype(BF16).reshape(NB, N_HEADS, RB, RB)

    _put_chains(u_ref, sol[:, :, 0:DH], NB, G, tl)
    _put_chains(w_ref, sol[:, :, DH:2 * DH].astype(BF16), NB, G, tl)
    _put_chains(qg_ref, (q * egcol).astype(BF16), NB, G, tl)
    _put_chains(kd_ref, (k * kdcol).astype(BF16), NB, G, tl)
    _put_seq_scalars(eg_ref, jnp.exp(glast), NB, G, 4)
    for h in range(N_HEADS):
        for g in range(G):
            qkm_ref[:, g, h] = qkm[:, h, g * tl:(g + 1) * tl, g * tl:(g + 1) * tl]


def _gdn_prep(proj, gc0, cw, alog, dtb, NB, G, tl, lv):
    nb, lp, _ = proj.shape
    assert tl % INV_BLOCK == 0 and (tl // INV_BLOCK) & (tl // INV_BLOCK - 1) == 0
    gx, tlx, NI, NCS = _step_tiling(nb, lp, NB, G, tl)
    nbt = nb * lp // RB
    step = lambda i, c: (i * NCS + c, 0, 0, 0, 0)
    par = lambda i, c: (0, 0)
    chain = lambda last, dt: jax.ShapeDtypeStruct((nbt, G, N_HEADS, tl, last), dt)
    cspec = lambda last: pl.BlockSpec((NB, G, N_HEADS, tl, last), step)
    return pl.pallas_call(
        functools.partial(_gdn_prep_body, NB=NB, G=G, tl=tl, tlx=tlx, NCS=NCS, lv=lv),
        grid=(NI, NCS),
        in_specs=[pl.BlockSpec((gx, tlx, 3 * D_MIX), lambda i, c: (i, c, OFF_QKVC // (3 * D_MIX))),
                  pl.BlockSpec((gx, HIST, 3 * D_MIX),
                               lambda i, c: (i, jnp.maximum(c * (tlx // HIST) - 1, 0), OFF_QKVC // (3 * D_MIX))),
                  pl.BlockSpec((gx, tlx, 128), lambda i, c: (i, c, OFF_SA // 128)),
                  pl.BlockSpec((gx, tlx, 128), lambda i, c: (i, c, OFF_SB // 128)),
                  pl.BlockSpec((gx, HIST, 3 * D_MIX), lambda i, c: (i, 0, 0)),
                  pl.BlockSpec((8, 3 * D_MIX), par),
                  pl.BlockSpec((1, 128), par),
                  pl.BlockSpec((1, 128), par)],
        out_specs=[cspec(DH), cspec(DH), cspec(DH), cspec(DH), cspec(tl),
                   pl.BlockSpec((NB, G, N_HEADS, 1, 128), step),
                   pl.BlockSpec((gx, CONV_C - 1, 3 * D_MIX), lambda i, c: (i, 0, 0))],
        out_shape=[chain(DH, F32), chain(DH, BF16), chain(DH, BF16), chain(DH, BF16), chain(tl, BF16),
                   jax.ShapeDtypeStruct((nbt, G, N_HEADS, 1, 128), F32),
                   jax.ShapeDtypeStruct((nb, CONV_C - 1, 3 * D_MIX), F32)],
        scratch_shapes=[pltpu.VMEM((gx, HIST + tlx, 3 * D_MIX), F32)],
        compiler_params=pltpu.CompilerParams(dimension_semantics=("arbitrary", "arbitrary"),
                                             vmem_limit_bytes=VMEM_LIMIT),
        name=f"gdn_prep_g{G}",
    )(proj, proj, proj, proj, gc0, cw, alog, dtb)


def _gdn_scan_body(u_ref, w_ref, qg_ref, kd_ref, qkm_ref, eg_ref, z_ref, S0_ref, gng_ref, *rest,
                   NS, tl, NC, first):
    yc_ref, S_out = rest[-2:]
    S_ref = S_out.at[0] if first else S_out
    c = pl.program_id(1)
    n = NS * N_HEADS

    @pl.when(c == 0)
    def _():
        S_ref[...] = S0_ref[...]

    S = S_ref[...].reshape(n, DH, DH)
    Sb = S.astype(BF16)
    chains = lambda ref: ref[...].reshape(n, tl, ref.shape[-1])
    v_new = chains(u_ref) - jnp.einsum('ntd,nde->nte', chains(w_ref), Sb, preferred_element_type=F32)
    vnb = v_new.astype(BF16)
    o = (jnp.einsum('ntd,nde->nte', chains(qg_ref), Sb, preferred_element_type=F32)
         + jnp.einsum('nts,nse->nte', chains(qkm_ref), vnb, preferred_element_type=F32))
    eg = eg_ref[...].reshape(n, 1, 128)[:, :, 0:1]
    S_new = eg * S + jnp.einsum('ntd,nte->nde', chains(kd_ref), vnb, preferred_element_type=F32)
    S_ref[...] = S_new.reshape(NS, N_HEADS, DH, DH)
    if first:
        _fill_later_layers(S_out, c == NC - 1)

    on = (o * lax.rsqrt(jnp.mean(o * o, axis=-1, keepdims=True) + NORM_EPS) * gng_ref[...]).reshape(NS, N_HEADS, tl, DH)
    for h in range(N_HEADS):
        yc_ref[:, :, h * DH:(h + 1) * DH] = (on[:, h] * _silu(z_ref[:, :, h * DH:(h + 1) * DH])).astype(BF16)


def _gdn_scan(pre, proj, S0, gng, SB, G, tl, layer, S_prev):
    nb, lp, _ = proj.shape
    NS = SB * G
    NI, NC = nb // NS, lp // tl
    six = lambda a: a.reshape((NI * SB, NC) + a.shape[1:])
    cspec = lambda last: pl.BlockSpec((SB, 1, G, N_HEADS, tl, last), lambda i, c: (i, c, 0, 0, 0, 0))
    u, w, qg, kd, qkm, eg = (six(a) for a in pre)
    st_in, st_out, st_shape, extra_in, extra_args = _layered_state(layer, nb, NS, S_prev)
    return pl.pallas_call(
        functools.partial(_gdn_scan_body, NS=NS, tl=tl, NC=NC, first=layer == 0),
        grid=(NI, NC),
        in_specs=[cspec(DH), cspec(DH), cspec(DH), cspec(DH), cspec(tl),
                  pl.BlockSpec((SB, 1, G, N_HEADS, 1, 128), lambda i, c: (i, c, 0, 0, 0, 0)),
                  pl.BlockSpec((NS, tl, D_MIX), lambda i, c: (i, c, OFF_Z // D_MIX)),
                  st_in,
                  pl.BlockSpec((1, DH), lambda i, c: (0, 0))] + extra_in,
        out_specs=[pl.BlockSpec((NS, tl, D_MIX), lambda i, c: (i, c, 0)), st_out],
        out_shape=[jax.ShapeDtypeStruct((nb, lp, D_MIX), BF16), st_shape],
        input_output_aliases={9: 1} if extra_in else {},
        compiler_params=pltpu.CompilerParams(dimension_semantics=("arbitrary", "arbitrary"),
                                             vmem_limit_bytes=VMEM_LIMIT),
        name=f"gdn_scan_g{G}",
    )(u, w, qg, kd, qkm, eg, proj, S0, gng, *extra_args)


def _route(rl):
    lane = lax.broadcasted_iota(jnp.int32, rl.shape, 1).astype(F32)
    is_g = (lane >= N_EXPERTS) & (lane < N_EXPERTS + N_GROUPS)
    gl = jnp.where(is_g, rl, NEG)
    gmax = jnp.max(gl, axis=-1, keepdims=True)
    grp = jnp.min(jnp.where(gl == gmax, lane - N_EXPERTS, 4.0 * N_EXPERTS), axis=-1, keepdims=True)
    p_grp = 1.0 / jnp.sum(jnp.where(is_g, jnp.exp(gl - gmax), 0.0), axis=-1, keepdims=True)
    lo = grp * EXPERTS_PER_GROUP
    in_grp = (lane >= lo) & (lane < lo + EXPERTS_PER_GROUP)
    el = jnp.where(in_grp, rl, NEG)
    m1 = jnp.max(el, axis=-1, keepdims=True)
    i1 = jnp.min(jnp.where(el == m1, lane, 4.0 * N_EXPERTS), axis=-1, keepdims=True)
    el2 = jnp.where(lane == i1, NEG, el)
    m2 = jnp.max(el2, axis=-1, keepdims=True)
    i2 = jnp.min(jnp.where(el2 == m2, lane, 4.0 * N_EXPERTS), axis=-1, keepdims=True)
    e2 = jnp.exp(m2 - m1)
    w1 = p_grp / (1.0 + e2)
    w2 = p_grp * e2 / (1.0 + e2)
    return i1, i2, w1, w2


def _merge_body(ya_ref, yb_ref, yc_ref, ga_ref, gb_ref, gc_ref, x_ref, mod_ref, wa_ref, wb_ref, wc_ref, wo_ref,
                lg_ref, lb_ref, wrh_ref, wrl_ref, br_ref, cnt0_ref, x1_ref, rinfo_ref, cnt_ref, *, bt, lt):
    tm = bt * lt

    @pl.when((pl.program_id(0) == 0) & (pl.program_id(1) == 0))
    def _():
        cnt_ref[...] = cnt0_ref[...]

    def r2(ref):
        return ref[...].reshape(tm, ref.shape[-1])

    merged = (_sigmoid(r2(ga_ref)) * _dot(r2(ya_ref), wa_ref[0])
              + _sigmoid(r2(gb_ref)) * _dot(r2(yb_ref), wb_ref[0])
              + _sigmoid(r2(gc_ref)) * _dot(r2(yc_ref), wc_ref[0]))
    out = _dot(merged.astype(BF16), wo_ref[0])
    y = DN_ALPHA * x_ref[...] + (1.0 + mod_ref[:, 2:3, :]) * out.reshape(bt, lt, D_MODEL)
    x1 = _layer_norm(y, lg_ref[0], lb_ref[0])
    x1_ref[...] = x1
    u2 = (x1 * (1.0 + mod_ref[:, 4:5, :]) + mod_ref[:, 3:4, :]).reshape(tm, D_MODEL)
    hi = u2.astype(BF16)
    lo = (u2 - hi.astype(F32)).astype(BF16)
    rl = _dot(hi, wrh_ref[0]) + _dot(lo, wrh_ref[0]) + _dot(hi, wrl_ref[0]) + br_ref[0]
    i1, i2, w1, w2 = _route(rl)
    lane = lax.broadcasted_iota(jnp.int32, (tm, 128), 1).astype(F32)
    onehot = jnp.where((lane == i1) | (lane == i2), 1.0, 0.0)
    rr = lax.broadcasted_iota(jnp.int32, (tm, tm), 0)
    cc = lax.broadcasted_iota(jnp.int32, (tm, tm), 1)
    before = _dot((rr > cc).astype(BF16), onehot.astype(BF16)) + cnt_ref[0:1, :]
    rank1 = jnp.sum(jnp.where(lane == i1, before, 0.0), axis=-1, keepdims=True)
    rank2 = jnp.sum(jnp.where(lane == i2, before, 0.0), axis=-1, keepdims=True)
    cnt_ref[0:1, :] += jnp.sum(onehot, axis=0, keepdims=True)
    rinfo = jnp.zeros((tm, 128), F32)
    for k, val in enumerate((i1, i2, w1, w2, rank1, rank2)):
        rinfo = jnp.where(lane == k, val, rinfo)
    rinfo_ref[...] = rinfo.reshape(bt, lt, 128)


def _merge(ya, yb, yc, proj, x, mod, wa, wb, wc, wo, lg, lb, wrh, wrl, br, cnt0, layer, bt, lt):
    nb, lp, _ = x.shape
    tok = lambda i, t: (i, t, 0)
    wsp = lambda shape: pl.BlockSpec((1,) + shape, lambda i, t: (layer, 0, 0))
    g0 = OFF_GTS // D_MODEL
    return pl.pallas_call(
        functools.partial(_merge_body, bt=bt, lt=lt),
        grid=(nb // bt, lp // lt),
        in_specs=[pl.BlockSpec((bt, lt, D_MIX), tok),
                  pl.BlockSpec((bt, lt, D_MIX), tok),
                  pl.BlockSpec((bt, lt, D_MIX), tok),
                  pl.BlockSpec((bt, lt, D_MODEL), lambda i, t: (i, t, g0)),
                  pl.BlockSpec((bt, lt, D_MODEL), lambda i, t: (i, t, g0 + 1)),
                  pl.BlockSpec((bt, lt, D_MODEL), lambda i, t: (i, t, g0 + 2)),
                  pl.BlockSpec((bt, lt, D_MODEL), tok),
                  pl.BlockSpec((bt, 6, D_MODEL), lambda i, t: (i, 0, 0)),
                  wsp((D_MIX, D_MODEL)), wsp((D_MIX, D_MODEL)), wsp((D_MIX, D_MODEL)), wsp((D_MODEL, D_MODEL)),
                  wsp((1, D_MODEL)), wsp((1, D_MODEL)),
                  wsp((D_MODEL, 128)), wsp((D_MODEL, 128)), wsp((1, 128)),
                  pl.BlockSpec((8, 128), lambda i, t: (0, 0))],
        out_specs=[pl.BlockSpec((bt, lt, D_MODEL), tok),
                   pl.BlockSpec((bt, lt, 128), tok),
                   pl.BlockSpec((8, 128), lambda i, t: (0, 0))],
        out_shape=[jax.ShapeDtypeStruct((nb, lp, D_MODEL), F32),
                   jax.ShapeDtypeStruct((nb, lp, 128), F32),
                   jax.ShapeDtypeStruct((8, 128), F32)],
        compiler_params=pltpu.CompilerParams(dimension_semantics=("arbitrary", "arbitrary"),
                                             vmem_limit_bytes=VMEM_LIMIT),
        name=f"merge_l{layer}_b{bt}",
    )(ya, yb, yc, proj, proj, proj, x, mod, wa, wb, wc, wo, lg, lb, wrh, wrl, br, cnt0)


def _moe_plan(cnt, cnt_first, n_tok, mb):
    n_blocks = 2 * n_tok // mb + N_EXPERTS
    counts = cnt[0, :N_EXPERTS].astype(jnp.int32)
    nblk = (counts + mb - 1) // mb
    pend = jnp.cumsum(nblk)
    block_e = jnp.sum(pend[None, :] <= jnp.arange(n_blocks, dtype=jnp.int32)[:, None], axis=1)
    block_e = jnp.minimum(block_e, N_EXPERTS - 1).astype(jnp.int32)
    n_used = pend[N_EXPERTS - 1:].astype(jnp.int32)
    first_row = (pend - nblk) * mb
    lo = jnp.minimum((first_row + cnt_first[0, :N_EXPERTS].astype(jnp.int32)) // mb, pend - 1)
    ztail = jnp.concatenate([jnp.where(nblk > 0, lo, 0), jnp.where(nblk > 0, pend - 1, -1), n_used])
    return first_row, block_e, n_used, ztail.astype(jnp.int32).reshape(1, 1, 2 * N_EXPERTS + 1), n_blocks


def _dispatch_rows(rinfo, first_row, tm):
    nb, lp, _ = rinfo.shape
    expert = rinfo[..., 0:2].astype(jnp.int32)
    rank = rinfo[..., 4:6].astype(jnp.int32)
    ids = jnp.arange(N_EXPERTS, dtype=jnp.int32)
    start = jnp.sum(jnp.where(expert[..., None] == ids, first_row, 0), axis=-1)
    return (start + rank).reshape(nb * lp // tm, 1, 2 * tm)


def _row_copy(src, dst, sem):
    return pltpu.make_async_copy(src, dst, sem)


def _dispatch_body(pos_ref, ztail_ref, x1_ref, mod_ref, *rest, bt, lt, n_blocks, mb, n_steps, first):
    disp_ref, u_scr, z_scr, sem, zsem = rest[-5:]
    tm = bt * lt

    @pl.when((pl.program_id(0) == 0) & (pl.program_id(1) == 0) & first)
    def _():
        z_scr[...] = jnp.zeros_like(z_scr)

        def zero_block(j):
            return _row_copy(z_scr, disp_ref.at[pl.ds(pl.multiple_of(j * mb, mb), mb)], zsem)

        def for_zeroed_blocks(fn):
            for e in range(N_EXPERTS):
                lax.fori_loop(ztail_ref[0, 0, e], ztail_ref[0, 0, N_EXPERTS + e] + 1, fn, 0)
            lax.fori_loop(ztail_ref[0, 0, 2 * N_EXPERTS], n_blocks, fn, 0)

        for_zeroed_blocks(lambda j, c: (zero_block(j).start(), c)[1])
        for_zeroed_blocks(lambda j, c: (zero_block(j).wait(), c)[1])

    step = pl.program_id(0) * pl.num_programs(1) + pl.program_id(1)
    slot = step % 2
    u2 = x1_ref[...] * (1.0 + mod_ref[:, 4:5, :]) + mod_ref[:, 3:4, :]
    u_scr[slot] = u2.reshape(tm // 8, 8, D_MODEL)

    def issue(i, carry):
        for r in range(8):
            for k in range(2):
                row = pos_ref[0, 0, 16 * i + 2 * r + k]
                _row_copy(u_scr.at[slot, i, pl.ds(r, 1)], disp_ref.at[pl.ds(row, 1)],
                          sem.at[slot]).start(priority=k)
        return carry

    lax.fori_loop(0, tm // 8, issue, 0)

    def drain(which):
        for k in range(2):
            _row_copy(disp_ref.at[pl.ds(0, tm)], disp_ref.at[pl.ds(0, tm)], sem.at[which]).wait()

    @pl.when(step > 0)
    def _():
        drain(1 - slot)

    @pl.when(step == n_steps - 1)
    def _():
        drain(slot)


def _dispatch(x1, mod, pos, ztail, n_blocks, mb, bt, lt, buf):
    nb, lp, _ = x1.shape
    nt = lp // lt
    extra_in, extra_args = ([], []) if buf is None else ([pl.BlockSpec(memory_space=pl.ANY)], [buf])
    return pl.pallas_call(
        functools.partial(_dispatch_body, bt=bt, lt=lt, n_blocks=n_blocks, mb=mb, n_steps=(nb // bt) * nt,
                          first=buf is None),
        grid=(nb // bt, nt),
        in_specs=[pl.BlockSpec((1, 1, 2 * bt * lt), lambda i, t: (i * nt + t, 0, 0), memory_space=pltpu.SMEM),
                  pl.BlockSpec((1, 1, 2 * N_EXPERTS + 1), lambda i, t: (0, 0, 0), memory_space=pltpu.SMEM),
                  pl.BlockSpec((bt, lt, D_MODEL), lambda i, t: (i, t, 0)),
                  pl.BlockSpec((bt, 6, D_MODEL), lambda i, t: (i, 0, 0))] + extra_in,
        out_specs=pl.BlockSpec(memory_space=pl.ANY),
        out_shape=jax.ShapeDtypeStruct((n_blocks * mb, D_MODEL), F32),
        scratch_shapes=[pltpu.VMEM((2, bt * lt // 8, 8, D_MODEL), F32), pltpu.VMEM((mb, D_MODEL), F32),
                        pltpu.SemaphoreType.DMA((2,)), pltpu.SemaphoreType.DMA],
        compiler_params=pltpu.CompilerParams(dimension_semantics=("arbitrary", "arbitrary"),
                                             vmem_limit_bytes=VMEM_LIMIT),
        input_output_aliases={4: 0} if extra_in else {},
        name=f"dispatch_b{bt}",
    )(pos, ztail, x1, mod, *extra_args)


def _experts_body(be_ref, nu_ref, x_ref, wg_ref, wu_ref, wd_ref, o_ref, wg_s, wu_s, wd_s):
    j = pl.program_id(0)

    @pl.when((j == 0) | (be_ref[j] != be_ref[jnp.maximum(j - 1, 0)]))
    def _():
        wg_s[...] = wg_ref[0, 0].astype(BF16)
        wu_s[...] = wu_ref[0, 0].astype(BF16)
        wd_s[...] = wd_ref[0, 0].astype(BF16)

    @pl.when(j < nu_ref[0])
    def _():
        x = x_ref[...].astype(BF16)
        hb = _silu(_dot(x, wg_s[...])) * _dot(x, wu_s[...])
        o_ref[...] = _dot(hb.astype(BF16), wd_s[...])

    @pl.when(j >= nu_ref[0])
    def _():
        o_ref[...] = jnp.zeros_like(o_ref)


def _experts(disp, block_e, n_used, wg, wu, wd, layer, mb):
    n_blocks = disp.shape[0] // mb
    wmap = lambda j, be, nu: (layer, be[j], 0, 0)
    return pl.pallas_call(
        _experts_body,
        grid_spec=pltpu.PrefetchScalarGridSpec(
            num_scalar_prefetch=2,
            grid=(n_blocks,),
            in_specs=[pl.BlockSpec((mb, D_MODEL), lambda j, be, nu: (jnp.minimum(j, nu[0] - 1), 0)),
                      pl.BlockSpec((1, 1, D_MODEL, D_EXPERT), wmap),
                      pl.BlockSpec((1, 1, D_MODEL, D_EXPERT), wmap),
                      pl.BlockSpec((1, 1, D_EXPERT, D_MODEL), wmap)],
            out_specs=pl.BlockSpec((mb, D_MODEL), lambda j, be, nu: (j, 0)),
            scratch_shapes=[pltpu.VMEM((D_MODEL, D_EXPERT), BF16), pltpu.VMEM((D_MODEL, D_EXPERT), BF16),
                            pltpu.VMEM((D_EXPERT, D_MODEL), BF16)]),
        out_shape=jax.ShapeDtypeStruct(disp.shape, F32),
        compiler_params=pltpu.CompilerParams(dimension_semantics=("arbitrary",), vmem_limit_bytes=VMEM_LIMIT),
        name=f"experts_l{layer}_n{n_blocks}",
    )(block_e, n_used, disp, wg, wu, wd)


def _combine_body(pos_ref, pos_next_ref, x1_ref, mod_ref, rinfo_ref, eo_ref, lg_ref, lb_ref, x2_ref, r_scr, sem,
                  *, bt, lt, n_steps):
    tm = bt * lt
    step = pl.program_id(0) * pl.num_programs(1) + pl.program_id(1)
    slot = step % 2

    def gather(p_ref, to_slot):
        def issue(i, carry):
            for r in range(8):
                for k in range(2):
                    row = p_ref[0, 0, 16 * i + 2 * r + k]
                    _row_copy(eo_ref.at[pl.ds(row, 1)], r_scr.at[to_slot, k, i, pl.ds(r, 1)],
                              sem.at[to_slot]).start(priority=k)
            return carry

        lax.fori_loop(0, tm // 8, issue, 0)

    @pl.when(step == 0)
    def _():
        gather(pos_ref, 0)

    @pl.when(step + 1 < n_steps)
    def _():
        gather(pos_next_ref, 1 - slot)

    for k in range(2):
        _row_copy(eo_ref.at[pl.ds(0, tm)], eo_ref.at[pl.ds(0, tm)], sem.at[slot]).wait()

    rinfo = rinfo_ref[...].reshape(tm, 128)
    rows = r_scr[slot]
    moe = (rows[0].reshape(tm, D_MODEL) * rinfo[:, 2:3] + rows[1].reshape(tm, D_MODEL) * rinfo[:, 3:4])
    y = DN_ALPHA * x1_ref[...] + (1.0 + mod_ref[:, 5:6, :]) * moe.reshape(bt, lt, D_MODEL)
    x2_ref[...] = _layer_norm(y, lg_ref[0], lb_ref[0])


def _combine(x1, mod, rinfo, eo, pos, lg, lb, layer, bt, lt):
    nb, lp, _ = x1.shape
    nt = lp // lt
    tok = lambda i, t: (i, t, 0)
    n_steps = (nb // bt) * nt
    return pl.pallas_call(
        functools.partial(_combine_body, bt=bt, lt=lt, n_steps=n_steps),
        grid=(nb // bt, nt),
        in_specs=[pl.BlockSpec((1, 1, 2 * bt * lt), lambda i, t: (i * nt + t, 0, 0), memory_space=pltpu.SMEM),
                  pl.BlockSpec((1, 1, 2 * bt * lt), lambda i, t: (jnp.minimum(i * nt + t + 1, n_steps - 1), 0, 0),
                               memory_space=pltpu.SMEM),
                  pl.BlockSpec((bt, lt, D_MODEL), tok),
                  pl.BlockSpec((bt, 6, D_MODEL), lambda i, t: (i, 0, 0)),
                  pl.BlockSpec((bt, lt, 128), tok),
                  pl.BlockSpec(memory_space=pl.ANY),
                  pl.BlockSpec((1, 1, D_MODEL), lambda i, t: (layer, 0, 0)),
                  pl.BlockSpec((1, 1, D_MODEL), lambda i, t: (layer, 0, 0))],
        out_specs=pl.BlockSpec((bt, lt, D_MODEL), tok),
        out_shape=jax.ShapeDtypeStruct((nb, lp, D_MODEL), F32),
        scratch_shapes=[pltpu.VMEM((2, 2, bt * lt // 8, 8, D_MODEL), F32), pltpu.SemaphoreType.DMA((2,))],
        compiler_params=pltpu.CompilerParams(dimension_semantics=("arbitrary", "arbitrary"),
                                             vmem_limit_bytes=VMEM_LIMIT),
        name=f"combine_l{layer}_b{bt}",
    )(pos, pos, x1, mod, rinfo, eo, lg, lb)


def _hist(state):
    return jnp.pad(state, ((0, 0), (HIST - state.shape[1], 0), (0, 0)))


def _trunks(groups, p, mb):
    for g in groups:
        g.update(C=None, S=None, new={key: [] for key in ('n', 'm', 'conv', 'gconv')})
    n_tok = sum(g['x'].shape[0] * g['x'].shape[1] for g in groups)
    for l in range(DEPTH):
        cnt = jnp.zeros((8, 128), F32)
        for g in groups:
            st, mix, lv, nb = g['st'], g['mix'], g['lv'], g['x'].shape[0]
            proj, x = _inproj(g['x'], g['mod'][l], p['ln_in_g'], p['ln_in_b'], p['w_in_r'], p['b_in_r'], l, l == 0,
                              *g['ip_tile'])
            m0 = jnp.broadcast_to(st['m'][l][:, :, None, None], (nb, N_HEADS, 1, 128))
            yb, conv, *pre = _mlstm_prep(proj, _hist(st['conv'][l]), p['conv_b_w8'][l],
                                         mix['NB'], mix['G'], mix['tl'], lv)
            ya, g['C'], n, m = _mlstm_scan(pre, proj, st['C'], st['n'][l], m0, p['mlstm_norm_g'][l:l + 1],
                                           mix['SB'], mix['G'], mix['tl'], l, g['C'])
            *pre, gconv = _gdn_prep(proj, _hist(st['gconv'][l]), p['conv_c_w8'][l], p['alog_row'][l],
                                    p['dtb_row'][l], mix['NB'], mix['G'], mix['tl'], lv)
            yc, g['S'] = _gdn_scan(pre, proj, st['S'], p['gdn_norm_g'][l:l + 1], mix['SB'], mix['G'], mix['tl'],
                                   l, g['S'])
            g['x1'], g['rinfo'], cnt = _merge(ya, yb, yc, proj, x, g['mod'][l], p['w_br_a'], p['w_br_b'], p['w_br_c'],
                                              p['w_out'], p['ln1_g'], p['ln1_b'], p['wr_hi'], p['wr_lo'], p['br'],
                                              cnt, l, g['bt'], g['lt'])
            g['cnt'] = cnt
            for key, val in (('n', n), ('m', m[:, :, 0, 0]), ('conv', conv), ('gconv', gconv)):
                g['new'][key].append(val)
        first_row, block_e, n_used, ztail, n_blocks = _moe_plan(cnt, groups[0]['cnt'], n_tok, mb)
        disp = None
        for g in groups:
            g['pos'] = _dispatch_rows(g['rinfo'], first_row, g['bt'] * g['lt'])
            disp = _dispatch(g['x1'], g['mod'][l], g['pos'], ztail, n_blocks, mb, g['bt'], g['lt'], disp)
        eo = _experts(disp, block_e, n_used, p['exp_w_gate'], p['exp_w_up'], p['exp_w_down'], l, mb)
        for g in groups:
            g['x'] = _combine(g['x1'], g['mod'][l], g['rinfo'], eo, g['pos'], p['ln2_g'], p['ln2_b'], l,
                              g['bt'], g['lt'])
    return [(g['x'], dict({key: jnp.stack(val) for key, val in g['new'].items()}, C=g['C'], S=g['S']))
            for g in groups]


def kernel(x_prompt, x_sample, state_mlstm_C, state_mlstm_n, state_mlstm_m, state_conv, state_gdn_S, state_gdn_conv, c_prompt, c_sample, ln_in_g, ln_in_b, w_ada, b_ada, w_in, b_in, mlstm_norm_g, conv_b_w, conv_c_w, gdn_a_log, gdn_dt_bias, gdn_norm_g, w_br_a, w_br_b, w_br_c, w_out, ln1_g, ln1_b, router_g_w, router_g_b, router_e_w, router_e_b, exp_w_gate, exp_w_up, exp_w_down, ln2_g, ln2_b):
    nbp, lp, _ = x_prompt.shape
    nbs, ls, _ = x_sample.shape
    lsp = 8

    def regroup(a):
        out = jnp.zeros(a.shape[:-1] + (N_PROJ,), a.dtype)
        for src, end, dst in PROJ_SEGMENTS:
            out = out.at[..., dst:dst + end - src].set(a[..., src:end])
        return out

    wr = jnp.concatenate([router_e_w, router_g_w, jnp.zeros((DEPTH, D_MODEL, 128 - N_EXPERTS - N_GROUPS), F32)], axis=-1)
    wr_hi = wr.astype(BF16)
    lane_pad = lambda a: jnp.pad(a, ((0, 0), (4, 128 - 4 - N_HEADS)))[:, None, :]
    p = dict(
        ln_in_g=ln_in_g.reshape(1, D_MODEL), ln_in_b=ln_in_b.reshape(1, D_MODEL),
        w_in_r=_regroup_w_in(w_in), b_in_r=regroup(b_in).reshape(DEPTH, 1, N_PROJ),
        mlstm_norm_g=mlstm_norm_g,
        conv_b_w8=jnp.pad(conv_b_w, ((0, 0), (0, 8 - CONV_B), (0, 0))),
        conv_c_w8=jnp.pad(conv_c_w, ((0, 0), (0, 8 - CONV_C), (0, 0))),
        alog_row=lane_pad(gdn_a_log), dtb_row=lane_pad(gdn_dt_bias), gdn_norm_g=gdn_norm_g,
        w_br_a=w_br_a.astype(BF16), w_br_b=w_br_b.astype(BF16), w_br_c=w_br_c.astype(BF16),
        w_out=w_out.astype(BF16),
        ln1_g=ln1_g.reshape(DEPTH, 1, D_MODEL), ln1_b=ln1_b.reshape(DEPTH, 1, D_MODEL),
        wr_hi=wr_hi, wr_lo=(wr - wr_hi.astype(F32)).astype(BF16),
        br=jnp.concatenate([router_e_b, router_g_b, jnp.zeros((DEPTH, 128 - N_EXPERTS - N_GROUPS), F32)],
                           axis=-1).reshape(DEPTH, 1, 128),
        exp_w_gate=exp_w_gate, exp_w_up=exp_w_up, exp_w_down=exp_w_down,
        ln2_g=ln2_g.reshape(DEPTH, 1, D_MODEL), ln2_b=ln2_b.reshape(DEPTH, 1, D_MODEL),
    )

    mod = _ada(jnp.concatenate([c_prompt, c_sample], axis=0), w_ada, b_ada)
    mod = mod.reshape(DEPTH, nbp + nbs, 6, D_MODEL)

    zeros = lambda *s: jnp.zeros((DEPTH, nbp) + s, F32)
    st_p = {'C': zeros(N_HEADS, DH, DH), 'n': zeros(N_HEADS, DH), 'm': zeros(N_HEADS),
            'conv': zeros(CONV_B - 1, D_MIX), 'S': zeros(N_HEADS, DH, DH), 'gconv': zeros(CONV_C - 1, 3 * D_MIX)}
    prompt = dict(x=x_prompt, mod=mod[:, :nbp], st=st_p, bt=1, lt=512, lv=lp,
                  mix=dict(NB=8, G=1, tl=RB, SB=nbp), ip_tile=(1, 256))

    st_s = {'C': state_mlstm_C, 'n': state_mlstm_n, 'm': state_mlstm_m, 'conv': state_conv,
            'S': state_gdn_S, 'gconv': state_gdn_conv}
    xs = jnp.pad(x_sample, ((0, 0), (0, lsp - ls), (0, 0)))
    sample = dict(x=xs, mod=mod[:, nbp:], st=st_s, bt=64, lt=lsp, lv=ls,
                  mix=dict(NB=2, G=RB // lsp, tl=lsp, SB=1), ip_tile=(32, lsp))
    (y_p, sp), (y_s, ss) = _trunks([prompt, sample], p, mb=512)
    y_s = y_s[:, :ls]

    return (y_p, y_s, sp['C'], sp['n'], sp['m'], sp['conv'], sp['S'], sp['gconv'],
            ss['C'], ss['n'], ss['m'], ss['conv'], ss['S'], ss['gconv'])
```

```python
import functools

import jax
import jax.numpy as jnp
from jax import lax
from jax.experimental import pallas as pl
from jax.experimental.pallas import tpu as pltpu

F32 = jnp.float32
BF16 = jnp.bfloat16

LANES = 128
SUBLANES = 8

D_MODEL = 1024
DEPTH = 2
N_HEADS = 4
DH = 128
D_MIX = N_HEADS * DH
N_EXPERTS = 32
EXPERTS_PER_GROUP = 8
N_GROUPS = 4
TOP_K = 2
D_EXPERT = 256
CONV_B = 3
CONV_C = 4
HIST = SUBLANES
RB = 64
INV_BLOCK = 8
DN_ALPHA = (2 * DEPTH) ** 0.25
LN_EPS = 1e-5
NORM_EPS = 1e-6
NEG = -1e30

OFF_QKVC = 0
OFF_BCH = 1536
OFF_GTS = 3072
OFF_QKVO = 6144
OFF_Z = 8192
OFF_SA = 8704
OFF_SB = 8832
N_PROJ = 8960
TN_PROJ = 1280


def _proj_segments():
    widths = (('qkvo', 4 * D_MIX), ('i', N_HEADS), ('f', N_HEADS), ('bch', 3 * D_MIX), ('qkvc', 3 * D_MIX),
              ('z', D_MIX), ('beta', N_HEADS), ('a', N_HEADS), ('gts', 3 * D_MODEL))
    dst = dict(qkvc=OFF_QKVC, bch=OFF_BCH, gts=OFF_GTS, qkvo=OFF_QKVO, z=OFF_Z,
               i=OFF_SA, beta=OFF_SA + N_HEADS, f=OFF_SB, a=OFF_SB + N_HEADS)
    segs, col = [], 0
    for name, width in widths:
        segs.append((col, col + width, dst[name]))
        col += width
    return tuple(segs)


PROJ_SEGMENTS = _proj_segments()

VMEM_LIMIT = 52 * 1024 * 1024


def _dot(a, b):
    return jnp.dot(a, b, preferred_element_type=F32)


def _split3(x):
    hi = x.astype(BF16)
    r = x - hi.astype(F32)
    mid = r.astype(BF16)
    lo = (r - mid.astype(F32)).astype(BF16)
    return hi, mid, lo


def _layer_norm(x, g, b):
    mu = jnp.mean(x, axis=-1, keepdims=True)
    xc = x - mu
    var = jnp.mean(xc * xc, axis=-1, keepdims=True)
    return xc * lax.rsqrt(var + LN_EPS) * g + b


def _sigmoid(x):
    return jax.nn.sigmoid(x)


def _silu(x):
    return x * _sigmoid(x)


def _log_sigmoid(x):
    return jnp.minimum(x, 0.0) - jnp.log1p(jnp.exp(-jnp.abs(x)))


def _softplus(x):
    return jnp.maximum(x, 0.0) + jnp.log1p(jnp.exp(-jnp.abs(x)))


def _ada_body(c_ref, w_ref, b_ref, o_ref):
    c = c_ref[...]
    s = _silu(c).astype(BF16)
    o_ref[0] = _dot(s, w_ref[0].astype(BF16)) + b_ref[0]


def _ada(c_all, w_ada, b_ada):
    nb = c_all.shape[0]
    return pl.pallas_call(
        _ada_body,
        grid=(DEPTH, 6),
        in_specs=[pl.BlockSpec((nb, D_MODEL), lambda l, j: (0, 0)),
                  pl.BlockSpec((1, D_MODEL, D_MODEL), lambda l, j: (l, 0, j)),
                  pl.BlockSpec((1, 1, D_MODEL), lambda l, j: (l, 0, j))],
        out_specs=pl.BlockSpec((1, nb, D_MODEL), lambda l, j: (l, 0, j)),
        out_shape=jax.ShapeDtypeStruct((DEPTH, nb, 6 * D_MODEL), F32),
        compiler_params=pltpu.CompilerParams(dimension_semantics=("arbitrary", "arbitrary"),
                                             vmem_limit_bytes=VMEM_LIMIT),
        name="ada",
    )(c_all, w_ada, b_ada.reshape(DEPTH, 1, 6 * D_MODEL))


def _regroup_body(w_ref, o_ref):
    o_ref[...] = jnp.zeros_like(o_ref)
    for src, end, dst in PROJ_SEGMENTS:
        o_ref[0, :, dst:dst + end - src] = w_ref[0, :, src:end].astype(BF16)


def _regroup_w_in(w_in):
    n_in = w_in.shape[-1]
    rows = 256
    return pl.pallas_call(
        _regroup_body,
        grid=(DEPTH, D_MODEL // rows),
        in_specs=[pl.BlockSpec((1, rows, n_in), lambda l, i: (l, i, 0))],
        out_specs=pl.BlockSpec((1, rows, N_PROJ), lambda l, i: (l, i, 0)),
        out_shape=jax.ShapeDtypeStruct((DEPTH, D_MODEL, N_PROJ), BF16),
        compiler_params=pltpu.CompilerParams(dimension_semantics=("arbitrary", "arbitrary"),
                                             vmem_limit_bytes=VMEM_LIMIT),
        name="regroup_w_in",
    )(w_in)


def _inproj_body(x_ref, mod_ref, g_ref, b_ref, w_ref, bias_ref, proj_ref, *rest, apply_ln, bt, lt):
    x = x_ref[...]
    if apply_ln:
        x = _layer_norm(x, g_ref[...], b_ref[...])
        rest[0][...] = x
    u = (x * (1.0 + mod_ref[:, 1:2, :]) + mod_ref[:, 0:1, :]).reshape(bt * lt, D_MODEL).astype(BF16)
    for j in range(N_PROJ // TN_PROJ):
        cs = slice(j * TN_PROJ, (j + 1) * TN_PROJ)
        proj_ref[:, :, cs] = (_dot(u, w_ref[0, :, cs]) + bias_ref[0, :, cs]).reshape(bt, lt, TN_PROJ)


def _inproj(x, mod, ln_g, ln_b, w_r, b_r, layer, apply_ln, bt, lt):
    nb, lp, _ = x.shape
    tok = lambda i, t: (i, t, 0)
    out_shape = [jax.ShapeDtypeStruct((nb, lp, N_PROJ), F32)]
    out_specs = [pl.BlockSpec((bt, lt, N_PROJ), tok)]
    if apply_ln:
        out_shape.append(jax.ShapeDtypeStruct((nb, lp, D_MODEL), F32))
        out_specs.append(pl.BlockSpec((bt, lt, D_MODEL), tok))
    res = pl.pallas_call(
        functools.partial(_inproj_body, apply_ln=apply_ln, bt=bt, lt=lt),
        grid=(nb // bt, lp // lt),
        in_specs=[pl.BlockSpec((bt, lt, D_MODEL), tok),
                  pl.BlockSpec((bt, 6, D_MODEL), lambda i, t: (i, 0, 0)),
                  pl.BlockSpec((1, D_MODEL), lambda i, t: (0, 0)),
                  pl.BlockSpec((1, D_MODEL), lambda i, t: (0, 0)),
                  pl.BlockSpec((1, D_MODEL, N_PROJ), lambda i, t: (layer, 0, 0), pipeline_mode=pl.Buffered(1)),
                  pl.BlockSpec((1, 1, N_PROJ), lambda i, t: (layer, 0, 0))],
        out_specs=out_specs,
        out_shape=out_shape,
        compiler_params=pltpu.CompilerParams(dimension_semantics=("arbitrary", "arbitrary"),
                                             vmem_limit_bytes=VMEM_LIMIT),
        name=f"inproj_l{layer}_b{bt}",
    )(x, mod, ln_g, ln_b, w_r, b_r)
    return (res[0], res[1]) if apply_ln else (res[0], x)


def _conv_taps(xp_s, w_ref, width, tl):
    acc = None
    for j in range(width):
        tap = xp_s[:, pl.ds(HIST - (width - 1) + j, tl), :] * w_ref[j:j + 1, :].reshape(1, 1, -1)
        acc = tap if acc is None else acc + tap
    return acc


def _conv_history(xp_s, hist_ref, prev_ref, n_steps):
    if n_steps > 1:
        @pl.when(pl.program_id(1) == 0)
        def _():
            xp_s[:, 0:HIST, :] = hist_ref[...]

        @pl.when(pl.program_id(1) > 0)
        def _():
            xp_s[:, 0:HIST, :] = prev_ref[...]
    else:
        xp_s[:, 0:HIST, :] = hist_ref[...]


def _heads(x, nb, width):
    return jnp.stack([x[:, :, h * width:(h + 1) * width] for h in range(N_HEADS)],
                     axis=1).reshape(nb * N_HEADS, RB, width)


def _gate_cols(x, nb, lane0):
    return jnp.stack([x[:, :, lane0 + h:lane0 + h + 1] for h in range(N_HEADS)],
                     axis=1).reshape(nb * N_HEADS, RB, 1)


def _gate_rows(x, nb, lane0):
    xt = jnp.swapaxes(x, 1, 2)
    return jnp.stack([xt[:, lane0 + h:lane0 + h + 1, :] for h in range(N_HEADS)],
                     axis=1).reshape(nb * N_HEADS, 1, RB)


def _bmm(a, b):
    return jnp.einsum('nts,nsu->ntu', a.astype(BF16), b.astype(BF16), preferred_element_type=F32)


def _bmm_nt(a, b):
    return jnp.einsum('ntd,nsd->nts', a, b, preferred_element_type=F32)


def _block_masks(tl):
    rr = lax.broadcasted_iota(jnp.int32, (RB, RB), 0)
    cc = lax.broadcasted_iota(jnp.int32, (RB, RB), 1)
    incl = rr >= cc
    if tl < RB:
        incl = incl & ((rr // tl) == (cc // tl))
    return rr, cc, incl


def _seq_cumsum(x, incl, nb):
    tril = jnp.broadcast_to(incl.astype(BF16)[None], (nb, RB, RB))
    hi, mid, lo = _split3(x)
    return _bmm(tril, hi) + _bmm(tril, mid) + _bmm(tril, lo)


def _seq_last(x, nb, G, tl):
    return x.reshape(nb * G, tl, 128)[:, tl - 1:tl, :]


def _seq_rows(x3, nb, tl):
    return jnp.broadcast_to(x3, (x3.shape[0], tl, 128)).reshape(nb, RB, 128)


def _put_chains(ref, val, nb, G, tl):
    val4 = val.reshape(nb, N_HEADS, RB, val.shape[-1])
    for h in range(N_HEADS):
        ref[:, :, h] = val4[:, h].reshape(nb, G, tl, val.shape[-1])


def _put_seq_scalars(ref, x3, nb, G, lane0):
    x4 = x3.reshape(nb, G, 1, 128)
    for h in range(N_HEADS):
        ref[:, :, h] = jnp.broadcast_to(x4[:, :, :, lane0 + h:lane0 + h + 1], (nb, G, 1, 128))


def _step_tiling(nb, lp, NB, G, tl):
    assert G * tl == RB
    if G == 1:
        gx, tlx = 1, NB * RB
    else:
        assert lp == tl
        gx, tlx = NB * G, tl
    return gx, tlx, nb // gx, lp // tlx


def _layered_state(layer, nb, NS, prev):
    st_in = pl.BlockSpec((None, NS, N_HEADS, DH, DH), lambda i, c: (layer, i, 0, 0, 0))
    shape = jax.ShapeDtypeStruct((DEPTH, nb, N_HEADS, DH, DH), F32)
    if layer == 0:
        return st_in, pl.BlockSpec((DEPTH, NS, N_HEADS, DH, DH), lambda i, c: (0, i, 0, 0, 0)), shape, [], []
    return st_in, st_in, shape, [pl.BlockSpec(memory_space=pl.ANY)], [prev]


def _fill_later_layers(ref, when):
    @pl.when(when)
    def _():
        for l in range(1, DEPTH):
            ref[l] = ref[0]


def _mlstm_prep_body(qkvo_ref, bch_ref, prev_ref, sa_ref, sb_ref, cv0_ref, cw_ref,
                     yb_ref, cv_ref, nv_ref, q_ref, kw_ref, v_ref, rows_ref, bl_ref, bc_ref, kn_ref, xp_s,
                     *, NB, G, tl, tlx, NCS, lv):
    cs = pl.program_id(1)
    n = NB * N_HEADS
    lvl = lv - (NCS - 1) * tlx

    if NCS > 1:
        @pl.when(cs == 0)
        def _():
            xp_s[:, 0:HIST, :] = cv0_ref[...]

        @pl.when(cs > 0)
        def _():
            xp_s[:, 0:HIST, :] = prev_ref[:, :, D_MIX:2 * D_MIX] * prev_ref[:, :, 2 * D_MIX:3 * D_MIX]
    else:
        xp_s[:, 0:HIST, :] = cv0_ref[...]
    xp_s[:, HIST:HIST + tlx, :] = bch_ref[:, :, D_MIX:2 * D_MIX] * bch_ref[:, :, 2 * D_MIX:3 * D_MIX]
    yb_ref[...] = (bch_ref[:, :, 0:D_MIX] * _conv_taps(xp_s, cw_ref, CONV_B, tlx)).astype(BF16)

    @pl.when(cs == NCS - 1)
    def _():
        cv_ref[...] = xp_s[:, pl.ds(HIST + lvl - (CONV_B - 1), CONV_B - 1), :]

    i_all = sa_ref[...].reshape(NB, RB, 128)
    f_all = _log_sigmoid(sb_ref[...].reshape(NB, RB, 128))
    if lv < NCS * tlx:
        assert NCS == 1
        valid = (lax.broadcasted_iota(jnp.int32, (NB, RB, 128), 1) % tl) < lv
        i_all = jnp.where(valid, i_all, NEG)
        f_all = jnp.where(valid, f_all, 0.0)
    _, _, incl = _block_masks(tl)
    bcum = _seq_cumsum(f_all, incl, NB)
    blast = _seq_last(bcum, NB, G, tl)
    val = _seq_rows(blast, NB, tl) - bcum + i_all
    bmax = jnp.max(val.reshape(NB * G, tl, 128), axis=1, keepdims=True)
    wk0 = jnp.exp(val - _seq_rows(bmax, NB, tl))

    qkvo = qkvo_ref[...].reshape(NB, RB, 4 * D_MIX)
    q = _heads(qkvo[:, :, 0:D_MIX], NB, DH)
    k = _heads(qkvo[:, :, D_MIX:2 * D_MIX], NB, DH) * (DH ** -0.5)
    v = _heads(qkvo[:, :, 2 * D_MIX:3 * D_MIX], NB, DH)
    qb, kb, vb = q.astype(BF16), k.astype(BF16), v.astype(BF16)

    b_col = _gate_cols(bcum, NB, 0)
    dlog = jnp.where(incl[None], b_col - _gate_rows(bcum, NB, 0) + _gate_rows(i_all, NB, 0), NEG)
    d = jnp.max(dlog, axis=-1, keepdims=True)
    s0 = _bmm_nt(qb, kb) * jnp.exp(dlog - d)
    kw0 = k * _gate_cols(wk0, NB, 0)

    _put_chains(nv_ref, _bmm(s0, vb), NB, G, tl)
    _put_chains(q_ref, qb, NB, G, tl)
    _put_chains(kw_ref, kw0.astype(BF16), NB, G, tl)
    _put_chains(v_ref, vb, NB, G, tl)
    rows = jnp.swapaxes(jnp.concatenate([d, b_col, jnp.sum(s0, axis=-1, keepdims=True),
                                         jnp.zeros((n, RB, 5), F32)], axis=-1), 1, 2).reshape(NB, N_HEADS, 8, RB)
    for h in range(N_HEADS):
        for g in range(G):
            rows_ref[:, g, h] = rows[:, h, :, g * tl:(g + 1) * tl]
    _put_seq_scalars(bl_ref, blast, NB, G, 0)
    _put_seq_scalars(bc_ref, bmax, NB, G, 0)
    kn = jnp.sum(kw0.reshape(n * G, tl, DH), axis=1, keepdims=True).reshape(NB, N_HEADS, G, 1, DH)
    for h in range(N_HEADS):
        kn_ref[:, :, h] = kn[:, h]


def _mlstm_prep(proj, cv0, cw, NB, G, tl, lv):
    nb, lp, _ = proj.shape
    gx, tlx, NI, NCS = _step_tiling(nb, lp, NB, G, tl)
    nbt = nb * lp // RB
    step = lambda i, c: (i * NCS + c, 0, 0, 0, 0)
    chain = lambda last, dt: jax.ShapeDtypeStruct((nbt, G, N_HEADS, tl, last), dt)
    cspec = lambda last: pl.BlockSpec((NB, G, N_HEADS, tl, last), step)
    scal = jax.ShapeDtypeStruct((nbt, G, N_HEADS, 1, 128), F32)
    sspec = pl.BlockSpec((NB, G, N_HEADS, 1, 128), step)
    bch = OFF_BCH // (3 * D_MIX)
    return pl.pallas_call(
        functools.partial(_mlstm_prep_body, NB=NB, G=G, tl=tl, tlx=tlx, NCS=NCS, lv=lv),
        grid=(NI, NCS),
        in_specs=[pl.BlockSpec((gx, tlx, 4 * D_MIX), lambda i, c: (i, c, OFF_QKVO // (4 * D_MIX))),
                  pl.BlockSpec((gx, tlx, 3 * D_MIX), lambda i, c: (i, c, bch)),
                  pl.BlockSpec((gx, HIST, 3 * D_MIX), lambda i, c: (i, jnp.maximum(c * (tlx // HIST) - 1, 0), bch)),
                  pl.BlockSpec((gx, tlx, 128), lambda i, c: (i, c, OFF_SA // 128)),
                  pl.BlockSpec((gx, tlx, 128), lambda i, c: (i, c, OFF_SB // 128)),
                  pl.BlockSpec((gx, HIST, D_MIX), lambda i, c: (i, 0, 0)),
                  pl.BlockSpec((8, D_MIX), lambda i, c: (0, 0))],
        out_specs=[pl.BlockSpec((gx, tlx, D_MIX), lambda i, c: (i, c, 0)),
                   pl.BlockSpec((gx, CONV_B - 1, D_MIX), lambda i, c: (i, 0, 0)),
                   cspec(DH), cspec(DH), cspec(DH), cspec(DH),
                   pl.BlockSpec((NB, G, N_HEADS, 8, tl), step), sspec, sspec, sspec],
        out_shape=[jax.ShapeDtypeStruct((nb, lp, D_MIX), BF16),
                   jax.ShapeDtypeStruct((nb, CONV_B - 1, D_MIX), F32),
                   chain(DH, F32), chain(DH, BF16), chain(DH, BF16), chain(DH, BF16),
                   jax.ShapeDtypeStruct((nbt, G, N_HEADS, 8, tl), F32),
                   scal, scal, scal],
        scratch_shapes=[pltpu.VMEM((gx, HIST + tlx, D_MIX), F32)],
        compiler_params=pltpu.CompilerParams(dimension_semantics=("arbitrary", "arbitrary"),
                                             vmem_limit_bytes=VMEM_LIMIT),
        name=f"mlstm_prep_g{G}",
    )(proj, proj, proj, proj, proj, cv0, cw)


def _mlstm_scan_body(nv_ref, q_ref, kw_ref, v_ref, rows_ref, bl_ref, bc_ref, kn_ref, o_ref, C0_ref, n0_ref, m0_ref,
                     ng_ref, *rest, NS, tl, NC, first):
    ya_ref, C_out, n_ref, m_ref = rest[-4:]
    C_ref = C_out.at[0] if first else C_out
    c = pl.program_id(1)
    n = NS * N_HEADS

    @pl.when(c == 0)
    def _():
        C_ref[...] = C0_ref[...]
        n_ref[...] = n0_ref[...]
        m_ref[...] = m0_ref[...]

    chains = lambda ref: ref[...].reshape(n, ref.shape[-2], ref.shape[-1])
    C = C_ref[...].reshape(n, DH, DH)
    nvec = n_ref[...].reshape(n, 1, DH)
    m_prev = chains(m_ref)[:, :, 0:1]
    rows = chains(rows_ref)
    d, b, ds0 = rows[:, 0:1, :], rows[:, 1:2, :], rows[:, 2:3, :]
    qb = chains(q_ref)

    m_inter = b + m_prev
    m_t = jnp.maximum(m_inter, d)
    f = jnp.exp(d - m_t)
    inter = jnp.exp(m_inter - m_t)
    qn = jnp.einsum('nod,ntd->not', nvec.astype(BF16), qb, preferred_element_type=F32)
    den = f * ds0 + inter * qn
    scale_t = 1.0 / jnp.maximum(jnp.abs(den), jnp.exp(-m_t))
    fi = jnp.swapaxes(jnp.concatenate([f * scale_t, inter * scale_t, jnp.zeros((n, 6, tl), F32)], axis=1), 1, 2)
    qC = jnp.einsum('ntd,nde->nte', qb, C.astype(BF16), preferred_element_type=F32)
    hh = fi[:, :, 0:1] * chains(nv_ref) + fi[:, :, 1:2] * qC

    m_new = m_t[:, :, tl - 1:tl]
    decay = jnp.exp(chains(bl_ref)[:, :, 0:1] + m_prev - m_new)
    scale = jnp.exp(chains(bc_ref)[:, :, 0:1] - m_new)
    kv = jnp.einsum('ntd,nte->nde', chains(kw_ref), chains(v_ref), preferred_element_type=F32)
    C_ref[...] = (decay * C + scale * kv).reshape(NS, N_HEADS, DH, DH)
    n_ref[...] = (decay * nvec + scale * chains(kn_ref)).reshape(NS, N_HEADS, DH)
    m_ref[...] = jnp.broadcast_to(m_new, (n, 1, 128)).reshape(NS, N_HEADS, 1, 128)
    if first:
        _fill_later_layers(C_out, c == NC - 1)

    mu = jnp.mean(hh, axis=-1, keepdims=True)
    hc = hh - mu
    hn = (hc * lax.rsqrt(jnp.mean(hc * hc, axis=-1, keepdims=True) + LN_EPS)).reshape(NS, N_HEADS, tl, DH)
    for h in range(N_HEADS):
        hs = slice(h * DH, (h + 1) * DH)
        ya_ref[:, :, hs] = (_sigmoid(o_ref[:, :, hs]) * hn[:, h] * ng_ref[:, hs]).astype(BF16)


def _mlstm_scan(pre, proj, C0, n0, m0, ng, SB, G, tl, layer, C_prev):
    nb, lp, _ = proj.shape
    NS = SB * G
    NI, NC = nb // NS, lp // tl
    six = lambda a: a.reshape((NI * SB, NC) + a.shape[1:])
    spec6 = lambda a: pl.BlockSpec((SB, 1) + a.shape[1:], lambda i, c: (i, c, 0, 0, 0, 0))
    seq4 = lambda i, c: (i, 0, 0, 0)
    st_in, st_out, st_shape, extra_in, extra_args = _layered_state(layer, nb, NS, C_prev)
    n_in = len(pre) + 5
    return pl.pallas_call(
        functools.partial(_mlstm_scan_body, NS=NS, tl=tl, NC=NC, first=layer == 0),
        grid=(NI, NC),
        in_specs=[spec6(a) for a in pre] + [
            pl.BlockSpec((NS, tl, D_MIX), lambda i, c: (i, c, (OFF_QKVO + 3 * D_MIX) // D_MIX)),
            st_in,
            pl.BlockSpec((NS, N_HEADS, DH), lambda i, c: (i, 0, 0)),
            pl.BlockSpec((NS, N_HEADS, 1, 128), seq4),
            pl.BlockSpec((1, D_MIX), lambda i, c: (0, 0))] + extra_in,
        out_specs=[pl.BlockSpec((NS, tl, D_MIX), lambda i, c: (i, c, 0)),
                   st_out,
                   pl.BlockSpec((NS, N_HEADS, DH), lambda i, c: (i, 0, 0)),
                   pl.BlockSpec((NS, N_HEADS, 1, 128), seq4)],
        out_shape=[jax.ShapeDtypeStruct((nb, lp, D_MIX), BF16),
                   st_shape,
                   jax.ShapeDtypeStruct((nb, N_HEADS, DH), F32),
                   jax.ShapeDtypeStruct((nb, N_HEADS, 1, 128), F32)],
        input_output_aliases={n_in: 1} if extra_in else {},
        compiler_params=pltpu.CompilerParams(dimension_semantics=("arbitrary", "arbitrary"),
                                             vmem_limit_bytes=VMEM_LIMIT),
        name=f"mlstm_scan_g{G}",
    )(*[six(a) for a in pre], proj, C0, n0, m0, ng, *extra_args)


def _unit_lower_inverse(n, rr, cc, tl):
    eye = (rr == cc).astype(F32)[None]
    p = jnp.where(((rr // INV_BLOCK) == (cc // INV_BLOCK))[None], n, 0.0)
    x = eye + p
    b = 2
    while b < INV_BLOCK:
        p = _bmm(p, p)
        x = x + _bmm(x, p)
        b *= 2
    b = INV_BLOCK
    while b < tl:
        off = jnp.where((((rr // (2 * b)) == (cc // (2 * b))) & ((rr // b) != (cc // b)))[None], n, 0.0)
        x = x + _bmm(x, _bmm(off, x))
        b *= 2
    return x


def _gdn_prep_body(x_ref, prev_ref, sa_ref, sb_ref, gc0_ref, cw_ref, alog_ref, dtb_ref,
                   u_ref, w_ref, qg_ref, kd_ref, qkm_ref, eg_ref, gcs_ref, xp_s, *, NB, G, tl, tlx, NCS, lv):
    cs = pl.program_id(1)
    lvl = lv - (NCS - 1) * tlx

    _conv_history(xp_s, gc0_ref, prev_ref, NCS)
    xp_s[:, HIST:HIST + tlx, :] = x_ref[...]
    qkv = _silu(_conv_taps(xp_s, cw_ref, CONV_C, tlx)).reshape(NB, RB, 3 * D_MIX)

    @pl.when(cs == NCS - 1)
    def _():
        gcs_ref[...] = xp_s[:, pl.ds(HIST + lvl - (CONV_C - 1), CONV_C - 1), :]

    beta_all = _sigmoid(sa_ref[...].reshape(NB, RB, 128))
    g_all = -jnp.exp(alog_ref[...]) * _softplus(sb_ref[...].reshape(NB, RB, 128) + dtb_ref[...])
    if lv < NCS * tlx:
        assert NCS == 1
        valid = (lax.broadcasted_iota(jnp.int32, (NB, RB, 128), 1) % tl) < lv
        beta_all = jnp.where(valid, beta_all, 0.0)
        g_all = jnp.where(valid, g_all, 0.0)
    rr, cc, incl = _block_masks(tl)
    diag = rr == cc
    gam = _seq_cumsum(g_all, incl, NB)
    glast = _seq_last(gam, NB, G, tl)
    gcol = _gate_cols(gam, NB, 4)
    bcol = _gate_cols(beta_all, NB, 4)
    egcol = _gate_cols(jnp.exp(gam), NB, 4)
    kdcol = _gate_cols(jnp.exp(_seq_rows(glast, NB, tl) - gam), NB, 4)

    q = _heads(qkv[:, :, 0:D_MIX], NB, DH)
    k = _heads(qkv[:, :, D_MIX:2 * D_MIX], NB, DH)
    v = _heads(qkv[:, :, 2 * D_MIX:3 * D_MIX], NB, DH)
    q = q * lax.rsqrt(jnp.sum(q * q, axis=-1, keepdims=True) + NORM_EPS) * (DH ** -0.5)
    k = k * lax.rsqrt(jnp.sum(k * k, axis=-1, keepdims=True) + NORM_EPS)
    qb, kb = q.astype(BF16), k.astype(BF16)

    dmat = jnp.exp(jnp.where(incl[None], gcol - _gate_rows(gam, NB, 4), NEG))
    nmat = jnp.where(diag[None], 0.0, -(bcol * _bmm_nt(kb, kb) * dmat))
    rhs = jnp.concatenate([bcol * v, (bcol * egcol) * k], axis=-1)
    sol = _bmm(_unit_lower_inverse(nmat, rr, cc, tl), rhs)
    qkm = (_bmm_nt(qb, kb) * dmat).astype(BF16).reshape(NB, N_HEADS, RB, RB)

    _put_chains(u_ref, sol[:, :, 0:DH], NB, G, tl)
    _put_chains(w_ref, sol[:, :, DH:2 * DH].astype(BF16), NB, G, tl)
    _put_chains(qg_ref, (q * egcol).astype(BF16), NB, G, tl)
    _put_chains(kd_ref, (k * kdcol).astype(BF16), NB, G, tl)
    _put_seq_scalars(eg_ref, jnp.exp(glast), NB, G, 4)
    for h in range(N_HEADS):
        for g in range(G):
            qkm_ref[:, g, h] = qkm[:, h, g * tl:(g + 1) * tl, g * tl:(g + 1) * tl]


def _gdn_prep(proj, gc0, cw, alog, dtb, NB, G, tl, lv):
    nb, lp, _ = proj.shape
    assert tl % INV_BLOCK == 0 and (tl // INV_BLOCK) & (tl // INV_BLOCK - 1) == 0
    gx, tlx, NI, NCS = _step_tiling(nb, lp, NB, G, tl)
    nbt = nb * lp // RB
    step = lambda i, c: (i * NCS + c, 0, 0, 0, 0)
    par = lambda i, c: (0, 0)
    chain = lambda last, dt: jax.ShapeDtypeStruct((nbt, G, N_HEADS, tl, last), dt)
    cspec = lambda last: pl.BlockSpec((NB, G, N_HEADS, tl, last), step)
    return pl.pallas_call(
        functools.partial(_gdn_prep_body, NB=NB, G=G, tl=tl, tlx=tlx, NCS=NCS, lv=lv),
        grid=(NI, NCS),
        in_specs=[pl.BlockSpec((gx, tlx, 3 * D_MIX), lambda i, c: (i, c, OFF_QKVC // (3 * D_MIX))),
                  pl.BlockSpec((gx, HIST, 3 * D_MIX),
                               lambda i, c: (i, jnp.maximum(c * (tlx // HIST) - 1, 0), OFF_QKVC // (3 * D_MIX))),
                  pl.BlockSpec((gx, tlx, 128), lambda i, c: (i, c, OFF_SA // 128)),
                  pl.BlockSpec((gx, tlx, 128), lambda i, c: (i, c, OFF_SB // 128)),
                  pl.BlockSpec((gx, HIST, 3 * D_MIX), lambda i, c: (i, 0, 0)),
                  pl.BlockSpec((8, 3 * D_MIX), par),
                  pl.BlockSpec((1, 128), par),
                  pl.BlockSpec((1, 128), par)],
        out_specs=[cspec(DH), cspec(DH), cspec(DH), cspec(DH), cspec(tl),
                   pl.BlockSpec((NB, G, N_HEADS, 1, 128), step),
                   pl.BlockSpec((gx, CONV_C - 1, 3 * D_MIX), lambda i, c: (i, 0, 0))],
        out_shape=[chain(DH, F32), chain(DH, BF16), chain(DH, BF16), chain(DH, BF16), chain(tl, BF16),
                   jax.ShapeDtypeStruct((nbt, G, N_HEADS, 1, 128), F32),
                   jax.ShapeDtypeStruct((nb, CONV_C - 1, 3 * D_MIX), F32)],
        scratch_shapes=[pltpu.VMEM((gx, HIST + tlx, 3 * D_MIX), F32)],
        compiler_params=pltpu.CompilerParams(dimension_semantics=("arbitrary", "arbitrary"),
                                             vmem_limit_bytes=VMEM_LIMIT),
        name=f"gdn_prep_g{G}",
    )(proj, proj, proj, proj, gc0, cw, alog, dtb)


def _gdn_scan_body(u_ref, w_ref, qg_ref, kd_ref, qkm_ref, eg_ref, z_ref, S0_ref, gng_ref, *rest,
                   NS, tl, NC, first):
    yc_ref, S_out = rest[-2:]
    S_ref = S_out.at[0] if first else S_out
    c = pl.program_id(1)
    n = NS * N_HEADS

    @pl.when(c == 0)
    def _():
        S_ref[...] = S0_ref[...]

    S = S_ref[...].reshape(n, DH, DH)
    Sb = S.astype(BF16)
    chains = lambda ref: ref[...].reshape(n, tl, ref.shape[-1])
    v_new = chains(u_ref) - jnp.einsum('ntd,nde->nte', chains(w_ref), Sb, preferred_element_type=F32)
    vnb = v_new.astype(BF16)
    o = (jnp.einsum('ntd,nde->nte', chains(qg_ref), Sb, preferred_element_type=F32)
         + jnp.einsum('nts,nse->nte', chains(qkm_ref), vnb, preferred_element_type=F32))
    eg = eg_ref[...].reshape(n, 1, 128)[:, :, 0:1]
    S_new = eg * S + jnp.einsum('ntd,nte->nde', chains(kd_ref), vnb, preferred_element_type=F32)
    S_ref[...] = S_new.reshape(NS, N_HEADS, DH, DH)
    if first:
        _fill_later_layers(S_out, c == NC - 1)

    on = (o * lax.rsqrt(jnp.mean(o * o, axis=-1, keepdims=True) + NORM_EPS) * gng_ref[...]).reshape(NS, N_HEADS, tl, DH)
    for h in range(N_HEADS):
        yc_ref[:, :, h * DH:(h + 1) * DH] = (on[:, h] * _silu(z_ref[:, :, h * DH:(h + 1) * DH])).astype(BF16)


def _gdn_scan(pre, proj, S0, gng, SB, G, tl, layer, S_prev):
    nb, lp, _ = proj.shape
    NS = SB * G
    NI, NC = nb // NS, lp // tl
    six = lambda a: a.reshape((NI * SB, NC) + a.shape[1:])
    cspec = lambda last: pl.BlockSpec((SB, 1, G, N_HEADS, tl, last), lambda i, c: (i, c, 0, 0, 0, 0))
    u, w, qg, kd, qkm, eg = (six(a) for a in pre)
    st_in, st_out, st_shape, extra_in, extra_args = _layered_state(layer, nb, NS, S_prev)
    return pl.pallas_call(
        functools.partial(_gdn_scan_body, NS=NS, tl=tl, NC=NC, first=layer == 0),
        grid=(NI, NC),
        in_specs=[cspec(DH), cspec(DH), cspec(DH), cspec(DH), cspec(tl),
                  pl.BlockSpec((SB, 1, G, N_HEADS, 1, 128), lambda i, c: (i, c, 0, 0, 0, 0)),
                  pl.BlockSpec((NS, tl, D_MIX), lambda i, c: (i, c, OFF_Z // D_MIX)),
                  st_in,
                  pl.BlockSpec((1, DH), lambda i, c: (0, 0))] + extra_in,
        out_specs=[pl.BlockSpec((NS, tl, D_MIX), lambda i, c: (i, c, 0)), st_out],
        out_shape=[jax.ShapeDtypeStruct((nb, lp, D_MIX), BF16), st_shape],
        input_output_aliases={9: 1} if extra_in else {},
        compiler_params=pltpu.CompilerParams(dimension_semantics=("arbitrary", "arbitrary"),
                                             vmem_limit_bytes=VMEM_LIMIT),
        name=f"gdn_scan_g{G}",
    )(u, w, qg, kd, qkm, eg, proj, S0, gng, *extra_args)


def _route(rl):
    lane = lax.broadcasted_iota(jnp.int32, rl.shape, 1).astype(F32)
    is_g = (lane >= N_EXPERTS) & (lane < N_EXPERTS + N_GROUPS)
    gl = jnp.where(is_g, rl, NEG)
    gmax = jnp.max(gl, axis=-1, keepdims=True)
    grp = jnp.min(jnp.where(gl == gmax, lane - N_EXPERTS, 4.0 * N_EXPERTS), axis=-1, keepdims=True)
    p_grp = 1.0 / jnp.sum(jnp.where(is_g, jnp.exp(gl - gmax), 0.0), axis=-1, keepdims=True)
    lo = grp * EXPERTS_PER_GROUP
    in_grp = (lane >= lo) & (lane < lo + EXPERTS_PER_GROUP)
    el = jnp.where(in_grp, rl, NEG)
    m1 = jnp.max(el, axis=-1, keepdims=True)
    i1 = jnp.min(jnp.where(el == m1, lane, 4.0 * N_EXPERTS), axis=-1, keepdims=True)
    el2 = jnp.where(lane == i1, NEG, el)
    m2 = jnp.max(el2, axis=-1, keepdims=True)
    i2 = jnp.min(jnp.where(el2 == m2, lane, 4.0 * N_EXPERTS), axis=-1, keepdims=True)
    e2 = jnp.exp(m2 - m1)
    w1 = p_grp / (1.0 + e2)
    w2 = p_grp * e2 / (1.0 + e2)
    return i1, i2, w1, w2


def _merge_body(ya_ref, yb_ref, yc_ref, ga_ref, gb_ref, gc_ref, x_ref, mod_ref, wa_ref, wb_ref, wc_ref, wo_ref,
                lg_ref, lb_ref, wrh_ref, wrl_ref, br_ref, cnt0_ref, x1_ref, rinfo_ref, cnt_ref, *, bt, lt):
    tm = bt * lt

    @pl.when((pl.program_id(0) == 0) & (pl.program_id(1) == 0))
    def _():
        cnt_ref[...] = cnt0_ref[...]

    def r2(ref):
        return ref[...].reshape(tm, ref.shape[-1])

    merged = (_sigmoid(r2(ga_ref)) * _dot(r2(ya_ref), wa_ref[0])
              + _sigmoid(r2(gb_ref)) * _dot(r2(yb_ref), wb_ref[0])
              + _sigmoid(r2(gc_ref)) * _dot(r2(yc_ref), wc_ref[0]))
    out = _dot(merged.astype(BF16), wo_ref[0])
    y = DN_ALPHA * x_ref[...] + (1.0 + mod_ref[:, 2:3, :]) * out.reshape(bt, lt, D_MODEL)
    x1 = _layer_norm(y, lg_ref[0], lb_ref[0])
    x1_ref[...] = x1
    u2 = (x1 * (1.0 + mod_ref[:, 4:5, :]) + mod_ref[:, 3:4, :]).reshape(tm, D_MODEL)
    hi = u2.astype(BF16)
    lo = (u2 - hi.astype(F32)).astype(BF16)
    rl = _dot(hi, wrh_ref[0]) + _dot(lo, wrh_ref[0]) + _dot(hi, wrl_ref[0]) + br_ref[0]
    i1, i2, w1, w2 = _route(rl)
    lane = lax.broadcasted_iota(jnp.int32, (tm, 128), 1).astype(F32)
    onehot = jnp.where((lane == i1) | (lane == i2), 1.0, 0.0)
    rr = lax.broadcasted_iota(jnp.int32, (tm, tm), 0)
    cc = lax.broadcasted_iota(jnp.int32, (tm, tm), 1)
    before = _dot((rr > cc).astype(BF16), onehot.astype(BF16)) + cnt_ref[0:1, :]
    rank1 = jnp.sum(jnp.where(lane == i1, before, 0.0), axis=-1, keepdims=True)
    rank2 = jnp.sum(jnp.where(lane == i2, before, 0.0), axis=-1, keepdims=True)
    cnt_ref[0:1, :] += jnp.sum(onehot, axis=0, keepdims=True)
    rinfo = jnp.zeros((tm, 128), F32)
    for k, val in enumerate((i1, i2, w1, w2, rank1, rank2)):
        rinfo = jnp.where(lane == k, val, rinfo)
    rinfo_ref[...] = rinfo.reshape(bt, lt, 128)


def _merge(ya, yb, yc, proj, x, mod, wa, wb, wc, wo, lg, lb, wrh, wrl, br, cnt0, layer, bt, lt):
    nb, lp, _ = x.shape
    tok = lambda i, t: (i, t, 0)
    wsp = lambda shape: pl.BlockSpec((1,) + shape, lambda i, t: (layer, 0, 0))
    g0 = OFF_GTS // D_MODEL
    return pl.pallas_call(
        functools.partial(_merge_body, bt=bt, lt=lt),
        grid=(nb // bt, lp // lt),
        in_specs=[pl.BlockSpec((bt, lt, D_MIX), tok),
                  pl.BlockSpec((bt, lt, D_MIX), tok),
                  pl.BlockSpec((bt, lt, D_MIX), tok),
                  pl.BlockSpec((bt, lt, D_MODEL), lambda i, t: (i, t, g0)),
                  pl.BlockSpec((bt, lt, D_MODEL), lambda i, t: (i, t, g0 + 1)),
                  pl.BlockSpec((bt, lt, D_MODEL), lambda i, t: (i, t, g0 + 2)),
                  pl.BlockSpec((bt, lt, D_MODEL), tok),
                  pl.BlockSpec((bt, 6, D_MODEL), lambda i, t: (i, 0, 0)),
                  wsp((D_MIX, D_MODEL)), wsp((D_MIX, D_MODEL)), wsp((D_MIX, D_MODEL)), wsp((D_MODEL, D_MODEL)),
                  wsp((1, D_MODEL)), wsp((1, D_MODEL)),
                  wsp((D_MODEL, 128)), wsp((D_MODEL, 128)), wsp((1, 128)),
                  pl.BlockSpec((SUBLANES, LANES), lambda i, t: (0, 0))],
        out_specs=[pl.BlockSpec((bt, lt, D_MODEL), tok),
                   pl.BlockSpec((bt, lt, 128), tok),
                   pl.BlockSpec((SUBLANES, LANES), lambda i, t: (0, 0))],
        out_shape=[jax.ShapeDtypeStruct((nb, lp, D_MODEL), F32),
                   jax.ShapeDtypeStruct((nb, lp, 128), F32),
                   jax.ShapeDtypeStruct((SUBLANES, LANES), F32)],
        compiler_params=pltpu.CompilerParams(dimension_semantics=("arbitrary", "arbitrary"),
                                             vmem_limit_bytes=VMEM_LIMIT),
        name=f"merge_l{layer}_b{bt}",
    )(ya, yb, yc, proj, proj, proj, x, mod, wa, wb, wc, wo, lg, lb, wrh, wrl, br, cnt0)


def _moe_plan(cnt, cnt_first, n_tok, mb):
    n_blocks = 2 * n_tok // mb + N_EXPERTS
    counts = cnt[0, :N_EXPERTS].astype(jnp.int32)
    nblk = (counts + mb - 1) // mb
    pend = jnp.cumsum(nblk)
    block_e = jnp.sum(pend[None, :] <= jnp.arange(n_blocks, dtype=jnp.int32)[:, None], axis=1)
    block_e = jnp.minimum(block_e, N_EXPERTS - 1).astype(jnp.int32)
    n_used = pend[N_EXPERTS - 1:].astype(jnp.int32)
    first_row = (pend - nblk) * mb
    lo = jnp.minimum((first_row + cnt_first[0, :N_EXPERTS].astype(jnp.int32)) // mb, pend - 1)
    ztail = jnp.concatenate([jnp.where(nblk > 0, lo, 0), jnp.where(nblk > 0, pend - 1, -1), n_used])
    return first_row, block_e, n_used, ztail.astype(jnp.int32).reshape(1, 1, 2 * N_EXPERTS + 1), n_blocks


def _dispatch_rows(rinfo, first_row, tm):
    nb, lp, _ = rinfo.shape
    expert = rinfo[..., 0:2].astype(jnp.int32)
    rank = rinfo[..., 4:6].astype(jnp.int32)
    ids = jnp.arange(N_EXPERTS, dtype=jnp.int32)
    start = jnp.sum(jnp.where(expert[..., None] == ids, first_row, 0), axis=-1)
    return (start + rank).reshape(nb * lp // tm, 1, 2 * tm)


def _row_copy(src, dst, sem):
    return pltpu.make_async_copy(src, dst, sem)


def _dispatch_body(pos_ref, ztail_ref, x1_ref, mod_ref, *rest, bt, lt, n_blocks, mb, n_steps, first):
    disp_ref, u_scr, z_scr, sem, zsem = rest[-5:]
    tm = bt * lt

    @pl.when((pl.program_id(0) == 0) & (pl.program_id(1) == 0) & first)
    def _():
        z_scr[...] = jnp.zeros_like(z_scr)

        def zero_block(j):
            return _row_copy(z_scr, disp_ref.at[pl.ds(pl.multiple_of(j * mb, mb), mb)], zsem)

        def for_zeroed_blocks(fn):
            for e in range(N_EXPERTS):
                lax.fori_loop(ztail_ref[0, 0, e], ztail_ref[0, 0, N_EXPERTS + e] + 1, fn, 0)
            lax.fori_loop(ztail_ref[0, 0, 2 * N_EXPERTS], n_blocks, fn, 0)

        for_zeroed_blocks(lambda j, c: (zero_block(j).start(), c)[1])
        for_zeroed_blocks(lambda j, c: (zero_block(j).wait(), c)[1])

    step = pl.program_id(0) * pl.num_programs(1) + pl.program_id(1)
    slot = step % 2
    u2 = x1_ref[...] * (1.0 + mod_ref[:, 4:5, :]) + mod_ref[:, 3:4, :]
    u_scr[slot] = u2.reshape(tm // SUBLANES, SUBLANES, D_MODEL)

    def issue(i, carry):
        for r in range(SUBLANES):
            for k in range(TOP_K):
                row = pos_ref[0, 0, TOP_K * (SUBLANES * i + r) + k]
                _row_copy(u_scr.at[slot, i, pl.ds(r, 1)], disp_ref.at[pl.ds(row, 1)],
                          sem.at[slot]).start(priority=k)
        return carry

    lax.fori_loop(0, tm // SUBLANES, issue, 0)

    def drain(which):
        for k in range(TOP_K):
            _row_copy(disp_ref.at[pl.ds(0, tm)], disp_ref.at[pl.ds(0, tm)], sem.at[which]).wait()

    @pl.when(step > 0)
    def _():
        drain(1 - slot)

    @pl.when(step == n_steps - 1)
    def _():
        drain(slot)


def _dispatch(x1, mod, pos, ztail, n_blocks, mb, bt, lt, buf):
    nb, lp, _ = x1.shape
    nt = lp // lt
    extra_in, extra_args = ([], []) if buf is None else ([pl.BlockSpec(memory_space=pl.ANY)], [buf])
    return pl.pallas_call(
        functools.partial(_dispatch_body, bt=bt, lt=lt, n_blocks=n_blocks, mb=mb, n_steps=(nb // bt) * nt,
                          first=buf is None),
        grid=(nb // bt, nt),
        in_specs=[pl.BlockSpec((1, 1, 2 * bt * lt), lambda i, t: (i * nt + t, 0, 0), memory_space=pltpu.SMEM),
                  pl.BlockSpec((1, 1, 2 * N_EXPERTS + 1), lambda i, t: (0, 0, 0), memory_space=pltpu.SMEM),
                  pl.BlockSpec((bt, lt, D_MODEL), lambda i, t: (i, t, 0)),
                  pl.BlockSpec((bt, 6, D_MODEL), lambda i, t: (i, 0, 0))] + extra_in,
        out_specs=pl.BlockSpec(memory_space=pl.ANY),
        out_shape=jax.ShapeDtypeStruct((n_blocks * mb, D_MODEL), F32),
        scratch_shapes=[pltpu.VMEM((2, bt * lt // SUBLANES, SUBLANES, D_MODEL), F32), pltpu.VMEM((mb, D_MODEL), F32),
                        pltpu.SemaphoreType.DMA((2,)), pltpu.SemaphoreType.DMA],
        compiler_params=pltpu.CompilerParams(dimension_semantics=("arbitrary", "arbitrary"),
                                             vmem_limit_bytes=VMEM_LIMIT),
        input_output_aliases={4: 0} if extra_in else {},
        name=f"dispatch_b{bt}",
    )(pos, ztail, x1, mod, *extra_args)


def _experts_body(be_ref, nu_ref, x_ref, wg_ref, wu_ref, wd_ref, o_ref):
    j = pl.program_id(0)

    @pl.when(j < nu_ref[0])
    def _():
        x = x_ref[...].astype(BF16)
        hb = _silu(_dot(x, wg_ref[0, 0].astype(BF16))) * _dot(x, wu_ref[0, 0].astype(BF16))
        o_ref[...] = _dot(hb.astype(BF16), wd_ref[0, 0].astype(BF16))

    @pl.when(j >= nu_ref[0])
    def _():
        o_ref[...] = jnp.zeros_like(o_ref)


def _experts(disp, block_e, n_used, wg, wu, wd, layer, mb):
    n_blocks = disp.shape[0] // mb
    wmap = lambda j, be, nu: (layer, be[j], 0, 0)
    return pl.pallas_call(
        _experts_body,
        grid_spec=pltpu.PrefetchScalarGridSpec(
            num_scalar_prefetch=2,
            grid=(n_blocks,),
            in_specs=[pl.BlockSpec((mb, D_MODEL), lambda j, be, nu: (jnp.minimum(j, nu[0] - 1), 0)),
                      pl.BlockSpec((1, 1, D_MODEL, D_EXPERT), wmap),
                      pl.BlockSpec((1, 1, D_MODEL, D_EXPERT), wmap),
                      pl.BlockSpec((1, 1, D_EXPERT, D_MODEL), wmap)],
            out_specs=pl.BlockSpec((mb, D_MODEL), lambda j, be, nu: (j, 0))),
        out_shape=jax.ShapeDtypeStruct(disp.shape, F32),
        compiler_params=pltpu.CompilerParams(dimension_semantics=("arbitrary",), vmem_limit_bytes=VMEM_LIMIT),
        name=f"experts_l{layer}_n{n_blocks}",
    )(block_e, n_used, disp, wg, wu, wd)


def _combine_body(pos_ref, pos_next_ref, x1_ref, mod_ref, rinfo_ref, eo_ref, lg_ref, lb_ref, x2_ref, r_scr, sem,
                  *, bt, lt, n_steps):
    tm = bt * lt
    step = pl.program_id(0) * pl.num_programs(1) + pl.program_id(1)
    slot = step % 2

    def gather(p_ref, to_slot):
        def issue(i, carry):
            for r in range(SUBLANES):
                for k in range(TOP_K):
                    row = p_ref[0, 0, TOP_K * (SUBLANES * i + r) + k]
                    _row_copy(eo_ref.at[pl.ds(row, 1)], r_scr.at[to_slot, k, i, pl.ds(r, 1)],
                              sem.at[to_slot]).start(priority=k)
            return carry

        lax.fori_loop(0, tm // SUBLANES, issue, 0)

    @pl.when(step == 0)
    def _():
        gather(pos_ref, 0)

    @pl.when(step + 1 < n_steps)
    def _():
        gather(pos_next_ref, 1 - slot)

    for k in range(TOP_K):
        _row_copy(eo_ref.at[pl.ds(0, tm)], eo_ref.at[pl.ds(0, tm)], sem.at[slot]).wait()

    rinfo = rinfo_ref[...].reshape(tm, 128)
    rows = r_scr[slot]
    moe = (rows[0].reshape(tm, D_MODEL) * rinfo[:, 2:3] + rows[1].reshape(tm, D_MODEL) * rinfo[:, 3:4])
    y = DN_ALPHA * x1_ref[...] + (1.0 + mod_ref[:, 5:6, :]) * moe.reshape(bt, lt, D_MODEL)
    x2_ref[...] = _layer_norm(y, lg_ref[0], lb_ref[0])


def _combine(x1, mod, rinfo, eo, pos, lg, lb, layer, bt, lt):
    nb, lp, _ = x1.shape
    nt = lp // lt
    tok = lambda i, t: (i, t, 0)
    n_steps = (nb // bt) * nt
    return pl.pallas_call(
        functools.partial(_combine_body, bt=bt, lt=lt, n_steps=n_steps),
        grid=(nb // bt, nt),
        in_specs=[pl.BlockSpec((1, 1, 2 * bt * lt), lambda i, t: (i * nt + t, 0, 0), memory_space=pltpu.SMEM),
                  pl.BlockSpec((1, 1, 2 * bt * lt), lambda i, t: (jnp.minimum(i * nt + t + 1, n_steps - 1), 0, 0),
                               memory_space=pltpu.SMEM),
                  pl.BlockSpec((bt, lt, D_MODEL), tok),
                  pl.BlockSpec((bt, 6, D_MODEL), lambda i, t: (i, 0, 0)),
                  pl.BlockSpec((bt, lt, 128), tok),
                  pl.BlockSpec(memory_space=pl.ANY),
                  pl.BlockSpec((1, 1, D_MODEL), lambda i, t: (layer, 0, 0)),
                  pl.BlockSpec((1, 1, D_MODEL), lambda i, t: (layer, 0, 0))],
        out_specs=pl.BlockSpec((bt, lt, D_MODEL), tok),
        out_shape=jax.ShapeDtypeStruct((nb, lp, D_MODEL), F32),
        scratch_shapes=[pltpu.VMEM((2, TOP_K, bt * lt // SUBLANES, SUBLANES, D_MODEL), F32),
                        pltpu.SemaphoreType.DMA((2,))],
        compiler_params=pltpu.CompilerParams(dimension_semantics=("arbitrary", "arbitrary"),
                                             vmem_limit_bytes=VMEM_LIMIT),
        name=f"combine_l{layer}_b{bt}",
    )(pos, pos, x1, mod, rinfo, eo, lg, lb)


def _hist(state):
    return jnp.pad(state, ((0, 0), (HIST - state.shape[1], 0), (0, 0)))


def _trunks(groups, p, mb):
    for g in groups:
        g.update(C=None, S=None, new={key: [] for key in ('n', 'm', 'conv', 'gconv')})
    n_tok = sum(g['x'].shape[0] * g['x'].shape[1] for g in groups)
    for l in range(DEPTH):
        cnt = jnp.zeros((SUBLANES, LANES), F32)
        for g in groups:
            st, mix, lv, nb = g['st'], g['mix'], g['lv'], g['x'].shape[0]
            proj, x = _inproj(g['x'], g['mod'][l], p['ln_in_g'], p['ln_in_b'], p['w_in_r'], p['b_in_r'], l, l == 0,
                              *g['ip_tile'])
            m0 = jnp.broadcast_to(st['m'][l][:, :, None, None], (nb, N_HEADS, 1, 128))
            yb, conv, *pre = _mlstm_prep(proj, _hist(st['conv'][l]), p['conv_b_w8'][l],
                                         mix['NB'], mix['G'], mix['tl'], lv)
            ya, g['C'], n, m = _mlstm_scan(pre, proj, st['C'], st['n'][l], m0, p['mlstm_norm_g'][l:l + 1],
                                           mix['SB'], mix['G'], mix['tl'], l, g['C'])
            *pre, gconv = _gdn_prep(proj, _hist(st['gconv'][l]), p['conv_c_w8'][l], p['alog_row'][l],
                                    p['dtb_row'][l], mix['NB'], mix['G'], mix['tl'], lv)
            yc, g['S'] = _gdn_scan(pre, proj, st['S'], p['gdn_norm_g'][l:l + 1], mix['SB'], mix['G'], mix['tl'],
                                   l, g['S'])
            g['x1'], g['rinfo'], cnt = _merge(ya, yb, yc, proj, x, g['mod'][l], p['w_br_a'], p['w_br_b'], p['w_br_c'],
                                              p['w_out'], p['ln1_g'], p['ln1_b'], p['wr_hi'], p['wr_lo'], p['br'],
                                              cnt, l, g['bt'], g['lt'])
            g['cnt'] = cnt
            for key, val in (('n', n), ('m', m[:, :, 0, 0]), ('conv', conv), ('gconv', gconv)):
                g['new'][key].append(val)
        first_row, block_e, n_used, ztail, n_blocks = _moe_plan(cnt, groups[0]['cnt'], n_tok, mb)
        disp = None
        for g in groups:
            g['pos'] = _dispatch_rows(g['rinfo'], first_row, g['bt'] * g['lt'])
            disp = _dispatch(g['x1'], g['mod'][l], g['pos'], ztail, n_blocks, mb, g['bt'], g['lt'], disp)
        eo = _experts(disp, block_e, n_used, p['exp_w_gate'], p['exp_w_up'], p['exp_w_down'], l, mb)
        for g in groups:
            g['x'] = _combine(g['x1'], g['mod'][l], g['rinfo'], eo, g['pos'], p['ln2_g'], p['ln2_b'], l,
                              g['bt'], g['lt'])
    return [(g['x'], dict({key: jnp.stack(val) for key, val in g['new'].items()}, C=g['C'], S=g['S']))
            for g in groups]


def kernel(x_prompt, x_sample, state_mlstm_C, state_mlstm_n, state_mlstm_m, state_conv, state_gdn_S, state_gdn_conv, c_prompt, c_sample, ln_in_g, ln_in_b, w_ada, b_ada, w_in, b_in, mlstm_norm_g, conv_b_w, conv_c_w, gdn_a_log, gdn_dt_bias, gdn_norm_g, w_br_a, w_br_b, w_br_c, w_out, ln1_g, ln1_b, router_g_w, router_g_b, router_e_w, router_e_b, exp_w_gate, exp_w_up, exp_w_down, ln2_g, ln2_b):
    nbp, lp, _ = x_prompt.shape
    nbs, ls, _ = x_sample.shape
    lsp = 8

    def regroup(a):
        out = jnp.zeros(a.shape[:-1] + (N_PROJ,), a.dtype)
        for src, end, dst in PROJ_SEGMENTS:
            out = out.at[..., dst:dst + end - src].set(a[..., src:end])
        return out

    wr = jnp.concatenate([router_e_w, router_g_w, jnp.zeros((DEPTH, D_MODEL, 128 - N_EXPERTS - N_GROUPS), F32)], axis=-1)
    wr_hi = wr.astype(BF16)
    lane_pad = lambda a: jnp.pad(a, ((0, 0), (4, 128 - 4 - N_HEADS)))[:, None, :]
    p = dict(
        ln_in_g=ln_in_g.reshape(1, D_MODEL), ln_in_b=ln_in_b.reshape(1, D_MODEL),
        w_in_r=_regroup_w_in(w_in), b_in_r=regroup(b_in).reshape(DEPTH, 1, N_PROJ),
        mlstm_norm_g=mlstm_norm_g,
        conv_b_w8=jnp.pad(conv_b_w, ((0, 0), (0, 8 - CONV_B), (0, 0))),
        conv_c_w8=jnp.pad(conv_c_w, ((0, 0), (0, 8 - CONV_C), (0, 0))),
        alog_row=lane_pad(gdn_a_log), dtb_row=lane_pad(gdn_dt_bias), gdn_norm_g=gdn_norm_g,
        w_br_a=w_br_a.astype(BF16), w_br_b=w_br_b.astype(BF16), w_br_c=w_br_c.astype(BF16),
        w_out=w_out.astype(BF16),
        ln1_g=ln1_g.reshape(DEPTH, 1, D_MODEL), ln1_b=ln1_b.reshape(DEPTH, 1, D_MODEL),
        wr_hi=wr_hi, wr_lo=(wr - wr_hi.astype(F32)).astype(BF16),
        br=jnp.concatenate([router_e_b, router_g_b, jnp.zeros((DEPTH, 128 - N_EXPERTS - N_GROUPS), F32)],
                           axis=-1).reshape(DEPTH, 1, 128),
        exp_w_gate=exp_w_gate, exp_w_up=exp_w_up, exp_w_down=exp_w_down,
        ln2_g=ln2_g.reshape(DEPTH, 1, D_MODEL), ln2_b=ln2_b.reshape(DEPTH, 1, D_MODEL),
    )

    mod = _ada(jnp.concatenate([c_prompt, c_sample], axis=0), w_ada, b_ada)
    mod = mod.reshape(DEPTH, nbp + nbs, 6, D_MODEL)

    zeros = lambda *s: jnp.zeros((DEPTH, nbp) + s, F32)
    st_p = {'C': zeros(N_HEADS, DH, DH), 'n': zeros(N_HEADS, DH), 'm': zeros(N_HEADS),
            'conv': zeros(CONV_B - 1, D_MIX), 'S': zeros(N_HEADS, DH, DH), 'gconv': zeros(CONV_C - 1, 3 * D_MIX)}
    prompt = dict(x=x_prompt, mod=mod[:, :nbp], st=st_p, bt=1, lt=512, lv=lp,
                  mix=dict(NB=8, G=1, tl=RB, SB=nbp), ip_tile=(1, 256))

    st_s = {'C': state_mlstm_C, 'n': state_mlstm_n, 'm': state_mlstm_m, 'conv': state_conv,
            'S': state_gdn_S, 'gconv': state_gdn_conv}
    xs = jnp.pad(x_sample, ((0, 0), (0, lsp - ls), (0, 0)))
    sample = dict(x=xs, mod=mod[:, nbp:], st=st_s, bt=64, lt=lsp, lv=ls,
                  mix=dict(NB=2, G=RB // lsp, tl=lsp, SB=1), ip_tile=(32, lsp))
    (y_p, sp), (y_s, ss) = _trunks([prompt, sample], p, mb=512)
    y_s = y_s[:, :ls]

    return (y_p, y_s, sp['C'], sp['n'], sp['m'], sp['conv'], sp['S'], sp['gconv'],
            ss['C'], ss['n'], ss['m'], ss['conv'], ss['S'], ss['gconv'])
```

```python
import functools

import jax
import jax.numpy as jnp
from jax import lax
from jax.experimental import pallas as pl
from jax.experimental.pallas import tpu as pltpu

F32 = jnp.float32
BF16 = jnp.bfloat16

LANES = 128
SUBLANES = 8

D_MODEL = 1024
DEPTH = 2
N_HEADS = 4
DH = 128
D_MIX = N_HEADS * DH
N_EXPERTS = 32
EXPERTS_PER_GROUP = 8
N_GROUPS = 4
TOP_K = 2
D_EXPERT = 256
CONV_B = 3
CONV_C = 4
HIST = SUBLANES
RB = 64
INV_BLOCK = 8
DN_ALPHA = (2 * DEPTH) ** 0.25
LN_EPS = 1e-5
NORM_EPS = 1e-6
NEG = -1e30

OFF_QKVC = 0
OFF_BCH = 1536
OFF_GTS = 3072
OFF_QKVO = 6144
OFF_Z = 8192
OFF_SA = 8704
OFF_SB = 8832
N_PROJ = 8960
TN_PROJ = 1280


def _proj_segments():
    widths = (('qkvo', 4 * D_MIX), ('i', N_HEADS), ('f', N_HEADS), ('bch', 3 * D_MIX), ('qkvc', 3 * D_MIX),
              ('z', D_MIX), ('beta', N_HEADS), ('a', N_HEADS), ('gts', 3 * D_MODEL))
    dst = dict(qkvc=OFF_QKVC, bch=OFF_BCH, gts=OFF_GTS, qkvo=OFF_QKVO, z=OFF_Z,
               i=OFF_SA, beta=OFF_SA + N_HEADS, f=OFF_SB, a=OFF_SB + N_HEADS)
    segs, col = [], 0
    for name, width in widths:
        segs.append((col, col + width, dst[name]))
        col += width
    return tuple(segs)


PROJ_SEGMENTS = _proj_segments()

VMEM_LIMIT = 52 * 1024 * 1024


def _dot(a, b):
    return jnp.dot(a, b, preferred_element_type=F32)


def _split3(x):
    hi = x.astype(BF16)
    r = x - hi.astype(F32)
    mid = r.astype(BF16)
    lo = (r - mid.astype(F32)).astype(BF16)
    return hi, mid, lo


def _layer_norm(x, g, b):
    mu = jnp.mean(x, axis=-1, keepdims=True)
    xc = x - mu
    var = jnp.mean(xc * xc, axis=-1, keepdims=True)
    return xc * lax.rsqrt(var + LN_EPS) * g + b


def _sigmoid(x):
    return jax.nn.sigmoid(x)


def _silu(x):
    return x * _sigmoid(x)


def _log_sigmoid(x):
    return jnp.minimum(x, 0.0) - jnp.log1p(jnp.exp(-jnp.abs(x)))


def _softplus(x):
    return jnp.maximum(x, 0.0) + jnp.log1p(jnp.exp(-jnp.abs(x)))


def _ada_body(c_ref, w_ref, b_ref, o_ref):
    c = c_ref[...]
    s = _silu(c).astype(BF16)
    o_ref[0] = _dot(s, w_ref[0].astype(BF16)) + b_ref[0]


def _ada(c_all, w_ada, b_ada):
    nb = c_all.shape[0]
    return pl.pallas_call(
        _ada_body,
        grid=(DEPTH, 6),
        in_specs=[pl.BlockSpec((nb, D_MODEL), lambda l, j: (0, 0)),
                  pl.BlockSpec((1, D_MODEL, D_MODEL), lambda l, j: (l, 0, j)),
                  pl.BlockSpec((1, 1, D_MODEL), lambda l, j: (l, 0, j))],
        out_specs=pl.BlockSpec((1, nb, D_MODEL), lambda l, j: (l, 0, j)),
        out_shape=jax.ShapeDtypeStruct((DEPTH, nb, 6 * D_MODEL), F32),
        compiler_params=pltpu.CompilerParams(dimension_semantics=("arbitrary", "arbitrary"),
                                             vmem_limit_bytes=VMEM_LIMIT),
        name="ada",
    )(c_all, w_ada, b_ada.reshape(DEPTH, 1, 6 * D_MODEL))


def _regroup_body(w_ref, o_ref):
    o_ref[...] = jnp.zeros_like(o_ref)
    for src, end, dst in PROJ_SEGMENTS:
        o_ref[0, :, dst:dst + end - src] = w_ref[0, :, src:end].astype(BF16)


def _regroup_w_in(w_in):
    n_in = w_in.shape[-1]
    rows = 256
    return pl.pallas_call(
        _regroup_body,
        grid=(DEPTH, D_MODEL // rows),
        in_specs=[pl.BlockSpec((1, rows, n_in), lambda l, i: (l, i, 0))],
        out_specs=pl.BlockSpec((1, rows, N_PROJ), lambda l, i: (l, i, 0)),
        out_shape=jax.ShapeDtypeStruct((DEPTH, D_MODEL, N_PROJ), BF16),
        compiler_params=pltpu.CompilerParams(dimension_semantics=("arbitrary", "arbitrary"),
                                             vmem_limit_bytes=VMEM_LIMIT),
        name="regroup_w_in",
    )(w_in)


def _inproj_body(x_ref, mod_ref, g_ref, b_ref, w_ref, bias_ref, proj_ref, *rest, apply_ln, bt, lt):
    x = x_ref[...]
    if apply_ln:
        x = _layer_norm(x, g_ref[...], b_ref[...])
        rest[0][...] = x
    u = (x * (1.0 + mod_ref[:, 1:2, :]) + mod_ref[:, 0:1, :]).reshape(bt * lt, D_MODEL).astype(BF16)
    for j in range(N_PROJ // TN_PROJ):
        cs = slice(j * TN_PROJ, (j + 1) * TN_PROJ)
        proj_ref[:, :, cs] = (_dot(u, w_ref[0, :, cs]) + bias_ref[0, :, cs]).reshape(bt, lt, TN_PROJ)


def _inproj(x, mod, ln_g, ln_b, w_r, b_r, layer, apply_ln, bt, lt):
    nb, lp, _ = x.shape
    tok = lambda i, t: (i, t, 0)
    out_shape = [jax.ShapeDtypeStruct((nb, lp, N_PROJ), F32)]
    out_specs = [pl.BlockSpec((bt, lt, N_PROJ), tok)]
    if apply_ln:
        out_shape.append(jax.ShapeDtypeStruct((nb, lp, D_MODEL), F32))
        out_specs.append(pl.BlockSpec((bt, lt, D_MODEL), tok))
    res = pl.pallas_call(
        functools.partial(_inproj_body, apply_ln=apply_ln, bt=bt, lt=lt),
        grid=(nb // bt, lp // lt),
        in_specs=[pl.BlockSpec((bt, lt, D_MODEL), tok),
                  pl.BlockSpec((bt, 6, D_MODEL), lambda i, t: (i, 0, 0)),
                  pl.BlockSpec((1, D_MODEL), lambda i, t: (0, 0)),
                  pl.BlockSpec((1, D_MODEL), lambda i, t: (0, 0)),
                  pl.BlockSpec((1, D_MODEL, N_PROJ), lambda i, t: (layer, 0, 0), pipeline_mode=pl.Buffered(1)),
                  pl.BlockSpec((1, 1, N_PROJ), lambda i, t: (layer, 0, 0))],
        out_specs=out_specs,
        out_shape=out_shape,
        compiler_params=pltpu.CompilerParams(dimension_semantics=("arbitrary", "arbitrary"),
                                             vmem_limit_bytes=VMEM_LIMIT),
        name=f"inproj_l{layer}_b{bt}",
    )(x, mod, ln_g, ln_b, w_r, b_r)
    return (res[0], res[1]) if apply_ln else (res[0], x)


def _conv_taps(xp_s, w_ref, width, tl):
    acc = None
    for j in range(width):
        tap = xp_s[:, pl.ds(HIST - (width - 1) + j, tl), :] * w_ref[j:j + 1, :].reshape(1, 1, -1)
        acc = tap if acc is None else acc + tap
    return acc


def _conv_history(xp_s, hist_ref, prev_ref, n_steps):
    if n_steps > 1:
        @pl.when(pl.program_id(1) == 0)
        def _():
            xp_s[:, 0:HIST, :] = hist_ref[...]

        @pl.when(pl.program_id(1) > 0)
        def _():
            xp_s[:, 0:HIST, :] = prev_ref[...]
    else:
        xp_s[:, 0:HIST, :] = hist_ref[...]


def _heads(x, nb, width):
    return jnp.stack([x[:, :, h * width:(h + 1) * width] for h in range(N_HEADS)],
                     axis=1).reshape(nb * N_HEADS, RB, width)


def _gate_cols(x, nb, lane0):
    return jnp.stack([x[:, :, lane0 + h:lane0 + h + 1] for h in range(N_HEADS)],
                     axis=1).reshape(nb * N_HEADS, RB, 1)


def _gate_rows(x, nb, lane0):
    xt = jnp.swapaxes(x, 1, 2)
    return jnp.stack([xt[:, lane0 + h:lane0 + h + 1, :] for h in range(N_HEADS)],
                     axis=1).reshape(nb * N_HEADS, 1, RB)


def _bmm(a, b):
    return jnp.einsum('nts,nsu->ntu', a.astype(BF16), b.astype(BF16), preferred_element_type=F32)


def _bmm_nt(a, b):
    return jnp.einsum('ntd,nsd->nts', a, b, preferred_element_type=F32)


def _block_masks(tl):
    rr = lax.broadcasted_iota(jnp.int32, (RB, RB), 0)
    cc = lax.broadcasted_iota(jnp.int32, (RB, RB), 1)
    incl = rr >= cc
    if tl < RB:
        incl = incl & ((rr // tl) == (cc // tl))
    return rr, cc, incl


def _seq_cumsum(x, incl, nb):
    tril = jnp.broadcast_to(incl.astype(BF16)[None], (nb, RB, RB))
    hi, mid, lo = _split3(x)
    return _bmm(tril, hi) + _bmm(tril, mid) + _bmm(tril, lo)


def _seq_last(x, nb, G, tl):
    return x.reshape(nb * G, tl, 128)[:, tl - 1:tl, :]


def _seq_rows(x3, nb, tl):
    return jnp.broadcast_to(x3, (x3.shape[0], tl, 128)).reshape(nb, RB, 128)


def _put_chains(ref, val, nb, G, tl):
    val4 = val.reshape(nb, N_HEADS, RB, val.shape[-1])
    for h in range(N_HEADS):
        ref[:, :, h] = val4[:, h].reshape(nb, G, tl, val.shape[-1])


def _put_seq_scalars(ref, x3, nb, G, lane0):
    x4 = x3.reshape(nb, G, 1, 128)
    for h in range(N_HEADS):
        ref[:, :, h] = jnp.broadcast_to(x4[:, :, :, lane0 + h:lane0 + h + 1], (nb, G, 1, 128))


def _step_tiling(nb, lp, NB, G, tl):
    assert G * tl == RB
    if G == 1:
        gx, tlx = 1, NB * RB
    else:
        assert lp == tl
        gx, tlx = NB * G, tl
    return gx, tlx, nb // gx, lp // tlx


def _layered_state(layer, nb, NS, prev):
    st_in = pl.BlockSpec((None, NS, N_HEADS, DH, DH), lambda i, c: (layer, i, 0, 0, 0))
    shape = jax.ShapeDtypeStruct((DEPTH, nb, N_HEADS, DH, DH), F32)
    if layer == 0:
        return st_in, pl.BlockSpec((DEPTH, NS, N_HEADS, DH, DH), lambda i, c: (0, i, 0, 0, 0)), shape, [], []
    return st_in, st_in, shape, [pl.BlockSpec(memory_space=pl.ANY)], [prev]


def _fill_later_layers(ref, when):
    @pl.when(when)
    def _():
        for l in range(1, DEPTH):
            ref[l] = ref[0]


def _mlstm_prep_body(qkvo_ref, bch_ref, prev_ref, sa_ref, sb_ref, cv0_ref, cw_ref,
                     yb_ref, cv_ref, nv_ref, q_ref, kw_ref, v_ref, rows_ref, bl_ref, bc_ref, kn_ref, xp_s,
                     *, NB, G, tl, tlx, NCS, lv):
    cs = pl.program_id(1)
    n = NB * N_HEADS
    lvl = lv - (NCS - 1) * tlx

    if NCS > 1:
        @pl.when(cs == 0)
        def _():
            xp_s[:, 0:HIST, :] = cv0_ref[...]

        @pl.when(cs > 0)
        def _():
            xp_s[:, 0:HIST, :] = prev_ref[:, :, D_MIX:2 * D_MIX] * prev_ref[:, :, 2 * D_MIX:3 * D_MIX]
    else:
        xp_s[:, 0:HIST, :] = cv0_ref[...]
    xp_s[:, HIST:HIST + tlx, :] = bch_ref[:, :, D_MIX:2 * D_MIX] * bch_ref[:, :, 2 * D_MIX:3 * D_MIX]
    yb_ref[...] = (bch_ref[:, :, 0:D_MIX] * _conv_taps(xp_s, cw_ref, CONV_B, tlx)).astype(BF16)

    @pl.when(cs == NCS - 1)
    def _():
        cv_ref[...] = xp_s[:, pl.ds(HIST + lvl - (CONV_B - 1), CONV_B - 1), :]

    i_all = sa_ref[...].reshape(NB, RB, 128)
    f_all = _log_sigmoid(sb_ref[...].reshape(NB, RB, 128))
    if lv < NCS * tlx:
        assert NCS == 1
        valid = (lax.broadcasted_iota(jnp.int32, (NB, RB, 128), 1) % tl) < lv
        i_all = jnp.where(valid, i_all, NEG)
        f_all = jnp.where(valid, f_all, 0.0)
    _, _, incl = _block_masks(tl)
    bcum = _seq_cumsum(f_all, incl, NB)
    blast = _seq_last(bcum, NB, G, tl)
    val = _seq_rows(blast, NB, tl) - bcum + i_all
    bmax = jnp.max(val.reshape(NB * G, tl, 128), axis=1, keepdims=True)
    wk0 = jnp.exp(val - _seq_rows(bmax, NB, tl))

    qkvo = qkvo_ref[...].reshape(NB, RB, 4 * D_MIX)
    q = _heads(qkvo[:, :, 0:D_MIX], NB, DH)
    k = _heads(qkvo[:, :, D_MIX:2 * D_MIX], NB, DH) * (DH ** -0.5)
    v = _heads(qkvo[:, :, 2 * D_MIX:3 * D_MIX], NB, DH)
    qb, kb, vb = q.astype(BF16), k.astype(BF16), v.astype(BF16)

    b_col = _gate_cols(bcum, NB, 0)
    dlog = jnp.where(incl[None], b_col - _gate_rows(bcum, NB, 0) + _gate_rows(i_all, NB, 0), NEG)
    d = jnp.max(dlog, axis=-1, keepdims=True)
    s0 = _bmm_nt(qb, kb) * jnp.exp(dlog - d)
    kw0 = k * _gate_cols(wk0, NB, 0)

    _put_chains(nv_ref, _bmm(s0, vb), NB, G, tl)
    _put_chains(q_ref, qb, NB, G, tl)
    _put_chains(kw_ref, kw0.astype(BF16), NB, G, tl)
    _put_chains(v_ref, vb, NB, G, tl)
    rows = jnp.swapaxes(jnp.concatenate([d, b_col, jnp.sum(s0, axis=-1, keepdims=True),
                                         jnp.zeros((n, RB, 5), F32)], axis=-1), 1, 2).reshape(NB, N_HEADS, 8, RB)
    for h in range(N_HEADS):
        for g in range(G):
            rows_ref[:, g, h] = rows[:, h, :, g * tl:(g + 1) * tl]
    _put_seq_scalars(bl_ref, blast, NB, G, 0)
    _put_seq_scalars(bc_ref, bmax, NB, G, 0)
    kn = jnp.sum(kw0.reshape(n * G, tl, DH), axis=1, keepdims=True).reshape(NB, N_HEADS, G, 1, DH)
    for h in range(N_HEADS):
        kn_ref[:, :, h] = kn[:, h]


def _mlstm_prep(proj, cv0, cw, NB, G, tl, lv):
    nb, lp, _ = proj.shape
    gx, tlx, NI, NCS = _step_tiling(nb, lp, NB, G, tl)
    nbt = nb * lp // RB
    step = lambda i, c: (i * NCS + c, 0, 0, 0, 0)
    chain = lambda last, dt: jax.ShapeDtypeStruct((nbt, G, N_HEADS, tl, last), dt)
    cspec = lambda last: pl.BlockSpec((NB, G, N_HEADS, tl, last), step)
    scal = jax.ShapeDtypeStruct((nbt, G, N_HEADS, 1, 128), F32)
    sspec = pl.BlockSpec((NB, G, N_HEADS, 1, 128), step)
    bch = OFF_BCH // (3 * D_MIX)
    return pl.pallas_call(
        functools.partial(_mlstm_prep_body, NB=NB, G=G, tl=tl, tlx=tlx, NCS=NCS, lv=lv),
        grid=(NI, NCS),
        in_specs=[pl.BlockSpec((gx, tlx, 4 * D_MIX), lambda i, c: (i, c, OFF_QKVO // (4 * D_MIX))),
                  pl.BlockSpec((gx, tlx, 3 * D_MIX), lambda i, c: (i, c, bch)),
                  pl.BlockSpec((gx, HIST, 3 * D_MIX), lambda i, c: (i, jnp.maximum(c * (tlx // HIST) - 1, 0), bch)),
                  pl.BlockSpec((gx, tlx, 128), lambda i, c: (i, c, OFF_SA // 128)),
                  pl.BlockSpec((gx, tlx, 128), lambda i, c: (i, c, OFF_SB // 128)),
                  pl.BlockSpec((gx, HIST, D_MIX), lambda i, c: (i, 0, 0)),
                  pl.BlockSpec((8, D_MIX), lambda i, c: (0, 0))],
        out_specs=[pl.BlockSpec((gx, tlx, D_MIX), lambda i, c: (i, c, 0)),
                   pl.BlockSpec((gx, CONV_B - 1, D_MIX), lambda i, c: (i, 0, 0)),
                   cspec(DH), cspec(DH), cspec(DH), cspec(DH),
                   pl.BlockSpec((NB, G, N_HEADS, 8, tl), step), sspec, sspec, sspec],
        out_shape=[jax.ShapeDtypeStruct((nb, lp, D_MIX), BF16),
                   jax.ShapeDtypeStruct((nb, CONV_B - 1, D_MIX), F32),
                   chain(DH, F32), chain(DH, BF16), chain(DH, BF16), chain(DH, BF16),
                   jax.ShapeDtypeStruct((nbt, G, N_HEADS, 8, tl), F32),
                   scal, scal, scal],
        scratch_shapes=[pltpu.VMEM((gx, HIST + tlx, D_MIX), F32)],
        compiler_params=pltpu.CompilerParams(dimension_semantics=("arbitrary", "arbitrary"),
                                             vmem_limit_bytes=VMEM_LIMIT),
        name=f"mlstm_prep_g{G}",
    )(proj, proj, proj, proj, proj, cv0, cw)


def _mlstm_scan_body(nv_ref, q_ref, kw_ref, v_ref, rows_ref, bl_ref, bc_ref, kn_ref, o_ref, C0_ref, n0_ref, m0_ref,
                     ng_ref, *rest, NS, tl, NC, first):
    ya_ref, C_out, n_ref, m_ref = rest[-4:]
    C_ref = C_out.at[0] if first else C_out
    c = pl.program_id(1)
    n = NS * N_HEADS

    @pl.when(c == 0)
    def _():
        C_ref[...] = C0_ref[...]
        n_ref[...] = n0_ref[...]
        m_ref[...] = m0_ref[...]

    chains = lambda ref: ref[...].reshape(n, ref.shape[-2], ref.shape[-1])
    C = C_ref[...].reshape(n, DH, DH)
    nvec = n_ref[...].reshape(n, 1, DH)
    m_prev = chains(m_ref)[:, :, 0:1]
    rows = chains(rows_ref)
    d, b, ds0 = rows[:, 0:1, :], rows[:, 1:2, :], rows[:, 2:3, :]
    qb = chains(q_ref)

    m_inter = b + m_prev
    m_t = jnp.maximum(m_inter, d)
    f = jnp.exp(d - m_t)
    inter = jnp.exp(m_inter - m_t)
    qn = jnp.einsum('nod,ntd->not', nvec.astype(BF16), qb, preferred_element_type=F32)
    den = f * ds0 + inter * qn
    scale_t = 1.0 / jnp.maximum(jnp.abs(den), jnp.exp(-m_t))
    fi = jnp.swapaxes(jnp.concatenate([f * scale_t, inter * scale_t, jnp.zeros((n, 6, tl), F32)], axis=1), 1, 2)
    qC = jnp.einsum('ntd,nde->nte', qb, C.astype(BF16), preferred_element_type=F32)
    hh = fi[:, :, 0:1] * chains(nv_ref) + fi[:, :, 1:2] * qC

    m_new = m_t[:, :, tl - 1:tl]
    decay = jnp.exp(chains(bl_ref)[:, :, 0:1] + m_prev - m_new)
    scale = jnp.exp(chains(bc_ref)[:, :, 0:1] - m_new)
    kv = jnp.einsum('ntd,nte->nde', chains(kw_ref), chains(v_ref), preferred_element_type=F32)
    C_ref[...] = (decay * C + scale * kv).reshape(NS, N_HEADS, DH, DH)
    n_ref[...] = (decay * nvec + scale * chains(kn_ref)).reshape(NS, N_HEADS, DH)
    m_ref[...] = jnp.broadcast_to(m_new, (n, 1, 128)).reshape(NS, N_HEADS, 1, 128)
    if first:
        _fill_later_layers(C_out, c == NC - 1)

    mu = jnp.mean(hh, axis=-1, keepdims=True)
    hc = hh - mu
    hn = (hc * lax.rsqrt(jnp.mean(hc * hc, axis=-1, keepdims=True) + LN_EPS)).reshape(NS, N_HEADS, tl, DH)
    for h in range(N_HEADS):
        hs = slice(h * DH, (h + 1) * DH)
        ya_ref[:, :, hs] = (_sigmoid(o_ref[:, :, hs]) * hn[:, h] * ng_ref[:, hs]).astype(BF16)


def _mlstm_scan(pre, proj, C0, n0, m0, ng, SB, G, tl, layer, C_prev):
    nb, lp, _ = proj.shape
    NS = SB * G
    NI, NC = nb // NS, lp // tl
    six = lambda a: a.reshape((NI * SB, NC) + a.shape[1:])
    spec6 = lambda a: pl.BlockSpec((SB, 1) + a.shape[1:], lambda i, c: (i, c, 0, 0, 0, 0))
    seq4 = lambda i, c: (i, 0, 0, 0)
    st_in, st_out, st_shape, extra_in, extra_args = _layered_state(layer, nb, NS, C_prev)
    n_in = len(pre) + 5
    return pl.pallas_call(
        functools.partial(_mlstm_scan_body, NS=NS, tl=tl, NC=NC, first=layer == 0),
        grid=(NI, NC),
        in_specs=[spec6(a) for a in pre] + [
            pl.BlockSpec((NS, tl, D_MIX), lambda i, c: (i, c, (OFF_QKVO + 3 * D_MIX) // D_MIX)),
            st_in,
            pl.BlockSpec((NS, N_HEADS, DH), lambda i, c: (i, 0, 0)),
            pl.BlockSpec((NS, N_HEADS, 1, 128), seq4),
            pl.BlockSpec((1, D_MIX), lambda i, c: (0, 0))] + extra_in,
        out_specs=[pl.BlockSpec((NS, tl, D_MIX), lambda i, c: (i, c, 0)),
                   st_out,
                   pl.BlockSpec((NS, N_HEADS, DH), lambda i, c: (i, 0, 0)),
                   pl.BlockSpec((NS, N_HEADS, 1, 128), seq4)],
        out_shape=[jax.ShapeDtypeStruct((nb, lp, D_MIX), BF16),
                   st_shape,
                   jax.ShapeDtypeStruct((nb, N_HEADS, DH), F32),
                   jax.ShapeDtypeStruct((nb, N_HEADS, 1, 128), F32)],
        input_output_aliases={n_in: 1} if extra_in else {},
        compiler_params=pltpu.CompilerParams(dimension_semantics=("arbitrary", "arbitrary"),
                                             vmem_limit_bytes=VMEM_LIMIT),
        name=f"mlstm_scan_g{G}",
    )(*[six(a) for a in pre], proj, C0, n0, m0, ng, *extra_args)


def _unit_lower_inverse(n, rr, cc, tl):
    eye = (rr == cc).astype(F32)[None]
    p = jnp.where(((rr // INV_BLOCK) == (cc // INV_BLOCK))[None], n, 0.0)
    x = eye + p
    b = 2
    while b < INV_BLOCK:
        p = _bmm(p, p)
        x = x + _bmm(x, p)
        b *= 2
    b = INV_BLOCK
    while b < tl:
        off = jnp.where((((rr // (2 * b)) == (cc // (2 * b))) & ((rr // b) != (cc // b)))[None], n, 0.0)
        x = x + _bmm(x, _bmm(off, x))
        b *= 2
    return x


def _gdn_prep_body(x_ref, prev_ref, sa_ref, sb_ref, gc0_ref, cw_ref, alog_ref, dtb_ref,
                   u_ref, w_ref, qg_ref, kd_ref, qkm_ref, eg_ref, gcs_ref, xp_s, *, NB, G, tl, tlx, NCS, lv):
    cs = pl.program_id(1)
    lvl = lv - (NCS - 1) * tlx

    _conv_history(xp_s, gc0_ref, prev_ref, NCS)
    xp_s[:, HIST:HIST + tlx, :] = x_ref[...]
    qkv = _silu(_conv_taps(xp_s, cw_ref, CONV_C, tlx)).reshape(NB, RB, 3 * D_MIX)

    @pl.when(cs == NCS - 1)
    def _():
        gcs_ref[...] = xp_s[:, pl.ds(HIST + lvl - (CONV_C - 1), CONV_C - 1), :]

    beta_all = _sigmoid(sa_ref[...].reshape(NB, RB, 128))
    g_all = -jnp.exp(alog_ref[...]) * _softplus(sb_ref[...].reshape(NB, RB, 128) + dtb_ref[...])
    if lv < NCS * tlx:
        assert NCS == 1
        valid = (lax.broadcasted_iota(jnp.int32, (NB, RB, 128), 1) % tl) < lv
        beta_all = jnp.where(valid, beta_all, 0.0)
        g_all = jnp.where(valid, g_all, 0.0)
    rr, cc, incl = _block_masks(tl)
    diag = rr == cc
    gam = _seq_cumsum(g_all, incl, NB)
    glast = _seq_last(gam, NB, G, tl)
    gcol = _gate_cols(gam, NB, 4)
    bcol = _gate_cols(beta_all, NB, 4)
    egcol = _gate_cols(jnp.exp(gam), NB, 4)
    kdcol = _gate_cols(jnp.exp(_seq_rows(glast, NB, tl) - gam), NB, 4)

    q = _heads(qkv[:, :, 0:D_MIX], NB, DH)
    k = _heads(qkv[:, :, D_MIX:2 * D_MIX], NB, DH)
    v = _heads(qkv[:, :, 2 * D_MIX:3 * D_MIX], NB, DH)
    q = q * lax.rsqrt(jnp.sum(q * q, axis=-1, keepdims=True) + NORM_EPS) * (DH ** -0.5)
    k = k * lax.rsqrt(jnp.sum(k * k, axis=-1, keepdims=True) + NORM_EPS)
    qb, kb = q.astype(BF16), k.astype(BF16)

    dmat = jnp.exp(jnp.where(incl[None], gcol - _gate_rows(gam, NB, 4), NEG))
    nmat = jnp.where(diag[None], 0.0, -(bcol * _bmm_nt(kb, kb) * dmat))
    rhs = jnp.concatenate([bcol * v, (bcol * egcol) * k], axis=-1)
    sol = _bmm(_unit_lower_inverse(nmat, rr, cc, tl), rhs)
    qkm = (_bmm_nt(qb, kb) * dmat).astype(BF16).reshape(NB, N_HEADS, RB, RB)

    _put_chains(u_ref, sol[:, :, 0:DH], NB, G, tl)
    _put_chains(w_ref, sol[:, :, DH:2 * DH].astype(BF16), NB, G, tl)
    _put_chains(qg_ref, (q * egcol).astype(BF16), NB, G, tl)
    _put_chains(kd_ref, (k * kdcol).astype(BF16), NB, G, tl)
    _put_seq_scalars(eg_ref, jnp.exp(glast), NB, G, 4)
    for h in range(N_HEADS):
        for g in range(G):
            qkm_ref[:, g, h] = qkm[:, h, g * tl:(g + 1) * tl, g * tl:(g + 1) * tl]


def _gdn_prep(proj, gc0, cw, alog, dtb, NB, G, tl, lv):
    nb, lp, _ = proj.shape
    assert tl % INV_BLOCK == 0 and (tl // INV_BLOCK) & (tl // INV_BLOCK - 1) == 0
    gx, tlx, NI, NCS = _step_tiling(nb, lp, NB, G, tl)
    nbt = nb * lp // RB
    step = lambda i, c: (i * NCS + c, 0, 0, 0, 0)
    par = lambda i, c: (0, 0)
    chain = lambda last, dt: jax.ShapeDtypeStruct((nbt, G, N_HEADS, tl, last), dt)
    cspec = lambda last: pl.BlockSpec((NB, G, N_HEADS, tl, last), step)
    return pl.pallas_call(
        functools.partial(_gdn_prep_body, NB=NB, G=G, tl=tl, tlx=tlx, NCS=NCS, lv=lv),
        grid=(NI, NCS),
        in_specs=[pl.BlockSpec((gx, tlx, 3 * D_MIX), lambda i, c: (i, c, OFF_QKVC // (3 * D_MIX))),
                  pl.BlockSpec((gx, HIST, 3 * D_MIX),
                               lambda i, c: (i, jnp.maximum(c * (tlx // HIST) - 1, 0), OFF_QKVC // (3 * D_MIX))),
                  pl.BlockSpec((gx, tlx, 128), lambda i, c: (i, c, OFF_SA // 128)),
                  pl.BlockSpec((gx, tlx, 128), lambda i, c: (i, c, OFF_SB // 128)),
                  pl.BlockSpec((gx, HIST, 3 * D_MIX), lambda i, c: (i, 0, 0)),
                  pl.BlockSpec((8, 3 * D_MIX), par),
                  pl.BlockSpec((1, 128), par),
                  pl.BlockSpec((1, 128), par)],
        out_specs=[cspec(DH), cspec(DH), cspec(DH), cspec(DH), cspec(tl),
                   pl.BlockSpec((NB, G, N_HEADS, 1, 128), step),
                   pl.BlockSpec((gx, CONV_C - 1, 3 * D_MIX), lambda i, c: (i, 0, 0))],
        out_shape=[chain(DH, F32), chain(DH, BF16), chain(DH, BF16), chain(DH, BF16), chain(tl, BF16),
                   jax.ShapeDtypeStruct((nbt, G, N_HEADS, 1, 128), F32),
                   jax.ShapeDtypeStruct((nb, CONV_C - 1, 3 * D_MIX), F32)],
        scratch_shapes=[pltpu.VMEM((gx, HIST + tlx, 3 * D_MIX), F32)],
        compiler_params=pltpu.CompilerParams(dimension_semantics=("arbitrary", "arbitrary"),
                                             vmem_limit_bytes=VMEM_LIMIT),
        name=f"gdn_prep_g{G}",
    )(proj, proj, proj, proj, gc0, cw, alog, dtb)


def _gdn_scan_body(u_ref, w_ref, qg_ref, kd_ref, qkm_ref, eg_ref, z_ref, S0_ref, gng_ref, *rest,
                   NS, tl, NC, first):
    yc_ref, S_out = rest[-2:]
    S_ref = S_out.at[0] if first else S_out
    c = pl.program_id(1)
    n = NS * N_HEADS

    @pl.when(c == 0)
    def _():
        S_ref[...] = S0_ref[...]

    S = S_ref[...].reshape(n, DH, DH)
    Sb = S.astype(BF16)
    chains = lambda ref: ref[...].reshape(n, tl, ref.shape[-1])
    v_new = chains(u_ref) - jnp.einsum('ntd,nde->nte', chains(w_ref), Sb, preferred_element_type=F32)
    vnb = v_new.astype(BF16)
    o = (jnp.einsum('ntd,nde->nte', chains(qg_ref), Sb, preferred_element_type=F32)
         + jnp.einsum('nts,nse->nte', chains(qkm_ref), vnb, preferred_element_type=F32))
    eg = eg_ref[...].reshape(n, 1, 128)[:, :, 0:1]
    S_new = eg * S + jnp.einsum('ntd,nte->nde', chains(kd_ref), vnb, preferred_element_type=F32)
    S_ref[...] = S_new.reshape(NS, N_HEADS, DH, DH)
    if first:
        _fill_later_layers(S_out, c == NC - 1)

    on = (o * lax.rsqrt(jnp.mean(o * o, axis=-1, keepdims=True) + NORM_EPS) * gng_ref[...]).reshape(NS, N_HEADS, tl, DH)
    for h in range(N_HEADS):
        yc_ref[:, :, h * DH:(h + 1) * DH] = (on[:, h] * _silu(z_ref[:, :, h * DH:(h + 1) * DH])).astype(BF16)


def _gdn_scan(pre, proj, S0, gng, SB, G, tl, layer, S_prev):
    nb, lp, _ = proj.shape
    NS = SB * G
    NI, NC = nb // NS, lp // tl
    six = lambda a: a.reshape((NI * SB, NC) + a.shape[1:])
    cspec = lambda last: pl.BlockSpec((SB, 1, G, N_HEADS, tl, last), lambda i, c: (i, c, 0, 0, 0, 0))
    u, w, qg, kd, qkm, eg = (six(a) for a in pre)
    st_in, st_out, st_shape, extra_in, extra_args = _layered_state(layer, nb, NS, S_prev)
    return pl.pallas_call(
        functools.partial(_gdn_scan_body, NS=NS, tl=tl, NC=NC, first=layer == 0),
        grid=(NI, NC),
        in_specs=[cspec(DH), cspec(DH), cspec(DH), cspec(DH), cspec(tl),
                  pl.BlockSpec((SB, 1, G, N_HEADS, 1, 128), lambda i, c: (i, c, 0, 0, 0, 0)),
                  pl.BlockSpec((NS, tl, D_MIX), lambda i, c: (i, c, OFF_Z // D_MIX)),
                  st_in,
                  pl.BlockSpec((1, DH), lambda i, c: (0, 0))] + extra_in,
        out_specs=[pl.BlockSpec((NS, tl, D_MIX), lambda i, c: (i, c, 0)), st_out],
        out_shape=[jax.ShapeDtypeStruct((nb, lp, D_MIX), BF16), st_shape],
        input_output_aliases={9: 1} if extra_in else {},
        compiler_params=pltpu.CompilerParams(dimension_semantics=("arbitrary", "arbitrary"),
                                             vmem_limit_bytes=VMEM_LIMIT),
        name=f"gdn_scan_g{G}",
    )(u, w, qg, kd, qkm, eg, proj, S0, gng, *extra_args)


def _route(rl):
    lane = lax.broadcasted_iota(jnp.int32, rl.shape, 1).astype(F32)
    is_g = (lane >= N_EXPERTS) & (lane < N_EXPERTS + N_GROUPS)
    gl = jnp.where(is_g, rl, NEG)
    gmax = jnp.max(gl, axis=-1, keepdims=True)
    grp = jnp.min(jnp.where(gl == gmax, lane - N_EXPERTS, 4.0 * N_EXPERTS), axis=-1, keepdims=True)
    p_grp = 1.0 / jnp.sum(jnp.where(is_g, jnp.exp(gl - gmax), 0.0), axis=-1, keepdims=True)
    lo = grp * EXPERTS_PER_GROUP
    in_grp = (lane >= lo) & (lane < lo + EXPERTS_PER_GROUP)
    el = jnp.where(in_grp, rl, NEG)
    m1 = jnp.max(el, axis=-1, keepdims=True)
    i1 = jnp.min(jnp.where(el == m1, lane, 4.0 * N_EXPERTS), axis=-1, keepdims=True)
    el2 = jnp.where(lane == i1, NEG, el)
    m2 = jnp.max(el2, axis=-1, keepdims=True)
    i2 = jnp.min(jnp.where(el2 == m2, lane, 4.0 * N_EXPERTS), axis=-1, keepdims=True)
    e2 = jnp.exp(m2 - m1)
    w1 = p_grp / (1.0 + e2)
    w2 = p_grp * e2 / (1.0 + e2)
    return i1, i2, w1, w2


def _merge_body(ya_ref, yb_ref, yc_ref, ga_ref, gb_ref, gc_ref, x_ref, mod_ref, wa_ref, wb_ref, wc_ref, wo_ref,
                lg_ref, lb_ref, wrh_ref, wrl_ref, br_ref, cnt0_ref, x1_ref, rinfo_ref, cnt_ref, tril_s, *, bt, lt):
    tm = bt * lt

    @pl.when((pl.program_id(0) == 0) & (pl.program_id(1) == 0))
    def _():
        cnt_ref[...] = cnt0_ref[...]
        rr = lax.broadcasted_iota(jnp.int32, (tm, tm), 0)
        cc = lax.broadcasted_iota(jnp.int32, (tm, tm), 1)
        tril_s[...] = (rr > cc).astype(BF16)

    def r2(ref):
        return ref[...].reshape(tm, ref.shape[-1])

    merged = (_sigmoid(r2(ga_ref)) * _dot(r2(ya_ref), wa_ref[0])
              + _sigmoid(r2(gb_ref)) * _dot(r2(yb_ref), wb_ref[0])
              + _sigmoid(r2(gc_ref)) * _dot(r2(yc_ref), wc_ref[0]))
    out = _dot(merged.astype(BF16), wo_ref[0])
    y = DN_ALPHA * x_ref[...] + (1.0 + mod_ref[:, 2:3, :]) * out.reshape(bt, lt, D_MODEL)
    x1 = _layer_norm(y, lg_ref[0], lb_ref[0])
    x1_ref[...] = x1
    u2 = (x1 * (1.0 + mod_ref[:, 4:5, :]) + mod_ref[:, 3:4, :]).reshape(tm, D_MODEL)
    hi = u2.astype(BF16)
    lo = (u2 - hi.astype(F32)).astype(BF16)
    rl = _dot(hi, wrh_ref[0]) + _dot(lo, wrh_ref[0]) + _dot(hi, wrl_ref[0]) + br_ref[0]
    i1, i2, w1, w2 = _route(rl)
    lane = lax.broadcasted_iota(jnp.int32, (tm, 128), 1).astype(F32)
    onehot = jnp.where((lane == i1) | (lane == i2), 1.0, 0.0)
    before = _dot(tril_s[...], onehot.astype(BF16)) + cnt_ref[0:1, :]
    rank1 = jnp.sum(jnp.where(lane == i1, before, 0.0), axis=-1, keepdims=True)
    rank2 = jnp.sum(jnp.where(lane == i2, before, 0.0), axis=-1, keepdims=True)
    cnt_ref[0:1, :] += jnp.sum(onehot, axis=0, keepdims=True)
    rinfo = jnp.zeros((tm, 128), F32)
    for k, val in enumerate((i1, i2, w1, w2, rank1, rank2)):
        rinfo = jnp.where(lane == k, val, rinfo)
    rinfo_ref[...] = rinfo.reshape(bt, lt, 128)


def _merge(ya, yb, yc, proj, x, mod, wa, wb, wc, wo, lg, lb, wrh, wrl, br, cnt0, layer, bt, lt):
    nb, lp, _ = x.shape
    tok = lambda i, t: (i, t, 0)
    wsp = lambda shape: pl.BlockSpec((1,) + shape, lambda i, t: (layer, 0, 0))
    g0 = OFF_GTS // D_MODEL
    return pl.pallas_call(
        functools.partial(_merge_body, bt=bt, lt=lt),
        grid=(nb // bt, lp // lt),
        in_specs=[pl.BlockSpec((bt, lt, D_MIX), tok),
                  pl.BlockSpec((bt, lt, D_MIX), tok),
                  pl.BlockSpec((bt, lt, D_MIX), tok),
                  pl.BlockSpec((bt, lt, D_MODEL), lambda i, t: (i, t, g0)),
                  pl.BlockSpec((bt, lt, D_MODEL), lambda i, t: (i, t, g0 + 1)),
                  pl.BlockSpec((bt, lt, D_MODEL), lambda i, t: (i, t, g0 + 2)),
                  pl.BlockSpec((bt, lt, D_MODEL), tok),
                  pl.BlockSpec((bt, 6, D_MODEL), lambda i, t: (i, 0, 0)),
                  wsp((D_MIX, D_MODEL)), wsp((D_MIX, D_MODEL)), wsp((D_MIX, D_MODEL)), wsp((D_MODEL, D_MODEL)),
                  wsp((1, D_MODEL)), wsp((1, D_MODEL)),
                  wsp((D_MODEL, 128)), wsp((D_MODEL, 128)), wsp((1, 128)),
                  pl.BlockSpec((SUBLANES, LANES), lambda i, t: (0, 0))],
        out_specs=[pl.BlockSpec((bt, lt, D_MODEL), tok),
                   pl.BlockSpec((bt, lt, 128), tok),
                   pl.BlockSpec((SUBLANES, LANES), lambda i, t: (0, 0))],
        out_shape=[jax.ShapeDtypeStruct((nb, lp, D_MODEL), F32),
                   jax.ShapeDtypeStruct((nb, lp, 128), F32),
                   jax.ShapeDtypeStruct((SUBLANES, LANES), F32)],
        scratch_shapes=[pltpu.VMEM((bt * lt, bt * lt), BF16)],
        compiler_params=pltpu.CompilerParams(dimension_semantics=("arbitrary", "arbitrary"),
                                             vmem_limit_bytes=VMEM_LIMIT),
        name=f"merge_l{layer}_b{bt}",
    )(ya, yb, yc, proj, proj, proj, x, mod, wa, wb, wc, wo, lg, lb, wrh, wrl, br, cnt0)


def _moe_plan(cnt, cnt_first, n_tok, mb):
    n_blocks = 2 * n_tok // mb + N_EXPERTS
    counts = cnt[0, :N_EXPERTS].astype(jnp.int32)
    nblk = (counts + mb - 1) // mb
    pend = jnp.cumsum(nblk)
    block_e = jnp.sum(pend[None, :] <= jnp.arange(n_blocks, dtype=jnp.int32)[:, None], axis=1)
    block_e = jnp.minimum(block_e, N_EXPERTS - 1).astype(jnp.int32)
    n_used = pend[N_EXPERTS - 1:].astype(jnp.int32)
    first_row = (pend - nblk) * mb
    lo = jnp.minimum((first_row + cnt_first[0, :N_EXPERTS].astype(jnp.int32)) // mb, pend - 1)
    ztail = jnp.concatenate([jnp.where(nblk > 0, lo, 0), jnp.where(nblk > 0, pend - 1, -1), n_used])
    return first_row, block_e, n_used, ztail.astype(jnp.int32).reshape(1, 1, 2 * N_EXPERTS + 1), n_blocks


def _dispatch_rows(rinfo, first_row, tm):
    nb, lp, _ = rinfo.shape
    expert = rinfo[..., 0:2].astype(jnp.int32)
    rank = rinfo[..., 4:6].astype(jnp.int32)
    ids = jnp.arange(N_EXPERTS, dtype=jnp.int32)
    start = jnp.sum(jnp.where(expert[..., None] == ids, first_row, 0), axis=-1)
    return (start + rank).reshape(nb * lp // tm, 1, 2 * tm)


def _row_copy(src, dst, sem):
    return pltpu.make_async_copy(src, dst, sem)


def _dispatch_body(pos_ref, ztail_ref, x1_ref, mod_ref, *rest, bt, lt, n_blocks, mb, n_steps, first):
    disp_ref, u_scr, z_scr, sem, zsem = rest[-5:]
    tm = bt * lt

    @pl.when((pl.program_id(0) == 0) & (pl.program_id(1) == 0) & first)
    def _():
        z_scr[...] = jnp.zeros_like(z_scr)

        def zero_block(j):
            return _row_copy(z_scr, disp_ref.at[pl.ds(pl.multiple_of(j * mb, mb), mb)], zsem)

        def for_zeroed_blocks(fn):
            for e in range(N_EXPERTS):
                lax.fori_loop(ztail_ref[0, 0, e], ztail_ref[0, 0, N_EXPERTS + e] + 1, fn, 0)
            lax.fori_loop(ztail_ref[0, 0, 2 * N_EXPERTS], n_blocks, fn, 0)

        for_zeroed_blocks(lambda j, c: (zero_block(j).start(), c)[1])
        for_zeroed_blocks(lambda j, c: (zero_block(j).wait(), c)[1])

    step = pl.program_id(0) * pl.num_programs(1) + pl.program_id(1)
    slot = step % 2
    u2 = x1_ref[...] * (1.0 + mod_ref[:, 4:5, :]) + mod_ref[:, 3:4, :]
    u_scr[slot] = u2.reshape(tm // SUBLANES, SUBLANES, D_MODEL)

    def issue(i, carry):
        for r in range(SUBLANES):
            for k in range(TOP_K):
                row = pos_ref[0, 0, TOP_K * (SUBLANES * i + r) + k]
                _row_copy(u_scr.at[slot, i, pl.ds(r, 1)], disp_ref.at[pl.ds(row, 1)],
                          sem.at[slot]).start(priority=k)
        return carry

    lax.fori_loop(0, tm // SUBLANES, issue, 0)

    def drain(which):
        for k in range(TOP_K):
            _row_copy(disp_ref.at[pl.ds(0, tm)], disp_ref.at[pl.ds(0, tm)], sem.at[which]).wait()

    @pl.when(step > 0)
    def _():
        drain(1 - slot)

    @pl.when(step == n_steps - 1)
    def _():
        drain(slot)


def _dispatch(x1, mod, pos, ztail, n_blocks, mb, bt, lt, buf):
    nb, lp, _ = x1.shape
    nt = lp // lt
    extra_in, extra_args = ([], []) if buf is None else ([pl.BlockSpec(memory_space=pl.ANY)], [buf])
    return pl.pallas_call(
        functools.partial(_dispatch_body, bt=bt, lt=lt, n_blocks=n_blocks, mb=mb, n_steps=(nb // bt) * nt,
                          first=buf is None),
        grid=(nb // bt, nt),
        in_specs=[pl.BlockSpec((1, 1, 2 * bt * lt), lambda i, t: (i * nt + t, 0, 0), memory_space=pltpu.SMEM),
                  pl.BlockSpec((1, 1, 2 * N_EXPERTS + 1), lambda i, t: (0, 0, 0), memory_space=pltpu.SMEM),
                  pl.BlockSpec((bt, lt, D_MODEL), lambda i, t: (i, t, 0)),
                  pl.BlockSpec((bt, 6, D_MODEL), lambda i, t: (i, 0, 0))] + extra_in,
        out_specs=pl.BlockSpec(memory_space=pl.ANY),
        out_shape=jax.ShapeDtypeStruct((n_blocks * mb, D_MODEL), F32),
        scratch_shapes=[pltpu.VMEM((2, bt * lt // SUBLANES, SUBLANES, D_MODEL), F32), pltpu.VMEM((mb, D_MODEL), F32),
                        pltpu.SemaphoreType.DMA((2,)), pltpu.SemaphoreType.DMA],
        compiler_params=pltpu.CompilerParams(dimension_semantics=("arbitrary", "arbitrary"),
                                             vmem_limit_bytes=VMEM_LIMIT),
        input_output_aliases={4: 0} if extra_in else {},
        name=f"dispatch_b{bt}",
    )(pos, ztail, x1, mod, *extra_args)


def _experts_body(be_ref, nu_ref, x_ref, wg_ref, wu_ref, wd_ref, o_ref):
    j = pl.program_id(0)

    @pl.when(j < nu_ref[0])
    def _():
        x = x_ref[...].astype(BF16)
        hb = _silu(_dot(x, wg_ref[0, 0].astype(BF16))) * _dot(x, wu_ref[0, 0].astype(BF16))
        o_ref[...] = _dot(hb.astype(BF16), wd_ref[0, 0].astype(BF16))

    @pl.when(j >= nu_ref[0])
    def _():
        o_ref[...] = jnp.zeros_like(o_ref)


def _experts(disp, block_e, n_used, wg, wu, wd, layer, mb):
    n_blocks = disp.shape[0] // mb
    wmap = lambda j, be, nu: (layer, be[j], 0, 0)
    return pl.pallas_call(
        _experts_body,
        grid_spec=pltpu.PrefetchScalarGridSpec(
            num_scalar_prefetch=2,
            grid=(n_blocks,),
            in_specs=[pl.BlockSpec((mb, D_MODEL), lambda j, be, nu: (jnp.minimum(j, nu[0] - 1), 0)),
                      pl.BlockSpec((1, 1, D_MODEL, D_EXPERT), wmap),
                      pl.BlockSpec((1, 1, D_MODEL, D_EXPERT), wmap),
                      pl.BlockSpec((1, 1, D_EXPERT, D_MODEL), wmap)],
            out_specs=pl.BlockSpec((mb, D_MODEL), lambda j, be, nu: (j, 0))),
        out_shape=jax.ShapeDtypeStruct(disp.shape, F32),
        compiler_params=pltpu.CompilerParams(dimension_semantics=("arbitrary",), vmem_limit_bytes=VMEM_LIMIT),
        name=f"experts_l{layer}_n{n_blocks}",
    )(block_e, n_used, disp, wg, wu, wd)


def _combine_body(pos_ref, pos_next_ref, x1_ref, mod_ref, rinfo_ref, eo_ref, lg_ref, lb_ref, x2_ref, r_scr, sem,
                  *, bt, lt, n_steps):
    tm = bt * lt
    step = pl.program_id(0) * pl.num_programs(1) + pl.program_id(1)
    slot = step % 2

    def gather(p_ref, to_slot):
        def issue(i, carry):
            for r in range(SUBLANES):
                for k in range(TOP_K):
                    row = p_ref[0, 0, TOP_K * (SUBLANES * i + r) + k]
                    _row_copy(eo_ref.at[pl.ds(row, 1)], r_scr.at[to_slot, k, i, pl.ds(r, 1)],
                              sem.at[to_slot]).start(priority=k)
            return carry

        lax.fori_loop(0, tm // SUBLANES, issue, 0)

    @pl.when(step == 0)
    def _():
        gather(pos_ref, 0)

    @pl.when(step + 1 < n_steps)
    def _():
        gather(pos_next_ref, 1 - slot)

    for k in range(TOP_K):
        _row_copy(eo_ref.at[pl.ds(0, tm)], eo_ref.at[pl.ds(0, tm)], sem.at[slot]).wait()

    rinfo = rinfo_ref[...].reshape(tm, 128)
    rows = r_scr[slot]
    moe = (rows[0].reshape(tm, D_MODEL) * rinfo[:, 2:3] + rows[1].reshape(tm, D_MODEL) * rinfo[:, 3:4])
    y = DN_ALPHA * x1_ref[...] + (1.0 + mod_ref[:, 5:6, :]) * moe.reshape(bt, lt, D_MODEL)
    x2_ref[...] = _layer_norm(y, lg_ref[0], lb_ref[0])


def _combine(x1, mod, rinfo, eo, pos, lg, lb, layer, bt, lt):
    nb, lp, _ = x1.shape
    nt = lp // lt
    tok = lambda i, t: (i, t, 0)
    n_steps = (nb // bt) * nt
    return pl.pallas_call(
        functools.partial(_combine_body, bt=bt, lt=lt, n_steps=n_steps),
        grid=(nb // bt, nt),
        in_specs=[pl.BlockSpec((1, 1, 2 * bt * lt), lambda i, t: (i * nt + t, 0, 0), memory_space=pltpu.SMEM),
                  pl.BlockSpec((1, 1, 2 * bt * lt), lambda i, t: (jnp.minimum(i * nt + t + 1, n_steps - 1), 0, 0),
                               memory_space=pltpu.SMEM),
                  pl.BlockSpec((bt, lt, D_MODEL), tok),
                  pl.BlockSpec((bt, 6, D_MODEL), lambda i, t: (i, 0, 0)),
                  pl.BlockSpec((bt, lt, 128), tok),
                  pl.BlockSpec(memory_space=pl.ANY),
                  pl.BlockSpec((1, 1, D_MODEL), lambda i, t: (layer, 0, 0)),
                  pl.BlockSpec((1, 1, D_MODEL), lambda i, t: (layer, 0, 0))],
        out_specs=pl.BlockSpec((bt, lt, D_MODEL), tok),
        out_shape=jax.ShapeDtypeStruct((nb, lp, D_MODEL), F32),
        scratch_shapes=[pltpu.VMEM((2, TOP_K, bt * lt // SUBLANES, SUBLANES, D_MODEL), F32),
                        pltpu.SemaphoreType.DMA((2,))],
        compiler_params=pltpu.CompilerParams(dimension_semantics=("arbitrary", "arbitrary"),
                                             vmem_limit_bytes=VMEM_LIMIT),
        name=f"combine_l{layer}_b{bt}",
    )(pos, pos, x1, mod, rinfo, eo, lg, lb)


def _hist(state):
    return jnp.pad(state, ((0, 0), (HIST - state.shape[1], 0), (0, 0)))


def _trunks(groups, p, mb):
    for g in groups:
        g.update(C=None, S=None, new={key: [] for key in ('n', 'm', 'conv', 'gconv')})
    n_tok = sum(g['x'].shape[0] * g['x'].shape[1] for g in groups)
    for l in range(DEPTH):
        cnt = jnp.zeros((SUBLANES, LANES), F32)
        for g in groups:
            st, mix, lv, nb = g['st'], g['mix'], g['lv'], g['x'].shape[0]
            proj, x = _inproj(g['x'], g['mod'][l], p['ln_in_g'], p['ln_in_b'], p['w_in_r'], p['b_in_r'], l, l == 0,
                              *g['ip_tile'])
            m0 = jnp.broadcast_to(st['m'][l][:, :, None, None], (nb, N_HEADS, 1, 128))
            yb, conv, *pre = _mlstm_prep(proj, _hist(st['conv'][l]), p['conv_b_w8'][l],
                                         mix['NB'], mix['G'], mix['tl'], lv)
            ya, g['C'], n, m = _mlstm_scan(pre, proj, st['C'], st['n'][l], m0, p['mlstm_norm_g'][l:l + 1],
                                           mix['SB'], mix['G'], mix['tl'], l, g['C'])
            *pre, gconv = _gdn_prep(proj, _hist(st['gconv'][l]), p['conv_c_w8'][l], p['alog_row'][l],
                                    p['dtb_row'][l], mix['NB'], mix['G'], mix['tl'], lv)
            yc, g['S'] = _gdn_scan(pre, proj, st['S'], p['gdn_norm_g'][l:l + 1], mix['SB'], mix['G'], mix['tl'],
                                   l, g['S'])
            g['x1'], g['rinfo'], cnt = _merge(ya, yb, yc, proj, x, g['mod'][l], p['w_br_a'], p['w_br_b'], p['w_br_c'],
                                              p['w_out'], p['ln1_g'], p['ln1_b'], p['wr_hi'], p['wr_lo'], p['br'],
                                              cnt, l, g['bt'], g['lt'])
            g['cnt'] = cnt
            for key, val in (('n', n), ('m', m[:, :, 0, 0]), ('conv', conv), ('gconv', gconv)):
                g['new'][key].append(val)
        first_row, block_e, n_used, ztail, n_blocks = _moe_plan(cnt, groups[0]['cnt'], n_tok, mb)
        disp = None
        for g in groups:
            g['pos'] = _dispatch_rows(g['rinfo'], first_row, g['moe_tile'][0] * g['moe_tile'][1])
            disp = _dispatch(g['x1'], g['mod'][l], g['pos'], ztail, n_blocks, mb, *g['moe_tile'], disp)
        eo = _experts(disp, block_e, n_used, p['exp_w_gate'], p['exp_w_up'], p['exp_w_down'], l, mb)
        for g in groups:
            g['x'] = _combine(g['x1'], g['mod'][l], g['rinfo'], eo, g['pos'], p['ln2_g'], p['ln2_b'], l,
                              *g['moe_tile'])
    return [(g['x'], dict({key: jnp.stack(val) for key, val in g['new'].items()}, C=g['C'], S=g['S']))
            for g in groups]


def kernel(x_prompt, x_sample, state_mlstm_C, state_mlstm_n, state_mlstm_m, state_conv, state_gdn_S, state_gdn_conv, c_prompt, c_sample, ln_in_g, ln_in_b, w_ada, b_ada, w_in, b_in, mlstm_norm_g, conv_b_w, conv_c_w, gdn_a_log, gdn_dt_bias, gdn_norm_g, w_br_a, w_br_b, w_br_c, w_out, ln1_g, ln1_b, router_g_w, router_g_b, router_e_w, router_e_b, exp_w_gate, exp_w_up, exp_w_down, ln2_g, ln2_b):
    nbp, lp, _ = x_prompt.shape
    nbs, ls, _ = x_sample.shape
    lsp = 8

    def regroup(a):
        out = jnp.zeros(a.shape[:-1] + (N_PROJ,), a.dtype)
        for src, end, dst in PROJ_SEGMENTS:
            out = out.at[..., dst:dst + end - src].set(a[..., src:end])
        return out

    wr = jnp.concatenate([router_e_w, router_g_w, jnp.zeros((DEPTH, D_MODEL, 128 - N_EXPERTS - N_GROUPS), F32)], axis=-1)
    wr_hi = wr.astype(BF16)
    lane_pad = lambda a: jnp.pad(a, ((0, 0), (4, 128 - 4 - N_HEADS)))[:, None, :]
    p = dict(
        ln_in_g=ln_in_g.reshape(1, D_MODEL), ln_in_b=ln_in_b.reshape(1, D_MODEL),
        w_in_r=_regroup_w_in(w_in), b_in_r=regroup(b_in).reshape(DEPTH, 1, N_PROJ),
        mlstm_norm_g=mlstm_norm_g,
        conv_b_w8=jnp.pad(conv_b_w, ((0, 0), (0, 8 - CONV_B), (0, 0))),
        conv_c_w8=jnp.pad(conv_c_w, ((0, 0), (0, 8 - CONV_C), (0, 0))),
        alog_row=lane_pad(gdn_a_log), dtb_row=lane_pad(gdn_dt_bias), gdn_norm_g=gdn_norm_g,
        w_br_a=w_br_a.astype(BF16), w_br_b=w_br_b.astype(BF16), w_br_c=w_br_c.astype(BF16),
        w_out=w_out.astype(BF16),
        ln1_g=ln1_g.reshape(DEPTH, 1, D_MODEL), ln1_b=ln1_b.reshape(DEPTH, 1, D_MODEL),
        wr_hi=wr_hi, wr_lo=(wr - wr_hi.astype(F32)).astype(BF16),
        br=jnp.concatenate([router_e_b, router_g_b, jnp.zeros((DEPTH, 128 - N_EXPERTS - N_GROUPS), F32)],
                           axis=-1).reshape(DEPTH, 1, 128),
        exp_w_gate=exp_w_gate, exp_w_up=exp_w_up, exp_w_down=exp_w_down,
        ln2_g=ln2_g.reshape(DEPTH, 1, D_MODEL), ln2_b=ln2_b.reshape(DEPTH, 1, D_MODEL),
    )

    mod = _ada(jnp.concatenate([c_prompt, c_sample], axis=0), w_ada, b_ada)
    mod = mod.reshape(DEPTH, nbp + nbs, 6, D_MODEL)

    zeros = lambda *s: jnp.zeros((DEPTH, nbp) + s, F32)
    st_p = {'C': zeros(N_HEADS, DH, DH), 'n': zeros(N_HEADS, DH), 'm': zeros(N_HEADS),
            'conv': zeros(CONV_B - 1, D_MIX), 'S': zeros(N_HEADS, DH, DH), 'gconv': zeros(CONV_C - 1, 3 * D_MIX)}
    prompt = dict(x=x_prompt, mod=mod[:, :nbp], st=st_p, bt=1, lt=512, lv=lp,
                  mix=dict(NB=8, G=1, tl=RB, SB=nbp), ip_tile=(1, 256), moe_tile=(1, 1024))

    st_s = {'C': state_mlstm_C, 'n': state_mlstm_n, 'm': state_mlstm_m, 'conv': state_conv,
            'S': state_gdn_S, 'gconv': state_gdn_conv}
    xs = jnp.pad(x_sample, ((0, 0), (0, lsp - ls), (0, 0)))
    sample = dict(x=xs, mod=mod[:, nbp:], st=st_s, bt=64, lt=lsp, lv=ls,
                  mix=dict(NB=4, G=RB // lsp, tl=lsp, SB=1), ip_tile=(32, lsp), moe_tile=(nbs, lsp))
    (y_p, sp), (y_s, ss) = _trunks([prompt, sample], p, mb=512)
    y_s = y_s[:, :ls]

    return (y_p, y_s, sp['C'], sp['n'], sp['m'], sp['conv'], sp['S'], sp['gconv'],
            ss['C'], ss['n'], ss['m'], ss['conv'], ss['S'], ss['gconv'])
```

```python
import functools

import jax
import jax.numpy as jnp
from jax import lax
from jax.experimental import pallas as pl
from jax.experimental.pallas import tpu as pltpu

F32 = jnp.float32
BF16 = jnp.bfloat16

LANES = 128
SUBLANES = 8

D_MODEL = 1024
DEPTH = 2
N_HEADS = 4
DH = 128
D_MIX = N_HEADS * DH
N_EXPERTS = 32
EXPERTS_PER_GROUP = 8
N_GROUPS = 4
TOP_K = 2
D_EXPERT = 256
ZERO_ROWS = 128
CONV_B = 3
CONV_C = 4
HIST = SUBLANES
RB = 64
INV_BLOCK = 8
DN_ALPHA = (2 * DEPTH) ** 0.25
LN_EPS = 1e-5
NORM_EPS = 1e-6
NEG = -1e30

OFF_QKVC = 0
OFF_BCH = 1536
OFF_GTS = 3072
OFF_QKVO = 6144
OFF_Z = 8192
OFF_SA = 8704
OFF_SB = 8832
N_PROJ = 8960
TN_PROJ = 1280


def _proj_segments():
    widths = (('qkvo', 4 * D_MIX), ('i', N_HEADS), ('f', N_HEADS), ('bch', 3 * D_MIX), ('qkvc', 3 * D_MIX),
              ('z', D_MIX), ('beta', N_HEADS), ('a', N_HEADS), ('gts', 3 * D_MODEL))
    dst = dict(qkvc=OFF_QKVC, bch=OFF_BCH, gts=OFF_GTS, qkvo=OFF_QKVO, z=OFF_Z,
               i=OFF_SA, beta=OFF_SA + N_HEADS, f=OFF_SB, a=OFF_SB + N_HEADS)
    segs, col = [], 0
    for name, width in widths:
        segs.append((col, col + width, dst[name]))
        col += width
    return tuple(segs)


PROJ_SEGMENTS = _proj_segments()

VMEM_LIMIT = 52 * 1024 * 1024


def _dot(a, b):
    return jnp.dot(a, b, preferred_element_type=F32)


def _split3(x):
    hi = x.astype(BF16)
    r = x - hi.astype(F32)
    mid = r.astype(BF16)
    lo = (r - mid.astype(F32)).astype(BF16)
    return hi, mid, lo


def _layer_norm(x, g, b):
    mu = jnp.mean(x, axis=-1, keepdims=True)
    xc = x - mu
    var = jnp.mean(xc * xc, axis=-1, keepdims=True)
    return xc * lax.rsqrt(var + LN_EPS) * g + b


def _sigmoid(x):
    return jax.nn.sigmoid(x)


def _silu(x):
    return x * _sigmoid(x)


def _log_sigmoid(x):
    return jnp.minimum(x, 0.0) - jnp.log1p(jnp.exp(-jnp.abs(x)))


def _softplus(x):
    return jnp.maximum(x, 0.0) + jnp.log1p(jnp.exp(-jnp.abs(x)))


def _ada_body(c_ref, w_ref, b_ref, o_ref):
    c = c_ref[...]
    s = _silu(c).astype(BF16)
    o_ref[0] = _dot(s, w_ref[0].astype(BF16)) + b_ref[0]


def _ada(c_all, w_ada, b_ada):
    nb = c_all.shape[0]
    return pl.pallas_call(
        _ada_body,
        grid=(DEPTH, 6),
        in_specs=[pl.BlockSpec((nb, D_MODEL), lambda l, j: (0, 0)),
                  pl.BlockSpec((1, D_MODEL, D_MODEL), lambda l, j: (l, 0, j)),
                  pl.BlockSpec((1, 1, D_MODEL), lambda l, j: (l, 0, j))],
        out_specs=pl.BlockSpec((1, nb, D_MODEL), lambda l, j: (l, 0, j)),
        out_shape=jax.ShapeDtypeStruct((DEPTH, nb, 6 * D_MODEL), F32),
        compiler_params=pltpu.CompilerParams(dimension_semantics=("arbitrary", "arbitrary"),
                                             vmem_limit_bytes=VMEM_LIMIT),
        name="ada",
    )(c_all, w_ada, b_ada.reshape(DEPTH, 1, 6 * D_MODEL))


def _regroup_body(w_ref, o_ref):
    o_ref[...] = jnp.zeros_like(o_ref)
    for src, end, dst in PROJ_SEGMENTS:
        o_ref[0, :, dst:dst + end - src] = w_ref[0, :, src:end].astype(BF16)


def _regroup_w_in(w_in):
    n_in = w_in.shape[-1]
    rows = 256
    return pl.pallas_call(
        _regroup_body,
        grid=(DEPTH, D_MODEL // rows),
        in_specs=[pl.BlockSpec((1, rows, n_in), lambda l, i: (l, i, 0))],
        out_specs=pl.BlockSpec((1, rows, N_PROJ), lambda l, i: (l, i, 0)),
        out_shape=jax.ShapeDtypeStruct((DEPTH, D_MODEL, N_PROJ), BF16),
        compiler_params=pltpu.CompilerParams(dimension_semantics=("arbitrary", "arbitrary"),
                                             vmem_limit_bytes=VMEM_LIMIT),
        name="regroup_w_in",
    )(w_in)


def _inproj_body(x_ref, mod_ref, g_ref, b_ref, w_ref, bias_ref, proj_ref, *rest, apply_ln, bt, lt):
    x = x_ref[...]
    if apply_ln:
        x = _layer_norm(x, g_ref[...], b_ref[...])
        rest[0][...] = x
    u = (x * (1.0 + mod_ref[:, 1:2, :]) + mod_ref[:, 0:1, :]).reshape(bt * lt, D_MODEL).astype(BF16)
    for j in range(N_PROJ // TN_PROJ):
        cs = slice(j * TN_PROJ, (j + 1) * TN_PROJ)
        proj_ref[:, :, cs] = (_dot(u, w_ref[0, :, cs]) + bias_ref[0, :, cs]).reshape(bt, lt, TN_PROJ)


def _inproj(x, mod, ln_g, ln_b, w_r, b_r, layer, apply_ln, bt, lt):
    nb, lp, _ = x.shape
    tok = lambda i, t: (i, t, 0)
    out_shape = [jax.ShapeDtypeStruct((nb, lp, N_PROJ), F32)]
    out_specs = [pl.BlockSpec((bt, lt, N_PROJ), tok)]
    if apply_ln:
        out_shape.append(jax.ShapeDtypeStruct((nb, lp, D_MODEL), F32))
        out_specs.append(pl.BlockSpec((bt, lt, D_MODEL), tok))
    res = pl.pallas_call(
        functools.partial(_inproj_body, apply_ln=apply_ln, bt=bt, lt=lt),
        grid=(nb // bt, lp // lt),
        in_specs=[pl.BlockSpec((bt, lt, D_MODEL), tok),
                  pl.BlockSpec((bt, 6, D_MODEL), lambda i, t: (i, 0, 0)),
                  pl.BlockSpec((1, D_MODEL), lambda i, t: (0, 0)),
                  pl.BlockSpec((1, D_MODEL), lambda i, t: (0, 0)),
                  pl.BlockSpec((1, D_MODEL, N_PROJ), lambda i, t: (layer, 0, 0), pipeline_mode=pl.Buffered(1)),
                  pl.BlockSpec((1, 1, N_PROJ), lambda i, t: (layer, 0, 0))],
        out_specs=out_specs,
        out_shape=out_shape,
        compiler_params=pltpu.CompilerParams(dimension_semantics=("arbitrary", "arbitrary"),
                                             vmem_limit_bytes=VMEM_LIMIT),
        name=f"inproj_l{layer}_b{bt}",
    )(x, mod, ln_g, ln_b, w_r, b_r)
    return (res[0], res[1]) if apply_ln else (res[0], x)


def _conv_taps(xp_s, w_ref, width, tl):
    acc = None
    for j in range(width):
        tap = xp_s[:, pl.ds(HIST - (width - 1) + j, tl), :] * w_ref[j:j + 1, :].reshape(1, 1, -1)
        acc = tap if acc is None else acc + tap
    return acc


def _conv_history(xp_s, hist_ref, prev_ref, n_steps):
    if n_steps > 1:
        @pl.when(pl.program_id(1) == 0)
        def _():
            xp_s[:, 0:HIST, :] = hist_ref[...]

        @pl.when(pl.program_id(1) > 0)
        def _():
            xp_s[:, 0:HIST, :] = prev_ref[...]
    else:
        xp_s[:, 0:HIST, :] = hist_ref[...]


def _heads(x, nb, width):
    return jnp.stack([x[:, :, h * width:(h + 1) * width] for h in range(N_HEADS)],
                     axis=1).reshape(nb * N_HEADS, RB, width)


def _gate_cols(x, nb, lane0):
    return jnp.stack([x[:, :, lane0 + h:lane0 + h + 1] for h in range(N_HEADS)],
                     axis=1).reshape(nb * N_HEADS, RB, 1)


def _gate_rows(x, nb, lane0):
    xt = jnp.swapaxes(x, 1, 2)
    return jnp.stack([xt[:, lane0 + h:lane0 + h + 1, :] for h in range(N_HEADS)],
                     axis=1).reshape(nb * N_HEADS, 1, RB)


def _bmm(a, b):
    return jnp.einsum('nts,nsu->ntu', a.astype(BF16), b.astype(BF16), preferred_element_type=F32)


def _bmm_nt(a, b):
    return jnp.einsum('ntd,nsd->nts', a, b, preferred_element_type=F32)


def _block_masks(tl):
    rr = lax.broadcasted_iota(jnp.int32, (RB, RB), 0)
    cc = lax.broadcasted_iota(jnp.int32, (RB, RB), 1)
    incl = rr >= cc
    if tl < RB:
        incl = incl & ((rr // tl) == (cc // tl))
    return rr, cc, incl


def _seq_cumsum(x, incl, nb):
    tril = jnp.broadcast_to(incl.astype(BF16)[None], (nb, RB, RB))
    hi, mid, lo = _split3(x)
    return _bmm(tril, hi) + _bmm(tril, mid) + _bmm(tril, lo)


def _seq_last(x, nb, G, tl):
    return x.reshape(nb * G, tl, 128)[:, tl - 1:tl, :]


def _seq_rows(x3, nb, tl):
    return jnp.broadcast_to(x3, (x3.shape[0], tl, 128)).reshape(nb, RB, 128)


def _put_chains(ref, val, nb, G, tl):
    val4 = val.reshape(nb, N_HEADS, RB, val.shape[-1])
    for h in range(N_HEADS):
        ref[:, :, h] = val4[:, h].reshape(nb, G, tl, val.shape[-1])


def _put_seq_scalars(ref, x3, nb, G, lane0):
    x4 = x3.reshape(nb, G, 1, 128)
    for h in range(N_HEADS):
        ref[:, :, h] = jnp.broadcast_to(x4[:, :, :, lane0 + h:lane0 + h + 1], (nb, G, 1, 128))


def _step_tiling(nb, lp, NB, G, tl):
    assert G * tl == RB
    if G == 1:
        gx, tlx = 1, NB * RB
    else:
        assert lp == tl
        gx, tlx = NB * G, tl
    return gx, tlx, nb // gx, lp // tlx


def _layered_state(layer, nb, NS, prev):
    st_in = pl.BlockSpec((None, NS, N_HEADS, DH, DH), lambda i, c: (layer, i, 0, 0, 0))
    shape = jax.ShapeDtypeStruct((DEPTH, nb, N_HEADS, DH, DH), F32)
    if layer == 0:
        return st_in, pl.BlockSpec((DEPTH, NS, N_HEADS, DH, DH), lambda i, c: (0, i, 0, 0, 0)), shape, [], []
    return st_in, st_in, shape, [pl.BlockSpec(memory_space=pl.ANY)], [prev]


def _fill_later_layers(ref, when):
    @pl.when(when)
    def _():
        for l in range(1, DEPTH):
            ref[l] = ref[0]


def _mlstm_prep_body(qkvo_ref, bch_ref, prev_ref, sa_ref, sb_ref, cv0_ref, cw_ref,
                     yb_ref, cv_ref, nv_ref, q_ref, kw_ref, v_ref, rows_ref, bl_ref, bc_ref, kn_ref, xp_s,
                     *, NB, G, tl, tlx, NCS, lv):
    cs = pl.program_id(1)
    n = NB * N_HEADS
    lvl = lv - (NCS - 1) * tlx

    if NCS > 1:
        @pl.when(cs == 0)
        def _():
            xp_s[:, 0:HIST, :] = cv0_ref[...]

        @pl.when(cs > 0)
        def _():
            xp_s[:, 0:HIST, :] = prev_ref[:, :, D_MIX:2 * D_MIX] * prev_ref[:, :, 2 * D_MIX:3 * D_MIX]
    else:
        xp_s[:, 0:HIST, :] = cv0_ref[...]
    xp_s[:, HIST:HIST + tlx, :] = bch_ref[:, :, D_MIX:2 * D_MIX] * bch_ref[:, :, 2 * D_MIX:3 * D_MIX]
    yb_ref[...] = (bch_ref[:, :, 0:D_MIX] * _conv_taps(xp_s, cw_ref, CONV_B, tlx)).astype(BF16)

    @pl.when(cs == NCS - 1)
    def _():
        cv_ref[...] = xp_s[:, pl.ds(HIST + lvl - (CONV_B - 1), CONV_B - 1), :]

    i_all = sa_ref[...].reshape(NB, RB, 128)
    f_all = _log_sigmoid(sb_ref[...].reshape(NB, RB, 128))
    if lv < NCS * tlx:
        assert NCS == 1
        valid = (lax.broadcasted_iota(jnp.int32, (NB, RB, 128), 1) % tl) < lv
        i_all = jnp.where(valid, i_all, NEG)
        f_all = jnp.where(valid, f_all, 0.0)
    _, _, incl = _block_masks(tl)
    bcum = _seq_cumsum(f_all, incl, NB)
    blast = _seq_last(bcum, NB, G, tl)
    val = _seq_rows(blast, NB, tl) - bcum + i_all
    bmax = jnp.max(val.reshape(NB * G, tl, 128), axis=1, keepdims=True)
    wk0 = jnp.exp(val - _seq_rows(bmax, NB, tl))

    qkvo = qkvo_ref[...].reshape(NB, RB, 4 * D_MIX)
    q = _heads(qkvo[:, :, 0:D_MIX], NB, DH)
    k = _heads(qkvo[:, :, D_MIX:2 * D_MIX], NB, DH) * (DH ** -0.5)
    v = _heads(qkvo[:, :, 2 * D_MIX:3 * D_MIX], NB, DH)
    qb, kb, vb = q.astype(BF16), k.astype(BF16), v.astype(BF16)

    b_col = _gate_cols(bcum, NB, 0)
    dlog = jnp.where(incl[None], b_col - _gate_rows(bcum, NB, 0) + _gate_rows(i_all, NB, 0), NEG)
    d = jnp.max(dlog, axis=-1, keepdims=True)
    s0 = _bmm_nt(qb, kb) * jnp.exp(dlog - d)
    kw0 = k * _gate_cols(wk0, NB, 0)

    _put_chains(nv_ref, _bmm(s0, vb), NB, G, tl)
    _put_chains(q_ref, qb, NB, G, tl)
    _put_chains(kw_ref, kw0.astype(BF16), NB, G, tl)
    _put_chains(v_ref, vb, NB, G, tl)
    rows = jnp.swapaxes(jnp.concatenate([d, b_col, jnp.sum(s0, axis=-1, keepdims=True),
                                         jnp.zeros((n, RB, 5), F32)], axis=-1), 1, 2).reshape(NB, N_HEADS, 8, RB)
    for h in range(N_HEADS):
        for g in range(G):
            rows_ref[:, g, h] = rows[:, h, :, g * tl:(g + 1) * tl]
    _put_seq_scalars(bl_ref, blast, NB, G, 0)
    _put_seq_scalars(bc_ref, bmax, NB, G, 0)
    kn = jnp.sum(kw0.reshape(n * G, tl, DH), axis=1, keepdims=True).reshape(NB, N_HEADS, G, 1, DH)
    for h in range(N_HEADS):
        kn_ref[:, :, h] = kn[:, h]


def _mlstm_prep(proj, cv0, cw, NB, G, tl, lv):
    nb, lp, _ = proj.shape
    gx, tlx, NI, NCS = _step_tiling(nb, lp, NB, G, tl)
    nbt = nb * lp // RB
    step = lambda i, c: (i * NCS + c, 0, 0, 0, 0)
    chain = lambda last, dt: jax.ShapeDtypeStruct((nbt, G, N_HEADS, tl, last), dt)
    cspec = lambda last: pl.BlockSpec((NB, G, N_HEADS, tl, last), step)
    scal = jax.ShapeDtypeStruct((nbt, G, N_HEADS, 1, 128), F32)
    sspec = pl.BlockSpec((NB, G, N_HEADS, 1, 128), step)
    bch = OFF_BCH // (3 * D_MIX)
    return pl.pallas_call(
        functools.partial(_mlstm_prep_body, NB=NB, G=G, tl=tl, tlx=tlx, NCS=NCS, lv=lv),
        grid=(NI, NCS),
        in_specs=[pl.BlockSpec((gx, tlx, 4 * D_MIX), lambda i, c: (i, c, OFF_QKVO // (4 * D_MIX))),
                  pl.BlockSpec((gx, tlx, 3 * D_MIX), lambda i, c: (i, c, bch)),
                  pl.BlockSpec((gx, HIST, 3 * D_MIX), lambda i, c: (i, jnp.maximum(c * (tlx // HIST) - 1, 0), bch)),
                  pl.BlockSpec((gx, tlx, 128), lambda i, c: (i, c, OFF_SA // 128)),
                  pl.BlockSpec((gx, tlx, 128), lambda i, c: (i, c, OFF_SB // 128)),
                  pl.BlockSpec((gx, HIST, D_MIX), lambda i, c: (i, 0, 0)),
                  pl.BlockSpec((8, D_MIX), lambda i, c: (0, 0))],
        out_specs=[pl.BlockSpec((gx, tlx, D_MIX), lambda i, c: (i, c, 0)),
                   pl.BlockSpec((gx, CONV_B - 1, D_MIX), lambda i, c: (i, 0, 0)),
                   cspec(DH), cspec(DH), cspec(DH), cspec(DH),
                   pl.BlockSpec((NB, G, N_HEADS, 8, tl), step), sspec, sspec, sspec],
        out_shape=[jax.ShapeDtypeStruct((nb, lp, D_MIX), BF16),
                   jax.ShapeDtypeStruct((nb, CONV_B - 1, D_MIX), F32),
                   chain(DH, F32), chain(DH, BF16), chain(DH, BF16), chain(DH, BF16),
                   jax.ShapeDtypeStruct((nbt, G, N_HEADS, 8, tl), F32),
                   scal, scal, scal],
        scratch_shapes=[pltpu.VMEM((gx, HIST + tlx, D_MIX), F32)],
        compiler_params=pltpu.CompilerParams(dimension_semantics=("arbitrary", "arbitrary"),
                                             vmem_limit_bytes=VMEM_LIMIT),
        name=f"mlstm_prep_g{G}",
    )(proj, proj, proj, proj, proj, cv0, cw)


def _mlstm_scan_body(nv_ref, q_ref, kw_ref, v_ref, rows_ref, bl_ref, bc_ref, kn_ref, o_ref, C0_ref, n0_ref, m0_ref,
                     ng_ref, *rest, NS, tl, NC, first):
    ya_ref, C_out, n_ref, m_ref = rest[-4:]
    C_ref = C_out.at[0] if first else C_out
    c = pl.program_id(1)
    n = NS * N_HEADS

    @pl.when(c == 0)
    def _():
        C_ref[...] = C0_ref[...]
        n_ref[...] = n0_ref[...]
        m_ref[...] = m0_ref[...]

    chains = lambda ref: ref[...].reshape(n, ref.shape[-2], ref.shape[-1])
    C = C_ref[...].reshape(n, DH, DH)
    nvec = n_ref[...].reshape(n, 1, DH)
    m_prev = chains(m_ref)[:, :, 0:1]
    rows = chains(rows_ref)
    d, b, ds0 = rows[:, 0:1, :], rows[:, 1:2, :], rows[:, 2:3, :]
    qb = chains(q_ref)

    m_inter = b + m_prev
    m_t = jnp.maximum(m_inter, d)
    f = jnp.exp(d - m_t)
    inter = jnp.exp(m_inter - m_t)
    qn = jnp.einsum('nod,ntd->not', nvec.astype(BF16), qb, preferred_element_type=F32)
    den = f * ds0 + inter * qn
    scale_t = 1.0 / jnp.maximum(jnp.abs(den), jnp.exp(-m_t))
    fi = jnp.swapaxes(jnp.concatenate([f * scale_t, inter * scale_t, jnp.zeros((n, 6, tl), F32)], axis=1), 1, 2)
    qC = jnp.einsum('ntd,nde->nte', qb, C.astype(BF16), preferred_element_type=F32)
    hh = fi[:, :, 0:1] * chains(nv_ref) + fi[:, :, 1:2] * qC

    m_new = m_t[:, :, tl - 1:tl]
    decay = jnp.exp(chains(bl_ref)[:, :, 0:1] + m_prev - m_new)
    scale = jnp.exp(chains(bc_ref)[:, :, 0:1] - m_new)
    kv = jnp.einsum('ntd,nte->nde', chains(kw_ref), chains(v_ref), preferred_element_type=F32)
    C_ref[...] = (decay * C + scale * kv).reshape(NS, N_HEADS, DH, DH)
    n_ref[...] = (decay * nvec + scale * chains(kn_ref)).reshape(NS, N_HEADS, DH)
    m_ref[...] = jnp.broadcast_to(m_new, (n, 1, 128)).reshape(NS, N_HEADS, 1, 128)
    if first:
        _fill_later_layers(C_out, c == NC - 1)

    mu = jnp.mean(hh, axis=-1, keepdims=True)
    hc = hh - mu
    hn = (hc * lax.rsqrt(jnp.mean(hc * hc, axis=-1, keepdims=True) + LN_EPS)).reshape(NS, N_HEADS, tl, DH)
    for h in range(N_HEADS):
        hs = slice(h * DH, (h + 1) * DH)
        ya_ref[:, :, hs] = (_sigmoid(o_ref[:, :, hs]) * hn[:, h] * ng_ref[:, hs]).astype(BF16)


def _mlstm_scan(pre, proj, C0, n0, m0, ng, SB, G, tl, layer, C_prev):
    nb, lp, _ = proj.shape
    NS = SB * G
    NI, NC = nb // NS, lp // tl
    six = lambda a: a.reshape((NI * SB, NC) + a.shape[1:])
    spec6 = lambda a: pl.BlockSpec((SB, 1) + a.shape[1:], lambda i, c: (i, c, 0, 0, 0, 0))
    seq4 = lambda i, c: (i, 0, 0, 0)
    st_in, st_out, st_shape, extra_in, extra_args = _layered_state(layer, nb, NS, C_prev)
    n_in = len(pre) + 5
    return pl.pallas_call(
        functools.partial(_mlstm_scan_body, NS=NS, tl=tl, NC=NC, first=layer == 0),
        grid=(NI, NC),
        in_specs=[spec6(a) for a in pre] + [
            pl.BlockSpec((NS, tl, D_MIX), lambda i, c: (i, c, (OFF_QKVO + 3 * D_MIX) // D_MIX)),
            st_in,
            pl.BlockSpec((NS, N_HEADS, DH), lambda i, c: (i, 0, 0)),
            pl.BlockSpec((NS, N_HEADS, 1, 128), seq4),
            pl.BlockSpec((1, D_MIX), lambda i, c: (0, 0))] + extra_in,
        out_specs=[pl.BlockSpec((NS, tl, D_MIX), lambda i, c: (i, c, 0)),
                   st_out,
                   pl.BlockSpec((NS, N_HEADS, DH), lambda i, c: (i, 0, 0)),
                   pl.BlockSpec((NS, N_HEADS, 1, 128), seq4)],
        out_shape=[jax.ShapeDtypeStruct((nb, lp, D_MIX), BF16),
                   st_shape,
                   jax.ShapeDtypeStruct((nb, N_HEADS, DH), F32),
                   jax.ShapeDtypeStruct((nb, N_HEADS, 1, 128), F32)],
        input_output_aliases={n_in: 1} if extra_in else {},
        compiler_params=pltpu.CompilerParams(dimension_semantics=("arbitrary", "arbitrary"),
                                             vmem_limit_bytes=VMEM_LIMIT),
        name=f"mlstm_scan_g{G}",
    )(*[six(a) for a in pre], proj, C0, n0, m0, ng, *extra_args)


def _unit_lower_inverse(n, rr, cc, tl):
    eye = (rr == cc).astype(F32)[None]
    p = jnp.where(((rr // INV_BLOCK) == (cc // INV_BLOCK))[None], n, 0.0)
    x = eye + p
    b = 2
    while b < INV_BLOCK:
        p = _bmm(p, p)
        x = x + _bmm(x, p)
        b *= 2
    b = INV_BLOCK
    while b < tl:
        off = jnp.where((((rr // (2 * b)) == (cc // (2 * b))) & ((rr // b) != (cc // b)))[None], n, 0.0)
        x = x + _bmm(x, _bmm(off, x))
        b *= 2
    return x


def _gdn_prep_body(x_ref, prev_ref, sa_ref, sb_ref, gc0_ref, cw_ref, alog_ref, dtb_ref,
                   u_ref, w_ref, qg_ref, kd_ref, qkm_ref, eg_ref, gcs_ref, xp_s, *, NB, G, tl, tlx, NCS, lv):
    cs = pl.program_id(1)
    lvl = lv - (NCS - 1) * tlx

    _conv_history(xp_s, gc0_ref, prev_ref, NCS)
    xp_s[:, HIST:HIST + tlx, :] = x_ref[...]
    qkv = _silu(_conv_taps(xp_s, cw_ref, CONV_C, tlx)).reshape(NB, RB, 3 * D_MIX)

    @pl.when(cs == NCS - 1)
    def _():
        gcs_ref[...] = xp_s[:, pl.ds(HIST + lvl - (CONV_C - 1), CONV_C - 1), :]

    beta_all = _sigmoid(sa_ref[...].reshape(NB, RB, 128))
    g_all = -jnp.exp(alog_ref[...]) * _softplus(sb_ref[...].reshape(NB, RB, 128) + dtb_ref[...])
    if lv < NCS * tlx:
        assert NCS == 1
        valid = (lax.broadcasted_iota(jnp.int32, (NB, RB, 128), 1) % tl) < lv
        beta_all = jnp.where(valid, beta_all, 0.0)
        g_all = jnp.where(valid, g_all, 0.0)
    rr, cc, incl = _block_masks(tl)
    diag = rr == cc
    gam = _seq_cumsum(g_all, incl, NB)
    glast = _seq_last(gam, NB, G, tl)
    gcol = _gate_cols(gam, NB, 4)
    bcol = _gate_cols(beta_all, NB, 4)
    egcol = _gate_cols(jnp.exp(gam), NB, 4)
    kdcol = _gate_cols(jnp.exp(_seq_rows(glast, NB, tl) - gam), NB, 4)

    q = _heads(qkv[:, :, 0:D_MIX], NB, DH)
    k = _heads(qkv[:, :, D_MIX:2 * D_MIX], NB, DH)
    v = _heads(qkv[:, :, 2 * D_MIX:3 * D_MIX], NB, DH)
    q = q * lax.rsqrt(jnp.sum(q * q, axis=-1, keepdims=True) + NORM_EPS) * (DH ** -0.5)
    k = k * lax.rsqrt(jnp.sum(k * k, axis=-1, keepdims=True) + NORM_EPS)
    qb, kb = q.astype(BF16), k.astype(BF16)

    dmat = jnp.exp(jnp.where(incl[None], gcol - _gate_rows(gam, NB, 4), NEG))
    nmat = jnp.where(diag[None], 0.0, -(bcol * _bmm_nt(kb, kb) * dmat))
    rhs = jnp.concatenate([bcol * v, (bcol * egcol) * k], axis=-1)
    sol = _bmm(_unit_lower_inverse(nmat, rr, cc, tl), rhs)
    qkm = (_bmm_nt(qb, kb) * dmat).astype(BF16).reshape(NB, N_HEADS, RB, RB)

    _put_chains(u_ref, sol[:, :, 0:DH], NB, G, tl)
    _put_chains(w_ref, sol[:, :, DH:2 * DH].astype(BF16), NB, G, tl)
    _put_chains(qg_ref, (q * egcol).astype(BF16), NB, G, tl)
    _put_chains(kd_ref, (k * kdcol).astype(BF16), NB, G, tl)
    _put_seq_scalars(eg_ref, jnp.exp(glast), NB, G, 4)
    for h in range(N_HEADS):
        for g in range(G):
            qkm_ref[:, g, h] = qkm[:, h, g * tl:(g + 1) * tl, g * tl:(g + 1) * tl]


def _gdn_prep(proj, gc0, cw, alog, dtb, NB, G, tl, lv):
    nb, lp, _ = proj.shape
    assert tl % INV_BLOCK == 0 and (tl // INV_BLOCK) & (tl // INV_BLOCK - 1) == 0
    gx, tlx, NI, NCS = _step_tiling(nb, lp, NB, G, tl)
    nbt = nb * lp // RB
    step = lambda i, c: (i * NCS + c, 0, 0, 0, 0)
    par = lambda i, c: (0, 0)
    chain = lambda last, dt: jax.ShapeDtypeStruct((nbt, G, N_HEADS, tl, last), dt)
    cspec = lambda last: pl.BlockSpec((NB, G, N_HEADS, tl, last), step)
    return pl.pallas_call(
        functools.partial(_gdn_prep_body, NB=NB, G=G, tl=tl, tlx=tlx, NCS=NCS, lv=lv),
        grid=(NI, NCS),
        in_specs=[pl.BlockSpec((gx, tlx, 3 * D_MIX), lambda i, c: (i, c, OFF_QKVC // (3 * D_MIX))),
                  pl.BlockSpec((gx, HIST, 3 * D_MIX),
                               lambda i, c: (i, jnp.maximum(c * (tlx // HIST) - 1, 0), OFF_QKVC // (3 * D_MIX))),
                  pl.BlockSpec((gx, tlx, 128), lambda i, c: (i, c, OFF_SA // 128)),
                  pl.BlockSpec((gx, tlx, 128), lambda i, c: (i, c, OFF_SB // 128)),
                  pl.BlockSpec((gx, HIST, 3 * D_MIX), lambda i, c: (i, 0, 0)),
                  pl.BlockSpec((8, 3 * D_MIX), par),
                  pl.BlockSpec((1, 128), par),
                  pl.BlockSpec((1, 128), par)],
        out_specs=[cspec(DH), cspec(DH), cspec(DH), cspec(DH), cspec(tl),
                   pl.BlockSpec((NB, G, N_HEADS, 1, 128), step),
                   pl.BlockSpec((gx, CONV_C - 1, 3 * D_MIX), lambda i, c: (i, 0, 0))],
        out_shape=[chain(DH, F32), chain(DH, BF16), chain(DH, BF16), chain(DH, BF16), chain(tl, BF16),
                   jax.ShapeDtypeStruct((nbt, G, N_HEADS, 1, 128), F32),
                   jax.ShapeDtypeStruct((nb, CONV_C - 1, 3 * D_MIX), F32)],
        scratch_shapes=[pltpu.VMEM((gx, HIST + tlx, 3 * D_MIX), F32)],
        compiler_params=pltpu.CompilerParams(dimension_semantics=("arbitrary", "arbitrary"),
                                             vmem_limit_bytes=VMEM_LIMIT),
        name=f"gdn_prep_g{G}",
    )(proj, proj, proj, proj, gc0, cw, alog, dtb)


def _gdn_scan_body(u_ref, w_ref, qg_ref, kd_ref, qkm_ref, eg_ref, z_ref, S0_ref, gng_ref, *rest,
                   NS, tl, NC, first):
    yc_ref, S_out = rest[-2:]
    S_ref = S_out.at[0] if first else S_out
    c = pl.program_id(1)
    n = NS * N_HEADS

    @pl.when(c == 0)
    def _():
        S_ref[...] = S0_ref[...]

    S = S_ref[...].reshape(n, DH, DH)
    Sb = S.astype(BF16)
    chains = lambda ref: ref[...].reshape(n, tl, ref.shape[-1])
    v_new = chains(u_ref) - jnp.einsum('ntd,nde->nte', chains(w_ref), Sb, preferred_element_type=F32)
    vnb = v_new.astype(BF16)
    o = (jnp.einsum('ntd,nde->nte', chains(qg_ref), Sb, preferred_element_type=F32)
         + jnp.einsum('nts,nse->nte', chains(qkm_ref), vnb, preferred_element_type=F32))
    eg = eg_ref[...].reshape(n, 1, 128)[:, :, 0:1]
    S_new = eg * S + jnp.einsum('ntd,nte->nde', chains(kd_ref), vnb, preferred_element_type=F32)
    S_ref[...] = S_new.reshape(NS, N_HEADS, DH, DH)
    if first:
        _fill_later_layers(S_out, c == NC - 1)

    on = (o * lax.rsqrt(jnp.mean(o * o, axis=-1, keepdims=True) + NORM_EPS) * gng_ref[...]).reshape(NS, N_HEADS, tl, DH)
    for h in range(N_HEADS):
        yc_ref[:, :, h * DH:(h + 1) * DH] = (on[:, h] * _silu(z_ref[:, :, h * DH:(h + 1) * DH])).astype(BF16)


def _gdn_scan(pre, proj, S0, gng, SB, G, tl, layer, S_prev):
    nb, lp, _ = proj.shape
    NS = SB * G
    NI, NC = nb // NS, lp // tl
    six = lambda a: a.reshape((NI * SB, NC) + a.shape[1:])
    cspec = lambda last: pl.BlockSpec((SB, 1, G, N_HEADS, tl, last), lambda i, c: (i, c, 0, 0, 0, 0))
    u, w, qg, kd, qkm, eg = (six(a) for a in pre)
    st_in, st_out, st_shape, extra_in, extra_args = _layered_state(layer, nb, NS, S_prev)
    return pl.pallas_call(
        functools.partial(_gdn_scan_body, NS=NS, tl=tl, NC=NC, first=layer == 0),
        grid=(NI, NC),
        in_specs=[cspec(DH), cspec(DH), cspec(DH), cspec(DH), cspec(tl),
                  pl.BlockSpec((SB, 1, G, N_HEADS, 1, 128), lambda i, c: (i, c, 0, 0, 0, 0)),
                  pl.BlockSpec((NS, tl, D_MIX), lambda i, c: (i, c, OFF_Z // D_MIX)),
                  st_in,
                  pl.BlockSpec((1, DH), lambda i, c: (0, 0))] + extra_in,
        out_specs=[pl.BlockSpec((NS, tl, D_MIX), lambda i, c: (i, c, 0)), st_out],
        out_shape=[jax.ShapeDtypeStruct((nb, lp, D_MIX), BF16), st_shape],
        input_output_aliases={9: 1} if extra_in else {},
        compiler_params=pltpu.CompilerParams(dimension_semantics=("arbitrary", "arbitrary"),
                                             vmem_limit_bytes=VMEM_LIMIT),
        name=f"gdn_scan_g{G}",
    )(u, w, qg, kd, qkm, eg, proj, S0, gng, *extra_args)


def _route(rl):
    lane = lax.broadcasted_iota(jnp.int32, rl.shape, 1).astype(F32)
    is_g = (lane >= N_EXPERTS) & (lane < N_EXPERTS + N_GROUPS)
    gl = jnp.where(is_g, rl, NEG)
    gmax = jnp.max(gl, axis=-1, keepdims=True)
    grp = jnp.min(jnp.where(gl == gmax, lane - N_EXPERTS, 4.0 * N_EXPERTS), axis=-1, keepdims=True)
    p_grp = 1.0 / jnp.sum(jnp.where(is_g, jnp.exp(gl - gmax), 0.0), axis=-1, keepdims=True)
    lo = grp * EXPERTS_PER_GROUP
    in_grp = (lane >= lo) & (lane < lo + EXPERTS_PER_GROUP)
    el = jnp.where(in_grp, rl, NEG)
    m1 = jnp.max(el, axis=-1, keepdims=True)
    i1 = jnp.min(jnp.where(el == m1, lane, 4.0 * N_EXPERTS), axis=-1, keepdims=True)
    el2 = jnp.where(lane == i1, NEG, el)
    m2 = jnp.max(el2, axis=-1, keepdims=True)
    i2 = jnp.min(jnp.where(el2 == m2, lane, 4.0 * N_EXPERTS), axis=-1, keepdims=True)
    e2 = jnp.exp(m2 - m1)
    w1 = p_grp / (1.0 + e2)
    w2 = p_grp * e2 / (1.0 + e2)
    return i1, i2, w1, w2


def _merge_body(ya_ref, yb_ref, yc_ref, ga_ref, gb_ref, gc_ref, x_ref, mod_ref, wa_ref, wb_ref, wc_ref, wo_ref,
                lg_ref, lb_ref, wrh_ref, wrl_ref, br_ref, cnt0_ref, x1_ref, rinfo_ref, cnt_ref, *, bt, lt):
    tm = bt * lt

    @pl.when((pl.program_id(0) == 0) & (pl.program_id(1) == 0))
    def _():
        cnt_ref[...] = cnt0_ref[...]

    def r2(ref):
        return ref[...].reshape(tm, ref.shape[-1])

    merged = (_sigmoid(r2(ga_ref)) * _dot(r2(ya_ref), wa_ref[0])
              + _sigmoid(r2(gb_ref)) * _dot(r2(yb_ref), wb_ref[0])
              + _sigmoid(r2(gc_ref)) * _dot(r2(yc_ref), wc_ref[0]))
    out = _dot(merged.astype(BF16), wo_ref[0])
    y = DN_ALPHA * x_ref[...] + (1.0 + mod_ref[:, 2:3, :]) * out.reshape(bt, lt, D_MODEL)
    x1 = _layer_norm(y, lg_ref[0], lb_ref[0])
    x1_ref[...] = x1
    u2 = (x1 * (1.0 + mod_ref[:, 4:5, :]) + mod_ref[:, 3:4, :]).reshape(tm, D_MODEL)
    hi = u2.astype(BF16)
    lo = (u2 - hi.astype(F32)).astype(BF16)
    rl = _dot(hi, wrh_ref[0]) + _dot(lo, wrh_ref[0]) + _dot(hi, wrl_ref[0]) + br_ref[0]
    i1, i2, w1, w2 = _route(rl)
    lane = lax.broadcasted_iota(jnp.int32, (tm, 128), 1).astype(F32)
    onehot = jnp.where((lane == i1) | (lane == i2), 1.0, 0.0)
    rr = lax.broadcasted_iota(jnp.int32, (tm, tm), 0)
    cc = lax.broadcasted_iota(jnp.int32, (tm, tm), 1)
    before = _dot((rr > cc).astype(BF16), onehot.astype(BF16)) + cnt_ref[0:1, :]
    rank1 = jnp.sum(jnp.where(lane == i1, before, 0.0), axis=-1, keepdims=True)
    rank2 = jnp.sum(jnp.where(lane == i2, before, 0.0), axis=-1, keepdims=True)
    cnt_ref[0:1, :] += jnp.sum(onehot, axis=0, keepdims=True)
    rinfo = jnp.zeros((tm, 128), F32)
    for k, val in enumerate((i1, i2, w1, w2, rank1, rank2)):
        rinfo = jnp.where(lane == k, val, rinfo)
    rinfo_ref[...] = rinfo.reshape(bt, lt, 128)


def _merge(ya, yb, yc, proj, x, mod, wa, wb, wc, wo, lg, lb, wrh, wrl, br, cnt0, layer, bt, lt):
    nb, lp, _ = x.shape
    tok = lambda i, t: (i, t, 0)
    wsp = lambda shape: pl.BlockSpec((1,) + shape, lambda i, t: (layer, 0, 0))
    g0 = OFF_GTS // D_MODEL
    return pl.pallas_call(
        functools.partial(_merge_body, bt=bt, lt=lt),
        grid=(nb // bt, lp // lt),
        in_specs=[pl.BlockSpec((bt, lt, D_MIX), tok),
                  pl.BlockSpec((bt, lt, D_MIX), tok),
                  pl.BlockSpec((bt, lt, D_MIX), tok),
                  pl.BlockSpec((bt, lt, D_MODEL), lambda i, t: (i, t, g0)),
                  pl.BlockSpec((bt, lt, D_MODEL), lambda i, t: (i, t, g0 + 1)),
                  pl.BlockSpec((bt, lt, D_MODEL), lambda i, t: (i, t, g0 + 2)),
                  pl.BlockSpec((bt, lt, D_MODEL), tok),
                  pl.BlockSpec((bt, 6, D_MODEL), lambda i, t: (i, 0, 0)),
                  wsp((D_MIX, D_MODEL)), wsp((D_MIX, D_MODEL)), wsp((D_MIX, D_MODEL)), wsp((D_MODEL, D_MODEL)),
                  wsp((1, D_MODEL)), wsp((1, D_MODEL)),
                  wsp((D_MODEL, 128)), wsp((D_MODEL, 128)), wsp((1, 128)),
                  pl.BlockSpec((SUBLANES, LANES), lambda i, t: (0, 0))],
        out_specs=[pl.BlockSpec((bt, lt, D_MODEL), tok),
                   pl.BlockSpec((bt, lt, 128), tok),
                   pl.BlockSpec((SUBLANES, LANES), lambda i, t: (0, 0))],
        out_shape=[jax.ShapeDtypeStruct((nb, lp, D_MODEL), F32),
                   jax.ShapeDtypeStruct((nb, lp, 128), F32),
                   jax.ShapeDtypeStruct((SUBLANES, LANES), F32)],
        compiler_params=pltpu.CompilerParams(dimension_semantics=("arbitrary", "arbitrary"),
                                             vmem_limit_bytes=VMEM_LIMIT),
        name=f"merge_l{layer}_b{bt}",
    )(ya, yb, yc, proj, proj, proj, x, mod, wa, wb, wc, wo, lg, lb, wrh, wrl, br, cnt0)


def _moe_plan(cnt, cnt_first, n_tok, mb):
    n_blocks = 2 * n_tok // mb + N_EXPERTS
    counts = cnt[0, :N_EXPERTS].astype(jnp.int32)
    nblk = (counts + mb - 1) // mb
    pend = jnp.cumsum(nblk)
    block_e = jnp.sum(pend[None, :] <= jnp.arange(n_blocks, dtype=jnp.int32)[:, None], axis=1)
    block_e = jnp.minimum(block_e, N_EXPERTS - 1).astype(jnp.int32)
    n_used = pend[N_EXPERTS - 1:].astype(jnp.int32)
    first_row = (pend - nblk) * mb
    per_block = mb // ZERO_ROWS
    hi = pend * per_block - 1
    lo = jnp.minimum((first_row + cnt_first[0, :N_EXPERTS].astype(jnp.int32)) // ZERO_ROWS, hi)
    ztail = jnp.concatenate([jnp.where(nblk > 0, lo, 0), jnp.where(nblk > 0, hi, -1), n_used * per_block])
    return first_row, block_e, n_used, ztail.astype(jnp.int32).reshape(1, 1, 2 * N_EXPERTS + 1), n_blocks


def _dispatch_rows(rinfo, first_row, tm):
    nb, lp, _ = rinfo.shape
    expert = rinfo[..., 0:2].astype(jnp.int32)
    rank = rinfo[..., 4:6].astype(jnp.int32)
    ids = jnp.arange(N_EXPERTS, dtype=jnp.int32)
    start = jnp.sum(jnp.where(expert[..., None] == ids, first_row, 0), axis=-1)
    return (start + rank).reshape(nb * lp // tm, 1, 2 * tm)


def _row_copy(src, dst, sem):
    return pltpu.make_async_copy(src, dst, sem)


def _dispatch_body(pos_ref, ztail_ref, x1_ref, mod_ref, *rest, bt, lt, n_blocks, mb, n_steps, first):
    disp_ref, u_scr, z_scr, sem, zsem = rest[-5:]
    tm = bt * lt

    @pl.when((pl.program_id(0) == 0) & (pl.program_id(1) == 0) & first)
    def _():
        z_scr[...] = jnp.zeros_like(z_scr)

        def zero_block(j):
            return _row_copy(z_scr, disp_ref.at[pl.ds(pl.multiple_of(j * ZERO_ROWS, ZERO_ROWS), ZERO_ROWS)], zsem)

        def for_zeroed_blocks(fn):
            for e in range(N_EXPERTS):
                lax.fori_loop(ztail_ref[0, 0, e], ztail_ref[0, 0, N_EXPERTS + e] + 1, fn, 0)
            lax.fori_loop(ztail_ref[0, 0, 2 * N_EXPERTS], n_blocks * (mb // ZERO_ROWS), fn, 0)

        for_zeroed_blocks(lambda j, c: (zero_block(j).start(), c)[1])
        for_zeroed_blocks(lambda j, c: (zero_block(j).wait(), c)[1])

    step = pl.program_id(0) * pl.num_programs(1) + pl.program_id(1)
    slot = step % 2
    u2 = x1_ref[...] * (1.0 + mod_ref[:, 4:5, :]) + mod_ref[:, 3:4, :]
    u_scr[slot] = u2.reshape(tm // SUBLANES, SUBLANES, D_MODEL)

    def issue(i, carry):
        for r in range(SUBLANES):
            for k in range(TOP_K):
                row = pos_ref[0, 0, TOP_K * (SUBLANES * i + r) + k]
                _row_copy(u_scr.at[slot, i, pl.ds(r, 1)], disp_ref.at[pl.ds(row, 1)],
                          sem.at[slot]).start(priority=k)
        return carry

    lax.fori_loop(0, tm // SUBLANES, issue, 0)

    def drain(which):
        for k in range(TOP_K):
            _row_copy(disp_ref.at[pl.ds(0, tm)], disp_ref.at[pl.ds(0, tm)], sem.at[which]).wait()

    @pl.when(step > 0)
    def _():
        drain(1 - slot)

    @pl.when(step == n_steps - 1)
    def _():
        drain(slot)


def _dispatch(x1, mod, pos, ztail, n_blocks, mb, bt, lt, buf):
    nb, lp, _ = x1.shape
    nt = lp // lt
    extra_in, extra_args = ([], []) if buf is None else ([pl.BlockSpec(memory_space=pl.ANY)], [buf])
    return pl.pallas_call(
        functools.partial(_dispatch_body, bt=bt, lt=lt, n_blocks=n_blocks, mb=mb, n_steps=(nb // bt) * nt,
                          first=buf is None),
        grid=(nb // bt, nt),
        in_specs=[pl.BlockSpec((1, 1, 2 * bt * lt), lambda i, t: (i * nt + t, 0, 0), memory_space=pltpu.SMEM),
                  pl.BlockSpec((1, 1, 2 * N_EXPERTS + 1), lambda i, t: (0, 0, 0), memory_space=pltpu.SMEM),
                  pl.BlockSpec((bt, lt, D_MODEL), lambda i, t: (i, t, 0)),
                  pl.BlockSpec((bt, 6, D_MODEL), lambda i, t: (i, 0, 0))] + extra_in,
        out_specs=pl.BlockSpec(memory_space=pl.ANY),
        out_shape=jax.ShapeDtypeStruct((n_blocks * mb, D_MODEL), F32),
        scratch_shapes=[pltpu.VMEM((2, bt * lt // SUBLANES, SUBLANES, D_MODEL), F32),
                        pltpu.VMEM((ZERO_ROWS, D_MODEL), F32),
                        pltpu.SemaphoreType.DMA((2,)), pltpu.SemaphoreType.DMA],
        compiler_params=pltpu.CompilerParams(dimension_semantics=("arbitrary", "arbitrary"),
                                             vmem_limit_bytes=VMEM_LIMIT),
        input_output_aliases={4: 0} if extra_in else {},
        name=f"dispatch_b{bt}",
    )(pos, ztail, x1, mod, *extra_args)


def _experts_body(be_ref, nu_ref, x_ref, wg_ref, wu_ref, wd_ref, o_ref):
    j = pl.program_id(0)

    @pl.when(j < nu_ref[0])
    def _():
        x = x_ref[...].astype(BF16)
        hb = _silu(_dot(x, wg_ref[0, 0].astype(BF16))) * _dot(x, wu_ref[0, 0].astype(BF16))
        o_ref[...] = _dot(hb.astype(BF16), wd_ref[0, 0].astype(BF16))

    @pl.when(j >= nu_ref[0])
    def _():
        o_ref[...] = jnp.zeros_like(o_ref)


def _experts(disp, block_e, n_used, wg, wu, wd, layer, mb):
    n_blocks = disp.shape[0] // mb
    wmap = lambda j, be, nu: (layer, be[j], 0, 0)
    return pl.pallas_call(
        _experts_body,
        grid_spec=pltpu.PrefetchScalarGridSpec(
            num_scalar_prefetch=2,
            grid=(n_blocks,),
            in_specs=[pl.BlockSpec((mb, D_MODEL), lambda j, be, nu: (jnp.minimum(j, nu[0] - 1), 0)),
                      pl.BlockSpec((1, 1, D_MODEL, D_EXPERT), wmap),
                      pl.BlockSpec((1, 1, D_MODEL, D_EXPERT), wmap),
                      pl.BlockSpec((1, 1, D_EXPERT, D_MODEL), wmap)],
            out_specs=pl.BlockSpec((mb, D_MODEL), lambda j, be, nu: (j, 0))),
        out_shape=jax.ShapeDtypeStruct(disp.shape, F32),
        compiler_params=pltpu.CompilerParams(dimension_semantics=("arbitrary",), vmem_limit_bytes=VMEM_LIMIT),
        name=f"experts_l{layer}_n{n_blocks}",
    )(block_e, n_used, disp, wg, wu, wd)


def _combine_body(pos_ref, pos_next_ref, x1_ref, mod_ref, rinfo_ref, eo_ref, lg_ref, lb_ref, x2_ref, r_scr, sem,
                  *, bt, lt, n_steps):
    tm = bt * lt
    step = pl.program_id(0) * pl.num_programs(1) + pl.program_id(1)
    slot = step % 2

    def gather(p_ref, to_slot):
        def issue(i, carry):
            for r in range(SUBLANES):
                for k in range(TOP_K):
                    row = p_ref[0, 0, TOP_K * (SUBLANES * i + r) + k]
                    _row_copy(eo_ref.at[pl.ds(row, 1)], r_scr.at[to_slot, k, i, pl.ds(r, 1)],
                              sem.at[to_slot]).start(priority=k)
            return carry

        lax.fori_loop(0, tm // SUBLANES, issue, 0)

    @pl.when(step == 0)
    def _():
        gather(pos_ref, 0)

    @pl.when(step + 1 < n_steps)
    def _():
        gather(pos_next_ref, 1 - slot)

    for k in range(TOP_K):
        _row_copy(eo_ref.at[pl.ds(0, tm)], eo_ref.at[pl.ds(0, tm)], sem.at[slot]).wait()

    rinfo = rinfo_ref[...].reshape(tm, 128)
    rows = r_scr[slot]
    moe = (rows[0].reshape(tm, D_MODEL) * rinfo[:, 2:3] + rows[1].reshape(tm, D_MODEL) * rinfo[:, 3:4])
    y = DN_ALPHA * x1_ref[...] + (1.0 + mod_ref[:, 5:6, :]) * moe.reshape(bt, lt, D_MODEL)
    x2_ref[...] = _layer_norm(y, lg_ref[0], lb_ref[0])


def _combine(x1, mod, rinfo, eo, pos, lg, lb, layer, bt, lt):
    nb, lp, _ = x1.shape
    nt = lp // lt
    tok = lambda i, t: (i, t, 0)
    n_steps = (nb // bt) * nt
    return pl.pallas_call(
        functools.partial(_combine_body, bt=bt, lt=lt, n_steps=n_steps),
        grid=(nb // bt, nt),
        in_specs=[pl.BlockSpec((1, 1, 2 * bt * lt), lambda i, t: (i * nt + t, 0, 0), memory_space=pltpu.SMEM),
                  pl.BlockSpec((1, 1, 2 * bt * lt), lambda i, t: (jnp.minimum(i * nt + t + 1, n_steps - 1), 0, 0),
                               memory_space=pltpu.SMEM),
                  pl.BlockSpec((bt, lt, D_MODEL), tok),
                  pl.BlockSpec((bt, 6, D_MODEL), lambda i, t: (i, 0, 0)),
                  pl.BlockSpec((bt, lt, 128), tok),
                  pl.BlockSpec(memory_space=pl.ANY),
                  pl.BlockSpec((1, 1, D_MODEL), lambda i, t: (layer, 0, 0)),
                  pl.BlockSpec((1, 1, D_MODEL), lambda i, t: (layer, 0, 0))],
        out_specs=pl.BlockSpec((bt, lt, D_MODEL), tok),
        out_shape=jax.ShapeDtypeStruct((nb, lp, D_MODEL), F32),
        scratch_shapes=[pltpu.VMEM((2, TOP_K, bt * lt // SUBLANES, SUBLANES, D_MODEL), F32),
                        pltpu.SemaphoreType.DMA((2,))],
        compiler_params=pltpu.CompilerParams(dimension_semantics=("arbitrary", "arbitrary"),
                                             vmem_limit_bytes=VMEM_LIMIT),
        name=f"combine_l{layer}_b{bt}",
    )(pos, pos, x1, mod, rinfo, eo, lg, lb)


def _hist(state):
    return jnp.pad(state, ((0, 0), (HIST - state.shape[1], 0), (0, 0)))


def _trunks(groups, p, mb):
    for g in groups:
        g.update(C=None, S=None, new={key: [] for key in ('n', 'm', 'conv', 'gconv')})
    n_tok = sum(g['x'].shape[0] * g['x'].shape[1] for g in groups)
    for l in range(DEPTH):
        cnt = jnp.zeros((SUBLANES, LANES), F32)
        for g in groups:
            st, mix, lv, nb = g['st'], g['mix'], g['lv'], g['x'].shape[0]
            proj, x = _inproj(g['x'], g['mod'][l], p['ln_in_g'], p['ln_in_b'], p['w_in_r'], p['b_in_r'], l, l == 0,
                              *g['ip_tile'])
            m0 = jnp.broadcast_to(st['m'][l][:, :, None, None], (nb, N_HEADS, 1, 128))
            yb, conv, *pre = _mlstm_prep(proj, _hist(st['conv'][l]), p['conv_b_w8'][l],
                                         mix['NB'], mix['G'], mix['tl'], lv)
            ya, g['C'], n, m = _mlstm_scan(pre, proj, st['C'], st['n'][l], m0, p['mlstm_norm_g'][l:l + 1],
                                           mix['SB'], mix['G'], mix['tl'], l, g['C'])
            *pre, gconv = _gdn_prep(proj, _hist(st['gconv'][l]), p['conv_c_w8'][l], p['alog_row'][l],
                                    p['dtb_row'][l], mix['NB'], mix['G'], mix['tl'], lv)
            yc, g['S'] = _gdn_scan(pre, proj, st['S'], p['gdn_norm_g'][l:l + 1], mix['SB'], mix['G'], mix['tl'],
                                   l, g['S'])
            g['x1'], g['rinfo'], cnt = _merge(ya, yb, yc, proj, x, g['mod'][l], p['w_br_a'], p['w_br_b'], p['w_br_c'],
                                              p['w_out'], p['ln1_g'], p['ln1_b'], p['wr_hi'], p['wr_lo'], p['br'],
                                              cnt, l, g['bt'], g['lt'])
            g['cnt'] = cnt
            for key, val in (('n', n), ('m', m[:, :, 0, 0]), ('conv', conv), ('gconv', gconv)):
                g['new'][key].append(val)
        first_row, block_e, n_used, ztail, n_blocks = _moe_plan(cnt, groups[0]['cnt'], n_tok, mb)
        disp = None
        for g in groups:
            g['pos'] = _dispatch_rows(g['rinfo'], first_row, g['bt'] * g['lt'])
            disp = _dispatch(g['x1'], g['mod'][l], g['pos'], ztail, n_blocks, mb, g['bt'], g['lt'], disp)
        eo = _experts(disp, block_e, n_used, p['exp_w_gate'], p['exp_w_up'], p['exp_w_down'], l, mb)
        for g in groups:
            g['x'] = _combine(g['x1'], g['mod'][l], g['rinfo'], eo, g['pos'], p['ln2_g'], p['ln2_b'], l,
                              g['bt'], g['lt'])
    return [(g['x'], dict({key: jnp.stack(val) for key, val in g['new'].items()}, C=g['C'], S=g['S']))
            for g in groups]


def kernel(x_prompt, x_sample, state_mlstm_C, state_mlstm_n, state_mlstm_m, state_conv, state_gdn_S, state_gdn_conv, c_prompt, c_sample, ln_in_g, ln_in_b, w_ada, b_ada, w_in, b_in, mlstm_norm_g, conv_b_w, conv_c_w, gdn_a_log, gdn_dt_bias, gdn_norm_g, w_br_a, w_br_b, w_br_c, w_out, ln1_g, ln1_b, router_g_w, router_g_b, router_e_w, router_e_b, exp_w_gate, exp_w_up, exp_w_down, ln2_g, ln2_b):
    nbp, lp, _ = x_prompt.shape
    nbs, ls, _ = x_sample.shape
    lsp = 8

    def regroup(a):
        out = jnp.zeros(a.shape[:-1] + (N_PROJ,), a.dtype)
        for src, end, dst in PROJ_SEGMENTS:
            out = out.at[..., dst:dst + end - src].set(a[..., src:end])
        return out

    wr = jnp.concatenate([router_e_w, router_g_w, jnp.zeros((DEPTH, D_MODEL, 128 - N_EXPERTS - N_GROUPS), F32)], axis=-1)
    wr_hi = wr.astype(BF16)
    lane_pad = lambda a: jnp.pad(a, ((0, 0), (4, 128 - 4 - N_HEADS)))[:, None, :]
    p = dict(
        ln_in_g=ln_in_g.reshape(1, D_MODEL), ln_in_b=ln_in_b.reshape(1, D_MODEL),
        w_in_r=_regroup_w_in(w_in), b_in_r=regroup(b_in).reshape(DEPTH, 1, N_PROJ),
        mlstm_norm_g=mlstm_norm_g,
        conv_b_w8=jnp.pad(conv_b_w, ((0, 0), (0, 8 - CONV_B), (0, 0))),
        conv_c_w8=jnp.pad(conv_c_w, ((0, 0), (0, 8 - CONV_C), (0, 0))),
        alog_row=lane_pad(gdn_a_log), dtb_row=lane_pad(gdn_dt_bias), gdn_norm_g=gdn_norm_g,
        w_br_a=w_br_a.astype(BF16), w_br_b=w_br_b.astype(BF16), w_br_c=w_br_c.astype(BF16),
        w_out=w_out.astype(BF16),
        ln1_g=ln1_g.reshape(DEPTH, 1, D_MODEL), ln1_b=ln1_b.reshape(DEPTH, 1, D_MODEL),
        wr_hi=wr_hi, wr_lo=(wr - wr_hi.astype(F32)).astype(BF16),
        br=jnp.concatenate([router_e_b, router_g_b, jnp.zeros((DEPTH, 128 - N_EXPERTS - N_GROUPS), F32)],
                           axis=-1).reshape(DEPTH, 1, 128),
        exp_w_gate=exp_w_gate, exp_w_up=exp_w_up, exp_w_down=exp_w_down,
        ln2_g=ln2_g.reshape(DEPTH, 1, D_MODEL), ln2_b=ln2_b.reshape(DEPTH, 1, D_MODEL),
    )

    mod = _ada(jnp.concatenate([c_prompt, c_sample], axis=0), w_ada, b_ada)
    mod = mod.reshape(DEPTH, nbp + nbs, 6, D_MODEL)

    zeros = lambda *s: jnp.zeros((DEPTH, nbp) + s, F32)
    st_p = {'C': zeros(N_HEADS, DH, DH), 'n': zeros(N_HEADS, DH), 'm': zeros(N_HEADS),
            'conv': zeros(CONV_B - 1, D_MIX), 'S': zeros(N_HEADS, DH, DH), 'gconv': zeros(CONV_C - 1, 3 * D_MIX)}
    prompt = dict(x=x_prompt, mod=mod[:, :nbp], st=st_p, bt=1, lt=512, lv=lp,
                  mix=dict(NB=8, G=1, tl=RB, SB=nbp), ip_tile=(1, 256))

    st_s = {'C': state_mlstm_C, 'n': state_mlstm_n, 'm': state_mlstm_m, 'conv': state_conv,
            'S': state_gdn_S, 'gconv': state_gdn_conv}
    xs = jnp.pad(x_sample, ((0, 0), (0, lsp - ls), (0, 0)))
    sample = dict(x=xs, mod=mod[:, nbp:], st=st_s, bt=64, lt=lsp, lv=ls,
                  mix=dict(NB=2, G=RB // lsp, tl=lsp, SB=1), ip_tile=(32, lsp))
    (y_p, sp), (y_s, ss) = _trunks([prompt, sample], p, mb=512)
    y_s = y_s[:, :ls]

    return (y_p, y_s, sp['C'], sp['n'], sp['m'], sp['conv'], sp['S'], sp['gconv'],
            ss['C'], ss['n'], ss['m'], ss['conv'], ss['S'], ss['gconv'])
```

```python
import functools

import jax
import jax.numpy as jnp
from jax import lax
from jax.experimental import pallas as pl
from jax.experimental.pallas import tpu as pltpu

F32 = jnp.float32
BF16 = jnp.bfloat16

LANES = 128
SUBLANES = 8

D_MODEL = 1024
DEPTH = 2
N_HEADS = 4
DH = 128
D_MIX = N_HEADS * DH
N_EXPERTS = 32
EXPERTS_PER_GROUP = 8
N_GROUPS = 4
TOP_K = 2
D_EXPERT = 256
ZERO_ROWS = 128
CONV_B = 3
CONV_C = 4
HIST = SUBLANES
RB = 64
INV_BLOCK = 8
DN_ALPHA = (2 * DEPTH) ** 0.25
LN_EPS = 1e-5
NORM_EPS = 1e-6
NEG = -1e30

OFF_QKVC = 0
OFF_BCH = 1536
OFF_GTS = 3072
OFF_QKVO = 6144
OFF_Z = 8192
OFF_SA = 8704
OFF_SB = 8832
N_PROJ = 8960
TN_PROJ = 1280


def _proj_segments():
    widths = (('qkvo', 4 * D_MIX), ('i', N_HEADS), ('f', N_HEADS), ('bch', 3 * D_MIX), ('qkvc', 3 * D_MIX),
              ('z', D_MIX), ('beta', N_HEADS), ('a', N_HEADS), ('gts', 3 * D_MODEL))
    dst = dict(qkvc=OFF_QKVC, bch=OFF_BCH, gts=OFF_GTS, qkvo=OFF_QKVO, z=OFF_Z,
               i=OFF_SA, beta=OFF_SA + N_HEADS, f=OFF_SB, a=OFF_SB + N_HEADS)
    segs, col = [], 0
    for name, width in widths:
        segs.append((col, col + width, dst[name]))
        col += width
    return tuple(segs)


PROJ_SEGMENTS = _proj_segments()

VMEM_LIMIT = 52 * 1024 * 1024


def _dot(a, b):
    return jnp.dot(a, b, preferred_element_type=F32)


def _split3(x):
    hi = x.astype(BF16)
    r = x - hi.astype(F32)
    mid = r.astype(BF16)
    lo = (r - mid.astype(F32)).astype(BF16)
    return hi, mid, lo


def _layer_norm(x, g, b):
    mu = jnp.mean(x, axis=-1, keepdims=True)
    xc = x - mu
    var = jnp.mean(xc * xc, axis=-1, keepdims=True)
    return xc * lax.rsqrt(var + LN_EPS) * g + b


def _sigmoid(x):
    return jax.nn.sigmoid(x)


def _silu(x):
    return x * _sigmoid(x)


def _log_sigmoid(x):
    return jnp.minimum(x, 0.0) - jnp.log1p(jnp.exp(-jnp.abs(x)))


def _softplus(x):
    return jnp.maximum(x, 0.0) + jnp.log1p(jnp.exp(-jnp.abs(x)))


def _ada_body(c_ref, w_ref, b_ref, o_ref):
    c = c_ref[...]
    s = _silu(c).astype(BF16)
    o_ref[0] = _dot(s, w_ref[0].astype(BF16)) + b_ref[0]


def _ada(c_all, w_ada, b_ada):
    nb = c_all.shape[0]
    return pl.pallas_call(
        _ada_body,
        grid=(DEPTH, 6),
        in_specs=[pl.BlockSpec((nb, D_MODEL), lambda l, j: (0, 0)),
                  pl.BlockSpec((1, D_MODEL, D_MODEL), lambda l, j: (l, 0, j)),
                  pl.BlockSpec((1, 1, D_MODEL), lambda l, j: (l, 0, j))],
        out_specs=pl.BlockSpec((1, nb, D_MODEL), lambda l, j: (l, 0, j)),
        out_shape=jax.ShapeDtypeStruct((DEPTH, nb, 6 * D_MODEL), F32),
        compiler_params=pltpu.CompilerParams(dimension_semantics=("arbitrary", "arbitrary"),
                                             vmem_limit_bytes=VMEM_LIMIT),
        name="ada",
    )(c_all, w_ada, b_ada.reshape(DEPTH, 1, 6 * D_MODEL))


def _regroup_body(w_ref, o_ref):
    o_ref[...] = jnp.zeros_like(o_ref)
    for src, end, dst in PROJ_SEGMENTS:
        o_ref[0, :, dst:dst + end - src] = w_ref[0, :, src:end].astype(BF16)


def _regroup_w_in(w_in):
    n_in = w_in.shape[-1]
    rows = 256
    return pl.pallas_call(
        _regroup_body,
        grid=(DEPTH, D_MODEL // rows),
        in_specs=[pl.BlockSpec((1, rows, n_in), lambda l, i: (l, i, 0))],
        out_specs=pl.BlockSpec((1, rows, N_PROJ), lambda l, i: (l, i, 0)),
        out_shape=jax.ShapeDtypeStruct((DEPTH, D_MODEL, N_PROJ), BF16),
        compiler_params=pltpu.CompilerParams(dimension_semantics=("arbitrary", "arbitrary"),
                                             vmem_limit_bytes=VMEM_LIMIT),
        name="regroup_w_in",
    )(w_in)


def _inproj_body(x_ref, mod_ref, g_ref, b_ref, w_ref, bias_ref, proj_ref, *rest, apply_ln, bt, lt):
    x = x_ref[...]
    if apply_ln:
        x = _layer_norm(x, g_ref[...], b_ref[...])
        rest[0][...] = x
    u = (x * (1.0 + mod_ref[:, 1:2, :]) + mod_ref[:, 0:1, :]).reshape(bt * lt, D_MODEL).astype(BF16)
    for j in range(N_PROJ // TN_PROJ):
        cs = slice(j * TN_PROJ, (j + 1) * TN_PROJ)
        proj_ref[:, :, cs] = (_dot(u, w_ref[0, :, cs]) + bias_ref[0, :, cs]).reshape(bt, lt, TN_PROJ)


def _inproj(x, mod, ln_g, ln_b, w_r, b_r, layer, apply_ln, bt, lt):
    nb, lp, _ = x.shape
    tok = lambda i, t: (i, t, 0)
    out_shape = [jax.ShapeDtypeStruct((nb, lp, N_PROJ), F32)]
    out_specs = [pl.BlockSpec((bt, lt, N_PROJ), tok)]
    if apply_ln:
        out_shape.append(jax.ShapeDtypeStruct((nb, lp, D_MODEL), F32))
        out_specs.append(pl.BlockSpec((bt, lt, D_MODEL), tok))
    res = pl.pallas_call(
        functools.partial(_inproj_body, apply_ln=apply_ln, bt=bt, lt=lt),
        grid=(nb // bt, lp // lt),
        in_specs=[pl.BlockSpec((bt, lt, D_MODEL), tok),
                  pl.BlockSpec((bt, 6, D_MODEL), lambda i, t: (i, 0, 0)),
                  pl.BlockSpec((1, D_MODEL), lambda i, t: (0, 0)),
                  pl.BlockSpec((1, D_MODEL), lambda i, t: (0, 0)),
                  pl.BlockSpec((1, D_MODEL, N_PROJ), lambda i, t: (layer, 0, 0), pipeline_mode=pl.Buffered(1)),
                  pl.BlockSpec((1, 1, N_PROJ), lambda i, t: (layer, 0, 0))],
        out_specs=out_specs,
        out_shape=out_shape,
        compiler_params=pltpu.CompilerParams(dimension_semantics=("arbitrary", "arbitrary"),
                                             vmem_limit_bytes=VMEM_LIMIT),
        name=f"inproj_l{layer}_b{bt}",
    )(x, mod, ln_g, ln_b, w_r, b_r)
    return (res[0], res[1]) if apply_ln else (res[0], x)


def _conv_taps(xp_s, w_ref, width, tl):
    acc = None
    for j in range(width):
        tap = xp_s[:, pl.ds(HIST - (width - 1) + j, tl), :] * w_ref[j:j + 1, :].reshape(1, 1, -1)
        acc = tap if acc is None else acc + tap
    return acc


def _conv_history(xp_s, hist_ref, prev_ref, n_steps):
    if n_steps > 1:
        @pl.when(pl.program_id(1) == 0)
        def _():
            xp_s[:, 0:HIST, :] = hist_ref[...]

        @pl.when(pl.program_id(1) > 0)
        def _():
            xp_s[:, 0:HIST, :] = prev_ref[...]
    else:
        xp_s[:, 0:HIST, :] = hist_ref[...]


def _heads(x, nb, width):
    return jnp.stack([x[:, :, h * width:(h + 1) * width] for h in range(N_HEADS)],
                     axis=1).reshape(nb * N_HEADS, RB, width)


def _gate_cols(x, nb, lane0):
    return jnp.stack([x[:, :, lane0 + h:lane0 + h + 1] for h in range(N_HEADS)],
                     axis=1).reshape(nb * N_HEADS, RB, 1)


def _gate_rows(x, nb, lane0):
    xt = jnp.swapaxes(x, 1, 2)
    return jnp.stack([xt[:, lane0 + h:lane0 + h + 1, :] for h in range(N_HEADS)],
                     axis=1).reshape(nb * N_HEADS, 1, RB)


def _bmm(a, b):
    return jnp.einsum('nts,nsu->ntu', a.astype(BF16), b.astype(BF16), preferred_element_type=F32)


def _bmm_nt(a, b):
    return jnp.einsum('ntd,nsd->nts', a, b, preferred_element_type=F32)


def _block_masks(tl):
    rr = lax.broadcasted_iota(jnp.int32, (RB, RB), 0)
    cc = lax.broadcasted_iota(jnp.int32, (RB, RB), 1)
    incl = rr >= cc
    if tl < RB:
        incl = incl & ((rr // tl) == (cc // tl))
    return rr, cc, incl


def _seq_cumsum(x, incl, nb):
    tril = jnp.broadcast_to(incl.astype(BF16)[None], (nb, RB, RB))
    hi, mid, lo = _split3(x)
    return _bmm(tril, hi) + _bmm(tril, mid) + _bmm(tril, lo)


def _seq_last(x, nb, G, tl):
    return x.reshape(nb * G, tl, 128)[:, tl - 1:tl, :]


def _seq_rows(x3, nb, tl):
    return jnp.broadcast_to(x3, (x3.shape[0], tl, 128)).reshape(nb, RB, 128)


def _put_chains(ref, val, nb, G, tl):
    val4 = val.reshape(nb, N_HEADS, RB, val.shape[-1])
    for h in range(N_HEADS):
        ref[:, :, h] = val4[:, h].reshape(nb, G, tl, val.shape[-1])


def _put_seq_scalars(ref, x3, nb, G, lane0):
    x4 = x3.reshape(nb, G, 1, 128)
    for h in range(N_HEADS):
        ref[:, :, h] = jnp.broadcast_to(x4[:, :, :, lane0 + h:lane0 + h + 1], (nb, G, 1, 128))


def _step_tiling(nb, lp, NB, G, tl):
    assert G * tl == RB
    if G == 1:
        gx, tlx = 1, NB * RB
    else:
        assert lp == tl
        gx, tlx = NB * G, tl
    return gx, tlx, nb // gx, lp // tlx


def _layered_state(layer, nb, NS, prev):
    st_in = pl.BlockSpec((None, NS, N_HEADS, DH, DH), lambda i, c: (layer, i, 0, 0, 0))
    shape = jax.ShapeDtypeStruct((DEPTH, nb, N_HEADS, DH, DH), F32)
    if layer == 0:
        return st_in, pl.BlockSpec((DEPTH, NS, N_HEADS, DH, DH), lambda i, c: (0, i, 0, 0, 0)), shape, [], []
    return st_in, st_in, shape, [pl.BlockSpec(memory_space=pl.ANY)], [prev]


def _fill_later_layers(ref, when):
    @pl.when(when)
    def _():
        for l in range(1, DEPTH):
            ref[l] = ref[0]


def _mlstm_prep_body(qkvo_ref, bch_ref, prev_ref, sa_ref, sb_ref, cv0_ref, cw_ref,
                     yb_ref, cv_ref, nv_ref, q_ref, kw_ref, v_ref, rows_ref, bl_ref, bc_ref, kn_ref, xp_s,
                     *, NB, G, tl, tlx, NCS, lv):
    cs = pl.program_id(1)
    n = NB * N_HEADS
    lvl = lv - (NCS - 1) * tlx

    if NCS > 1:
        @pl.when(cs == 0)
        def _():
            xp_s[:, 0:HIST, :] = cv0_ref[...]

        @pl.when(cs > 0)
        def _():
            xp_s[:, 0:HIST, :] = prev_ref[:, :, D_MIX:2 * D_MIX] * prev_ref[:, :, 2 * D_MIX:3 * D_MIX]
    else:
        xp_s[:, 0:HIST, :] = cv0_ref[...]
    xp_s[:, HIST:HIST + tlx, :] = bch_ref[:, :, D_MIX:2 * D_MIX] * bch_ref[:, :, 2 * D_MIX:3 * D_MIX]
    yb_ref[...] = (bch_ref[:, :, 0:D_MIX] * _conv_taps(xp_s, cw_ref, CONV_B, tlx)).astype(BF16)

    @pl.when(cs == NCS - 1)
    def _():
        cv_ref[...] = xp_s[:, pl.ds(HIST + lvl - (CONV_B - 1), CONV_B - 1), :]

    i_all = sa_ref[...].reshape(NB, RB, 128)
    f_all = _log_sigmoid(sb_ref[...].reshape(NB, RB, 128))
    if lv < NCS * tlx:
        assert NCS == 1
        valid = (lax.broadcasted_iota(jnp.int32, (NB, RB, 128), 1) % tl) < lv
        i_all = jnp.where(valid, i_all, NEG)
        f_all = jnp.where(valid, f_all, 0.0)
    _, _, incl = _block_masks(tl)
    bcum = _seq_cumsum(f_all, incl, NB)
    blast = _seq_last(bcum, NB, G, tl)
    val = _seq_rows(blast, NB, tl) - bcum + i_all
    bmax = jnp.max(val.reshape(NB * G, tl, 128), axis=1, keepdims=True)
    wk0 = jnp.exp(val - _seq_rows(bmax, NB, tl))

    qkvo = qkvo_ref[...].reshape(NB, RB, 4 * D_MIX)
    q = _heads(qkvo[:, :, 0:D_MIX], NB, DH)
    k = _heads(qkvo[:, :, D_MIX:2 * D_MIX], NB, DH) * (DH ** -0.5)
    v = _heads(qkvo[:, :, 2 * D_MIX:3 * D_MIX], NB, DH)
    qb, kb, vb = q.astype(BF16), k.astype(BF16), v.astype(BF16)

    b_col = _gate_cols(bcum, NB, 0)
    dlog = jnp.where(incl[None], b_col - _gate_rows(bcum, NB, 0) + _gate_rows(i_all, NB, 0), NEG)
    d = jnp.max(dlog, axis=-1, keepdims=True)
    s0 = _bmm_nt(qb, kb) * jnp.exp(dlog - d)
    kw0 = k * _gate_cols(wk0, NB, 0)

    _put_chains(nv_ref, _bmm(s0, vb), NB, G, tl)
    _put_chains(q_ref, qb, NB, G, tl)
    _put_chains(kw_ref, kw0.astype(BF16), NB, G, tl)
    _put_chains(v_ref, vb, NB, G, tl)
    rows = jnp.swapaxes(jnp.concatenate([d, b_col, jnp.sum(s0, axis=-1, keepdims=True),
                                         jnp.zeros((n, RB, 5), F32)], axis=-1), 1, 2).reshape(NB, N_HEADS, 8, RB)
    for h in range(N_HEADS):
        for g in range(G):
            rows_ref[:, g, h] = rows[:, h, :, g * tl:(g + 1) * tl]
    _put_seq_scalars(bl_ref, blast, NB, G, 0)
    _put_seq_scalars(bc_ref, bmax, NB, G, 0)
    kn = jnp.sum(kw0.reshape(n * G, tl, DH), axis=1, keepdims=True).reshape(NB, N_HEADS, G, 1, DH)
    for h in range(N_HEADS):
        kn_ref[:, :, h] = kn[:, h]


def _mlstm_prep(proj, cv0, cw, NB, G, tl, lv):
    nb, lp, _ = proj.shape
    gx, tlx, NI, NCS = _step_tiling(nb, lp, NB, G, tl)
    nbt = nb * lp // RB
    step = lambda i, c: (i * NCS + c, 0, 0, 0, 0)
    chain = lambda last, dt: jax.ShapeDtypeStruct((nbt, G, N_HEADS, tl, last), dt)
    cspec = lambda last: pl.BlockSpec((NB, G, N_HEADS, tl, last), step)
    scal = jax.ShapeDtypeStruct((nbt, G, N_HEADS, 1, 128), F32)
    sspec = pl.BlockSpec((NB, G, N_HEADS, 1, 128), step)
    bch = OFF_BCH // (3 * D_MIX)
    return pl.pallas_call(
        functools.partial(_mlstm_prep_body, NB=NB, G=G, tl=tl, tlx=tlx, NCS=NCS, lv=lv),
        grid=(NI, NCS),
        in_specs=[pl.BlockSpec((gx, tlx, 4 * D_MIX), lambda i, c: (i, c, OFF_QKVO // (4 * D_MIX))),
                  pl.BlockSpec((gx, tlx, 3 * D_MIX), lambda i, c: (i, c, bch)),
                  pl.BlockSpec((gx, HIST, 3 * D_MIX), lambda i, c: (i, jnp.maximum(c * (tlx // HIST) - 1, 0), bch)),
                  pl.BlockSpec((gx, tlx, 128), lambda i, c: (i, c, OFF_SA // 128)),
                  pl.BlockSpec((gx, tlx, 128), lambda i, c: (i, c, OFF_SB // 128)),
                  pl.BlockSpec((gx, HIST, D_MIX), lambda i, c: (i, 0, 0)),
                  pl.BlockSpec((8, D_MIX), lambda i, c: (0, 0))],
        out_specs=[pl.BlockSpec((gx, tlx, D_MIX), lambda i, c: (i, c, 0)),
                   pl.BlockSpec((gx, CONV_B - 1, D_MIX), lambda i, c: (i, 0, 0)),
                   cspec(DH), cspec(DH), cspec(DH), cspec(DH),
                   pl.BlockSpec((NB, G, N_HEADS, 8, tl), step), sspec, sspec, sspec],
        out_shape=[jax.ShapeDtypeStruct((nb, lp, D_MIX), BF16),
                   jax.ShapeDtypeStruct((nb, CONV_B - 1, D_MIX), F32),
                   chain(DH, F32), chain(DH, BF16), chain(DH, BF16), chain(DH, BF16),
                   jax.ShapeDtypeStruct((nbt, G, N_HEADS, 8, tl), F32),
                   scal, scal, scal],
        scratch_shapes=[pltpu.VMEM((gx, HIST + tlx, D_MIX), F32)],
        compiler_params=pltpu.CompilerParams(dimension_semantics=("arbitrary", "arbitrary"),
                                             vmem_limit_bytes=VMEM_LIMIT),
        name=f"mlstm_prep_g{G}",
    )(proj, proj, proj, proj, proj, cv0, cw)


def _mlstm_scan_body(nv_ref, q_ref, kw_ref, v_ref, rows_ref, bl_ref, bc_ref, kn_ref, o_ref, C0_ref, n0_ref, m0_ref,
                     ng_ref, *rest, NS, tl, NC, first):
    ya_ref, C_out, n_ref, m_ref = rest[-4:]
    C_ref = C_out.at[0] if first else C_out
    c = pl.program_id(1)
    n = NS * N_HEADS

    @pl.when(c == 0)
    def _():
        C_ref[...] = C0_ref[...]
        n_ref[...] = n0_ref[...]
        m_ref[...] = m0_ref[...]

    chains = lambda ref: ref[...].reshape(n, ref.shape[-2], ref.shape[-1])
    C = C_ref[...].reshape(n, DH, DH)
    nvec = n_ref[...].reshape(n, 1, DH)
    m_prev = chains(m_ref)[:, :, 0:1]
    rows = chains(rows_ref)
    d, b, ds0 = rows[:, 0:1, :], rows[:, 1:2, :], rows[:, 2:3, :]
    qb = chains(q_ref)

    m_inter = b + m_prev
    m_t = jnp.maximum(m_inter, d)
    f = jnp.exp(d - m_t)
    inter = jnp.exp(m_inter - m_t)
    qn = jnp.einsum('nod,ntd->not', nvec.astype(BF16), qb, preferred_element_type=F32)
    den = f * ds0 + inter * qn
    scale_t = 1.0 / jnp.maximum(jnp.abs(den), jnp.exp(-m_t))
    fi = jnp.swapaxes(jnp.concatenate([f * scale_t, inter * scale_t, jnp.zeros((n, 6, tl), F32)], axis=1), 1, 2)
    qC = jnp.einsum('ntd,nde->nte', qb, C.astype(BF16), preferred_element_type=F32)
    hh = fi[:, :, 0:1] * chains(nv_ref) + fi[:, :, 1:2] * qC

    m_new = m_t[:, :, tl - 1:tl]
    decay = jnp.exp(chains(bl_ref)[:, :, 0:1] + m_prev - m_new)
    scale = jnp.exp(chains(bc_ref)[:, :, 0:1] - m_new)
    kv = jnp.einsum('ntd,nte->nde', chains(kw_ref), chains(v_ref), preferred_element_type=F32)
    C_ref[...] = (decay * C + scale * kv).reshape(NS, N_HEADS, DH, DH)
    n_ref[...] = (decay * nvec + scale * chains(kn_ref)).reshape(NS, N_HEADS, DH)
    m_ref[...] = jnp.broadcast_to(m_new, (n, 1, 128)).reshape(NS, N_HEADS, 1, 128)
    if first:
        _fill_later_layers(C_out, c == NC - 1)

    mu = jnp.mean(hh, axis=-1, keepdims=True)
    hc = hh - mu
    hn = (hc * lax.rsqrt(jnp.mean(hc * hc, axis=-1, keepdims=True) + LN_EPS)).reshape(NS, N_HEADS, tl, DH)
    for h in range(N_HEADS):
        hs = slice(h * DH, (h + 1) * DH)
        ya_ref[:, :, hs] = (_sigmoid(o_ref[:, :, hs]) * hn[:, h] * ng_ref[:, hs]).astype(BF16)


def _mlstm_scan(pre, proj, C0, n0, m0, ng, SB, G, tl, layer, C_prev):
    nb, lp, _ = proj.shape
    NS = SB * G
    NI, NC = nb // NS, lp // tl
    six = lambda a: a.reshape((NI * SB, NC) + a.shape[1:])
    spec6 = lambda a: pl.BlockSpec((SB, 1) + a.shape[1:], lambda i, c: (i, c, 0, 0, 0, 0))
    seq4 = lambda i, c: (i, 0, 0, 0)
    st_in, st_out, st_shape, extra_in, extra_args = _layered_state(layer, nb, NS, C_prev)
    n_in = len(pre) + 5
    return pl.pallas_call(
        functools.partial(_mlstm_scan_body, NS=NS, tl=tl, NC=NC, first=layer == 0),
        grid=(NI, NC),
        in_specs=[spec6(a) for a in pre] + [
            pl.BlockSpec((NS, tl, D_MIX), lambda i, c: (i, c, (OFF_QKVO + 3 * D_MIX) // D_MIX)),
            st_in,
            pl.BlockSpec((NS, N_HEADS, DH), lambda i, c: (i, 0, 0)),
            pl.BlockSpec((NS, N_HEADS, 1, 128), seq4),
            pl.BlockSpec((1, D_MIX), lambda i, c: (0, 0))] + extra_in,
        out_specs=[pl.BlockSpec((NS, tl, D_MIX), lambda i, c: (i, c, 0)),
                   st_out,
                   pl.BlockSpec((NS, N_HEADS, DH), lambda i, c: (i, 0, 0)),
                   pl.BlockSpec((NS, N_HEADS, 1, 128), seq4)],
        out_shape=[jax.ShapeDtypeStruct((nb, lp, D_MIX), BF16),
                   st_shape,
                   jax.ShapeDtypeStruct((nb, N_HEADS, DH), F32),
                   jax.ShapeDtypeStruct((nb, N_HEADS, 1, 128), F32)],
        input_output_aliases={n_in: 1} if extra_in else {},
        compiler_params=pltpu.CompilerParams(dimension_semantics=("arbitrary", "arbitrary"),
                                             vmem_limit_bytes=VMEM_LIMIT),
        name=f"mlstm_scan_g{G}",
    )(*[six(a) for a in pre], proj, C0, n0, m0, ng, *extra_args)


def _unit_lower_inverse(n, rr, cc, tl):
    eye = (rr == cc).astype(F32)[None]
    p = jnp.where(((rr // INV_BLOCK) == (cc // INV_BLOCK))[None], n, 0.0)
    x = eye + p
    b = 2
    while b < INV_BLOCK:
        p = _bmm(p, p)
        x = x + _bmm(x, p)
        b *= 2
    b = INV_BLOCK
    while b < tl:
        off = jnp.where((((rr // (2 * b)) == (cc // (2 * b))) & ((rr // b) != (cc // b)))[None], n, 0.0)
        x = x + _bmm(x, _bmm(off, x))
        b *= 2
    return x


def _gdn_prep_body(x_ref, prev_ref, sa_ref, sb_ref, gc0_ref, cw_ref, alog_ref, dtb_ref,
                   u_ref, w_ref, qg_ref, kd_ref, qkm_ref, eg_ref, gcs_ref, xp_s, *, NB, G, tl, tlx, NCS, lv):
    cs = pl.program_id(1)
    lvl = lv - (NCS - 1) * tlx

    _conv_history(xp_s, gc0_ref, prev_ref, NCS)
    xp_s[:, HIST:HIST + tlx, :] = x_ref[...]
    qkv = _silu(_conv_taps(xp_s, cw_ref, CONV_C, tlx)).reshape(NB, RB, 3 * D_MIX)

    @pl.when(cs == NCS - 1)
    def _():
        gcs_ref[...] = xp_s[:, pl.ds(HIST + lvl - (CONV_C - 1), CONV_C - 1), :]

    beta_all = _sigmoid(sa_ref[...].reshape(NB, RB, 128))
    g_all = -jnp.exp(alog_ref[...]) * _softplus(sb_ref[...].reshape(NB, RB, 128) + dtb_ref[...])
    if lv < NCS * tlx:
        assert NCS == 1
        valid = (lax.broadcasted_iota(jnp.int32, (NB, RB, 128), 1) % tl) < lv
        beta_all = jnp.where(valid, beta_all, 0.0)
        g_all = jnp.where(valid, g_all, 0.0)
    rr, cc, incl = _block_masks(tl)
    diag = rr == cc
    gam = _seq_cumsum(g_all, incl, NB)
    glast = _seq_last(gam, NB, G, tl)
    gcol = _gate_cols(gam, NB, 4)
    bcol = _gate_cols(beta_all, NB, 4)
    egcol = _gate_cols(jnp.exp(gam), NB, 4)
    kdcol = _gate_cols(jnp.exp(_seq_rows(glast, NB, tl) - gam), NB, 4)

    q = _heads(qkv[:, :, 0:D_MIX], NB, DH)
    k = _heads(qkv[:, :, D_MIX:2 * D_MIX], NB, DH)
    v = _heads(qkv[:, :, 2 * D_MIX:3 * D_MIX], NB, DH)
    q = q * lax.rsqrt(jnp.sum(q * q, axis=-1, keepdims=True) + NORM_EPS) * (DH ** -0.5)
    k = k * lax.rsqrt(jnp.sum(k * k, axis=-1, keepdims=True) + NORM_EPS)
    qb, kb = q.astype(BF16), k.astype(BF16)

    dmat = jnp.exp(jnp.where(incl[None], gcol - _gate_rows(gam, NB, 4), NEG))
    nmat = jnp.where(diag[None], 0.0, -(bcol * _bmm_nt(kb, kb) * dmat))
    rhs = jnp.concatenate([bcol * v, (bcol * egcol) * k], axis=-1)
    sol = _bmm(_unit_lower_inverse(nmat, rr, cc, tl), rhs)
    qkm = (_bmm_nt(qb, kb) * dmat).astype(BF16).reshape(NB, N_HEADS, RB, RB)

    _put_chains(u_ref, sol[:, :, 0:DH], NB, G, tl)
    _put_chains(w_ref, sol[:, :, DH:2 * DH].astype(BF16), NB, G, tl)
    _put_chains(qg_ref, (q * egcol).astype(BF16), NB, G, tl)
    _put_chains(kd_ref, (k * kdcol).astype(BF16), NB, G, tl)
    _put_seq_scalars(eg_ref, jnp.exp(glast), NB, G, 4)
    for h in range(N_HEADS):
        for g in range(G):
            qkm_ref[:, g, h] = qkm[:, h, g * tl:(g + 1) * tl, g * tl:(g + 1) * tl]


def _gdn_prep(proj, gc0, cw, alog, dtb, NB, G, tl, lv):
    nb, lp, _ = proj.shape
    assert tl % INV_BLOCK == 0 and (tl // INV_BLOCK) & (tl // INV_BLOCK - 1) == 0
    gx, tlx, NI, NCS = _step_tiling(nb, lp, NB, G, tl)
    nbt = nb * lp // RB
    step = lambda i, c: (i * NCS + c, 0, 0, 0, 0)
    par = lambda i, c: (0, 0)
    chain = lambda last, dt: jax.ShapeDtypeStruct((nbt, G, N_HEADS, tl, last), dt)
    cspec = lambda last: pl.BlockSpec((NB, G, N_HEADS, tl, last), step)
    return pl.pallas_call(
        functools.partial(_gdn_prep_body, NB=NB, G=G, tl=tl, tlx=tlx, NCS=NCS, lv=lv),
        grid=(NI, NCS),
        in_specs=[pl.BlockSpec((gx, tlx, 3 * D_MIX), lambda i, c: (i, c, OFF_QKVC // (3 * D_MIX))),
                  pl.BlockSpec((gx, HIST, 3 * D_MIX),
                               lambda i, c: (i, jnp.maximum(c * (tlx // HIST) - 1, 0), OFF_QKVC // (3 * D_MIX))),
                  pl.BlockSpec((gx, tlx, 128), lambda i, c: (i, c, OFF_SA // 128)),
                  pl.BlockSpec((gx, tlx, 128), lambda i, c: (i, c, OFF_SB // 128)),
                  pl.BlockSpec((gx, HIST, 3 * D_MIX), lambda i, c: (i, 0, 0)),
                  pl.BlockSpec((8, 3 * D_MIX), par),
                  pl.BlockSpec((1, 128), par),
                  pl.BlockSpec((1, 128), par)],
        out_specs=[cspec(DH), cspec(DH), cspec(DH), cspec(DH), cspec(tl),
                   pl.BlockSpec((NB, G, N_HEADS, 1, 128), step),
                   pl.BlockSpec((gx, CONV_C - 1, 3 * D_MIX), lambda i, c: (i, 0, 0))],
        out_shape=[chain(DH, F32), chain(DH, BF16), chain(DH, BF16), chain(DH, BF16), chain(tl, BF16),
                   jax.ShapeDtypeStruct((nbt, G, N_HEADS, 1, 128), F32),
                   jax.ShapeDtypeStruct((nb, CONV_C - 1, 3 * D_MIX), F32)],
        scratch_shapes=[pltpu.VMEM((gx, HIST + tlx, 3 * D_MIX), F32)],
        compiler_params=pltpu.CompilerParams(dimension_semantics=("arbitrary", "arbitrary"),
                                             vmem_limit_bytes=VMEM_LIMIT),
        name=f"gdn_prep_g{G}",
    )(proj, proj, proj, proj, gc0, cw, alog, dtb)


def _gdn_scan_body(u_ref, w_ref, qg_ref, kd_ref, qkm_ref, eg_ref, z_ref, S0_ref, gng_ref, *rest,
                   NS, tl, NC, first):
    yc_ref, S_out = rest[-2:]
    S_ref = S_out.at[0] if first else S_out
    c = pl.program_id(1)
    n = NS * N_HEADS

    @pl.when(c == 0)
    def _():
        S_ref[...] = S0_ref[...]

    S = S_ref[...].reshape(n, DH, DH)
    Sb = S.astype(BF16)
    chains = lambda ref: ref[...].reshape(n, tl, ref.shape[-1])
    v_new = chains(u_ref) - jnp.einsum('ntd,nde->nte', chains(w_ref), Sb, preferred_element_type=F32)
    vnb = v_new.astype(BF16)
    o = (jnp.einsum('ntd,nde->nte', chains(qg_ref), Sb, preferred_element_type=F32)
         + jnp.einsum('nts,nse->nte', chains(qkm_ref), vnb, preferred_element_type=F32))
    eg = eg_ref[...].reshape(n, 1, 128)[:, :, 0:1]
    S_new = eg * S + jnp.einsum('ntd,nte->nde', chains(kd_ref), vnb, preferred_element_type=F32)
    S_ref[...] = S_new.reshape(NS, N_HEADS, DH, DH)
    if first:
        _fill_later_layers(S_out, c == NC - 1)

    on = (o * lax.rsqrt(jnp.mean(o * o, axis=-1, keepdims=True) + NORM_EPS) * gng_ref[...]).reshape(NS, N_HEADS, tl, DH)
    for h in range(N_HEADS):
        yc_ref[:, :, h * DH:(h + 1) * DH] = (on[:, h] * _silu(z_ref[:, :, h * DH:(h + 1) * DH])).astype(BF16)


def _gdn_scan(pre, proj, S0, gng, SB, G, tl, layer, S_prev):
    nb, lp, _ = proj.shape
    NS = SB * G
    NI, NC = nb // NS, lp // tl
    six = lambda a: a.reshape((NI * SB, NC) + a.shape[1:])
    cspec = lambda last: pl.BlockSpec((SB, 1, G, N_HEADS, tl, last), lambda i, c: (i, c, 0, 0, 0, 0))
    u, w, qg, kd, qkm, eg = (six(a) for a in pre)
    st_in, st_out, st_shape, extra_in, extra_args = _layered_state(layer, nb, NS, S_prev)
    return pl.pallas_call(
        functools.partial(_gdn_scan_body, NS=NS, tl=tl, NC=NC, first=layer == 0),
        grid=(NI, NC),
        in_specs=[cspec(DH), cspec(DH), cspec(DH), cspec(DH), cspec(tl),
                  pl.BlockSpec((SB, 1, G, N_HEADS, 1, 128), lambda i, c: (i, c, 0, 0, 0, 0)),
                  pl.BlockSpec((NS, tl, D_MIX), lambda i, c: (i, c, OFF_Z // D_MIX)),
                  st_in,
                  pl.BlockSpec((1, DH), lambda i, c: (0, 0))] + extra_in,
        out_specs=[pl.BlockSpec((NS, tl, D_MIX), lambda i, c: (i, c, 0)), st_out],
        out_shape=[jax.ShapeDtypeStruct((nb, lp, D_MIX), BF16), st_shape],
        input_output_aliases={9: 1} if extra_in else {},
        compiler_params=pltpu.CompilerParams(dimension_semantics=("arbitrary", "arbitrary"),
                                             vmem_limit_bytes=VMEM_LIMIT),
        name=f"gdn_scan_g{G}",
    )(u, w, qg, kd, qkm, eg, proj, S0, gng, *extra_args)


def _route(rl):
    lane = lax.broadcasted_iota(jnp.int32, rl.shape, 1).astype(F32)
    is_g = (lane >= N_EXPERTS) & (lane < N_EXPERTS + N_GROUPS)
    gl = jnp.where(is_g, rl, NEG)
    gmax = jnp.max(gl, axis=-1, keepdims=True)
    grp = jnp.min(jnp.where(gl == gmax, lane - N_EXPERTS, 4.0 * N_EXPERTS), axis=-1, keepdims=True)
    p_grp = 1.0 / jnp.sum(jnp.where(is_g, jnp.exp(gl - gmax), 0.0), axis=-1, keepdims=True)
    lo = grp * EXPERTS_PER_GROUP
    in_grp = (lane >= lo) & (lane < lo + EXPERTS_PER_GROUP)
    el = jnp.where(in_grp, rl, NEG)
    m1 = jnp.max(el, axis=-1, keepdims=True)
    i1 = jnp.min(jnp.where(el == m1, lane, 4.0 * N_EXPERTS), axis=-1, keepdims=True)
    el2 = jnp.where(lane == i1, NEG, el)
    m2 = jnp.max(el2, axis=-1, keepdims=True)
    i2 = jnp.min(jnp.where(el2 == m2, lane, 4.0 * N_EXPERTS), axis=-1, keepdims=True)
    e2 = jnp.exp(m2 - m1)
    w1 = p_grp / (1.0 + e2)
    w2 = p_grp * e2 / (1.0 + e2)
    return i1, i2, w1, w2


def _merge_body(ya_ref, yb_ref, yc_ref, ga_ref, gb_ref, gc_ref, x_ref, mod_ref, wa_ref, wb_ref, wc_ref, wo_ref,
                lg_ref, lb_ref, wrh_ref, wrl_ref, br_ref, cnt0_ref, x1_ref, rinfo_ref, cnt_ref, *, bt, lt):
    tm = bt * lt

    @pl.when((pl.program_id(0) == 0) & (pl.program_id(1) == 0))
    def _():
        cnt_ref[...] = cnt0_ref[...]

    def r2(ref):
        return ref[...].reshape(tm, ref.shape[-1])

    merged = (_sigmoid(r2(ga_ref)) * _dot(r2(ya_ref), wa_ref[0])
              + _sigmoid(r2(gb_ref)) * _dot(r2(yb_ref), wb_ref[0])
              + _sigmoid(r2(gc_ref)) * _dot(r2(yc_ref), wc_ref[0]))
    out = _dot(merged.astype(BF16), wo_ref[0])
    y = DN_ALPHA * x_ref[...] + (1.0 + mod_ref[:, 2:3, :]) * out.reshape(bt, lt, D_MODEL)
    x1 = _layer_norm(y, lg_ref[0], lb_ref[0])
    x1_ref[...] = x1
    u2 = (x1 * (1.0 + mod_ref[:, 4:5, :]) + mod_ref[:, 3:4, :]).reshape(tm, D_MODEL)
    hi = u2.astype(BF16)
    lo = (u2 - hi.astype(F32)).astype(BF16)
    rl = _dot(hi, wrh_ref[0]) + _dot(lo, wrh_ref[0]) + _dot(hi, wrl_ref[0]) + br_ref[0]
    i1, i2, w1, w2 = _route(rl)
    lane = lax.broadcasted_iota(jnp.int32, (tm, 128), 1).astype(F32)
    onehot = jnp.where((lane == i1) | (lane == i2), 1.0, 0.0)
    rr = lax.broadcasted_iota(jnp.int32, (tm, tm), 0)
    cc = lax.broadcasted_iota(jnp.int32, (tm, tm), 1)
    before = _dot((rr > cc).astype(BF16), onehot.astype(BF16)) + cnt_ref[0:1, :]
    rank1 = jnp.sum(jnp.where(lane == i1, before, 0.0), axis=-1, keepdims=True)
    rank2 = jnp.sum(jnp.where(lane == i2, before, 0.0), axis=-1, keepdims=True)
    cnt_ref[0:1, :] += jnp.sum(onehot, axis=0, keepdims=True)
    rinfo = jnp.zeros((tm, 128), F32)
    for k, val in enumerate((i1, i2, w1, w2, rank1, rank2)):
        rinfo = jnp.where(lane == k, val, rinfo)
    rinfo_ref[...] = rinfo.reshape(bt, lt, 128)


def _merge(ya, yb, yc, proj, x, mod, wa, wb, wc, wo, lg, lb, wrh, wrl, br, cnt0, layer, bt, lt):
    nb, lp, _ = x.shape
    tok = lambda i, t: (i, t, 0)
    wsp = lambda shape: pl.BlockSpec((1,) + shape, lambda i, t: (layer, 0, 0))
    g0 = OFF_GTS // D_MODEL
    return pl.pallas_call(
        functools.partial(_merge_body, bt=bt, lt=lt),
        grid=(nb // bt, lp // lt),
        in_specs=[pl.BlockSpec((bt, lt, D_MIX), tok),
                  pl.BlockSpec((bt, lt, D_MIX), tok),
                  pl.BlockSpec((bt, lt, D_MIX), tok),
                  pl.BlockSpec((bt, lt, D_MODEL), lambda i, t: (i, t, g0)),
                  pl.BlockSpec((bt, lt, D_MODEL), lambda i, t: (i, t, g0 + 1)),
                  pl.BlockSpec((bt, lt, D_MODEL), lambda i, t: (i, t, g0 + 2)),
                  pl.BlockSpec((bt, lt, D_MODEL), tok),
                  pl.BlockSpec((bt, 6, D_MODEL), lambda i, t: (i, 0, 0)),
                  wsp((D_MIX, D_MODEL)), wsp((D_MIX, D_MODEL)), wsp((D_MIX, D_MODEL)), wsp((D_MODEL, D_MODEL)),
                  wsp((1, D_MODEL)), wsp((1, D_MODEL)),
                  wsp((D_MODEL, 128)), wsp((D_MODEL, 128)), wsp((1, 128)),
                  pl.BlockSpec((SUBLANES, LANES), lambda i, t: (0, 0))],
        out_specs=[pl.BlockSpec((bt, lt, D_MODEL), tok),
                   pl.BlockSpec((bt, lt, 128), tok),
                   pl.BlockSpec((SUBLANES, LANES), lambda i, t: (0, 0))],
        out_shape=[jax.ShapeDtypeStruct((nb, lp, D_MODEL), F32),
                   jax.ShapeDtypeStruct((nb, lp, 128), F32),
                   jax.ShapeDtypeStruct((SUBLANES, LANES), F32)],
        compiler_params=pltpu.CompilerParams(dimension_semantics=("arbitrary", "arbitrary"),
                                             vmem_limit_bytes=VMEM_LIMIT),
        name=f"merge_l{layer}_b{bt}",
    )(ya, yb, yc, proj, proj, proj, x, mod, wa, wb, wc, wo, lg, lb, wrh, wrl, br, cnt0)


def _moe_plan(cnt, cnt_first, n_tok, mb):
    n_blocks = 2 * n_tok // mb + N_EXPERTS
    counts = cnt[0, :N_EXPERTS].astype(jnp.int32)
    nblk = (counts + mb - 1) // mb
    pend = jnp.cumsum(nblk)
    block_e = jnp.sum(pend[None, :] <= jnp.arange(n_blocks, dtype=jnp.int32)[:, None], axis=1)
    block_e = jnp.minimum(block_e, N_EXPERTS - 1).astype(jnp.int32)
    n_used = pend[N_EXPERTS - 1:].astype(jnp.int32)
    first_row = (pend - nblk) * mb
    per_block = mb // ZERO_ROWS
    hi = pend * per_block - 1
    lo = jnp.minimum((first_row + cnt_first[0, :N_EXPERTS].astype(jnp.int32)) // ZERO_ROWS, hi)
    ztail = jnp.concatenate([jnp.where(nblk > 0, lo, 0), jnp.where(nblk > 0, hi, -1), n_used * per_block])
    return first_row, block_e, n_used, ztail.astype(jnp.int32).reshape(1, 1, 2 * N_EXPERTS + 1), n_blocks


def _dispatch_rows(rinfo, first_row, tm):
    nb, lp, _ = rinfo.shape
    expert = rinfo[..., 0:2].astype(jnp.int32)
    rank = rinfo[..., 4:6].astype(jnp.int32)
    ids = jnp.arange(N_EXPERTS, dtype=jnp.int32)
    start = jnp.sum(jnp.where(expert[..., None] == ids, first_row, 0), axis=-1)
    return (start + rank).reshape(nb * lp // tm, 1, 2 * tm)


def _row_copy(src, dst, sem):
    return pltpu.make_async_copy(src, dst, sem)


def _dispatch_body(pos_ref, ztail_ref, x1_ref, mod_ref, *rest, bt, lt, n_blocks, mb, n_steps, first):
    disp_ref, u_scr, z_scr, sem, zsem = rest[-5:]
    tm = bt * lt

    @pl.when((pl.program_id(0) == 0) & (pl.program_id(1) == 0) & first)
    def _():
        z_scr[...] = jnp.zeros_like(z_scr)

        def zero_block(j):
            return _row_copy(z_scr, disp_ref.at[pl.ds(pl.multiple_of(j * ZERO_ROWS, ZERO_ROWS), ZERO_ROWS)], zsem)

        def for_zeroed_blocks(fn):
            for e in range(N_EXPERTS):
                lax.fori_loop(ztail_ref[0, 0, e], ztail_ref[0, 0, N_EXPERTS + e] + 1, fn, 0)
            lax.fori_loop(ztail_ref[0, 0, 2 * N_EXPERTS], n_blocks * (mb // ZERO_ROWS), fn, 0)

        for_zeroed_blocks(lambda j, c: (zero_block(j).start(), c)[1])
        for_zeroed_blocks(lambda j, c: (zero_block(j).wait(), c)[1])

    step = pl.program_id(0) * pl.num_programs(1) + pl.program_id(1)
    slot = step % 2
    u2 = x1_ref[...] * (1.0 + mod_ref[:, 4:5, :]) + mod_ref[:, 3:4, :]
    u_scr[slot] = u2.reshape(tm // SUBLANES, SUBLANES, D_MODEL)

    def issue(i, carry):
        for r in range(SUBLANES):
            for k in range(TOP_K):
                row = pos_ref[0, 0, TOP_K * (SUBLANES * i + r) + k]
                _row_copy(u_scr.at[slot, i, pl.ds(r, 1)], disp_ref.at[pl.ds(row, 1)],
                          sem.at[slot]).start(priority=k)
        return carry

    lax.fori_loop(0, tm // SUBLANES, issue, 0)

    def drain(which):
        for k in range(TOP_K):
            _row_copy(disp_ref.at[pl.ds(0, tm)], disp_ref.at[pl.ds(0, tm)], sem.at[which]).wait()

    @pl.when(step > 0)
    def _():
        drain(1 - slot)

    @pl.when(step == n_steps - 1)
    def _():
        drain(slot)


def _dispatch(x1, mod, pos, ztail, n_blocks, mb, bt, lt, buf):
    nb, lp, _ = x1.shape
    nt = lp // lt
    extra_in, extra_args = ([], []) if buf is None else ([pl.BlockSpec(memory_space=pl.ANY)], [buf])
    return pl.pallas_call(
        functools.partial(_dispatch_body, bt=bt, lt=lt, n_blocks=n_blocks, mb=mb, n_steps=(nb // bt) * nt,
                          first=buf is None),
        grid=(nb // bt, nt),
        in_specs=[pl.BlockSpec((1, 1, 2 * bt * lt), lambda i, t: (i * nt + t, 0, 0), memory_space=pltpu.SMEM),
                  pl.BlockSpec((1, 1, 2 * N_EXPERTS + 1), lambda i, t: (0, 0, 0), memory_space=pltpu.SMEM),
                  pl.BlockSpec((bt, lt, D_MODEL), lambda i, t: (i, t, 0)),
                  pl.BlockSpec((bt, 6, D_MODEL), lambda i, t: (i, 0, 0))] + extra_in,
        out_specs=pl.BlockSpec(memory_space=pl.ANY),
        out_shape=jax.ShapeDtypeStruct((n_blocks * mb, D_MODEL), F32),
        scratch_shapes=[pltpu.VMEM((2, bt * lt // SUBLANES, SUBLANES, D_MODEL), F32),
                        pltpu.VMEM((ZERO_ROWS, D_MODEL), F32),
                        pltpu.SemaphoreType.DMA((2,)), pltpu.SemaphoreType.DMA],
        compiler_params=pltpu.CompilerParams(dimension_semantics=("arbitrary", "arbitrary"),
                                             vmem_limit_bytes=VMEM_LIMIT),
        input_output_aliases={4: 0} if extra_in else {},
        name=f"dispatch_b{bt}",
    )(pos, ztail, x1, mod, *extra_args)


def _experts_body(be_ref, nu_ref, x_ref, wg_ref, wu_ref, wd_ref, o_ref):
    j = pl.program_id(0)

    @pl.when(j < nu_ref[0])
    def _():
        x = x_ref[...].astype(BF16)
        hb = _silu(_dot(x, wg_ref[0, 0].astype(BF16))) * _dot(x, wu_ref[0, 0].astype(BF16))
        o_ref[...] = _dot(hb.astype(BF16), wd_ref[0, 0].astype(BF16))

    @pl.when(j >= nu_ref[0])
    def _():
        o_ref[...] = jnp.zeros_like(o_ref)


def _experts(disp, block_e, n_used, wg, wu, wd, layer, mb):
    n_blocks = disp.shape[0] // mb
    wmap = lambda j, be, nu: (layer, be[j], 0, 0)
    return pl.pallas_call(
        _experts_body,
        grid_spec=pltpu.PrefetchScalarGridSpec(
            num_scalar_prefetch=2,
            grid=(n_blocks,),
            in_specs=[pl.BlockSpec((mb, D_MODEL), lambda j, be, nu: (jnp.minimum(j, nu[0] - 1), 0)),
                      pl.BlockSpec((1, 1, D_MODEL, D_EXPERT), wmap),
                      pl.BlockSpec((1, 1, D_MODEL, D_EXPERT), wmap),
                      pl.BlockSpec((1, 1, D_EXPERT, D_MODEL), wmap)],
            out_specs=pl.BlockSpec((mb, D_MODEL), lambda j, be, nu: (j, 0))),
        out_shape=jax.ShapeDtypeStruct(disp.shape, F32),
        compiler_params=pltpu.CompilerParams(dimension_semantics=("arbitrary",), vmem_limit_bytes=VMEM_LIMIT),
        name=f"experts_l{layer}_n{n_blocks}",
    )(block_e, n_used, disp, wg, wu, wd)


def _combine_body(pos_ref, pos_next_ref, x1_ref, mod_ref, rinfo_ref, eo_ref, lg_ref, lb_ref, x2_ref, r_scr, sem,
                  *, bt, lt, n_steps):
    tm = bt * lt
    step = pl.program_id(0) * pl.num_programs(1) + pl.program_id(1)
    slot = step % 2

    def gather(p_ref, to_slot):
        def issue(i, carry):
            for r in range(SUBLANES):
                for k in range(TOP_K):
                    row = p_ref[0, 0, TOP_K * (SUBLANES * i + r) + k]
                    _row_copy(eo_ref.at[pl.ds(row, 1)], r_scr.at[to_slot, k, i, pl.ds(r, 1)],
                              sem.at[to_slot]).start(priority=k)
            return carry

        lax.fori_loop(0, tm // SUBLANES, issue, 0)

    @pl.when(step == 0)
    def _():
        gather(pos_ref, 0)

    @pl.when(step + 1 < n_steps)
    def _():
        gather(pos_next_ref, 1 - slot)

    for k in range(TOP_K):
        _row_copy(eo_ref.at[pl.ds(0, tm)], eo_ref.at[pl.ds(0, tm)], sem.at[slot]).wait()

    rinfo = rinfo_ref[...].reshape(tm, 128)
    rows = r_scr[slot]
    moe = (rows[0].reshape(tm, D_MODEL) * rinfo[:, 2:3] + rows[1].reshape(tm, D_MODEL) * rinfo[:, 3:4])
    y = DN_ALPHA * x1_ref[...] + (1.0 + mod_ref[:, 5:6, :]) * moe.reshape(bt, lt, D_MODEL)
    x2_ref[...] = _layer_norm(y, lg_ref[0], lb_ref[0])


def _combine(x1, mod, rinfo, eo, pos, lg, lb, layer, bt, lt):
    nb, lp, _ = x1.shape
    nt = lp // lt
    tok = lambda i, t: (i, t, 0)
    n_steps = (nb // bt) * nt
    return pl.pallas_call(
        functools.partial(_combine_body, bt=bt, lt=lt, n_steps=n_steps),
        grid=(nb // bt, nt),
        in_specs=[pl.BlockSpec((1, 1, 2 * bt * lt), lambda i, t: (i * nt + t, 0, 0), memory_space=pltpu.SMEM),
                  pl.BlockSpec((1, 1, 2 * bt * lt), lambda i, t: (jnp.minimum(i * nt + t + 1, n_steps - 1), 0, 0),
                               memory_space=pltpu.SMEM),
                  pl.BlockSpec((bt, lt, D_MODEL), tok),
                  pl.BlockSpec((bt, 6, D_MODEL), lambda i, t: (i, 0, 0)),
                  pl.BlockSpec((bt, lt, 128), tok),
                  pl.BlockSpec(memory_space=pl.ANY),
                  pl.BlockSpec((1, 1, D_MODEL), lambda i, t: (layer, 0, 0)),
                  pl.BlockSpec((1, 1, D_MODEL), lambda i, t: (layer, 0, 0))],
        out_specs=pl.BlockSpec((bt, lt, D_MODEL), tok),
        out_shape=jax.ShapeDtypeStruct((nb, lp, D_MODEL), F32),
        scratch_shapes=[pltpu.VMEM((2, TOP_K, bt * lt // SUBLANES, SUBLANES, D_MODEL), F32),
                        pltpu.SemaphoreType.DMA((2,))],
        compiler_params=pltpu.CompilerParams(dimension_semantics=("arbitrary", "arbitrary"),
                                             vmem_limit_bytes=VMEM_LIMIT),
        name=f"combine_l{layer}_b{bt}",
    )(pos, pos, x1, mod, rinfo, eo, lg, lb)


def _hist(state):
    return jnp.pad(state, ((0, 0), (HIST - state.shape[1], 0), (0, 0)))


def _trunks(groups, p, mb):
    for g in groups:
        g.update(C=None, S=None, new={key: [] for key in ('n', 'm', 'conv', 'gconv')})
    n_tok = sum(g['x'].shape[0] * g['x'].shape[1] for g in groups)
    for l in range(DEPTH):
        cnt = jnp.zeros((SUBLANES, LANES), F32)
        for g in groups:
            st, mix, lv, nb = g['st'], g['mix'], g['lv'], g['x'].shape[0]
            proj, x = _inproj(g['x'], g['mod'][l], p['ln_in_g'], p['ln_in_b'], p['w_in_r'], p['b_in_r'], l, l == 0,
                              *g['ip_tile'])
            m0 = jnp.broadcast_to(st['m'][l][:, :, None, None], (nb, N_HEADS, 1, 128))
            yb, conv, *pre = _mlstm_prep(proj, _hist(st['conv'][l]), p['conv_b_w8'][l],
                                         mix['NB'], mix['G'], mix['tl'], lv)
            ya, g['C'], n, m = _mlstm_scan(pre, proj, st['C'], st['n'][l], m0, p['mlstm_norm_g'][l:l + 1],
                                           mix['SB'], mix['G'], mix['tl'], l, g['C'])
            *pre, gconv = _gdn_prep(proj, _hist(st['gconv'][l]), p['conv_c_w8'][l], p['alog_row'][l],
                                    p['dtb_row'][l], mix['NB'], mix['G'], mix['tl'], lv)
            yc, g['S'] = _gdn_scan(pre, proj, st['S'], p['gdn_norm_g'][l:l + 1], mix['SB'], mix['G'], mix['tl'],
                                   l, g['S'])
            g['x1'], g['rinfo'], cnt = _merge(ya, yb, yc, proj, x, g['mod'][l], p['w_br_a'], p['w_br_b'], p['w_br_c'],
                                              p['w_out'], p['ln1_g'], p['ln1_b'], p['wr_hi'], p['wr_lo'], p['br'],
                                              cnt, l, g['bt'], g['lt'])
            g['cnt'] = cnt
            for key, val in (('n', n), ('m', m[:, :, 0, 0]), ('conv', conv), ('gconv', gconv)):
                g['new'][key].append(val)
        first_row, block_e, n_used, ztail, n_blocks = _moe_plan(cnt, groups[0]['cnt'], n_tok, mb)
        disp = None
        for g in groups:
            g['pos'] = _dispatch_rows(g['rinfo'], first_row, g['bt'] * g['lt'])
            disp = _dispatch(g['x1'], g['mod'][l], g['pos'], ztail, n_blocks, mb, g['bt'], g['lt'], disp)
        eo = _experts(disp, block_e, n_used, p['exp_w_gate'], p['exp_w_up'], p['exp_w_down'], l, mb)
        for g in groups:
            g['x'] = _combine(g['x1'], g['mod'][l], g['rinfo'], eo, g['pos'], p['ln2_g'], p['ln2_b'], l,
                              g['bt'], g['lt'])
    return [(g['x'], dict({key: jnp.stack(val) for key, val in g['new'].items()}, C=g['C'], S=g['S']))
            for g in groups]


def kernel(x_prompt, x_sample, state_mlstm_C, state_mlstm_n, state_mlstm_m, state_conv, state_gdn_S, state_gdn_conv, c_prompt, c_sample, ln_in_g, ln_in_b, w_ada, b_ada, w_in, b_in, mlstm_norm_g, conv_b_w, conv_c_w, gdn_a_log, gdn_dt_bias, gdn_norm_g, w_br_a, w_br_b, w_br_c, w_out, ln1_g, ln1_b, router_g_w, router_g_b, router_e_w, router_e_b, exp_w_gate, exp_w_up, exp_w_down, ln2_g, ln2_b):
    nbp, lp, _ = x_prompt.shape
    nbs, ls, _ = x_sample.shape
    lsp = 8

    def regroup(a):
        out = jnp.zeros(a.shape[:-1] + (N_PROJ,), a.dtype)
        for src, end, dst in PROJ_SEGMENTS:
            out = out.at[..., dst:dst + end - src].set(a[..., src:end])
        return out

    wr = jnp.concatenate([router_e_w, router_g_w, jnp.zeros((DEPTH, D_MODEL, 128 - N_EXPERTS - N_GROUPS), F32)], axis=-1)
    wr_hi = wr.astype(BF16)
    lane_pad = lambda a: jnp.pad(a, ((0, 0), (4, 128 - 4 - N_HEADS)))[:, None, :]
    p = dict(
        ln_in_g=ln_in_g.reshape(1, D_MODEL), ln_in_b=ln_in_b.reshape(1, D_MODEL),
        w_in_r=_regroup_w_in(w_in), b_in_r=regroup(b_in).reshape(DEPTH, 1, N_PROJ),
        mlstm_norm_g=mlstm_norm_g,
        conv_b_w8=jnp.pad(conv_b_w, ((0, 0), (0, 8 - CONV_B), (0, 0))),
        conv_c_w8=jnp.pad(conv_c_w, ((0, 0), (0, 8 - CONV_C), (0, 0))),
        alog_row=lane_pad(gdn_a_log), dtb_row=lane_pad(gdn_dt_bias), gdn_norm_g=gdn_norm_g,
        w_br_a=w_br_a.astype(BF16), w_br_b=w_br_b.astype(BF16), w_br_c=w_br_c.astype(BF16),
        w_out=w_out.astype(BF16),
        ln1_g=ln1_g.reshape(DEPTH, 1, D_MODEL), ln1_b=ln1_b.reshape(DEPTH, 1, D_MODEL),
        wr_hi=wr_hi, wr_lo=(wr - wr_hi.astype(F32)).astype(BF16),
        br=jnp.concatenate([router_e_b, router_g_b, jnp.zeros((DEPTH, 128 - N_EXPERTS - N_GROUPS), F32)],
                           axis=-1).reshape(DEPTH, 1, 128),
        exp_w_gate=exp_w_gate, exp_w_up=exp_w_up, exp_w_down=exp_w_down,
        ln2_g=ln2_g.reshape(DEPTH, 1, D_MODEL), ln2_b=ln2_b.reshape(DEPTH, 1, D_MODEL),
    )

    mod = _ada(jnp.concatenate([c_prompt, c_sample], axis=0), w_ada, b_ada)
    mod = mod.reshape(DEPTH, nbp + nbs, 6, D_MODEL)

    zeros = lambda *s: jnp.zeros((DEPTH, nbp) + s, F32)
    st_p = {'C': zeros(N_HEADS, DH, DH), 'n': zeros(N_HEADS, DH), 'm': zeros(N_HEADS),
            'conv': zeros(CONV_B - 1, D_MIX), 'S': zeros(N_HEADS, DH, DH), 'gconv': zeros(CONV_C - 1, 3 * D_MIX)}
    prompt = dict(x=x_prompt, mod=mod[:, :nbp], st=st_p, bt=1, lt=512, lv=lp,
                  mix=dict(NB=8, G=1, tl=RB, SB=nbp), ip_tile=(1, 256))

    st_s = {'C': state_mlstm_C, 'n': state_mlstm_n, 'm': state_mlstm_m, 'conv': state_conv,
            'S': state_gdn_S, 'gconv': state_gdn_conv}
    xs = jnp.pad(x_sample, ((0, 0), (0, lsp - ls), (0, 0)))
    sample = dict(x=xs, mod=mod[:, nbp:], st=st_s, bt=64, lt=lsp, lv=ls,
                  mix=dict(NB=2, G=RB // lsp, tl=lsp, SB=2), ip_tile=(32, lsp))
    (y_p, sp), (y_s, ss) = _trunks([prompt, sample], p, mb=512)
    y_s = y_s[:, :ls]

    return (y_p, y_s, sp['C'], sp['n'], sp['m'], sp['conv'], sp['S'], sp['gconv'],
            ss['C'], ss['n'], ss['m'], ss['conv'], ss['S'], ss['gconv'])
```
